```python
import math
import jax, jax.numpy as jnp
from jax import lax
import numpy as np

D_MODEL = 2048
BATCH = 2
SEQ = 8192
DEPTH = 2
DEC_BATCH = 8
DEC_SEQ = 16
PAST_LEN = 4096

CHUNK = 64
QBLOCK = 128
EPS = 1e-6
N_AB = (DEPTH + 1) // 2
N_CD = DEPTH // 2

RET_HEADS = 4
RET_DK = D_MODEL // 16
RET_DV = D_MODEL // 8
ROPE_BASE = 10000.0
SSD_DINNER = D_MODEL // 2
SSD_HEADDIM = 64
SSD_HEADS = SSD_DINNER // SSD_HEADDIM
SSD_GROUPS = 2
SSD_DSTATE = 128
SSD_CONV = 4
SSD_CONV_DIM = SSD_DINNER + 2 * SSD_GROUPS * SSD_DSTATE
FOX_HEADS = 8
FOX_HEAD_DIM = 128
FOX_WIDTH = FOX_HEADS * FOX_HEAD_DIM
FOX_F_BIAS_INIT = 2.0
SC_DIM = D_MODEL // 2
SC_WIDTH = 3
D_FF = 5632
FFN_CONV = 3

AB_SIZES = (RET_HEADS * RET_DK, RET_HEADS * RET_DK, RET_HEADS * RET_DV, RET_HEADS * RET_DV,
            SSD_DINNER, SSD_CONV_DIM, SSD_HEADS)
AB_IN = sum(AB_SIZES)
AB_OUT = RET_HEADS * RET_DV + SSD_DINNER
CD_SIZES = (FOX_WIDTH, FOX_WIDTH, FOX_WIDTH, FOX_HEADS, SC_DIM, SC_DIM, SC_DIM)
CD_IN = sum(CD_SIZES)
CD_OUT = FOX_WIDTH + SC_DIM

kernel_name = "hybrid_streaming_encoder_step"


def _split(t, sizes):
    outs, o = [], 0
    for s in sizes:
        outs.append(t[..., o:o + s])
        o += s
    return outs


def _block_len(length, blk):
    return blk if length % blk == 0 else length


def _rmsnorm(x, w):
    xf = x.astype(jnp.float32)
    y = xf * lax.rsqrt(jnp.mean(xf * xf, axis=-1, keepdims=True) + EPS)
    return (y * w.astype(jnp.float32)).astype(x.dtype)


def _head_norm(y):
    mu = jnp.mean(y, axis=-1, keepdims=True)
    var = jnp.mean(jnp.square(y - mu), axis=-1, keepdims=True)
    return (y - mu) * lax.rsqrt(var + EPS)


def _rotary(x, pos):
    half = x.shape[-1] // 2
    inv = ROPE_BASE ** (-jnp.arange(half, dtype=jnp.float32) / half)
    ang = pos.astype(jnp.float32)[:, None] * inv[None, :]
    cos = jnp.cos(ang)[None, :, None, :]
    sin = jnp.sin(ang)[None, :, None, :]
    xf = x.astype(jnp.float32)
    x1, x2 = xf[..., :half], xf[..., half:]
    return jnp.concatenate([x1 * cos - x2 * sin, x2 * cos + x1 * sin], axis=-1).astype(x.dtype)


def _causal_dwconv(u, hist, w):
    width = w.shape[0]
    length = u.shape[1]
    uu = jnp.concatenate([hist.astype(u.dtype), u], axis=1)
    out = uu[:, 0:length] * w[0]
    for j in range(1, width):
        out = out + uu[:, j:j + length] * w[j]
    return out, uu[:, uu.shape[1] - (width - 1):]


def _to_blocks(t, n, cs):
    return jnp.swapaxes(t.reshape((t.shape[0], n, cs) + t.shape[2:]), 0, 1)


def _retention(q, k, v, state, log_gamma):
    bsz, length, nh, _ = q.shape
    dv = v.shape[-1]
    cs = _block_len(length, CHUNK)
    n = length // cs
    idx = jnp.arange(cs, dtype=jnp.float32)
    diff = idx[:, None] - idx[None, :]
    decay = jnp.exp(jnp.where((diff >= 0)[None], diff[None] * log_gamma[:, None, None], -jnp.inf))
    inner_decay = jnp.exp((idx[:, None] + 1.0) * log_gamma[None, :])
    state_decay = jnp.exp((cs - 1.0 - idx)[:, None] * log_gamma[None, :])
    chunk_decay = jnp.exp(cs * log_gamma)

    def step(s, blk):
        qc, kc, vc = (t.astype(jnp.float32) for t in blk)
        scores = jnp.einsum('bihd,bjhd->bhij', qc, kc) * decay[None]
        y = jnp.einsum('bhij,bjhe->bihe', scores, vc)
        y = y + jnp.einsum('bihd,bhde->bihe', qc, s) * inner_decay[None, :, :, None]
        s_new = chunk_decay[None, :, None, None] * s + jnp.einsum(
            'bjhd,bjhe->bhde', kc * state_decay[None, :, :, None], vc)
        return s_new, y

    s_fin, ys = lax.scan(step, state.astype(jnp.float32),
                         (_to_blocks(q, n, cs), _to_blocks(k, n, cs), _to_blocks(v, n, cs)))
    return jnp.swapaxes(ys, 0, 1).reshape(bsz, length, nh, dv), s_fin


def _ssd(x, dt, a, b_ssm, c_ssm, state):
    bsz, length, nh, hp = x.shape
    ng, ns = b_ssm.shape[2], b_ssm.shape[3]
    rep = nh // ng
    cs = _block_len(length, CHUNK)
    n = length // cs
    tri = jnp.arange(cs)[:, None] >= jnp.arange(cs)[None, :]
    a_g = a.reshape(ng, rep)
    xg = x.reshape(bsz, length, ng, rep, hp)
    dtg = dt.reshape(bsz, length, ng, rep)

    def step(s, blk):
        xc, dtc, bc, cc = blk
        xc = xc.astype(jnp.float32)
        bc = bc.astype(jnp.float32)
        cc = cc.astype(jnp.float32)
        acs = jnp.cumsum(dtc * a_g, axis=1)
        seg = acs[:, :, None] - acs[:, None, :]
        lmat = jnp.exp(jnp.where(tri[None, :, :, None, None], seg, -jnp.inf))
        xdt = xc * dtc[..., None]
        cb = jnp.einsum('bign,bjgn->bijg', cc, bc)
        y = jnp.einsum('bijgr,bjgrp->bigrp', cb[..., None] * lmat, xdt)
        y = y + jnp.einsum('bign,bgrpn->bigrp', cc, s) * jnp.exp(acs)[..., None]
        to_end = jnp.exp(acs[:, -1:] - acs)
        s_new = jnp.exp(acs[:, -1])[..., None, None] * s + jnp.einsum(
            'bjgn,bjgrp->bgrpn', bc, xdt * to_end[..., None])
        return s_new, y

    s0 = state.astype(jnp.float32).reshape(bsz, ng, rep, hp, ns)
    s_fin, ys = lax.scan(step, s0, (_to_blocks(xg, n, cs), _to_blocks(dtg, n, cs),
                                    _to_blocks(b_ssm, n, cs), _to_blocks(c_ssm, n, cs)))
    y = jnp.swapaxes(ys, 0, 1).reshape(bsz, length, nh, hp)
    return y, s_fin.reshape(bsz, nh, hp, ns)


def _fox_attention(q, k, v, cum_q, cum_k, q_start):
    bsz, lq, nh, hd = q.shape
    lk = k.shape[1]
    blk = _block_len(lq, QBLOCK)
    nb = lq // blk
    scale = hd ** -0.5
    kpos = jnp.arange(lk)
    ck = jnp.swapaxes(cum_k, 1, 2)

    def attend(args):
        qb, cqb, start = args
        s = jnp.einsum('bihd,bjhd->bhij', qb, k, preferred_element_type=jnp.float32) * scale
        s = s + jnp.swapaxes(cqb, 1, 2)[..., None] - ck[:, :, None, :]
        qpos = q_start + start + jnp.arange(blk)
        s = jnp.where(kpos[None, :] <= qpos[:, None], s, -jnp.inf)
        p = jax.nn.softmax(s, axis=-1)
        return jnp.einsum('bhij,bjhd->bihd', p.astype(v.dtype), v)

    qs = jnp.swapaxes(q.reshape(bsz, nb, blk, nh, hd), 0, 1)
    cqs = jnp.swapaxes(cum_q.reshape(bsz, nb, blk, nh), 0, 1)
    out = lax.map(attend, (qs, cqs, jnp.arange(nb) * blk))
    return jnp.swapaxes(out, 0, 1).reshape(bsz, lq, nh, hd)


def _mixer_ab(h, pos0, s_ret, s_ssd, s_conv, w_in, ret_norm_w, conv_w, conv_b, dt_bias, a_log, d_skip,
              ssd_norm_w, w_out):
    bsz, length, _ = h.shape
    q, k, v, g, z, xbc, dt = _split(h @ w_in, AB_SIZES)
    pos = pos0 + jnp.arange(length)
    q = _rotary(q.reshape(bsz, length, RET_HEADS, RET_DK), pos)
    k = _rotary(k.reshape(bsz, length, RET_HEADS, RET_DK), pos) * (RET_DK ** -0.5)
    v = v.reshape(bsz, length, RET_HEADS, RET_DV)
    log_gamma = jnp.log1p(-jnp.exp2(-5.0 - jnp.arange(RET_HEADS, dtype=jnp.float32)))
    y_ret, s_ret_new = _retention(q, k, v, s_ret, log_gamma)
    y_ret = _head_norm(y_ret).reshape(bsz, length, RET_HEADS * RET_DV) * ret_norm_w.astype(jnp.float32)
    y_ret = jax.nn.silu(g) * y_ret.astype(h.dtype)
    xbc, s_conv_new = _causal_dwconv(xbc, s_conv, conv_w)
    xbc = jax.nn.silu(xbc + conv_b)
    xs, b_ssm, c_ssm = _split(xbc, (SSD_DINNER, SSD_GROUPS * SSD_DSTATE, SSD_GROUPS * SSD_DSTATE))
    xs = xs.reshape(bsz, length, SSD_HEADS, SSD_HEADDIM)
    dt = jax.nn.softplus(dt.astype(jnp.float32) + dt_bias.astype(jnp.float32))
    a = -jnp.exp(a_log.astype(jnp.float32))
    y_ssd, s_ssd_new = _ssd(xs, dt, a, b_ssm.reshape(bsz, length, SSD_GROUPS, SSD_DSTATE),
                            c_ssm.reshape(bsz, length, SSD_GROUPS, SSD_DSTATE), s_ssd)
    y_ssd = y_ssd + d_skip.astype(jnp.float32)[:, None] * xs.astype(jnp.float32)
    y_ssd = _rmsnorm(y_ssd.reshape(bsz, length, SSD_DINNER) * jax.nn.silu(z.astype(jnp.float32)),
                     ssd_norm_w).astype(h.dtype)
    out = jnp.concatenate([y_ret, y_ssd], axis=-1) @ w_out
    return out, s_ret_new.astype(s_ret.dtype), s_ssd_new.astype(s_ssd.dtype), s_conv_new


def _mixer_cd(h, c_k, c_v, c_logf, s_sc, w_in, f_bias, sc_w, w_out):
    bsz, length, _ = h.shape
    q, k, v, fl, u, bg, cg = _split(h @ w_in, CD_SIZES)
    shp = (bsz, length, FOX_HEADS, FOX_HEAD_DIM)
    q, k, v = q.reshape(shp), k.reshape(shp), v.reshape(shp)
    logf = jax.nn.log_sigmoid(fl.astype(jnp.float32) + f_bias.astype(jnp.float32))
    past = c_k.shape[1]
    k_all = jnp.concatenate([c_k.astype(k.dtype), k], axis=1)
    v_all = jnp.concatenate([c_v.astype(v.dtype), v], axis=1)
    cum = jnp.cumsum(jnp.concatenate([c_logf.astype(jnp.float32), logf], axis=1), axis=1)
    y_fox = _fox_attention(q, k_all, v_all, cum[:, past:], cum, past).reshape(bsz, length, FOX_WIDTH)
    y_sc, s_sc_new = _causal_dwconv(cg * u, s_sc, sc_w)
    y_sc = bg * y_sc
    out = jnp.concatenate([y_fox.astype(h.dtype), y_sc], axis=-1) @ w_out
    return out, k, v, logf, s_sc_new


def _conv_ffn(h, hist, w_gate, w_up, conv_w, conv_b, w_down):
    a, hist_new = _causal_dwconv(h @ w_gate, hist, conv_w)
    return (jax.nn.silu(a + conv_b) * (h @ w_up)) @ w_down, hist_new


def _trunk(x, pos0, st_ret, st_ssd, st_ssd_conv, c_k, c_v, c_logf, st_sconv, st_ffn, W):
    n_ret, n_ssd, n_ssdc, n_k, n_v, n_lf, n_sc, n_ffn = [], [], [], [], [], [], [], []
    for i in range(DEPTH):
        j = i // 2
        if i % 2 == 0:
            h = _rmsnorm(x, W['ab_norm_w'][j])
            out, r, s, c = _mixer_ab(h, pos0, st_ret[j], st_ssd[j], st_ssd_conv[j], W['ab_w_in'][j],
                                     W['ret_norm_w'][j], W['ssd_conv_w'][j], W['ssd_conv_b'][j],
                                     W['ssd_dt_bias'][j], W['ssd_A_log'][j], W['ssd_D'][j],
                                     W['ssd_norm_w'][j], W['ab_w_out'][j])
            n_ret.append(r)
            n_ssd.append(s)
            n_ssdc.append(c)
        else:
            h = _rmsnorm(x, W['cd_norm_w'][j])
            out, kk, vv, lf, sc = _mixer_cd(h, c_k[j], c_v[j], c_logf[j], st_sconv[j], W['cd_w_in'][j],
                                            W['fox_f_bias'][j], W['sconv_w'][j], W['cd_w_out'][j])
            n_k.append(kk)
            n_v.append(vv)
            n_lf.append(lf)
            n_sc.append(sc)
        x = x + out.astype(x.dtype)
        f, sf = _conv_ffn(_rmsnorm(x, W['ffn_norm_w'][i]), st_ffn[i], W['ffn_w_gate'][i], W['ffn_w_up'][i],
                          W['ffn_conv_w'][i], W['ffn_conv_b'][i], W['ffn_w_down'][i])
        x = x + f.astype(x.dtype)
        n_ffn.append(sf)
    y = _rmsnorm(x, W['final_norm_w'])
    return (y, jnp.stack(n_ret), jnp.stack(n_ssd), jnp.stack(n_ssdc), jnp.stack(n_k), jnp.stack(n_v),
            jnp.stack(n_lf), jnp.stack(n_sc), jnp.stack(n_ffn))


def setup_inputs(seed: int = 0) -> dict:
    key = jax.random.key(seed)
    ks = iter(jax.random.split(key, 48))
    f32 = jnp.float32

    def nrm(shape, scale):
        return jax.random.normal(next(ks), shape, f32) * scale

    x_prompt = nrm((BATCH, SEQ, D_MODEL), 1.0)
    x_sample = nrm((DEC_BATCH, DEC_SEQ, D_MODEL), 1.0)
    state_ret = nrm((N_AB, DEC_BATCH, RET_HEADS, RET_DK, RET_DV), 0.5)
    state_ssd = nrm((N_AB, DEC_BATCH, SSD_HEADS, SSD_HEADDIM, SSD_DSTATE), 0.1)
    state_ssd_conv = nrm((N_AB, DEC_BATCH, SSD_CONV - 1, SSD_CONV_DIM), 1.0)
    cache_fox_k = nrm((N_CD, DEC_BATCH, PAST_LEN, FOX_HEADS, FOX_HEAD_DIM), 1.0)
    cache_fox_v = nrm((N_CD, DEC_BATCH, PAST_LEN, FOX_HEADS, FOX_HEAD_DIM), 1.0)
    cache_fox_logf = jax.nn.log_sigmoid(FOX_F_BIAS_INIT + nrm((N_CD, DEC_BATCH, PAST_LEN, FOX_HEADS), 1.0))
    state_sconv = nrm((N_CD, DEC_BATCH, SC_WIDTH - 1, SC_DIM), 1.0)
    state_ffn_conv = nrm((DEPTH, DEC_BATCH, FFN_CONV - 1, D_FF), 1.0)

    ab_norm_w = 1.0 + nrm((N_AB, D_MODEL), 0.02)
    ab_w_in = nrm((N_AB, D_MODEL, AB_IN), D_MODEL ** -0.5)
    ret_norm_w = 1.0 + nrm((N_AB, RET_HEADS * RET_DV), 0.02)
    ssd_conv_w = nrm((N_AB, SSD_CONV, SSD_CONV_DIM), SSD_CONV ** -0.5)
    ssd_conv_b = nrm((N_AB, SSD_CONV_DIM), 0.02)
    dt0 = jnp.exp(jax.random.uniform(next(ks), (N_AB, SSD_HEADS), f32, math.log(1e-3), math.log(1e-1)))
    ssd_dt_bias = dt0 + jnp.log(-jnp.expm1(-dt0))
    ssd_A_log = jnp.log(jax.random.uniform(next(ks), (N_AB, SSD_HEADS), f32, 1.0, 16.0))
    ssd_D = 1.0 + nrm((N_AB, SSD_HEADS), 0.1)
    ssd_norm_w = 1.0 + nrm((N_AB, SSD_DINNER), 0.02)
    ab_w_out = nrm((N_AB, AB_OUT, D_MODEL), AB_OUT ** -0.5)

    cd_norm_w = 1.0 + nrm((N_CD, D_MODEL), 0.02)
    cd_w_in = nrm((N_CD, D_MODEL, CD_IN), D_MODEL ** -0.5)
    fox_f_bias = FOX_F_BIAS_INIT + nrm((N_CD, FOX_HEADS), 0.1)
    sconv_w = nrm((N_CD, SC_WIDTH, SC_DIM), SC_WIDTH ** -0.5)
    cd_w_out = nrm((N_CD, CD_OUT, D_MODEL), CD_OUT ** -0.5)

    ffn_norm_w = 1.0 + nrm((DEPTH, D_MODEL), 0.02)
    ffn_w_gate = nrm((DEPTH, D_MODEL, D_FF), D_MODEL ** -0.5)
    ffn_w_up = nrm((DEPTH, D_MODEL, D_FF), D_MODEL ** -0.5)
    ffn_conv_w = nrm((DEPTH, FFN_CONV, D_FF), FFN_CONV ** -0.5)
    ffn_conv_b = nrm((DEPTH, D_FF), 0.02)
    ffn_w_down = nrm((DEPTH, D_FF, D_MODEL), D_FF ** -0.5)
    final_norm_w = 1.0 + nrm((D_MODEL,), 0.02)
    return {
        'x_prompt': x_prompt, 'x_sample': x_sample,
        'state_ret': state_ret, 'state_ssd': state_ssd, 'state_ssd_conv': state_ssd_conv,
        'cache_fox_k': cache_fox_k, 'cache_fox_v': cache_fox_v, 'cache_fox_logf': cache_fox_logf,
        'state_sconv': state_sconv, 'state_ffn_conv': state_ffn_conv,
        'ab_norm_w': ab_norm_w, 'ab_w_in': ab_w_in, 'ret_norm_w': ret_norm_w,
        'ssd_conv_w': ssd_conv_w, 'ssd_conv_b': ssd_conv_b, 'ssd_dt_bias': ssd_dt_bias,
        'ssd_A_log': ssd_A_log, 'ssd_D': ssd_D, 'ssd_norm_w': ssd_norm_w, 'ab_w_out': ab_w_out,
        'cd_norm_w': cd_norm_w, 'cd_w_in': cd_w_in, 'fox_f_bias': fox_f_bias, 'sconv_w': sconv_w,
        'cd_w_out': cd_w_out,
        'ffn_norm_w': ffn_norm_w, 'ffn_w_gate': ffn_w_gate, 'ffn_w_up': ffn_w_up,
        'ffn_conv_w': ffn_conv_w, 'ffn_conv_b': ffn_conv_b, 'ffn_w_down': ffn_w_down,
        'final_norm_w': final_norm_w,
    }


def reference(x_prompt, x_sample, state_ret, state_ssd, state_ssd_conv, cache_fox_k, cache_fox_v,
              cache_fox_logf, state_sconv, state_ffn_conv,
              ab_norm_w, ab_w_in, ret_norm_w, ssd_conv_w, ssd_conv_b, ssd_dt_bias, ssd_A_log, ssd_D,
              ssd_norm_w, ab_w_out, cd_norm_w, cd_w_in, fox_f_bias, sconv_w, cd_w_out,
              ffn_norm_w, ffn_w_gate, ffn_w_up, ffn_conv_w, ffn_conv_b, ffn_w_down, final_norm_w):
    W = dict(ab_norm_w=ab_norm_w, ab_w_in=ab_w_in, ret_norm_w=ret_norm_w, ssd_conv_w=ssd_conv_w,
             ssd_conv_b=ssd_conv_b, ssd_dt_bias=ssd_dt_bias, ssd_A_log=ssd_A_log, ssd_D=ssd_D,
             ssd_norm_w=ssd_norm_w, ab_w_out=ab_w_out, cd_norm_w=cd_norm_w, cd_w_in=cd_w_in,
             fox_f_bias=fox_f_bias, sconv_w=sconv_w, cd_w_out=cd_w_out, ffn_norm_w=ffn_norm_w,
             ffn_w_gate=ffn_w_gate, ffn_w_up=ffn_w_up, ffn_conv_w=ffn_conv_w, ffn_conv_b=ffn_conv_b,
             ffn_w_down=ffn_w_down, final_norm_w=final_norm_w)
    bp = x_prompt.shape[0]
    dtp = x_prompt.dtype
    (y_prompt, p_ret, p_ssd, p_ssd_conv, p_fox_k, p_fox_v, p_fox_logf, p_sconv, p_ffn_conv) = _trunk(
        x_prompt, 0,
        jnp.zeros((N_AB, bp, RET_HEADS, RET_DK, RET_DV), dtp),
        jnp.zeros((N_AB, bp, SSD_HEADS, SSD_HEADDIM, SSD_DSTATE), dtp),
        jnp.zeros((N_AB, bp, SSD_CONV - 1, SSD_CONV_DIM), dtp),
        jnp.zeros((N_CD, bp, 0, FOX_HEADS, FOX_HEAD_DIM), dtp),
        jnp.zeros((N_CD, bp, 0, FOX_HEADS, FOX_HEAD_DIM), dtp),
        jnp.zeros((N_CD, bp, 0, FOX_HEADS), dtp),
        jnp.zeros((N_CD, bp, SC_WIDTH - 1, SC_DIM), dtp),
        jnp.zeros((DEPTH, bp, FFN_CONV - 1, D_FF), dtp),
        W)
    (y_sample, s_ret, s_ssd, s_ssd_conv, s_fox_k, s_fox_v, s_fox_logf, s_sconv, s_ffn_conv) = _trunk(
        x_sample, cache_fox_k.shape[2], state_ret, state_ssd, state_ssd_conv, cache_fox_k, cache_fox_v,
        cache_fox_logf, state_sconv, state_ffn_conv, W)
    return (y_prompt, y_sample, p_ret, s_ret, p_ssd, s_ssd, p_ssd_conv, s_ssd_conv, p_fox_k, s_fox_k,
            p_fox_v, s_fox_v, p_fox_logf, s_fox_logf, p_sconv, s_sconv, p_ffn_conv, s_ffn_conv)
```

```python
import functools
import math

import numpy as np
import jax
import jax.numpy as jnp
from jax import lax
from jax.experimental import pallas as pl
from jax.experimental.pallas import tpu as pltpu

F32 = jnp.float32
BF16 = jnp.bfloat16
EPS = 1e-6
ROPE_BASE = 10000.0
NEG_INF = float("-inf")

D_MODEL = 2048
RET_HEADS, RET_DK, RET_DV = 4, 128, 256
SSD_DINNER, SSD_HEADDIM, SSD_HEADS, SSD_GROUPS, SSD_DSTATE, SSD_CONV = 1024, 64, 16, 2, 128, 4
SSD_CONV_DIM = SSD_DINNER + 2 * SSD_GROUPS * SSD_DSTATE
FOX_HEADS, FOX_HEAD_DIM = 8, 128
FOX_WIDTH = FOX_HEADS * FOX_HEAD_DIM
SC_DIM, SC_WIDTH = 1024, 3
D_FF, FFN_CONV = 5632, 3
AB_MAIN = 2 * RET_HEADS * RET_DK + 2 * RET_HEADS * RET_DV + SSD_DINNER + SSD_CONV_DIM
AB_PAD = AB_MAIN + 128
CD_MAIN = 3 * FOX_WIDTH + 3 * SC_DIM
CD_PAD = CD_MAIN + 128

LANES = 128
SUBLANES = 8
VMEM_LIMIT = 52 * 1024 * 1024


def _cparams(n_axes):
    return pltpu.CompilerParams(dimension_semantics=("arbitrary",) * n_axes,
                                vmem_limit_bytes=VMEM_LIMIT)


def _rms(xf, w):
    return xf * lax.rsqrt(jnp.mean(xf * xf, axis=-1, keepdims=True) + EPS) * w


def _softplus(x):
    return jnp.maximum(x, 0.0) + jnp.log1p(jnp.exp(-jnp.abs(x)))


def _split3(x):
    hi = x.astype(BF16)
    r1 = x - hi.astype(F32)
    mid = r1.astype(BF16)
    lo = (r1 - mid.astype(F32)).astype(BF16)
    return hi, mid, lo


def _dot(a, b):
    return jnp.dot(a, b, preferred_element_type=F32)


def _dot_nt(a, b):
    return lax.dot_general(a, b, (((1,), (1,)), ((), ())), preferred_element_type=F32)


def _dot_tn(a, b):
    return lax.dot_general(a, b, (((0,), (0,)), ((), ())), preferred_element_type=F32)


def _exact_lhs_dot(m_bf16, x):
    hi, mid, lo = _split3(x)
    return _dot(m_bf16, hi) + _dot(m_bf16, mid) + _dot(m_bf16, lo)


def _exact_rhs_dot(x, m_bf16):
    hi, mid, lo = _split3(x)
    return _dot(hi, m_bf16) + _dot(mid, m_bf16) + _dot(lo, m_bf16)


def _conv_rows(ext_ref, x, prev8, w_ref, width, rows):
    ext_ref[0:SUBLANES, :] = prev8
    ext_ref[SUBLANES:SUBLANES + rows, :] = x
    out = None
    for j in range(width):
        off = SUBLANES - (width - 1) + j
        term = ext_ref[off:off + rows, :] * w_ref[j:j + 1, :]
        out = term if out is None else out + term
    return out


def _seq_tiling(seq_len, tile_rows):
    if seq_len >= tile_rows:
        assert seq_len % tile_rows == 0
        return tile_rows, 1, seq_len // tile_rows
    assert tile_rows % seq_len == 0 and seq_len % SUBLANES == 0
    return seq_len, tile_rows // seq_len, 1


def _norm_matmul_kernel(x_ref, nw_ref, w_ref, o_ref, h_ref):
    @pl.when(pl.program_id(1) == 0)
    def _():
        h_ref[...] = _rms(x_ref[...], nw_ref[...]).astype(BF16)

    o_ref[...] = _dot(h_ref[...], w_ref[...])


def _norm_matmul(x, norm_w, w, *, tm, tn):
    m, d = x.shape
    n = w.shape[1]
    assert m % tm == 0 and n % tn == 0
    return pl.pallas_call(
        _norm_matmul_kernel,
        grid=(m // tm, n // tn),
        in_specs=[pl.BlockSpec((tm, d), lambda i, j: (i, 0)),
                  pl.BlockSpec((1, d), lambda i, j: (0, 0)),
                  pl.BlockSpec((d, tn), lambda i, j: (0, j))],
        out_specs=pl.BlockSpec((tm, tn), lambda i, j: (i, j)),
        out_shape=jax.ShapeDtypeStruct((m, n), F32),
        scratch_shapes=[pltpu.VMEM((tm, d), BF16)],
        compiler_params=_cparams(2),
        name="norm_in_proj",
    )(x, norm_w.reshape(1, d), w)


def _proj_res_kernel(x_ref, a_ref, b_ref, wa_ref, wb_ref, o_ref):
    acc = _dot(a_ref[...], wa_ref[...])
    acc = acc + _dot(b_ref[...], wb_ref[...])
    o_ref[...] = x_ref[...] + acc


def _proj_residual(x, a, b, wa, wb, *, tm, tn):
    m, d = x.shape
    ka, kb = a.shape[1], b.shape[1]
    assert m % tm == 0 and d % tn == 0
    return pl.pallas_call(
        _proj_res_kernel,
        grid=(m // tm, d // tn),
        in_specs=[pl.BlockSpec((tm, tn), lambda i, j: (i, j)),
                  pl.BlockSpec((tm, ka), lambda i, j: (i, 0)),
                  pl.BlockSpec((tm, kb), lambda i, j: (i, 0)),
                  pl.BlockSpec((ka, tn), lambda i, j: (0, j)),
                  pl.BlockSpec((kb, tn), lambda i, j: (0, j))],
        out_specs=pl.BlockSpec((tm, tn), lambda i, j: (i, j)),
        out_shape=jax.ShapeDtypeStruct((m, d), F32),
        compiler_params=_cparams(2),
        name="out_proj_residual",
    )(x, a, b, wa, wb)


def _ffn_kernel(x_ref, nw_ref, wg_ref, wu_ref, wd_ref, cw_ref, cb_ref, hist_ref, fw_ref,
                o_ref, tail_ref, h_ref, carry_ref, ext_ref, *, tm, rows, spt, tpb, nf, final):
    i = pl.program_id(0)
    f = pl.program_id(1)

    @pl.when(f == 0)
    def _():
        xf = x_ref[...]
        h_ref[...] = _rms(xf, nw_ref[...]).astype(BF16)
        o_ref[...] = xf

    h = h_ref[...]
    a = _dot(h, wg_ref[...])
    u = _dot(h, wu_ref[...])
    first = (i % tpb) == 0
    convs = []
    for s in range(spt):
        a_s = a[s * rows:(s + 1) * rows]
        if tpb == 1:
            prev = hist_ref[s]
        else:
            prev = jnp.where(first, hist_ref[s], carry_ref[f])
        convs.append(_conv_rows(ext_ref, a_s, prev, cw_ref, FFN_CONV, rows))
        tail_ref[s] = a_s[rows - SUBLANES:rows]
    if tpb > 1:
        carry_ref[f] = a[tm - SUBLANES:tm]
    conv = convs[0] if spt == 1 else jnp.concatenate(convs, axis=0)
    act = (jax.nn.silu(conv + cb_ref[...]) * u).astype(BF16)
    o_ref[...] += _dot(act, wd_ref[...])
    if final:
        @pl.when(f == nf - 1)
        def _():
            o_ref[...] = _rms(o_ref[...], fw_ref[...])


def _conv_ffn(x, norm_w, wg, wu, wd, conv_w, conv_b, hist8, final_w, *, seq_len, tm, tf, final):
    m, d = x.shape
    ff = wg.shape[1]
    assert m % tm == 0 and ff % tf == 0
    rows, spt, tpb = _seq_tiling(seq_len, tm)
    nm, nf = m // tm, ff // tf
    hist_map = (lambda i, f: (i // tpb, 0, f)) if spt == 1 else (lambda i, f: (i, 0, f))
    kern = functools.partial(_ffn_kernel, tm=tm, rows=rows, spt=spt, tpb=tpb, nf=nf, final=final)
    out, tails = pl.pallas_call(
        kern,
        grid=(nm, nf),
        in_specs=[pl.BlockSpec((tm, d), lambda i, f: (i, 0)),
                  pl.BlockSpec((1, d), lambda i, f: (0, 0)),
                  pl.BlockSpec((d, tf), lambda i, f: (0, f)),
                  pl.BlockSpec((d, tf), lambda i, f: (0, f)),
                  pl.BlockSpec((tf, d), lambda i, f: (f, 0)),
                  pl.BlockSpec((FFN_CONV, tf), lambda i, f: (0, f)),
                  pl.BlockSpec((1, tf), lambda i, f: (0, f)),
                  pl.BlockSpec((spt, SUBLANES, tf), hist_map),
                  pl.BlockSpec((1, d), lambda i, f: (0, 0))],
        out_specs=[pl.BlockSpec((tm, d), lambda i, f: (i, 0)),
                   pl.BlockSpec((spt, SUBLANES, tf), lambda i, f: (i, 0, f))],
        out_shape=[jax.ShapeDtypeStruct((m, d), F32),
                   jax.ShapeDtypeStruct((nm * spt, SUBLANES, ff), F32)],
        scratch_shapes=[pltpu.VMEM((tm, d), BF16),
                        pltpu.VMEM((nf, SUBLANES, tf), F32),
                        pltpu.VMEM((rows + SUBLANES, tf), F32)],
        compiler_params=_cparams(2),
        name="conv_ffn",
    )(x, norm_w.reshape(1, d), wg, wu, wd, conv_w, conv_b.reshape(1, ff), hist8, final_w.reshape(1, d))
    return out, tails


def _retention_kernel(q_ref, k_ref, v_ref, g_ref, cos_ref, sin_ref, st_ref, nw_ref,
                      y_ref, so_ref, *, c):
    ci = pl.program_id(1)

    @pl.when(ci == 0)
    def _():
        so_ref[...] = st_ref[...]

    cos = cos_ref[...]
    sin = sin_ref[...]
    ii = lax.broadcasted_iota(jnp.int32, (c, c), 0)
    jj = lax.broadcasted_iota(jnp.int32, (c, c), 1)
    diff = (ii - jj).astype(F32)
    causal = ii >= jj
    ridx = lax.broadcasted_iota(jnp.int32, (c, 1), 0).astype(F32)
    for h in range(RET_HEADS):
        lg = math.log1p(-(2.0 ** (-5.0 - h)))
        q = q_ref[:, h * RET_DK:(h + 1) * RET_DK]
        k = k_ref[:, h * RET_DK:(h + 1) * RET_DK]
        v = v_ref[:, h * RET_DV:(h + 1) * RET_DV]
        qr = q * cos + pltpu.roll(q, RET_DK // 2, 1) * sin
        kr = (k * cos + pltpu.roll(k, RET_DK // 2, 1) * sin) * (RET_DK ** -0.5)
        qb = qr.astype(BF16)
        kb = kr.astype(BF16)
        vb = v.astype(BF16)
        decay = jnp.exp(jnp.where(causal, diff * lg, NEG_INF))
        inner = jnp.exp((ridx + 1.0) * lg)
        sdecay = jnp.exp((c - 1.0 - ridx) * lg)
        s = so_ref[0, h]
        scores = _dot_nt(qb, kb) * decay
        y = _dot(scores.astype(BF16), vb)
        y = y + _dot(qb, s.astype(BF16)) * inner
        kd = (kr * sdecay).astype(BF16)
        so_ref[0, h] = math.exp(c * lg) * s + _dot_tn(kd, vb)
        mu = jnp.mean(y, axis=-1, keepdims=True)
        yc = y - mu
        var = jnp.mean(yc * yc, axis=-1, keepdims=True)
        yn = yc * lax.rsqrt(var + EPS) * nw_ref[:, h * RET_DV:(h + 1) * RET_DV]
        g = g_ref[:, h * RET_DV:(h + 1) * RET_DV]
        y_ref[:, h * RET_DV:(h + 1) * RET_DV] = (jax.nn.silu(g) * yn).astype(BF16)


def _retention(proj, cosf, sinf, state, norm_w, *, bsz, seq_len, c):
    m = proj.shape[0]
    nc = seq_len // c
    assert seq_len % c == 0
    qk_w = RET_HEADS * RET_DK
    v_w = RET_HEADS * RET_DV
    row = lambda b, ci: b * nc + ci
    y, s_new = pl.pallas_call(
        functools.partial(_retention_kernel, c=c),
        grid=(bsz, nc),
        in_specs=[pl.BlockSpec((c, qk_w), lambda b, ci: (row(b, ci), 0)),
                  pl.BlockSpec((c, qk_w), lambda b, ci: (row(b, ci), 1)),
                  pl.BlockSpec((c, v_w), lambda b, ci: (row(b, ci), 1)),
                  pl.BlockSpec((c, v_w), lambda b, ci: (row(b, ci), 2)),
                  pl.BlockSpec((c, RET_DK), lambda b, ci: (ci, 0)),
                  pl.BlockSpec((c, RET_DK), lambda b, ci: (ci, 0)),
                  pl.BlockSpec((1, RET_HEADS, RET_DK, RET_DV), lambda b, ci: (b, 0, 0, 0)),
                  pl.BlockSpec((1, v_w), lambda b, ci: (0, 0))],
        out_specs=[pl.BlockSpec((c, v_w), lambda b, ci: (row(b, ci), 0)),
                   pl.BlockSpec((1, RET_HEADS, RET_DK, RET_DV), lambda b, ci: (b, 0, 0, 0))],
        out_shape=[jax.ShapeDtypeStruct((m, v_w), BF16),
                   jax.ShapeDtypeStruct(state.shape, F32)],
        compiler_params=_cparams(2),
        name="retention",
    )(proj, proj, proj, proj, cosf, sinf, state, norm_w.reshape(1, v_w))
    return y, s_new


def _ssd_kernel(z_ref, xs_ref, bc_ref, dt_ref, hx_ref, hbc_ref, st_ref,
                cwx_ref, cwbc_ref, cbx_ref, cbbc_ref, dtb_ref, alog_ref, dsk_ref, nw_ref,
                tri_ref, exp_ref,
                y_ref, so_ref,
                st_scr, cx_scr, cbc_scr, extx_scr, extbc_scr, yh_scr, *, c, nc):
    ci = pl.program_id(1)
    gw = SSD_DINNER // SSD_GROUPS
    hpg = SSD_HEADS // SSD_GROUPS

    @pl.when(ci == 0)
    def _():
        st_scr[...] = st_ref[0].T
        cx_scr[...] = hx_ref[0]
        cbc_scr[...] = hbc_ref[0]

    xs_raw = xs_ref[...]
    bc_raw = bc_ref[...]
    xs = jax.nn.silu(_conv_rows(extx_scr, xs_raw, cx_scr[...], cwx_ref, SSD_CONV, c) + cbx_ref[...])
    bcm = jax.nn.silu(_conv_rows(extbc_scr, bc_raw, cbc_scr[...], cwbc_ref, SSD_CONV, c) + cbbc_ref[...])
    cx_scr[...] = xs_raw[c - SUBLANES:c]
    cbc_scr[...] = bc_raw[c - SUBLANES:c]

    tri = tri_ref[...]
    expand = exp_ref[...]
    dt = _softplus(dt_ref[...] + dtb_ref[...])
    a = -jnp.exp(alog_ref[...])
    acs = _exact_lhs_dot(tri, dt * a)
    acs_t = acs.T
    acs_last = acs[c - 1:c, :]
    exp_acs = jnp.exp(acs)
    to_end = jnp.exp(acs_last - acs)
    dt_e = _exact_rhs_dot(dt, expand)
    to_end_e = _exact_rhs_dot(to_end, expand)
    exp_acs_e = _exact_rhs_dot(exp_acs, expand)
    chunk_dec_e = _exact_rhs_dot(jnp.exp(acs_last), expand)

    xdt = xs * dt_e
    xdt_b = xdt.astype(BF16)
    xend_b = (xdt * to_end_e).astype(BF16)
    ii = lax.broadcasted_iota(jnp.int32, (c, c), 0)
    jj = lax.broadcasted_iota(jnp.int32, (c, c), 1)
    causal = ii >= jj
    nb = SSD_GROUPS * SSD_DSTATE
    for g in range(SSD_GROUPS):
        b_g = bcm[:, g * SSD_DSTATE:(g + 1) * SSD_DSTATE].astype(BF16)
        c_g = bcm[:, nb + g * SSD_DSTATE:nb + (g + 1) * SSD_DSTATE].astype(BF16)
        cb = _dot_nt(c_g, b_g)
        s_g = st_scr[:, g * gw:(g + 1) * gw]
        y_state = _dot(c_g, s_g.astype(BF16)) * exp_acs_e[:, g * gw:(g + 1) * gw]
        for r in range(hpg):
            hh = g * hpg + r
            seg = acs[:, hh:hh + 1] - acs_t[hh:hh + 1, :]
            lmat = jnp.exp(jnp.where(causal, seg, NEG_INF))
            mm = (cb * lmat).astype(BF16)
            lo = hh * SSD_HEADDIM
            yh_scr[:, lo:lo + SSD_HEADDIM] = (
                _dot(mm, xdt_b[:, lo:lo + SSD_HEADDIM]) + y_state[:, r * SSD_HEADDIM:(r + 1) * SSD_HEADDIM])
        upd = _dot_tn(b_g, xend_b[:, g * gw:(g + 1) * gw])
        st_scr[:, g * gw:(g + 1) * gw] = chunk_dec_e[:, g * gw:(g + 1) * gw] * s_g + upd

    y = yh_scr[...] + dsk_ref[...] * xs
    z = z_ref[...]
    y_ref[...] = _rms(y * jax.nn.silu(z), nw_ref[...]).astype(BF16)

    @pl.when(ci == nc - 1)
    def _():
        so_ref[0] = st_scr[...].T


def _ssd(proj, hist8, state, conv_w, conv_b, dt_bias, a_log, d_skip, norm_w, *, bsz, seq_len, c):
    m = proj.shape[0]
    nc = seq_len // c
    assert seq_len % c == 0
    row = lambda b, ci: b * nc + ci
    const2 = lambda b, ci: (0, 0)
    di, bcw = SSD_DINNER, 2 * SSD_GROUPS * SSD_DSTATE
    tri = jnp.asarray(np.tril(np.ones((c, c), np.float32)), BF16)
    expand = np.zeros((LANES, di), np.float32)
    for h in range(SSD_HEADS):
        expand[h, h * SSD_HEADDIM:(h + 1) * SSD_HEADDIM] = 1.0
    expand = jnp.asarray(expand, BF16)
    pad_row = lambda v: jnp.pad(v.astype(F32), (0, LANES - v.shape[0])).reshape(1, LANES)
    st2 = state.reshape(bsz, di, SSD_DSTATE)
    y, s_new = pl.pallas_call(
        functools.partial(_ssd_kernel, c=c, nc=nc),
        grid=(bsz, nc),
        in_specs=[pl.BlockSpec((c, di), lambda b, ci: (row(b, ci), 3)),
                  pl.BlockSpec((c, di), lambda b, ci: (row(b, ci), 4)),
                  pl.BlockSpec((c, bcw), lambda b, ci: (row(b, ci), 10)),
                  pl.BlockSpec((c, LANES), lambda b, ci: (row(b, ci), AB_MAIN // LANES)),
                  pl.BlockSpec((1, SUBLANES, di), lambda b, ci: (b, 0, 0)),
                  pl.BlockSpec((1, SUBLANES, bcw), lambda b, ci: (b, 0, 2)),
                  pl.BlockSpec((1, di, SSD_DSTATE), lambda b, ci: (b, 0, 0)),
                  pl.BlockSpec((SSD_CONV, di), const2),
                  pl.BlockSpec((SSD_CONV, bcw), lambda b, ci: (0, 2)),
                  pl.BlockSpec((1, di), const2),
                  pl.BlockSpec((1, bcw), lambda b, ci: (0, 2)),
                  pl.BlockSpec((1, LANES), const2),
                  pl.BlockSpec((1, LANES), const2),
                  pl.BlockSpec((1, di), const2),
                  pl.BlockSpec((1, di), const2),
                  pl.BlockSpec((c, c), const2),
                  pl.BlockSpec((LANES, di), const2)],
        out_specs=[pl.BlockSpec((c, di), lambda b, ci: (row(b, ci), 0)),
                   pl.BlockSpec((1, di, SSD_DSTATE), lambda b, ci: (b, 0, 0))],
        out_shape=[jax.ShapeDtypeStruct((m, di), BF16),
                   jax.ShapeDtypeStruct(st2.shape, F32)],
        scratch_shapes=[pltpu.VMEM((SSD_DSTATE, di), F32),
                        pltpu.VMEM((SUBLANES, di), F32),
                        pltpu.VMEM((SUBLANES, bcw), F32),
                        pltpu.VMEM((c + SUBLANES, di), F32),
                        pltpu.VMEM((c + SUBLANES, bcw), F32),
                        pltpu.VMEM((c, di), F32)],
        compiler_params=_cparams(2),
        name="ssd",
    )(proj, proj, proj, proj, hist8, hist8, st2,
      conv_w, conv_w, conv_b.reshape(1, -1), conv_b.reshape(1, -1),
      pad_row(dt_bias), pad_row(a_log), jnp.repeat(d_skip.astype(F32), SSD_HEADDIM).reshape(1, di),
      norm_w.reshape(1, di), tri, expand)
    return y, s_new.reshape(state.shape)


def _logf_cum_kernel(x_ref, b_ref, tri_ref, lf_ref, cum_ref, cumt_ref, carry_ref, *, c, apply_gate):
    @pl.when(pl.program_id(1) == 0)
    def _():
        carry_ref[...] = jnp.zeros_like(carry_ref)

    x = x_ref[...]
    if apply_gate:
        lf = -_softplus(-(x + b_ref[...]))
    else:
        lf = x
    lf_ref[...] = lf
    cum = _exact_lhs_dot(tri_ref[...], lf) + carry_ref[...]
    carry_ref[...] = cum[c - 1:c, :]
    cum_ref[...] = cum
    cumt_ref[0] = cum.T[0:SUBLANES, :]


def _logf_cum(src, col_block, bias, *, bsz, seq_len, c, apply_gate):
    m = bsz * seq_len
    nc = seq_len // c
    assert seq_len % c == 0 and c % LANES == 0
    tri = jnp.asarray(np.tril(np.ones((c, c), np.float32)), BF16)
    row = lambda b, ci: b * nc + ci
    return pl.pallas_call(
        functools.partial(_logf_cum_kernel, c=c, apply_gate=apply_gate),
        grid=(bsz, nc),
        in_specs=[pl.BlockSpec((c, LANES), lambda b, ci: (row(b, ci), col_block)),
                  pl.BlockSpec((1, LANES), lambda b, ci: (0, 0)),
                  pl.BlockSpec((c, c), lambda b, ci: (0, 0))],
        out_specs=[pl.BlockSpec((c, LANES), lambda b, ci: (row(b, ci), 0)),
                   pl.BlockSpec((c, LANES), lambda b, ci: (row(b, ci), 0)),
                   pl.BlockSpec((1, SUBLANES, c), lambda b, ci: (b, 0, ci))],
        out_shape=[jax.ShapeDtypeStruct((m, LANES), F32),
                   jax.ShapeDtypeStruct((m, LANES), F32),
                   jax.ShapeDtypeStruct((bsz, SUBLANES, seq_len), F32)],
        scratch_shapes=[pltpu.VMEM((1, LANES), F32)],
        compiler_params=_cparams(2),
        name="logf_cumsum",
    )(src, bias, tri)


def _fox_kernel(qi_ref, ki_ref, q_ref, k_ref, v_ref, cum_ref, cumt_ref, o_ref,
                m_ref, l_ref, acc_ref, cq_ref, *, t):
    h = pl.program_id(1)
    step = pl.program_id(2)
    qi = qi_ref[step]
    ki = ki_ref[step]

    @pl.when(ki == 0)
    def _():
        m_ref[...] = jnp.full_like(m_ref, NEG_INF)
        l_ref[...] = jnp.zeros_like(l_ref)
        acc_ref[...] = jnp.zeros_like(acc_ref)
        lane = lax.broadcasted_iota(jnp.int32, (t, LANES), 1)
        cq_ref[...] = jnp.sum(jnp.where(lane == h, cum_ref[...], 0.0), axis=1, keepdims=True)

    q = q_ref[...].astype(BF16)
    k = k_ref[...].astype(BF16)
    s = _dot_nt(q, k) * (FOX_HEAD_DIM ** -0.5)
    ck = cumt_ref[0, pl.ds(h, 1), :]
    s = s + (cq_ref[...] - ck)
    rows = lax.broadcasted_iota(jnp.int32, (t, t), 0)
    cols = lax.broadcasted_iota(jnp.int32, (t, t), 1)
    s = jnp.where(jnp.logical_or(ki < qi, cols <= rows), s, NEG_INF)
    m_old = m_ref[...]
    m_new = jnp.maximum(m_old, jnp.max(s, axis=1, keepdims=True))
    alpha = jnp.exp(m_old - m_new)
    p = jnp.exp(s - m_new)
    l_ref[...] = alpha * l_ref[...] + jnp.sum(p, axis=1, keepdims=True)
    acc_ref[...] = alpha * acc_ref[...] + _dot(p.astype(BF16), v_ref[...].astype(BF16))
    m_ref[...] = m_new

    @pl.when(ki == qi)
    def _():
        o_ref[...] = (acc_ref[...] / l_ref[...]).astype(BF16)


def _fox_prompt(proj, cum, cum_t, *, bsz, seq_len, t):
    m = proj.shape[0]
    nq = seq_len // t
    assert seq_len % t == 0
    pairs = [(qi, ki) for qi in range(nq) for ki in range(qi + 1)]
    qi_tab = jnp.asarray([p[0] for p in pairs], jnp.int32)
    ki_tab = jnp.asarray([p[1] for p in pairs], jnp.int32)
    nh = FOX_HEADS
    grid_spec = pltpu.PrefetchScalarGridSpec(
        num_scalar_prefetch=2,
        grid=(bsz, nh, len(pairs)),
        in_specs=[pl.BlockSpec((t, FOX_HEAD_DIM), lambda b, h, s, qi, ki: (b * nq + qi[s], h)),
                  pl.BlockSpec((t, FOX_HEAD_DIM), lambda b, h, s, qi, ki: (b * nq + ki[s], nh + h)),
                  pl.BlockSpec((t, FOX_HEAD_DIM), lambda b, h, s, qi, ki: (b * nq + ki[s], 2 * nh + h)),
                  pl.BlockSpec((t, LANES), lambda b, h, s, qi, ki: (b * nq + qi[s], 0)),
                  pl.BlockSpec((1, SUBLANES, t), lambda b, h, s, qi, ki: (b, 0, ki[s]))],
        out_specs=pl.BlockSpec((t, FOX_HEAD_DIM), lambda b, h, s, qi, ki: (b * nq + qi[s], h)),
        scratch_shapes=[pltpu.VMEM((t, 1), F32), pltpu.VMEM((t, 1), F32),
                        pltpu.VMEM((t, FOX_HEAD_DIM), F32), pltpu.VMEM((t, 1), F32)],
    )
    return pl.pallas_call(
        functools.partial(_fox_kernel, t=t),
        grid_spec=grid_spec,
        out_shape=jax.ShapeDtypeStruct((m, FOX_WIDTH), BF16),
        compiler_params=_cparams(3),
        name="fox_attention",
    )(qi_tab, ki_tab, proj, proj, proj, cum, cum_t)


def _fox_decode_kernel(q_ref, kn_ref, vn_ref, kc_ref, vc_ref, cum_ref, ctc_ref, ctn_ref, o_ref,
                       m_ref, l_ref, acc_ref, *, lq, ncb):
    j = pl.program_id(1)

    @pl.when(j == 0)
    def _():
        m_ref[...] = jnp.full_like(m_ref, NEG_INF)
        l_ref[...] = jnp.zeros_like(l_ref)
        acc_ref[...] = jnp.zeros_like(acc_ref)

    def attend(k_head, v_head, ck_all, causal):
        tk = ck_all.shape[1]
        for h in range(FOX_HEADS):
            lo = h * FOX_HEAD_DIM
            qh = q_ref[:, lo:lo + FOX_HEAD_DIM].astype(BF16)
            kh = k_head(lo).astype(BF16)
            vh = v_head(lo).astype(BF16)
            s = _dot_nt(qh, kh) * (FOX_HEAD_DIM ** -0.5)
            s = s + (cum_ref[0, :, h:h + 1] - ck_all[h:h + 1, :])
            if causal:
                rows = lax.broadcasted_iota(jnp.int32, (lq, tk), 0)
                cols = lax.broadcasted_iota(jnp.int32, (lq, tk), 1)
                s = jnp.where(cols <= rows, s, NEG_INF)
            m_old = m_ref[h]
            m_new = jnp.maximum(m_old, jnp.max(s, axis=1, keepdims=True))
            alpha = jnp.exp(m_old - m_new)
            p = jnp.exp(s - m_new)
            l_ref[h] = alpha * l_ref[h] + jnp.sum(p, axis=1, keepdims=True)
            acc_ref[:, lo:lo + FOX_HEAD_DIM] = alpha * acc_ref[:, lo:lo + FOX_HEAD_DIM] + _dot(p.astype(BF16), vh)
            m_ref[h] = m_new

    @pl.when(j < ncb)
    def _():
        attend(lambda lo: kc_ref[0, :, lo:lo + FOX_HEAD_DIM], lambda lo: vc_ref[0, :, lo:lo + FOX_HEAD_DIM],
               ctc_ref[0], False)

    @pl.when(j == ncb)
    def _():
        attend(lambda lo: kn_ref[:, lo:lo + FOX_HEAD_DIM], lambda lo: vn_ref[:, lo:lo + FOX_HEAD_DIM],
               ctn_ref[0][:, 0:lq], True)
        for h in range(FOX_HEADS):
            lo = h * FOX_HEAD_DIM
            o_ref[:, lo:lo + FOX_HEAD_DIM] = (acc_ref[:, lo:lo + FOX_HEAD_DIM] / l_ref[h]).astype(BF16)


def _fox_decode(proj, cache_k, cache_v, cum, cum_t, *, bsz, lq, past, tk):
    assert past % tk == 0 and past % lq == 0 and past % LANES == 0 and lq <= LANES
    ncb = past // tk
    w = FOX_WIDTH
    cache_blk = lambda b, j: (b, jnp.minimum(j, ncb - 1), 0)
    return pl.pallas_call(
        functools.partial(_fox_decode_kernel, lq=lq, ncb=ncb),
        grid=(bsz, ncb + 1),
        in_specs=[pl.BlockSpec((lq, w), lambda b, j: (b, 0)),
                  pl.BlockSpec((lq, w), lambda b, j: (b, 1)),
                  pl.BlockSpec((lq, w), lambda b, j: (b, 2)),
                  pl.BlockSpec((1, tk, w), cache_blk),
                  pl.BlockSpec((1, tk, w), cache_blk),
                  pl.BlockSpec((1, lq, LANES), lambda b, j: (b, past // lq, 0)),
                  pl.BlockSpec((1, SUBLANES, tk), lambda b, j: (b, 0, jnp.minimum(j, ncb - 1))),
                  pl.BlockSpec((1, SUBLANES, LANES), lambda b, j: (b, 0, past // LANES))],
        out_specs=pl.BlockSpec((lq, w), lambda b, j: (b, 0)),
        out_shape=jax.ShapeDtypeStruct((bsz * lq, w), BF16),
        scratch_shapes=[pltpu.VMEM((FOX_HEADS, lq, 1), F32), pltpu.VMEM((FOX_HEADS, lq, 1), F32),
                        pltpu.VMEM((lq, w), F32)],
        compiler_params=_cparams(2),
        name="fox_decode",
    )(proj, proj, proj, cache_k, cache_v, cum, cum_t, cum_t)


def _sconv_kernel(u_ref, bg_ref, cg_ref, cw_ref, hist_ref, y_ref, tail_ref, carry_ref, ext_ref,
                  *, tm, rows, spt, tpb):
    i = pl.program_id(0)
    w = cg_ref[...] * u_ref[...]
    first = (i % tpb) == 0
    for s in range(spt):
        w_s = w[s * rows:(s + 1) * rows]
        if tpb == 1:
            prev = hist_ref[s]
        else:
            prev = jnp.where(first, hist_ref[s], carry_ref[...])
        conv = _conv_rows(ext_ref, w_s, prev, cw_ref, SC_WIDTH, rows)
        y_ref[s * rows:(s + 1) * rows, :] = (bg_ref[s * rows:(s + 1) * rows, :] * conv).astype(BF16)
        tail_ref[s] = w_s[rows - SUBLANES:rows]
    if tpb > 1:
        carry_ref[...] = w[tm - SUBLANES:tm]


def _sconv(proj, conv_w, hist8, *, seq_len, tm):
    m = proj.shape[0]
    assert m % tm == 0
    rows, spt, tpb = _seq_tiling(seq_len, tm)
    nm = m // tm
    hist_map = (lambda i: (i // tpb, 0, 0)) if spt == 1 else (lambda i: (i, 0, 0))
    base = 3 * FOX_WIDTH // SC_DIM
    return pl.pallas_call(
        functools.partial(_sconv_kernel, tm=tm, rows=rows, spt=spt, tpb=tpb),
        grid=(nm,),
        in_specs=[pl.BlockSpec((tm, SC_DIM), lambda i: (i, base)),
                  pl.BlockSpec((tm, SC_DIM), lambda i: (i, base + 1)),
                  pl.BlockSpec((tm, SC_DIM), lambda i: (i, base + 2)),
                  pl.BlockSpec((SC_WIDTH, SC_DIM), lambda i: (0, 0)),
                  pl.BlockSpec((spt, SUBLANES, SC_DIM), hist_map)],
        out_specs=[pl.BlockSpec((tm, SC_DIM), lambda i: (i, 0)),
                   pl.BlockSpec((spt, SUBLANES, SC_DIM), lambda i: (i, 0, 0))],
        out_shape=[jax.ShapeDtypeStruct((m, SC_DIM), BF16),
                   jax.ShapeDtypeStruct((nm * spt, SUBLANES, SC_DIM), F32)],
        scratch_shapes=[pltpu.VMEM((SUBLANES, SC_DIM), F32),
                        pltpu.VMEM((rows + SUBLANES, SC_DIM), F32)],
        compiler_params=_cparams(1),
        name="gated_short_conv",
    )(proj, proj, proj, conv_w, hist8)


def _hist8(state):
    n, w1, c = state.shape
    return jnp.concatenate([jnp.zeros((n, SUBLANES - w1, c), F32), state.astype(F32)], axis=1)


def _tails(tails, n_seq, seq_len, tile_rows, keep):
    per_seq = max(1, seq_len // tile_rows)
    idx = (jnp.arange(n_seq) + 1) * per_seq - 1
    return tails[idx][:, SUBLANES - keep:, :]


def _rope_tables(pos0, length):
    half = RET_DK // 2
    inv = ROPE_BASE ** (-jnp.arange(half, dtype=F32) / half)
    ang = (pos0 + jnp.arange(length)).astype(F32)[:, None] * inv[None, :]
    cos, sin = jnp.cos(ang), jnp.sin(ang)
    return jnp.concatenate([cos, cos], axis=1), jnp.concatenate([-sin, sin], axis=1)


def _prep_weights(p):
    d = D_MODEL
    ab_in = p['ab_w_in'][0]
    cd_in = p['cd_w_in'][0]
    f0 = 3 * FOX_WIDTH
    return dict(
        ab_in=jnp.concatenate([ab_in, jnp.zeros((d, AB_PAD - ab_in.shape[1]), F32)], axis=1).astype(BF16),
        cd_in=jnp.concatenate([cd_in[:, :f0], cd_in[:, f0 + FOX_HEADS:], cd_in[:, f0:f0 + FOX_HEADS],
                               jnp.zeros((d, LANES - FOX_HEADS), F32)], axis=1).astype(BF16),
        ab_out=p['ab_w_out'][0].astype(BF16),
        cd_out=p['cd_w_out'][0].astype(BF16),
        ffn_gate=p['ffn_w_gate'].astype(BF16),
        ffn_up=p['ffn_w_up'].astype(BF16),
        ffn_down=p['ffn_w_down'].astype(BF16),
    )


def _trunk(x, pos0, st_ret, st_ssd, st_ssd_conv, c_k, c_v, c_logf, st_sconv, st_ffn, p, wb, t):
    bsz, length, d = x.shape
    m = bsz * length
    xf = x.reshape(m, d)
    zeros = lambda *shape: jnp.zeros(shape, F32)

    proj = _norm_matmul(xf, p['ab_norm_w'][0], wb['ab_in'], tm=t['tm_proj'], tn=t['tn_ab'])
    cosf, sinf = _rope_tables(pos0, length)
    ret_state = zeros(bsz, RET_HEADS, RET_DK, RET_DV) if st_ret is None else st_ret
    y_ret, ret_new = _retention(proj, cosf, sinf, ret_state, p['ret_norm_w'][0],
                                bsz=bsz, seq_len=length, c=t['c_ret'])
    ssd_state = zeros(bsz, SSD_HEADS, SSD_HEADDIM, SSD_DSTATE) if st_ssd is None else st_ssd
    ssd_hist = zeros(bsz, SSD_CONV - 1, SSD_CONV_DIM) if st_ssd_conv is None else st_ssd_conv
    y_ssd, ssd_new = _ssd(proj, _hist8(ssd_hist), ssd_state, p['ssd_conv_w'][0], p['ssd_conv_b'][0],
                          p['ssd_dt_bias'][0], p['ssd_A_log'][0], p['ssd_D'][0], p['ssd_norm_w'][0],
                          bsz=bsz, seq_len=length, c=t['c_ssd'])
    xbc_lo = AB_MAIN - SSD_CONV_DIM
    ssd_conv_new = proj.reshape(bsz, length, -1)[:, length - (SSD_CONV - 1):, xbc_lo:AB_MAIN]
    ab_out = wb['ab_out']
    xf = _proj_residual(xf, y_ret, y_ssd, ab_out[:RET_HEADS * RET_DV], ab_out[RET_HEADS * RET_DV:],
                        tm=t['tm_out'], tn=t['tn_out'])

    ffn_new = []
    ffn_hist0 = zeros(bsz, FFN_CONV - 1, D_FF) if st_ffn is None else st_ffn[0]
    xf, tails = _conv_ffn(xf, p['ffn_norm_w'][0], wb['ffn_gate'][0], wb['ffn_up'][0], wb['ffn_down'][0],
                          p['ffn_conv_w'][0], p['ffn_conv_b'][0], _hist8(ffn_hist0), p['final_norm_w'],
                          seq_len=length, tm=t['tm_ffn'], tf=t['tf_ffn'], final=False)
    ffn_new.append(_tails(tails, bsz, length, t['tm_ffn'], FFN_CONV - 1))

    proj = _norm_matmul(xf, p['cd_norm_w'][0], wb['cd_in'], tm=t['tm_proj'], tn=t['tn_cd'])
    proj3 = proj.reshape(bsz, length, -1)
    k_new = proj3[:, :, FOX_WIDTH:2 * FOX_WIDTH].reshape(bsz, length, FOX_HEADS, FOX_HEAD_DIM)
    v_new = proj3[:, :, 2 * FOX_WIDTH:3 * FOX_WIDTH].reshape(bsz, length, FOX_HEADS, FOX_HEAD_DIM)
    f_bias = jnp.pad(p['fox_f_bias'][0].astype(F32), (0, LANES - FOX_HEADS)).reshape(1, LANES)
    fl_block = CD_MAIN // LANES
    if c_k is None:
        logf, cum, cum_t = _logf_cum(proj, fl_block, f_bias, bsz=bsz, seq_len=length, c=t['c_cum'],
                                     apply_gate=True)
        y_fox = _fox_prompt(proj, cum, cum_t, bsz=bsz, seq_len=length, t=t['t_fox'])
        logf_new = logf.reshape(bsz, length, LANES)[:, :, :FOX_HEADS]
    else:
        past = c_k.shape[1]
        assert length % LANES == 0 or LANES % length == 0
        c_small = max(length, LANES)
        pad_rows = c_small - length
        proj_fl = proj3[:, :, CD_MAIN:]
        if pad_rows:
            proj_fl = jnp.pad(proj_fl, ((0, 0), (0, pad_rows), (0, 0)))
        logf, _, _ = _logf_cum(proj_fl.reshape(bsz * c_small, LANES), 0, f_bias, bsz=bsz, seq_len=c_small,
                               c=c_small, apply_gate=True)
        logf_new = logf.reshape(bsz, c_small, LANES)[:, :length, :FOX_HEADS]
        c_cum = t['c_cum']
        lp = -(-(past + length) // c_cum) * c_cum
        all_lf = jnp.concatenate([c_logf.astype(F32), logf_new], axis=1)
        all_lf = jnp.pad(all_lf, ((0, 0), (0, lp - past - length), (0, LANES - FOX_HEADS)))
        _, cum, cum_t = _logf_cum(all_lf.reshape(bsz * lp, LANES), 0, f_bias, bsz=bsz, seq_len=lp, c=c_cum,
                                  apply_gate=False)
        y_fox = _fox_decode(proj, c_k.reshape(bsz, past, FOX_WIDTH), c_v.reshape(bsz, past, FOX_WIDTH),
                            cum.reshape(bsz, lp, LANES), cum_t, bsz=bsz, lq=length, past=past, tk=t['tk_dec'])
    sc_hist = zeros(bsz, SC_WIDTH - 1, SC_DIM) if st_sconv is None else st_sconv
    y_sc, sc_tails = _sconv(proj, p['sconv_w'][0], _hist8(sc_hist), seq_len=length, tm=t['tm_sc'])
    sconv_new = _tails(sc_tails, bsz, length, t['tm_sc'], SC_WIDTH - 1)
    cd_out = wb['cd_out']
    xf = _proj_residual(xf, y_fox, y_sc, cd_out[:FOX_WIDTH], cd_out[FOX_WIDTH:], tm=t['tm_out'], tn=t['tn_out'])

    ffn_hist1 = zeros(bsz, FFN_CONV - 1, D_FF) if st_ffn is None else st_ffn[1]
    xf, tails = _conv_ffn(xf, p['ffn_norm_w'][1], wb['ffn_gate'][1], wb['ffn_up'][1], wb['ffn_down'][1],
                          p['ffn_conv_w'][1], p['ffn_conv_b'][1], _hist8(ffn_hist1), p['final_norm_w'],
                          seq_len=length, tm=t['tm_ffn'], tf=t['tf_ffn'], final=True)
    ffn_new.append(_tails(tails, bsz, length, t['tm_ffn'], FFN_CONV - 1))

    return (xf.reshape(bsz, length, d), ret_new[None], ssd_new[None], ssd_conv_new[None], k_new[None],
            v_new[None], logf_new[None], sconv_new[None], jnp.stack(ffn_new))


def _largest_divisor(n, cap, multiple=1):
    best = None
    for cand in range(multiple, min(n, cap) + 1, multiple):
        if n % cand == 0:
            best = cand
    assert best is not None, (n, cap, multiple)
    return best


def _tiles(bsz, length, past=None):
    m = bsz * length
    seq_tile = lambda cap: _largest_divisor(length, cap, SUBLANES) if length <= cap else _largest_divisor(length, cap, SUBLANES)
    row_tile = lambda cap: (_largest_divisor(length, cap, SUBLANES) if length >= cap
                            else _largest_divisor(m, cap, length))
    t = dict(
        tm_proj=row_tile(1024), tn_ab=640, tn_cd=896,
        tm_out=row_tile(1024), tn_out=512,
        tm_ffn=row_tile(512), tf_ffn=512,
        tm_sc=row_tile(512),
        c_ret=seq_tile(256), c_ssd=seq_tile(64),
        c_cum=256,
    )
    if past is None:
        t['t_fox'] = seq_tile(1024)
        t['c_cum'] = _largest_divisor(length, 256, LANES)
    else:
        t['tk_dec'] = _largest_divisor(past, 1024, LANES)
    return t


def kernel(x_prompt, x_sample, state_ret, state_ssd, state_ssd_conv, cache_fox_k, cache_fox_v, cache_fox_logf, state_sconv, state_ffn_conv, ab_norm_w, ab_w_in, ret_norm_w, ssd_conv_w, ssd_conv_b, ssd_dt_bias, ssd_A_log, ssd_D, ssd_norm_w, ab_w_out, cd_norm_w, cd_w_in, fox_f_bias, sconv_w, cd_w_out, ffn_norm_w, ffn_w_gate, ffn_w_up, ffn_conv_w, ffn_conv_b, ffn_w_down, final_norm_w):
    p = dict(ab_norm_w=ab_norm_w, ab_w_in=ab_w_in, ret_norm_w=ret_norm_w, ssd_conv_w=ssd_conv_w,
             ssd_conv_b=ssd_conv_b, ssd_dt_bias=ssd_dt_bias, ssd_A_log=ssd_A_log, ssd_D=ssd_D,
             ssd_norm_w=ssd_norm_w, ab_w_out=ab_w_out, cd_norm_w=cd_norm_w, cd_w_in=cd_w_in,
             fox_f_bias=fox_f_bias, sconv_w=sconv_w, cd_w_out=cd_w_out, ffn_norm_w=ffn_norm_w,
             ffn_w_gate=ffn_w_gate, ffn_w_up=ffn_w_up, ffn_conv_w=ffn_conv_w, ffn_conv_b=ffn_conv_b,
             ffn_w_down=ffn_w_down, final_norm_w=final_norm_w)
    assert x_prompt.shape[-1] == D_MODEL and ab_w_in.shape == (1, D_MODEL, AB_MAIN + SSD_HEADS)
    assert cd_w_in.shape == (1, D_MODEL, CD_MAIN + FOX_HEADS) and ffn_w_gate.shape == (2, D_MODEL, D_FF)
    wb = _prep_weights(p)
    bp, lp_, _ = x_prompt.shape
    bs, ls, _ = x_sample.shape
    past = cache_fox_k.shape[2]
    (y_prompt, p_ret, p_ssd, p_ssd_conv, p_fox_k, p_fox_v, p_fox_logf, p_sconv, p_ffn_conv) = _trunk(
        x_prompt, 0, None, None, None, None, None, None, None, None, p, wb, _tiles(bp, lp_))
    (y_sample, s_ret, s_ssd, s_ssd_conv, s_fox_k, s_fox_v, s_fox_logf, s_sconv, s_ffn_conv) = _trunk(
        x_sample, past, state_ret[0], state_ssd[0], state_ssd_conv[0], cache_fox_k[0], cache_fox_v[0],
        cache_fox_logf[0], state_sconv[0], state_ffn_conv, p, wb, _tiles(bs, ls, past))
    return (y_prompt, y_sample, p_ret, s_ret, p_ssd, s_ssd, p_ssd_conv, s_ssd_conv, p_fox_k, s_fox_k,
            p_fox_v, s_fox_v, p_fox_logf, s_fox_logf, p_sconv, s_sconv, p_ffn_conv, s_ffn_conv)
```

```python
import functools
import math

import numpy as np
import jax
import jax.numpy as jnp
from jax import lax
from jax.experimental import pallas as pl
from jax.experimental.pallas import tpu as pltpu

F32 = jnp.float32
BF16 = jnp.bfloat16
EPS = 1e-6
ROPE_BASE = 10000.0
NEG_INF = float("-inf")

D_MODEL = 2048
RET_HEADS, RET_DK, RET_DV = 4, 128, 256
SSD_DINNER, SSD_HEADDIM, SSD_HEADS, SSD_GROUPS, SSD_DSTATE, SSD_CONV = 1024, 64, 16, 2, 128, 4
SSD_CONV_DIM = SSD_DINNER + 2 * SSD_GROUPS * SSD_DSTATE
FOX_HEADS, FOX_HEAD_DIM = 8, 128
FOX_WIDTH = FOX_HEADS * FOX_HEAD_DIM
SC_DIM, SC_WIDTH = 1024, 3
D_FF, FFN_CONV = 5632, 3
AB_MAIN = 2 * RET_HEADS * RET_DK + 2 * RET_HEADS * RET_DV + SSD_DINNER + SSD_CONV_DIM
AB_PAD = AB_MAIN + 128
CD_MAIN = 3 * FOX_WIDTH + 3 * SC_DIM
CD_PAD = CD_MAIN + 128

LANES = 128
SUBLANES = 8
VMEM_LIMIT = 52 * 1024 * 1024


def _cparams(n_axes):
    return pltpu.CompilerParams(dimension_semantics=("arbitrary",) * n_axes,
                                vmem_limit_bytes=VMEM_LIMIT)


def _rms(xf, w):
    return xf * lax.rsqrt(jnp.mean(xf * xf, axis=-1, keepdims=True) + EPS) * w


def _softplus(x):
    return jnp.maximum(x, 0.0) + jnp.log1p(jnp.exp(-jnp.abs(x)))


def _split3(x):
    hi = x.astype(BF16)
    r1 = x - hi.astype(F32)
    mid = r1.astype(BF16)
    lo = (r1 - mid.astype(F32)).astype(BF16)
    return hi, mid, lo


def _dot(a, b):
    return jnp.dot(a, b, preferred_element_type=F32)


def _dot_nt(a, b):
    return lax.dot_general(a, b, (((1,), (1,)), ((), ())), preferred_element_type=F32)


def _dot_tn(a, b):
    return lax.dot_general(a, b, (((0,), (0,)), ((), ())), preferred_element_type=F32)


def _exact_lhs_dot(m_bf16, x):
    hi, mid, lo = _split3(x)
    return _dot(m_bf16, hi) + _dot(m_bf16, mid) + _dot(m_bf16, lo)


def _exact_rhs_dot(x, m_bf16):
    hi, mid, lo = _split3(x)
    return _dot(hi, m_bf16) + _dot(mid, m_bf16) + _dot(lo, m_bf16)


def _conv_rows(ext_ref, x, prev8, w_ref, width, rows):
    ext_ref[0:SUBLANES, :] = prev8
    ext_ref[SUBLANES:SUBLANES + rows, :] = x
    out = None
    for j in range(width):
        off = SUBLANES - (width - 1) + j
        term = ext_ref[off:off + rows, :] * w_ref[j:j + 1, :]
        out = term if out is None else out + term
    return out


def _seq_tiling(seq_len, tile_rows):
    if seq_len >= tile_rows:
        assert seq_len % tile_rows == 0
        return tile_rows, 1, seq_len // tile_rows
    assert tile_rows % seq_len == 0 and seq_len % SUBLANES == 0
    return seq_len, tile_rows // seq_len, 1


def _norm_matmul_kernel(x_ref, nw_ref, w_ref, ws_ref, o_ref, os_ref, h_ref):
    @pl.when(pl.program_id(1) == 0)
    def _():
        h_ref[...] = _rms(x_ref[...], nw_ref[...]).astype(BF16)
        os_ref[...] = _dot(h_ref[...], ws_ref[...])

    o_ref[...] = _dot(h_ref[...], w_ref[...])


def _norm_matmul(x, norm_w, w, w_small, *, tm, tn):
    m, d = x.shape
    n = w.shape[1]
    assert m % tm == 0 and n % tn == 0 and w_small.shape == (d, LANES)
    return pl.pallas_call(
        _norm_matmul_kernel,
        grid=(m // tm, n // tn),
        in_specs=[pl.BlockSpec((tm, d), lambda i, j: (i, 0)),
                  pl.BlockSpec((1, d), lambda i, j: (0, 0)),
                  pl.BlockSpec((d, tn), lambda i, j: (0, j)),
                  pl.BlockSpec((d, LANES), lambda i, j: (0, 0))],
        out_specs=[pl.BlockSpec((tm, tn), lambda i, j: (i, j)),
                   pl.BlockSpec((tm, LANES), lambda i, j: (i, 0))],
        out_shape=[jax.ShapeDtypeStruct((m, n), F32),
                   jax.ShapeDtypeStruct((m, LANES), F32)],
        scratch_shapes=[pltpu.VMEM((tm, d), BF16)],
        compiler_params=_cparams(2),
        name="norm_in_proj",
    )(x, norm_w.reshape(1, d), w, w_small)


def _proj_res_kernel(x_ref, a_ref, b_ref, wa_ref, wb_ref, o_ref):
    acc = _dot(a_ref[...], wa_ref[...])
    acc = acc + _dot(b_ref[...], wb_ref[...])
    o_ref[...] = x_ref[...] + acc


def _proj_residual(x, a, b, wa, wb, *, tm, tn):
    m, d = x.shape
    ka, kb = a.shape[1], b.shape[1]
    assert m % tm == 0 and d % tn == 0
    return pl.pallas_call(
        _proj_res_kernel,
        grid=(m // tm, d // tn),
        in_specs=[pl.BlockSpec((tm, tn), lambda i, j: (i, j)),
                  pl.BlockSpec((tm, ka), lambda i, j: (i, 0)),
                  pl.BlockSpec((tm, kb), lambda i, j: (i, 0)),
                  pl.BlockSpec((ka, tn), lambda i, j: (0, j)),
                  pl.BlockSpec((kb, tn), lambda i, j: (0, j))],
        out_specs=pl.BlockSpec((tm, tn), lambda i, j: (i, j)),
        out_shape=jax.ShapeDtypeStruct((m, d), F32),
        compiler_params=_cparams(2),
        name="out_proj_residual",
    )(x, a, b, wa, wb)


def _ffn_kernel(x_ref, nw_ref, wg_ref, wu_ref, wd_ref, cw_ref, cb_ref, hist_ref, fw_ref,
                o_ref, tail_ref, h_ref, carry_ref, ext_ref, *, tm, rows, spt, tpb, nf, final):
    i = pl.program_id(0)
    f = pl.program_id(1)

    @pl.when(f == 0)
    def _():
        xf = x_ref[...]
        h_ref[...] = _rms(xf, nw_ref[...]).astype(BF16)
        o_ref[...] = xf

    h = h_ref[...]
    a = _dot(h, wg_ref[...])
    u = _dot(h, wu_ref[...])
    first = (i % tpb) == 0
    convs = []
    for s in range(spt):
        a_s = a[s * rows:(s + 1) * rows]
        if tpb == 1:
            prev = hist_ref[s]
        else:
            prev = jnp.where(first, hist_ref[s], carry_ref[f])
        convs.append(_conv_rows(ext_ref, a_s, prev, cw_ref, FFN_CONV, rows))
        tail_ref[s] = a_s[rows - SUBLANES:rows]
    if tpb > 1:
        carry_ref[f] = a[tm - SUBLANES:tm]
    conv = convs[0] if spt == 1 else jnp.concatenate(convs, axis=0)
    act = (jax.nn.silu(conv + cb_ref[...]) * u).astype(BF16)
    o_ref[...] += _dot(act, wd_ref[...])
    if final:
        @pl.when(f == nf - 1)
        def _():
            o_ref[...] = _rms(o_ref[...], fw_ref[...])


def _conv_ffn(x, norm_w, wg, wu, wd, conv_w, conv_b, hist8, final_w, *, seq_len, tm, tf, final):
    m, d = x.shape
    ff = wg.shape[1]
    assert m % tm == 0 and ff % tf == 0
    rows, spt, tpb = _seq_tiling(seq_len, tm)
    nm, nf = m // tm, ff // tf
    hist_map = (lambda i, f: (i // tpb, 0, f)) if spt == 1 else (lambda i, f: (i, 0, f))
    kern = functools.partial(_ffn_kernel, tm=tm, rows=rows, spt=spt, tpb=tpb, nf=nf, final=final)
    out, tails = pl.pallas_call(
        kern,
        grid=(nm, nf),
        in_specs=[pl.BlockSpec((tm, d), lambda i, f: (i, 0)),
                  pl.BlockSpec((1, d), lambda i, f: (0, 0)),
                  pl.BlockSpec((d, tf), lambda i, f: (0, f)),
                  pl.BlockSpec((d, tf), lambda i, f: (0, f)),
                  pl.BlockSpec((tf, d), lambda i, f: (f, 0)),
                  pl.BlockSpec((FFN_CONV, tf), lambda i, f: (0, f)),
                  pl.BlockSpec((1, tf), lambda i, f: (0, f)),
                  pl.BlockSpec((spt, SUBLANES, tf), hist_map),
                  pl.BlockSpec((1, d), lambda i, f: (0, 0))],
        out_specs=[pl.BlockSpec((tm, d), lambda i, f: (i, 0)),
                   pl.BlockSpec((spt, SUBLANES, tf), lambda i, f: (i, 0, f))],
        out_shape=[jax.ShapeDtypeStruct((m, d), F32),
                   jax.ShapeDtypeStruct((nm * spt, SUBLANES, ff), F32)],
        scratch_shapes=[pltpu.VMEM((tm, d), BF16),
                        pltpu.VMEM((nf, SUBLANES, tf), F32),
                        pltpu.VMEM((rows + SUBLANES, tf), F32)],
        compiler_params=_cparams(2),
        name="conv_ffn",
    )(x, norm_w.reshape(1, d), wg, wu, wd, conv_w, conv_b.reshape(1, ff), hist8, final_w.reshape(1, d))
    return out, tails


def _retention_kernel(q_ref, k_ref, v_ref, g_ref, cos_ref, sin_ref, st_ref, nw_ref,
                      y_ref, so_ref, *, c):
    ci = pl.program_id(1)

    @pl.when(ci == 0)
    def _():
        so_ref[...] = st_ref[...]

    cos = cos_ref[...]
    sin = sin_ref[...]
    ii = lax.broadcasted_iota(jnp.int32, (c, c), 0)
    jj = lax.broadcasted_iota(jnp.int32, (c, c), 1)
    diff = (ii - jj).astype(F32)
    causal = ii >= jj
    ridx = lax.broadcasted_iota(jnp.int32, (c, 1), 0).astype(F32)
    for h in range(RET_HEADS):
        lg = math.log1p(-(2.0 ** (-5.0 - h)))
        q = q_ref[:, h * RET_DK:(h + 1) * RET_DK]
        k = k_ref[:, h * RET_DK:(h + 1) * RET_DK]
        v = v_ref[:, h * RET_DV:(h + 1) * RET_DV]
        qr = q * cos + pltpu.roll(q, RET_DK // 2, 1) * sin
        kr = (k * cos + pltpu.roll(k, RET_DK // 2, 1) * sin) * (RET_DK ** -0.5)
        qb = qr.astype(BF16)
        kb = kr.astype(BF16)
        vb = v.astype(BF16)
        decay = jnp.exp(jnp.where(causal, diff * lg, NEG_INF))
        inner = jnp.exp((ridx + 1.0) * lg)
        sdecay = jnp.exp((c - 1.0 - ridx) * lg)
        s = so_ref[0, h]
        scores = _dot_nt(qb, kb) * decay
        y = _dot(scores.astype(BF16), vb)
        y = y + _dot(qb, s.astype(BF16)) * inner
        kd = (kr * sdecay).astype(BF16)
        so_ref[0, h] = math.exp(c * lg) * s + _dot_tn(kd, vb)
        mu = jnp.mean(y, axis=-1, keepdims=True)
        yc = y - mu
        var = jnp.mean(yc * yc, axis=-1, keepdims=True)
        yn = yc * lax.rsqrt(var + EPS) * nw_ref[:, h * RET_DV:(h + 1) * RET_DV]
        g = g_ref[:, h * RET_DV:(h + 1) * RET_DV]
        y_ref[:, h * RET_DV:(h + 1) * RET_DV] = (jax.nn.silu(g) * yn).astype(BF16)


def _retention(proj, cosf, sinf, state, norm_w, *, bsz, seq_len, c):
    m = proj.shape[0]
    nc = seq_len // c
    assert seq_len % c == 0
    qk_w = RET_HEADS * RET_DK
    v_w = RET_HEADS * RET_DV
    row = lambda b, ci: b * nc + ci
    y, s_new = pl.pallas_call(
        functools.partial(_retention_kernel, c=c),
        grid=(bsz, nc),
        in_specs=[pl.BlockSpec((c, qk_w), lambda b, ci: (row(b, ci), 0)),
                  pl.BlockSpec((c, qk_w), lambda b, ci: (row(b, ci), 1)),
                  pl.BlockSpec((c, v_w), lambda b, ci: (row(b, ci), 1)),
                  pl.BlockSpec((c, v_w), lambda b, ci: (row(b, ci), 2)),
                  pl.BlockSpec((c, RET_DK), lambda b, ci: (ci, 0)),
                  pl.BlockSpec((c, RET_DK), lambda b, ci: (ci, 0)),
                  pl.BlockSpec((1, RET_HEADS, RET_DK, RET_DV), lambda b, ci: (b, 0, 0, 0)),
                  pl.BlockSpec((1, v_w), lambda b, ci: (0, 0))],
        out_specs=[pl.BlockSpec((c, v_w), lambda b, ci: (row(b, ci), 0)),
                   pl.BlockSpec((1, RET_HEADS, RET_DK, RET_DV), lambda b, ci: (b, 0, 0, 0))],
        out_shape=[jax.ShapeDtypeStruct((m, v_w), BF16),
                   jax.ShapeDtypeStruct(state.shape, F32)],
        compiler_params=_cparams(2),
        name="retention",
    )(proj, proj, proj, proj, cosf, sinf, state, norm_w.reshape(1, v_w))
    return y, s_new


def _ssd_kernel(z_ref, xs_ref, bc_ref, dt_ref, hx_ref, hbc_ref, st_ref,
                cwx_ref, cwbc_ref, cbx_ref, cbbc_ref, dtb_ref, alog_ref, dsk_ref, nw_ref,
                tri_ref, exp_ref,
                y_ref, so_ref,
                st_scr, cx_scr, cbc_scr, extx_scr, extbc_scr, yh_scr, *, c, nc):
    ci = pl.program_id(1)
    gw = SSD_DINNER // SSD_GROUPS
    hpg = SSD_HEADS // SSD_GROUPS

    @pl.when(ci == 0)
    def _():
        st_scr[...] = st_ref[0].T
        cx_scr[...] = hx_ref[0]
        cbc_scr[...] = hbc_ref[0]

    xs_raw = xs_ref[...]
    bc_raw = bc_ref[...]
    xs = jax.nn.silu(_conv_rows(extx_scr, xs_raw, cx_scr[...], cwx_ref, SSD_CONV, c) + cbx_ref[...])
    bcm = jax.nn.silu(_conv_rows(extbc_scr, bc_raw, cbc_scr[...], cwbc_ref, SSD_CONV, c) + cbbc_ref[...])
    cx_scr[...] = xs_raw[c - SUBLANES:c]
    cbc_scr[...] = bc_raw[c - SUBLANES:c]

    tri = tri_ref[...]
    expand = exp_ref[...]
    dt = _softplus(dt_ref[...] + dtb_ref[...])
    a = -jnp.exp(alog_ref[...])
    acs = _exact_lhs_dot(tri, dt * a)
    acs_t = acs.T
    acs_last = acs[c - 1:c, :]
    exp_acs = jnp.exp(acs)
    to_end = jnp.exp(acs_last - acs)
    dt_e = _exact_rhs_dot(dt, expand)
    to_end_e = _exact_rhs_dot(to_end, expand)
    exp_acs_e = _exact_rhs_dot(exp_acs, expand)
    chunk_dec_e = _exact_rhs_dot(jnp.exp(acs_last), expand)

    xdt = xs * dt_e
    xdt_b = xdt.astype(BF16)
    xend_b = (xdt * to_end_e).astype(BF16)
    ii = lax.broadcasted_iota(jnp.int32, (c, c), 0)
    jj = lax.broadcasted_iota(jnp.int32, (c, c), 1)
    causal = ii >= jj
    nb = SSD_GROUPS * SSD_DSTATE
    for g in range(SSD_GROUPS):
        b_g = bcm[:, g * SSD_DSTATE:(g + 1) * SSD_DSTATE].astype(BF16)
        c_g = bcm[:, nb + g * SSD_DSTATE:nb + (g + 1) * SSD_DSTATE].astype(BF16)
        cb = _dot_nt(c_g, b_g)
        s_g = st_scr[:, g * gw:(g + 1) * gw]
        y_state = _dot(c_g, s_g.astype(BF16)) * exp_acs_e[:, g * gw:(g + 1) * gw]
        for r in range(hpg):
            hh = g * hpg + r
            seg = acs[:, hh:hh + 1] - acs_t[hh:hh + 1, :]
            lmat = jnp.exp(jnp.where(causal, seg, NEG_INF))
            mm = (cb * lmat).astype(BF16)
            lo = hh * SSD_HEADDIM
            yh_scr[:, lo:lo + SSD_HEADDIM] = (
                _dot(mm, xdt_b[:, lo:lo + SSD_HEADDIM]) + y_state[:, r * SSD_HEADDIM:(r + 1) * SSD_HEADDIM])
        upd = _dot_tn(b_g, xend_b[:, g * gw:(g + 1) * gw])
        st_scr[:, g * gw:(g + 1) * gw] = chunk_dec_e[:, g * gw:(g + 1) * gw] * s_g + upd

    y = yh_scr[...] + dsk_ref[...] * xs
    z = z_ref[...]
    y_ref[...] = _rms(y * jax.nn.silu(z), nw_ref[...]).astype(BF16)

    @pl.when(ci == nc - 1)
    def _():
        so_ref[0] = st_scr[...].T


def _ssd(proj, dt_proj, hist8, state, conv_w, conv_b, dt_bias, a_log, d_skip, norm_w, *, bsz, seq_len, c):
    m = proj.shape[0]
    nc = seq_len // c
    assert seq_len % c == 0
    row = lambda b, ci: b * nc + ci
    const2 = lambda b, ci: (0, 0)
    di, bcw = SSD_DINNER, 2 * SSD_GROUPS * SSD_DSTATE
    tri = jnp.asarray(np.tril(np.ones((c, c), np.float32)), BF16)
    expand = np.zeros((LANES, di), np.float32)
    for h in range(SSD_HEADS):
        expand[h, h * SSD_HEADDIM:(h + 1) * SSD_HEADDIM] = 1.0
    expand = jnp.asarray(expand, BF16)
    pad_row = lambda v: jnp.pad(v.astype(F32), (0, LANES - v.shape[0])).reshape(1, LANES)
    st2 = state.reshape(bsz, di, SSD_DSTATE)
    y, s_new = pl.pallas_call(
        functools.partial(_ssd_kernel, c=c, nc=nc),
        grid=(bsz, nc),
        in_specs=[pl.BlockSpec((c, di), lambda b, ci: (row(b, ci), 3)),
                  pl.BlockSpec((c, di), lambda b, ci: (row(b, ci), 4)),
                  pl.BlockSpec((c, bcw), lambda b, ci: (row(b, ci), 10)),
                  pl.BlockSpec((c, LANES), lambda b, ci: (row(b, ci), 0)),
                  pl.BlockSpec((1, SUBLANES, di), lambda b, ci: (b, 0, 0)),
                  pl.BlockSpec((1, SUBLANES, bcw), lambda b, ci: (b, 0, 2)),
                  pl.BlockSpec((1, di, SSD_DSTATE), lambda b, ci: (b, 0, 0)),
                  pl.BlockSpec((SSD_CONV, di), const2),
                  pl.BlockSpec((SSD_CONV, bcw), lambda b, ci: (0, 2)),
                  pl.BlockSpec((1, di), const2),
                  pl.BlockSpec((1, bcw), lambda b, ci: (0, 2)),
                  pl.BlockSpec((1, LANES), const2),
                  pl.BlockSpec((1, LANES), const2),
                  pl.BlockSpec((1, di), const2),
                  pl.BlockSpec((1, di), const2),
                  pl.BlockSpec((c, c), const2),
                  pl.BlockSpec((LANES, di), const2)],
        out_specs=[pl.BlockSpec((c, di), lambda b, ci: (row(b, ci), 0)),
                   pl.BlockSpec((1, di, SSD_DSTATE), lambda b, ci: (b, 0, 0))],
        out_shape=[jax.ShapeDtypeStruct((m, di), BF16),
                   jax.ShapeDtypeStruct(st2.shape, F32)],
        scratch_shapes=[pltpu.VMEM((SSD_DSTATE, di), F32),
                        pltpu.VMEM((SUBLANES, di), F32),
                        pltpu.VMEM((SUBLANES, bcw), F32),
                        pltpu.VMEM((c + SUBLANES, di), F32),
                        pltpu.VMEM((c + SUBLANES, bcw), F32),
                        pltpu.VMEM((c, di), F32)],
        compiler_params=_cparams(2),
        name="ssd",
    )(proj, proj, proj, dt_proj, hist8, hist8, st2,
      conv_w, conv_w, conv_b.reshape(1, -1), conv_b.reshape(1, -1),
      pad_row(dt_bias), pad_row(a_log), jnp.repeat(d_skip.astype(F32), SSD_HEADDIM).reshape(1, di),
      norm_w.reshape(1, di), tri, expand)
    return y, s_new.reshape(state.shape)


def _logf_cum_kernel(x_ref, b_ref, tri_ref, lf_ref, cum_ref, cumt_ref, carry_ref, *, c, apply_gate):
    @pl.when(pl.program_id(1) == 0)
    def _():
        carry_ref[...] = jnp.zeros_like(carry_ref)

    x = x_ref[...]
    if apply_gate:
        lf = -_softplus(-(x + b_ref[...]))
    else:
        lf = x
    lf_ref[...] = lf
    cum = _exact_lhs_dot(tri_ref[...], lf) + carry_ref[...]
    carry_ref[...] = cum[c - 1:c, :]
    cum_ref[...] = cum
    cumt_ref[0] = cum.T[0:SUBLANES, :]


def _logf_cum(src, col_block, bias, *, bsz, seq_len, c, apply_gate):
    m = bsz * seq_len
    nc = seq_len // c
    assert seq_len % c == 0 and c % LANES == 0
    tri = jnp.asarray(np.tril(np.ones((c, c), np.float32)), BF16)
    row = lambda b, ci: b * nc + ci
    return pl.pallas_call(
        functools.partial(_logf_cum_kernel, c=c, apply_gate=apply_gate),
        grid=(bsz, nc),
        in_specs=[pl.BlockSpec((c, LANES), lambda b, ci: (row(b, ci), col_block)),
                  pl.BlockSpec((1, LANES), lambda b, ci: (0, 0)),
                  pl.BlockSpec((c, c), lambda b, ci: (0, 0))],
        out_specs=[pl.BlockSpec((c, LANES), lambda b, ci: (row(b, ci), 0)),
                   pl.BlockSpec((c, LANES), lambda b, ci: (row(b, ci), 0)),
                   pl.BlockSpec((1, SUBLANES, c), lambda b, ci: (b, 0, ci))],
        out_shape=[jax.ShapeDtypeStruct((m, LANES), F32),
                   jax.ShapeDtypeStruct((m, LANES), F32),
                   jax.ShapeDtypeStruct((bsz, SUBLANES, seq_len), F32)],
        scratch_shapes=[pltpu.VMEM((1, LANES), F32)],
        compiler_params=_cparams(2),
        name="logf_cumsum",
    )(src, bias, tri)


FOX_AUG = 2 * FOX_HEAD_DIM
N_BIAS_PIECES = 3


def _fox_prep_kernel(q_ref, k_ref, v_ref, fl_ref, fb_ref, tri_ref, place_ref, ones_ref,
                     qa_ref, ka_ref, vb_ref, k32_ref, v32_ref, lf_ref, carry_ref, *, tp):
    @pl.when(pl.program_id(1) == 0)
    def _():
        carry_ref[...] = jnp.zeros_like(carry_ref)

    lf = -_softplus(-(fl_ref[...] + fb_ref[...]))
    lf_ref[...] = lf
    cum = _exact_lhs_dot(tri_ref[...], lf) + carry_ref[...]
    carry_ref[...] = cum[tp - 1:tp, :]
    pieces = _split3(cum * (FOX_HEAD_DIM ** 0.5))
    n = N_BIAS_PIECES
    aug_q = ones_ref[0:1, :] + sum(_dot(pieces[r], place_ref[r]) for r in range(n))
    aug_k = ones_ref[1:2, :] - sum(_dot(pieces[r], place_ref[n + r]) for r in range(n))
    for h in range(FOX_HEADS):
        src = slice(h * FOX_HEAD_DIM, (h + 1) * FOX_HEAD_DIM)
        feat = slice(h * FOX_AUG, h * FOX_AUG + FOX_HEAD_DIM)
        bias = slice(h * FOX_AUG + FOX_HEAD_DIM, (h + 1) * FOX_AUG)
        qa_ref[:, feat] = q_ref[:, src].astype(BF16)
        qa_ref[:, bias] = aug_q[:, src].astype(BF16)
        ka_ref[:, feat] = k_ref[:, src].astype(BF16)
        ka_ref[:, bias] = aug_k[:, src].astype(BF16)
    k = k_ref[...]
    v = v_ref[...]
    k32_ref[...] = k
    v32_ref[...] = v
    vb_ref[...] = v.astype(BF16)


def _fox_prep(proj, fl_proj, f_bias, *, bsz, seq_len, tp):
    m = proj.shape[0]
    nt = seq_len // tp
    assert seq_len % tp == 0
    w = FOX_WIDTH
    tri = jnp.asarray(np.tril(np.ones((tp, tp), np.float32)), BF16)
    n = N_BIAS_PIECES
    place = np.zeros((2 * n, LANES, w), np.float32)
    ones = np.zeros((SUBLANES, w), np.float32)
    for h in range(FOX_HEADS):
        for r in range(2 * n):
            place[r, h, h * FOX_HEAD_DIM + r] = 1.0
        ones[0, h * FOX_HEAD_DIM + n:h * FOX_HEAD_DIM + 2 * n] = 1.0
        ones[1, h * FOX_HEAD_DIM:h * FOX_HEAD_DIM + n] = 1.0
    row = lambda b, ti: (b * nt + ti, 0)
    const2 = lambda b, ti: (0, 0)
    return pl.pallas_call(
        functools.partial(_fox_prep_kernel, tp=tp),
        grid=(bsz, nt),
        in_specs=[pl.BlockSpec((tp, w), lambda b, ti: (b * nt + ti, 0)),
                  pl.BlockSpec((tp, w), lambda b, ti: (b * nt + ti, 1)),
                  pl.BlockSpec((tp, w), lambda b, ti: (b * nt + ti, 2)),
                  pl.BlockSpec((tp, LANES), row),
                  pl.BlockSpec((1, LANES), const2),
                  pl.BlockSpec((tp, tp), const2),
                  pl.BlockSpec((2 * n, LANES, w), lambda b, ti: (0, 0, 0)),
                  pl.BlockSpec((SUBLANES, w), const2)],
        out_specs=[pl.BlockSpec((tp, FOX_HEADS * FOX_AUG), row),
                   pl.BlockSpec((tp, FOX_HEADS * FOX_AUG), row),
                   pl.BlockSpec((tp, w), row),
                   pl.BlockSpec((tp, w), row),
                   pl.BlockSpec((tp, w), row),
                   pl.BlockSpec((tp, LANES), row)],
        out_shape=[jax.ShapeDtypeStruct((m, FOX_HEADS * FOX_AUG), BF16),
                   jax.ShapeDtypeStruct((m, FOX_HEADS * FOX_AUG), BF16),
                   jax.ShapeDtypeStruct((m, w), BF16),
                   jax.ShapeDtypeStruct((m, w), F32),
                   jax.ShapeDtypeStruct((m, w), F32),
                   jax.ShapeDtypeStruct((m, LANES), F32)],
        scratch_shapes=[pltpu.VMEM((1, LANES), F32)],
        compiler_params=_cparams(2),
        name="fox_prep",
    )(proj, proj, proj, fl_proj, f_bias, tri, jnp.asarray(place, BF16), jnp.asarray(ones, F32))


def _fox_kernel(qi_ref, ki_ref, q_ref, k_ref, v_ref, o_ref, m_ref, l_ref, acc_ref, *, t, ts):
    step = pl.program_id(2)
    qi = qi_ref[step]
    ki = ki_ref[step]
    to_log2 = (FOX_HEAD_DIM ** -0.5) * math.log2(math.e)

    @pl.when(ki == 0)
    def _():
        m_ref[...] = jnp.full_like(m_ref, NEG_INF)
        l_ref[...] = jnp.zeros_like(l_ref)
        acc_ref[...] = jnp.zeros_like(acc_ref)

    def update(diagonal):
        for r in range(t // ts):
            rows = slice(r * ts, (r + 1) * ts)
            nk = (r + 1) * ts if diagonal else t
            s = _dot_nt(q_ref[rows, :], k_ref[0:nk, :])
            if diagonal:
                ri = lax.broadcasted_iota(jnp.int32, (ts, nk), 0) + r * ts
                ci = lax.broadcasted_iota(jnp.int32, (ts, nk), 1)
                s = jnp.where(ci <= ri, s, NEG_INF)
            m_old = m_ref[rows, :]
            m_new = jnp.maximum(m_old, jnp.max(s, axis=1, keepdims=True))
            alpha = jnp.exp2((m_old - m_new) * to_log2)
            p = jnp.exp2((s - pltpu.repeat(m_new, nk // LANES, axis=1)) * to_log2)
            l_ref[rows, :] = alpha * l_ref[rows, :] + jnp.sum(p, axis=1, keepdims=True)
            acc_ref[rows, :] = alpha * acc_ref[rows, :] + _dot(p.astype(BF16), v_ref[0:nk, :])
            m_ref[rows, :] = m_new

    @pl.when(ki < qi)
    def _():
        update(False)

    @pl.when(ki == qi)
    def _():
        update(True)
        o_ref[...] = (acc_ref[...] / l_ref[...]).astype(BF16)


def _fox_prompt(qa, ka, vb, *, bsz, seq_len, t, ts):
    m = qa.shape[0]
    nq = seq_len // t
    assert seq_len % t == 0 and t % ts == 0
    pairs = [(qi, ki) for qi in range(nq) for ki in range(qi + 1)]
    qi_tab = jnp.asarray([p[0] for p in pairs], jnp.int32)
    ki_tab = jnp.asarray([p[1] for p in pairs], jnp.int32)
    grid_spec = pltpu.PrefetchScalarGridSpec(
        num_scalar_prefetch=2,
        grid=(bsz, FOX_HEADS, len(pairs)),
        in_specs=[pl.BlockSpec((t, FOX_AUG), lambda b, h, s, qi, ki: (b * nq + qi[s], h)),
                  pl.BlockSpec((t, FOX_AUG), lambda b, h, s, qi, ki: (b * nq + ki[s], h)),
                  pl.BlockSpec((t, FOX_HEAD_DIM), lambda b, h, s, qi, ki: (b * nq + ki[s], h))],
        out_specs=pl.BlockSpec((t, FOX_HEAD_DIM), lambda b, h, s, qi, ki: (b * nq + qi[s], h)),
        scratch_shapes=[pltpu.VMEM((t, LANES), F32), pltpu.VMEM((t, LANES), F32),
                        pltpu.VMEM((t, FOX_HEAD_DIM), F32)],
    )
    return pl.pallas_call(
        functools.partial(_fox_kernel, t=t, ts=ts),
        grid_spec=grid_spec,
        out_shape=jax.ShapeDtypeStruct((m, FOX_WIDTH), BF16),
        compiler_params=_cparams(3),
        name="fox_attention",
    )(qi_tab, ki_tab, qa, ka, vb)


def _fox_decode_kernel(q_ref, kn_ref, vn_ref, kc_ref, vc_ref, cum_ref, ctc_ref, ctn_ref, o_ref,
                       m_ref, l_ref, acc_ref, *, lq, ncb):
    j = pl.program_id(1)

    @pl.when(j == 0)
    def _():
        m_ref[...] = jnp.full_like(m_ref, NEG_INF)
        l_ref[...] = jnp.zeros_like(l_ref)
        acc_ref[...] = jnp.zeros_like(acc_ref)

    def attend(k_head, v_head, ck_all, causal):
        tk = ck_all.shape[1]
        for h in range(FOX_HEADS):
            lo = h * FOX_HEAD_DIM
            qh = q_ref[:, lo:lo + FOX_HEAD_DIM].astype(BF16)
            kh = k_head(lo).astype(BF16)
            vh = v_head(lo).astype(BF16)
            s = _dot_nt(qh, kh) * (FOX_HEAD_DIM ** -0.5)
            s = s + (cum_ref[0, :, h:h + 1] - ck_all[h:h + 1, :])
            if causal:
                rows = lax.broadcasted_iota(jnp.int32, (lq, tk), 0)
                cols = lax.broadcasted_iota(jnp.int32, (lq, tk), 1)
                s = jnp.where(cols <= rows, s, NEG_INF)
            m_old = m_ref[h]
            m_new = jnp.maximum(m_old, jnp.max(s, axis=1, keepdims=True))
            alpha = jnp.exp(m_old - m_new)
            p = jnp.exp(s - m_new)
            l_ref[h] = alpha * l_ref[h] + jnp.sum(p, axis=1, keepdims=True)
            acc_ref[:, lo:lo + FOX_HEAD_DIM] = alpha * acc_ref[:, lo:lo + FOX_HEAD_DIM] + _dot(p.astype(BF16), vh)
            m_ref[h] = m_new

    @pl.when(j < ncb)
    def _():
        attend(lambda lo: kc_ref[0, :, lo:lo + FOX_HEAD_DIM], lambda lo: vc_ref[0, :, lo:lo + FOX_HEAD_DIM],
               ctc_ref[0], False)

    @pl.when(j == ncb)
    def _():
        attend(lambda lo: kn_ref[:, lo:lo + FOX_HEAD_DIM], lambda lo: vn_ref[:, lo:lo + FOX_HEAD_DIM],
               ctn_ref[0][:, 0:lq], True)
        for h in range(FOX_HEADS):
            lo = h * FOX_HEAD_DIM
            o_ref[:, lo:lo + FOX_HEAD_DIM] = (acc_ref[:, lo:lo + FOX_HEAD_DIM] / l_ref[h]).astype(BF16)


def _fox_decode(proj, cache_k, cache_v, cum, cum_t, *, bsz, lq, past, tk):
    assert past % tk == 0 and past % lq == 0 and past % LANES == 0 and lq <= LANES
    ncb = past // tk
    w = FOX_WIDTH
    cache_blk = lambda b, j: (b, jnp.minimum(j, ncb - 1), 0)
    return pl.pallas_call(
        functools.partial(_fox_decode_kernel, lq=lq, ncb=ncb),
        grid=(bsz, ncb + 1),
        in_specs=[pl.BlockSpec((lq, w), lambda b, j: (b, 0)),
                  pl.BlockSpec((lq, w), lambda b, j: (b, 1)),
                  pl.BlockSpec((lq, w), lambda b, j: (b, 2)),
                  pl.BlockSpec((1, tk, w), cache_blk),
                  pl.BlockSpec((1, tk, w), cache_blk),
                  pl.BlockSpec((1, lq, LANES), lambda b, j: (b, past // lq, 0)),
                  pl.BlockSpec((1, SUBLANES, tk), lambda b, j: (b, 0, jnp.minimum(j, ncb - 1))),
                  pl.BlockSpec((1, SUBLANES, LANES), lambda b, j: (b, 0, past // LANES))],
        out_specs=pl.BlockSpec((lq, w), lambda b, j: (b, 0)),
        out_shape=jax.ShapeDtypeStruct((bsz * lq, w), BF16),
        scratch_shapes=[pltpu.VMEM((FOX_HEADS, lq, 1), F32), pltpu.VMEM((FOX_HEADS, lq, 1), F32),
                        pltpu.VMEM((lq, w), F32)],
        compiler_params=_cparams(2),
        name="fox_decode",
    )(proj, proj, proj, cache_k, cache_v, cum, cum_t, cum_t)


def _sconv_kernel(u_ref, bg_ref, cg_ref, cw_ref, hist_ref, y_ref, tail_ref, carry_ref, ext_ref,
                  *, tm, rows, spt, tpb):
    i = pl.program_id(0)
    w = cg_ref[...] * u_ref[...]
    first = (i % tpb) == 0
    for s in range(spt):
        w_s = w[s * rows:(s + 1) * rows]
        if tpb == 1:
            prev = hist_ref[s]
        else:
            prev = jnp.where(first, hist_ref[s], carry_ref[...])
        conv = _conv_rows(ext_ref, w_s, prev, cw_ref, SC_WIDTH, rows)
        y_ref[s * rows:(s + 1) * rows, :] = (bg_ref[s * rows:(s + 1) * rows, :] * conv).astype(BF16)
        tail_ref[s] = w_s[rows - SUBLANES:rows]
    if tpb > 1:
        carry_ref[...] = w[tm - SUBLANES:tm]


def _sconv(proj, conv_w, hist8, *, seq_len, tm):
    m = proj.shape[0]
    assert m % tm == 0
    rows, spt, tpb = _seq_tiling(seq_len, tm)
    nm = m // tm
    hist_map = (lambda i: (i // tpb, 0, 0)) if spt == 1 else (lambda i: (i, 0, 0))
    base = 3 * FOX_WIDTH // SC_DIM
    return pl.pallas_call(
        functools.partial(_sconv_kernel, tm=tm, rows=rows, spt=spt, tpb=tpb),
        grid=(nm,),
        in_specs=[pl.BlockSpec((tm, SC_DIM), lambda i: (i, base)),
                  pl.BlockSpec((tm, SC_DIM), lambda i: (i, base + 1)),
                  pl.BlockSpec((tm, SC_DIM), lambda i: (i, base + 2)),
                  pl.BlockSpec((SC_WIDTH, SC_DIM), lambda i: (0, 0)),
                  pl.BlockSpec((spt, SUBLANES, SC_DIM), hist_map)],
        out_specs=[pl.BlockSpec((tm, SC_DIM), lambda i: (i, 0)),
                   pl.BlockSpec((spt, SUBLANES, SC_DIM), lambda i: (i, 0, 0))],
        out_shape=[jax.ShapeDtypeStruct((m, SC_DIM), BF16),
                   jax.ShapeDtypeStruct((nm * spt, SUBLANES, SC_DIM), F32)],
        scratch_shapes=[pltpu.VMEM((SUBLANES, SC_DIM), F32),
                        pltpu.VMEM((rows + SUBLANES, SC_DIM), F32)],
        compiler_params=_cparams(1),
        name="gated_short_conv",
    )(proj, proj, proj, conv_w, hist8)


def _hist8(state):
    n, w1, c = state.shape
    return jnp.concatenate([jnp.zeros((n, SUBLANES - w1, c), F32), state.astype(F32)], axis=1)


def _tails(tails, n_seq, seq_len, tile_rows, keep):
    per_seq = max(1, seq_len // tile_rows)
    idx = (jnp.arange(n_seq) + 1) * per_seq - 1
    return tails[idx][:, SUBLANES - keep:, :]


def _rope_tables(pos0, length):
    half = RET_DK // 2
    inv = ROPE_BASE ** (-jnp.arange(half, dtype=F32) / half)
    ang = (pos0 + jnp.arange(length)).astype(F32)[:, None] * inv[None, :]
    cos, sin = jnp.cos(ang), jnp.sin(ang)
    return jnp.concatenate([cos, cos], axis=1), jnp.concatenate([-sin, sin], axis=1)


def _prep_weights(p):
    d = D_MODEL
    ab_in = p['ab_w_in'][0]
    cd_in = p['cd_w_in'][0]
    f0 = 3 * FOX_WIDTH
    pad_cols = lambda w: jnp.pad(w, ((0, 0), (0, LANES - w.shape[1]))).astype(BF16)
    return dict(
        ab_in=ab_in[:, :AB_MAIN].astype(BF16),
        ab_small=pad_cols(ab_in[:, AB_MAIN:]),
        cd_in=jnp.concatenate([cd_in[:, :f0], cd_in[:, f0 + FOX_HEADS:]], axis=1).astype(BF16),
        cd_small=pad_cols(cd_in[:, f0:f0 + FOX_HEADS]),
        ab_out=p['ab_w_out'][0].astype(BF16),
        cd_out=p['cd_w_out'][0].astype(BF16),
        ffn_gate=p['ffn_w_gate'].astype(BF16),
        ffn_up=p['ffn_w_up'].astype(BF16),
        ffn_down=p['ffn_w_down'].astype(BF16),
    )


def _trunk(x, pos0, st_ret, st_ssd, st_ssd_conv, c_k, c_v, c_logf, st_sconv, st_ffn, p, wb, t):
    bsz, length, d = x.shape
    m = bsz * length
    xf = x.reshape(m, d)
    zeros = lambda *shape: jnp.zeros(shape, F32)

    proj, dt_proj = _norm_matmul(xf, p['ab_norm_w'][0], wb['ab_in'], wb['ab_small'],
                                 tm=t['tm_proj'], tn=t['tn_ab'])
    cosf, sinf = _rope_tables(pos0, length)
    ret_state = zeros(bsz, RET_HEADS, RET_DK, RET_DV) if st_ret is None else st_ret
    y_ret, ret_new = _retention(proj, cosf, sinf, ret_state, p['ret_norm_w'][0],
                                bsz=bsz, seq_len=length, c=t['c_ret'])
    ssd_state = zeros(bsz, SSD_HEADS, SSD_HEADDIM, SSD_DSTATE) if st_ssd is None else st_ssd
    ssd_hist = zeros(bsz, SSD_CONV - 1, SSD_CONV_DIM) if st_ssd_conv is None else st_ssd_conv
    y_ssd, ssd_new = _ssd(proj, dt_proj, _hist8(ssd_hist), ssd_state, p['ssd_conv_w'][0], p['ssd_conv_b'][0],
                          p['ssd_dt_bias'][0], p['ssd_A_log'][0], p['ssd_D'][0], p['ssd_norm_w'][0],
                          bsz=bsz, seq_len=length, c=t['c_ssd'])
    xbc_lo = AB_MAIN - SSD_CONV_DIM
    ssd_conv_new = proj.reshape(bsz, length, -1)[:, length - (SSD_CONV - 1):, xbc_lo:AB_MAIN]
    ab_out = wb['ab_out']
    xf = _proj_residual(xf, y_ret, y_ssd, ab_out[:RET_HEADS * RET_DV], ab_out[RET_HEADS * RET_DV:],
                        tm=t['tm_out'], tn=t['tn_out'])

    ffn_new = []
    ffn_hist0 = zeros(bsz, FFN_CONV - 1, D_FF) if st_ffn is None else st_ffn[0]
    xf, tails = _conv_ffn(xf, p['ffn_norm_w'][0], wb['ffn_gate'][0], wb['ffn_up'][0], wb['ffn_down'][0],
                          p['ffn_conv_w'][0], p['ffn_conv_b'][0], _hist8(ffn_hist0), p['final_norm_w'],
                          seq_len=length, tm=t['tm_ffn'], tf=t['tf_ffn'], final=False)
    ffn_new.append(_tails(tails, bsz, length, t['tm_ffn'], FFN_CONV - 1))

    proj, fl_proj = _norm_matmul(xf, p['cd_norm_w'][0], wb['cd_in'], wb['cd_small'],
                                 tm=t['tm_proj'], tn=t['tn_cd'])
    f_bias = jnp.pad(p['fox_f_bias'][0].astype(F32), (0, LANES - FOX_HEADS)).reshape(1, LANES)
    head_shape = (bsz, length, FOX_HEADS, FOX_HEAD_DIM)
    if c_k is None:
        qa, ka, vb, k32, v32, logf = _fox_prep(proj, fl_proj, f_bias, bsz=bsz, seq_len=length, tp=t['t_prep'])
        y_fox = _fox_prompt(qa, ka, vb, bsz=bsz, seq_len=length, t=t['t_fox'], ts=t['ts_fox'])
        logf_new = logf.reshape(bsz, length, LANES)[:, :, :FOX_HEADS]
        k_new, v_new = k32.reshape(head_shape), v32.reshape(head_shape)
    else:
        proj3 = proj.reshape(bsz, length, -1)
        k_new = proj3[:, :, FOX_WIDTH:2 * FOX_WIDTH].reshape(head_shape)
        v_new = proj3[:, :, 2 * FOX_WIDTH:3 * FOX_WIDTH].reshape(head_shape)
        past = c_k.shape[1]
        assert length % LANES == 0 or LANES % length == 0
        c_small = max(length, LANES)
        pad_rows = c_small - length
        proj_fl = fl_proj.reshape(bsz, length, LANES)
        if pad_rows:
            proj_fl = jnp.pad(proj_fl, ((0, 0), (0, pad_rows), (0, 0)))
        logf, _, _ = _logf_cum(proj_fl.reshape(bsz * c_small, LANES), 0, f_bias, bsz=bsz, seq_len=c_small,
                               c=c_small, apply_gate=True)
        logf_new = logf.reshape(bsz, c_small, LANES)[:, :length, :FOX_HEADS]
        c_cum = t['c_cum']
        lp = -(-(past + length) // c_cum) * c_cum
        all_lf = jnp.concatenate([c_logf.astype(F32), logf_new], axis=1)
        all_lf = jnp.pad(all_lf, ((0, 0), (0, lp - past - length), (0, LANES - FOX_HEADS)))
        _, cum, cum_t = _logf_cum(all_lf.reshape(bsz * lp, LANES), 0, f_bias, bsz=bsz, seq_len=lp, c=c_cum,
                                  apply_gate=False)
        y_fox = _fox_decode(proj, c_k.reshape(bsz, past, FOX_WIDTH), c_v.reshape(bsz, past, FOX_WIDTH),
                            cum.reshape(bsz, lp, LANES), cum_t, bsz=bsz, lq=length, past=past, tk=t['tk_dec'])
    sc_hist = zeros(bsz, SC_WIDTH - 1, SC_DIM) if st_sconv is None else st_sconv
    y_sc, sc_tails = _sconv(proj, p['sconv_w'][0], _hist8(sc_hist), seq_len=length, tm=t['tm_sc'])
    sconv_new = _tails(sc_tails, bsz, length, t['tm_sc'], SC_WIDTH - 1)
    cd_out = wb['cd_out']
    xf = _proj_residual(xf, y_fox, y_sc, cd_out[:FOX_WIDTH], cd_out[FOX_WIDTH:], tm=t['tm_out'], tn=t['tn_out'])

    ffn_hist1 = zeros(bsz, FFN_CONV - 1, D_FF) if st_ffn is None else st_ffn[1]
    xf, tails = _conv_ffn(xf, p['ffn_norm_w'][1], wb['ffn_gate'][1], wb['ffn_up'][1], wb['ffn_down'][1],
                          p['ffn_conv_w'][1], p['ffn_conv_b'][1], _hist8(ffn_hist1), p['final_norm_w'],
                          seq_len=length, tm=t['tm_ffn'], tf=t['tf_ffn'], final=True)
    ffn_new.append(_tails(tails, bsz, length, t['tm_ffn'], FFN_CONV - 1))

    return (xf.reshape(bsz, length, d), ret_new[None], ssd_new[None], ssd_conv_new[None], k_new[None],
            v_new[None], logf_new[None], sconv_new[None], jnp.stack(ffn_new))


def _largest_divisor(n, cap, multiple=1):
    best = None
    for cand in range(multiple, min(n, cap) + 1, multiple):
        if n % cand == 0:
            best = cand
    assert best is not None, (n, cap, multiple)
    return best


def _tiles(bsz, length, past=None):
    m = bsz * length
    seq_tile = lambda cap: _largest_divisor(length, cap, SUBLANES) if length <= cap else _largest_divisor(length, cap, SUBLANES)
    row_tile = lambda cap: (_largest_divisor(length, cap, SUBLANES) if length >= cap
                            else _largest_divisor(m, cap, length))
    t = dict(
        tm_proj=row_tile(1024), tn_ab=512, tn_cd=1024,
        tm_out=row_tile(1024), tn_out=1024,
        tm_ffn=row_tile(512), tf_ffn=512,
        tm_sc=row_tile(512),
        c_ret=seq_tile(256), c_ssd=seq_tile(64),
        c_cum=256,
    )
    if past is None:
        t['t_fox'] = seq_tile(1024)
        t['ts_fox'] = _largest_divisor(t['t_fox'], 256, LANES)
        t['t_prep'] = seq_tile(512)
    else:
        t['tk_dec'] = _largest_divisor(past, 1024, LANES)
    return t


def kernel(x_prompt, x_sample, state_ret, state_ssd, state_ssd_conv, cache_fox_k, cache_fox_v, cache_fox_logf, state_sconv, state_ffn_conv, ab_norm_w, ab_w_in, ret_norm_w, ssd_conv_w, ssd_conv_b, ssd_dt_bias, ssd_A_log, ssd_D, ssd_norm_w, ab_w_out, cd_norm_w, cd_w_in, fox_f_bias, sconv_w, cd_w_out, ffn_norm_w, ffn_w_gate, ffn_w_up, ffn_conv_w, ffn_conv_b, ffn_w_down, final_norm_w):
    p = dict(ab_norm_w=ab_norm_w, ab_w_in=ab_w_in, ret_norm_w=ret_norm_w, ssd_conv_w=ssd_conv_w,
             ssd_conv_b=ssd_conv_b, ssd_dt_bias=ssd_dt_bias, ssd_A_log=ssd_A_log, ssd_D=ssd_D,
             ssd_norm_w=ssd_norm_w, ab_w_out=ab_w_out, cd_norm_w=cd_norm_w, cd_w_in=cd_w_in,
             fox_f_bias=fox_f_bias, sconv_w=sconv_w, cd_w_out=cd_w_out, ffn_norm_w=ffn_norm_w,
             ffn_w_gate=ffn_w_gate, ffn_w_up=ffn_w_up, ffn_conv_w=ffn_conv_w, ffn_conv_b=ffn_conv_b,
             ffn_w_down=ffn_w_down, final_norm_w=final_norm_w)
    assert x_prompt.shape[-1] == D_MODEL and ab_w_in.shape == (1, D_MODEL, AB_MAIN + SSD_HEADS)
    assert cd_w_in.shape == (1, D_MODEL, CD_MAIN + FOX_HEADS) and ffn_w_gate.shape == (2, D_MODEL, D_FF)
    wb = _prep_weights(p)
    bp, lp_, _ = x_prompt.shape
    bs, ls, _ = x_sample.shape
    past = cache_fox_k.shape[2]
    (y_prompt, p_ret, p_ssd, p_ssd_conv, p_fox_k, p_fox_v, p_fox_logf, p_sconv, p_ffn_conv) = _trunk(
        x_prompt, 0, None, None, None, None, None, None, None, None, p, wb, _tiles(bp, lp_))
    (y_sample, s_ret, s_ssd, s_ssd_conv, s_fox_k, s_fox_v, s_fox_logf, s_sconv, s_ffn_conv) = _trunk(
        x_sample, past, state_ret[0], state_ssd[0], state_ssd_conv[0], cache_fox_k[0], cache_fox_v[0],
        cache_fox_logf[0], state_sconv[0], state_ffn_conv, p, wb, _tiles(bs, ls, past))
    return (y_prompt, y_sample, p_ret, s_ret, p_ssd, s_ssd, p_ssd_conv, s_ssd_conv, p_fox_k, s_fox_k,
            p_fox_v, s_fox_v, p_fox_logf, s_fox_logf, p_sconv, s_sconv, p_ffn_conv, s_ffn_conv)
```

```python
import functools
import math

import numpy as np
import jax
import jax.numpy as jnp
from jax import lax
from jax.experimental import pallas as pl
from jax.experimental.pallas import tpu as pltpu

F32 = jnp.float32
BF16 = jnp.bfloat16
EPS = 1e-6
ROPE_BASE = 10000.0
NEG_INF = float("-inf")

D_MODEL = 2048
RET_HEADS, RET_DK, RET_DV = 4, 128, 256
SSD_DINNER, SSD_HEADDIM, SSD_HEADS, SSD_GROUPS, SSD_DSTATE, SSD_CONV = 1024, 64, 16, 2, 128, 4
SSD_CONV_DIM = SSD_DINNER + 2 * SSD_GROUPS * SSD_DSTATE
FOX_HEADS, FOX_HEAD_DIM = 8, 128
FOX_WIDTH = FOX_HEADS * FOX_HEAD_DIM
SC_DIM, SC_WIDTH = 1024, 3
D_FF, FFN_CONV = 5632, 3
AB_MAIN = 2 * RET_HEADS * RET_DK + 2 * RET_HEADS * RET_DV + SSD_DINNER + SSD_CONV_DIM
AB_PAD = AB_MAIN + 128
CD_MAIN = 3 * FOX_WIDTH + 3 * SC_DIM
CD_PAD = CD_MAIN + 128

LANES = 128
SUBLANES = 8
VMEM_LIMIT = 52 * 1024 * 1024


def _cparams(n_axes):
    return pltpu.CompilerParams(dimension_semantics=("arbitrary",) * n_axes,
                                vmem_limit_bytes=VMEM_LIMIT)


def _rms(xf, w):
    return xf * lax.rsqrt(jnp.mean(xf * xf, axis=-1, keepdims=True) + EPS) * w


def _softplus(x):
    return jnp.maximum(x, 0.0) + jnp.log1p(jnp.exp(-jnp.abs(x)))


def _split3(x):
    hi = x.astype(BF16)
    r1 = x - hi.astype(F32)
    mid = r1.astype(BF16)
    lo = (r1 - mid.astype(F32)).astype(BF16)
    return hi, mid, lo


def _dot(a, b):
    return jnp.dot(a, b, preferred_element_type=F32)


def _dot_nt(a, b):
    return lax.dot_general(a, b, (((1,), (1,)), ((), ())), preferred_element_type=F32)


def _dot_tn(a, b):
    return lax.dot_general(a, b, (((0,), (0,)), ((), ())), preferred_element_type=F32)


def _exact_lhs_dot(m_bf16, x):
    hi, mid, lo = _split3(x)
    return _dot(m_bf16, hi) + _dot(m_bf16, mid) + _dot(m_bf16, lo)


def _exact_rhs_dot(x, m_bf16):
    hi, mid, lo = _split3(x)
    return _dot(hi, m_bf16) + _dot(mid, m_bf16) + _dot(lo, m_bf16)


def _conv_rows(ext_ref, x, prev8, w_ref, width, rows):
    ext_ref[0:SUBLANES, :] = prev8
    ext_ref[SUBLANES:SUBLANES + rows, :] = x
    out = None
    for j in range(width):
        off = SUBLANES - (width - 1) + j
        term = ext_ref[off:off + rows, :] * w_ref[j:j + 1, :]
        out = term if out is None else out + term
    return out


def _seq_tiling(seq_len, tile_rows):
    if seq_len >= tile_rows:
        assert seq_len % tile_rows == 0
        return tile_rows, 1, seq_len // tile_rows
    assert tile_rows % seq_len == 0 and seq_len % SUBLANES == 0
    return seq_len, tile_rows // seq_len, 1


def _norm_matmul_kernel(x_ref, nw_ref, w_ref, ws_ref, o_ref, os_ref, h_ref):
    @pl.when(pl.program_id(1) == 0)
    def _():
        h_ref[...] = _rms(x_ref[...], nw_ref[...]).astype(BF16)
        os_ref[...] = _dot(h_ref[...], ws_ref[...])

    o_ref[...] = _dot(h_ref[...], w_ref[...])


def _norm_matmul(x, norm_w, w, w_small, *, tm, tn):
    m, d = x.shape
    n = w.shape[1]
    assert m % tm == 0 and n % tn == 0 and w_small.shape == (d, LANES)
    return pl.pallas_call(
        _norm_matmul_kernel,
        grid=(m // tm, n // tn),
        in_specs=[pl.BlockSpec((tm, d), lambda i, j: (i, 0)),
                  pl.BlockSpec((1, d), lambda i, j: (0, 0)),
                  pl.BlockSpec((d, tn), lambda i, j: (0, j)),
                  pl.BlockSpec((d, LANES), lambda i, j: (0, 0))],
        out_specs=[pl.BlockSpec((tm, tn), lambda i, j: (i, j)),
                   pl.BlockSpec((tm, LANES), lambda i, j: (i, 0))],
        out_shape=[jax.ShapeDtypeStruct((m, n), F32),
                   jax.ShapeDtypeStruct((m, LANES), F32)],
        scratch_shapes=[pltpu.VMEM((tm, d), BF16)],
        compiler_params=_cparams(2),
        name="norm_in_proj",
    )(x, norm_w.reshape(1, d), w, w_small)


def _proj_res_kernel(x_ref, a_ref, b_ref, wa_ref, wb_ref, o_ref):
    acc = _dot(a_ref[...], wa_ref[...])
    acc = acc + _dot(b_ref[...], wb_ref[...])
    o_ref[...] = x_ref[...] + acc


def _proj_residual(x, a, b, wa, wb, *, tm, tn):
    m, d = x.shape
    ka, kb = a.shape[1], b.shape[1]
    assert m % tm == 0 and d % tn == 0
    return pl.pallas_call(
        _proj_res_kernel,
        grid=(m // tm, d // tn),
        in_specs=[pl.BlockSpec((tm, tn), lambda i, j: (i, j)),
                  pl.BlockSpec((tm, ka), lambda i, j: (i, 0)),
                  pl.BlockSpec((tm, kb), lambda i, j: (i, 0)),
                  pl.BlockSpec((ka, tn), lambda i, j: (0, j)),
                  pl.BlockSpec((kb, tn), lambda i, j: (0, j))],
        out_specs=pl.BlockSpec((tm, tn), lambda i, j: (i, j)),
        out_shape=jax.ShapeDtypeStruct((m, d), F32),
        compiler_params=_cparams(2),
        name="out_proj_residual",
    )(x, a, b, wa, wb)


def _ffn_kernel(x_ref, nw_ref, wg_ref, wu_ref, wd_ref, cw_ref, cb_ref, hist_ref, fw_ref,
                o_ref, tail_ref, h_ref, carry_ref, ext_ref, *, tm, rows, spt, tpb, nf, final):
    i = pl.program_id(0)
    f = pl.program_id(1)

    @pl.when(f == 0)
    def _():
        xf = x_ref[...]
        h_ref[...] = _rms(xf, nw_ref[...]).astype(BF16)
        o_ref[...] = xf

    h = h_ref[...]
    a = _dot(h, wg_ref[...])
    u = _dot(h, wu_ref[...])
    first = (i % tpb) == 0
    convs = []
    for s in range(spt):
        a_s = a[s * rows:(s + 1) * rows]
        if tpb == 1:
            prev = hist_ref[s]
        else:
            prev = jnp.where(first, hist_ref[s], carry_ref[f])
        convs.append(_conv_rows(ext_ref, a_s, prev, cw_ref, FFN_CONV, rows))
        tail_ref[s] = a_s[rows - SUBLANES:rows]
    if tpb > 1:
        carry_ref[f] = a[tm - SUBLANES:tm]
    conv = convs[0] if spt == 1 else jnp.concatenate(convs, axis=0)
    act = (jax.nn.silu(conv + cb_ref[...]) * u).astype(BF16)
    o_ref[...] += _dot(act, wd_ref[...])
    if final:
        @pl.when(f == nf - 1)
        def _():
            o_ref[...] = _rms(o_ref[...], fw_ref[...])


def _conv_ffn(x, norm_w, wg, wu, wd, conv_w, conv_b, hist8, final_w, *, seq_len, tm, tf, final):
    m, d = x.shape
    ff = wg.shape[1]
    assert m % tm == 0 and ff % tf == 0
    rows, spt, tpb = _seq_tiling(seq_len, tm)
    nm, nf = m // tm, ff // tf
    hist_map = (lambda i, f: (i // tpb, 0, f)) if spt == 1 else (lambda i, f: (i, 0, f))
    kern = functools.partial(_ffn_kernel, tm=tm, rows=rows, spt=spt, tpb=tpb, nf=nf, final=final)
    out, tails = pl.pallas_call(
        kern,
        grid=(nm, nf),
        in_specs=[pl.BlockSpec((tm, d), lambda i, f: (i, 0), pipeline_mode=pl.Buffered(1)),
                  pl.BlockSpec((1, d), lambda i, f: (0, 0)),
                  pl.BlockSpec((d, tf), lambda i, f: (0, f)),
                  pl.BlockSpec((d, tf), lambda i, f: (0, f)),
                  pl.BlockSpec((tf, d), lambda i, f: (f, 0)),
                  pl.BlockSpec((FFN_CONV, tf), lambda i, f: (0, f)),
                  pl.BlockSpec((1, tf), lambda i, f: (0, f)),
                  pl.BlockSpec((spt, SUBLANES, tf), hist_map),
                  pl.BlockSpec((1, d), lambda i, f: (0, 0))],
        out_specs=[pl.BlockSpec((tm, d), lambda i, f: (i, 0)),
                   pl.BlockSpec((spt, SUBLANES, tf), lambda i, f: (i, 0, f))],
        out_shape=[jax.ShapeDtypeStruct((m, d), F32),
                   jax.ShapeDtypeStruct((nm * spt, SUBLANES, ff), F32)],
        scratch_shapes=[pltpu.VMEM((tm, d), BF16),
                        pltpu.VMEM((nf, SUBLANES, tf), F32),
                        pltpu.VMEM((rows + SUBLANES, tf), F32)],
        compiler_params=_cparams(2),
        name="conv_ffn",
    )(x, norm_w.reshape(1, d), wg, wu, wd, conv_w, conv_b.reshape(1, ff), hist8, final_w.reshape(1, d))
    return out, tails


def _retention_kernel(q_ref, k_ref, v_ref, g_ref, cos_ref, sin_ref, st_ref, nw_ref,
                      y_ref, so_ref, *, c):
    ci = pl.program_id(1)

    @pl.when(ci == 0)
    def _():
        so_ref[...] = st_ref[...]

    cos = cos_ref[...]
    sin = sin_ref[...]
    ii = lax.broadcasted_iota(jnp.int32, (c, c), 0)
    jj = lax.broadcasted_iota(jnp.int32, (c, c), 1)
    diff = (ii - jj).astype(F32)
    causal = ii >= jj
    ridx = lax.broadcasted_iota(jnp.int32, (c, 1), 0).astype(F32)
    for h in range(RET_HEADS):
        lg = math.log1p(-(2.0 ** (-5.0 - h)))
        q = q_ref[:, h * RET_DK:(h + 1) * RET_DK]
        k = k_ref[:, h * RET_DK:(h + 1) * RET_DK]
        v = v_ref[:, h * RET_DV:(h + 1) * RET_DV]
        qr = q * cos + pltpu.roll(q, RET_DK // 2, 1) * sin
        kr = (k * cos + pltpu.roll(k, RET_DK // 2, 1) * sin) * (RET_DK ** -0.5)
        qb = qr.astype(BF16)
        kb = kr.astype(BF16)
        vb = v.astype(BF16)
        decay = jnp.exp(jnp.where(causal, diff * lg, NEG_INF))
        inner = jnp.exp((ridx + 1.0) * lg)
        sdecay = jnp.exp((c - 1.0 - ridx) * lg)
        s = so_ref[0, h]
        scores = _dot_nt(qb, kb) * decay
        y = _dot(scores.astype(BF16), vb)
        y = y + _dot(qb, s.astype(BF16)) * inner
        kd = (kr * sdecay).astype(BF16)
        so_ref[0, h] = math.exp(c * lg) * s + _dot_tn(kd, vb)
        mu = jnp.mean(y, axis=-1, keepdims=True)
        yc = y - mu
        var = jnp.mean(yc * yc, axis=-1, keepdims=True)
        yn = yc * lax.rsqrt(var + EPS) * nw_ref[:, h * RET_DV:(h + 1) * RET_DV]
        g = g_ref[:, h * RET_DV:(h + 1) * RET_DV]
        y_ref[:, h * RET_DV:(h + 1) * RET_DV] = (jax.nn.silu(g) * yn).astype(BF16)


def _retention(proj, cosf, sinf, state, norm_w, *, bsz, seq_len, c):
    m = proj.shape[0]
    nc = seq_len // c
    assert seq_len % c == 0
    qk_w = RET_HEADS * RET_DK
    v_w = RET_HEADS * RET_DV
    row = lambda b, ci: b * nc + ci
    y, s_new = pl.pallas_call(
        functools.partial(_retention_kernel, c=c),
        grid=(bsz, nc),
        in_specs=[pl.BlockSpec((c, qk_w), lambda b, ci: (row(b, ci), 0)),
                  pl.BlockSpec((c, qk_w), lambda b, ci: (row(b, ci), 1)),
                  pl.BlockSpec((c, v_w), lambda b, ci: (row(b, ci), 1)),
                  pl.BlockSpec((c, v_w), lambda b, ci: (row(b, ci), 2)),
                  pl.BlockSpec((c, RET_DK), lambda b, ci: (ci, 0)),
                  pl.BlockSpec((c, RET_DK), lambda b, ci: (ci, 0)),
                  pl.BlockSpec((1, RET_HEADS, RET_DK, RET_DV), lambda b, ci: (b, 0, 0, 0)),
                  pl.BlockSpec((1, v_w), lambda b, ci: (0, 0))],
        out_specs=[pl.BlockSpec((c, v_w), lambda b, ci: (row(b, ci), 0)),
                   pl.BlockSpec((1, RET_HEADS, RET_DK, RET_DV), lambda b, ci: (b, 0, 0, 0))],
        out_shape=[jax.ShapeDtypeStruct((m, v_w), BF16),
                   jax.ShapeDtypeStruct(state.shape, F32)],
        compiler_params=_cparams(2),
        name="retention",
    )(proj, proj, proj, proj, cosf, sinf, state, norm_w.reshape(1, v_w))
    return y, s_new


def _ssd_kernel(z_ref, xs_ref, bc_ref, dt_ref, hx_ref, hbc_ref, st_ref,
                cwx_ref, cwbc_ref, cbx_ref, cbbc_ref, dtb_ref, alog_ref, dsk_ref, nw_ref,
                tri_ref, exp_ref,
                y_ref, so_ref,
                st_scr, cx_scr, cbc_scr, extx_scr, extbc_scr, yh_scr, *, c, nc):
    ci = pl.program_id(1)
    gw = SSD_DINNER // SSD_GROUPS
    hpg = SSD_HEADS // SSD_GROUPS

    @pl.when(ci == 0)
    def _():
        st_scr[...] = st_ref[0].T
        cx_scr[...] = hx_ref[0]
        cbc_scr[...] = hbc_ref[0]

    xs_raw = xs_ref[...]
    bc_raw = bc_ref[...]
    xs = jax.nn.silu(_conv_rows(extx_scr, xs_raw, cx_scr[...], cwx_ref, SSD_CONV, c) + cbx_ref[...])
    bcm = jax.nn.silu(_conv_rows(extbc_scr, bc_raw, cbc_scr[...], cwbc_ref, SSD_CONV, c) + cbbc_ref[...])
    cx_scr[...] = xs_raw[c - SUBLANES:c]
    cbc_scr[...] = bc_raw[c - SUBLANES:c]

    tri = tri_ref[...]
    expand = exp_ref[...]
    dt = _softplus(dt_ref[...] + dtb_ref[...])
    a = -jnp.exp(alog_ref[...])
    acs = _exact_lhs_dot(tri, dt * a)
    acs_t = acs.T
    acs_last = acs[c - 1:c, :]
    exp_acs = jnp.exp(acs)
    to_end = jnp.exp(acs_last - acs)
    dt_e = _exact_rhs_dot(dt, expand)
    to_end_e = _exact_rhs_dot(to_end, expand)
    exp_acs_e = _exact_rhs_dot(exp_acs, expand)
    chunk_dec_e = _exact_rhs_dot(jnp.exp(acs_last), expand)

    xdt = xs * dt_e
    xdt_b = xdt.astype(BF16)
    xend_b = (xdt * to_end_e).astype(BF16)
    ii = lax.broadcasted_iota(jnp.int32, (c, c), 0)
    jj = lax.broadcasted_iota(jnp.int32, (c, c), 1)
    causal = ii >= jj
    nb = SSD_GROUPS * SSD_DSTATE
    for g in range(SSD_GROUPS):
        b_g = bcm[:, g * SSD_DSTATE:(g + 1) * SSD_DSTATE].astype(BF16)
        c_g = bcm[:, nb + g * SSD_DSTATE:nb + (g + 1) * SSD_DSTATE].astype(BF16)
        cb = _dot_nt(c_g, b_g)
        s_g = st_scr[:, g * gw:(g + 1) * gw]
        y_state = _dot(c_g, s_g.astype(BF16)) * exp_acs_e[:, g * gw:(g + 1) * gw]
        for r in range(hpg):
            hh = g * hpg + r
            seg = acs[:, hh:hh + 1] - acs_t[hh:hh + 1, :]
            lmat = jnp.exp(jnp.where(causal, seg, NEG_INF))
            mm = (cb * lmat).astype(BF16)
            lo = hh * SSD_HEADDIM
            yh_scr[:, lo:lo + SSD_HEADDIM] = (
                _dot(mm, xdt_b[:, lo:lo + SSD_HEADDIM]) + y_state[:, r * SSD_HEADDIM:(r + 1) * SSD_HEADDIM])
        upd = _dot_tn(b_g, xend_b[:, g * gw:(g + 1) * gw])
        st_scr[:, g * gw:(g + 1) * gw] = chunk_dec_e[:, g * gw:(g + 1) * gw] * s_g + upd

    y = yh_scr[...] + dsk_ref[...] * xs
    z = z_ref[...]
    y_ref[...] = _rms(y * jax.nn.silu(z), nw_ref[...]).astype(BF16)

    @pl.when(ci == nc - 1)
    def _():
        so_ref[0] = st_scr[...].T


def _ssd(proj, dt_proj, hist8, state, conv_w, conv_b, dt_bias, a_log, d_skip, norm_w, *, bsz, seq_len, c):
    m = proj.shape[0]
    nc = seq_len // c
    assert seq_len % c == 0
    row = lambda b, ci: b * nc + ci
    const2 = lambda b, ci: (0, 0)
    di, bcw = SSD_DINNER, 2 * SSD_GROUPS * SSD_DSTATE
    tri = jnp.asarray(np.tril(np.ones((c, c), np.float32)), BF16)
    expand = np.zeros((LANES, di), np.float32)
    for h in range(SSD_HEADS):
        expand[h, h * SSD_HEADDIM:(h + 1) * SSD_HEADDIM] = 1.0
    expand = jnp.asarray(expand, BF16)
    pad_row = lambda v: jnp.pad(v.astype(F32), (0, LANES - v.shape[0])).reshape(1, LANES)
    st2 = state.reshape(bsz, di, SSD_DSTATE)
    y, s_new = pl.pallas_call(
        functools.partial(_ssd_kernel, c=c, nc=nc),
        grid=(bsz, nc),
        in_specs=[pl.BlockSpec((c, di), lambda b, ci: (row(b, ci), 3)),
                  pl.BlockSpec((c, di), lambda b, ci: (row(b, ci), 4)),
                  pl.BlockSpec((c, bcw), lambda b, ci: (row(b, ci), 10)),
                  pl.BlockSpec((c, LANES), lambda b, ci: (row(b, ci), 0)),
                  pl.BlockSpec((1, SUBLANES, di), lambda b, ci: (b, 0, 0)),
                  pl.BlockSpec((1, SUBLANES, bcw), lambda b, ci: (b, 0, 2)),
                  pl.BlockSpec((1, di, SSD_DSTATE), lambda b, ci: (b, 0, 0)),
                  pl.BlockSpec((SSD_CONV, di), const2),
                  pl.BlockSpec((SSD_CONV, bcw), lambda b, ci: (0, 2)),
                  pl.BlockSpec((1, di), const2),
                  pl.BlockSpec((1, bcw), lambda b, ci: (0, 2)),
                  pl.BlockSpec((1, LANES), const2),
                  pl.BlockSpec((1, LANES), const2),
                  pl.BlockSpec((1, di), const2),
                  pl.BlockSpec((1, di), const2),
                  pl.BlockSpec((c, c), const2),
                  pl.BlockSpec((LANES, di), const2)],
        out_specs=[pl.BlockSpec((c, di), lambda b, ci: (row(b, ci), 0)),
                   pl.BlockSpec((1, di, SSD_DSTATE), lambda b, ci: (b, 0, 0))],
        out_shape=[jax.ShapeDtypeStruct((m, di), BF16),
                   jax.ShapeDtypeStruct(st2.shape, F32)],
        scratch_shapes=[pltpu.VMEM((SSD_DSTATE, di), F32),
                        pltpu.VMEM((SUBLANES, di), F32),
                        pltpu.VMEM((SUBLANES, bcw), F32),
                        pltpu.VMEM((c + SUBLANES, di), F32),
                        pltpu.VMEM((c + SUBLANES, bcw), F32),
                        pltpu.VMEM((c, di), F32)],
        compiler_params=_cparams(2),
        name="ssd",
    )(proj, proj, proj, dt_proj, hist8, hist8, st2,
      conv_w, conv_w, conv_b.reshape(1, -1), conv_b.reshape(1, -1),
      pad_row(dt_bias), pad_row(a_log), jnp.repeat(d_skip.astype(F32), SSD_HEADDIM).reshape(1, di),
      norm_w.reshape(1, di), tri, expand)
    return y, s_new.reshape(state.shape)


def _decode_cum_kernel(lfc_ref, fl_ref, b_ref, tri_ref, cumt_c_ref, lfn_ref, cumn_ref, cumt_n_ref,
                       carry_ref, *, c, ncb):
    j = pl.program_id(0)

    @pl.when(j == 0)
    def _():
        carry_ref[...] = jnp.zeros_like(carry_ref)

    @pl.when(j < ncb)
    def _():
        cum = _exact_lhs_dot(tri_ref[...], lfc_ref[...]) + carry_ref[...]
        carry_ref[...] = cum[c - 1:c, :]
        cumt_c_ref[...] = cum.T

    @pl.when(j == ncb)
    def _():
        lf = -_softplus(-(fl_ref[...] + b_ref[...]))
        lfn_ref[...] = lf
        cum = _exact_lhs_dot(tri_ref[0:LANES, 0:LANES], lf) + carry_ref[...]
        cumn_ref[...] = cum
        cumt_n_ref[...] = cum.T


def _decode_cum(cache_lf, fl_new, bias, *, c):
    past = cache_lf.shape[0]
    assert past % c == 0 and c % LANES == 0 and fl_new.shape == (LANES, LANES)
    ncb = past // c
    tri = jnp.asarray(np.tril(np.ones((c, c), np.float32)), BF16)
    blk = lambda j: jnp.minimum(j, ncb - 1)
    sq = jax.ShapeDtypeStruct((LANES, LANES), F32)
    return pl.pallas_call(
        functools.partial(_decode_cum_kernel, c=c, ncb=ncb),
        grid=(ncb + 1,),
        in_specs=[pl.BlockSpec((c, LANES), lambda j: (blk(j), 0)),
                  pl.BlockSpec((LANES, LANES), lambda j: (0, 0)),
                  pl.BlockSpec((1, LANES), lambda j: (0, 0)),
                  pl.BlockSpec((c, c), lambda j: (0, 0))],
        out_specs=[pl.BlockSpec((LANES, c), lambda j: (0, blk(j))),
                   pl.BlockSpec((LANES, LANES), lambda j: (0, 0)),
                   pl.BlockSpec((LANES, LANES), lambda j: (0, 0)),
                   pl.BlockSpec((LANES, LANES), lambda j: (0, 0))],
        out_shape=[jax.ShapeDtypeStruct((LANES, past), F32), sq, sq, sq],
        scratch_shapes=[pltpu.VMEM((1, LANES), F32)],
        compiler_params=_cparams(1),
        name="decode_logf_cumsum",
    )(cache_lf, fl_new, bias, tri)


FOX_AUG = 2 * FOX_HEAD_DIM
N_BIAS_PIECES = 3


def _fox_prep_kernel(q_ref, k_ref, v_ref, fl_ref, fb_ref, tri_ref, place_ref, ones_ref,
                     qa_ref, ka_ref, vb_ref, k32_ref, v32_ref, lf_ref, carry_ref, *, tp):
    @pl.when(pl.program_id(1) == 0)
    def _():
        carry_ref[...] = jnp.zeros_like(carry_ref)

    lf = -_softplus(-(fl_ref[...] + fb_ref[...]))
    lf_ref[...] = lf
    cum = _exact_lhs_dot(tri_ref[...], lf) + carry_ref[...]
    carry_ref[...] = cum[tp - 1:tp, :]
    pieces = _split3(cum * (FOX_HEAD_DIM ** 0.5))
    n = N_BIAS_PIECES
    aug_q = ones_ref[0:1, :] + sum(_dot(pieces[r], place_ref[r]) for r in range(n))
    aug_k = ones_ref[1:2, :] - sum(_dot(pieces[r], place_ref[n + r]) for r in range(n))
    for h in range(FOX_HEADS):
        src = slice(h * FOX_HEAD_DIM, (h + 1) * FOX_HEAD_DIM)
        feat = slice(h * FOX_AUG, h * FOX_AUG + FOX_HEAD_DIM)
        bias = slice(h * FOX_AUG + FOX_HEAD_DIM, (h + 1) * FOX_AUG)
        qa_ref[:, feat] = q_ref[:, src].astype(BF16)
        qa_ref[:, bias] = aug_q[:, src].astype(BF16)
        ka_ref[:, feat] = k_ref[:, src].astype(BF16)
        ka_ref[:, bias] = aug_k[:, src].astype(BF16)
    k = k_ref[...]
    v = v_ref[...]
    k32_ref[...] = k
    v32_ref[...] = v
    vb_ref[...] = v.astype(BF16)


def _fox_prep(proj, fl_proj, f_bias, *, bsz, seq_len, tp):
    m = proj.shape[0]
    nt = seq_len // tp
    assert seq_len % tp == 0
    w = FOX_WIDTH
    tri = jnp.asarray(np.tril(np.ones((tp, tp), np.float32)), BF16)
    n = N_BIAS_PIECES
    place = np.zeros((2 * n, LANES, w), np.float32)
    ones = np.zeros((SUBLANES, w), np.float32)
    for h in range(FOX_HEADS):
        for r in range(2 * n):
            place[r, h, h * FOX_HEAD_DIM + r] = 1.0
        ones[0, h * FOX_HEAD_DIM + n:h * FOX_HEAD_DIM + 2 * n] = 1.0
        ones[1, h * FOX_HEAD_DIM:h * FOX_HEAD_DIM + n] = 1.0
    row = lambda b, ti: (b * nt + ti, 0)
    const2 = lambda b, ti: (0, 0)
    return pl.pallas_call(
        functools.partial(_fox_prep_kernel, tp=tp),
        grid=(bsz, nt),
        in_specs=[pl.BlockSpec((tp, w), lambda b, ti: (b * nt + ti, 0)),
                  pl.BlockSpec((tp, w), lambda b, ti: (b * nt + ti, 1)),
                  pl.BlockSpec((tp, w), lambda b, ti: (b * nt + ti, 2)),
                  pl.BlockSpec((tp, LANES), row),
                  pl.BlockSpec((1, LANES), const2),
                  pl.BlockSpec((tp, tp), const2),
                  pl.BlockSpec((2 * n, LANES, w), lambda b, ti: (0, 0, 0)),
                  pl.BlockSpec((SUBLANES, w), const2)],
        out_specs=[pl.BlockSpec((tp, FOX_HEADS * FOX_AUG), row),
                   pl.BlockSpec((tp, FOX_HEADS * FOX_AUG), row),
                   pl.BlockSpec((tp, w), row),
                   pl.BlockSpec((tp, w), row),
                   pl.BlockSpec((tp, w), row),
                   pl.BlockSpec((tp, LANES), row)],
        out_shape=[jax.ShapeDtypeStruct((m, FOX_HEADS * FOX_AUG), BF16),
                   jax.ShapeDtypeStruct((m, FOX_HEADS * FOX_AUG), BF16),
                   jax.ShapeDtypeStruct((m, w), BF16),
                   jax.ShapeDtypeStruct((m, w), F32),
                   jax.ShapeDtypeStruct((m, w), F32),
                   jax.ShapeDtypeStruct((m, LANES), F32)],
        scratch_shapes=[pltpu.VMEM((1, LANES), F32)],
        compiler_params=_cparams(2),
        name="fox_prep",
    )(proj, proj, proj, fl_proj, f_bias, tri, jnp.asarray(place, BF16), jnp.asarray(ones, F32))


def _fox_kernel(qi_ref, ki_ref, q_ref, k_ref, v_ref, o_ref, m_ref, l_ref, acc_ref, *, t, ts):
    step = pl.program_id(2)
    qi = qi_ref[step]
    ki = ki_ref[step]
    to_log2 = (FOX_HEAD_DIM ** -0.5) * math.log2(math.e)

    @pl.when(ki == 0)
    def _():
        m_ref[...] = jnp.full_like(m_ref, NEG_INF)
        l_ref[...] = jnp.zeros_like(l_ref)
        acc_ref[...] = jnp.zeros_like(acc_ref)

    def update(diagonal):
        for r in range(t // ts):
            rows = slice(r * ts, (r + 1) * ts)
            nk = (r + 1) * ts if diagonal else t
            s = _dot_nt(q_ref[rows, :], k_ref[0:nk, :])
            if diagonal:
                ri = lax.broadcasted_iota(jnp.int32, (ts, nk), 0) + r * ts
                ci = lax.broadcasted_iota(jnp.int32, (ts, nk), 1)
                s = jnp.where(ci <= ri, s, NEG_INF)
            m_old = m_ref[rows, :]
            m_new = jnp.maximum(m_old, jnp.max(s, axis=1, keepdims=True))
            alpha = jnp.exp2((m_old - m_new) * to_log2)
            p = jnp.exp2((s - pltpu.repeat(m_new, nk // LANES, axis=1)) * to_log2)
            l_ref[rows, :] = alpha * l_ref[rows, :] + jnp.sum(p, axis=1, keepdims=True)
            acc_ref[rows, :] = alpha * acc_ref[rows, :] + _dot(p.astype(BF16), v_ref[0:nk, :])
            m_ref[rows, :] = m_new

    @pl.when(ki < qi)
    def _():
        update(False)

    @pl.when(ki == qi)
    def _():
        update(True)
        o_ref[...] = (acc_ref[...] / l_ref[...]).astype(BF16)


def _fox_prompt(qa, ka, vb, *, bsz, seq_len, t, ts):
    m = qa.shape[0]
    nq = seq_len // t
    assert seq_len % t == 0 and t % ts == 0
    pairs = [(qi, ki) for qi in range(nq) for ki in range(qi + 1)]
    qi_tab = jnp.asarray([p[0] for p in pairs], jnp.int32)
    ki_tab = jnp.asarray([p[1] for p in pairs], jnp.int32)
    grid_spec = pltpu.PrefetchScalarGridSpec(
        num_scalar_prefetch=2,
        grid=(bsz, FOX_HEADS, len(pairs)),
        in_specs=[pl.BlockSpec((t, FOX_AUG), lambda b, h, s, qi, ki: (b * nq + qi[s], h)),
                  pl.BlockSpec((t, FOX_AUG), lambda b, h, s, qi, ki: (b * nq + ki[s], h)),
                  pl.BlockSpec((t, FOX_HEAD_DIM), lambda b, h, s, qi, ki: (b * nq + ki[s], h))],
        out_specs=pl.BlockSpec((t, FOX_HEAD_DIM), lambda b, h, s, qi, ki: (b * nq + qi[s], h)),
        scratch_shapes=[pltpu.VMEM((t, LANES), F32), pltpu.VMEM((t, LANES), F32),
                        pltpu.VMEM((t, FOX_HEAD_DIM), F32)],
    )
    return pl.pallas_call(
        functools.partial(_fox_kernel, t=t, ts=ts),
        grid_spec=grid_spec,
        out_shape=jax.ShapeDtypeStruct((m, FOX_WIDTH), BF16),
        compiler_params=_cparams(3),
        name="fox_attention",
    )(qi_tab, ki_tab, qa, ka, vb)


def _fox_decode_kernel(q_ref, kn_ref, vn_ref, kc_ref, vc_ref, cq_ref, ctc_ref, ctn_ref, o_ref,
                       m_ref, l_ref, acc_ref, *, lq, ncb):
    j = pl.program_id(1)

    @pl.when(j == 0)
    def _():
        m_ref[...] = jnp.full_like(m_ref, NEG_INF)
        l_ref[...] = jnp.zeros_like(l_ref)
        acc_ref[...] = jnp.zeros_like(acc_ref)

    def attend(k_head, v_head, ck_all, causal):
        tk = ck_all.shape[1]
        for h in range(FOX_HEADS):
            lo = h * FOX_HEAD_DIM
            qh = q_ref[:, lo:lo + FOX_HEAD_DIM].astype(BF16)
            kh = k_head(lo).astype(BF16)
            vh = v_head(lo).astype(BF16)
            s = _dot_nt(qh, kh) * (FOX_HEAD_DIM ** -0.5)
            s = s + (cq_ref[0, :, h:h + 1] - ck_all[h:h + 1, :])
            if causal:
                rows = lax.broadcasted_iota(jnp.int32, (lq, tk), 0)
                cols = lax.broadcasted_iota(jnp.int32, (lq, tk), 1)
                s = jnp.where(cols <= rows, s, NEG_INF)
            m_old = m_ref[h]
            m_new = jnp.maximum(m_old, jnp.max(s, axis=1, keepdims=True))
            alpha = jnp.exp(m_old - m_new)
            p = jnp.exp(s - m_new)
            l_ref[h] = alpha * l_ref[h] + jnp.sum(p, axis=1, keepdims=True)
            acc_ref[:, lo:lo + FOX_HEAD_DIM] = alpha * acc_ref[:, lo:lo + FOX_HEAD_DIM] + _dot(p.astype(BF16), vh)
            m_ref[h] = m_new

    @pl.when(j < ncb)
    def _():
        attend(lambda lo: kc_ref[0, :, lo // FOX_HEAD_DIM, :], lambda lo: vc_ref[0, :, lo // FOX_HEAD_DIM, :],
               ctc_ref[...], False)

    @pl.when(j == ncb)
    def _():
        attend(lambda lo: kn_ref[:, lo:lo + FOX_HEAD_DIM], lambda lo: vn_ref[:, lo:lo + FOX_HEAD_DIM],
               ctn_ref[:, 0:lq], True)
        for h in range(FOX_HEADS):
            lo = h * FOX_HEAD_DIM
            o_ref[:, lo:lo + FOX_HEAD_DIM] = (acc_ref[:, lo:lo + FOX_HEAD_DIM] / l_ref[h]).astype(BF16)


def _fox_decode(proj, cache_k, cache_v, cq, cum_t_cache, cum_t_new, *, bsz, lq, tk):
    past = cache_k.shape[1]
    assert past % tk == 0 and lq <= LANES
    ncb = past // tk
    w = FOX_WIDTH
    cache_spec = pl.BlockSpec((1, tk, FOX_HEADS, FOX_HEAD_DIM), lambda b, j: (b, jnp.minimum(j, ncb - 1), 0, 0))
    return pl.pallas_call(
        functools.partial(_fox_decode_kernel, lq=lq, ncb=ncb),
        grid=(bsz, ncb + 1),
        in_specs=[pl.BlockSpec((lq, w), lambda b, j: (b, 0)),
                  pl.BlockSpec((lq, w), lambda b, j: (b, 1)),
                  pl.BlockSpec((lq, w), lambda b, j: (b, 2)),
                  cache_spec,
                  cache_spec,
                  pl.BlockSpec((1, lq, LANES), lambda b, j: (b, 0, 0)),
                  pl.BlockSpec((SUBLANES, tk), lambda b, j: (b, jnp.minimum(j, ncb - 1))),
                  pl.BlockSpec((SUBLANES, LANES), lambda b, j: (b, 0))],
        out_specs=pl.BlockSpec((lq, w), lambda b, j: (b, 0)),
        out_shape=jax.ShapeDtypeStruct((bsz * lq, w), BF16),
        scratch_shapes=[pltpu.VMEM((FOX_HEADS, lq, 1), F32), pltpu.VMEM((FOX_HEADS, lq, 1), F32),
                        pltpu.VMEM((lq, w), F32)],
        compiler_params=_cparams(2),
        name="fox_decode",
    )(proj, proj, proj, cache_k, cache_v, cq, cum_t_cache, cum_t_new)


def _sconv_kernel(u_ref, bg_ref, cg_ref, cw_ref, hist_ref, y_ref, tail_ref, carry_ref, ext_ref,
                  *, tm, rows, spt, tpb):
    i = pl.program_id(0)
    w = cg_ref[...] * u_ref[...]
    first = (i % tpb) == 0
    for s in range(spt):
        w_s = w[s * rows:(s + 1) * rows]
        if tpb == 1:
            prev = hist_ref[s]
        else:
            prev = jnp.where(first, hist_ref[s], carry_ref[...])
        conv = _conv_rows(ext_ref, w_s, prev, cw_ref, SC_WIDTH, rows)
        y_ref[s * rows:(s + 1) * rows, :] = (bg_ref[s * rows:(s + 1) * rows, :] * conv).astype(BF16)
        tail_ref[s] = w_s[rows - SUBLANES:rows]
    if tpb > 1:
        carry_ref[...] = w[tm - SUBLANES:tm]


def _sconv(proj, conv_w, hist8, *, seq_len, tm):
    m = proj.shape[0]
    assert m % tm == 0
    rows, spt, tpb = _seq_tiling(seq_len, tm)
    nm = m // tm
    hist_map = (lambda i: (i // tpb, 0, 0)) if spt == 1 else (lambda i: (i, 0, 0))
    base = 3 * FOX_WIDTH // SC_DIM
    return pl.pallas_call(
        functools.partial(_sconv_kernel, tm=tm, rows=rows, spt=spt, tpb=tpb),
        grid=(nm,),
        in_specs=[pl.BlockSpec((tm, SC_DIM), lambda i: (i, base)),
                  pl.BlockSpec((tm, SC_DIM), lambda i: (i, base + 1)),
                  pl.BlockSpec((tm, SC_DIM), lambda i: (i, base + 2)),
                  pl.BlockSpec((SC_WIDTH, SC_DIM), lambda i: (0, 0)),
                  pl.BlockSpec((spt, SUBLANES, SC_DIM), hist_map)],
        out_specs=[pl.BlockSpec((tm, SC_DIM), lambda i: (i, 0)),
                   pl.BlockSpec((spt, SUBLANES, SC_DIM), lambda i: (i, 0, 0))],
        out_shape=[jax.ShapeDtypeStruct((m, SC_DIM), BF16),
                   jax.ShapeDtypeStruct((nm * spt, SUBLANES, SC_DIM), F32)],
        scratch_shapes=[pltpu.VMEM((SUBLANES, SC_DIM), F32),
                        pltpu.VMEM((rows + SUBLANES, SC_DIM), F32)],
        compiler_params=_cparams(1),
        name="gated_short_conv",
    )(proj, proj, proj, conv_w, hist8)


def _hist8(state):
    n, w1, c = state.shape
    return jnp.concatenate([jnp.zeros((n, SUBLANES - w1, c), F32), state.astype(F32)], axis=1)


def _tails(tails, n_seq, seq_len, tile_rows, keep):
    per_seq = max(1, seq_len // tile_rows)
    idx = (jnp.arange(n_seq) + 1) * per_seq - 1
    return tails[idx][:, SUBLANES - keep:, :]


def _rope_tables(pos0, length):
    half = RET_DK // 2
    inv = ROPE_BASE ** (-jnp.arange(half, dtype=F32) / half)
    ang = (pos0 + jnp.arange(length)).astype(F32)[:, None] * inv[None, :]
    cos, sin = jnp.cos(ang), jnp.sin(ang)
    return jnp.concatenate([cos, cos], axis=1), jnp.concatenate([-sin, sin], axis=1)


def _prep_weights(p):
    d = D_MODEL
    ab_in = p['ab_w_in'][0]
    cd_in = p['cd_w_in'][0]
    f0 = 3 * FOX_WIDTH
    pad_cols = lambda w: jnp.pad(w, ((0, 0), (0, LANES - w.shape[1]))).astype(BF16)
    return dict(
        ab_in=ab_in[:, :AB_MAIN].astype(BF16),
        ab_small=pad_cols(ab_in[:, AB_MAIN:]),
        cd_in=jnp.concatenate([cd_in[:, :f0], cd_in[:, f0 + FOX_HEADS:]], axis=1).astype(BF16),
        cd_small=pad_cols(cd_in[:, f0:f0 + FOX_HEADS]),
        ab_out=p['ab_w_out'][0].astype(BF16),
        cd_out=p['cd_w_out'][0].astype(BF16),
        ffn_gate=p['ffn_w_gate'].astype(BF16),
        ffn_up=p['ffn_w_up'].astype(BF16),
        ffn_down=p['ffn_w_down'].astype(BF16),
    )


def _trunk(x, pos0, st_ret, st_ssd, st_ssd_conv, c_k, c_v, c_logf, st_sconv, st_ffn, p, wb, t):
    bsz, length, d = x.shape
    m = bsz * length
    xf = x.reshape(m, d)
    zeros = lambda *shape: jnp.zeros(shape, F32)

    proj, dt_proj = _norm_matmul(xf, p['ab_norm_w'][0], wb['ab_in'], wb['ab_small'],
                                 tm=t['tm_proj'], tn=t['tn_ab'])
    cosf, sinf = _rope_tables(pos0, length)
    ret_state = zeros(bsz, RET_HEADS, RET_DK, RET_DV) if st_ret is None else st_ret
    y_ret, ret_new = _retention(proj, cosf, sinf, ret_state, p['ret_norm_w'][0],
                                bsz=bsz, seq_len=length, c=t['c_ret'])
    ssd_state = zeros(bsz, SSD_HEADS, SSD_HEADDIM, SSD_DSTATE) if st_ssd is None else st_ssd
    ssd_hist = zeros(bsz, SSD_CONV - 1, SSD_CONV_DIM) if st_ssd_conv is None else st_ssd_conv
    y_ssd, ssd_new = _ssd(proj, dt_proj, _hist8(ssd_hist), ssd_state, p['ssd_conv_w'][0], p['ssd_conv_b'][0],
                          p['ssd_dt_bias'][0], p['ssd_A_log'][0], p['ssd_D'][0], p['ssd_norm_w'][0],
                          bsz=bsz, seq_len=length, c=t['c_ssd'])
    xbc_lo = AB_MAIN - SSD_CONV_DIM
    ssd_conv_new = proj.reshape(bsz, length, -1)[:, length - (SSD_CONV - 1):, xbc_lo:AB_MAIN]
    ab_out = wb['ab_out']
    xf = _proj_residual(xf, y_ret, y_ssd, ab_out[:RET_HEADS * RET_DV], ab_out[RET_HEADS * RET_DV:],
                        tm=t['tm_out'], tn=t['tn_out'])

    ffn_new = []
    ffn_hist0 = zeros(bsz, FFN_CONV - 1, D_FF) if st_ffn is None else st_ffn[0]
    xf, tails = _conv_ffn(xf, p['ffn_norm_w'][0], wb['ffn_gate'][0], wb['ffn_up'][0], wb['ffn_down'][0],
                          p['ffn_conv_w'][0], p['ffn_conv_b'][0], _hist8(ffn_hist0), p['final_norm_w'],
                          seq_len=length, tm=t['tm_ffn'], tf=t['tf_ffn'], final=False)
    ffn_new.append(_tails(tails, bsz, length, t['tm_ffn'], FFN_CONV - 1))

    proj, fl_proj = _norm_matmul(xf, p['cd_norm_w'][0], wb['cd_in'], wb['cd_small'],
                                 tm=t['tm_proj'], tn=t['tn_cd'])
    f_bias = jnp.pad(p['fox_f_bias'][0].astype(F32), (0, LANES - FOX_HEADS)).reshape(1, LANES)
    head_shape = (bsz, length, FOX_HEADS, FOX_HEAD_DIM)
    if c_k is None:
        qa, ka, vb, k32, v32, logf = _fox_prep(proj, fl_proj, f_bias, bsz=bsz, seq_len=length, tp=t['t_prep'])
        y_fox = _fox_prompt(qa, ka, vb, bsz=bsz, seq_len=length, t=t['t_fox'], ts=t['ts_fox'])
        logf_new = logf.reshape(bsz, length, LANES)[:, :, :FOX_HEADS]
        k_new, v_new = k32.reshape(head_shape), v32.reshape(head_shape)
    else:
        proj3 = proj.reshape(bsz, length, -1)
        k_new = proj3[:, :, FOX_WIDTH:2 * FOX_WIDTH].reshape(head_shape)
        v_new = proj3[:, :, 2 * FOX_WIDTH:3 * FOX_WIDTH].reshape(head_shape)
        past = c_k.shape[1]
        pairs = bsz * FOX_HEADS
        assert pairs <= LANES and length <= LANES
        to_lanes = lambda a, rows: jnp.pad(jnp.swapaxes(a, 0, 1).reshape(a.shape[1], pairs),
                                           ((0, rows - a.shape[1]), (0, LANES - pairs)))
        from_lanes = lambda a: jnp.swapaxes(a[:length, :pairs].reshape(length, bsz, FOX_HEADS), 0, 1)
        cache_lf = to_lanes(c_logf.astype(F32), past)
        fl_rows = to_lanes(fl_proj.reshape(bsz, length, LANES)[:, :, :FOX_HEADS], LANES)
        bias_lanes = jnp.pad(jnp.tile(p['fox_f_bias'][0].astype(F32), bsz), (0, LANES - pairs)).reshape(1, LANES)
        cum_t_cache, lf_rows, cum_rows, cum_t_new = _decode_cum(cache_lf, fl_rows, bias_lanes, c=t['c_cum'])
        logf_new = from_lanes(lf_rows)
        cq = jnp.pad(from_lanes(cum_rows), ((0, 0), (0, 0), (0, LANES - FOX_HEADS)))
        y_fox = _fox_decode(proj, c_k, c_v, cq, cum_t_cache, cum_t_new, bsz=bsz, lq=length, tk=t['tk_dec'])
    sc_hist = zeros(bsz, SC_WIDTH - 1, SC_DIM) if st_sconv is None else st_sconv
    y_sc, sc_tails = _sconv(proj, p['sconv_w'][0], _hist8(sc_hist), seq_len=length, tm=t['tm_sc'])
    sconv_new = _tails(sc_tails, bsz, length, t['tm_sc'], SC_WIDTH - 1)
    cd_out = wb['cd_out']
    xf = _proj_residual(xf, y_fox, y_sc, cd_out[:FOX_WIDTH], cd_out[FOX_WIDTH:], tm=t['tm_out'], tn=t['tn_out'])

    ffn_hist1 = zeros(bsz, FFN_CONV - 1, D_FF) if st_ffn is None else st_ffn[1]
    xf, tails = _conv_ffn(xf, p['ffn_norm_w'][1], wb['ffn_gate'][1], wb['ffn_up'][1], wb['ffn_down'][1],
                          p['ffn_conv_w'][1], p['ffn_conv_b'][1], _hist8(ffn_hist1), p['final_norm_w'],
                          seq_len=length, tm=t['tm_ffn'], tf=t['tf_ffn'], final=True)
    ffn_new.append(_tails(tails, bsz, length, t['tm_ffn'], FFN_CONV - 1))

    return (xf.reshape(bsz, length, d), ret_new[None], ssd_new[None], ssd_conv_new[None], k_new[None],
            v_new[None], logf_new[None], sconv_new[None], jnp.stack(ffn_new))


def _largest_divisor(n, cap, multiple=1):
    best = None
    for cand in range(multiple, min(n, cap) + 1, multiple):
        if n % cand == 0:
            best = cand
    assert best is not None, (n, cap, multiple)
    return best


def _tiles(bsz, length, past=None):
    m = bsz * length
    seq_tile = lambda cap: _largest_divisor(length, cap, SUBLANES)
    row_tile = lambda cap: (_largest_divisor(length, cap, SUBLANES) if length >= cap
                            else _largest_divisor(m, cap, length))
    t = dict(
        tm_proj=row_tile(1024), tn_ab=512, tn_cd=1024,
        tm_out=row_tile(1024), tn_out=1024,
        tm_ffn=row_tile(1024), tf_ffn=512,
        tm_sc=row_tile(512),
        c_ret=seq_tile(256), c_ssd=seq_tile(64),
    )
    if past is None:
        t['t_fox'] = seq_tile(1024)
        t['ts_fox'] = _largest_divisor(t['t_fox'], 256, LANES)
        t['t_prep'] = seq_tile(512)
    else:
        t['tk_dec'] = _largest_divisor(past, 1024, LANES)
        t['c_cum'] = _largest_divisor(past, 256, LANES)
    return t


def kernel(x_prompt, x_sample, state_ret, state_ssd, state_ssd_conv, cache_fox_k, cache_fox_v, cache_fox_logf, state_sconv, state_ffn_conv, ab_norm_w, ab_w_in, ret_norm_w, ssd_conv_w, ssd_conv_b, ssd_dt_bias, ssd_A_log, ssd_D, ssd_norm_w, ab_w_out, cd_norm_w, cd_w_in, fox_f_bias, sconv_w, cd_w_out, ffn_norm_w, ffn_w_gate, ffn_w_up, ffn_conv_w, ffn_conv_b, ffn_w_down, final_norm_w):
    p = dict(ab_norm_w=ab_norm_w, ab_w_in=ab_w_in, ret_norm_w=ret_norm_w, ssd_conv_w=ssd_conv_w,
             ssd_conv_b=ssd_conv_b, ssd_dt_bias=ssd_dt_bias, ssd_A_log=ssd_A_log, ssd_D=ssd_D,
             ssd_norm_w=ssd_norm_w, ab_w_out=ab_w_out, cd_norm_w=cd_norm_w, cd_w_in=cd_w_in,
             fox_f_bias=fox_f_bias, sconv_w=sconv_w, cd_w_out=cd_w_out, ffn_norm_w=ffn_norm_w,
             ffn_w_gate=ffn_w_gate, ffn_w_up=ffn_w_up, ffn_conv_w=ffn_conv_w, ffn_conv_b=ffn_conv_b,
             ffn_w_down=ffn_w_down, final_norm_w=final_norm_w)
    assert x_prompt.shape[-1] == D_MODEL and ab_w_in.shape == (1, D_MODEL, AB_MAIN + SSD_HEADS)
    assert cd_w_in.shape == (1, D_MODEL, CD_MAIN + FOX_HEADS) and ffn_w_gate.shape == (2, D_MODEL, D_FF)
    wb = _prep_weights(p)
    bp, lp_, _ = x_prompt.shape
    bs, ls, _ = x_sample.shape
    past = cache_fox_k.shape[2]
    (y_prompt, p_ret, p_ssd, p_ssd_conv, p_fox_k, p_fox_v, p_fox_logf, p_sconv, p_ffn_conv) = _trunk(
        x_prompt, 0, None, None, None, None, None, None, None, None, p, wb, _tiles(bp, lp_))
    (y_sample, s_ret, s_ssd, s_ssd_conv, s_fox_k, s_fox_v, s_fox_logf, s_sconv, s_ffn_conv) = _trunk(
        x_sample, past, state_ret[0], state_ssd[0], state_ssd_conv[0], cache_fox_k[0], cache_fox_v[0],
        cache_fox_logf[0], state_sconv[0], state_ffn_conv, p, wb, _tiles(bs, ls, past))
    return (y_prompt, y_sample, p_ret, s_ret, p_ssd, s_ssd, p_ssd_conv, s_ssd_conv, p_fox_k, s_fox_k,
            p_fox_v, s_fox_v, p_fox_logf, s_fox_logf, p_sconv, s_sconv, p_ffn_conv, s_ffn_conv)
```

```python
import functools
import math

import numpy as np
import jax
import jax.numpy as jnp
from jax import lax
from jax.experimental import pallas as pl
from jax.experimental.pallas import tpu as pltpu

F32 = jnp.float32
BF16 = jnp.bfloat16
EPS = 1e-6
ROPE_BASE = 10000.0
NEG_INF = float("-inf")

D_MODEL = 2048
RET_HEADS, RET_DK, RET_DV = 4, 128, 256
SSD_DINNER, SSD_HEADDIM, SSD_HEADS, SSD_GROUPS, SSD_DSTATE, SSD_CONV = 1024, 64, 16, 2, 128, 4
SSD_CONV_DIM = SSD_DINNER + 2 * SSD_GROUPS * SSD_DSTATE
FOX_HEADS, FOX_HEAD_DIM = 8, 128
FOX_WIDTH = FOX_HEADS * FOX_HEAD_DIM
SC_DIM, SC_WIDTH = 1024, 3
D_FF, FFN_CONV = 5632, 3
AB_MAIN = 2 * RET_HEADS * RET_DK + 2 * RET_HEADS * RET_DV + SSD_DINNER + SSD_CONV_DIM
AB_PAD = AB_MAIN + 128
CD_MAIN = 3 * FOX_WIDTH + 3 * SC_DIM
CD_PAD = CD_MAIN + 128

LANES = 128
SUBLANES = 8
VMEM_LIMIT = 52 * 1024 * 1024


def _cparams(n_axes):
    return pltpu.CompilerParams(dimension_semantics=("arbitrary",) * n_axes,
                                vmem_limit_bytes=VMEM_LIMIT)


def _rms(xf, w):
    return xf * lax.rsqrt(jnp.mean(xf * xf, axis=-1, keepdims=True) + EPS) * w


def _softplus(x):
    return jnp.maximum(x, 0.0) + jnp.log1p(jnp.exp(-jnp.abs(x)))


def _split3(x):
    hi = x.astype(BF16)
    r1 = x - hi.astype(F32)
    mid = r1.astype(BF16)
    lo = (r1 - mid.astype(F32)).astype(BF16)
    return hi, mid, lo


def _widen(x, n):
    return x[:, 0:n] if n <= LANES else jnp.concatenate([x] * (n // LANES), axis=1)


def _dot(a, b):
    return jnp.dot(a, b, preferred_element_type=F32)


def _dot_nt(a, b):
    return lax.dot_general(a, b, (((1,), (1,)), ((), ())), preferred_element_type=F32)


def _dot_tn(a, b):
    return lax.dot_general(a, b, (((0,), (0,)), ((), ())), preferred_element_type=F32)


def _exact_lhs_dot(m_bf16, x):
    hi, mid, lo = _split3(x)
    return _dot(m_bf16, hi) + _dot(m_bf16, mid) + _dot(m_bf16, lo)


def _exact_rhs_dot(x, m_bf16):
    hi, mid, lo = _split3(x)
    return _dot(hi, m_bf16) + _dot(mid, m_bf16) + _dot(lo, m_bf16)


def _conv_rows(ext_ref, x, prev8, w_ref, width, rows):
    ext_ref[0:SUBLANES, :] = prev8
    ext_ref[SUBLANES:SUBLANES + rows, :] = x
    out = None
    for j in range(width):
        off = SUBLANES - (width - 1) + j
        term = ext_ref[off:off + rows, :] * w_ref[j:j + 1, :]
        out = term if out is None else out + term
    return out


def _seq_tiling(seq_len, tile_rows):
    if seq_len >= tile_rows:
        assert seq_len % tile_rows == 0
        return tile_rows, 1, seq_len // tile_rows
    assert tile_rows % seq_len == 0 and seq_len % SUBLANES == 0
    return seq_len, tile_rows // seq_len, 1


def _norm_matmul_kernel(x_ref, nw_ref, w_ref, ws_ref, o_ref, os_ref, h_ref):
    @pl.when(pl.program_id(1) == 0)
    def _():
        h_ref[...] = _rms(x_ref[...], nw_ref[...]).astype(BF16)
        os_ref[...] = _dot(h_ref[...], ws_ref[...])

    o_ref[...] = _dot(h_ref[...], w_ref[...])


def _norm_matmul(x, norm_w, w, w_small, *, tm, tn):
    m, d = x.shape
    n = w.shape[1]
    assert m % tm == 0 and n % tn == 0 and w_small.shape == (d, LANES)
    return pl.pallas_call(
        _norm_matmul_kernel,
        grid=(m // tm, n // tn),
        in_specs=[pl.BlockSpec((tm, d), lambda i, j: (i, 0)),
                  pl.BlockSpec((1, d), lambda i, j: (0, 0)),
                  pl.BlockSpec((d, tn), lambda i, j: (0, j)),
                  pl.BlockSpec((d, LANES), lambda i, j: (0, 0))],
        out_specs=[pl.BlockSpec((tm, tn), lambda i, j: (i, j)),
                   pl.BlockSpec((tm, LANES), lambda i, j: (i, 0))],
        out_shape=[jax.ShapeDtypeStruct((m, n), F32),
                   jax.ShapeDtypeStruct((m, LANES), F32)],
        scratch_shapes=[pltpu.VMEM((tm, d), BF16)],
        compiler_params=_cparams(2),
        name="norm_in_proj",
    )(x, norm_w.reshape(1, d), w, w_small)


def _proj_res_kernel(x_ref, a_ref, b_ref, wa_ref, wb_ref, o_ref):
    acc = _dot(a_ref[...], wa_ref[...])
    acc = acc + _dot(b_ref[...], wb_ref[...])
    o_ref[...] = x_ref[...] + acc


def _proj_residual(x, a, b, wa, wb, *, tm, tn):
    m, d = x.shape
    ka, kb = a.shape[1], b.shape[1]
    assert m % tm == 0 and d % tn == 0
    return pl.pallas_call(
        _proj_res_kernel,
        grid=(m // tm, d // tn),
        in_specs=[pl.BlockSpec((tm, tn), lambda i, j: (i, j)),
                  pl.BlockSpec((tm, ka), lambda i, j: (i, 0)),
                  pl.BlockSpec((tm, kb), lambda i, j: (i, 0)),
                  pl.BlockSpec((ka, tn), lambda i, j: (0, j)),
                  pl.BlockSpec((kb, tn), lambda i, j: (0, j))],
        out_specs=pl.BlockSpec((tm, tn), lambda i, j: (i, j)),
        out_shape=jax.ShapeDtypeStruct((m, d), F32),
        compiler_params=_cparams(2),
        name="out_proj_residual",
    )(x, a, b, wa, wb)


def _ffn_kernel(x_ref, nw_ref, wg_ref, wu_ref, wd_ref, cw_ref, cb_ref, hist_ref, fw_ref,
                o_ref, tail_ref, h_ref, carry_ref, ext_ref, *, tm, rows, spt, tpb, nf, final):
    i = pl.program_id(0)
    f = pl.program_id(1)

    @pl.when(f == 0)
    def _():
        xf = x_ref[...]
        h_ref[...] = _rms(xf, nw_ref[...]).astype(BF16)
        o_ref[...] = xf

    h = h_ref[...]
    a = _dot(h, wg_ref[...])
    u = _dot(h, wu_ref[...])
    first = (i % tpb) == 0
    convs = []
    for s in range(spt):
        a_s = a[s * rows:(s + 1) * rows]
        if tpb == 1:
            prev = hist_ref[s]
        else:
            prev = jnp.where(first, hist_ref[s], carry_ref[f])
        convs.append(_conv_rows(ext_ref, a_s, prev, cw_ref, FFN_CONV, rows))
        tail_ref[s] = a_s[rows - SUBLANES:rows]
    if tpb > 1:
        carry_ref[f] = a[tm - SUBLANES:tm]
    conv = convs[0] if spt == 1 else jnp.concatenate(convs, axis=0)
    act = (jax.nn.silu(conv + cb_ref[...]) * u).astype(BF16)
    o_ref[...] += _dot(act, wd_ref[...])
    if final:
        @pl.when(f == nf - 1)
        def _():
            o_ref[...] = _rms(o_ref[...], fw_ref[...])


def _conv_ffn(x, norm_w, wg, wu, wd, conv_w, conv_b, hist8, final_w, *, seq_len, tm, tf, final):
    m, d = x.shape
    ff = wg.shape[1]
    assert m % tm == 0 and ff % tf == 0
    rows, spt, tpb = _seq_tiling(seq_len, tm)
    nm, nf = m // tm, ff // tf
    hist_map = (lambda i, f: (i // tpb, 0, f)) if spt == 1 else (lambda i, f: (i, 0, f))
    kern = functools.partial(_ffn_kernel, tm=tm, rows=rows, spt=spt, tpb=tpb, nf=nf, final=final)
    out, tails = pl.pallas_call(
        kern,
        grid=(nm, nf),
        in_specs=[pl.BlockSpec((tm, d), lambda i, f: (i, 0)),
                  pl.BlockSpec((1, d), lambda i, f: (0, 0)),
                  pl.BlockSpec((d, tf), lambda i, f: (0, f)),
                  pl.BlockSpec((d, tf), lambda i, f: (0, f)),
                  pl.BlockSpec((tf, d), lambda i, f: (f, 0)),
                  pl.BlockSpec((FFN_CONV, tf), lambda i, f: (0, f)),
                  pl.BlockSpec((1, tf), lambda i, f: (0, f)),
                  pl.BlockSpec((spt, SUBLANES, tf), hist_map),
                  pl.BlockSpec((1, d), lambda i, f: (0, 0))],
        out_specs=[pl.BlockSpec((tm, d), lambda i, f: (i, 0)),
                   pl.BlockSpec((spt, SUBLANES, tf), lambda i, f: (i, 0, f))],
        out_shape=[jax.ShapeDtypeStruct((m, d), F32),
                   jax.ShapeDtypeStruct((nm * spt, SUBLANES, ff), F32)],
        scratch_shapes=[pltpu.VMEM((tm, d), BF16),
                        pltpu.VMEM((nf, SUBLANES, tf), F32),
                        pltpu.VMEM((rows + SUBLANES, tf), F32)],
        compiler_params=_cparams(2),
        name="conv_ffn",
    )(x, norm_w.reshape(1, d), wg, wu, wd, conv_w, conv_b.reshape(1, ff), hist8, final_w.reshape(1, d))
    return out, tails


def _retention_kernel(q_ref, k_ref, v_ref, g_ref, cos_ref, sin_ref, st_ref, nw_ref,
                      y_ref, so_ref, *, c):
    ci = pl.program_id(1)

    @pl.when(ci == 0)
    def _():
        so_ref[...] = st_ref[...]

    cos = cos_ref[...]
    sin = sin_ref[...]
    ii = lax.broadcasted_iota(jnp.int32, (c, c), 0)
    jj = lax.broadcasted_iota(jnp.int32, (c, c), 1)
    diff = (ii - jj).astype(F32)
    causal = ii >= jj
    ridx = lax.broadcasted_iota(jnp.int32, (c, 1), 0).astype(F32)
    for h in range(RET_HEADS):
        lg = math.log1p(-(2.0 ** (-5.0 - h)))
        q = q_ref[:, h * RET_DK:(h + 1) * RET_DK]
        k = k_ref[:, h * RET_DK:(h + 1) * RET_DK]
        v = v_ref[:, h * RET_DV:(h + 1) * RET_DV]
        qr = q * cos + pltpu.roll(q, RET_DK // 2, 1) * sin
        kr = (k * cos + pltpu.roll(k, RET_DK // 2, 1) * sin) * (RET_DK ** -0.5)
        qb = qr.astype(BF16)
        kb = kr.astype(BF16)
        vb = v.astype(BF16)
        decay = jnp.exp(jnp.where(causal, diff * lg, NEG_INF))
        inner = jnp.exp((ridx + 1.0) * lg)
        sdecay = jnp.exp((c - 1.0 - ridx) * lg)
        s = so_ref[0, h]
        scores = _dot_nt(qb, kb) * decay
        y = _dot(scores.astype(BF16), vb)
        y = y + _dot(qb, s.astype(BF16)) * inner
        kd = (kr * sdecay).astype(BF16)
        so_ref[0, h] = math.exp(c * lg) * s + _dot_tn(kd, vb)
        mu = jnp.mean(y, axis=-1, keepdims=True)
        yc = y - mu
        var = jnp.mean(yc * yc, axis=-1, keepdims=True)
        yn = yc * lax.rsqrt(var + EPS) * nw_ref[:, h * RET_DV:(h + 1) * RET_DV]
        g = g_ref[:, h * RET_DV:(h + 1) * RET_DV]
        y_ref[:, h * RET_DV:(h + 1) * RET_DV] = (jax.nn.silu(g) * yn).astype(BF16)


def _retention(proj, cosf, sinf, state, norm_w, *, bsz, seq_len, c):
    m = proj.shape[0]
    nc = seq_len // c
    assert seq_len % c == 0
    qk_w = RET_HEADS * RET_DK
    v_w = RET_HEADS * RET_DV
    row = lambda b, ci: b * nc + ci
    y, s_new = pl.pallas_call(
        functools.partial(_retention_kernel, c=c),
        grid=(bsz, nc),
        in_specs=[pl.BlockSpec((c, qk_w), lambda b, ci: (row(b, ci), 0)),
                  pl.BlockSpec((c, qk_w), lambda b, ci: (row(b, ci), 1)),
                  pl.BlockSpec((c, v_w), lambda b, ci: (row(b, ci), 1)),
                  pl.BlockSpec((c, v_w), lambda b, ci: (row(b, ci), 2)),
                  pl.BlockSpec((c, RET_DK), lambda b, ci: (ci, 0)),
                  pl.BlockSpec((c, RET_DK), lambda b, ci: (ci, 0)),
                  pl.BlockSpec((1, RET_HEADS, RET_DK, RET_DV), lambda b, ci: (b, 0, 0, 0)),
                  pl.BlockSpec((1, v_w), lambda b, ci: (0, 0))],
        out_specs=[pl.BlockSpec((c, v_w), lambda b, ci: (row(b, ci), 0)),
                   pl.BlockSpec((1, RET_HEADS, RET_DK, RET_DV), lambda b, ci: (b, 0, 0, 0))],
        out_shape=[jax.ShapeDtypeStruct((m, v_w), BF16),
                   jax.ShapeDtypeStruct(state.shape, F32)],
        compiler_params=_cparams(2),
        name="retention",
    )(proj, proj, proj, proj, cosf, sinf, state, norm_w.reshape(1, v_w))
    return y, s_new


def _ssd_kernel(z_ref, xs_ref, bc_ref, dt_ref, hx_ref, hbc_ref, st_ref,
                cwx_ref, cwbc_ref, cbx_ref, cbbc_ref, dtb_ref, alog_ref, dsk_ref, nw_ref,
                tri_ref, exp_ref,
                y_ref, so_ref,
                st_scr, cx_scr, cbc_scr, extx_scr, extbc_scr, yh_scr, *, c, nc):
    ci = pl.program_id(1)
    gw = SSD_DINNER // SSD_GROUPS
    hpg = SSD_HEADS // SSD_GROUPS

    @pl.when(ci == 0)
    def _():
        st_scr[...] = st_ref[0].T
        cx_scr[...] = hx_ref[0]
        cbc_scr[...] = hbc_ref[0]

    xs_raw = xs_ref[...]
    bc_raw = bc_ref[...]
    xs = jax.nn.silu(_conv_rows(extx_scr, xs_raw, cx_scr[...], cwx_ref, SSD_CONV, c) + cbx_ref[...])
    bcm = jax.nn.silu(_conv_rows(extbc_scr, bc_raw, cbc_scr[...], cwbc_ref, SSD_CONV, c) + cbbc_ref[...])
    cx_scr[...] = xs_raw[c - SUBLANES:c]
    cbc_scr[...] = bc_raw[c - SUBLANES:c]

    tri = tri_ref[...]
    expand = exp_ref[...]
    dt = _softplus(dt_ref[...] + dtb_ref[...])
    a = -jnp.exp(alog_ref[...])
    acs = _exact_lhs_dot(tri, dt * a)
    acs_t = acs.T
    acs_last = acs[c - 1:c, :]
    exp_acs = jnp.exp(acs)
    to_end = jnp.exp(acs_last - acs)
    dt_e = _exact_rhs_dot(dt, expand)
    to_end_e = _exact_rhs_dot(to_end, expand)
    exp_acs_e = _exact_rhs_dot(exp_acs, expand)
    chunk_dec_e = _exact_rhs_dot(jnp.exp(acs_last), expand)

    xdt = xs * dt_e
    xdt_b = xdt.astype(BF16)
    xend_b = (xdt * to_end_e).astype(BF16)
    ii = lax.broadcasted_iota(jnp.int32, (c, c), 0)
    jj = lax.broadcasted_iota(jnp.int32, (c, c), 1)
    causal = ii >= jj
    nb = SSD_GROUPS * SSD_DSTATE
    for g in range(SSD_GROUPS):
        b_g = bcm[:, g * SSD_DSTATE:(g + 1) * SSD_DSTATE].astype(BF16)
        c_g = bcm[:, nb + g * SSD_DSTATE:nb + (g + 1) * SSD_DSTATE].astype(BF16)
        cb = _dot_nt(c_g, b_g)
        s_g = st_scr[:, g * gw:(g + 1) * gw]
        y_state = _dot(c_g, s_g.astype(BF16)) * exp_acs_e[:, g * gw:(g + 1) * gw]
        for r in range(hpg):
            hh = g * hpg + r
            seg = acs[:, hh:hh + 1] - acs_t[hh:hh + 1, :]
            lmat = jnp.exp(jnp.where(causal, seg, NEG_INF))
            mm = (cb * lmat).astype(BF16)
            lo = hh * SSD_HEADDIM
            yh_scr[:, lo:lo + SSD_HEADDIM] = (
                _dot(mm, xdt_b[:, lo:lo + SSD_HEADDIM]) + y_state[:, r * SSD_HEADDIM:(r + 1) * SSD_HEADDIM])
        upd = _dot_tn(b_g, xend_b[:, g * gw:(g + 1) * gw])
        st_scr[:, g * gw:(g + 1) * gw] = chunk_dec_e[:, g * gw:(g + 1) * gw] * s_g + upd

    y = yh_scr[...] + dsk_ref[...] * xs
    z = z_ref[...]
    y_ref[...] = _rms(y * jax.nn.silu(z), nw_ref[...]).astype(BF16)

    @pl.when(ci == nc - 1)
    def _():
        so_ref[0] = st_scr[...].T


def _ssd(proj, dt_proj, hist8, state, conv_w, conv_b, dt_bias, a_log, d_skip, norm_w, *, bsz, seq_len, c):
    m = proj.shape[0]
    nc = seq_len // c
    assert seq_len % c == 0
    row = lambda b, ci: b * nc + ci
    const2 = lambda b, ci: (0, 0)
    di, bcw = SSD_DINNER, 2 * SSD_GROUPS * SSD_DSTATE
    tri = jnp.asarray(np.tril(np.ones((c, c), np.float32)), BF16)
    expand = np.zeros((LANES, di), np.float32)
    for h in range(SSD_HEADS):
        expand[h, h * SSD_HEADDIM:(h + 1) * SSD_HEADDIM] = 1.0
    expand = jnp.asarray(expand, BF16)
    pad_row = lambda v: jnp.pad(v.astype(F32), (0, LANES - v.shape[0])).reshape(1, LANES)
    st2 = state.reshape(bsz, di, SSD_DSTATE)
    y, s_new = pl.pallas_call(
        functools.partial(_ssd_kernel, c=c, nc=nc),
        grid=(bsz, nc),
        in_specs=[pl.BlockSpec((c, di), lambda b, ci: (row(b, ci), 3)),
                  pl.BlockSpec((c, di), lambda b, ci: (row(b, ci), 4)),
                  pl.BlockSpec((c, bcw), lambda b, ci: (row(b, ci), 10)),
                  pl.BlockSpec((c, LANES), lambda b, ci: (row(b, ci), 0)),
                  pl.BlockSpec((1, SUBLANES, di), lambda b, ci: (b, 0, 0)),
                  pl.BlockSpec((1, SUBLANES, bcw), lambda b, ci: (b, 0, 2)),
                  pl.BlockSpec((1, di, SSD_DSTATE), lambda b, ci: (b, 0, 0)),
                  pl.BlockSpec((SSD_CONV, di), const2),
                  pl.BlockSpec((SSD_CONV, bcw), lambda b, ci: (0, 2)),
                  pl.BlockSpec((1, di), const2),
                  pl.BlockSpec((1, bcw), lambda b, ci: (0, 2)),
                  pl.BlockSpec((1, LANES), const2),
                  pl.BlockSpec((1, LANES), const2),
                  pl.BlockSpec((1, di), const2),
                  pl.BlockSpec((1, di), const2),
                  pl.BlockSpec((c, c), const2),
                  pl.BlockSpec((LANES, di), const2)],
        out_specs=[pl.BlockSpec((c, di), lambda b, ci: (row(b, ci), 0)),
                   pl.BlockSpec((1, di, SSD_DSTATE), lambda b, ci: (b, 0, 0))],
        out_shape=[jax.ShapeDtypeStruct((m, di), BF16),
                   jax.ShapeDtypeStruct(st2.shape, F32)],
        scratch_shapes=[pltpu.VMEM((SSD_DSTATE, di), F32),
                        pltpu.VMEM((SUBLANES, di), F32),
                        pltpu.VMEM((SUBLANES, bcw), F32),
                        pltpu.VMEM((c + SUBLANES, di), F32),
                        pltpu.VMEM((c + SUBLANES, bcw), F32),
                        pltpu.VMEM((c, di), F32)],
        compiler_params=_cparams(2),
        name="ssd",
    )(proj, proj, proj, dt_proj, hist8, hist8, st2,
      conv_w, conv_w, conv_b.reshape(1, -1), conv_b.reshape(1, -1),
      pad_row(dt_bias), pad_row(a_log), jnp.repeat(d_skip.astype(F32), SSD_HEADDIM).reshape(1, di),
      norm_w.reshape(1, di), tri, expand)
    return y, s_new.reshape(state.shape)


def _decode_cum_kernel(lfc_ref, fl_ref, b_ref, tri_ref, cumt_c_ref, lfn_ref, cumn_ref, cumt_n_ref,
                       carry_ref, *, c, ncb):
    j = pl.program_id(0)

    @pl.when(j == 0)
    def _():
        carry_ref[...] = jnp.zeros_like(carry_ref)

    @pl.when(j < ncb)
    def _():
        cum = _exact_lhs_dot(tri_ref[...], lfc_ref[...]) + carry_ref[...]
        carry_ref[...] = cum[c - 1:c, :]
        cumt_c_ref[...] = cum.T

    @pl.when(j == ncb)
    def _():
        lf = -_softplus(-(fl_ref[...] + b_ref[...]))
        lfn_ref[...] = lf
        cum = _exact_lhs_dot(tri_ref[0:LANES, 0:LANES], lf) + carry_ref[...]
        cumn_ref[...] = cum
        cumt_n_ref[...] = cum.T


def _decode_cum(cache_lf, fl_new, bias, *, c):
    past = cache_lf.shape[0]
    assert past % c == 0 and c % LANES == 0 and fl_new.shape == (LANES, LANES)
    ncb = past // c
    tri = jnp.asarray(np.tril(np.ones((c, c), np.float32)), BF16)
    blk = lambda j: jnp.minimum(j, ncb - 1)
    sq = jax.ShapeDtypeStruct((LANES, LANES), F32)
    return pl.pallas_call(
        functools.partial(_decode_cum_kernel, c=c, ncb=ncb),
        grid=(ncb + 1,),
        in_specs=[pl.BlockSpec((c, LANES), lambda j: (blk(j), 0)),
                  pl.BlockSpec((LANES, LANES), lambda j: (0, 0)),
                  pl.BlockSpec((1, LANES), lambda j: (0, 0)),
                  pl.BlockSpec((c, c), lambda j: (0, 0))],
        out_specs=[pl.BlockSpec((LANES, c), lambda j: (0, blk(j))),
                   pl.BlockSpec((LANES, LANES), lambda j: (0, 0)),
                   pl.BlockSpec((LANES, LANES), lambda j: (0, 0)),
                   pl.BlockSpec((LANES, LANES), lambda j: (0, 0))],
        out_shape=[jax.ShapeDtypeStruct((LANES, past), F32), sq, sq, sq],
        scratch_shapes=[pltpu.VMEM((1, LANES), F32)],
        compiler_params=_cparams(1),
        name="decode_logf_cumsum",
    )(cache_lf, fl_new, bias, tri)


FOX_AUG = 2 * FOX_HEAD_DIM
N_BIAS_PIECES = 3


def _fox_prep_kernel(q_ref, k_ref, v_ref, fl_ref, fb_ref, tri_ref, place_ref, ones_ref,
                     qa_ref, ka_ref, vb_ref, k32_ref, v32_ref, lf_ref, carry_ref, *, tp):
    @pl.when(pl.program_id(1) == 0)
    def _():
        carry_ref[...] = jnp.zeros_like(carry_ref)

    lf = -_softplus(-(fl_ref[...] + fb_ref[...]))
    lf_ref[...] = lf
    cum = _exact_lhs_dot(tri_ref[...], lf) + carry_ref[...]
    carry_ref[...] = cum[tp - 1:tp, :]
    pieces = _split3(cum * (FOX_HEAD_DIM ** 0.5))
    n = N_BIAS_PIECES
    aug_q = ones_ref[0:1, :] + sum(_dot(pieces[r], place_ref[r]) for r in range(n))
    aug_k = ones_ref[1:2, :] - sum(_dot(pieces[r], place_ref[n + r]) for r in range(n))
    for h in range(FOX_HEADS):
        src = slice(h * FOX_HEAD_DIM, (h + 1) * FOX_HEAD_DIM)
        feat = slice(h * FOX_AUG, h * FOX_AUG + FOX_HEAD_DIM)
        bias = slice(h * FOX_AUG + FOX_HEAD_DIM, (h + 1) * FOX_AUG)
        qa_ref[:, feat] = q_ref[:, src].astype(BF16)
        qa_ref[:, bias] = aug_q[:, src].astype(BF16)
        ka_ref[:, feat] = k_ref[:, src].astype(BF16)
        ka_ref[:, bias] = aug_k[:, src].astype(BF16)
    k = k_ref[...]
    v = v_ref[...]
    k32_ref[...] = k
    v32_ref[...] = v
    vb_ref[...] = v.astype(BF16)


def _fox_prep(proj, fl_proj, f_bias, *, bsz, seq_len, tp):
    m = proj.shape[0]
    nt = seq_len // tp
    assert seq_len % tp == 0
    w = FOX_WIDTH
    tri = jnp.asarray(np.tril(np.ones((tp, tp), np.float32)), BF16)
    n = N_BIAS_PIECES
    place = np.zeros((2 * n, LANES, w), np.float32)
    ones = np.zeros((SUBLANES, w), np.float32)
    for h in range(FOX_HEADS):
        for r in range(2 * n):
            place[r, h, h * FOX_HEAD_DIM + r] = 1.0
        ones[0, h * FOX_HEAD_DIM + n:h * FOX_HEAD_DIM + 2 * n] = 1.0
        ones[1, h * FOX_HEAD_DIM:h * FOX_HEAD_DIM + n] = 1.0
    row = lambda b, ti: (b * nt + ti, 0)
    const2 = lambda b, ti: (0, 0)
    return pl.pallas_call(
        functools.partial(_fox_prep_kernel, tp=tp),
        grid=(bsz, nt),
        in_specs=[pl.BlockSpec((tp, w), lambda b, ti: (b * nt + ti, 0)),
                  pl.BlockSpec((tp, w), lambda b, ti: (b * nt + ti, 1)),
                  pl.BlockSpec((tp, w), lambda b, ti: (b * nt + ti, 2)),
                  pl.BlockSpec((tp, LANES), row),
                  pl.BlockSpec((1, LANES), const2),
                  pl.BlockSpec((tp, tp), const2),
                  pl.BlockSpec((2 * n, LANES, w), lambda b, ti: (0, 0, 0)),
                  pl.BlockSpec((SUBLANES, w), const2)],
        out_specs=[pl.BlockSpec((tp, FOX_HEADS * FOX_AUG), row),
                   pl.BlockSpec((tp, FOX_HEADS * FOX_AUG), row),
                   pl.BlockSpec((tp, w), row),
                   pl.BlockSpec((tp, w), row),
                   pl.BlockSpec((tp, w), row),
                   pl.BlockSpec((tp, LANES), row)],
        out_shape=[jax.ShapeDtypeStruct((m, FOX_HEADS * FOX_AUG), BF16),
                   jax.ShapeDtypeStruct((m, FOX_HEADS * FOX_AUG), BF16),
                   jax.ShapeDtypeStruct((m, w), BF16),
                   jax.ShapeDtypeStruct((m, w), F32),
                   jax.ShapeDtypeStruct((m, w), F32),
                   jax.ShapeDtypeStruct((m, LANES), F32)],
        scratch_shapes=[pltpu.VMEM((1, LANES), F32)],
        compiler_params=_cparams(2),
        name="fox_prep",
    )(proj, proj, proj, fl_proj, f_bias, tri, jnp.asarray(place, BF16), jnp.asarray(ones, F32))


def _fox_kernel(qi_ref, ki_ref, q_ref, k_ref, v_ref, o_ref, m_ref, l_ref, acc_ref, *, t, ts):
    step = pl.program_id(2)
    qi = qi_ref[step]
    ki = ki_ref[step]
    to_log2 = (FOX_HEAD_DIM ** -0.5) * math.log2(math.e)

    @pl.when(ki == 0)
    def _():
        m_ref[...] = jnp.full_like(m_ref, NEG_INF)
        l_ref[...] = jnp.zeros_like(l_ref)
        acc_ref[...] = jnp.zeros_like(acc_ref)

    def update(diagonal):
        for r in range(t // ts):
            rows = slice(r * ts, (r + 1) * ts)
            nk = (r + 1) * ts if diagonal else t
            s = _dot_nt(q_ref[rows, :], k_ref[0:nk, :])
            if diagonal:
                ri = lax.broadcasted_iota(jnp.int32, (ts, nk), 0) + r * ts
                ci = lax.broadcasted_iota(jnp.int32, (ts, nk), 1)
                s = jnp.where(ci <= ri, s, NEG_INF)
            m_old = m_ref[rows, :]
            m_new = jnp.maximum(m_old, jnp.max(s, axis=1, keepdims=True))
            alpha = jnp.exp2((m_old - m_new) * to_log2)
            p = jnp.exp2((s - _widen(m_new, nk)) * to_log2)
            l_ref[rows, :] = alpha * l_ref[rows, :] + jnp.sum(p, axis=1, keepdims=True)
            acc_ref[rows, :] = alpha * acc_ref[rows, :] + _dot(p.astype(BF16), v_ref[0:nk, :])
            m_ref[rows, :] = m_new

    @pl.when(ki < qi)
    def _():
        update(False)

    @pl.when(ki == qi)
    def _():
        update(True)
        o_ref[...] = (acc_ref[...] / l_ref[...]).astype(BF16)


def _fox_prompt(qa, ka, vb, *, bsz, seq_len, t, ts):
    m = qa.shape[0]
    nq = seq_len // t
    assert seq_len % t == 0 and t % ts == 0
    pairs = [(qi, ki) for qi in range(nq) for ki in range(qi + 1)]
    qi_tab = jnp.asarray([p[0] for p in pairs], jnp.int32)
    ki_tab = jnp.asarray([p[1] for p in pairs], jnp.int32)
    grid_spec = pltpu.PrefetchScalarGridSpec(
        num_scalar_prefetch=2,
        grid=(bsz, FOX_HEADS, len(pairs)),
        in_specs=[pl.BlockSpec((t, FOX_AUG), lambda b, h, s, qi, ki: (b * nq + qi[s], h)),
                  pl.BlockSpec((t, FOX_AUG), lambda b, h, s, qi, ki: (b * nq + ki[s], h)),
                  pl.BlockSpec((t, FOX_HEAD_DIM), lambda b, h, s, qi, ki: (b * nq + ki[s], h))],
        out_specs=pl.BlockSpec((t, FOX_HEAD_DIM), lambda b, h, s, qi, ki: (b * nq + qi[s], h)),
        scratch_shapes=[pltpu.VMEM((t, LANES), F32), pltpu.VMEM((t, LANES), F32),
                        pltpu.VMEM((t, FOX_HEAD_DIM), F32)],
    )
    return pl.pallas_call(
        functools.partial(_fox_kernel, t=t, ts=ts),
        grid_spec=grid_spec,
        out_shape=jax.ShapeDtypeStruct((m, FOX_WIDTH), BF16),
        compiler_params=_cparams(3),
        name="fox_attention",
    )(qi_tab, ki_tab, qa, ka, vb)


def _fox_decode_kernel(q_ref, kn_ref, vn_ref, kc_ref, vc_ref, cq_ref, ckc_ref, ckn_ref, o_ref,
                       m_ref, l_ref, acc_ref, *, lq, ncb):
    j = pl.program_id(1)
    nh, hd = FOX_HEADS, FOX_HEAD_DIM

    @pl.when(j == 0)
    def _():
        m_ref[...] = jnp.full_like(m_ref, NEG_INF)
        l_ref[...] = jnp.zeros_like(l_ref)
        acc_ref[...] = jnp.zeros_like(acc_ref)

    def attend(k_head, v_head, ck_head, causal):
        for h in range(nh):
            qh = q_ref[:, h * hd:(h + 1) * hd].astype(BF16)
            ck = ck_head(h)
            tk = ck.shape[1]
            s = _dot_nt(qh, k_head(h).astype(BF16)) * (hd ** -0.5)
            s = s + (_widen(cq_ref[h], tk) - ck)
            if causal:
                rows = lax.broadcasted_iota(jnp.int32, (lq, tk), 0)
                cols = lax.broadcasted_iota(jnp.int32, (lq, tk), 1)
                s = jnp.where(cols <= rows, s, NEG_INF)
            m_old = m_ref[h]
            m_new = jnp.maximum(m_old, jnp.max(s, axis=1, keepdims=True))
            alpha = jnp.exp(m_old - m_new)
            p = jnp.exp(s - _widen(m_new, tk))
            l_ref[h] = alpha * l_ref[h] + jnp.sum(p, axis=1, keepdims=True)
            acc_ref[h] = alpha * acc_ref[h] + _dot(p.astype(BF16), v_head(h).astype(BF16))
            m_ref[h] = m_new

    @pl.when(j < ncb)
    def _():
        k_hm = pltpu.einshape("mhd->hmd", kc_ref[0])
        v_hm = pltpu.einshape("mhd->hmd", vc_ref[0])
        attend(lambda h: k_hm[h], lambda h: v_hm[h], lambda h: ckc_ref[h], False)

    @pl.when(j == ncb)
    def _():
        attend(lambda h: kn_ref[:, h * hd:(h + 1) * hd], lambda h: vn_ref[:, h * hd:(h + 1) * hd],
               lambda h: ckn_ref[h][:, 0:lq], True)
        for h in range(nh):
            o_ref[:, h * hd:(h + 1) * hd] = (acc_ref[h] / l_ref[h]).astype(BF16)


def _fox_decode(proj, cache_k, cache_v, cq, ck_cache, ck_new, *, bsz, lq, tk):
    past = cache_k.shape[1]
    assert past % tk == 0 and tk % LANES == 0 and lq <= LANES
    ncb = past // tk
    nh, hd, w = FOX_HEADS, FOX_HEAD_DIM, FOX_WIDTH
    tile = lambda j: jnp.minimum(j, ncb - 1)
    cache_spec = pl.BlockSpec((1, tk, nh, hd), lambda b, j: (b, tile(j), 0, 0))
    return pl.pallas_call(
        functools.partial(_fox_decode_kernel, lq=lq, ncb=ncb),
        grid=(bsz, ncb + 1),
        in_specs=[pl.BlockSpec((lq, w), lambda b, j: (b, 0)),
                  pl.BlockSpec((lq, w), lambda b, j: (b, 1)),
                  pl.BlockSpec((lq, w), lambda b, j: (b, 2)),
                  cache_spec,
                  cache_spec,
                  pl.BlockSpec((nh, lq, LANES), lambda b, j: (b, 0, 0)),
                  pl.BlockSpec((nh, 1, tk), lambda b, j: (b, 0, tile(j))),
                  pl.BlockSpec((nh, 1, LANES), lambda b, j: (b, 0, 0))],
        out_specs=pl.BlockSpec((lq, w), lambda b, j: (b, 0)),
        out_shape=jax.ShapeDtypeStruct((bsz * lq, w), BF16),
        scratch_shapes=[pltpu.VMEM((nh, lq, LANES), F32), pltpu.VMEM((nh, lq, LANES), F32),
                        pltpu.VMEM((nh, lq, hd), F32)],
        compiler_params=_cparams(2),
        name="fox_decode",
    )(proj, proj, proj, cache_k, cache_v, cq, ck_cache, ck_new)


def _sconv_kernel(u_ref, bg_ref, cg_ref, cw_ref, hist_ref, y_ref, tail_ref, carry_ref, ext_ref,
                  *, tm, rows, spt, tpb):
    i = pl.program_id(0)
    w = cg_ref[...] * u_ref[...]
    first = (i % tpb) == 0
    for s in range(spt):
        w_s = w[s * rows:(s + 1) * rows]
        if tpb == 1:
            prev = hist_ref[s]
        else:
            prev = jnp.where(first, hist_ref[s], carry_ref[...])
        conv = _conv_rows(ext_ref, w_s, prev, cw_ref, SC_WIDTH, rows)
        y_ref[s * rows:(s + 1) * rows, :] = (bg_ref[s * rows:(s + 1) * rows, :] * conv).astype(BF16)
        tail_ref[s] = w_s[rows - SUBLANES:rows]
    if tpb > 1:
        carry_ref[...] = w[tm - SUBLANES:tm]


def _sconv(proj, conv_w, hist8, *, seq_len, tm):
    m = proj.shape[0]
    assert m % tm == 0
    rows, spt, tpb = _seq_tiling(seq_len, tm)
    nm = m // tm
    hist_map = (lambda i: (i // tpb, 0, 0)) if spt == 1 else (lambda i: (i, 0, 0))
    base = 3 * FOX_WIDTH // SC_DIM
    return pl.pallas_call(
        functools.partial(_sconv_kernel, tm=tm, rows=rows, spt=spt, tpb=tpb),
        grid=(nm,),
        in_specs=[pl.BlockSpec((tm, SC_DIM), lambda i: (i, base)),
                  pl.BlockSpec((tm, SC_DIM), lambda i: (i, base + 1)),
                  pl.BlockSpec((tm, SC_DIM), lambda i: (i, base + 2)),
                  pl.BlockSpec((SC_WIDTH, SC_DIM), lambda i: (0, 0)),
                  pl.BlockSpec((spt, SUBLANES, SC_DIM), hist_map)],
        out_specs=[pl.BlockSpec((tm, SC_DIM), lambda i: (i, 0)),
                   pl.BlockSpec((spt, SUBLANES, SC_DIM), lambda i: (i, 0, 0))],
        out_shape=[jax.ShapeDtypeStruct((m, SC_DIM), BF16),
                   jax.ShapeDtypeStruct((nm * spt, SUBLANES, SC_DIM), F32)],
        scratch_shapes=[pltpu.VMEM((SUBLANES, SC_DIM), F32),
                        pltpu.VMEM((rows + SUBLANES, SC_DIM), F32)],
        compiler_params=_cparams(1),
        name="gated_short_conv",
    )(proj, proj, proj, conv_w, hist8)


def _hist8(state):
    n, w1, c = state.shape
    return jnp.concatenate([jnp.zeros((n, SUBLANES - w1, c), F32), state.astype(F32)], axis=1)


def _tails(tails, n_seq, seq_len, tile_rows, keep):
    per_seq = max(1, seq_len // tile_rows)
    idx = (jnp.arange(n_seq) + 1) * per_seq - 1
    return tails[idx][:, SUBLANES - keep:, :]


def _rope_tables(pos0, length):
    half = RET_DK // 2
    inv = ROPE_BASE ** (-jnp.arange(half, dtype=F32) / half)
    ang = (pos0 + jnp.arange(length)).astype(F32)[:, None] * inv[None, :]
    cos, sin = jnp.cos(ang), jnp.sin(ang)
    return jnp.concatenate([cos, cos], axis=1), jnp.concatenate([-sin, sin], axis=1)


def _prep_weights(p):
    d = D_MODEL
    ab_in = p['ab_w_in'][0]
    cd_in = p['cd_w_in'][0]
    f0 = 3 * FOX_WIDTH
    pad_cols = lambda w: jnp.pad(w, ((0, 0), (0, LANES - w.shape[1]))).astype(BF16)
    return dict(
        ab_in=ab_in[:, :AB_MAIN].astype(BF16),
        ab_small=pad_cols(ab_in[:, AB_MAIN:]),
        cd_in=jnp.concatenate([cd_in[:, :f0], cd_in[:, f0 + FOX_HEADS:]], axis=1).astype(BF16),
        cd_small=pad_cols(cd_in[:, f0:f0 + FOX_HEADS]),
        ab_out=p['ab_w_out'][0].astype(BF16),
        cd_out=p['cd_w_out'][0].astype(BF16),
        ffn_gate=p['ffn_w_gate'].astype(BF16),
        ffn_up=p['ffn_w_up'].astype(BF16),
        ffn_down=p['ffn_w_down'].astype(BF16),
    )


def _trunk(x, pos0, st_ret, st_ssd, st_ssd_conv, c_k, c_v, c_logf, st_sconv, st_ffn, p, wb, t):
    bsz, length, d = x.shape
    m = bsz * length
    xf = x.reshape(m, d)
    zeros = lambda *shape: jnp.zeros(shape, F32)

    proj, dt_proj = _norm_matmul(xf, p['ab_norm_w'][0], wb['ab_in'], wb['ab_small'],
                                 tm=t['tm_proj'], tn=t['tn_ab'])
    cosf, sinf = _rope_tables(pos0, length)
    ret_state = zeros(bsz, RET_HEADS, RET_DK, RET_DV) if st_ret is None else st_ret
    y_ret, ret_new = _retention(proj, cosf, sinf, ret_state, p['ret_norm_w'][0],
                                bsz=bsz, seq_len=length, c=t['c_ret'])
    ssd_state = zeros(bsz, SSD_HEADS, SSD_HEADDIM, SSD_DSTATE) if st_ssd is None else st_ssd
    ssd_hist = zeros(bsz, SSD_CONV - 1, SSD_CONV_DIM) if st_ssd_conv is None else st_ssd_conv
    y_ssd, ssd_new = _ssd(proj, dt_proj, _hist8(ssd_hist), ssd_state, p['ssd_conv_w'][0], p['ssd_conv_b'][0],
                          p['ssd_dt_bias'][0], p['ssd_A_log'][0], p['ssd_D'][0], p['ssd_norm_w'][0],
                          bsz=bsz, seq_len=length, c=t['c_ssd'])
    xbc_lo = AB_MAIN - SSD_CONV_DIM
    ssd_conv_new = proj.reshape(bsz, length, -1)[:, length - (SSD_CONV - 1):, xbc_lo:AB_MAIN]
    ab_out = wb['ab_out']
    xf = _proj_residual(xf, y_ret, y_ssd, ab_out[:RET_HEADS * RET_DV], ab_out[RET_HEADS * RET_DV:],
                        tm=t['tm_out'], tn=t['tn_out'])

    ffn_new = []
    ffn_hist0 = zeros(bsz, FFN_CONV - 1, D_FF) if st_ffn is None else st_ffn[0]
    xf, tails = _conv_ffn(xf, p['ffn_norm_w'][0], wb['ffn_gate'][0], wb['ffn_up'][0], wb['ffn_down'][0],
                          p['ffn_conv_w'][0], p['ffn_conv_b'][0], _hist8(ffn_hist0), p['final_norm_w'],
                          seq_len=length, tm=t['tm_ffn'], tf=t['tf_ffn'], final=False)
    ffn_new.append(_tails(tails, bsz, length, t['tm_ffn'], FFN_CONV - 1))

    proj, fl_proj = _norm_matmul(xf, p['cd_norm_w'][0], wb['cd_in'], wb['cd_small'],
                                 tm=t['tm_proj'], tn=t['tn_cd'])
    f_bias = jnp.pad(p['fox_f_bias'][0].astype(F32), (0, LANES - FOX_HEADS)).reshape(1, LANES)
    head_shape = (bsz, length, FOX_HEADS, FOX_HEAD_DIM)
    if c_k is None:
        qa, ka, vb, k32, v32, logf = _fox_prep(proj, fl_proj, f_bias, bsz=bsz, seq_len=length, tp=t['t_prep'])
        y_fox = _fox_prompt(qa, ka, vb, bsz=bsz, seq_len=length, t=t['t_fox'], ts=t['ts_fox'])
        logf_new = logf.reshape(bsz, length, LANES)[:, :, :FOX_HEADS]
        k_new, v_new = k32.reshape(head_shape), v32.reshape(head_shape)
    else:
        proj3 = proj.reshape(bsz, length, -1)
        k_new = proj3[:, :, FOX_WIDTH:2 * FOX_WIDTH].reshape(head_shape)
        v_new = proj3[:, :, 2 * FOX_WIDTH:3 * FOX_WIDTH].reshape(head_shape)
        past = c_k.shape[1]
        pairs = bsz * FOX_HEADS
        assert pairs <= LANES and length <= LANES
        to_lanes = lambda a, rows: jnp.pad(jnp.swapaxes(a, 0, 1).reshape(a.shape[1], pairs),
                                           ((0, rows - a.shape[1]), (0, LANES - pairs)))
        from_lanes = lambda a: jnp.swapaxes(a[:length, :pairs].reshape(length, bsz, FOX_HEADS), 0, 1)
        cache_lf = to_lanes(c_logf.astype(F32), past)
        fl_rows = to_lanes(fl_proj.reshape(bsz, length, LANES)[:, :, :FOX_HEADS], LANES)
        bias_lanes = jnp.pad(jnp.tile(p['fox_f_bias'][0].astype(F32), bsz), (0, LANES - pairs)).reshape(1, LANES)
        cum_t_cache, lf_rows, cum_rows, cum_t_new = _decode_cum(cache_lf, fl_rows, bias_lanes, c=t['c_cum'])
        logf_new = from_lanes(lf_rows)
        cq = jnp.broadcast_to(cum_rows[:length, :pairs].T[:, :, None], (pairs, length, LANES))
        y_fox = _fox_decode(proj, c_k, c_v, cq, cum_t_cache[:pairs, None, :], cum_t_new[:pairs, None, :],
                            bsz=bsz, lq=length, tk=t['tk_dec'])
    sc_hist = zeros(bsz, SC_WIDTH - 1, SC_DIM) if st_sconv is None else st_sconv
    y_sc, sc_tails = _sconv(proj, p['sconv_w'][0], _hist8(sc_hist), seq_len=length, tm=t['tm_sc'])
    sconv_new = _tails(sc_tails, bsz, length, t['tm_sc'], SC_WIDTH - 1)
    cd_out = wb['cd_out']
    xf = _proj_residual(xf, y_fox, y_sc, cd_out[:FOX_WIDTH], cd_out[FOX_WIDTH:], tm=t['tm_out'], tn=t['tn_out'])

    ffn_hist1 = zeros(bsz, FFN_CONV - 1, D_FF) if st_ffn is None else st_ffn[1]
    xf, tails = _conv_ffn(xf, p['ffn_norm_w'][1], wb['ffn_gate'][1], wb['ffn_up'][1], wb['ffn_down'][1],
                          p['ffn_conv_w'][1], p['ffn_conv_b'][1], _hist8(ffn_hist1), p['final_norm_w'],
                          seq_len=length, tm=t['tm_ffn'], tf=t['tf_ffn'], final=True)
    ffn_new.append(_tails(tails, bsz, length, t['tm_ffn'], FFN_CONV - 1))

    return (xf.reshape(bsz, length, d), ret_new[None], ssd_new[None], ssd_conv_new[None], k_new[None],
            v_new[None], logf_new[None], sconv_new[None], jnp.stack(ffn_new))


def _largest_divisor(n, cap, multiple=1):
    best = None
    for cand in range(multiple, min(n, cap) + 1, multiple):
        if n % cand == 0:
            best = cand
    assert best is not None, (n, cap, multiple)
    return best


def _tiles(bsz, length, past=None):
    m = bsz * length
    seq_tile = lambda cap: _largest_divisor(length, cap, SUBLANES)
    row_tile = lambda cap: (_largest_divisor(length, cap, SUBLANES) if length >= cap
                            else _largest_divisor(m, cap, length))
    t = dict(
        tm_proj=row_tile(1024), tn_ab=512, tn_cd=1024,
        tm_out=row_tile(1024), tn_out=1024,
        tm_ffn=row_tile(512), tf_ffn=512,
        tm_sc=row_tile(512),
        c_ret=seq_tile(256), c_ssd=seq_tile(64),
    )
    if past is None:
        t['t_fox'] = seq_tile(1024)
        t['ts_fox'] = _largest_divisor(t['t_fox'], 256, LANES)
        t['t_prep'] = seq_tile(512)
    else:
        t['tk_dec'] = _largest_divisor(past, 1024, LANES)
        t['c_cum'] = _largest_divisor(past, 256, LANES)
    return t


def kernel(x_prompt, x_sample, state_ret, state_ssd, state_ssd_conv, cache_fox_k, cache_fox_v, cache_fox_logf, state_sconv, state_ffn_conv, ab_norm_w, ab_w_in, ret_norm_w, ssd_conv_w, ssd_conv_b, ssd_dt_bias, ssd_A_log, ssd_D, ssd_norm_w, ab_w_out, cd_norm_w, cd_w_in, fox_f_bias, sconv_w, cd_w_out, ffn_norm_w, ffn_w_gate, ffn_w_up, ffn_conv_w, ffn_conv_b, ffn_w_down, final_norm_w):
    p = dict(ab_norm_w=ab_norm_w, ab_w_in=ab_w_in, ret_norm_w=ret_norm_w, ssd_conv_w=ssd_conv_w,
             ssd_conv_b=ssd_conv_b, ssd_dt_bias=ssd_dt_bias, ssd_A_log=ssd_A_log, ssd_D=ssd_D,
             ssd_norm_w=ssd_norm_w, ab_w_out=ab_w_out, cd_norm_w=cd_norm_w, cd_w_in=cd_w_in,
             fox_f_bias=fox_f_bias, sconv_w=sconv_w, cd_w_out=cd_w_out, ffn_norm_w=ffn_norm_w,
             ffn_w_gate=ffn_w_gate, ffn_w_up=ffn_w_up, ffn_conv_w=ffn_conv_w, ffn_conv_b=ffn_conv_b,
             ffn_w_down=ffn_w_down, final_norm_w=final_norm_w)
    assert x_prompt.shape[-1] == D_MODEL and ab_w_in.shape == (1, D_MODEL, AB_MAIN + SSD_HEADS)
    assert cd_w_in.shape == (1, D_MODEL, CD_MAIN + FOX_HEADS) and ffn_w_gate.shape == (2, D_MODEL, D_FF)
    wb = _prep_weights(p)
    bp, lp_, _ = x_prompt.shape
    bs, ls, _ = x_sample.shape
    past = cache_fox_k.shape[2]
    (y_prompt, p_ret, p_ssd, p_ssd_conv, p_fox_k, p_fox_v, p_fox_logf, p_sconv, p_ffn_conv) = _trunk(
        x_prompt, 0, None, None, None, None, None, None, None, None, p, wb, _tiles(bp, lp_))
    (y_sample, s_ret, s_ssd, s_ssd_conv, s_fox_k, s_fox_v, s_fox_logf, s_sconv, s_ffn_conv) = _trunk(
        x_sample, past, state_ret[0], state_ssd[0], state_ssd_conv[0], cache_fox_k[0], cache_fox_v[0],
        cache_fox_logf[0], state_sconv[0], state_ffn_conv, p, wb, _tiles(bs, ls, past))
    return (y_prompt, y_sample, p_ret, s_ret, p_ssd, s_ssd, p_ssd_conv, s_ssd_conv, p_fox_k, s_fox_k,
            p_fox_v, s_fox_v, p_fox_logf, s_fox_logf, p_sconv, s_sconv, p_ffn_conv, s_ffn_conv)
```

```python
import functools
import math

import numpy as np
import jax
import jax.numpy as jnp
from jax import lax
from jax.experimental import pallas as pl
from jax.experimental.pallas import tpu as pltpu

F32 = jnp.float32
BF16 = jnp.bfloat16
EPS = 1e-6
ROPE_BASE = 10000.0
NEG_INF = float("-inf")

D_MODEL = 2048
RET_HEADS, RET_DK, RET_DV = 4, 128, 256
SSD_DINNER, SSD_HEADDIM, SSD_HEADS, SSD_GROUPS, SSD_DSTATE, SSD_CONV = 1024, 64, 16, 2, 128, 4
SSD_CONV_DIM = SSD_DINNER + 2 * SSD_GROUPS * SSD_DSTATE
FOX_HEADS, FOX_HEAD_DIM = 8, 128
FOX_WIDTH = FOX_HEADS * FOX_HEAD_DIM
SC_DIM, SC_WIDTH = 1024, 3
D_FF, FFN_CONV = 5632, 3
AB_MAIN = 2 * RET_HEADS * RET_DK + 2 * RET_HEADS * RET_DV + SSD_DINNER + SSD_CONV_DIM
AB_PAD = AB_MAIN + 128
CD_MAIN = 3 * FOX_WIDTH + 3 * SC_DIM
CD_PAD = CD_MAIN + 128

LANES = 128
SUBLANES = 8
VMEM_LIMIT = 52 * 1024 * 1024


def _cparams(n_axes):
    return pltpu.CompilerParams(dimension_semantics=("arbitrary",) * n_axes,
                                vmem_limit_bytes=VMEM_LIMIT)


def _rms(xf, w):
    return xf * lax.rsqrt(jnp.mean(xf * xf, axis=-1, keepdims=True) + EPS) * w


def _softplus(x):
    return jnp.maximum(x, 0.0) + jnp.log1p(jnp.exp(-jnp.abs(x)))


def _split3(x):
    hi = x.astype(BF16)
    r1 = x - hi.astype(F32)
    mid = r1.astype(BF16)
    lo = (r1 - mid.astype(F32)).astype(BF16)
    return hi, mid, lo


def _widen(x, n):
    return x[:, 0:n] if n <= LANES else jnp.concatenate([x] * (n // LANES), axis=1)


def _dot(a, b):
    return jnp.dot(a, b, preferred_element_type=F32)


def _dot_nt(a, b):
    return lax.dot_general(a, b, (((1,), (1,)), ((), ())), preferred_element_type=F32)


def _dot_tn(a, b):
    return lax.dot_general(a, b, (((0,), (0,)), ((), ())), preferred_element_type=F32)


def _exact_lhs_dot(m_bf16, x):
    hi, mid, lo = _split3(x)
    return _dot(m_bf16, hi) + _dot(m_bf16, mid) + _dot(m_bf16, lo)


def _exact_rhs_dot(x, m_bf16):
    hi, mid, lo = _split3(x)
    return _dot(hi, m_bf16) + _dot(mid, m_bf16) + _dot(lo, m_bf16)


def _conv_rows(ext_ref, x, prev8, w_ref, width, rows):
    ext_ref[0:SUBLANES, :] = prev8
    ext_ref[SUBLANES:SUBLANES + rows, :] = x
    out = None
    for j in range(width):
        off = SUBLANES - (width - 1) + j
        term = ext_ref[off:off + rows, :] * w_ref[j:j + 1, :]
        out = term if out is None else out + term
    return out


def _seq_tiling(seq_len, tile_rows):
    if seq_len >= tile_rows:
        assert seq_len % tile_rows == 0
        return tile_rows, 1, seq_len // tile_rows
    assert tile_rows % seq_len == 0 and seq_len % SUBLANES == 0
    return seq_len, tile_rows // seq_len, 1


def _norm_matmul_kernel(x_ref, nw_ref, w_ref, ws_ref, o_ref, os_ref, h_ref):
    @pl.when(pl.program_id(1) == 0)
    def _():
        h_ref[...] = _rms(x_ref[...], nw_ref[...]).astype(BF16)
        os_ref[...] = _dot(h_ref[...], ws_ref[...])

    o_ref[...] = _dot(h_ref[...], w_ref[...])


def _norm_matmul(x, norm_w, w, w_small, *, tm, tn):
    m, d = x.shape
    n = w.shape[1]
    assert m % tm == 0 and n % tn == 0 and w_small.shape == (d, LANES)
    return pl.pallas_call(
        _norm_matmul_kernel,
        grid=(m // tm, n // tn),
        in_specs=[pl.BlockSpec((tm, d), lambda i, j: (i, 0)),
                  pl.BlockSpec((1, d), lambda i, j: (0, 0)),
                  pl.BlockSpec((d, tn), lambda i, j: (0, j)),
                  pl.BlockSpec((d, LANES), lambda i, j: (0, 0))],
        out_specs=[pl.BlockSpec((tm, tn), lambda i, j: (i, j)),
                   pl.BlockSpec((tm, LANES), lambda i, j: (i, 0))],
        out_shape=[jax.ShapeDtypeStruct((m, n), F32),
                   jax.ShapeDtypeStruct((m, LANES), F32)],
        scratch_shapes=[pltpu.VMEM((tm, d), BF16)],
        compiler_params=_cparams(2),
        name="norm_in_proj",
    )(x, norm_w.reshape(1, d), w, w_small)


def _proj_res_kernel(x_ref, a_ref, b_ref, wa_ref, wb_ref, o_ref):
    acc = _dot(a_ref[...], wa_ref[...])
    acc = acc + _dot(b_ref[...], wb_ref[...])
    o_ref[...] = x_ref[...] + acc


def _proj_residual(x, a, b, wa, wb, *, tm, tn):
    m, d = x.shape
    ka, kb = a.shape[1], b.shape[1]
    assert m % tm == 0 and d % tn == 0
    return pl.pallas_call(
        _proj_res_kernel,
        grid=(m // tm, d // tn),
        in_specs=[pl.BlockSpec((tm, tn), lambda i, j: (i, j)),
                  pl.BlockSpec((tm, ka), lambda i, j: (i, 0)),
                  pl.BlockSpec((tm, kb), lambda i, j: (i, 0)),
                  pl.BlockSpec((ka, tn), lambda i, j: (0, j)),
                  pl.BlockSpec((kb, tn), lambda i, j: (0, j))],
        out_specs=pl.BlockSpec((tm, tn), lambda i, j: (i, j)),
        out_shape=jax.ShapeDtypeStruct((m, d), F32),
        compiler_params=_cparams(2),
        name="out_proj_residual",
    )(x, a, b, wa, wb)


def _ffn_kernel(x_ref, nw_ref, wg_ref, wu_ref, wd_ref, cw_ref, cb_ref, hist_ref, fw_ref,
                o_ref, tail_ref, h_ref, carry_ref, ext_ref, *, tm, rows, spt, tpb, nf, final):
    i = pl.program_id(0)
    f = pl.program_id(1)

    @pl.when(f == 0)
    def _():
        xf = x_ref[...]
        h_ref[...] = _rms(xf, nw_ref[...]).astype(BF16)
        o_ref[...] = xf

    h = h_ref[...]
    a = _dot(h, wg_ref[...])
    u = _dot(h, wu_ref[...])
    first = (i % tpb) == 0
    convs = []
    for s in range(spt):
        a_s = a[s * rows:(s + 1) * rows]
        if tpb == 1:
            prev = hist_ref[s]
        else:
            prev = jnp.where(first, hist_ref[s], carry_ref[f])
        convs.append(_conv_rows(ext_ref, a_s, prev, cw_ref, FFN_CONV, rows))
        tail_ref[s] = a_s[rows - SUBLANES:rows]
    if tpb > 1:
        carry_ref[f] = a[tm - SUBLANES:tm]
    conv = convs[0] if spt == 1 else jnp.concatenate(convs, axis=0)
    act = (jax.nn.silu(conv + cb_ref[...]) * u).astype(BF16)
    o_ref[...] += _dot(act, wd_ref[...])
    if final:
        @pl.when(f == nf - 1)
        def _():
            o_ref[...] = _rms(o_ref[...], fw_ref[...])


def _conv_ffn(x, norm_w, wg, wu, wd, conv_w, conv_b, hist8, final_w, *, seq_len, tm, tf, final):
    m, d = x.shape
    ff = wg.shape[1]
    assert m % tm == 0 and ff % tf == 0
    rows, spt, tpb = _seq_tiling(seq_len, tm)
    nm, nf = m // tm, ff // tf
    hist_map = (lambda i, f: (i // tpb, 0, f)) if spt == 1 else (lambda i, f: (i, 0, f))
    kern = functools.partial(_ffn_kernel, tm=tm, rows=rows, spt=spt, tpb=tpb, nf=nf, final=final)
    out, tails = pl.pallas_call(
        kern,
        grid=(nm, nf),
        in_specs=[pl.BlockSpec((tm, d), lambda i, f: (i, 0)),
                  pl.BlockSpec((1, d), lambda i, f: (0, 0)),
                  pl.BlockSpec((d, tf), lambda i, f: (0, f)),
                  pl.BlockSpec((d, tf), lambda i, f: (0, f)),
                  pl.BlockSpec((tf, d), lambda i, f: (f, 0)),
                  pl.BlockSpec((FFN_CONV, tf), lambda i, f: (0, f)),
                  pl.BlockSpec((1, tf), lambda i, f: (0, f)),
                  pl.BlockSpec((spt, SUBLANES, tf), hist_map),
                  pl.BlockSpec((1, d), lambda i, f: (0, 0))],
        out_specs=[pl.BlockSpec((tm, d), lambda i, f: (i, 0)),
                   pl.BlockSpec((spt, SUBLANES, tf), lambda i, f: (i, 0, f))],
        out_shape=[jax.ShapeDtypeStruct((m, d), F32),
                   jax.ShapeDtypeStruct((nm * spt, SUBLANES, ff), F32)],
        scratch_shapes=[pltpu.VMEM((tm, d), BF16),
                        pltpu.VMEM((nf, SUBLANES, tf), F32),
                        pltpu.VMEM((rows + SUBLANES, tf), F32)],
        compiler_params=_cparams(2),
        name="conv_ffn",
    )(x, norm_w.reshape(1, d), wg, wu, wd, conv_w, conv_b.reshape(1, ff), hist8, final_w.reshape(1, d))
    return out, tails


def _retention_kernel(q_ref, k_ref, v_ref, g_ref, cos_ref, sin_ref, st_ref, nw_ref,
                      y_ref, so_ref, *, c):
    ci = pl.program_id(1)

    @pl.when(ci == 0)
    def _():
        so_ref[...] = st_ref[...]

    cos = cos_ref[...]
    sin = sin_ref[...]
    ii = lax.broadcasted_iota(jnp.int32, (c, c), 0)
    jj = lax.broadcasted_iota(jnp.int32, (c, c), 1)
    diff = (ii - jj).astype(F32)
    causal = ii >= jj
    ridx = lax.broadcasted_iota(jnp.int32, (c, 1), 0).astype(F32)
    for h in range(RET_HEADS):
        lg = math.log1p(-(2.0 ** (-5.0 - h)))
        q = q_ref[:, h * RET_DK:(h + 1) * RET_DK]
        k = k_ref[:, h * RET_DK:(h + 1) * RET_DK]
        v = v_ref[:, h * RET_DV:(h + 1) * RET_DV]
        qr = q * cos + pltpu.roll(q, RET_DK // 2, 1) * sin
        kr = (k * cos + pltpu.roll(k, RET_DK // 2, 1) * sin) * (RET_DK ** -0.5)
        qb = qr.astype(BF16)
        kb = kr.astype(BF16)
        vb = v.astype(BF16)
        decay = jnp.exp(jnp.where(causal, diff * lg, NEG_INF))
        inner = jnp.exp((ridx + 1.0) * lg)
        sdecay = jnp.exp((c - 1.0 - ridx) * lg)
        s = so_ref[0, h]
        scores = _dot_nt(qb, kb) * decay
        y = _dot(scores.astype(BF16), vb)
        y = y + _dot(qb, s.astype(BF16)) * inner
        kd = (kr * sdecay).astype(BF16)
        so_ref[0, h] = math.exp(c * lg) * s + _dot_tn(kd, vb)
        mu = jnp.mean(y, axis=-1, keepdims=True)
        yc = y - mu
        var = jnp.mean(yc * yc, axis=-1, keepdims=True)
        yn = yc * lax.rsqrt(var + EPS) * nw_ref[:, h * RET_DV:(h + 1) * RET_DV]
        g = g_ref[:, h * RET_DV:(h + 1) * RET_DV]
        y_ref[:, h * RET_DV:(h + 1) * RET_DV] = (jax.nn.silu(g) * yn).astype(BF16)


def _retention(proj, cosf, sinf, state, norm_w, *, bsz, seq_len, c):
    m = proj.shape[0]
    nc = seq_len // c
    assert seq_len % c == 0
    qk_w = RET_HEADS * RET_DK
    v_w = RET_HEADS * RET_DV
    row = lambda b, ci: b * nc + ci
    y, s_new = pl.pallas_call(
        functools.partial(_retention_kernel, c=c),
        grid=(bsz, nc),
        in_specs=[pl.BlockSpec((c, qk_w), lambda b, ci: (row(b, ci), 0)),
                  pl.BlockSpec((c, qk_w), lambda b, ci: (row(b, ci), 1)),
                  pl.BlockSpec((c, v_w), lambda b, ci: (row(b, ci), 1)),
                  pl.BlockSpec((c, v_w), lambda b, ci: (row(b, ci), 2)),
                  pl.BlockSpec((c, RET_DK), lambda b, ci: (ci, 0)),
                  pl.BlockSpec((c, RET_DK), lambda b, ci: (ci, 0)),
                  pl.BlockSpec((1, RET_HEADS, RET_DK, RET_DV), lambda b, ci: (b, 0, 0, 0)),
                  pl.BlockSpec((1, v_w), lambda b, ci: (0, 0))],
        out_specs=[pl.BlockSpec((c, v_w), lambda b, ci: (row(b, ci), 0)),
                   pl.BlockSpec((1, RET_HEADS, RET_DK, RET_DV), lambda b, ci: (b, 0, 0, 0))],
        out_shape=[jax.ShapeDtypeStruct((m, v_w), BF16),
                   jax.ShapeDtypeStruct(state.shape, F32)],
        compiler_params=_cparams(2),
        name="retention",
    )(proj, proj, proj, proj, cosf, sinf, state, norm_w.reshape(1, v_w))
    return y, s_new


def _ssd_kernel(z_ref, xs_ref, bc_ref, dt_ref, hx_ref, hbc_ref, st_ref,
                cwx_ref, cwbc_ref, cbx_ref, cbbc_ref, dtb_ref, alog_ref, dsk_ref, nw_ref,
                tri_ref, exp_ref,
                y_ref, so_ref,
                st_scr, cx_scr, cbc_scr, extx_scr, extbc_scr, yh_scr, *, c, nc):
    ci = pl.program_id(1)
    gw = SSD_DINNER // SSD_GROUPS
    hpg = SSD_HEADS // SSD_GROUPS

    @pl.when(ci == 0)
    def _():
        st_scr[...] = st_ref[0].T
        cx_scr[...] = hx_ref[0]
        cbc_scr[...] = hbc_ref[0]

    xs_raw = xs_ref[...]
    bc_raw = bc_ref[...]
    xs = jax.nn.silu(_conv_rows(extx_scr, xs_raw, cx_scr[...], cwx_ref, SSD_CONV, c) + cbx_ref[...])
    bcm = jax.nn.silu(_conv_rows(extbc_scr, bc_raw, cbc_scr[...], cwbc_ref, SSD_CONV, c) + cbbc_ref[...])
    cx_scr[...] = xs_raw[c - SUBLANES:c]
    cbc_scr[...] = bc_raw[c - SUBLANES:c]

    tri = tri_ref[...]
    expand = exp_ref[...]
    dt = _softplus(dt_ref[...] + dtb_ref[...])
    a = -jnp.exp(alog_ref[...])
    acs = _exact_lhs_dot(tri, dt * a)
    acs_t = acs.T
    acs_last = acs[c - 1:c, :]
    exp_acs = jnp.exp(acs)
    to_end = jnp.exp(acs_last - acs)
    dt_e = _exact_rhs_dot(dt, expand)
    to_end_e = _exact_rhs_dot(to_end, expand)
    exp_acs_e = _exact_rhs_dot(exp_acs, expand)
    chunk_dec_e = _exact_rhs_dot(jnp.exp(acs_last), expand)

    xdt = xs * dt_e
    xdt_b = xdt.astype(BF16)
    xend_b = (xdt * to_end_e).astype(BF16)
    ii = lax.broadcasted_iota(jnp.int32, (c, c), 0)
    jj = lax.broadcasted_iota(jnp.int32, (c, c), 1)
    causal = ii >= jj
    nb = SSD_GROUPS * SSD_DSTATE
    for g in range(SSD_GROUPS):
        b_g = bcm[:, g * SSD_DSTATE:(g + 1) * SSD_DSTATE].astype(BF16)
        c_g = bcm[:, nb + g * SSD_DSTATE:nb + (g + 1) * SSD_DSTATE].astype(BF16)
        cb = _dot_nt(c_g, b_g)
        s_g = st_scr[:, g * gw:(g + 1) * gw]
        y_state = _dot(c_g, s_g.astype(BF16)) * exp_acs_e[:, g * gw:(g + 1) * gw]
        for r in range(hpg):
            hh = g * hpg + r
            seg = acs[:, hh:hh + 1] - acs_t[hh:hh + 1, :]
            lmat = jnp.exp(jnp.where(causal, seg, NEG_INF))
            mm = (cb * lmat).astype(BF16)
            lo = hh * SSD_HEADDIM
            yh_scr[:, lo:lo + SSD_HEADDIM] = (
                _dot(mm, xdt_b[:, lo:lo + SSD_HEADDIM]) + y_state[:, r * SSD_HEADDIM:(r + 1) * SSD_HEADDIM])
        upd = _dot_tn(b_g, xend_b[:, g * gw:(g + 1) * gw])
        st_scr[:, g * gw:(g + 1) * gw] = chunk_dec_e[:, g * gw:(g + 1) * gw] * s_g + upd

    y = yh_scr[...] + dsk_ref[...] * xs
    z = z_ref[...]
    y_ref[...] = _rms(y * jax.nn.silu(z), nw_ref[...]).astype(BF16)

    @pl.when(ci == nc - 1)
    def _():
        so_ref[0] = st_scr[...].T


def _ssd(proj, dt_proj, hist8, state, conv_w, conv_b, dt_bias, a_log, d_skip, norm_w, *, bsz, seq_len, c):
    m = proj.shape[0]
    nc = seq_len // c
    assert seq_len % c == 0
    row = lambda b, ci: b * nc + ci
    const2 = lambda b, ci: (0, 0)
    di, bcw = SSD_DINNER, 2 * SSD_GROUPS * SSD_DSTATE
    tri = jnp.asarray(np.tril(np.ones((c, c), np.float32)), BF16)
    expand = np.zeros((LANES, di), np.float32)
    for h in range(SSD_HEADS):
        expand[h, h * SSD_HEADDIM:(h + 1) * SSD_HEADDIM] = 1.0
    expand = jnp.asarray(expand, BF16)
    pad_row = lambda v: jnp.pad(v.astype(F32), (0, LANES - v.shape[0])).reshape(1, LANES)
    st2 = state.reshape(bsz, di, SSD_DSTATE)
    y, s_new = pl.pallas_call(
        functools.partial(_ssd_kernel, c=c, nc=nc),
        grid=(bsz, nc),
        in_specs=[pl.BlockSpec((c, di), lambda b, ci: (row(b, ci), 3)),
                  pl.BlockSpec((c, di), lambda b, ci: (row(b, ci), 4)),
                  pl.BlockSpec((c, bcw), lambda b, ci: (row(b, ci), 10)),
                  pl.BlockSpec((c, LANES), lambda b, ci: (row(b, ci), 0)),
                  pl.BlockSpec((1, SUBLANES, di), lambda b, ci: (b, 0, 0)),
                  pl.BlockSpec((1, SUBLANES, bcw), lambda b, ci: (b, 0, 2)),
                  pl.BlockSpec((1, di, SSD_DSTATE), lambda b, ci: (b, 0, 0)),
                  pl.BlockSpec((SSD_CONV, di), const2),
                  pl.BlockSpec((SSD_CONV, bcw), lambda b, ci: (0, 2)),
                  pl.BlockSpec((1, di), const2),
                  pl.BlockSpec((1, bcw), lambda b, ci: (0, 2)),
                  pl.BlockSpec((1, LANES), const2),
                  pl.BlockSpec((1, LANES), const2),
                  pl.BlockSpec((1, di), const2),
                  pl.BlockSpec((1, di), const2),
                  pl.BlockSpec((c, c), const2),
                  pl.BlockSpec((LANES, di), const2)],
        out_specs=[pl.BlockSpec((c, di), lambda b, ci: (row(b, ci), 0)),
                   pl.BlockSpec((1, di, SSD_DSTATE), lambda b, ci: (b, 0, 0))],
        out_shape=[jax.ShapeDtypeStruct((m, di), BF16),
                   jax.ShapeDtypeStruct(st2.shape, F32)],
        scratch_shapes=[pltpu.VMEM((SSD_DSTATE, di), F32),
                        pltpu.VMEM((SUBLANES, di), F32),
                        pltpu.VMEM((SUBLANES, bcw), F32),
                        pltpu.VMEM((c + SUBLANES, di), F32),
                        pltpu.VMEM((c + SUBLANES, bcw), F32),
                        pltpu.VMEM((c, di), F32)],
        compiler_params=_cparams(2),
        name="ssd",
    )(proj, proj, proj, dt_proj, hist8, hist8, st2,
      conv_w, conv_w, conv_b.reshape(1, -1), conv_b.reshape(1, -1),
      pad_row(dt_bias), pad_row(a_log), jnp.repeat(d_skip.astype(F32), SSD_HEADDIM).reshape(1, di),
      norm_w.reshape(1, di), tri, expand)
    return y, s_new.reshape(state.shape)


def _decode_cum_kernel(lfc_ref, fl_ref, b_ref, tri_ref, cumt_c_ref, lfn_ref, cumn_ref, cumt_n_ref,
                       carry_ref, *, c, ncb):
    j = pl.program_id(0)

    @pl.when(j == 0)
    def _():
        carry_ref[...] = jnp.zeros_like(carry_ref)

    @pl.when(j < ncb)
    def _():
        cum = _exact_lhs_dot(tri_ref[...], lfc_ref[...]) + carry_ref[...]
        carry_ref[...] = cum[c - 1:c, :]
        cumt_c_ref[...] = cum.T

    @pl.when(j == ncb)
    def _():
        lf = -_softplus(-(fl_ref[...] + b_ref[...]))
        lfn_ref[...] = lf
        cum = _exact_lhs_dot(tri_ref[0:LANES, 0:LANES], lf) + carry_ref[...]
        cumn_ref[...] = cum
        cumt_n_ref[...] = cum.T


def _decode_cum(cache_lf, fl_new, bias, *, c):
    past = cache_lf.shape[0]
    assert past % c == 0 and c % LANES == 0 and fl_new.shape == (LANES, LANES)
    ncb = past // c
    tri = jnp.asarray(np.tril(np.ones((c, c), np.float32)), BF16)
    blk = lambda j: jnp.minimum(j, ncb - 1)
    sq = jax.ShapeDtypeStruct((LANES, LANES), F32)
    return pl.pallas_call(
        functools.partial(_decode_cum_kernel, c=c, ncb=ncb),
        grid=(ncb + 1,),
        in_specs=[pl.BlockSpec((c, LANES), lambda j: (blk(j), 0)),
                  pl.BlockSpec((LANES, LANES), lambda j: (0, 0)),
                  pl.BlockSpec((1, LANES), lambda j: (0, 0)),
                  pl.BlockSpec((c, c), lambda j: (0, 0))],
        out_specs=[pl.BlockSpec((LANES, c), lambda j: (0, blk(j))),
                   pl.BlockSpec((LANES, LANES), lambda j: (0, 0)),
                   pl.BlockSpec((LANES, LANES), lambda j: (0, 0)),
                   pl.BlockSpec((LANES, LANES), lambda j: (0, 0))],
        out_shape=[jax.ShapeDtypeStruct((LANES, past), F32), sq, sq, sq],
        scratch_shapes=[pltpu.VMEM((1, LANES), F32)],
        compiler_params=_cparams(1),
        name="decode_logf_cumsum",
    )(cache_lf, fl_new, bias, tri)


FOX_AUG = 2 * FOX_HEAD_DIM
N_BIAS_PIECES = 3


def _fox_prep_kernel(q_ref, k_ref, v_ref, fl_ref, fb_ref, tri_ref, place_ref, ones_ref,
                     qa_ref, ka_ref, vb_ref, k32_ref, v32_ref, lf_ref, carry_ref, *, tp):
    @pl.when(pl.program_id(1) == 0)
    def _():
        carry_ref[...] = jnp.zeros_like(carry_ref)

    lf = -_softplus(-(fl_ref[...] + fb_ref[...]))
    lf_ref[...] = lf
    cum = _exact_lhs_dot(tri_ref[...], lf) + carry_ref[...]
    carry_ref[...] = cum[tp - 1:tp, :]
    pieces = _split3(cum * (FOX_HEAD_DIM ** 0.5))
    n = N_BIAS_PIECES
    aug_q = ones_ref[0:1, :] + sum(_dot(pieces[r], place_ref[r]) for r in range(n))
    aug_k = ones_ref[1:2, :] - sum(_dot(pieces[r], place_ref[n + r]) for r in range(n))
    for h in range(FOX_HEADS):
        src = slice(h * FOX_HEAD_DIM, (h + 1) * FOX_HEAD_DIM)
        feat = slice(h * FOX_AUG, h * FOX_AUG + FOX_HEAD_DIM)
        bias = slice(h * FOX_AUG + FOX_HEAD_DIM, (h + 1) * FOX_AUG)
        qa_ref[:, feat] = q_ref[:, src].astype(BF16)
        qa_ref[:, bias] = aug_q[:, src].astype(BF16)
        ka_ref[:, feat] = k_ref[:, src].astype(BF16)
        ka_ref[:, bias] = aug_k[:, src].astype(BF16)
    k = k_ref[...]
    v = v_ref[...]
    k32_ref[...] = pltpu.einshape("m(hd)->mhd", k, h=FOX_HEADS)
    v32_ref[...] = pltpu.einshape("m(hd)->mhd", v, h=FOX_HEADS)
    vb_ref[...] = v.astype(BF16)


def _fox_prep(proj, fl_proj, f_bias, *, bsz, seq_len, tp):
    m = proj.shape[0]
    nt = seq_len // tp
    assert seq_len % tp == 0
    w = FOX_WIDTH
    tri = jnp.asarray(np.tril(np.ones((tp, tp), np.float32)), BF16)
    n = N_BIAS_PIECES
    place = np.zeros((2 * n, LANES, w), np.float32)
    ones = np.zeros((SUBLANES, w), np.float32)
    for h in range(FOX_HEADS):
        for r in range(2 * n):
            place[r, h, h * FOX_HEAD_DIM + r] = 1.0
        ones[0, h * FOX_HEAD_DIM + n:h * FOX_HEAD_DIM + 2 * n] = 1.0
        ones[1, h * FOX_HEAD_DIM:h * FOX_HEAD_DIM + n] = 1.0
    row = lambda b, ti: (b * nt + ti, 0)
    const2 = lambda b, ti: (0, 0)
    return pl.pallas_call(
        functools.partial(_fox_prep_kernel, tp=tp),
        grid=(bsz, nt),
        in_specs=[pl.BlockSpec((tp, w), lambda b, ti: (b * nt + ti, 0)),
                  pl.BlockSpec((tp, w), lambda b, ti: (b * nt + ti, 1)),
                  pl.BlockSpec((tp, w), lambda b, ti: (b * nt + ti, 2)),
                  pl.BlockSpec((tp, LANES), row),
                  pl.BlockSpec((1, LANES), const2),
                  pl.BlockSpec((tp, tp), const2),
                  pl.BlockSpec((2 * n, LANES, w), lambda b, ti: (0, 0, 0)),
                  pl.BlockSpec((SUBLANES, w), const2)],
        out_specs=[pl.BlockSpec((tp, FOX_HEADS * FOX_AUG), row),
                   pl.BlockSpec((tp, FOX_HEADS * FOX_AUG), row),
                   pl.BlockSpec((tp, w), row),
                   pl.BlockSpec((tp, FOX_HEADS, FOX_HEAD_DIM), lambda b, ti: (b * nt + ti, 0, 0)),
                   pl.BlockSpec((tp, FOX_HEADS, FOX_HEAD_DIM), lambda b, ti: (b * nt + ti, 0, 0)),
                   pl.BlockSpec((tp, LANES), row)],
        out_shape=[jax.ShapeDtypeStruct((m, FOX_HEADS * FOX_AUG), BF16),
                   jax.ShapeDtypeStruct((m, FOX_HEADS * FOX_AUG), BF16),
                   jax.ShapeDtypeStruct((m, w), BF16),
                   jax.ShapeDtypeStruct((m, FOX_HEADS, FOX_HEAD_DIM), F32),
                   jax.ShapeDtypeStruct((m, FOX_HEADS, FOX_HEAD_DIM), F32),
                   jax.ShapeDtypeStruct((m, LANES), F32)],
        scratch_shapes=[pltpu.VMEM((1, LANES), F32)],
        compiler_params=_cparams(2),
        name="fox_prep",
    )(proj, proj, proj, fl_proj, f_bias, tri, jnp.asarray(place, BF16), jnp.asarray(ones, F32))


def _fox_kernel(qi_ref, ki_ref, q_ref, k_ref, v_ref, o_ref, m_ref, acc_ref, va_ref, *, t, ts):
    step = pl.program_id(2)
    qi = qi_ref[step]
    ki = ki_ref[step]
    to_log2 = (FOX_HEAD_DIM ** -0.5) * math.log2(math.e)
    hd = FOX_HEAD_DIM

    @pl.when(ki == 0)
    def _():
        m_ref[...] = jnp.full_like(m_ref, NEG_INF)
        acc_ref[...] = jnp.zeros_like(acc_ref)
        va_ref[:, hd:2 * hd] = jnp.ones((t, hd), BF16)

    va_ref[:, 0:hd] = v_ref[...]

    def scores(rows, nk, r, diagonal):
        s = _dot_nt(q_ref[rows, :], k_ref[0:nk, :])
        if diagonal:
            ri = lax.broadcasted_iota(jnp.int32, (ts, nk), 0) + r * ts
            ci = lax.broadcasted_iota(jnp.int32, (ts, nk), 1)
            s = jnp.where(ci <= ri, s, NEG_INF)
        return s

    def update(diagonal):
        for r in range(t // ts):
            rows = slice(r * ts, (r + 1) * ts)
            nk = (r + 1) * ts if diagonal else t
            s = scores(rows, nk, r, diagonal)
            m_old = m_ref[rows, :]
            m_new = jnp.maximum(m_old, jnp.max(s, axis=1, keepdims=True))
            m_ref[rows, :] = m_new
            alpha = jnp.exp2((m_old - m_new) * to_log2)
            p = jnp.exp2((s - _widen(m_new, nk)) * to_log2)
            acc_ref[rows, :] = _widen(alpha, 2 * hd) * acc_ref[rows, :] + _dot(p.astype(BF16), va_ref[0:nk, :])

    @pl.when(ki < qi)
    def _():
        update(False)

    @pl.when(ki == qi)
    def _():
        update(True)
        o_ref[...] = (acc_ref[:, 0:hd] / acc_ref[:, hd:2 * hd]).astype(BF16)


def _fox_prompt(qa, ka, vb, *, bsz, seq_len, t, ts):
    m = qa.shape[0]
    nq = seq_len // t
    assert seq_len % t == 0 and t % ts == 0
    pairs = [(qi, ki) for qi in range(nq) for ki in range(qi + 1)]
    qi_tab = jnp.asarray([p[0] for p in pairs], jnp.int32)
    ki_tab = jnp.asarray([p[1] for p in pairs], jnp.int32)
    grid_spec = pltpu.PrefetchScalarGridSpec(
        num_scalar_prefetch=2,
        grid=(bsz, FOX_HEADS, len(pairs)),
        in_specs=[pl.BlockSpec((t, FOX_AUG), lambda b, h, s, qi, ki: (b * nq + qi[s], h)),
                  pl.BlockSpec((t, FOX_AUG), lambda b, h, s, qi, ki: (b * nq + ki[s], h)),
                  pl.BlockSpec((t, FOX_HEAD_DIM), lambda b, h, s, qi, ki: (b * nq + ki[s], h))],
        out_specs=pl.BlockSpec((t, FOX_HEAD_DIM), lambda b, h, s, qi, ki: (b * nq + qi[s], h)),
        scratch_shapes=[pltpu.VMEM((t, LANES), F32), pltpu.VMEM((t, 2 * FOX_HEAD_DIM), F32),
                        pltpu.VMEM((t, 2 * FOX_HEAD_DIM), BF16)],
    )
    return pl.pallas_call(
        functools.partial(_fox_kernel, t=t, ts=ts),
        grid_spec=grid_spec,
        out_shape=jax.ShapeDtypeStruct((m, FOX_WIDTH), BF16),
        compiler_params=_cparams(3),
        name="fox_attention",
    )(qi_tab, ki_tab, qa, ka, vb)


def _fox_decode_kernel(q_ref, kn_ref, vn_ref, kc_ref, vc_ref, cq_ref, ckc_ref, ckn_ref, o_ref,
                       m_ref, l_ref, acc_ref, *, lq, ncb):
    j = pl.program_id(1)
    nh, hd = FOX_HEADS, FOX_HEAD_DIM

    @pl.when(j == 0)
    def _():
        m_ref[...] = jnp.full_like(m_ref, NEG_INF)
        l_ref[...] = jnp.zeros_like(l_ref)
        acc_ref[...] = jnp.zeros_like(acc_ref)

    def attend(k_head, v_head, ck_head, causal):
        for h in range(nh):
            qh = q_ref[:, h * hd:(h + 1) * hd].astype(BF16)
            ck = ck_head(h)
            tk = ck.shape[1]
            s = _dot_nt(qh, k_head(h).astype(BF16)) * (hd ** -0.5)
            s = s + (_widen(cq_ref[h], tk) - ck)
            if causal:
                rows = lax.broadcasted_iota(jnp.int32, (lq, tk), 0)
                cols = lax.broadcasted_iota(jnp.int32, (lq, tk), 1)
                s = jnp.where(cols <= rows, s, NEG_INF)
            m_old = m_ref[h]
            m_new = jnp.maximum(m_old, jnp.max(s, axis=1, keepdims=True))
            alpha = jnp.exp(m_old - m_new)
            p = jnp.exp(s - _widen(m_new, tk))
            l_ref[h] = alpha * l_ref[h] + jnp.sum(p, axis=1, keepdims=True)
            acc_ref[h] = alpha * acc_ref[h] + _dot(p.astype(BF16), v_head(h).astype(BF16))
            m_ref[h] = m_new

    @pl.when(j < ncb)
    def _():
        k_hm = pltpu.einshape("mhd->hmd", kc_ref[0])
        v_hm = pltpu.einshape("mhd->hmd", vc_ref[0])
        attend(lambda h: k_hm[h], lambda h: v_hm[h], lambda h: ckc_ref[h], False)

    @pl.when(j == ncb)
    def _():
        attend(lambda h: kn_ref[:, h * hd:(h + 1) * hd], lambda h: vn_ref[:, h * hd:(h + 1) * hd],
               lambda h: ckn_ref[h][:, 0:lq], True)
        for h in range(nh):
            o_ref[:, h * hd:(h + 1) * hd] = (acc_ref[h] / l_ref[h]).astype(BF16)


def _fox_decode(proj, cache_k, cache_v, cq, ck_cache, ck_new, *, bsz, lq, tk):
    past = cache_k.shape[1]
    assert past % tk == 0 and tk % LANES == 0 and lq <= LANES
    ncb = past // tk
    nh, hd, w = FOX_HEADS, FOX_HEAD_DIM, FOX_WIDTH
    tile = lambda j: jnp.minimum(j, ncb - 1)
    cache_spec = pl.BlockSpec((1, tk, nh, hd), lambda b, j: (b, tile(j), 0, 0))
    return pl.pallas_call(
        functools.partial(_fox_decode_kernel, lq=lq, ncb=ncb),
        grid=(bsz, ncb + 1),
        in_specs=[pl.BlockSpec((lq, w), lambda b, j: (b, 0)),
                  pl.BlockSpec((lq, w), lambda b, j: (b, 1)),
                  pl.BlockSpec((lq, w), lambda b, j: (b, 2)),
                  cache_spec,
                  cache_spec,
                  pl.BlockSpec((nh, lq, LANES), lambda b, j: (b, 0, 0)),
                  pl.BlockSpec((nh, 1, tk), lambda b, j: (b, 0, tile(j))),
                  pl.BlockSpec((nh, 1, LANES), lambda b, j: (b, 0, 0))],
        out_specs=pl.BlockSpec((lq, w), lambda b, j: (b, 0)),
        out_shape=jax.ShapeDtypeStruct((bsz * lq, w), BF16),
        scratch_shapes=[pltpu.VMEM((nh, lq, LANES), F32), pltpu.VMEM((nh, lq, LANES), F32),
                        pltpu.VMEM((nh, lq, hd), F32)],
        compiler_params=_cparams(2),
        name="fox_decode",
    )(proj, proj, proj, cache_k, cache_v, cq, ck_cache, ck_new)


def _sconv_kernel(u_ref, bg_ref, cg_ref, cw_ref, hist_ref, y_ref, tail_ref, carry_ref, ext_ref,
                  *, tm, rows, spt, tpb):
    i = pl.program_id(0)
    w = cg_ref[...] * u_ref[...]
    first = (i % tpb) == 0
    for s in range(spt):
        w_s = w[s * rows:(s + 1) * rows]
        if tpb == 1:
            prev = hist_ref[s]
        else:
            prev = jnp.where(first, hist_ref[s], carry_ref[...])
        conv = _conv_rows(ext_ref, w_s, prev, cw_ref, SC_WIDTH, rows)
        y_ref[s * rows:(s + 1) * rows, :] = (bg_ref[s * rows:(s + 1) * rows, :] * conv).astype(BF16)
        tail_ref[s] = w_s[rows - SUBLANES:rows]
    if tpb > 1:
        carry_ref[...] = w[tm - SUBLANES:tm]


def _sconv(proj, conv_w, hist8, *, seq_len, tm):
    m = proj.shape[0]
    assert m % tm == 0
    rows, spt, tpb = _seq_tiling(seq_len, tm)
    nm = m // tm
    hist_map = (lambda i: (i // tpb, 0, 0)) if spt == 1 else (lambda i: (i, 0, 0))
    base = 3 * FOX_WIDTH // SC_DIM
    return pl.pallas_call(
        functools.partial(_sconv_kernel, tm=tm, rows=rows, spt=spt, tpb=tpb),
        grid=(nm,),
        in_specs=[pl.BlockSpec((tm, SC_DIM), lambda i: (i, base)),
                  pl.BlockSpec((tm, SC_DIM), lambda i: (i, base + 1)),
                  pl.BlockSpec((tm, SC_DIM), lambda i: (i, base + 2)),
                  pl.BlockSpec((SC_WIDTH, SC_DIM), lambda i: (0, 0)),
                  pl.BlockSpec((spt, SUBLANES, SC_DIM), hist_map)],
        out_specs=[pl.BlockSpec((tm, SC_DIM), lambda i: (i, 0)),
                   pl.BlockSpec((spt, SUBLANES, SC_DIM), lambda i: (i, 0, 0))],
        out_shape=[jax.ShapeDtypeStruct((m, SC_DIM), BF16),
                   jax.ShapeDtypeStruct((nm * spt, SUBLANES, SC_DIM), F32)],
        scratch_shapes=[pltpu.VMEM((SUBLANES, SC_DIM), F32),
                        pltpu.VMEM((rows + SUBLANES, SC_DIM), F32)],
        compiler_params=_cparams(1),
        name="gated_short_conv",
    )(proj, proj, proj, conv_w, hist8)


def _hist8(state):
    n, w1, c = state.shape
    return jnp.concatenate([jnp.zeros((n, SUBLANES - w1, c), F32), state.astype(F32)], axis=1)


def _tails(tails, n_seq, seq_len, tile_rows, keep):
    per_seq = max(1, seq_len // tile_rows)
    idx = (jnp.arange(n_seq) + 1) * per_seq - 1
    return tails[idx][:, SUBLANES - keep:, :]


def _rope_tables(pos0, length):
    half = RET_DK // 2
    inv = ROPE_BASE ** (-jnp.arange(half, dtype=F32) / half)
    ang = (pos0 + jnp.arange(length)).astype(F32)[:, None] * inv[None, :]
    cos, sin = jnp.cos(ang), jnp.sin(ang)
    return jnp.concatenate([cos, cos], axis=1), jnp.concatenate([-sin, sin], axis=1)


def _prep_weights(p):
    d = D_MODEL
    ab_in = p['ab_w_in'][0]
    cd_in = p['cd_w_in'][0]
    f0 = 3 * FOX_WIDTH
    pad_cols = lambda w: jnp.pad(w, ((0, 0), (0, LANES - w.shape[1]))).astype(BF16)
    return dict(
        ab_in=ab_in[:, :AB_MAIN].astype(BF16),
        ab_small=pad_cols(ab_in[:, AB_MAIN:]),
        cd_in=jnp.concatenate([cd_in[:, :f0], cd_in[:, f0 + FOX_HEADS:]], axis=1).astype(BF16),
        cd_small=pad_cols(cd_in[:, f0:f0 + FOX_HEADS]),
        ab_out=p['ab_w_out'][0].astype(BF16),
        cd_out=p['cd_w_out'][0].astype(BF16),
        ffn_gate=[p['ffn_w_gate'][i].astype(BF16) for i in range(2)],
        ffn_up=[p['ffn_w_up'][i].astype(BF16) for i in range(2)],
        ffn_down=[p['ffn_w_down'][i].astype(BF16) for i in range(2)],
    )


def _trunk(x, pos0, st_ret, st_ssd, st_ssd_conv, c_k, c_v, c_logf, st_sconv, st_ffn, p, wb, t):
    bsz, length, d = x.shape
    m = bsz * length
    xf = x.reshape(m, d)
    zeros = lambda *shape: jnp.zeros(shape, F32)

    proj, dt_proj = _norm_matmul(xf, p['ab_norm_w'][0], wb['ab_in'], wb['ab_small'],
                                 tm=t['tm_proj'], tn=t['tn_ab'])
    cosf, sinf = _rope_tables(pos0, length)
    ret_state = zeros(bsz, RET_HEADS, RET_DK, RET_DV) if st_ret is None else st_ret
    y_ret, ret_new = _retention(proj, cosf, sinf, ret_state, p['ret_norm_w'][0],
                                bsz=bsz, seq_len=length, c=t['c_ret'])
    ssd_state = zeros(bsz, SSD_HEADS, SSD_HEADDIM, SSD_DSTATE) if st_ssd is None else st_ssd
    ssd_hist = zeros(bsz, SSD_CONV - 1, SSD_CONV_DIM) if st_ssd_conv is None else st_ssd_conv
    y_ssd, ssd_new = _ssd(proj, dt_proj, _hist8(ssd_hist), ssd_state, p['ssd_conv_w'][0], p['ssd_conv_b'][0],
                          p['ssd_dt_bias'][0], p['ssd_A_log'][0], p['ssd_D'][0], p['ssd_norm_w'][0],
                          bsz=bsz, seq_len=length, c=t['c_ssd'])
    xbc_lo = AB_MAIN - SSD_CONV_DIM
    ssd_conv_new = proj.reshape(bsz, length, -1)[:, length - (SSD_CONV - 1):, xbc_lo:AB_MAIN]
    ab_out = wb['ab_out']
    xf = _proj_residual(xf, y_ret, y_ssd, ab_out[:RET_HEADS * RET_DV], ab_out[RET_HEADS * RET_DV:],
                        tm=t['tm_out'], tn=t['tn_out'])

    ffn_new = []
    ffn_hist0 = zeros(bsz, FFN_CONV - 1, D_FF) if st_ffn is None else st_ffn[0]
    xf, tails = _conv_ffn(xf, p['ffn_norm_w'][0], wb['ffn_gate'][0], wb['ffn_up'][0], wb['ffn_down'][0],
                          p['ffn_conv_w'][0], p['ffn_conv_b'][0], _hist8(ffn_hist0), p['final_norm_w'],
                          seq_len=length, tm=t['tm_ffn'], tf=t['tf_ffn'], final=False)
    ffn_new.append(_tails(tails, bsz, length, t['tm_ffn'], FFN_CONV - 1))

    proj, fl_proj = _norm_matmul(xf, p['cd_norm_w'][0], wb['cd_in'], wb['cd_small'],
                                 tm=t['tm_proj'], tn=t['tn_cd'])
    f_bias = jnp.pad(p['fox_f_bias'][0].astype(F32), (0, LANES - FOX_HEADS)).reshape(1, LANES)
    head_shape = (bsz, length, FOX_HEADS, FOX_HEAD_DIM)
    if c_k is None:
        qa, ka, vb, k32, v32, logf = _fox_prep(proj, fl_proj, f_bias, bsz=bsz, seq_len=length, tp=t['t_prep'])
        y_fox = _fox_prompt(qa, ka, vb, bsz=bsz, seq_len=length, t=t['t_fox'], ts=t['ts_fox'])
        logf_new = logf.reshape(bsz, length, LANES)[:, :, :FOX_HEADS]
        k_new, v_new = k32.reshape(head_shape), v32.reshape(head_shape)
    else:
        proj3 = proj.reshape(bsz, length, -1)
        k_new = proj3[:, :, FOX_WIDTH:2 * FOX_WIDTH].reshape(head_shape)
        v_new = proj3[:, :, 2 * FOX_WIDTH:3 * FOX_WIDTH].reshape(head_shape)
        past = c_k.shape[1]
        pairs = bsz * FOX_HEADS
        assert pairs <= LANES and length <= LANES
        to_lanes = lambda a, rows: jnp.pad(jnp.swapaxes(a, 0, 1).reshape(a.shape[1], pairs),
                                           ((0, rows - a.shape[1]), (0, LANES - pairs)))
        from_lanes = lambda a: jnp.swapaxes(a[:length, :pairs].reshape(length, bsz, FOX_HEADS), 0, 1)
        cache_lf = to_lanes(c_logf.astype(F32), past)
        fl_rows = to_lanes(fl_proj.reshape(bsz, length, LANES)[:, :, :FOX_HEADS], LANES)
        bias_lanes = jnp.pad(jnp.tile(p['fox_f_bias'][0].astype(F32), bsz), (0, LANES - pairs)).reshape(1, LANES)
        cum_t_cache, lf_rows, cum_rows, cum_t_new = _decode_cum(cache_lf, fl_rows, bias_lanes, c=t['c_cum'])
        logf_new = from_lanes(lf_rows)
        cq = jnp.broadcast_to(cum_rows[:length, :pairs].T[:, :, None], (pairs, length, LANES))
        y_fox = _fox_decode(proj, c_k, c_v, cq, cum_t_cache[:pairs, None, :], cum_t_new[:pairs, None, :],
                            bsz=bsz, lq=length, tk=t['tk_dec'])
    sc_hist = zeros(bsz, SC_WIDTH - 1, SC_DIM) if st_sconv is None else st_sconv
    y_sc, sc_tails = _sconv(proj, p['sconv_w'][0], _hist8(sc_hist), seq_len=length, tm=t['tm_sc'])
    sconv_new = _tails(sc_tails, bsz, length, t['tm_sc'], SC_WIDTH - 1)
    cd_out = wb['cd_out']
    xf = _proj_residual(xf, y_fox, y_sc, cd_out[:FOX_WIDTH], cd_out[FOX_WIDTH:], tm=t['tm_out'], tn=t['tn_out'])

    ffn_hist1 = zeros(bsz, FFN_CONV - 1, D_FF) if st_ffn is None else st_ffn[1]
    xf, tails = _conv_ffn(xf, p['ffn_norm_w'][1], wb['ffn_gate'][1], wb['ffn_up'][1], wb['ffn_down'][1],
                          p['ffn_conv_w'][1], p['ffn_conv_b'][1], _hist8(ffn_hist1), p['final_norm_w'],
                          seq_len=length, tm=t['tm_ffn'], tf=t['tf_ffn'], final=True)
    ffn_new.append(_tails(tails, bsz, length, t['tm_ffn'], FFN_CONV - 1))

    return (xf.reshape(bsz, length, d), ret_new[None], ssd_new[None], ssd_conv_new[None], k_new[None],
            v_new[None], logf_new[None], sconv_new[None], jnp.stack(ffn_new))


def _largest_divisor(n, cap, multiple=1):
    best = None
    for cand in range(multiple, min(n, cap) + 1, multiple):
        if n % cand == 0:
            best = cand
    assert best is not None, (n, cap, multiple)
    return best


def _tiles(bsz, length, past=None):
    m = bsz * length
    seq_tile = lambda cap: _largest_divisor(length, cap, SUBLANES)
    row_tile = lambda cap: (_largest_divisor(length, cap, SUBLANES) if length >= cap
                            else _largest_divisor(m, cap, length))
    t = dict(
        tm_proj=row_tile(1024), tn_ab=512, tn_cd=1024,
        tm_out=row_tile(1024), tn_out=1024,
        tm_ffn=row_tile(512), tf_ffn=512,
        tm_sc=row_tile(512),
        c_ret=seq_tile(256), c_ssd=seq_tile(256),
    )
    if past is None:
        t['t_fox'] = seq_tile(1024)
        t['ts_fox'] = _largest_divisor(t['t_fox'], 256, LANES)
        t['t_prep'] = seq_tile(512)
    else:
        t['tk_dec'] = _largest_divisor(past, 1024, LANES)
        t['c_cum'] = _largest_divisor(past, 256, LANES)
    return t


def kernel(x_prompt, x_sample, state_ret, state_ssd, state_ssd_conv, cache_fox_k, cache_fox_v, cache_fox_logf, state_sconv, state_ffn_conv, ab_norm_w, ab_w_in, ret_norm_w, ssd_conv_w, ssd_conv_b, ssd_dt_bias, ssd_A_log, ssd_D, ssd_norm_w, ab_w_out, cd_norm_w, cd_w_in, fox_f_bias, sconv_w, cd_w_out, ffn_norm_w, ffn_w_gate, ffn_w_up, ffn_conv_w, ffn_conv_b, ffn_w_down, final_norm_w):
    p = dict(ab_norm_w=ab_norm_w, ab_w_in=ab_w_in, ret_norm_w=ret_norm_w, ssd_conv_w=ssd_conv_w,
             ssd_conv_b=ssd_conv_b, ssd_dt_bias=ssd_dt_bias, ssd_A_log=ssd_A_log, ssd_D=ssd_D,
             ssd_norm_w=ssd_norm_w, ab_w_out=ab_w_out, cd_norm_w=cd_norm_w, cd_w_in=cd_w_in,
             fox_f_bias=fox_f_bias, sconv_w=sconv_w, cd_w_out=cd_w_out, ffn_norm_w=ffn_norm_w,
             ffn_w_gate=ffn_w_gate, ffn_w_up=ffn_w_up, ffn_conv_w=ffn_conv_w, ffn_conv_b=ffn_conv_b,
             ffn_w_down=ffn_w_down, final_norm_w=final_norm_w)
    assert x_prompt.shape[-1] == D_MODEL and ab_w_in.shape == (1, D_MODEL, AB_MAIN + SSD_HEADS)
    assert cd_w_in.shape == (1, D_MODEL, CD_MAIN + FOX_HEADS) and ffn_w_gate.shape == (2, D_MODEL, D_FF)
    wb = _prep_weights(p)
    bp, lp_, _ = x_prompt.shape
    bs, ls, _ = x_sample.shape
    past = cache_fox_k.shape[2]
    (y_prompt, p_ret, p_ssd, p_ssd_conv, p_fox_k, p_fox_v, p_fox_logf, p_sconv, p_ffn_conv) = _trunk(
        x_prompt, 0, None, None, None, None, None, None, None, None, p, wb, _tiles(bp, lp_))
    (y_sample, s_ret, s_ssd, s_ssd_conv, s_fox_k, s_fox_v, s_fox_logf, s_sconv, s_ffn_conv) = _trunk(
        x_sample, past, state_ret[0], state_ssd[0], state_ssd_conv[0], cache_fox_k[0], cache_fox_v[0],
        cache_fox_logf[0], state_sconv[0], state_ffn_conv, p, wb, _tiles(bs, ls, past))
    return (y_prompt, y_sample, p_ret, s_ret, p_ssd, s_ssd, p_ssd_conv, s_ssd_conv, p_fox_k, s_fox_k,
            p_fox_v, s_fox_v, p_fox_logf, s_fox_logf, p_sconv, s_sconv, p_ffn_conv, s_ffn_conv)
```

```python
import functools
import math

import numpy as np
import jax
import jax.numpy as jnp
from jax import lax
from jax.experimental import pallas as pl
from jax.experimental.pallas import tpu as pltpu

F32 = jnp.float32
BF16 = jnp.bfloat16
EPS = 1e-6
ROPE_BASE = 10000.0
NEG_INF = float("-inf")

D_MODEL = 2048
RET_HEADS, RET_DK, RET_DV = 4, 128, 256
SSD_DINNER, SSD_HEADDIM, SSD_HEADS, SSD_GROUPS, SSD_DSTATE, SSD_CONV = 1024, 64, 16, 2, 128, 4
SSD_CONV_DIM = SSD_DINNER + 2 * SSD_GROUPS * SSD_DSTATE
FOX_HEADS, FOX_HEAD_DIM = 8, 128
FOX_WIDTH = FOX_HEADS * FOX_HEAD_DIM
SC_DIM, SC_WIDTH = 1024, 3
D_FF, FFN_CONV = 5632, 3
AB_MAIN = 2 * RET_HEADS * RET_DK + 2 * RET_HEADS * RET_DV + SSD_DINNER + SSD_CONV_DIM
AB_PAD = AB_MAIN + 128
CD_MAIN = 3 * FOX_WIDTH + 3 * SC_DIM
CD_PAD = CD_MAIN + 128

LANES = 128
SUBLANES = 8
VMEM_LIMIT = 52 * 1024 * 1024


def _cparams(n_axes):
    return pltpu.CompilerParams(dimension_semantics=("arbitrary",) * n_axes,
                                vmem_limit_bytes=VMEM_LIMIT)


def _rms(xf, w):
    return xf * lax.rsqrt(jnp.mean(xf * xf, axis=-1, keepdims=True) + EPS) * w


def _softplus(x):
    return jnp.maximum(x, 0.0) + jnp.log1p(jnp.exp(-jnp.abs(x)))


def _split3(x):
    hi = x.astype(BF16)
    r1 = x - hi.astype(F32)
    mid = r1.astype(BF16)
    lo = (r1 - mid.astype(F32)).astype(BF16)
    return hi, mid, lo


def _widen(x, n):
    return x[:, 0:n] if n <= LANES else jnp.concatenate([x] * (n // LANES), axis=1)


def _dot(a, b):
    return jnp.dot(a, b, preferred_element_type=F32)


def _dot_nt(a, b):
    return lax.dot_general(a, b, (((1,), (1,)), ((), ())), preferred_element_type=F32)


def _dot_tn(a, b):
    return lax.dot_general(a, b, (((0,), (0,)), ((), ())), preferred_element_type=F32)


def _exact_lhs_dot(m_bf16, x):
    hi, mid, lo = _split3(x)
    return _dot(m_bf16, hi) + _dot(m_bf16, mid) + _dot(m_bf16, lo)


def _exact_rhs_dot(x, m_bf16):
    hi, mid, lo = _split3(x)
    return _dot(hi, m_bf16) + _dot(mid, m_bf16) + _dot(lo, m_bf16)


def _conv_rows(ext_ref, x, prev8, w_ref, width, rows):
    ext_ref[0:SUBLANES, :] = prev8
    ext_ref[SUBLANES:SUBLANES + rows, :] = x
    out = None
    for j in range(width):
        off = SUBLANES - (width - 1) + j
        term = ext_ref[off:off + rows, :] * w_ref[j:j + 1, :]
        out = term if out is None else out + term
    return out


def _seq_tiling(seq_len, tile_rows):
    if seq_len >= tile_rows:
        assert seq_len % tile_rows == 0
        return tile_rows, 1, seq_len // tile_rows
    assert tile_rows % seq_len == 0 and seq_len % SUBLANES == 0
    return seq_len, tile_rows // seq_len, 1


def _norm_matmul_kernel(x_ref, nw_ref, w_ref, ws_ref, o_ref, os_ref, h_ref):
    @pl.when(pl.program_id(1) == 0)
    def _():
        h_ref[...] = _rms(x_ref[...], nw_ref[...]).astype(BF16)
        os_ref[...] = _dot(h_ref[...], ws_ref[...])

    o_ref[...] = _dot(h_ref[...], w_ref[...])


def _norm_matmul(x, norm_w, w, w_small, *, n, tm, tn):
    m, d = x.shape
    assert m % tm == 0 and n % tn == 0 and n <= w.shape[1] and w_small.shape == (d, LANES)
    return pl.pallas_call(
        _norm_matmul_kernel,
        grid=(m // tm, n // tn),
        in_specs=[pl.BlockSpec((tm, d), lambda i, j: (i, 0)),
                  pl.BlockSpec((1, d), lambda i, j: (0, 0)),
                  pl.BlockSpec((d, tn), lambda i, j: (0, j)),
                  pl.BlockSpec((d, LANES), lambda i, j: (0, 0))],
        out_specs=[pl.BlockSpec((tm, tn), lambda i, j: (i, j)),
                   pl.BlockSpec((tm, LANES), lambda i, j: (i, 0))],
        out_shape=[jax.ShapeDtypeStruct((m, n), F32),
                   jax.ShapeDtypeStruct((m, LANES), F32)],
        scratch_shapes=[pltpu.VMEM((tm, d), BF16)],
        compiler_params=_cparams(2),
        name="norm_in_proj",
    )(x, norm_w.reshape(1, d), w, w_small)


def _proj_res_kernel(x_ref, a_ref, b_ref, wa_ref, wb_ref, o_ref):
    acc = _dot(a_ref[...], wa_ref[...])
    acc = acc + _dot(b_ref[...], wb_ref[...])
    o_ref[...] = x_ref[...] + acc


def _proj_residual(x, a, b, w, *, tm, tn):
    m, d = x.shape
    ka, kb = a.shape[1], b.shape[1]
    assert m % tm == 0 and d % tn == 0 and ka == kb and w.shape == (ka + kb, d)
    return pl.pallas_call(
        _proj_res_kernel,
        grid=(m // tm, d // tn),
        in_specs=[pl.BlockSpec((tm, tn), lambda i, j: (i, j)),
                  pl.BlockSpec((tm, ka), lambda i, j: (i, 0)),
                  pl.BlockSpec((tm, kb), lambda i, j: (i, 0)),
                  pl.BlockSpec((ka, tn), lambda i, j: (0, j)),
                  pl.BlockSpec((kb, tn), lambda i, j: (1, j))],
        out_specs=pl.BlockSpec((tm, tn), lambda i, j: (i, j)),
        out_shape=jax.ShapeDtypeStruct((m, d), F32),
        compiler_params=_cparams(2),
        name="out_proj_residual",
    )(x, a, b, w, w)


def _ffn_kernel(x_ref, nw_ref, wg_ref, wu_ref, wd_ref, cw_ref, cb_ref, hist_ref, fw_ref,
                o_ref, tail_ref, h_ref, carry_ref, ext_ref, *, tm, rows, spt, tpb, nf, final):
    i = pl.program_id(0)
    f = pl.program_id(1)

    @pl.when(f == 0)
    def _():
        xf = x_ref[...]
        h_ref[...] = _rms(xf, nw_ref[...]).astype(BF16)
        o_ref[...] = xf

    h = h_ref[...]
    a = _dot(h, wg_ref[...])
    u = _dot(h, wu_ref[...])
    first = (i % tpb) == 0
    convs = []
    for s in range(spt):
        a_s = a[s * rows:(s + 1) * rows]
        if tpb == 1:
            prev = hist_ref[s]
        else:
            prev = jnp.where(first, hist_ref[s], carry_ref[f])
        convs.append(_conv_rows(ext_ref, a_s, prev, cw_ref, FFN_CONV, rows))
        tail_ref[s] = a_s[rows - SUBLANES:rows]
    if tpb > 1:
        carry_ref[f] = a[tm - SUBLANES:tm]
    conv = convs[0] if spt == 1 else jnp.concatenate(convs, axis=0)
    act = (jax.nn.silu(conv + cb_ref[...]) * u).astype(BF16)
    o_ref[...] += _dot(act, wd_ref[...])
    if final:
        @pl.when(f == nf - 1)
        def _():
            o_ref[...] = _rms(o_ref[...], fw_ref[...])


def _conv_ffn(x, norm_w, wg, wu, wd, conv_w, conv_b, hist8, final_w, *, layer, seq_len, tm, tf, final):
    m, d = x.shape
    ff = wg.shape[2]
    assert m % tm == 0 and ff % tf == 0
    rows, spt, tpb = _seq_tiling(seq_len, tm)
    nm, nf = m // tm, ff // tf
    hist_map = (lambda i, f: (i // tpb, 0, f)) if spt == 1 else (lambda i, f: (i, 0, f))
    kern = functools.partial(_ffn_kernel, tm=tm, rows=rows, spt=spt, tpb=tpb, nf=nf, final=final)
    out, tails = pl.pallas_call(
        kern,
        grid=(nm, nf),
        in_specs=[pl.BlockSpec((tm, d), lambda i, f: (i, 0)),
                  pl.BlockSpec((1, d), lambda i, f: (0, 0)),
                  pl.BlockSpec((None, d, tf), lambda i, f: (layer, 0, f)),
                  pl.BlockSpec((None, d, tf), lambda i, f: (layer, 0, f)),
                  pl.BlockSpec((None, tf, d), lambda i, f: (layer, f, 0)),
                  pl.BlockSpec((FFN_CONV, tf), lambda i, f: (0, f)),
                  pl.BlockSpec((1, tf), lambda i, f: (0, f)),
                  pl.BlockSpec((spt, SUBLANES, tf), hist_map),
                  pl.BlockSpec((1, d), lambda i, f: (0, 0))],
        out_specs=[pl.BlockSpec((tm, d), lambda i, f: (i, 0)),
                   pl.BlockSpec((spt, SUBLANES, tf), lambda i, f: (i, 0, f))],
        out_shape=[jax.ShapeDtypeStruct((m, d), F32),
                   jax.ShapeDtypeStruct((nm * spt, SUBLANES, ff), F32)],
        scratch_shapes=[pltpu.VMEM((tm, d), BF16),
                        pltpu.VMEM((nf, SUBLANES, tf), F32),
                        pltpu.VMEM((rows + SUBLANES, tf), F32)],
        compiler_params=_cparams(2),
        name="conv_ffn",
    )(x, norm_w.reshape(1, d), wg, wu, wd, conv_w, conv_b.reshape(1, ff), hist8, final_w.reshape(1, d))
    return out, tails


def _retention_kernel(q_ref, k_ref, v_ref, g_ref, cos_ref, sin_ref, st_ref, nw_ref,
                      y_ref, so_ref, *, c):
    ci = pl.program_id(1)

    @pl.when(ci == 0)
    def _():
        so_ref[...] = st_ref[...]

    cos = cos_ref[...]
    sin = sin_ref[...]
    ii = lax.broadcasted_iota(jnp.int32, (c, c), 0)
    jj = lax.broadcasted_iota(jnp.int32, (c, c), 1)
    diff = (ii - jj).astype(F32)
    causal = ii >= jj
    ridx = lax.broadcasted_iota(jnp.int32, (c, 1), 0).astype(F32)
    for h in range(RET_HEADS):
        lg = math.log1p(-(2.0 ** (-5.0 - h)))
        q = q_ref[:, h * RET_DK:(h + 1) * RET_DK]
        k = k_ref[:, h * RET_DK:(h + 1) * RET_DK]
        v = v_ref[:, h * RET_DV:(h + 1) * RET_DV]
        qr = q * cos + pltpu.roll(q, RET_DK // 2, 1) * sin
        kr = (k * cos + pltpu.roll(k, RET_DK // 2, 1) * sin) * (RET_DK ** -0.5)
        qb = qr.astype(BF16)
        kb = kr.astype(BF16)
        vb = v.astype(BF16)
        decay = jnp.exp(jnp.where(causal, diff * lg, NEG_INF))
        inner = jnp.exp((ridx + 1.0) * lg)
        sdecay = jnp.exp((c - 1.0 - ridx) * lg)
        s = so_ref[0, h]
        scores = _dot_nt(qb, kb) * decay
        y = _dot(scores.astype(BF16), vb)
        y = y + _dot(qb, s.astype(BF16)) * inner
        kd = (kr * sdecay).astype(BF16)
        so_ref[0, h] = math.exp(c * lg) * s + _dot_tn(kd, vb)
        mu = jnp.mean(y, axis=-1, keepdims=True)
        yc = y - mu
        var = jnp.mean(yc * yc, axis=-1, keepdims=True)
        yn = yc * lax.rsqrt(var + EPS) * nw_ref[:, h * RET_DV:(h + 1) * RET_DV]
        g = g_ref[:, h * RET_DV:(h + 1) * RET_DV]
        y_ref[:, h * RET_DV:(h + 1) * RET_DV] = (jax.nn.silu(g) * yn).astype(BF16)


def _retention(proj, cosf, sinf, state, norm_w, *, bsz, seq_len, c):
    m = proj.shape[0]
    nc = seq_len // c
    assert seq_len % c == 0
    qk_w = RET_HEADS * RET_DK
    v_w = RET_HEADS * RET_DV
    row = lambda b, ci: b * nc + ci
    y, s_new = pl.pallas_call(
        functools.partial(_retention_kernel, c=c),
        grid=(bsz, nc),
        in_specs=[pl.BlockSpec((c, qk_w), lambda b, ci: (row(b, ci), 0)),
                  pl.BlockSpec((c, qk_w), lambda b, ci: (row(b, ci), 1)),
                  pl.BlockSpec((c, v_w), lambda b, ci: (row(b, ci), 1)),
                  pl.BlockSpec((c, v_w), lambda b, ci: (row(b, ci), 2)),
                  pl.BlockSpec((c, RET_DK), lambda b, ci: (ci, 0)),
                  pl.BlockSpec((c, RET_DK), lambda b, ci: (ci, 0)),
                  pl.BlockSpec((1, RET_HEADS, RET_DK, RET_DV), lambda b, ci: (b, 0, 0, 0)),
                  pl.BlockSpec((1, v_w), lambda b, ci: (0, 0))],
        out_specs=[pl.BlockSpec((c, v_w), lambda b, ci: (row(b, ci), 0)),
                   pl.BlockSpec((1, RET_HEADS, RET_DK, RET_DV), lambda b, ci: (b, 0, 0, 0))],
        out_shape=[jax.ShapeDtypeStruct((m, v_w), BF16),
                   jax.ShapeDtypeStruct(state.shape, F32)],
        compiler_params=_cparams(2),
        name="retention",
    )(proj, proj, proj, proj, cosf, sinf, state, norm_w.reshape(1, v_w))
    return y, s_new


def _ssd_kernel(z_ref, xs_ref, bc_ref, dt_ref, hx_ref, hbc_ref, st_ref,
                cwx_ref, cwbc_ref, cbx_ref, cbbc_ref, dtb_ref, alog_ref, dsk_ref, nw_ref,
                tri_ref, exp_ref,
                y_ref, so_ref,
                st_scr, cx_scr, cbc_scr, extx_scr, extbc_scr, yh_scr, *, c, nc):
    ci = pl.program_id(1)
    gw = SSD_DINNER // SSD_GROUPS
    hpg = SSD_HEADS // SSD_GROUPS

    @pl.when(ci == 0)
    def _():
        st_scr[...] = st_ref[0].T
        cx_scr[...] = hx_ref[0]
        cbc_scr[...] = hbc_ref[0]

    xs_raw = xs_ref[...]
    bc_raw = bc_ref[...]
    xs = jax.nn.silu(_conv_rows(extx_scr, xs_raw, cx_scr[...], cwx_ref, SSD_CONV, c) + cbx_ref[...])
    bcm = jax.nn.silu(_conv_rows(extbc_scr, bc_raw, cbc_scr[...], cwbc_ref, SSD_CONV, c) + cbbc_ref[...])
    cx_scr[...] = xs_raw[c - SUBLANES:c]
    cbc_scr[...] = bc_raw[c - SUBLANES:c]

    tri = tri_ref[...]
    expand = exp_ref[...]
    dt = _softplus(dt_ref[...] + dtb_ref[...])
    a = -jnp.exp(alog_ref[...])
    acs = _exact_lhs_dot(tri, dt * a)
    acs_t = acs.T
    acs_last = acs[c - 1:c, :]
    exp_acs = jnp.exp(acs)
    to_end = jnp.exp(acs_last - acs)
    dt_e = _exact_rhs_dot(dt, expand)
    to_end_e = _exact_rhs_dot(to_end, expand)
    exp_acs_e = _exact_rhs_dot(exp_acs, expand)
    chunk_dec_e = _exact_rhs_dot(jnp.exp(acs_last), expand)

    xdt = xs * dt_e
    xdt_b = xdt.astype(BF16)
    xend_b = (xdt * to_end_e).astype(BF16)
    ii = lax.broadcasted_iota(jnp.int32, (c, c), 0)
    jj = lax.broadcasted_iota(jnp.int32, (c, c), 1)
    causal = ii >= jj
    nb = SSD_GROUPS * SSD_DSTATE
    for g in range(SSD_GROUPS):
        b_g = bcm[:, g * SSD_DSTATE:(g + 1) * SSD_DSTATE].astype(BF16)
        c_g = bcm[:, nb + g * SSD_DSTATE:nb + (g + 1) * SSD_DSTATE].astype(BF16)
        cb = _dot_nt(c_g, b_g)
        s_g = st_scr[:, g * gw:(g + 1) * gw]
        y_state = _dot(c_g, s_g.astype(BF16)) * exp_acs_e[:, g * gw:(g + 1) * gw]
        for r in range(hpg):
            hh = g * hpg + r
            seg = acs[:, hh:hh + 1] - acs_t[hh:hh + 1, :]
            lmat = jnp.exp(jnp.where(causal, seg, NEG_INF))
            mm = (cb * lmat).astype(BF16)
            lo = hh * SSD_HEADDIM
            yh_scr[:, lo:lo + SSD_HEADDIM] = (
                _dot(mm, xdt_b[:, lo:lo + SSD_HEADDIM]) + y_state[:, r * SSD_HEADDIM:(r + 1) * SSD_HEADDIM])
        upd = _dot_tn(b_g, xend_b[:, g * gw:(g + 1) * gw])
        st_scr[:, g * gw:(g + 1) * gw] = chunk_dec_e[:, g * gw:(g + 1) * gw] * s_g + upd

    y = yh_scr[...] + dsk_ref[...] * xs
    z = z_ref[...]
    y_ref[...] = _rms(y * jax.nn.silu(z), nw_ref[...]).astype(BF16)

    @pl.when(ci == nc - 1)
    def _():
        so_ref[0] = st_scr[...].T


def _ssd(proj, dt_proj, hist8, state, conv_w, conv_b, dt_bias, a_log, d_skip, norm_w, *, bsz, seq_len, c):
    m = proj.shape[0]
    nc = seq_len // c
    assert seq_len % c == 0
    row = lambda b, ci: b * nc + ci
    const2 = lambda b, ci: (0, 0)
    di, bcw = SSD_DINNER, 2 * SSD_GROUPS * SSD_DSTATE
    tri = jnp.asarray(np.tril(np.ones((c, c), np.float32)), BF16)
    expand = np.zeros((LANES, di), np.float32)
    for h in range(SSD_HEADS):
        expand[h, h * SSD_HEADDIM:(h + 1) * SSD_HEADDIM] = 1.0
    expand = jnp.asarray(expand, BF16)
    pad_row = lambda v: jnp.pad(v.astype(F32), (0, LANES - v.shape[0])).reshape(1, LANES)
    st2 = state.reshape(bsz, di, SSD_DSTATE)
    y, s_new = pl.pallas_call(
        functools.partial(_ssd_kernel, c=c, nc=nc),
        grid=(bsz, nc),
        in_specs=[pl.BlockSpec((c, di), lambda b, ci: (row(b, ci), 3)),
                  pl.BlockSpec((c, di), lambda b, ci: (row(b, ci), 4)),
                  pl.BlockSpec((c, bcw), lambda b, ci: (row(b, ci), 10)),
                  pl.BlockSpec((c, LANES), lambda b, ci: (row(b, ci), 0)),
                  pl.BlockSpec((1, SUBLANES, di), lambda b, ci: (b, 0, 0)),
                  pl.BlockSpec((1, SUBLANES, bcw), lambda b, ci: (b, 0, 2)),
                  pl.BlockSpec((1, di, SSD_DSTATE), lambda b, ci: (b, 0, 0)),
                  pl.BlockSpec((SSD_CONV, di), const2),
                  pl.BlockSpec((SSD_CONV, bcw), lambda b, ci: (0, 2)),
                  pl.BlockSpec((1, di), const2),
                  pl.BlockSpec((1, bcw), lambda b, ci: (0, 2)),
                  pl.BlockSpec((1, LANES), const2),
                  pl.BlockSpec((1, LANES), const2),
                  pl.BlockSpec((1, di), const2),
                  pl.BlockSpec((1, di), const2),
                  pl.BlockSpec((c, c), const2),
                  pl.BlockSpec((LANES, di), const2)],
        out_specs=[pl.BlockSpec((c, di), lambda b, ci: (row(b, ci), 0)),
                   pl.BlockSpec((1, di, SSD_DSTATE), lambda b, ci: (b, 0, 0))],
        out_shape=[jax.ShapeDtypeStruct((m, di), BF16),
                   jax.ShapeDtypeStruct(st2.shape, F32)],
        scratch_shapes=[pltpu.VMEM((SSD_DSTATE, di), F32),
                        pltpu.VMEM((SUBLANES, di), F32),
                        pltpu.VMEM((SUBLANES, bcw), F32),
                        pltpu.VMEM((c + SUBLANES, di), F32),
                        pltpu.VMEM((c + SUBLANES, bcw), F32),
                        pltpu.VMEM((c, di), F32)],
        compiler_params=_cparams(2),
        name="ssd",
    )(proj, proj, proj, dt_proj, hist8, hist8, st2,
      conv_w, conv_w, conv_b.reshape(1, -1), conv_b.reshape(1, -1),
      pad_row(dt_bias), pad_row(a_log), jnp.repeat(d_skip.astype(F32), SSD_HEADDIM).reshape(1, di),
      norm_w.reshape(1, di), tri, expand)
    return y, s_new.reshape(state.shape)


def _decode_cum_kernel(lfc_ref, fl_ref, b_ref, tri_ref, cumt_c_ref, lfn_ref, cumn_ref, cumt_n_ref,
                       carry_ref, *, c, ncb):
    j = pl.program_id(0)

    @pl.when(j == 0)
    def _():
        carry_ref[...] = jnp.zeros_like(carry_ref)

    @pl.when(j < ncb)
    def _():
        cum = _exact_lhs_dot(tri_ref[...], lfc_ref[...]) + carry_ref[...]
        carry_ref[...] = cum[c - 1:c, :]
        cumt_c_ref[...] = cum.T

    @pl.when(j == ncb)
    def _():
        lf = -_softplus(-(fl_ref[...] + b_ref[...]))
        lfn_ref[...] = lf
        cum = _exact_lhs_dot(tri_ref[0:LANES, 0:LANES], lf) + carry_ref[...]
        cumn_ref[...] = cum
        cumt_n_ref[...] = cum.T


def _decode_cum(cache_lf, fl_new, bias, *, c):
    past = cache_lf.shape[0]
    assert past % c == 0 and c % LANES == 0 and fl_new.shape == (LANES, LANES)
    ncb = past // c
    tri = jnp.asarray(np.tril(np.ones((c, c), np.float32)), BF16)
    blk = lambda j: jnp.minimum(j, ncb - 1)
    sq = jax.ShapeDtypeStruct((LANES, LANES), F32)
    return pl.pallas_call(
        functools.partial(_decode_cum_kernel, c=c, ncb=ncb),
        grid=(ncb + 1,),
        in_specs=[pl.BlockSpec((c, LANES), lambda j: (blk(j), 0)),
                  pl.BlockSpec((LANES, LANES), lambda j: (0, 0)),
                  pl.BlockSpec((1, LANES), lambda j: (0, 0)),
                  pl.BlockSpec((c, c), lambda j: (0, 0))],
        out_specs=[pl.BlockSpec((LANES, c), lambda j: (0, blk(j))),
                   pl.BlockSpec((LANES, LANES), lambda j: (0, 0)),
                   pl.BlockSpec((LANES, LANES), lambda j: (0, 0)),
                   pl.BlockSpec((LANES, LANES), lambda j: (0, 0))],
        out_shape=[jax.ShapeDtypeStruct((LANES, past), F32), sq, sq, sq],
        scratch_shapes=[pltpu.VMEM((1, LANES), F32)],
        compiler_params=_cparams(1),
        name="decode_logf_cumsum",
    )(cache_lf, fl_new, bias, tri)


FOX_AUG = 2 * FOX_HEAD_DIM
N_BIAS_PIECES = 3


def _fox_prep_kernel(q_ref, k_ref, v_ref, fl_ref, fb_ref, tri_ref, place_ref, ones_ref,
                     qa_ref, ka_ref, vb_ref, k32_ref, v32_ref, lf_ref, carry_ref, *, tp):
    @pl.when(pl.program_id(1) == 0)
    def _():
        carry_ref[...] = jnp.zeros_like(carry_ref)

    lf = -_softplus(-(fl_ref[...] + fb_ref[...]))
    lf_ref[...] = lf
    cum = _exact_lhs_dot(tri_ref[...], lf) + carry_ref[...]
    carry_ref[...] = cum[tp - 1:tp, :]
    pieces = _split3(cum * (FOX_HEAD_DIM ** 0.5))
    n = N_BIAS_PIECES
    aug_q = ones_ref[0:1, :] + sum(_dot(pieces[r], place_ref[r]) for r in range(n))
    aug_k = ones_ref[1:2, :] - sum(_dot(pieces[r], place_ref[n + r]) for r in range(n))
    for h in range(FOX_HEADS):
        src = slice(h * FOX_HEAD_DIM, (h + 1) * FOX_HEAD_DIM)
        feat = slice(h * FOX_AUG, h * FOX_AUG + FOX_HEAD_DIM)
        bias = slice(h * FOX_AUG + FOX_HEAD_DIM, (h + 1) * FOX_AUG)
        qa_ref[:, feat] = q_ref[:, src].astype(BF16)
        qa_ref[:, bias] = aug_q[:, src].astype(BF16)
        ka_ref[:, feat] = k_ref[:, src].astype(BF16)
        ka_ref[:, bias] = aug_k[:, src].astype(BF16)
    k = k_ref[...]
    v = v_ref[...]
    k32_ref[...] = pltpu.einshape("m(hd)->mhd", k, h=FOX_HEADS)
    v32_ref[...] = pltpu.einshape("m(hd)->mhd", v, h=FOX_HEADS)
    vb_ref[...] = v.astype(BF16)


def _fox_prep(proj, fl_proj, f_bias, *, bsz, seq_len, tp):
    m = proj.shape[0]
    nt = seq_len // tp
    assert seq_len % tp == 0
    w = FOX_WIDTH
    tri = jnp.asarray(np.tril(np.ones((tp, tp), np.float32)), BF16)
    n = N_BIAS_PIECES
    place = np.zeros((2 * n, LANES, w), np.float32)
    ones = np.zeros((SUBLANES, w), np.float32)
    for h in range(FOX_HEADS):
        for r in range(2 * n):
            place[r, h, h * FOX_HEAD_DIM + r] = 1.0
        ones[0, h * FOX_HEAD_DIM + n:h * FOX_HEAD_DIM + 2 * n] = 1.0
        ones[1, h * FOX_HEAD_DIM:h * FOX_HEAD_DIM + n] = 1.0
    row = lambda b, ti: (b * nt + ti, 0)
    const2 = lambda b, ti: (0, 0)
    return pl.pallas_call(
        functools.partial(_fox_prep_kernel, tp=tp),
        grid=(bsz, nt),
        in_specs=[pl.BlockSpec((tp, w), lambda b, ti: (b * nt + ti, 0)),
                  pl.BlockSpec((tp, w), lambda b, ti: (b * nt + ti, 1)),
                  pl.BlockSpec((tp, w), lambda b, ti: (b * nt + ti, 2)),
                  pl.BlockSpec((tp, LANES), row),
                  pl.BlockSpec((1, LANES), const2),
                  pl.BlockSpec((tp, tp), const2),
                  pl.BlockSpec((2 * n, LANES, w), lambda b, ti: (0, 0, 0)),
                  pl.BlockSpec((SUBLANES, w), const2)],
        out_specs=[pl.BlockSpec((tp, FOX_HEADS * FOX_AUG), row),
                   pl.BlockSpec((tp, FOX_HEADS * FOX_AUG), row),
                   pl.BlockSpec((tp, w), row),
                   pl.BlockSpec((tp, FOX_HEADS, FOX_HEAD_DIM), lambda b, ti: (b * nt + ti, 0, 0)),
                   pl.BlockSpec((tp, FOX_HEADS, FOX_HEAD_DIM), lambda b, ti: (b * nt + ti, 0, 0)),
                   pl.BlockSpec((tp, LANES), row)],
        out_shape=[jax.ShapeDtypeStruct((m, FOX_HEADS * FOX_AUG), BF16),
                   jax.ShapeDtypeStruct((m, FOX_HEADS * FOX_AUG), BF16),
                   jax.ShapeDtypeStruct((m, w), BF16),
                   jax.ShapeDtypeStruct((m, FOX_HEADS, FOX_HEAD_DIM), F32),
                   jax.ShapeDtypeStruct((m, FOX_HEADS, FOX_HEAD_DIM), F32),
                   jax.ShapeDtypeStruct((m, LANES), F32)],
        scratch_shapes=[pltpu.VMEM((1, LANES), F32)],
        compiler_params=_cparams(2),
        name="fox_prep",
    )(proj, proj, proj, fl_proj, f_bias, tri, jnp.asarray(place, BF16), jnp.asarray(ones, F32))


def _fox_kernel(qi_ref, ki_ref, q_ref, k_ref, v_ref, o_ref, m_ref, acc_ref, va_ref, *, t, ts):
    step = pl.program_id(2)
    qi = qi_ref[step]
    ki = ki_ref[step]
    to_log2 = (FOX_HEAD_DIM ** -0.5) * math.log2(math.e)
    hd = FOX_HEAD_DIM

    @pl.when(ki == 0)
    def _():
        m_ref[...] = jnp.full_like(m_ref, NEG_INF)
        acc_ref[...] = jnp.zeros_like(acc_ref)
        va_ref[:, hd:2 * hd] = jnp.ones((t, hd), BF16)

    va_ref[:, 0:hd] = v_ref[...]

    def scores(rows, nk, r, diagonal):
        s = _dot_nt(q_ref[rows, :], k_ref[0:nk, :])
        if diagonal:
            ri = lax.broadcasted_iota(jnp.int32, (ts, nk), 0) + r * ts
            ci = lax.broadcasted_iota(jnp.int32, (ts, nk), 1)
            s = jnp.where(ci <= ri, s, NEG_INF)
        return s

    def update(diagonal):
        for r in range(t // ts):
            rows = slice(r * ts, (r + 1) * ts)
            nk = (r + 1) * ts if diagonal else t
            s = scores(rows, nk, r, diagonal)
            m_old = m_ref[rows, :]
            m_new = jnp.maximum(m_old, jnp.max(s, axis=1, keepdims=True))
            m_ref[rows, :] = m_new
            alpha = jnp.exp2((m_old - m_new) * to_log2)
            p = jnp.exp2((s - _widen(m_new, nk)) * to_log2)
            acc_ref[rows, :] = _widen(alpha, 2 * hd) * acc_ref[rows, :] + _dot(p.astype(BF16), va_ref[0:nk, :])

    @pl.when(ki < qi)
    def _():
        update(False)

    @pl.when(ki == qi)
    def _():
        update(True)
        o_ref[...] = (acc_ref[:, 0:hd] / acc_ref[:, hd:2 * hd]).astype(BF16)


def _fox_prompt(qa, ka, vb, *, bsz, seq_len, t, ts):
    m = qa.shape[0]
    nq = seq_len // t
    assert seq_len % t == 0 and t % ts == 0
    pairs = [(qi, ki) for qi in range(nq) for ki in range(qi + 1)]
    qi_tab = jnp.asarray([p[0] for p in pairs], jnp.int32)
    ki_tab = jnp.asarray([p[1] for p in pairs], jnp.int32)
    grid_spec = pltpu.PrefetchScalarGridSpec(
        num_scalar_prefetch=2,
        grid=(bsz, FOX_HEADS, len(pairs)),
        in_specs=[pl.BlockSpec((t, FOX_AUG), lambda b, h, s, qi, ki: (b * nq + qi[s], h)),
                  pl.BlockSpec((t, FOX_AUG), lambda b, h, s, qi, ki: (b * nq + ki[s], h)),
                  pl.BlockSpec((t, FOX_HEAD_DIM), lambda b, h, s, qi, ki: (b * nq + ki[s], h))],
        out_specs=pl.BlockSpec((t, FOX_HEAD_DIM), lambda b, h, s, qi, ki: (b * nq + qi[s], h)),
        scratch_shapes=[pltpu.VMEM((t, LANES), F32), pltpu.VMEM((t, 2 * FOX_HEAD_DIM), F32),
                        pltpu.VMEM((t, 2 * FOX_HEAD_DIM), BF16)],
    )
    return pl.pallas_call(
        functools.partial(_fox_kernel, t=t, ts=ts),
        grid_spec=grid_spec,
        out_shape=jax.ShapeDtypeStruct((m, FOX_WIDTH), BF16),
        compiler_params=_cparams(3),
        name="fox_attention",
    )(qi_tab, ki_tab, qa, ka, vb)


def _fox_decode_kernel(q_ref, kn_ref, vn_ref, kc_ref, vc_ref, cq_ref, ckc_ref, ckn_ref, o_ref,
                       m_ref, l_ref, acc_ref, *, lq, ncb):
    j = pl.program_id(1)
    nh, hd = FOX_HEADS, FOX_HEAD_DIM

    @pl.when(j == 0)
    def _():
        m_ref[...] = jnp.full_like(m_ref, NEG_INF)
        l_ref[...] = jnp.zeros_like(l_ref)
        acc_ref[...] = jnp.zeros_like(acc_ref)

    def attend(k_head, v_head, ck_head, causal):
        for h in range(nh):
            qh = q_ref[:, h * hd:(h + 1) * hd].astype(BF16)
            ck = ck_head(h)
            tk = ck.shape[1]
            s = _dot_nt(qh, k_head(h).astype(BF16)) * (hd ** -0.5)
            s = s + (_widen(cq_ref[h], tk) - ck)
            if causal:
                rows = lax.broadcasted_iota(jnp.int32, (lq, tk), 0)
                cols = lax.broadcasted_iota(jnp.int32, (lq, tk), 1)
                s = jnp.where(cols <= rows, s, NEG_INF)
            m_old = m_ref[h]
            m_new = jnp.maximum(m_old, jnp.max(s, axis=1, keepdims=True))
            alpha = jnp.exp(m_old - m_new)
            p = jnp.exp(s - _widen(m_new, tk))
            l_ref[h] = alpha * l_ref[h] + jnp.sum(p, axis=1, keepdims=True)
            acc_ref[h] = alpha * acc_ref[h] + _dot(p.astype(BF16), v_head(h).astype(BF16))
            m_ref[h] = m_new

    @pl.when(j < ncb)
    def _():
        k_hm = pltpu.einshape("mhd->hmd", kc_ref[0])
        v_hm = pltpu.einshape("mhd->hmd", vc_ref[0])
        attend(lambda h: k_hm[h], lambda h: v_hm[h], lambda h: ckc_ref[h], False)

    @pl.when(j == ncb)
    def _():
        attend(lambda h: kn_ref[:, h * hd:(h + 1) * hd], lambda h: vn_ref[:, h * hd:(h + 1) * hd],
               lambda h: ckn_ref[h][:, 0:lq], True)
        for h in range(nh):
            o_ref[:, h * hd:(h + 1) * hd] = (acc_ref[h] / l_ref[h]).astype(BF16)


def _fox_decode(proj, cache_k, cache_v, cq, ck_cache, ck_new, *, bsz, lq, tk):
    past = cache_k.shape[1]
    assert past % tk == 0 and tk % LANES == 0 and lq <= LANES
    ncb = past // tk
    nh, hd, w = FOX_HEADS, FOX_HEAD_DIM, FOX_WIDTH
    tile = lambda j: jnp.minimum(j, ncb - 1)
    cache_spec = pl.BlockSpec((1, tk, nh, hd), lambda b, j: (b, tile(j), 0, 0))
    return pl.pallas_call(
        functools.partial(_fox_decode_kernel, lq=lq, ncb=ncb),
        grid=(bsz, ncb + 1),
        in_specs=[pl.BlockSpec((lq, w), lambda b, j: (b, 0)),
                  pl.BlockSpec((lq, w), lambda b, j: (b, 1)),
                  pl.BlockSpec((lq, w), lambda b, j: (b, 2)),
                  cache_spec,
                  cache_spec,
                  pl.BlockSpec((nh, lq, LANES), lambda b, j: (b, 0, 0)),
                  pl.BlockSpec((nh, 1, tk), lambda b, j: (b, 0, tile(j))),
                  pl.BlockSpec((nh, 1, LANES), lambda b, j: (b, 0, 0))],
        out_specs=pl.BlockSpec((lq, w), lambda b, j: (b, 0)),
        out_shape=jax.ShapeDtypeStruct((bsz * lq, w), BF16),
        scratch_shapes=[pltpu.VMEM((nh, lq, LANES), F32), pltpu.VMEM((nh, lq, LANES), F32),
                        pltpu.VMEM((nh, lq, hd), F32)],
        compiler_params=_cparams(2),
        name="fox_decode",
    )(proj, proj, proj, cache_k, cache_v, cq, ck_cache, ck_new)


def _sconv_kernel(u_ref, bg_ref, cg_ref, cw_ref, hist_ref, y_ref, tail_ref, carry_ref, ext_ref,
                  *, tm, rows, spt, tpb):
    i = pl.program_id(0)
    w = cg_ref[...] * u_ref[...]
    first = (i % tpb) == 0
    for s in range(spt):
        w_s = w[s * rows:(s + 1) * rows]
        if tpb == 1:
            prev = hist_ref[s]
        else:
            prev = jnp.where(first, hist_ref[s], carry_ref[...])
        conv = _conv_rows(ext_ref, w_s, prev, cw_ref, SC_WIDTH, rows)
        y_ref[s * rows:(s + 1) * rows, :] = (bg_ref[s * rows:(s + 1) * rows, :] * conv).astype(BF16)
        tail_ref[s] = w_s[rows - SUBLANES:rows]
    if tpb > 1:
        carry_ref[...] = w[tm - SUBLANES:tm]


def _sconv(proj, conv_w, hist8, *, seq_len, tm):
    m = proj.shape[0]
    assert m % tm == 0
    rows, spt, tpb = _seq_tiling(seq_len, tm)
    nm = m // tm
    hist_map = (lambda i: (i // tpb, 0, 0)) if spt == 1 else (lambda i: (i, 0, 0))
    base = 3 * FOX_WIDTH // SC_DIM
    return pl.pallas_call(
        functools.partial(_sconv_kernel, tm=tm, rows=rows, spt=spt, tpb=tpb),
        grid=(nm,),
        in_specs=[pl.BlockSpec((tm, SC_DIM), lambda i: (i, base)),
                  pl.BlockSpec((tm, SC_DIM), lambda i: (i, base + 1)),
                  pl.BlockSpec((tm, SC_DIM), lambda i: (i, base + 2)),
                  pl.BlockSpec((SC_WIDTH, SC_DIM), lambda i: (0, 0)),
                  pl.BlockSpec((spt, SUBLANES, SC_DIM), hist_map)],
        out_specs=[pl.BlockSpec((tm, SC_DIM), lambda i: (i, 0)),
                   pl.BlockSpec((spt, SUBLANES, SC_DIM), lambda i: (i, 0, 0))],
        out_shape=[jax.ShapeDtypeStruct((m, SC_DIM), BF16),
                   jax.ShapeDtypeStruct((nm * spt, SUBLANES, SC_DIM), F32)],
        scratch_shapes=[pltpu.VMEM((SUBLANES, SC_DIM), F32),
                        pltpu.VMEM((rows + SUBLANES, SC_DIM), F32)],
        compiler_params=_cparams(1),
        name="gated_short_conv",
    )(proj, proj, proj, conv_w, hist8)


def _hist8(state):
    n, w1, c = state.shape
    return jnp.concatenate([jnp.zeros((n, SUBLANES - w1, c), F32), state.astype(F32)], axis=1)


def _tails(tails, n_seq, seq_len, tile_rows, keep):
    per_seq = max(1, seq_len // tile_rows)
    idx = (jnp.arange(n_seq) + 1) * per_seq - 1
    return tails[idx][:, SUBLANES - keep:, :]


def _rope_tables(pos0, length):
    half = RET_DK // 2
    inv = ROPE_BASE ** (-jnp.arange(half, dtype=F32) / half)
    ang = (pos0 + jnp.arange(length)).astype(F32)[:, None] * inv[None, :]
    cos, sin = jnp.cos(ang), jnp.sin(ang)
    return jnp.concatenate([cos, cos], axis=1), jnp.concatenate([-sin, sin], axis=1)


def _prep_weights(p):
    d = D_MODEL
    ab_in = p['ab_w_in'][0]
    cd_in = p['cd_w_in'][0]
    f0 = 3 * FOX_WIDTH
    pad_cols = lambda w: jnp.pad(w, ((0, 0), (0, LANES - w.shape[1]))).astype(BF16)
    return dict(
        ab_in=ab_in.astype(BF16),
        ab_small=pad_cols(ab_in[:, AB_MAIN:]),
        cd_in=jnp.concatenate([cd_in[:, :f0], cd_in[:, f0 + FOX_HEADS:]], axis=1).astype(BF16),
        cd_small=pad_cols(cd_in[:, f0:f0 + FOX_HEADS]),
        ab_out=p['ab_w_out'][0].astype(BF16),
        cd_out=p['cd_w_out'][0].astype(BF16),
        ffn_gate=p['ffn_w_gate'].astype(BF16),
        ffn_up=p['ffn_w_up'].astype(BF16),
        ffn_down=p['ffn_w_down'].astype(BF16),
    )


def _trunk(x, pos0, st_ret, st_ssd, st_ssd_conv, c_k, c_v, c_logf, st_sconv, st_ffn, p, wb, t):
    bsz, length, d = x.shape
    m = bsz * length
    xf = x.reshape(m, d)
    zeros = lambda *shape: jnp.zeros(shape, F32)

    proj, dt_proj = _norm_matmul(xf, p['ab_norm_w'][0], wb['ab_in'], wb['ab_small'],
                                 n=AB_MAIN, tm=t['tm_proj'], tn=t['tn_ab'])
    cosf, sinf = _rope_tables(pos0, length)
    ret_state = zeros(bsz, RET_HEADS, RET_DK, RET_DV) if st_ret is None else st_ret
    y_ret, ret_new = _retention(proj, cosf, sinf, ret_state, p['ret_norm_w'][0],
                                bsz=bsz, seq_len=length, c=t['c_ret'])
    ssd_state = zeros(bsz, SSD_HEADS, SSD_HEADDIM, SSD_DSTATE) if st_ssd is None else st_ssd
    ssd_hist = zeros(bsz, SSD_CONV - 1, SSD_CONV_DIM) if st_ssd_conv is None else st_ssd_conv
    y_ssd, ssd_new = _ssd(proj, dt_proj, _hist8(ssd_hist), ssd_state, p['ssd_conv_w'][0], p['ssd_conv_b'][0],
                          p['ssd_dt_bias'][0], p['ssd_A_log'][0], p['ssd_D'][0], p['ssd_norm_w'][0],
                          bsz=bsz, seq_len=length, c=t['c_ssd'])
    xbc_lo = AB_MAIN - SSD_CONV_DIM
    ssd_conv_new = proj.reshape(bsz, length, -1)[:, length - (SSD_CONV - 1):, xbc_lo:AB_MAIN]
    xf = _proj_residual(xf, y_ret, y_ssd, wb['ab_out'], tm=t['tm_out'], tn=t['tn_out'])

    ffn_new = []
    ffn_hist0 = zeros(bsz, FFN_CONV - 1, D_FF) if st_ffn is None else st_ffn[0]
    xf, tails = _conv_ffn(xf, p['ffn_norm_w'][0], wb['ffn_gate'], wb['ffn_up'], wb['ffn_down'],
                          p['ffn_conv_w'][0], p['ffn_conv_b'][0], _hist8(ffn_hist0), p['final_norm_w'],
                          layer=0, seq_len=length, tm=t['tm_ffn'], tf=t['tf_ffn'], final=False)
    ffn_new.append(_tails(tails, bsz, length, t['tm_ffn'], FFN_CONV - 1))

    proj, fl_proj = _norm_matmul(xf, p['cd_norm_w'][0], wb['cd_in'], wb['cd_small'],
                                 n=CD_MAIN, tm=t['tm_proj'], tn=t['tn_cd'])
    f_bias = jnp.pad(p['fox_f_bias'][0].astype(F32), (0, LANES - FOX_HEADS)).reshape(1, LANES)
    head_shape = (bsz, length, FOX_HEADS, FOX_HEAD_DIM)
    if c_k is None:
        qa, ka, vb, k32, v32, logf = _fox_prep(proj, fl_proj, f_bias, bsz=bsz, seq_len=length, tp=t['t_prep'])
        y_fox = _fox_prompt(qa, ka, vb, bsz=bsz, seq_len=length, t=t['t_fox'], ts=t['ts_fox'])
        logf_new = logf.reshape(bsz, length, LANES)[:, :, :FOX_HEADS]
        k_new, v_new = k32.reshape(head_shape), v32.reshape(head_shape)
    else:
        proj3 = proj.reshape(bsz, length, -1)
        k_new = proj3[:, :, FOX_WIDTH:2 * FOX_WIDTH].reshape(head_shape)
        v_new = proj3[:, :, 2 * FOX_WIDTH:3 * FOX_WIDTH].reshape(head_shape)
        past = c_k.shape[1]
        pairs = bsz * FOX_HEADS
        assert pairs <= LANES and length <= LANES
        to_lanes = lambda a, rows: jnp.pad(jnp.swapaxes(a, 0, 1).reshape(a.shape[1], pairs),
                                           ((0, rows - a.shape[1]), (0, LANES - pairs)))
        from_lanes = lambda a: jnp.swapaxes(a[:length, :pairs].reshape(length, bsz, FOX_HEADS), 0, 1)
        cache_lf = to_lanes(c_logf.astype(F32), past)
        fl_rows = to_lanes(fl_proj.reshape(bsz, length, LANES)[:, :, :FOX_HEADS], LANES)
        bias_lanes = jnp.pad(jnp.tile(p['fox_f_bias'][0].astype(F32), bsz), (0, LANES - pairs)).reshape(1, LANES)
        cum_t_cache, lf_rows, cum_rows, cum_t_new = _decode_cum(cache_lf, fl_rows, bias_lanes, c=t['c_cum'])
        logf_new = from_lanes(lf_rows)
        cq = jnp.broadcast_to(cum_rows[:length, :pairs].T[:, :, None], (pairs, length, LANES))
        y_fox = _fox_decode(proj, c_k, c_v, cq, cum_t_cache[:pairs, None, :], cum_t_new[:pairs, None, :],
                            bsz=bsz, lq=length, tk=t['tk_dec'])
    sc_hist = zeros(bsz, SC_WIDTH - 1, SC_DIM) if st_sconv is None else st_sconv
    y_sc, sc_tails = _sconv(proj, p['sconv_w'][0], _hist8(sc_hist), seq_len=length, tm=t['tm_sc'])
    sconv_new = _tails(sc_tails, bsz, length, t['tm_sc'], SC_WIDTH - 1)
    xf = _proj_residual(xf, y_fox, y_sc, wb['cd_out'], tm=t['tm_out'], tn=t['tn_out'])

    ffn_hist1 = zeros(bsz, FFN_CONV - 1, D_FF) if st_ffn is None else st_ffn[1]
    xf, tails = _conv_ffn(xf, p['ffn_norm_w'][1], wb['ffn_gate'], wb['ffn_up'], wb['ffn_down'],
                          p['ffn_conv_w'][1], p['ffn_conv_b'][1], _hist8(ffn_hist1), p['final_norm_w'],
                          layer=1, seq_len=length, tm=t['tm_ffn'], tf=t['tf_ffn'], final=True)
    ffn_new.append(_tails(tails, bsz, length, t['tm_ffn'], FFN_CONV - 1))

    return (xf.reshape(bsz, length, d), ret_new[None], ssd_new[None], ssd_conv_new[None], k_new[None],
            v_new[None], logf_new[None], sconv_new[None], jnp.stack(ffn_new))


def _largest_divisor(n, cap, multiple=1):
    best = None
    for cand in range(multiple, min(n, cap) + 1, multiple):
        if n % cand == 0:
            best = cand
    assert best is not None, (n, cap, multiple)
    return best


def _tiles(bsz, length, past=None):
    m = bsz * length
    seq_tile = lambda cap: _largest_divisor(length, cap, SUBLANES)
    row_tile = lambda cap: (_largest_divisor(length, cap, SUBLANES) if length >= cap
                            else _largest_divisor(m, cap, length))
    t = dict(
        tm_proj=row_tile(1024), tn_ab=512, tn_cd=1024,
        tm_out=row_tile(1024), tn_out=1024,
        tm_ffn=row_tile(512), tf_ffn=512,
        tm_sc=row_tile(512),
        c_ret=seq_tile(256), c_ssd=seq_tile(256),
    )
    if past is None:
        t['t_fox'] = seq_tile(1024)
        t['ts_fox'] = _largest_divisor(t['t_fox'], 256, LANES)
        t['t_prep'] = seq_tile(512)
    else:
        t['tk_dec'] = _largest_divisor(past, 1024, LANES)
        t['c_cum'] = _largest_divisor(past, 256, LANES)
    return t


def kernel(x_prompt, x_sample, state_ret, state_ssd, state_ssd_conv, cache_fox_k, cache_fox_v, cache_fox_logf, state_sconv, state_ffn_conv, ab_norm_w, ab_w_in, ret_norm_w, ssd_conv_w, ssd_conv_b, ssd_dt_bias, ssd_A_log, ssd_D, ssd_norm_w, ab_w_out, cd_norm_w, cd_w_in, fox_f_bias, sconv_w, cd_w_out, ffn_norm_w, ffn_w_gate, ffn_w_up, ffn_conv_w, ffn_conv_b, ffn_w_down, final_norm_w):
    p = dict(ab_norm_w=ab_norm_w, ab_w_in=ab_w_in, ret_norm_w=ret_norm_w, ssd_conv_w=ssd_conv_w,
             ssd_conv_b=ssd_conv_b, ssd_dt_bias=ssd_dt_bias, ssd_A_log=ssd_A_log, ssd_D=ssd_D,
             ssd_norm_w=ssd_norm_w, ab_w_out=ab_w_out, cd_norm_w=cd_norm_w, cd_w_in=cd_w_in,
             fox_f_bias=fox_f_bias, sconv_w=sconv_w, cd_w_out=cd_w_out, ffn_norm_w=ffn_norm_w,
             ffn_w_gate=ffn_w_gate, ffn_w_up=ffn_w_up, ffn_conv_w=ffn_conv_w, ffn_conv_b=ffn_conv_b,
             ffn_w_down=ffn_w_down, final_norm_w=final_norm_w)
    assert x_prompt.shape[-1] == D_MODEL and ab_w_in.shape == (1, D_MODEL, AB_MAIN + SSD_HEADS)
    assert cd_w_in.shape == (1, D_MODEL, CD_MAIN + FOX_HEADS) and ffn_w_gate.shape == (2, D_MODEL, D_FF)
    wb = _prep_weights(p)
    bp, lp_, _ = x_prompt.shape
    bs, ls, _ = x_sample.shape
    past = cache_fox_k.shape[2]
    (y_prompt, p_ret, p_ssd, p_ssd_conv, p_fox_k, p_fox_v, p_fox_logf, p_sconv, p_ffn_conv) = _trunk(
        x_prompt, 0, None, None, None, None, None, None, None, None, p, wb, _tiles(bp, lp_))
    (y_sample, s_ret, s_ssd, s_ssd_conv, s_fox_k, s_fox_v, s_fox_logf, s_sconv, s_ffn_conv) = _trunk(
        x_sample, past, state_ret[0], state_ssd[0], state_ssd_conv[0], cache_fox_k[0], cache_fox_v[0],
        cache_fox_logf[0], state_sconv[0], state_ffn_conv, p, wb, _tiles(bs, ls, past))
    return (y_prompt, y_sample, p_ret, s_ret, p_ssd, s_ssd, p_ssd_conv, s_ssd_conv, p_fox_k, s_fox_k,
            p_fox_v, s_fox_v, p_fox_logf, s_fox_logf, p_sconv, s_sconv, p_ffn_conv, s_ffn_conv)
```

```python
import functools
import math

import numpy as np
import jax
import jax.numpy as jnp
from jax import lax
from jax.experimental import pallas as pl
from jax.experimental.pallas import tpu as pltpu

F32 = jnp.float32
BF16 = jnp.bfloat16
EPS = 1e-6
ROPE_BASE = 10000.0
NEG_INF = float("-inf")

D_MODEL = 2048
RET_HEADS, RET_DK, RET_DV = 4, 128, 256
SSD_DINNER, SSD_HEADDIM, SSD_HEADS, SSD_GROUPS, SSD_DSTATE, SSD_CONV = 1024, 64, 16, 2, 128, 4
SSD_CONV_DIM = SSD_DINNER + 2 * SSD_GROUPS * SSD_DSTATE
FOX_HEADS, FOX_HEAD_DIM = 8, 128
FOX_WIDTH = FOX_HEADS * FOX_HEAD_DIM
SC_DIM, SC_WIDTH = 1024, 3
D_FF, FFN_CONV = 5632, 3
AB_MAIN = 2 * RET_HEADS * RET_DK + 2 * RET_HEADS * RET_DV + SSD_DINNER + SSD_CONV_DIM
AB_PAD = AB_MAIN + 128
CD_MAIN = 3 * FOX_WIDTH + 3 * SC_DIM
CD_PAD = CD_MAIN + 128

LANES = 128
SUBLANES = 8
VMEM_LIMIT = 52 * 1024 * 1024


def _cparams(n_axes):
    return pltpu.CompilerParams(dimension_semantics=("arbitrary",) * n_axes,
                                vmem_limit_bytes=VMEM_LIMIT)


def _rms(xf, w):
    return xf * lax.rsqrt(jnp.mean(xf * xf, axis=-1, keepdims=True) + EPS) * w


def _softplus(x):
    return jnp.maximum(x, 0.0) + jnp.log1p(jnp.exp(-jnp.abs(x)))


def _split3(x):
    hi = x.astype(BF16)
    r1 = x - hi.astype(F32)
    mid = r1.astype(BF16)
    lo = (r1 - mid.astype(F32)).astype(BF16)
    return hi, mid, lo


def _widen(x, n):
    return x[:, 0:n] if n <= LANES else jnp.concatenate([x] * (n // LANES), axis=1)


def _dot(a, b):
    return jnp.dot(a, b, preferred_element_type=F32)


def _dot_nt(a, b):
    return lax.dot_general(a, b, (((1,), (1,)), ((), ())), preferred_element_type=F32)


def _dot_tn(a, b):
    return lax.dot_general(a, b, (((0,), (0,)), ((), ())), preferred_element_type=F32)


def _exact_lhs_dot(m_bf16, x):
    hi, mid, lo = _split3(x)
    return _dot(m_bf16, hi) + _dot(m_bf16, mid) + _dot(m_bf16, lo)


def _exact_rhs_dot(x, m_bf16):
    hi, mid, lo = _split3(x)
    return _dot(hi, m_bf16) + _dot(mid, m_bf16) + _dot(lo, m_bf16)


def _conv_rows(ext_ref, x, prev8, w_ref, width, rows):
    ext_ref[0:SUBLANES, :] = prev8
    ext_ref[SUBLANES:SUBLANES + rows, :] = x
    out = None
    for j in range(width):
        off = SUBLANES - (width - 1) + j
        term = ext_ref[off:off + rows, :] * w_ref[j:j + 1, :]
        out = term if out is None else out + term
    return out


def _seq_tiling(seq_len, tile_rows):
    if seq_len >= tile_rows:
        assert seq_len % tile_rows == 0
        return tile_rows, 1, seq_len // tile_rows
    assert tile_rows % seq_len == 0 and seq_len % SUBLANES == 0
    return seq_len, tile_rows // seq_len, 1


def _norm_matmul_kernel(x_ref, nw_ref, w_ref, ws_ref, o_ref, os_ref, h_ref):
    @pl.when(pl.program_id(1) == 0)
    def _():
        h_ref[...] = _rms(x_ref[...], nw_ref[...]).astype(BF16)
        os_ref[...] = _dot(h_ref[...], ws_ref[...])

    o_ref[...] = _dot(h_ref[...], w_ref[...])


def _norm_matmul(x, norm_w, w, w_small, *, n, tm, tn):
    m, d = x.shape
    assert m % tm == 0 and n % tn == 0 and n <= w.shape[1] and w_small.shape == (d, LANES)
    return pl.pallas_call(
        _norm_matmul_kernel,
        grid=(m // tm, n // tn),
        in_specs=[pl.BlockSpec((tm, d), lambda i, j: (i, 0)),
                  pl.BlockSpec((1, d), lambda i, j: (0, 0)),
                  pl.BlockSpec((d, tn), lambda i, j: (0, j)),
                  pl.BlockSpec((d, LANES), lambda i, j: (0, 0))],
        out_specs=[pl.BlockSpec((tm, tn), lambda i, j: (i, j)),
                   pl.BlockSpec((tm, LANES), lambda i, j: (i, 0))],
        out_shape=[jax.ShapeDtypeStruct((m, n), F32),
                   jax.ShapeDtypeStruct((m, LANES), F32)],
        scratch_shapes=[pltpu.VMEM((tm, d), BF16)],
        compiler_params=_cparams(2),
        name="norm_in_proj",
    )(x, norm_w.reshape(1, d), w, w_small)


def _proj_res_kernel(x_ref, a_ref, b_ref, wa_ref, wb_ref, o_ref):
    acc = _dot(a_ref[...], wa_ref[...])
    acc = acc + _dot(b_ref[...], wb_ref[...])
    o_ref[...] = x_ref[...] + acc


def _proj_residual(x, a, b, w, *, tm, tn):
    m, d = x.shape
    ka, kb = a.shape[1], b.shape[1]
    assert m % tm == 0 and d % tn == 0 and ka == kb and w.shape == (ka + kb, d)
    return pl.pallas_call(
        _proj_res_kernel,
        grid=(m // tm, d // tn),
        in_specs=[pl.BlockSpec((tm, tn), lambda i, j: (i, j)),
                  pl.BlockSpec((tm, ka), lambda i, j: (i, 0)),
                  pl.BlockSpec((tm, kb), lambda i, j: (i, 0)),
                  pl.BlockSpec((ka, tn), lambda i, j: (0, j)),
                  pl.BlockSpec((kb, tn), lambda i, j: (1, j))],
        out_specs=pl.BlockSpec((tm, tn), lambda i, j: (i, j)),
        out_shape=jax.ShapeDtypeStruct((m, d), F32),
        compiler_params=_cparams(2),
        name="out_proj_residual",
    )(x, a, b, w, w)


def _ffn_kernel(x_ref, nw_ref, wg_ref, wu_ref, wd_ref, cw_ref, cb_ref, hist_ref, fw_ref,
                o_ref, tail_ref, h_ref, carry_ref, ext_ref, *, tm, rows, spt, tpb, nf, final):
    i = pl.program_id(0)
    f = pl.program_id(1)

    @pl.when(f == 0)
    def _():
        xf = x_ref[...]
        h_ref[...] = _rms(xf, nw_ref[...]).astype(BF16)
        o_ref[...] = xf

    h = h_ref[...]
    a = _dot(h, wg_ref[...])
    u = _dot(h, wu_ref[...])
    first = (i % tpb) == 0
    convs = []
    for s in range(spt):
        a_s = a[s * rows:(s + 1) * rows]
        if tpb == 1:
            prev = hist_ref[s]
        else:
            prev = jnp.where(first, hist_ref[s], carry_ref[f])
        convs.append(_conv_rows(ext_ref, a_s, prev, cw_ref, FFN_CONV, rows))
        tail_ref[s] = a_s[rows - SUBLANES:rows]
    if tpb > 1:
        carry_ref[f] = a[tm - SUBLANES:tm]
    conv = convs[0] if spt == 1 else jnp.concatenate(convs, axis=0)
    act = (jax.nn.silu(conv + cb_ref[...]) * u).astype(BF16)
    o_ref[...] += _dot(act, wd_ref[...])
    if final:
        @pl.when(f == nf - 1)
        def _():
            o_ref[...] = _rms(o_ref[...], fw_ref[...])


def _conv_ffn(x, norm_w, wg, wu, wd, conv_w, conv_b, hist8, final_w, *, layer, seq_len, tm, tf, final):
    m, d = x.shape
    ff = wg.shape[2]
    assert m % tm == 0 and ff % tf == 0
    rows, spt, tpb = _seq_tiling(seq_len, tm)
    nm, nf = m // tm, ff // tf
    hist_map = (lambda i, f: (i // tpb, 0, f)) if spt == 1 else (lambda i, f: (i, 0, f))
    kern = functools.partial(_ffn_kernel, tm=tm, rows=rows, spt=spt, tpb=tpb, nf=nf, final=final)
    out, tails = pl.pallas_call(
        kern,
        grid=(nm, nf),
        in_specs=[pl.BlockSpec((tm, d), lambda i, f: (i, 0)),
                  pl.BlockSpec((1, d), lambda i, f: (0, 0)),
                  pl.BlockSpec((None, d, tf), lambda i, f: (layer, 0, f)),
                  pl.BlockSpec((None, d, tf), lambda i, f: (layer, 0, f)),
                  pl.BlockSpec((None, tf, d), lambda i, f: (layer, f, 0)),
                  pl.BlockSpec((FFN_CONV, tf), lambda i, f: (0, f)),
                  pl.BlockSpec((1, tf), lambda i, f: (0, f)),
                  pl.BlockSpec((spt, SUBLANES, tf), hist_map),
                  pl.BlockSpec((1, d), lambda i, f: (0, 0))],
        out_specs=[pl.BlockSpec((tm, d), lambda i, f: (i, 0)),
                   pl.BlockSpec((spt, SUBLANES, tf), lambda i, f: (i, 0, f))],
        out_shape=[jax.ShapeDtypeStruct((m, d), F32),
                   jax.ShapeDtypeStruct((nm * spt, SUBLANES, ff), F32)],
        scratch_shapes=[pltpu.VMEM((tm, d), BF16),
                        pltpu.VMEM((nf, SUBLANES, tf), F32),
                        pltpu.VMEM((rows + SUBLANES, tf), F32)],
        compiler_params=_cparams(2),
        name="conv_ffn",
    )(x, norm_w.reshape(1, d), wg, wu, wd, conv_w, conv_b.reshape(1, ff), hist8, final_w.reshape(1, d))
    return out, tails


def _retention_kernel(q_ref, k_ref, v_ref, g_ref, cos_ref, sin_ref, st_ref, nw_ref,
                      y_ref, so_ref, *, c):
    ci = pl.program_id(1)

    @pl.when(ci == 0)
    def _():
        so_ref[...] = st_ref[...]

    cos = cos_ref[...]
    sin = sin_ref[...]
    ii = lax.broadcasted_iota(jnp.int32, (c, c), 0)
    jj = lax.broadcasted_iota(jnp.int32, (c, c), 1)
    diff = (ii - jj).astype(F32)
    causal = ii >= jj
    ridx = lax.broadcasted_iota(jnp.int32, (c, 1), 0).astype(F32)
    for h in range(RET_HEADS):
        lg = math.log1p(-(2.0 ** (-5.0 - h)))
        q = q_ref[:, h * RET_DK:(h + 1) * RET_DK]
        k = k_ref[:, h * RET_DK:(h + 1) * RET_DK]
        v = v_ref[:, h * RET_DV:(h + 1) * RET_DV]
        qr = q * cos + pltpu.roll(q, RET_DK // 2, 1) * sin
        kr = (k * cos + pltpu.roll(k, RET_DK // 2, 1) * sin) * (RET_DK ** -0.5)
        qb = qr.astype(BF16)
        kb = kr.astype(BF16)
        vb = v.astype(BF16)
        decay = jnp.exp(jnp.where(causal, diff * lg, NEG_INF))
        inner = jnp.exp((ridx + 1.0) * lg)
        sdecay = jnp.exp((c - 1.0 - ridx) * lg)
        s = so_ref[0, h]
        scores = _dot_nt(qb, kb) * decay
        y = _dot(scores.astype(BF16), vb)
        y = y + _dot(qb, s.astype(BF16)) * inner
        kd = (kr * sdecay).astype(BF16)
        so_ref[0, h] = math.exp(c * lg) * s + _dot_tn(kd, vb)
        mu = jnp.mean(y, axis=-1, keepdims=True)
        yc = y - mu
        var = jnp.mean(yc * yc, axis=-1, keepdims=True)
        yn = yc * lax.rsqrt(var + EPS) * nw_ref[:, h * RET_DV:(h + 1) * RET_DV]
        g = g_ref[:, h * RET_DV:(h + 1) * RET_DV]
        y_ref[:, h * RET_DV:(h + 1) * RET_DV] = (jax.nn.silu(g) * yn).astype(BF16)


def _retention(proj, cosf, sinf, state, norm_w, *, bsz, seq_len, c):
    m = proj.shape[0]
    nc = seq_len // c
    assert seq_len % c == 0
    qk_w = RET_HEADS * RET_DK
    v_w = RET_HEADS * RET_DV
    row = lambda b, ci: b * nc + ci
    y, s_new = pl.pallas_call(
        functools.partial(_retention_kernel, c=c),
        grid=(bsz, nc),
        in_specs=[pl.BlockSpec((c, qk_w), lambda b, ci: (row(b, ci), 0)),
                  pl.BlockSpec((c, qk_w), lambda b, ci: (row(b, ci), 1)),
                  pl.BlockSpec((c, v_w), lambda b, ci: (row(b, ci), 1)),
                  pl.BlockSpec((c, v_w), lambda b, ci: (row(b, ci), 2)),
                  pl.BlockSpec((c, RET_DK), lambda b, ci: (ci, 0)),
                  pl.BlockSpec((c, RET_DK), lambda b, ci: (ci, 0)),
                  pl.BlockSpec((1, RET_HEADS, RET_DK, RET_DV), lambda b, ci: (b, 0, 0, 0)),
                  pl.BlockSpec((1, v_w), lambda b, ci: (0, 0))],
        out_specs=[pl.BlockSpec((c, v_w), lambda b, ci: (row(b, ci), 0)),
                   pl.BlockSpec((1, RET_HEADS, RET_DK, RET_DV), lambda b, ci: (b, 0, 0, 0))],
        out_shape=[jax.ShapeDtypeStruct((m, v_w), BF16),
                   jax.ShapeDtypeStruct(state.shape, F32)],
        compiler_params=_cparams(2),
        name="retention",
    )(proj, proj, proj, proj, cosf, sinf, state, norm_w.reshape(1, v_w))
    return y, s_new


def _ssd_kernel(z_ref, xs_ref, bc_ref, dt_ref, hx_ref, hbc_ref, st_ref,
                cwx_ref, cwbc_ref, cbx_ref, cbbc_ref, dtb_ref, alog_ref, dsk_ref, nw_ref,
                tri_ref, exp_ref,
                y_ref, so_ref,
                st_scr, cx_scr, cbc_scr, extx_scr, extbc_scr, yh_scr, *, c, nc):
    ci = pl.program_id(1)
    gw = SSD_DINNER // SSD_GROUPS
    hpg = SSD_HEADS // SSD_GROUPS

    @pl.when(ci == 0)
    def _():
        st_scr[...] = st_ref[0].T
        cx_scr[...] = hx_ref[0]
        cbc_scr[...] = hbc_ref[0]

    xs_raw = xs_ref[...]
    bc_raw = bc_ref[...]
    xs = jax.nn.silu(_conv_rows(extx_scr, xs_raw, cx_scr[...], cwx_ref, SSD_CONV, c) + cbx_ref[...])
    bcm = jax.nn.silu(_conv_rows(extbc_scr, bc_raw, cbc_scr[...], cwbc_ref, SSD_CONV, c) + cbbc_ref[...])
    cx_scr[...] = xs_raw[c - SUBLANES:c]
    cbc_scr[...] = bc_raw[c - SUBLANES:c]

    tri = tri_ref[...]
    expand = exp_ref[...]
    dt = _softplus(dt_ref[...] + dtb_ref[...])
    a = -jnp.exp(alog_ref[...])
    acs = _exact_lhs_dot(tri, dt * a)
    acs_t = acs.T
    acs_last = acs[c - 1:c, :]
    exp_acs = jnp.exp(acs)
    to_end = jnp.exp(acs_last - acs)
    dt_e = _exact_rhs_dot(dt, expand)
    to_end_e = _exact_rhs_dot(to_end, expand)
    exp_acs_e = _exact_rhs_dot(exp_acs, expand)
    chunk_dec_e = _exact_rhs_dot(jnp.exp(acs_last), expand)

    xdt = xs * dt_e
    xdt_b = xdt.astype(BF16)
    xend_b = (xdt * to_end_e).astype(BF16)
    ii = lax.broadcasted_iota(jnp.int32, (c, c), 0)
    jj = lax.broadcasted_iota(jnp.int32, (c, c), 1)
    causal = ii >= jj
    nb = SSD_GROUPS * SSD_DSTATE
    for g in range(SSD_GROUPS):
        b_g = bcm[:, g * SSD_DSTATE:(g + 1) * SSD_DSTATE].astype(BF16)
        c_g = bcm[:, nb + g * SSD_DSTATE:nb + (g + 1) * SSD_DSTATE].astype(BF16)
        cb = _dot_nt(c_g, b_g)
        s_g = st_scr[:, g * gw:(g + 1) * gw]
        y_state = _dot(c_g, s_g.astype(BF16)) * exp_acs_e[:, g * gw:(g + 1) * gw]
        for r in range(hpg):
            hh = g * hpg + r
            seg = acs[:, hh:hh + 1] - acs_t[hh:hh + 1, :]
            lmat = jnp.exp(jnp.where(causal, seg, NEG_INF))
            mm = (cb * lmat).astype(BF16)
            lo = hh * SSD_HEADDIM
            yh_scr[:, lo:lo + SSD_HEADDIM] = (
                _dot(mm, xdt_b[:, lo:lo + SSD_HEADDIM]) + y_state[:, r * SSD_HEADDIM:(r + 1) * SSD_HEADDIM])
        upd = _dot_tn(b_g, xend_b[:, g * gw:(g + 1) * gw])
        st_scr[:, g * gw:(g + 1) * gw] = chunk_dec_e[:, g * gw:(g + 1) * gw] * s_g + upd

    y = yh_scr[...] + dsk_ref[...] * xs
    z = z_ref[...]
    y_ref[...] = _rms(y * jax.nn.silu(z), nw_ref[...]).astype(BF16)

    @pl.when(ci == nc - 1)
    def _():
        so_ref[0] = st_scr[...].T


def _ssd(proj, dt_proj, hist8, state, conv_w, conv_b, dt_bias, a_log, d_skip, norm_w, *, bsz, seq_len, c):
    m = proj.shape[0]
    nc = seq_len // c
    assert seq_len % c == 0
    row = lambda b, ci: b * nc + ci
    const2 = lambda b, ci: (0, 0)
    di, bcw = SSD_DINNER, 2 * SSD_GROUPS * SSD_DSTATE
    tri = jnp.asarray(np.tril(np.ones((c, c), np.float32)), BF16)
    expand = np.zeros((LANES, di), np.float32)
    for h in range(SSD_HEADS):
        expand[h, h * SSD_HEADDIM:(h + 1) * SSD_HEADDIM] = 1.0
    expand = jnp.asarray(expand, BF16)
    pad_row = lambda v: jnp.pad(v.astype(F32), (0, LANES - v.shape[0])).reshape(1, LANES)
    st2 = state.reshape(bsz, di, SSD_DSTATE)
    y, s_new = pl.pallas_call(
        functools.partial(_ssd_kernel, c=c, nc=nc),
        grid=(bsz, nc),
        in_specs=[pl.BlockSpec((c, di), lambda b, ci: (row(b, ci), 3)),
                  pl.BlockSpec((c, di), lambda b, ci: (row(b, ci), 4)),
                  pl.BlockSpec((c, bcw), lambda b, ci: (row(b, ci), 10)),
                  pl.BlockSpec((c, LANES), lambda b, ci: (row(b, ci), 0)),
                  pl.BlockSpec((1, SUBLANES, di), lambda b, ci: (b, 0, 0)),
                  pl.BlockSpec((1, SUBLANES, bcw), lambda b, ci: (b, 0, 2)),
                  pl.BlockSpec((1, di, SSD_DSTATE), lambda b, ci: (b, 0, 0)),
                  pl.BlockSpec((SSD_CONV, di), const2),
                  pl.BlockSpec((SSD_CONV, bcw), lambda b, ci: (0, 2)),
                  pl.BlockSpec((1, di), const2),
                  pl.BlockSpec((1, bcw), lambda b, ci: (0, 2)),
                  pl.BlockSpec((1, LANES), const2),
                  pl.BlockSpec((1, LANES), const2),
                  pl.BlockSpec((1, di), const2),
                  pl.BlockSpec((1, di), const2),
                  pl.BlockSpec((c, c), const2),
                  pl.BlockSpec((LANES, di), const2)],
        out_specs=[pl.BlockSpec((c, di), lambda b, ci: (row(b, ci), 0)),
                   pl.BlockSpec((1, di, SSD_DSTATE), lambda b, ci: (b, 0, 0))],
        out_shape=[jax.ShapeDtypeStruct((m, di), BF16),
                   jax.ShapeDtypeStruct(st2.shape, F32)],
        scratch_shapes=[pltpu.VMEM((SSD_DSTATE, di), F32),
                        pltpu.VMEM((SUBLANES, di), F32),
                        pltpu.VMEM((SUBLANES, bcw), F32),
                        pltpu.VMEM((c + SUBLANES, di), F32),
                        pltpu.VMEM((c + SUBLANES, bcw), F32),
                        pltpu.VMEM((c, di), F32)],
        compiler_params=_cparams(2),
        name="ssd",
    )(proj, proj, proj, dt_proj, hist8, hist8, st2,
      conv_w, conv_w, conv_b.reshape(1, -1), conv_b.reshape(1, -1),
      pad_row(dt_bias), pad_row(a_log), jnp.repeat(d_skip.astype(F32), SSD_HEADDIM).reshape(1, di),
      norm_w.reshape(1, di), tri, expand)
    return y, s_new.reshape(state.shape)


def _decode_cum_kernel(lfc_ref, fl_ref, b_ref, tri_ref, cumt_c_ref, lfn_ref, cumn_ref, cumt_n_ref,
                       carry_ref, *, c, ncb):
    j = pl.program_id(0)

    @pl.when(j == 0)
    def _():
        carry_ref[...] = jnp.zeros_like(carry_ref)

    @pl.when(j < ncb)
    def _():
        cum = _exact_lhs_dot(tri_ref[...], lfc_ref[...]) + carry_ref[...]
        carry_ref[...] = cum[c - 1:c, :]
        cumt_c_ref[...] = cum.T

    @pl.when(j == ncb)
    def _():
        lf = -_softplus(-(fl_ref[...] + b_ref[...]))
        lfn_ref[...] = lf
        cum = _exact_lhs_dot(tri_ref[0:LANES, 0:LANES], lf) + carry_ref[...]
        cumn_ref[...] = cum
        cumt_n_ref[...] = cum.T


def _decode_cum(cache_lf, fl_new, bias, *, c):
    past = cache_lf.shape[0]
    assert past % c == 0 and c % LANES == 0 and fl_new.shape == (LANES, LANES)
    ncb = past // c
    tri = jnp.asarray(np.tril(np.ones((c, c), np.float32)), BF16)
    blk = lambda j: jnp.minimum(j, ncb - 1)
    sq = jax.ShapeDtypeStruct((LANES, LANES), F32)
    return pl.pallas_call(
        functools.partial(_decode_cum_kernel, c=c, ncb=ncb),
        grid=(ncb + 1,),
        in_specs=[pl.BlockSpec((c, LANES), lambda j: (blk(j), 0)),
                  pl.BlockSpec((LANES, LANES), lambda j: (0, 0)),
                  pl.BlockSpec((1, LANES), lambda j: (0, 0)),
                  pl.BlockSpec((c, c), lambda j: (0, 0))],
        out_specs=[pl.BlockSpec((LANES, c), lambda j: (0, blk(j))),
                   pl.BlockSpec((LANES, LANES), lambda j: (0, 0)),
                   pl.BlockSpec((LANES, LANES), lambda j: (0, 0)),
                   pl.BlockSpec((LANES, LANES), lambda j: (0, 0))],
        out_shape=[jax.ShapeDtypeStruct((LANES, past), F32), sq, sq, sq],
        scratch_shapes=[pltpu.VMEM((1, LANES), F32)],
        compiler_params=_cparams(1),
        name="decode_logf_cumsum",
    )(cache_lf, fl_new, bias, tri)


FOX_AUG = 2 * FOX_HEAD_DIM
N_BIAS_PIECES = 3


def _fox_prep_kernel(q_ref, k_ref, v_ref, fl_ref, fb_ref, tri_ref, place_ref, ones_ref,
                     qa_ref, ka_ref, vb_ref, k32_ref, v32_ref, lf_ref, carry_ref, *, tp):
    @pl.when(pl.program_id(1) == 0)
    def _():
        carry_ref[...] = jnp.zeros_like(carry_ref)

    lf = -_softplus(-(fl_ref[...] + fb_ref[...]))
    lf_ref[...] = lf
    cum = _exact_lhs_dot(tri_ref[...], lf) + carry_ref[...]
    carry_ref[...] = cum[tp - 1:tp, :]
    pieces = _split3(cum * (FOX_HEAD_DIM ** 0.5))
    n = N_BIAS_PIECES
    aug_q = ones_ref[0:1, :] + sum(_dot(pieces[r], place_ref[r]) for r in range(n))
    aug_k = ones_ref[1:2, :] - sum(_dot(pieces[r], place_ref[n + r]) for r in range(n))
    for h in range(FOX_HEADS):
        src = slice(h * FOX_HEAD_DIM, (h + 1) * FOX_HEAD_DIM)
        feat = slice(h * FOX_AUG, h * FOX_AUG + FOX_HEAD_DIM)
        bias = slice(h * FOX_AUG + FOX_HEAD_DIM, (h + 1) * FOX_AUG)
        qa_ref[:, feat] = q_ref[:, src].astype(BF16)
        qa_ref[:, bias] = aug_q[:, src].astype(BF16)
        ka_ref[:, feat] = k_ref[:, src].astype(BF16)
        ka_ref[:, bias] = aug_k[:, src].astype(BF16)
    k = k_ref[...]
    v = v_ref[...]
    k32_ref[...] = pltpu.einshape("m(hd)->mhd", k, h=FOX_HEADS)
    v32_ref[...] = pltpu.einshape("m(hd)->mhd", v, h=FOX_HEADS)
    vb_ref[...] = v.astype(BF16)


def _fox_prep(proj, fl_proj, f_bias, *, bsz, seq_len, tp):
    m = proj.shape[0]
    nt = seq_len // tp
    assert seq_len % tp == 0
    w = FOX_WIDTH
    tri = jnp.asarray(np.tril(np.ones((tp, tp), np.float32)), BF16)
    n = N_BIAS_PIECES
    place = np.zeros((2 * n, LANES, w), np.float32)
    ones = np.zeros((SUBLANES, w), np.float32)
    for h in range(FOX_HEADS):
        for r in range(2 * n):
            place[r, h, h * FOX_HEAD_DIM + r] = 1.0
        ones[0, h * FOX_HEAD_DIM + n:h * FOX_HEAD_DIM + 2 * n] = 1.0
        ones[1, h * FOX_HEAD_DIM:h * FOX_HEAD_DIM + n] = 1.0
    row = lambda b, ti: (b * nt + ti, 0)
    const2 = lambda b, ti: (0, 0)
    return pl.pallas_call(
        functools.partial(_fox_prep_kernel, tp=tp),
        grid=(bsz, nt),
        in_specs=[pl.BlockSpec((tp, w), lambda b, ti: (b * nt + ti, 0)),
                  pl.BlockSpec((tp, w), lambda b, ti: (b * nt + ti, 1)),
                  pl.BlockSpec((tp, w), lambda b, ti: (b * nt + ti, 2)),
                  pl.BlockSpec((tp, LANES), row),
                  pl.BlockSpec((1, LANES), const2),
                  pl.BlockSpec((tp, tp), const2),
                  pl.BlockSpec((2 * n, LANES, w), lambda b, ti: (0, 0, 0)),
                  pl.BlockSpec((SUBLANES, w), const2)],
        out_specs=[pl.BlockSpec((tp, FOX_HEADS * FOX_AUG), row),
                   pl.BlockSpec((tp, FOX_HEADS * FOX_AUG), row),
                   pl.BlockSpec((tp, w), row),
                   pl.BlockSpec((tp, FOX_HEADS, FOX_HEAD_DIM), lambda b, ti: (b * nt + ti, 0, 0)),
                   pl.BlockSpec((tp, FOX_HEADS, FOX_HEAD_DIM), lambda b, ti: (b * nt + ti, 0, 0)),
                   pl.BlockSpec((tp, LANES), row)],
        out_shape=[jax.ShapeDtypeStruct((m, FOX_HEADS * FOX_AUG), BF16),
                   jax.ShapeDtypeStruct((m, FOX_HEADS * FOX_AUG), BF16),
                   jax.ShapeDtypeStruct((m, w), BF16),
                   jax.ShapeDtypeStruct((m, FOX_HEADS, FOX_HEAD_DIM), F32),
                   jax.ShapeDtypeStruct((m, FOX_HEADS, FOX_HEAD_DIM), F32),
                   jax.ShapeDtypeStruct((m, LANES), F32)],
        scratch_shapes=[pltpu.VMEM((1, LANES), F32)],
        compiler_params=_cparams(2),
        name="fox_prep",
    )(proj, proj, proj, fl_proj, f_bias, tri, jnp.asarray(place, BF16), jnp.asarray(ones, F32))


def _fox_kernel(qi_ref, ki_ref, q_ref, k_ref, v_ref, o_ref, m_ref, acc_ref, va_ref, *, t, ts):
    step = pl.program_id(2)
    qi = qi_ref[step]
    ki = ki_ref[step]
    to_log2 = (FOX_HEAD_DIM ** -0.5) * math.log2(math.e)
    hd = FOX_HEAD_DIM

    @pl.when(ki == 0)
    def _():
        m_ref[...] = jnp.full_like(m_ref, NEG_INF)
        acc_ref[...] = jnp.zeros_like(acc_ref)
        va_ref[:, hd:2 * hd] = jnp.ones((t, hd), BF16)

    va_ref[:, 0:hd] = v_ref[...]

    def scores(rows, nk, r, diagonal):
        s = _dot_nt(q_ref[rows, :], k_ref[0:nk, :])
        if diagonal:
            ri = lax.broadcasted_iota(jnp.int32, (ts, nk), 0) + r * ts
            ci = lax.broadcasted_iota(jnp.int32, (ts, nk), 1)
            s = jnp.where(ci <= ri, s, NEG_INF)
        return s

    def update(diagonal):
        for r in range(t // ts):
            rows = slice(r * ts, (r + 1) * ts)
            nk = (r + 1) * ts if diagonal else t
            s = scores(rows, nk, r, diagonal)
            m_old = m_ref[rows, :]
            m_new = jnp.maximum(m_old, jnp.max(s, axis=1, keepdims=True))
            m_ref[rows, :] = m_new
            alpha = jnp.exp2((m_old - m_new) * to_log2)
            p = jnp.exp2((s - _widen(m_new, nk)) * to_log2)
            acc_ref[rows, :] = _widen(alpha, 2 * hd) * acc_ref[rows, :] + _dot(p.astype(BF16), va_ref[0:nk, :])

    @pl.when(ki < qi)
    def _():
        update(False)

    @pl.when(ki == qi)
    def _():
        update(True)
        o_ref[...] = (acc_ref[:, 0:hd] / acc_ref[:, hd:2 * hd]).astype(BF16)


def _fox_prompt(qa, ka, vb, *, bsz, seq_len, t, ts):
    m = qa.shape[0]
    nq = seq_len // t
    assert seq_len % t == 0 and t % ts == 0
    pairs = [(qi, ki) for qi in range(nq) for ki in range(qi + 1)]
    qi_tab = jnp.asarray([p[0] for p in pairs], jnp.int32)
    ki_tab = jnp.asarray([p[1] for p in pairs], jnp.int32)
    grid_spec = pltpu.PrefetchScalarGridSpec(
        num_scalar_prefetch=2,
        grid=(bsz, FOX_HEADS, len(pairs)),
        in_specs=[pl.BlockSpec((t, FOX_AUG), lambda b, h, s, qi, ki: (b * nq + qi[s], h)),
                  pl.BlockSpec((t, FOX_AUG), lambda b, h, s, qi, ki: (b * nq + ki[s], h)),
                  pl.BlockSpec((t, FOX_HEAD_DIM), lambda b, h, s, qi, ki: (b * nq + ki[s], h))],
        out_specs=pl.BlockSpec((t, FOX_HEAD_DIM), lambda b, h, s, qi, ki: (b * nq + qi[s], h)),
        scratch_shapes=[pltpu.VMEM((t, LANES), F32), pltpu.VMEM((t, 2 * FOX_HEAD_DIM), F32),
                        pltpu.VMEM((t, 2 * FOX_HEAD_DIM), BF16)],
    )
    return pl.pallas_call(
        functools.partial(_fox_kernel, t=t, ts=ts),
        grid_spec=grid_spec,
        out_shape=jax.ShapeDtypeStruct((m, FOX_WIDTH), BF16),
        compiler_params=_cparams(3),
        name="fox_attention",
    )(qi_tab, ki_tab, qa, ka, vb)


def _fox_decode_kernel(q_ref, kn_ref, vn_ref, kc_ref, vc_ref, cq_ref, ckc_ref, ckn_ref, o_ref,
                       m_ref, l_ref, acc_ref, *, lq, ncb):
    j = pl.program_id(1)
    nh, hd = FOX_HEADS, FOX_HEAD_DIM

    @pl.when(j == 0)
    def _():
        m_ref[...] = jnp.full_like(m_ref, NEG_INF)
        l_ref[...] = jnp.zeros_like(l_ref)
        acc_ref[...] = jnp.zeros_like(acc_ref)

    def attend(k_head, v_head, ck_head, causal):
        for h in range(nh):
            qh = q_ref[:, h * hd:(h + 1) * hd].astype(BF16)
            ck = ck_head(h)
            tk = ck.shape[1]
            s = _dot_nt(qh, k_head(h).astype(BF16)) * (hd ** -0.5)
            s = s + (_widen(cq_ref[h], tk) - ck)
            if causal:
                rows = lax.broadcasted_iota(jnp.int32, (lq, tk), 0)
                cols = lax.broadcasted_iota(jnp.int32, (lq, tk), 1)
                s = jnp.where(cols <= rows, s, NEG_INF)
            m_old = m_ref[h]
            m_new = jnp.maximum(m_old, jnp.max(s, axis=1, keepdims=True))
            alpha = jnp.exp(m_old - m_new)
            p = jnp.exp(s - _widen(m_new, tk))
            l_ref[h] = alpha * l_ref[h] + jnp.sum(p, axis=1, keepdims=True)
            acc_ref[h] = alpha * acc_ref[h] + _dot(p.astype(BF16), v_head(h).astype(BF16))
            m_ref[h] = m_new

    @pl.when(j < ncb)
    def _():
        k_hm = pltpu.einshape("mhd->hmd", kc_ref[0])
        v_hm = pltpu.einshape("mhd->hmd", vc_ref[0])
        attend(lambda h: k_hm[h], lambda h: v_hm[h], lambda h: ckc_ref[h], False)

    @pl.when(j == ncb)
    def _():
        attend(lambda h: kn_ref[:, h * hd:(h + 1) * hd], lambda h: vn_ref[:, h * hd:(h + 1) * hd],
               lambda h: ckn_ref[h][:, 0:lq], True)
        for h in range(nh):
            o_ref[:, h * hd:(h + 1) * hd] = (acc_ref[h] / l_ref[h]).astype(BF16)


def _fox_decode(proj, cache_k, cache_v, cq, ck_cache, ck_new, *, bsz, lq, tk):
    past = cache_k.shape[1]
    assert past % tk == 0 and tk % LANES == 0 and lq <= LANES
    ncb = past // tk
    nh, hd, w = FOX_HEADS, FOX_HEAD_DIM, FOX_WIDTH
    tile = lambda j: jnp.minimum(j, ncb - 1)
    cache_spec = pl.BlockSpec((1, tk, nh, hd), lambda b, j: (b, tile(j), 0, 0))
    return pl.pallas_call(
        functools.partial(_fox_decode_kernel, lq=lq, ncb=ncb),
        grid=(bsz, ncb + 1),
        in_specs=[pl.BlockSpec((lq, w), lambda b, j: (b, 0)),
                  pl.BlockSpec((lq, w), lambda b, j: (b, 1)),
                  pl.BlockSpec((lq, w), lambda b, j: (b, 2)),
                  cache_spec,
                  cache_spec,
                  pl.BlockSpec((nh, lq, LANES), lambda b, j: (b, 0, 0)),
                  pl.BlockSpec((nh, 1, tk), lambda b, j: (b, 0, tile(j))),
                  pl.BlockSpec((nh, 1, LANES), lambda b, j: (b, 0, 0))],
        out_specs=pl.BlockSpec((lq, w), lambda b, j: (b, 0)),
        out_shape=jax.ShapeDtypeStruct((bsz * lq, w), BF16),
        scratch_shapes=[pltpu.VMEM((nh, lq, LANES), F32), pltpu.VMEM((nh, lq, LANES), F32),
                        pltpu.VMEM((nh, lq, hd), F32)],
        compiler_params=_cparams(2),
        name="fox_decode",
    )(proj, proj, proj, cache_k, cache_v, cq, ck_cache, ck_new)


def _sconv_kernel(u_ref, bg_ref, cg_ref, cw_ref, hist_ref, y_ref, tail_ref, carry_ref, ext_ref,
                  *, tm, rows, spt, tpb):
    i = pl.program_id(0)
    w = cg_ref[...] * u_ref[...]
    first = (i % tpb) == 0
    for s in range(spt):
        w_s = w[s * rows:(s + 1) * rows]
        if tpb == 1:
            prev = hist_ref[s]
        else:
            prev = jnp.where(first, hist_ref[s], carry_ref[...])
        conv = _conv_rows(ext_ref, w_s, prev, cw_ref, SC_WIDTH, rows)
        y_ref[s * rows:(s + 1) * rows, :] = (bg_ref[s * rows:(s + 1) * rows, :] * conv).astype(BF16)
        tail_ref[s] = w_s[rows - SUBLANES:rows]
    if tpb > 1:
        carry_ref[...] = w[tm - SUBLANES:tm]


def _sconv(proj, conv_w, hist8, *, seq_len, tm):
    m = proj.shape[0]
    assert m % tm == 0
    rows, spt, tpb = _seq_tiling(seq_len, tm)
    nm = m // tm
    hist_map = (lambda i: (i // tpb, 0, 0)) if spt == 1 else (lambda i: (i, 0, 0))
    base = 3 * FOX_WIDTH // SC_DIM
    return pl.pallas_call(
        functools.partial(_sconv_kernel, tm=tm, rows=rows, spt=spt, tpb=tpb),
        grid=(nm,),
        in_specs=[pl.BlockSpec((tm, SC_DIM), lambda i: (i, base)),
                  pl.BlockSpec((tm, SC_DIM), lambda i: (i, base + 1)),
                  pl.BlockSpec((tm, SC_DIM), lambda i: (i, base + 2)),
                  pl.BlockSpec((SC_WIDTH, SC_DIM), lambda i: (0, 0)),
                  pl.BlockSpec((spt, SUBLANES, SC_DIM), hist_map)],
        out_specs=[pl.BlockSpec((tm, SC_DIM), lambda i: (i, 0)),
                   pl.BlockSpec((spt, SUBLANES, SC_DIM), lambda i: (i, 0, 0))],
        out_shape=[jax.ShapeDtypeStruct((m, SC_DIM), BF16),
                   jax.ShapeDtypeStruct((nm * spt, SUBLANES, SC_DIM), F32)],
        scratch_shapes=[pltpu.VMEM((SUBLANES, SC_DIM), F32),
                        pltpu.VMEM((rows + SUBLANES, SC_DIM), F32)],
        compiler_params=_cparams(1),
        name="gated_short_conv",
    )(proj, proj, proj, conv_w, hist8)


def _hist8(state):
    n, w1, c = state.shape
    return jnp.concatenate([jnp.zeros((n, SUBLANES - w1, c), F32), state.astype(F32)], axis=1)


def _tails(tails, n_seq, seq_len, tile_rows, keep):
    per_seq = max(1, seq_len // tile_rows)
    idx = (jnp.arange(n_seq) + 1) * per_seq - 1
    return tails[idx][:, SUBLANES - keep:, :]


def _rope_tables(pos0, length):
    half = RET_DK // 2
    inv = ROPE_BASE ** (-np.arange(half, dtype=np.float64) / half)
    ang = (pos0 + np.arange(length, dtype=np.float64))[:, None] * inv[None, :]
    cos, sin = np.cos(ang), np.sin(ang)
    return (jnp.asarray(np.concatenate([cos, cos], axis=1), F32),
            jnp.asarray(np.concatenate([-sin, sin], axis=1), F32))


def _prep_weights(p):
    d = D_MODEL
    ab_in = p['ab_w_in'][0]
    cd_in = p['cd_w_in'][0]
    f0 = 3 * FOX_WIDTH
    pad_cols = lambda w: jnp.pad(w, ((0, 0), (0, LANES - w.shape[1]))).astype(BF16)
    return dict(
        ab_in=ab_in.astype(BF16),
        ab_small=pad_cols(ab_in[:, AB_MAIN:]),
        cd_in=jnp.concatenate([cd_in[:, :f0], cd_in[:, f0 + FOX_HEADS:]], axis=1).astype(BF16),
        cd_small=pad_cols(cd_in[:, f0:f0 + FOX_HEADS]),
        ab_out=p['ab_w_out'][0].astype(BF16),
        cd_out=p['cd_w_out'][0].astype(BF16),
        ffn_gate=p['ffn_w_gate'].astype(BF16),
        ffn_up=p['ffn_w_up'].astype(BF16),
        ffn_down=p['ffn_w_down'].astype(BF16),
    )


def _trunk(x, pos0, st_ret, st_ssd, st_ssd_conv, c_k, c_v, c_logf, st_sconv, st_ffn, p, wb, t):
    bsz, length, d = x.shape
    m = bsz * length
    xf = x.reshape(m, d)
    zeros = lambda *shape: jnp.zeros(shape, F32)

    proj, dt_proj = _norm_matmul(xf, p['ab_norm_w'][0], wb['ab_in'], wb['ab_small'],
                                 n=AB_MAIN, tm=t['tm_proj'], tn=t['tn_ab'])
    cosf, sinf = _rope_tables(pos0, length)
    ret_state = zeros(bsz, RET_HEADS, RET_DK, RET_DV) if st_ret is None else st_ret
    y_ret, ret_new = _retention(proj, cosf, sinf, ret_state, p['ret_norm_w'][0],
                                bsz=bsz, seq_len=length, c=t['c_ret'])
    ssd_state = zeros(bsz, SSD_HEADS, SSD_HEADDIM, SSD_DSTATE) if st_ssd is None else st_ssd
    ssd_hist = zeros(bsz, SSD_CONV - 1, SSD_CONV_DIM) if st_ssd_conv is None else st_ssd_conv
    y_ssd, ssd_new = _ssd(proj, dt_proj, _hist8(ssd_hist), ssd_state, p['ssd_conv_w'][0], p['ssd_conv_b'][0],
                          p['ssd_dt_bias'][0], p['ssd_A_log'][0], p['ssd_D'][0], p['ssd_norm_w'][0],
                          bsz=bsz, seq_len=length, c=t['c_ssd'])
    xbc_lo = AB_MAIN - SSD_CONV_DIM
    ssd_conv_new = proj.reshape(bsz, length, -1)[:, length - (SSD_CONV - 1):, xbc_lo:AB_MAIN]
    xf = _proj_residual(xf, y_ret, y_ssd, wb['ab_out'], tm=t['tm_out'], tn=t['tn_out'])

    ffn_new = []
    ffn_hist0 = zeros(bsz, FFN_CONV - 1, D_FF) if st_ffn is None else st_ffn[0]
    xf, tails = _conv_ffn(xf, p['ffn_norm_w'][0], wb['ffn_gate'], wb['ffn_up'], wb['ffn_down'],
                          p['ffn_conv_w'][0], p['ffn_conv_b'][0], _hist8(ffn_hist0), p['final_norm_w'],
                          layer=0, seq_len=length, tm=t['tm_ffn'], tf=t['tf_ffn'], final=False)
    ffn_new.append(_tails(tails, bsz, length, t['tm_ffn'], FFN_CONV - 1))

    proj, fl_proj = _norm_matmul(xf, p['cd_norm_w'][0], wb['cd_in'], wb['cd_small'],
                                 n=CD_MAIN, tm=t['tm_proj'], tn=t['tn_cd'])
    f_bias = jnp.pad(p['fox_f_bias'][0].astype(F32), (0, LANES - FOX_HEADS)).reshape(1, LANES)
    head_shape = (bsz, length, FOX_HEADS, FOX_HEAD_DIM)
    if c_k is None:
        qa, ka, vb, k32, v32, logf = _fox_prep(proj, fl_proj, f_bias, bsz=bsz, seq_len=length, tp=t['t_prep'])
        y_fox = _fox_prompt(qa, ka, vb, bsz=bsz, seq_len=length, t=t['t_fox'], ts=t['ts_fox'])
        logf_new = logf.reshape(bsz, length, LANES)[:, :, :FOX_HEADS]
        k_new, v_new = k32.reshape(head_shape), v32.reshape(head_shape)
    else:
        proj3 = proj.reshape(bsz, length, -1)
        k_new = proj3[:, :, FOX_WIDTH:2 * FOX_WIDTH].reshape(head_shape)
        v_new = proj3[:, :, 2 * FOX_WIDTH:3 * FOX_WIDTH].reshape(head_shape)
        past = c_k.shape[1]
        pairs = bsz * FOX_HEADS
        assert pairs <= LANES and length <= LANES
        to_lanes = lambda a, rows: jnp.pad(jnp.swapaxes(a, 0, 1).reshape(a.shape[1], pairs),
                                           ((0, rows - a.shape[1]), (0, LANES - pairs)))
        from_lanes = lambda a: jnp.swapaxes(a[:length, :pairs].reshape(length, bsz, FOX_HEADS), 0, 1)
        cache_lf = to_lanes(c_logf.astype(F32), past)
        fl_rows = to_lanes(fl_proj.reshape(bsz, length, LANES)[:, :, :FOX_HEADS], LANES)
        bias_lanes = jnp.pad(jnp.tile(p['fox_f_bias'][0].astype(F32), bsz), (0, LANES - pairs)).reshape(1, LANES)
        cum_t_cache, lf_rows, cum_rows, cum_t_new = _decode_cum(cache_lf, fl_rows, bias_lanes, c=t['c_cum'])
        logf_new = from_lanes(lf_rows)
        cq = jnp.broadcast_to(cum_rows[:length, :pairs].T[:, :, None], (pairs, length, LANES))
        y_fox = _fox_decode(proj, c_k, c_v, cq, cum_t_cache[:pairs, None, :], cum_t_new[:pairs, None, :],
                            bsz=bsz, lq=length, tk=t['tk_dec'])
    sc_hist = zeros(bsz, SC_WIDTH - 1, SC_DIM) if st_sconv is None else st_sconv
    y_sc, sc_tails = _sconv(proj, p['sconv_w'][0], _hist8(sc_hist), seq_len=length, tm=t['tm_sc'])
    sconv_new = _tails(sc_tails, bsz, length, t['tm_sc'], SC_WIDTH - 1)
    xf = _proj_residual(xf, y_fox, y_sc, wb['cd_out'], tm=t['tm_out'], tn=t['tn_out'])

    ffn_hist1 = zeros(bsz, FFN_CONV - 1, D_FF) if st_ffn is None else st_ffn[1]
    xf, tails = _conv_ffn(xf, p['ffn_norm_w'][1], wb['ffn_gate'], wb['ffn_up'], wb['ffn_down'],
                          p['ffn_conv_w'][1], p['ffn_conv_b'][1], _hist8(ffn_hist1), p['final_norm_w'],
                          layer=1, seq_len=length, tm=t['tm_ffn'], tf=t['tf_ffn'], final=True)
    ffn_new.append(_tails(tails, bsz, length, t['tm_ffn'], FFN_CONV - 1))

    return (xf.reshape(bsz, length, d), ret_new[None], ssd_new[None], ssd_conv_new[None], k_new[None],
            v_new[None], logf_new[None], sconv_new[None], jnp.stack(ffn_new))


def _largest_divisor(n, cap, multiple=1):
    best = None
    for cand in range(multiple, min(n, cap) + 1, multiple):
        if n % cand == 0:
            best = cand
    assert best is not None, (n, cap, multiple)
    return best


def _tiles(bsz, length, past=None):
    m = bsz * length
    seq_tile = lambda cap: _largest_divisor(length, cap, SUBLANES)
    row_tile = lambda cap: (_largest_divisor(length, cap, SUBLANES) if length >= cap
                            else _largest_divisor(m, cap, length))
    t = dict(
        tm_proj=row_tile(1024), tn_ab=512, tn_cd=1024,
        tm_out=row_tile(1024), tn_out=1024,
        tm_ffn=row_tile(512), tf_ffn=512,
        tm_sc=row_tile(512),
        c_ret=seq_tile(256), c_ssd=seq_tile(256),
    )
    if past is None:
        t['t_fox'] = seq_tile(2048)
        t['ts_fox'] = _largest_divisor(t['t_fox'], 256, LANES)
        t['t_prep'] = seq_tile(512)
    else:
        t['tk_dec'] = _largest_divisor(past, 1024, LANES)
        t['c_cum'] = _largest_divisor(past, 256, LANES)
    return t


def kernel(x_prompt, x_sample, state_ret, state_ssd, state_ssd_conv, cache_fox_k, cache_fox_v, cache_fox_logf, state_sconv, state_ffn_conv, ab_norm_w, ab_w_in, ret_norm_w, ssd_conv_w, ssd_conv_b, ssd_dt_bias, ssd_A_log, ssd_D, ssd_norm_w, ab_w_out, cd_norm_w, cd_w_in, fox_f_bias, sconv_w, cd_w_out, ffn_norm_w, ffn_w_gate, ffn_w_up, ffn_conv_w, ffn_conv_b, ffn_w_down, final_norm_w):
    p = dict(ab_norm_w=ab_norm_w, ab_w_in=ab_w_in, ret_norm_w=ret_norm_w, ssd_conv_w=ssd_conv_w,
             ssd_conv_b=ssd_conv_b, ssd_dt_bias=ssd_dt_bias, ssd_A_log=ssd_A_log, ssd_D=ssd_D,
             ssd_norm_w=ssd_norm_w, ab_w_out=ab_w_out, cd_norm_w=cd_norm_w, cd_w_in=cd_w_in,
             fox_f_bias=fox_f_bias, sconv_w=sconv_w, cd_w_out=cd_w_out, ffn_norm_w=ffn_norm_w,
             ffn_w_gate=ffn_w_gate, ffn_w_up=ffn_w_up, ffn_conv_w=ffn_conv_w, ffn_conv_b=ffn_conv_b,
             ffn_w_down=ffn_w_down, final_norm_w=final_norm_w)
    assert x_prompt.shape[-1] == D_MODEL and ab_w_in.shape == (1, D_MODEL, AB_MAIN + SSD_HEADS)
    assert cd_w_in.shape == (1, D_MODEL, CD_MAIN + FOX_HEADS) and ffn_w_gate.shape == (2, D_MODEL, D_FF)
    wb = _prep_weights(p)
    bp, lp_, _ = x_prompt.shape
    bs, ls, _ = x_sample.shape
    past = cache_fox_k.shape[2]
    (y_prompt, p_ret, p_ssd, p_ssd_conv, p_fox_k, p_fox_v, p_fox_logf, p_sconv, p_ffn_conv) = _trunk(
        x_prompt, 0, None, None, None, None, None, None, None, None, p, wb, _tiles(bp, lp_))
    (y_sample, s_ret, s_ssd, s_ssd_conv, s_fox_k, s_fox_v, s_fox_logf, s_sconv, s_ffn_conv) = _trunk(
        x_sample, past, state_ret[0], state_ssd[0], state_ssd_conv[0], cache_fox_k[0], cache_fox_v[0],
        cache_fox_logf[0], state_sconv[0], state_ffn_conv, p, wb, _tiles(bs, ls, past))
    return (y_prompt, y_sample, p_ret, s_ret, p_ssd, s_ssd, p_ssd_conv, s_ssd_conv, p_fox_k, s_fox_k,
            p_fox_v, s_fox_v, p_fox_logf, s_fox_logf, p_sconv, s_sconv, p_ffn_conv, s_ffn_conv)
```

```python
import functools
import math

import numpy as np
import jax
import jax.numpy as jnp
from jax import lax
from jax.experimental import pallas as pl
from jax.experimental.pallas import tpu as pltpu

F32 = jnp.float32
BF16 = jnp.bfloat16
EPS = 1e-6
ROPE_BASE = 10000.0
NEG_INF = float("-inf")

D_MODEL = 2048
RET_HEADS, RET_DK, RET_DV = 4, 128, 256
SSD_DINNER, SSD_HEADDIM, SSD_HEADS, SSD_GROUPS, SSD_DSTATE, SSD_CONV = 1024, 64, 16, 2, 128, 4
SSD_CONV_DIM = SSD_DINNER + 2 * SSD_GROUPS * SSD_DSTATE
FOX_HEADS, FOX_HEAD_DIM = 8, 128
FOX_WIDTH = FOX_HEADS * FOX_HEAD_DIM
SC_DIM, SC_WIDTH = 1024, 3
D_FF, FFN_CONV = 5632, 3
AB_MAIN = 2 * RET_HEADS * RET_DK + 2 * RET_HEADS * RET_DV + SSD_DINNER + SSD_CONV_DIM
AB_PAD = AB_MAIN + 128
CD_MAIN = 3 * FOX_WIDTH + 3 * SC_DIM
CD_PAD = CD_MAIN + 128

LANES = 128
SUBLANES = 8
VMEM_LIMIT = 52 * 1024 * 1024


def _cparams(n_axes):
    return pltpu.CompilerParams(dimension_semantics=("arbitrary",) * n_axes,
                                vmem_limit_bytes=VMEM_LIMIT)


def _rms(xf, w):
    return xf * lax.rsqrt(jnp.mean(xf * xf, axis=-1, keepdims=True) + EPS) * w


def _softplus(x):
    return jnp.maximum(x, 0.0) + jnp.log1p(jnp.exp(-jnp.abs(x)))


def _split3(x):
    hi = x.astype(BF16)
    r1 = x - hi.astype(F32)
    mid = r1.astype(BF16)
    lo = (r1 - mid.astype(F32)).astype(BF16)
    return hi, mid, lo


def _widen(x, n):
    return x[:, 0:n] if n <= LANES else jnp.concatenate([x] * (n // LANES), axis=1)


def _dot(a, b):
    return jnp.dot(a, b, preferred_element_type=F32)


def _dot_nt(a, b):
    return lax.dot_general(a, b, (((1,), (1,)), ((), ())), preferred_element_type=F32)


def _dot_tn(a, b):
    return lax.dot_general(a, b, (((0,), (0,)), ((), ())), preferred_element_type=F32)


def _exact_lhs_dot(m_bf16, x):
    hi, mid, lo = _split3(x)
    return _dot(m_bf16, hi) + _dot(m_bf16, mid) + _dot(m_bf16, lo)


def _exact_rhs_dot(x, m_bf16):
    hi, mid, lo = _split3(x)
    return _dot(hi, m_bf16) + _dot(mid, m_bf16) + _dot(lo, m_bf16)


def _conv_rows(ext_ref, x, prev8, w_ref, width, rows, w_cols=slice(None)):
    ext_ref[0:SUBLANES, :] = prev8
    ext_ref[SUBLANES:SUBLANES + rows, :] = x
    out = None
    for j in range(width):
        off = SUBLANES - (width - 1) + j
        term = ext_ref[off:off + rows, :] * w_ref[j:j + 1, w_cols]
        out = term if out is None else out + term
    return out


def _seq_tiling(seq_len, tile_rows):
    if seq_len >= tile_rows:
        assert seq_len % tile_rows == 0
        return tile_rows, 1, seq_len // tile_rows
    assert tile_rows % seq_len == 0 and seq_len % SUBLANES == 0
    return seq_len, tile_rows // seq_len, 1


def _norm_matmul_kernel(x_ref, nw_ref, w_ref, ws_ref, o_ref, os_ref, h_ref):
    @pl.when(pl.program_id(1) == 0)
    def _():
        h_ref[...] = _rms(x_ref[...], nw_ref[...]).astype(BF16)
        os_ref[...] = _dot(h_ref[...], ws_ref[...])

    o_ref[...] = _dot(h_ref[...], w_ref[...])


def _norm_matmul(x, norm_w, w, w_small, *, n, tm, tn):
    m, d = x.shape
    assert m % tm == 0 and n % tn == 0 and n <= w.shape[1] and w_small.shape == (d, LANES)
    return pl.pallas_call(
        _norm_matmul_kernel,
        grid=(m // tm, n // tn),
        in_specs=[pl.BlockSpec((tm, d), lambda i, j: (i, 0)),
                  pl.BlockSpec((1, d), lambda i, j: (0, 0)),
                  pl.BlockSpec((d, tn), lambda i, j: (0, j)),
                  pl.BlockSpec((d, LANES), lambda i, j: (0, 0))],
        out_specs=[pl.BlockSpec((tm, tn), lambda i, j: (i, j)),
                   pl.BlockSpec((tm, LANES), lambda i, j: (i, 0))],
        out_shape=[jax.ShapeDtypeStruct((m, n), F32),
                   jax.ShapeDtypeStruct((m, LANES), F32)],
        scratch_shapes=[pltpu.VMEM((tm, d), BF16)],
        compiler_params=_cparams(2),
        name="norm_in_proj",
    )(x, norm_w.reshape(1, d), w, w_small)


def _proj_res_kernel(x_ref, a_ref, b_ref, wa_ref, wb_ref, o_ref):
    acc = _dot(a_ref[...], wa_ref[...])
    acc = acc + _dot(b_ref[...], wb_ref[...])
    o_ref[...] = x_ref[...] + acc


def _proj_residual(x, a, b, w, *, tm, tn):
    m, d = x.shape
    ka, kb = a.shape[1], b.shape[1]
    assert m % tm == 0 and d % tn == 0 and ka == kb and w.shape == (ka + kb, d)
    return pl.pallas_call(
        _proj_res_kernel,
        grid=(m // tm, d // tn),
        in_specs=[pl.BlockSpec((tm, tn), lambda i, j: (i, j)),
                  pl.BlockSpec((tm, ka), lambda i, j: (i, 0)),
                  pl.BlockSpec((tm, kb), lambda i, j: (i, 0)),
                  pl.BlockSpec((ka, tn), lambda i, j: (0, j)),
                  pl.BlockSpec((kb, tn), lambda i, j: (1, j))],
        out_specs=pl.BlockSpec((tm, tn), lambda i, j: (i, j)),
        out_shape=jax.ShapeDtypeStruct((m, d), F32),
        compiler_params=_cparams(2),
        name="out_proj_residual",
    )(x, a, b, w, w)


def _ffn_kernel(x_ref, nw_ref, wg_ref, wu_ref, wd_ref, cw_ref, cb_ref, hist_ref, fw_ref,
                o_ref, tail_ref, h_ref, carry_ref, ext_ref,
                *, tm, rows, spt, tpb, nf, ts, nsub, nsub_last, final):
    i = pl.program_id(0)
    f = pl.program_id(1)

    @pl.when(f == 0)
    def _():
        xf = x_ref[...]
        h_ref[...] = _rms(xf, nw_ref[...]).astype(BF16)
        o_ref[...] = xf

    first = (i % tpb) == 0

    def sub_block(sb):
        cols = slice(sb * ts, (sb + 1) * ts)
        h = h_ref[...]
        a = _dot(h, wg_ref[:, cols])
        u = _dot(h, wu_ref[:, cols])
        convs = []
        for s in range(spt):
            a_s = a[s * rows:(s + 1) * rows]
            if tpb == 1:
                prev = hist_ref[s, :, cols]
            else:
                prev = jnp.where(first, hist_ref[s, :, cols], carry_ref[f * nsub + sb])
            convs.append(_conv_rows(ext_ref, a_s, prev, cw_ref, FFN_CONV, rows, cols))
            tail_ref[s, :, cols] = a_s[rows - SUBLANES:rows]
        if tpb > 1:
            carry_ref[f * nsub + sb] = a[tm - SUBLANES:tm]
        conv = convs[0] if spt == 1 else jnp.concatenate(convs, axis=0)
        act = (jax.nn.silu(conv + cb_ref[:, cols]) * u).astype(BF16)
        o_ref[...] += _dot(act, wd_ref[cols, :])

    if nsub_last == nsub:
        for sb in range(nsub):
            sub_block(sb)
    else:
        @pl.when(f < nf - 1)
        def _():
            for sb in range(nsub):
                sub_block(sb)

        @pl.when(f == nf - 1)
        def _():
            for sb in range(nsub_last):
                sub_block(sb)

    if final:
        @pl.when(f == nf - 1)
        def _():
            o_ref[...] = _rms(o_ref[...], fw_ref[...])


def _conv_ffn(x, norm_w, wg, wu, wd, conv_w, conv_b, hist8, final_w, *, layer, seq_len, tm, tf, ts, final):
    m, d = x.shape
    ff = wg.shape[2]
    assert m % tm == 0 and tf % ts == 0 and ff % ts == 0
    rows, spt, tpb = _seq_tiling(seq_len, tm)
    nm, nf = m // tm, pl.cdiv(ff, tf)
    nsub = tf // ts
    nsub_last = (ff - (nf - 1) * tf) // ts
    hist_map = (lambda i, f: (i // tpb, 0, f)) if spt == 1 else (lambda i, f: (i, 0, f))
    kern = functools.partial(_ffn_kernel, tm=tm, rows=rows, spt=spt, tpb=tpb, nf=nf, ts=ts, nsub=nsub,
                             nsub_last=nsub_last, final=final)
    out, tails = pl.pallas_call(
        kern,
        grid=(nm, nf),
        in_specs=[pl.BlockSpec((tm, d), lambda i, f: (i, 0)),
                  pl.BlockSpec((1, d), lambda i, f: (0, 0)),
                  pl.BlockSpec((None, d, tf), lambda i, f: (layer, 0, f)),
                  pl.BlockSpec((None, d, tf), lambda i, f: (layer, 0, f)),
                  pl.BlockSpec((None, tf, d), lambda i, f: (layer, f, 0)),
                  pl.BlockSpec((FFN_CONV, tf), lambda i, f: (0, f)),
                  pl.BlockSpec((1, tf), lambda i, f: (0, f)),
                  pl.BlockSpec((spt, SUBLANES, tf), hist_map),
                  pl.BlockSpec((1, d), lambda i, f: (0, 0))],
        out_specs=[pl.BlockSpec((tm, d), lambda i, f: (i, 0)),
                   pl.BlockSpec((spt, SUBLANES, tf), lambda i, f: (i, 0, f))],
        out_shape=[jax.ShapeDtypeStruct((m, d), F32),
                   jax.ShapeDtypeStruct((nm * spt, SUBLANES, ff), F32)],
        scratch_shapes=[pltpu.VMEM((tm, d), BF16),
                        pltpu.VMEM((nf * nsub, SUBLANES, ts), F32),
                        pltpu.VMEM((rows + SUBLANES, ts), F32)],
        compiler_params=_cparams(2),
        name="conv_ffn",
    )(x, norm_w.reshape(1, d), wg, wu, wd, conv_w, conv_b.reshape(1, ff), hist8, final_w.reshape(1, d))
    return out, tails


def _retention_kernel(q_ref, k_ref, v_ref, g_ref, cos_ref, sin_ref, st_ref, nw_ref,
                      y_ref, so_ref, *, c):
    ci = pl.program_id(1)

    @pl.when(ci == 0)
    def _():
        so_ref[...] = st_ref[...]

    cos = cos_ref[...]
    sin = sin_ref[...]
    ii = lax.broadcasted_iota(jnp.int32, (c, c), 0)
    jj = lax.broadcasted_iota(jnp.int32, (c, c), 1)
    diff = (ii - jj).astype(F32)
    causal = ii >= jj
    ridx = lax.broadcasted_iota(jnp.int32, (c, 1), 0).astype(F32)
    for h in range(RET_HEADS):
        lg = math.log1p(-(2.0 ** (-5.0 - h)))
        q = q_ref[:, h * RET_DK:(h + 1) * RET_DK]
        k = k_ref[:, h * RET_DK:(h + 1) * RET_DK]
        v = v_ref[:, h * RET_DV:(h + 1) * RET_DV]
        qr = q * cos + pltpu.roll(q, RET_DK // 2, 1) * sin
        kr = (k * cos + pltpu.roll(k, RET_DK // 2, 1) * sin) * (RET_DK ** -0.5)
        qb = qr.astype(BF16)
        kb = kr.astype(BF16)
        vb = v.astype(BF16)
        decay = jnp.exp(jnp.where(causal, diff * lg, NEG_INF))
        inner = jnp.exp((ridx + 1.0) * lg)
        sdecay = jnp.exp((c - 1.0 - ridx) * lg)
        s = so_ref[0, h]
        scores = _dot_nt(qb, kb) * decay
        y = _dot(scores.astype(BF16), vb)
        y = y + _dot(qb, s.astype(BF16)) * inner
        kd = (kr * sdecay).astype(BF16)
        so_ref[0, h] = math.exp(c * lg) * s + _dot_tn(kd, vb)
        mu = jnp.mean(y, axis=-1, keepdims=True)
        yc = y - mu
        var = jnp.mean(yc * yc, axis=-1, keepdims=True)
        yn = yc * lax.rsqrt(var + EPS) * nw_ref[:, h * RET_DV:(h + 1) * RET_DV]
        g = g_ref[:, h * RET_DV:(h + 1) * RET_DV]
        y_ref[:, h * RET_DV:(h + 1) * RET_DV] = (jax.nn.silu(g) * yn).astype(BF16)


def _retention(proj, cosf, sinf, state, norm_w, *, bsz, seq_len, c):
    m = proj.shape[0]
    nc = seq_len // c
    assert seq_len % c == 0
    qk_w = RET_HEADS * RET_DK
    v_w = RET_HEADS * RET_DV
    row = lambda b, ci: b * nc + ci
    y, s_new = pl.pallas_call(
        functools.partial(_retention_kernel, c=c),
        grid=(bsz, nc),
        in_specs=[pl.BlockSpec((c, qk_w), lambda b, ci: (row(b, ci), 0)),
                  pl.BlockSpec((c, qk_w), lambda b, ci: (row(b, ci), 1)),
                  pl.BlockSpec((c, v_w), lambda b, ci: (row(b, ci), 1)),
                  pl.BlockSpec((c, v_w), lambda b, ci: (row(b, ci), 2)),
                  pl.BlockSpec((c, RET_DK), lambda b, ci: (ci, 0)),
                  pl.BlockSpec((c, RET_DK), lambda b, ci: (ci, 0)),
                  pl.BlockSpec((1, RET_HEADS, RET_DK, RET_DV), lambda b, ci: (b, 0, 0, 0)),
                  pl.BlockSpec((1, v_w), lambda b, ci: (0, 0))],
        out_specs=[pl.BlockSpec((c, v_w), lambda b, ci: (row(b, ci), 0)),
                   pl.BlockSpec((1, RET_HEADS, RET_DK, RET_DV), lambda b, ci: (b, 0, 0, 0))],
        out_shape=[jax.ShapeDtypeStruct((m, v_w), BF16),
                   jax.ShapeDtypeStruct(state.shape, F32)],
        compiler_params=_cparams(2),
        name="retention",
    )(proj, proj, proj, proj, cosf, sinf, state, norm_w.reshape(1, v_w))
    return y, s_new


def _ssd_kernel(z_ref, xs_ref, bc_ref, dt_ref, hx_ref, hbc_ref, st_ref,
                cwx_ref, cwbc_ref, cbx_ref, cbbc_ref, dtb_ref, alog_ref, dsk_ref, nw_ref,
                tri_ref, exp_ref,
                y_ref, so_ref,
                st_scr, cx_scr, cbc_scr, extx_scr, extbc_scr, yh_scr, *, c, nc):
    ci = pl.program_id(1)
    gw = SSD_DINNER // SSD_GROUPS
    hpg = SSD_HEADS // SSD_GROUPS

    @pl.when(ci == 0)
    def _():
        st_scr[...] = st_ref[0].T
        cx_scr[...] = hx_ref[0]
        cbc_scr[...] = hbc_ref[0]

    xs_raw = xs_ref[...]
    bc_raw = bc_ref[...]
    xs = jax.nn.silu(_conv_rows(extx_scr, xs_raw, cx_scr[...], cwx_ref, SSD_CONV, c) + cbx_ref[...])
    bcm = jax.nn.silu(_conv_rows(extbc_scr, bc_raw, cbc_scr[...], cwbc_ref, SSD_CONV, c) + cbbc_ref[...])
    cx_scr[...] = xs_raw[c - SUBLANES:c]
    cbc_scr[...] = bc_raw[c - SUBLANES:c]

    tri = tri_ref[...]
    expand = exp_ref[...]
    dt = _softplus(dt_ref[...] + dtb_ref[...])
    a = -jnp.exp(alog_ref[...])
    acs = _exact_lhs_dot(tri, dt * a)
    acs_t = acs.T
    acs_last = acs[c - 1:c, :]
    exp_acs = jnp.exp(acs)
    to_end = jnp.exp(acs_last - acs)
    dt_e = _exact_rhs_dot(dt, expand)
    to_end_e = _exact_rhs_dot(to_end, expand)
    exp_acs_e = _exact_rhs_dot(exp_acs, expand)
    chunk_dec_e = _exact_rhs_dot(jnp.exp(acs_last), expand)

    xdt = xs * dt_e
    xdt_b = xdt.astype(BF16)
    xend_b = (xdt * to_end_e).astype(BF16)
    ii = lax.broadcasted_iota(jnp.int32, (c, c), 0)
    jj = lax.broadcasted_iota(jnp.int32, (c, c), 1)
    causal = ii >= jj
    nb = SSD_GROUPS * SSD_DSTATE
    for g in range(SSD_GROUPS):
        b_g = bcm[:, g * SSD_DSTATE:(g + 1) * SSD_DSTATE].astype(BF16)
        c_g = bcm[:, nb + g * SSD_DSTATE:nb + (g + 1) * SSD_DSTATE].astype(BF16)
        cb = _dot_nt(c_g, b_g)
        s_g = st_scr[:, g * gw:(g + 1) * gw]
        y_state = _dot(c_g, s_g.astype(BF16)) * exp_acs_e[:, g * gw:(g + 1) * gw]
        for r in range(hpg):
            hh = g * hpg + r
            seg = acs[:, hh:hh + 1] - acs_t[hh:hh + 1, :]
            lmat = jnp.exp(jnp.where(causal, seg, NEG_INF))
            mm = (cb * lmat).astype(BF16)
            lo = hh * SSD_HEADDIM
            yh_scr[:, lo:lo + SSD_HEADDIM] = (
                _dot(mm, xdt_b[:, lo:lo + SSD_HEADDIM]) + y_state[:, r * SSD_HEADDIM:(r + 1) * SSD_HEADDIM])
        upd = _dot_tn(b_g, xend_b[:, g * gw:(g + 1) * gw])
        st_scr[:, g * gw:(g + 1) * gw] = chunk_dec_e[:, g * gw:(g + 1) * gw] * s_g + upd

    y = yh_scr[...] + dsk_ref[...] * xs
    z = z_ref[...]
    y_ref[...] = _rms(y * jax.nn.silu(z), nw_ref[...]).astype(BF16)

    @pl.when(ci == nc - 1)
    def _():
        so_ref[0] = st_scr[...].T


def _ssd(proj, dt_proj, hist8, state, conv_w, conv_b, dt_bias, a_log, d_skip, norm_w, *, bsz, seq_len, c):
    m = proj.shape[0]
    nc = seq_len // c
    assert seq_len % c == 0
    row = lambda b, ci: b * nc + ci
    const2 = lambda b, ci: (0, 0)
    di, bcw = SSD_DINNER, 2 * SSD_GROUPS * SSD_DSTATE
    tri = jnp.asarray(np.tril(np.ones((c, c), np.float32)), BF16)
    expand = np.zeros((LANES, di), np.float32)
    for h in range(SSD_HEADS):
        expand[h, h * SSD_HEADDIM:(h + 1) * SSD_HEADDIM] = 1.0
    expand = jnp.asarray(expand, BF16)
    pad_row = lambda v: jnp.pad(v.astype(F32), (0, LANES - v.shape[0])).reshape(1, LANES)
    st2 = state.reshape(bsz, di, SSD_DSTATE)
    y, s_new = pl.pallas_call(
        functools.partial(_ssd_kernel, c=c, nc=nc),
        grid=(bsz, nc),
        in_specs=[pl.BlockSpec((c, di), lambda b, ci: (row(b, ci), 3)),
                  pl.BlockSpec((c, di), lambda b, ci: (row(b, ci), 4)),
                  pl.BlockSpec((c, bcw), lambda b, ci: (row(b, ci), 10)),
                  pl.BlockSpec((c, LANES), lambda b, ci: (row(b, ci), 0)),
                  pl.BlockSpec((1, SUBLANES, di), lambda b, ci: (b, 0, 0)),
                  pl.BlockSpec((1, SUBLANES, bcw), lambda b, ci: (b, 0, 2)),
                  pl.BlockSpec((1, di, SSD_DSTATE), lambda b, ci: (b, 0, 0)),
                  pl.BlockSpec((SSD_CONV, di), const2),
                  pl.BlockSpec((SSD_CONV, bcw), lambda b, ci: (0, 2)),
                  pl.BlockSpec((1, di), const2),
                  pl.BlockSpec((1, bcw), lambda b, ci: (0, 2)),
                  pl.BlockSpec((1, LANES), const2),
                  pl.BlockSpec((1, LANES), const2),
                  pl.BlockSpec((1, di), const2),
                  pl.BlockSpec((1, di), const2),
                  pl.BlockSpec((c, c), const2),
                  pl.BlockSpec((LANES, di), const2)],
        out_specs=[pl.BlockSpec((c, di), lambda b, ci: (row(b, ci), 0)),
                   pl.BlockSpec((1, di, SSD_DSTATE), lambda b, ci: (b, 0, 0))],
        out_shape=[jax.ShapeDtypeStruct((m, di), BF16),
                   jax.ShapeDtypeStruct(st2.shape, F32)],
        scratch_shapes=[pltpu.VMEM((SSD_DSTATE, di), F32),
                        pltpu.VMEM((SUBLANES, di), F32),
                        pltpu.VMEM((SUBLANES, bcw), F32),
                        pltpu.VMEM((c + SUBLANES, di), F32),
                        pltpu.VMEM((c + SUBLANES, bcw), F32),
                        pltpu.VMEM((c, di), F32)],
        compiler_params=_cparams(2),
        name="ssd",
    )(proj, proj, proj, dt_proj, hist8, hist8, st2,
      conv_w, conv_w, conv_b.reshape(1, -1), conv_b.reshape(1, -1),
      pad_row(dt_bias), pad_row(a_log), jnp.repeat(d_skip.astype(F32), SSD_HEADDIM).reshape(1, di),
      norm_w.reshape(1, di), tri, expand)
    return y, s_new.reshape(state.shape)


def _decode_cum_kernel(lfc_ref, fl_ref, b_ref, tri_ref, cumt_c_ref, lfn_ref, cumn_ref, cumt_n_ref,
                       carry_ref, *, c, ncb):
    j = pl.program_id(0)

    @pl.when(j == 0)
    def _():
        carry_ref[...] = jnp.zeros_like(carry_ref)

    @pl.when(j < ncb)
    def _():
        cum = _exact_lhs_dot(tri_ref[...], lfc_ref[...]) + carry_ref[...]
        carry_ref[...] = cum[c - 1:c, :]
        cumt_c_ref[...] = cum.T

    @pl.when(j == ncb)
    def _():
        lf = -_softplus(-(fl_ref[...] + b_ref[...]))
        lfn_ref[...] = lf
        cum = _exact_lhs_dot(tri_ref[0:LANES, 0:LANES], lf) + carry_ref[...]
        cumn_ref[...] = cum
        cumt_n_ref[...] = cum.T


def _decode_cum(cache_lf, fl_new, bias, *, c):
    past = cache_lf.shape[0]
    assert past % c == 0 and c % LANES == 0 and fl_new.shape == (LANES, LANES)
    ncb = past // c
    tri = jnp.asarray(np.tril(np.ones((c, c), np.float32)), BF16)
    blk = lambda j: jnp.minimum(j, ncb - 1)
    sq = jax.ShapeDtypeStruct((LANES, LANES), F32)
    return pl.pallas_call(
        functools.partial(_decode_cum_kernel, c=c, ncb=ncb),
        grid=(ncb + 1,),
        in_specs=[pl.BlockSpec((c, LANES), lambda j: (blk(j), 0)),
                  pl.BlockSpec((LANES, LANES), lambda j: (0, 0)),
                  pl.BlockSpec((1, LANES), lambda j: (0, 0)),
                  pl.BlockSpec((c, c), lambda j: (0, 0))],
        out_specs=[pl.BlockSpec((LANES, c), lambda j: (0, blk(j))),
                   pl.BlockSpec((LANES, LANES), lambda j: (0, 0)),
                   pl.BlockSpec((LANES, LANES), lambda j: (0, 0)),
                   pl.BlockSpec((LANES, LANES), lambda j: (0, 0))],
        out_shape=[jax.ShapeDtypeStruct((LANES, past), F32), sq, sq, sq],
        scratch_shapes=[pltpu.VMEM((1, LANES), F32)],
        compiler_params=_cparams(1),
        name="decode_logf_cumsum",
    )(cache_lf, fl_new, bias, tri)


FOX_AUG = 2 * FOX_HEAD_DIM
N_BIAS_PIECES = 3


def _fox_prep_kernel(q_ref, k_ref, v_ref, fl_ref, fb_ref, tri_ref, place_ref, ones_ref,
                     qa_ref, ka_ref, vb_ref, k32_ref, v32_ref, lf_ref, carry_ref, *, tp):
    @pl.when(pl.program_id(1) == 0)
    def _():
        carry_ref[...] = jnp.zeros_like(carry_ref)

    lf = -_softplus(-(fl_ref[...] + fb_ref[...]))
    lf_ref[...] = lf
    cum = _exact_lhs_dot(tri_ref[...], lf) + carry_ref[...]
    carry_ref[...] = cum[tp - 1:tp, :]
    pieces = _split3(cum * (FOX_HEAD_DIM ** 0.5))
    n = N_BIAS_PIECES
    aug_q = ones_ref[0:1, :] + sum(_dot(pieces[r], place_ref[r]) for r in range(n))
    aug_k = ones_ref[1:2, :] - sum(_dot(pieces[r], place_ref[n + r]) for r in range(n))
    for h in range(FOX_HEADS):
        src = slice(h * FOX_HEAD_DIM, (h + 1) * FOX_HEAD_DIM)
        feat = slice(h * FOX_AUG, h * FOX_AUG + FOX_HEAD_DIM)
        bias = slice(h * FOX_AUG + FOX_HEAD_DIM, (h + 1) * FOX_AUG)
        qa_ref[:, feat] = q_ref[:, src].astype(BF16)
        qa_ref[:, bias] = aug_q[:, src].astype(BF16)
        ka_ref[:, feat] = k_ref[:, src].astype(BF16)
        ka_ref[:, bias] = aug_k[:, src].astype(BF16)
    k = k_ref[...]
    v = v_ref[...]
    k32_ref[...] = pltpu.einshape("m(hd)->mhd", k, h=FOX_HEADS)
    v32_ref[...] = pltpu.einshape("m(hd)->mhd", v, h=FOX_HEADS)
    vb_ref[...] = v.astype(BF16)


def _fox_prep(proj, fl_proj, f_bias, *, bsz, seq_len, tp):
    m = proj.shape[0]
    nt = seq_len // tp
    assert seq_len % tp == 0
    w = FOX_WIDTH
    tri = jnp.asarray(np.tril(np.ones((tp, tp), np.float32)), BF16)
    n = N_BIAS_PIECES
    place = np.zeros((2 * n, LANES, w), np.float32)
    ones = np.zeros((SUBLANES, w), np.float32)
    for h in range(FOX_HEADS):
        for r in range(2 * n):
            place[r, h, h * FOX_HEAD_DIM + r] = 1.0
        ones[0, h * FOX_HEAD_DIM + n:h * FOX_HEAD_DIM + 2 * n] = 1.0
        ones[1, h * FOX_HEAD_DIM:h * FOX_HEAD_DIM + n] = 1.0
    row = lambda b, ti: (b * nt + ti, 0)
    const2 = lambda b, ti: (0, 0)
    return pl.pallas_call(
        functools.partial(_fox_prep_kernel, tp=tp),
        grid=(bsz, nt),
        in_specs=[pl.BlockSpec((tp, w), lambda b, ti: (b * nt + ti, 0)),
                  pl.BlockSpec((tp, w), lambda b, ti: (b * nt + ti, 1)),
                  pl.BlockSpec((tp, w), lambda b, ti: (b * nt + ti, 2)),
                  pl.BlockSpec((tp, LANES), row),
                  pl.BlockSpec((1, LANES), const2),
                  pl.BlockSpec((tp, tp), const2),
                  pl.BlockSpec((2 * n, LANES, w), lambda b, ti: (0, 0, 0)),
                  pl.BlockSpec((SUBLANES, w), const2)],
        out_specs=[pl.BlockSpec((tp, FOX_HEADS * FOX_AUG), row),
                   pl.BlockSpec((tp, FOX_HEADS * FOX_AUG), row),
                   pl.BlockSpec((tp, w), row),
                   pl.BlockSpec((tp, FOX_HEADS, FOX_HEAD_DIM), lambda b, ti: (b * nt + ti, 0, 0)),
                   pl.BlockSpec((tp, FOX_HEADS, FOX_HEAD_DIM), lambda b, ti: (b * nt + ti, 0, 0)),
                   pl.BlockSpec((tp, LANES), row)],
        out_shape=[jax.ShapeDtypeStruct((m, FOX_HEADS * FOX_AUG), BF16),
                   jax.ShapeDtypeStruct((m, FOX_HEADS * FOX_AUG), BF16),
                   jax.ShapeDtypeStruct((m, w), BF16),
                   jax.ShapeDtypeStruct((m, FOX_HEADS, FOX_HEAD_DIM), F32),
                   jax.ShapeDtypeStruct((m, FOX_HEADS, FOX_HEAD_DIM), F32),
                   jax.ShapeDtypeStruct((m, LANES), F32)],
        scratch_shapes=[pltpu.VMEM((1, LANES), F32)],
        compiler_params=_cparams(2),
        name="fox_prep",
    )(proj, proj, proj, fl_proj, f_bias, tri, jnp.asarray(place, BF16), jnp.asarray(ones, F32))


def _fox_kernel(qi_ref, ki_ref, q_ref, k_ref, v_ref, o_ref, m_ref, acc_ref, va_ref, *, t, ts):
    step = pl.program_id(2)
    qi = qi_ref[step]
    ki = ki_ref[step]
    to_log2 = (FOX_HEAD_DIM ** -0.5) * math.log2(math.e)
    hd = FOX_HEAD_DIM

    @pl.when(ki == 0)
    def _():
        m_ref[...] = jnp.full_like(m_ref, NEG_INF)
        acc_ref[...] = jnp.zeros_like(acc_ref)
        va_ref[:, hd:2 * hd] = jnp.ones((t, hd), BF16)

    va_ref[:, 0:hd] = v_ref[...]

    def scores(rows, nk, r, diagonal):
        s = _dot_nt(q_ref[rows, :], k_ref[0:nk, :])
        if diagonal:
            ri = lax.broadcasted_iota(jnp.int32, (ts, nk), 0) + r * ts
            ci = lax.broadcasted_iota(jnp.int32, (ts, nk), 1)
            s = jnp.where(ci <= ri, s, NEG_INF)
        return s

    def update(diagonal):
        for r in range(t // ts):
            rows = slice(r * ts, (r + 1) * ts)
            nk = (r + 1) * ts if diagonal else t
            s = scores(rows, nk, r, diagonal)
            m_old = m_ref[rows, :]
            m_new = jnp.maximum(m_old, jnp.max(s, axis=1, keepdims=True))
            m_ref[rows, :] = m_new
            alpha = jnp.exp2((m_old - m_new) * to_log2)
            p = jnp.exp2((s - _widen(m_new, nk)) * to_log2)
            acc_ref[rows, :] = _widen(alpha, 2 * hd) * acc_ref[rows, :] + _dot(p.astype(BF16), va_ref[0:nk, :])

    @pl.when(ki < qi)
    def _():
        update(False)

    @pl.when(ki == qi)
    def _():
        update(True)
        o_ref[...] = (acc_ref[:, 0:hd] / acc_ref[:, hd:2 * hd]).astype(BF16)


def _fox_prompt(qa, ka, vb, *, bsz, seq_len, t, ts):
    m = qa.shape[0]
    nq = seq_len // t
    assert seq_len % t == 0 and t % ts == 0
    pairs = [(qi, ki) for qi in range(nq) for ki in range(qi + 1)]
    qi_tab = jnp.asarray([p[0] for p in pairs], jnp.int32)
    ki_tab = jnp.asarray([p[1] for p in pairs], jnp.int32)
    grid_spec = pltpu.PrefetchScalarGridSpec(
        num_scalar_prefetch=2,
        grid=(bsz, FOX_HEADS, len(pairs)),
        in_specs=[pl.BlockSpec((t, FOX_AUG), lambda b, h, s, qi, ki: (b * nq + qi[s], h)),
                  pl.BlockSpec((t, FOX_AUG), lambda b, h, s, qi, ki: (b * nq + ki[s], h)),
                  pl.BlockSpec((t, FOX_HEAD_DIM), lambda b, h, s, qi, ki: (b * nq + ki[s], h))],
        out_specs=pl.BlockSpec((t, FOX_HEAD_DIM), lambda b, h, s, qi, ki: (b * nq + qi[s], h)),
        scratch_shapes=[pltpu.VMEM((t, LANES), F32), pltpu.VMEM((t, 2 * FOX_HEAD_DIM), F32),
                        pltpu.VMEM((t, 2 * FOX_HEAD_DIM), BF16)],
    )
    return pl.pallas_call(
        functools.partial(_fox_kernel, t=t, ts=ts),
        grid_spec=grid_spec,
        out_shape=jax.ShapeDtypeStruct((m, FOX_WIDTH), BF16),
        compiler_params=_cparams(3),
        name="fox_attention",
    )(qi_tab, ki_tab, qa, ka, vb)


def _fox_decode_kernel(q_ref, kn_ref, vn_ref, kc_ref, vc_ref, cq_ref, ckc_ref, ckn_ref, o_ref,
                       m_ref, l_ref, acc_ref, *, lq, ncb):
    j = pl.program_id(1)
    nh, hd = FOX_HEADS, FOX_HEAD_DIM

    @pl.when(j == 0)
    def _():
        m_ref[...] = jnp.full_like(m_ref, NEG_INF)
        l_ref[...] = jnp.zeros_like(l_ref)
        acc_ref[...] = jnp.zeros_like(acc_ref)

    def attend(k_head, v_head, ck_head, causal):
        for h in range(nh):
            qh = q_ref[:, h * hd:(h + 1) * hd].astype(BF16)
            ck = ck_head(h)
            tk = ck.shape[1]
            s = _dot_nt(qh, k_head(h).astype(BF16)) * (hd ** -0.5)
            s = s + (_widen(cq_ref[h], tk) - ck)
            if causal:
                rows = lax.broadcasted_iota(jnp.int32, (lq, tk), 0)
                cols = lax.broadcasted_iota(jnp.int32, (lq, tk), 1)
                s = jnp.where(cols <= rows, s, NEG_INF)
            m_old = m_ref[h]
            m_new = jnp.maximum(m_old, jnp.max(s, axis=1, keepdims=True))
            alpha = jnp.exp(m_old - m_new)
            p = jnp.exp(s - _widen(m_new, tk))
            l_ref[h] = alpha * l_ref[h] + jnp.sum(p, axis=1, keepdims=True)
            acc_ref[h] = alpha * acc_ref[h] + _dot(p.astype(BF16), v_head(h).astype(BF16))
            m_ref[h] = m_new

    @pl.when(j < ncb)
    def _():
        k_hm = pltpu.einshape("mhd->hmd", kc_ref[0])
        v_hm = pltpu.einshape("mhd->hmd", vc_ref[0])
        attend(lambda h: k_hm[h], lambda h: v_hm[h], lambda h: ckc_ref[h], False)

    @pl.when(j == ncb)
    def _():
        attend(lambda h: kn_ref[:, h * hd:(h + 1) * hd], lambda h: vn_ref[:, h * hd:(h + 1) * hd],
               lambda h: ckn_ref[h][:, 0:lq], True)
        for h in range(nh):
            o_ref[:, h * hd:(h + 1) * hd] = (acc_ref[h] / l_ref[h]).astype(BF16)


def _fox_decode(proj, cache_k, cache_v, cq, ck_cache, ck_new, *, bsz, lq, tk):
    past = cache_k.shape[1]
    assert past % tk == 0 and tk % LANES == 0 and lq <= LANES
    ncb = past // tk
    nh, hd, w = FOX_HEADS, FOX_HEAD_DIM, FOX_WIDTH
    tile = lambda j: jnp.minimum(j, ncb - 1)
    cache_spec = pl.BlockSpec((1, tk, nh, hd), lambda b, j: (b, tile(j), 0, 0))
    return pl.pallas_call(
        functools.partial(_fox_decode_kernel, lq=lq, ncb=ncb),
        grid=(bsz, ncb + 1),
        in_specs=[pl.BlockSpec((lq, w), lambda b, j: (b, 0)),
                  pl.BlockSpec((lq, w), lambda b, j: (b, 1)),
                  pl.BlockSpec((lq, w), lambda b, j: (b, 2)),
                  cache_spec,
                  cache_spec,
                  pl.BlockSpec((nh, lq, LANES), lambda b, j: (b, 0, 0)),
                  pl.BlockSpec((nh, 1, tk), lambda b, j: (b, 0, tile(j))),
                  pl.BlockSpec((nh, 1, LANES), lambda b, j: (b, 0, 0))],
        out_specs=pl.BlockSpec((lq, w), lambda b, j: (b, 0)),
        out_shape=jax.ShapeDtypeStruct((bsz * lq, w), BF16),
        scratch_shapes=[pltpu.VMEM((nh, lq, LANES), F32), pltpu.VMEM((nh, lq, LANES), F32),
                        pltpu.VMEM((nh, lq, hd), F32)],
        compiler_params=_cparams(2),
        name="fox_decode",
    )(proj, proj, proj, cache_k, cache_v, cq, ck_cache, ck_new)


def _sconv_kernel(u_ref, bg_ref, cg_ref, cw_ref, hist_ref, y_ref, tail_ref, carry_ref, ext_ref,
                  *, tm, rows, spt, tpb):
    i = pl.program_id(0)
    w = cg_ref[...] * u_ref[...]
    first = (i % tpb) == 0
    for s in range(spt):
        w_s = w[s * rows:(s + 1) * rows]
        if tpb == 1:
            prev = hist_ref[s]
        else:
            prev = jnp.where(first, hist_ref[s], carry_ref[...])
        conv = _conv_rows(ext_ref, w_s, prev, cw_ref, SC_WIDTH, rows)
        y_ref[s * rows:(s + 1) * rows, :] = (bg_ref[s * rows:(s + 1) * rows, :] * conv).astype(BF16)
        tail_ref[s] = w_s[rows - SUBLANES:rows]
    if tpb > 1:
        carry_ref[...] = w[tm - SUBLANES:tm]


def _sconv(proj, conv_w, hist8, *, seq_len, tm):
    m = proj.shape[0]
    assert m % tm == 0
    rows, spt, tpb = _seq_tiling(seq_len, tm)
    nm = m // tm
    hist_map = (lambda i: (i // tpb, 0, 0)) if spt == 1 else (lambda i: (i, 0, 0))
    base = 3 * FOX_WIDTH // SC_DIM
    return pl.pallas_call(
        functools.partial(_sconv_kernel, tm=tm, rows=rows, spt=spt, tpb=tpb),
        grid=(nm,),
        in_specs=[pl.BlockSpec((tm, SC_DIM), lambda i: (i, base)),
                  pl.BlockSpec((tm, SC_DIM), lambda i: (i, base + 1)),
                  pl.BlockSpec((tm, SC_DIM), lambda i: (i, base + 2)),
                  pl.BlockSpec((SC_WIDTH, SC_DIM), lambda i: (0, 0)),
                  pl.BlockSpec((spt, SUBLANES, SC_DIM), hist_map)],
        out_specs=[pl.BlockSpec((tm, SC_DIM), lambda i: (i, 0)),
                   pl.BlockSpec((spt, SUBLANES, SC_DIM), lambda i: (i, 0, 0))],
        out_shape=[jax.ShapeDtypeStruct((m, SC_DIM), BF16),
                   jax.ShapeDtypeStruct((nm * spt, SUBLANES, SC_DIM), F32)],
        scratch_shapes=[pltpu.VMEM((SUBLANES, SC_DIM), F32),
                        pltpu.VMEM((rows + SUBLANES, SC_DIM), F32)],
        compiler_params=_cparams(1),
        name="gated_short_conv",
    )(proj, proj, proj, conv_w, hist8)


def _hist8(state):
    n, w1, c = state.shape
    return jnp.concatenate([jnp.zeros((n, SUBLANES - w1, c), F32), state.astype(F32)], axis=1)


def _tails(tails, n_seq, seq_len, tile_rows, keep):
    per_seq = max(1, seq_len // tile_rows)
    idx = (jnp.arange(n_seq) + 1) * per_seq - 1
    return tails[idx][:, SUBLANES - keep:, :]


def _rope_tables(pos0, length):
    half = RET_DK // 2
    inv = ROPE_BASE ** (-np.arange(half, dtype=np.float64) / half)
    ang = (pos0 + np.arange(length, dtype=np.float64))[:, None] * inv[None, :]
    cos, sin = np.cos(ang), np.sin(ang)
    return (jnp.asarray(np.concatenate([cos, cos], axis=1), F32),
            jnp.asarray(np.concatenate([-sin, sin], axis=1), F32))


def _prep_weights(p):
    d = D_MODEL
    ab_in = p['ab_w_in'][0]
    cd_in = p['cd_w_in'][0]
    f0 = 3 * FOX_WIDTH
    pad_cols = lambda w: jnp.pad(w, ((0, 0), (0, LANES - w.shape[1]))).astype(BF16)
    return dict(
        ab_in=ab_in.astype(BF16),
        ab_small=pad_cols(ab_in[:, AB_MAIN:]),
        cd_in=jnp.concatenate([cd_in[:, :f0], cd_in[:, f0 + FOX_HEADS:]], axis=1).astype(BF16),
        cd_small=pad_cols(cd_in[:, f0:f0 + FOX_HEADS]),
        ab_out=p['ab_w_out'][0].astype(BF16),
        cd_out=p['cd_w_out'][0].astype(BF16),
        ffn_gate=p['ffn_w_gate'].astype(BF16),
        ffn_up=p['ffn_w_up'].astype(BF16),
        ffn_down=p['ffn_w_down'].astype(BF16),
    )


def _trunk(x, pos0, st_ret, st_ssd, st_ssd_conv, c_k, c_v, c_logf, st_sconv, st_ffn, p, wb, t):
    bsz, length, d = x.shape
    m = bsz * length
    xf = x.reshape(m, d)
    zeros = lambda *shape: jnp.zeros(shape, F32)

    proj, dt_proj = _norm_matmul(xf, p['ab_norm_w'][0], wb['ab_in'], wb['ab_small'],
                                 n=AB_MAIN, tm=t['tm_proj'], tn=t['tn_ab'])
    cosf, sinf = _rope_tables(pos0, length)
    ret_state = zeros(bsz, RET_HEADS, RET_DK, RET_DV) if st_ret is None else st_ret
    y_ret, ret_new = _retention(proj, cosf, sinf, ret_state, p['ret_norm_w'][0],
                                bsz=bsz, seq_len=length, c=t['c_ret'])
    ssd_state = zeros(bsz, SSD_HEADS, SSD_HEADDIM, SSD_DSTATE) if st_ssd is None else st_ssd
    ssd_hist = zeros(bsz, SSD_CONV - 1, SSD_CONV_DIM) if st_ssd_conv is None else st_ssd_conv
    y_ssd, ssd_new = _ssd(proj, dt_proj, _hist8(ssd_hist), ssd_state, p['ssd_conv_w'][0], p['ssd_conv_b'][0],
                          p['ssd_dt_bias'][0], p['ssd_A_log'][0], p['ssd_D'][0], p['ssd_norm_w'][0],
                          bsz=bsz, seq_len=length, c=t['c_ssd'])
    xbc_lo = AB_MAIN - SSD_CONV_DIM
    ssd_conv_new = proj.reshape(bsz, length, -1)[:, length - (SSD_CONV - 1):, xbc_lo:AB_MAIN]
    xf = _proj_residual(xf, y_ret, y_ssd, wb['ab_out'], tm=t['tm_out'], tn=t['tn_out'])

    ffn_new = []
    ffn_hist0 = zeros(bsz, FFN_CONV - 1, D_FF) if st_ffn is None else st_ffn[0]
    xf, tails = _conv_ffn(xf, p['ffn_norm_w'][0], wb['ffn_gate'], wb['ffn_up'], wb['ffn_down'],
                          p['ffn_conv_w'][0], p['ffn_conv_b'][0], _hist8(ffn_hist0), p['final_norm_w'],
                          layer=0, seq_len=length, tm=t['tm_ffn'], tf=t['tf_ffn'], ts=t['ts_ffn'], final=False)
    ffn_new.append(_tails(tails, bsz, length, t['tm_ffn'], FFN_CONV - 1))

    proj, fl_proj = _norm_matmul(xf, p['cd_norm_w'][0], wb['cd_in'], wb['cd_small'],
                                 n=CD_MAIN, tm=t['tm_proj'], tn=t['tn_cd'])
    f_bias = jnp.pad(p['fox_f_bias'][0].astype(F32), (0, LANES - FOX_HEADS)).reshape(1, LANES)
    head_shape = (bsz, length, FOX_HEADS, FOX_HEAD_DIM)
    if c_k is None:
        qa, ka, vb, k32, v32, logf = _fox_prep(proj, fl_proj, f_bias, bsz=bsz, seq_len=length, tp=t['t_prep'])
        y_fox = _fox_prompt(qa, ka, vb, bsz=bsz, seq_len=length, t=t['t_fox'], ts=t['ts_fox'])
        logf_new = logf.reshape(bsz, length, LANES)[:, :, :FOX_HEADS]
        k_new, v_new = k32.reshape(head_shape), v32.reshape(head_shape)
    else:
        proj3 = proj.reshape(bsz, length, -1)
        k_new = proj3[:, :, FOX_WIDTH:2 * FOX_WIDTH].reshape(head_shape)
        v_new = proj3[:, :, 2 * FOX_WIDTH:3 * FOX_WIDTH].reshape(head_shape)
        past = c_k.shape[1]
        pairs = bsz * FOX_HEADS
        assert pairs <= LANES and length <= LANES
        to_lanes = lambda a, rows: jnp.pad(jnp.swapaxes(a, 0, 1).reshape(a.shape[1], pairs),
                                           ((0, rows - a.shape[1]), (0, LANES - pairs)))
        from_lanes = lambda a: jnp.swapaxes(a[:length, :pairs].reshape(length, bsz, FOX_HEADS), 0, 1)
        cache_lf = to_lanes(c_logf.astype(F32), past)
        fl_rows = to_lanes(fl_proj.reshape(bsz, length, LANES)[:, :, :FOX_HEADS], LANES)
        bias_lanes = jnp.pad(jnp.tile(p['fox_f_bias'][0].astype(F32), bsz), (0, LANES - pairs)).reshape(1, LANES)
        cum_t_cache, lf_rows, cum_rows, cum_t_new = _decode_cum(cache_lf, fl_rows, bias_lanes, c=t['c_cum'])
        logf_new = from_lanes(lf_rows)
        cq = jnp.broadcast_to(cum_rows[:length, :pairs].T[:, :, None], (pairs, length, LANES))
        y_fox = _fox_decode(proj, c_k, c_v, cq, cum_t_cache[:pairs, None, :], cum_t_new[:pairs, None, :],
                            bsz=bsz, lq=length, tk=t['tk_dec'])
    sc_hist = zeros(bsz, SC_WIDTH - 1, SC_DIM) if st_sconv is None else st_sconv
    y_sc, sc_tails = _sconv(proj, p['sconv_w'][0], _hist8(sc_hist), seq_len=length, tm=t['tm_sc'])
    sconv_new = _tails(sc_tails, bsz, length, t['tm_sc'], SC_WIDTH - 1)
    xf = _proj_residual(xf, y_fox, y_sc, wb['cd_out'], tm=t['tm_out'], tn=t['tn_out'])

    ffn_hist1 = zeros(bsz, FFN_CONV - 1, D_FF) if st_ffn is None else st_ffn[1]
    xf, tails = _conv_ffn(xf, p['ffn_norm_w'][1], wb['ffn_gate'], wb['ffn_up'], wb['ffn_down'],
                          p['ffn_conv_w'][1], p['ffn_conv_b'][1], _hist8(ffn_hist1), p['final_norm_w'],
                          layer=1, seq_len=length, tm=t['tm_ffn'], tf=t['tf_ffn'], ts=t['ts_ffn'], final=True)
    ffn_new.append(_tails(tails, bsz, length, t['tm_ffn'], FFN_CONV - 1))

    return (xf.reshape(bsz, length, d), ret_new[None], ssd_new[None], ssd_conv_new[None], k_new[None],
            v_new[None], logf_new[None], sconv_new[None], jnp.stack(ffn_new))


def _largest_divisor(n, cap, multiple=1):
    best = None
    for cand in range(multiple, min(n, cap) + 1, multiple):
        if n % cand == 0:
            best = cand
    assert best is not None, (n, cap, multiple)
    return best


def _tiles(bsz, length, past=None):
    m = bsz * length
    seq_tile = lambda cap: _largest_divisor(length, cap, SUBLANES)
    row_tile = lambda cap: (_largest_divisor(length, cap, SUBLANES) if length >= cap
                            else _largest_divisor(m, cap, length))
    t = dict(
        tm_proj=row_tile(1024), tn_ab=512, tn_cd=1024,
        tm_out=row_tile(1024), tn_out=1024,
        tm_ffn=row_tile(512), tf_ffn=1024, ts_ffn=512,
        tm_sc=row_tile(512),
        c_ret=seq_tile(256), c_ssd=seq_tile(256),
    )
    if past is None:
        t['t_fox'] = seq_tile(2048)
        t['ts_fox'] = _largest_divisor(t['t_fox'], 256, LANES)
        t['t_prep'] = seq_tile(512)
    else:
        t['tk_dec'] = _largest_divisor(past, 1024, LANES)
        t['c_cum'] = _largest_divisor(past, 256, LANES)
    return t


def kernel(x_prompt, x_sample, state_ret, state_ssd, state_ssd_conv, cache_fox_k, cache_fox_v, cache_fox_logf, state_sconv, state_ffn_conv, ab_norm_w, ab_w_in, ret_norm_w, ssd_conv_w, ssd_conv_b, ssd_dt_bias, ssd_A_log, ssd_D, ssd_norm_w, ab_w_out, cd_norm_w, cd_w_in, fox_f_bias, sconv_w, cd_w_out, ffn_norm_w, ffn_w_gate, ffn_w_up, ffn_conv_w, ffn_conv_b, ffn_w_down, final_norm_w):
    p = dict(ab_norm_w=ab_norm_w, ab_w_in=ab_w_in, ret_norm_w=ret_norm_w, ssd_conv_w=ssd_conv_w,
             ssd_conv_b=ssd_conv_b, ssd_dt_bias=ssd_dt_bias, ssd_A_log=ssd_A_log, ssd_D=ssd_D,
             ssd_norm_w=ssd_norm_w, ab_w_out=ab_w_out, cd_norm_w=cd_norm_w, cd_w_in=cd_w_in,
             fox_f_bias=fox_f_bias, sconv_w=sconv_w, cd_w_out=cd_w_out, ffn_norm_w=ffn_norm_w,
             ffn_w_gate=ffn_w_gate, ffn_w_up=ffn_w_up, ffn_conv_w=ffn_conv_w, ffn_conv_b=ffn_conv_b,
             ffn_w_down=ffn_w_down, final_norm_w=final_norm_w)
    assert x_prompt.shape[-1] == D_MODEL and ab_w_in.shape == (1, D_MODEL, AB_MAIN + SSD_HEADS)
    assert cd_w_in.shape == (1, D_MODEL, CD_MAIN + FOX_HEADS) and ffn_w_gate.shape == (2, D_MODEL, D_FF)
    wb = _prep_weights(p)
    bp, lp_, _ = x_prompt.shape
    bs, ls, _ = x_sample.shape
    past = cache_fox_k.shape[2]
    (y_prompt, p_ret, p_ssd, p_ssd_conv, p_fox_k, p_fox_v, p_fox_logf, p_sconv, p_ffn_conv) = _trunk(
        x_prompt, 0, None, None, None, None, None, None, None, None, p, wb, _tiles(bp, lp_))
    (y_sample, s_ret, s_ssd, s_ssd_conv, s_fox_k, s_fox_v, s_fox_logf, s_sconv, s_ffn_conv) = _trunk(
        x_sample, past, state_ret[0], state_ssd[0], state_ssd_conv[0], cache_fox_k[0], cache_fox_v[0],
        cache_fox_logf[0], state_sconv[0], state_ffn_conv, p, wb, _tiles(bs, ls, past))
    return (y_prompt, y_sample, p_ret, s_ret, p_ssd, s_ssd, p_ssd_conv, s_ssd_conv, p_fox_k, s_fox_k,
            p_fox_v, s_fox_v, p_fox_logf, s_fox_logf, p_sconv, s_sconv, p_ffn_conv, s_ffn_conv)
```

```python
import functools
import math

import numpy as np
import jax
import jax.numpy as jnp
from jax import lax
from jax.experimental import pallas as pl
from jax.experimental.pallas import tpu as pltpu

F32 = jnp.float32
BF16 = jnp.bfloat16
EPS = 1e-6
ROPE_BASE = 10000.0
NEG_INF = float("-inf")

D_MODEL = 2048
RET_HEADS, RET_DK, RET_DV = 4, 128, 256
SSD_DINNER, SSD_HEADDIM, SSD_HEADS, SSD_GROUPS, SSD_DSTATE, SSD_CONV = 1024, 64, 16, 2, 128, 4
SSD_CONV_DIM = SSD_DINNER + 2 * SSD_GROUPS * SSD_DSTATE
FOX_HEADS, FOX_HEAD_DIM = 8, 128
FOX_WIDTH = FOX_HEADS * FOX_HEAD_DIM
SC_DIM, SC_WIDTH = 1024, 3
D_FF, FFN_CONV = 5632, 3
AB_MAIN = 2 * RET_HEADS * RET_DK + 2 * RET_HEADS * RET_DV + SSD_DINNER + SSD_CONV_DIM
AB_PAD = AB_MAIN + 128
CD_MAIN = 3 * FOX_WIDTH + 3 * SC_DIM
CD_PAD = CD_MAIN + 128

LANES = 128
SUBLANES = 8
VMEM_LIMIT = 60 * 1024 * 1024


def _cparams(n_axes):
    return pltpu.CompilerParams(dimension_semantics=("arbitrary",) * n_axes,
                                vmem_limit_bytes=VMEM_LIMIT)


def _rms(xf, w):
    return xf * lax.rsqrt(jnp.mean(xf * xf, axis=-1, keepdims=True) + EPS) * w


def _softplus(x):
    return jnp.maximum(x, 0.0) + jnp.log1p(jnp.exp(-jnp.abs(x)))


def _split3(x):
    hi = x.astype(BF16)
    r1 = x - hi.astype(F32)
    mid = r1.astype(BF16)
    lo = (r1 - mid.astype(F32)).astype(BF16)
    return hi, mid, lo


def _widen(x, n):
    return x[:, 0:n] if n <= LANES else jnp.concatenate([x] * (n // LANES), axis=1)


def _dot(a, b):
    return jnp.dot(a, b, preferred_element_type=F32)


def _dot_nt(a, b):
    return lax.dot_general(a, b, (((1,), (1,)), ((), ())), preferred_element_type=F32)


def _dot_tn(a, b):
    return lax.dot_general(a, b, (((0,), (0,)), ((), ())), preferred_element_type=F32)


def _exact_lhs_dot(m_bf16, x):
    hi, mid, lo = _split3(x)
    return _dot(m_bf16, hi) + _dot(m_bf16, mid) + _dot(m_bf16, lo)


def _exact_rhs_dot(x, m_bf16):
    hi, mid, lo = _split3(x)
    return _dot(hi, m_bf16) + _dot(mid, m_bf16) + _dot(lo, m_bf16)


def _conv_rows(ext_ref, x, prev8, w_ref, width, rows, w_cols=slice(None)):
    ext_ref[0:SUBLANES, :] = prev8
    ext_ref[SUBLANES:SUBLANES + rows, :] = x
    out = None
    for j in range(width):
        off = SUBLANES - (width - 1) + j
        term = ext_ref[off:off + rows, :] * w_ref[j:j + 1, w_cols]
        out = term if out is None else out + term
    return out


def _seq_tiling(seq_len, tile_rows):
    if seq_len >= tile_rows:
        assert seq_len % tile_rows == 0
        return tile_rows, 1, seq_len // tile_rows
    assert tile_rows % seq_len == 0 and seq_len % SUBLANES == 0
    return seq_len, tile_rows // seq_len, 1


def _norm_matmul_kernel(x_ref, nw_ref, w_ref, ws_ref, o_ref, os_ref, h_ref):
    @pl.when(pl.program_id(1) == 0)
    def _():
        h_ref[...] = _rms(x_ref[...], nw_ref[...]).astype(BF16)
        os_ref[...] = _dot(h_ref[...], ws_ref[...])

    o_ref[...] = _dot(h_ref[...], w_ref[...])


def _norm_matmul(x, norm_w, w, w_small, *, n, tm, tn):
    m, d = x.shape
    assert m % tm == 0 and n % tn == 0 and n <= w.shape[1] and w_small.shape == (d, LANES)
    return pl.pallas_call(
        _norm_matmul_kernel,
        grid=(m // tm, n // tn),
        in_specs=[pl.BlockSpec((tm, d), lambda i, j: (i, 0)),
                  pl.BlockSpec((1, d), lambda i, j: (0, 0)),
                  pl.BlockSpec((d, tn), lambda i, j: (0, j)),
                  pl.BlockSpec((d, LANES), lambda i, j: (0, 0))],
        out_specs=[pl.BlockSpec((tm, tn), lambda i, j: (i, j)),
                   pl.BlockSpec((tm, LANES), lambda i, j: (i, 0))],
        out_shape=[jax.ShapeDtypeStruct((m, n), F32),
                   jax.ShapeDtypeStruct((m, LANES), F32)],
        scratch_shapes=[pltpu.VMEM((tm, d), BF16)],
        compiler_params=_cparams(2),
        name="norm_in_proj",
    )(x, norm_w.reshape(1, d), w, w_small)


def _proj_res_kernel(x_ref, a_ref, b_ref, wa_ref, wb_ref, o_ref):
    acc = _dot(a_ref[...], wa_ref[...])
    acc = acc + _dot(b_ref[...], wb_ref[...])
    o_ref[...] = x_ref[...] + acc


def _proj_residual(x, a, b, w, *, tm, tn):
    m, d = x.shape
    ka, kb = a.shape[1], b.shape[1]
    assert m % tm == 0 and d % tn == 0 and ka == kb and w.shape == (ka + kb, d)
    return pl.pallas_call(
        _proj_res_kernel,
        grid=(m // tm, d // tn),
        in_specs=[pl.BlockSpec((tm, tn), lambda i, j: (i, j)),
                  pl.BlockSpec((tm, ka), lambda i, j: (i, 0)),
                  pl.BlockSpec((tm, kb), lambda i, j: (i, 0)),
                  pl.BlockSpec((ka, tn), lambda i, j: (0, j)),
                  pl.BlockSpec((kb, tn), lambda i, j: (1, j))],
        out_specs=pl.BlockSpec((tm, tn), lambda i, j: (i, j)),
        out_shape=jax.ShapeDtypeStruct((m, d), F32),
        compiler_params=_cparams(2),
        name="out_proj_residual",
    )(x, a, b, w, w)


def _ffn_kernel(x_ref, nw_ref, wg_ref, wu_ref, wd_ref, cw_ref, cb_ref, hist_ref, fw_ref,
                o_ref, tail_ref, h_ref, carry_ref, ext_ref,
                *, tm, rows, spt, tpb, nf, ts, nsub, nsub_last, final):
    i = pl.program_id(0)
    f = pl.program_id(1)

    @pl.when(f == 0)
    def _():
        xf = x_ref[...]
        h_ref[...] = _rms(xf, nw_ref[...]).astype(BF16)
        o_ref[...] = xf

    first = (i % tpb) == 0

    def sub_block(sb):
        cols = slice(sb * ts, (sb + 1) * ts)
        h = h_ref[...]
        a = _dot(h, wg_ref[:, cols])
        u = _dot(h, wu_ref[:, cols])
        convs = []
        for s in range(spt):
            a_s = a[s * rows:(s + 1) * rows]
            if tpb == 1:
                prev = hist_ref[s, :, cols]
            else:
                prev = jnp.where(first, hist_ref[s, :, cols], carry_ref[f * nsub + sb])
            convs.append(_conv_rows(ext_ref, a_s, prev, cw_ref, FFN_CONV, rows, cols))
            tail_ref[s, :, cols] = a_s[rows - SUBLANES:rows]
        if tpb > 1:
            carry_ref[f * nsub + sb] = a[tm - SUBLANES:tm]
        conv = convs[0] if spt == 1 else jnp.concatenate(convs, axis=0)
        act = (jax.nn.silu(conv + cb_ref[:, cols]) * u).astype(BF16)
        o_ref[...] += _dot(act, wd_ref[cols, :])

    if nsub_last == nsub:
        for sb in range(nsub):
            sub_block(sb)
    else:
        @pl.when(f < nf - 1)
        def _():
            for sb in range(nsub):
                sub_block(sb)

        @pl.when(f == nf - 1)
        def _():
            for sb in range(nsub_last):
                sub_block(sb)

    if final:
        @pl.when(f == nf - 1)
        def _():
            o_ref[...] = _rms(o_ref[...], fw_ref[...])


def _conv_ffn(x, norm_w, wg, wu, wd, conv_w, conv_b, hist8, final_w, *, layer, seq_len, tm, tf, ts, final):
    m, d = x.shape
    ff = wg.shape[2]
    assert m % tm == 0 and tf % ts == 0 and ff % ts == 0
    rows, spt, tpb = _seq_tiling(seq_len, tm)
    nm, nf = m // tm, pl.cdiv(ff, tf)
    nsub = tf // ts
    nsub_last = (ff - (nf - 1) * tf) // ts
    hist_map = (lambda i, f: (i // tpb, 0, f)) if spt == 1 else (lambda i, f: (i, 0, f))
    kern = functools.partial(_ffn_kernel, tm=tm, rows=rows, spt=spt, tpb=tpb, nf=nf, ts=ts, nsub=nsub,
                             nsub_last=nsub_last, final=final)
    out, tails = pl.pallas_call(
        kern,
        grid=(nm, nf),
        in_specs=[pl.BlockSpec((tm, d), lambda i, f: (i, 0)),
                  pl.BlockSpec((1, d), lambda i, f: (0, 0)),
                  pl.BlockSpec((None, d, tf), lambda i, f: (layer, 0, f)),
                  pl.BlockSpec((None, d, tf), lambda i, f: (layer, 0, f)),
                  pl.BlockSpec((None, tf, d), lambda i, f: (layer, f, 0)),
                  pl.BlockSpec((FFN_CONV, tf), lambda i, f: (0, f)),
                  pl.BlockSpec((1, tf), lambda i, f: (0, f)),
                  pl.BlockSpec((spt, SUBLANES, tf), hist_map),
                  pl.BlockSpec((1, d), lambda i, f: (0, 0))],
        out_specs=[pl.BlockSpec((tm, d), lambda i, f: (i, 0)),
                   pl.BlockSpec((spt, SUBLANES, tf), lambda i, f: (i, 0, f))],
        out_shape=[jax.ShapeDtypeStruct((m, d), F32),
                   jax.ShapeDtypeStruct((nm * spt, SUBLANES, ff), F32)],
        scratch_shapes=[pltpu.VMEM((tm, d), BF16),
                        pltpu.VMEM((nf * nsub, SUBLANES, ts), F32),
                        pltpu.VMEM((rows + SUBLANES, ts), F32)],
        compiler_params=_cparams(2),
        name="conv_ffn",
    )(x, norm_w.reshape(1, d), wg, wu, wd, conv_w, conv_b.reshape(1, ff), hist8, final_w.reshape(1, d))
    return out, tails


def _retention_kernel(q_ref, k_ref, v_ref, g_ref, cos_ref, sin_ref, st_ref, nw_ref,
                      y_ref, so_ref, *, c):
    ci = pl.program_id(1)

    @pl.when(ci == 0)
    def _():
        so_ref[...] = st_ref[...]

    cos = cos_ref[...]
    sin = sin_ref[...]
    ii = lax.broadcasted_iota(jnp.int32, (c, c), 0)
    jj = lax.broadcasted_iota(jnp.int32, (c, c), 1)
    diff = (ii - jj).astype(F32)
    causal = ii >= jj
    ridx = lax.broadcasted_iota(jnp.int32, (c, 1), 0).astype(F32)
    for h in range(RET_HEADS):
        lg = math.log1p(-(2.0 ** (-5.0 - h)))
        q = q_ref[:, h * RET_DK:(h + 1) * RET_DK]
        k = k_ref[:, h * RET_DK:(h + 1) * RET_DK]
        v = v_ref[:, h * RET_DV:(h + 1) * RET_DV]
        qr = q * cos + pltpu.roll(q, RET_DK // 2, 1) * sin
        kr = (k * cos + pltpu.roll(k, RET_DK // 2, 1) * sin) * (RET_DK ** -0.5)
        qb = qr.astype(BF16)
        kb = kr.astype(BF16)
        vb = v.astype(BF16)
        decay = jnp.exp(jnp.where(causal, diff * lg, NEG_INF))
        inner = jnp.exp((ridx + 1.0) * lg)
        sdecay = jnp.exp((c - 1.0 - ridx) * lg)
        s = so_ref[0, h]
        scores = _dot_nt(qb, kb) * decay
        y = _dot(scores.astype(BF16), vb)
        y = y + _dot(qb, s.astype(BF16)) * inner
        kd = (kr * sdecay).astype(BF16)
        so_ref[0, h] = math.exp(c * lg) * s + _dot_tn(kd, vb)
        mu = jnp.mean(y, axis=-1, keepdims=True)
        yc = y - mu
        var = jnp.mean(yc * yc, axis=-1, keepdims=True)
        yn = yc * lax.rsqrt(var + EPS) * nw_ref[:, h * RET_DV:(h + 1) * RET_DV]
        g = g_ref[:, h * RET_DV:(h + 1) * RET_DV]
        y_ref[:, h * RET_DV:(h + 1) * RET_DV] = (jax.nn.silu(g) * yn).astype(BF16)


def _retention(proj, cosf, sinf, state, norm_w, *, bsz, seq_len, c):
    m = proj.shape[0]
    nc = seq_len // c
    assert seq_len % c == 0
    qk_w = RET_HEADS * RET_DK
    v_w = RET_HEADS * RET_DV
    row = lambda b, ci: b * nc + ci
    y, s_new = pl.pallas_call(
        functools.partial(_retention_kernel, c=c),
        grid=(bsz, nc),
        in_specs=[pl.BlockSpec((c, qk_w), lambda b, ci: (row(b, ci), 0)),
                  pl.BlockSpec((c, qk_w), lambda b, ci: (row(b, ci), 1)),
                  pl.BlockSpec((c, v_w), lambda b, ci: (row(b, ci), 1)),
                  pl.BlockSpec((c, v_w), lambda b, ci: (row(b, ci), 2)),
                  pl.BlockSpec((c, RET_DK), lambda b, ci: (ci, 0)),
                  pl.BlockSpec((c, RET_DK), lambda b, ci: (ci, 0)),
                  pl.BlockSpec((1, RET_HEADS, RET_DK, RET_DV), lambda b, ci: (b, 0, 0, 0)),
                  pl.BlockSpec((1, v_w), lambda b, ci: (0, 0))],
        out_specs=[pl.BlockSpec((c, v_w), lambda b, ci: (row(b, ci), 0)),
                   pl.BlockSpec((1, RET_HEADS, RET_DK, RET_DV), lambda b, ci: (b, 0, 0, 0))],
        out_shape=[jax.ShapeDtypeStruct((m, v_w), BF16),
                   jax.ShapeDtypeStruct(state.shape, F32)],
        compiler_params=_cparams(2),
        name="retention",
    )(proj, proj, proj, proj, cosf, sinf, state, norm_w.reshape(1, v_w))
    return y, s_new


def _ssd_kernel(z_ref, xs_ref, bc_ref, dt_ref, hx_ref, hbc_ref, st_ref,
                cwx_ref, cwbc_ref, cbx_ref, cbbc_ref, dtb_ref, alog_ref, dsk_ref, nw_ref,
                tri_ref, exp_ref,
                y_ref, so_ref,
                st_scr, cx_scr, cbc_scr, extx_scr, extbc_scr, yh_scr, *, c, nc):
    ci = pl.program_id(1)
    gw = SSD_DINNER // SSD_GROUPS
    hpg = SSD_HEADS // SSD_GROUPS

    @pl.when(ci == 0)
    def _():
        st_scr[...] = st_ref[0].T
        cx_scr[...] = hx_ref[0]
        cbc_scr[...] = hbc_ref[0]

    xs_raw = xs_ref[...]
    bc_raw = bc_ref[...]
    xs = jax.nn.silu(_conv_rows(extx_scr, xs_raw, cx_scr[...], cwx_ref, SSD_CONV, c) + cbx_ref[...])
    bcm = jax.nn.silu(_conv_rows(extbc_scr, bc_raw, cbc_scr[...], cwbc_ref, SSD_CONV, c) + cbbc_ref[...])
    cx_scr[...] = xs_raw[c - SUBLANES:c]
    cbc_scr[...] = bc_raw[c - SUBLANES:c]

    tri = tri_ref[...]
    expand = exp_ref[...]
    dt = _softplus(dt_ref[...] + dtb_ref[...])
    a = -jnp.exp(alog_ref[...])
    acs = _exact_lhs_dot(tri, dt * a)
    acs_t = acs.T
    acs_last = acs[c - 1:c, :]
    exp_acs = jnp.exp(acs)
    to_end = jnp.exp(acs_last - acs)
    dt_e = _exact_rhs_dot(dt, expand)
    to_end_e = _exact_rhs_dot(to_end, expand)
    exp_acs_e = _exact_rhs_dot(exp_acs, expand)
    chunk_dec_e = _exact_rhs_dot(jnp.exp(acs_last), expand)

    xdt = xs * dt_e
    xdt_b = xdt.astype(BF16)
    xend_b = (xdt * to_end_e).astype(BF16)
    ii = lax.broadcasted_iota(jnp.int32, (c, c), 0)
    jj = lax.broadcasted_iota(jnp.int32, (c, c), 1)
    causal = ii >= jj
    nb = SSD_GROUPS * SSD_DSTATE
    for g in range(SSD_GROUPS):
        b_g = bcm[:, g * SSD_DSTATE:(g + 1) * SSD_DSTATE].astype(BF16)
        c_g = bcm[:, nb + g * SSD_DSTATE:nb + (g + 1) * SSD_DSTATE].astype(BF16)
        cb = _dot_nt(c_g, b_g)
        s_g = st_scr[:, g * gw:(g + 1) * gw]
        y_state = _dot(c_g, s_g.astype(BF16)) * exp_acs_e[:, g * gw:(g + 1) * gw]
        for r in range(hpg):
            hh = g * hpg + r
            seg = acs[:, hh:hh + 1] - acs_t[hh:hh + 1, :]
            lmat = jnp.exp(jnp.where(causal, seg, NEG_INF))
            mm = (cb * lmat).astype(BF16)
            lo = hh * SSD_HEADDIM
            yh_scr[:, lo:lo + SSD_HEADDIM] = (
                _dot(mm, xdt_b[:, lo:lo + SSD_HEADDIM]) + y_state[:, r * SSD_HEADDIM:(r + 1) * SSD_HEADDIM])
        upd = _dot_tn(b_g, xend_b[:, g * gw:(g + 1) * gw])
        st_scr[:, g * gw:(g + 1) * gw] = chunk_dec_e[:, g * gw:(g + 1) * gw] * s_g + upd

    y = yh_scr[...] + dsk_ref[...] * xs
    z = z_ref[...]
    y_ref[...] = _rms(y * jax.nn.silu(z), nw_ref[...]).astype(BF16)

    @pl.when(ci == nc - 1)
    def _():
        so_ref[0] = st_scr[...].T


def _ssd(proj, dt_proj, hist8, state, conv_w, conv_b, dt_bias, a_log, d_skip, norm_w, *, bsz, seq_len, c):
    m = proj.shape[0]
    nc = seq_len // c
    assert seq_len % c == 0
    row = lambda b, ci: b * nc + ci
    const2 = lambda b, ci: (0, 0)
    di, bcw = SSD_DINNER, 2 * SSD_GROUPS * SSD_DSTATE
    tri = jnp.asarray(np.tril(np.ones((c, c), np.float32)), BF16)
    expand = np.zeros((LANES, di), np.float32)
    for h in range(SSD_HEADS):
        expand[h, h * SSD_HEADDIM:(h + 1) * SSD_HEADDIM] = 1.0
    expand = jnp.asarray(expand, BF16)
    pad_row = lambda v: jnp.pad(v.astype(F32), (0, LANES - v.shape[0])).reshape(1, LANES)
    st2 = state.reshape(bsz, di, SSD_DSTATE)
    y, s_new = pl.pallas_call(
        functools.partial(_ssd_kernel, c=c, nc=nc),
        grid=(bsz, nc),
        in_specs=[pl.BlockSpec((c, di), lambda b, ci: (row(b, ci), 3)),
                  pl.BlockSpec((c, di), lambda b, ci: (row(b, ci), 4)),
                  pl.BlockSpec((c, bcw), lambda b, ci: (row(b, ci), 10)),
                  pl.BlockSpec((c, LANES), lambda b, ci: (row(b, ci), 0)),
                  pl.BlockSpec((1, SUBLANES, di), lambda b, ci: (b, 0, 0)),
                  pl.BlockSpec((1, SUBLANES, bcw), lambda b, ci: (b, 0, 2)),
                  pl.BlockSpec((1, di, SSD_DSTATE), lambda b, ci: (b, 0, 0)),
                  pl.BlockSpec((SSD_CONV, di), const2),
                  pl.BlockSpec((SSD_CONV, bcw), lambda b, ci: (0, 2)),
                  pl.BlockSpec((1, di), const2),
                  pl.BlockSpec((1, bcw), lambda b, ci: (0, 2)),
                  pl.BlockSpec((1, LANES), const2),
                  pl.BlockSpec((1, LANES), const2),
                  pl.BlockSpec((1, di), const2),
                  pl.BlockSpec((1, di), const2),
                  pl.BlockSpec((c, c), const2),
                  pl.BlockSpec((LANES, di), const2)],
        out_specs=[pl.BlockSpec((c, di), lambda b, ci: (row(b, ci), 0)),
                   pl.BlockSpec((1, di, SSD_DSTATE), lambda b, ci: (b, 0, 0))],
        out_shape=[jax.ShapeDtypeStruct((m, di), BF16),
                   jax.ShapeDtypeStruct(st2.shape, F32)],
        scratch_shapes=[pltpu.VMEM((SSD_DSTATE, di), F32),
                        pltpu.VMEM((SUBLANES, di), F32),
                        pltpu.VMEM((SUBLANES, bcw), F32),
                        pltpu.VMEM((c + SUBLANES, di), F32),
                        pltpu.VMEM((c + SUBLANES, bcw), F32),
                        pltpu.VMEM((c, di), F32)],
        compiler_params=_cparams(2),
        name="ssd",
    )(proj, proj, proj, dt_proj, hist8, hist8, st2,
      conv_w, conv_w, conv_b.reshape(1, -1), conv_b.reshape(1, -1),
      pad_row(dt_bias), pad_row(a_log), jnp.repeat(d_skip.astype(F32), SSD_HEADDIM).reshape(1, di),
      norm_w.reshape(1, di), tri, expand)
    return y, s_new.reshape(state.shape)


def _decode_cum_kernel(lfc_ref, fl_ref, b_ref, tri_ref, cumt_c_ref, lfn_ref, cumn_ref, cumt_n_ref,
                       carry_ref, *, c, ncb):
    j = pl.program_id(0)

    @pl.when(j == 0)
    def _():
        carry_ref[...] = jnp.zeros_like(carry_ref)

    @pl.when(j < ncb)
    def _():
        cum = _exact_lhs_dot(tri_ref[...], lfc_ref[...]) + carry_ref[...]
        carry_ref[...] = cum[c - 1:c, :]
        cumt_c_ref[...] = cum.T

    @pl.when(j == ncb)
    def _():
        lf = -_softplus(-(fl_ref[...] + b_ref[...]))
        lfn_ref[...] = lf
        cum = _exact_lhs_dot(tri_ref[0:LANES, 0:LANES], lf) + carry_ref[...]
        cumn_ref[...] = cum
        cumt_n_ref[...] = cum.T


def _decode_cum(cache_lf, fl_new, bias, *, c):
    past = cache_lf.shape[0]
    assert past % c == 0 and c % LANES == 0 and fl_new.shape == (LANES, LANES)
    ncb = past // c
    tri = jnp.asarray(np.tril(np.ones((c, c), np.float32)), BF16)
    blk = lambda j: jnp.minimum(j, ncb - 1)
    sq = jax.ShapeDtypeStruct((LANES, LANES), F32)
    return pl.pallas_call(
        functools.partial(_decode_cum_kernel, c=c, ncb=ncb),
        grid=(ncb + 1,),
        in_specs=[pl.BlockSpec((c, LANES), lambda j: (blk(j), 0)),
                  pl.BlockSpec((LANES, LANES), lambda j: (0, 0)),
                  pl.BlockSpec((1, LANES), lambda j: (0, 0)),
                  pl.BlockSpec((c, c), lambda j: (0, 0))],
        out_specs=[pl.BlockSpec((LANES, c), lambda j: (0, blk(j))),
                   pl.BlockSpec((LANES, LANES), lambda j: (0, 0)),
                   pl.BlockSpec((LANES, LANES), lambda j: (0, 0)),
                   pl.BlockSpec((LANES, LANES), lambda j: (0, 0))],
        out_shape=[jax.ShapeDtypeStruct((LANES, past), F32), sq, sq, sq],
        scratch_shapes=[pltpu.VMEM((1, LANES), F32)],
        compiler_params=_cparams(1),
        name="decode_logf_cumsum",
    )(cache_lf, fl_new, bias, tri)


FOX_AUG = 2 * FOX_HEAD_DIM
N_BIAS_PIECES = 3


def _fox_prep_kernel(q_ref, k_ref, v_ref, fl_ref, fb_ref, tri_ref, place_ref, ones_ref,
                     qa_ref, ka_ref, vb_ref, k32_ref, v32_ref, lf_ref, carry_ref, *, tp):
    @pl.when(pl.program_id(1) == 0)
    def _():
        carry_ref[...] = jnp.zeros_like(carry_ref)

    lf = -_softplus(-(fl_ref[...] + fb_ref[...]))
    lf_ref[...] = lf
    cum = _exact_lhs_dot(tri_ref[...], lf) + carry_ref[...]
    carry_ref[...] = cum[tp - 1:tp, :]
    pieces = _split3(cum * (FOX_HEAD_DIM ** 0.5))
    n = N_BIAS_PIECES
    aug_q = ones_ref[0:1, :] + sum(_dot(pieces[r], place_ref[r]) for r in range(n))
    aug_k = ones_ref[1:2, :] - sum(_dot(pieces[r], place_ref[n + r]) for r in range(n))
    for h in range(FOX_HEADS):
        src = slice(h * FOX_HEAD_DIM, (h + 1) * FOX_HEAD_DIM)
        feat = slice(h * FOX_AUG, h * FOX_AUG + FOX_HEAD_DIM)
        bias = slice(h * FOX_AUG + FOX_HEAD_DIM, (h + 1) * FOX_AUG)
        qa_ref[:, feat] = q_ref[:, src].astype(BF16)
        qa_ref[:, bias] = aug_q[:, src].astype(BF16)
        ka_ref[:, feat] = k_ref[:, src].astype(BF16)
        ka_ref[:, bias] = aug_k[:, src].astype(BF16)
    k = k_ref[...]
    v = v_ref[...]
    k32_ref[...] = pltpu.einshape("m(hd)->mhd", k, h=FOX_HEADS)
    v32_ref[...] = pltpu.einshape("m(hd)->mhd", v, h=FOX_HEADS)
    vb_ref[...] = v.astype(BF16)


def _fox_prep(proj, fl_proj, f_bias, *, bsz, seq_len, tp):
    m = proj.shape[0]
    nt = seq_len // tp
    assert seq_len % tp == 0
    w = FOX_WIDTH
    tri = jnp.asarray(np.tril(np.ones((tp, tp), np.float32)), BF16)
    n = N_BIAS_PIECES
    place = np.zeros((2 * n, LANES, w), np.float32)
    ones = np.zeros((SUBLANES, w), np.float32)
    for h in range(FOX_HEADS):
        for r in range(2 * n):
            place[r, h, h * FOX_HEAD_DIM + r] = 1.0
        ones[0, h * FOX_HEAD_DIM + n:h * FOX_HEAD_DIM + 2 * n] = 1.0
        ones[1, h * FOX_HEAD_DIM:h * FOX_HEAD_DIM + n] = 1.0
    row = lambda b, ti: (b * nt + ti, 0)
    const2 = lambda b, ti: (0, 0)
    return pl.pallas_call(
        functools.partial(_fox_prep_kernel, tp=tp),
        grid=(bsz, nt),
        in_specs=[pl.BlockSpec((tp, w), lambda b, ti: (b * nt + ti, 0)),
                  pl.BlockSpec((tp, w), lambda b, ti: (b * nt + ti, 1)),
                  pl.BlockSpec((tp, w), lambda b, ti: (b * nt + ti, 2)),
                  pl.BlockSpec((tp, LANES), row),
                  pl.BlockSpec((1, LANES), const2),
                  pl.BlockSpec((tp, tp), const2),
                  pl.BlockSpec((2 * n, LANES, w), lambda b, ti: (0, 0, 0)),
                  pl.BlockSpec((SUBLANES, w), const2)],
        out_specs=[pl.BlockSpec((tp, FOX_HEADS * FOX_AUG), row),
                   pl.BlockSpec((tp, FOX_HEADS * FOX_AUG), row),
                   pl.BlockSpec((tp, w), row),
                   pl.BlockSpec((tp, FOX_HEADS, FOX_HEAD_DIM), lambda b, ti: (b * nt + ti, 0, 0)),
                   pl.BlockSpec((tp, FOX_HEADS, FOX_HEAD_DIM), lambda b, ti: (b * nt + ti, 0, 0)),
                   pl.BlockSpec((tp, LANES), row)],
        out_shape=[jax.ShapeDtypeStruct((m, FOX_HEADS * FOX_AUG), BF16),
                   jax.ShapeDtypeStruct((m, FOX_HEADS * FOX_AUG), BF16),
                   jax.ShapeDtypeStruct((m, w), BF16),
                   jax.ShapeDtypeStruct((m, FOX_HEADS, FOX_HEAD_DIM), F32),
                   jax.ShapeDtypeStruct((m, FOX_HEADS, FOX_HEAD_DIM), F32),
                   jax.ShapeDtypeStruct((m, LANES), F32)],
        scratch_shapes=[pltpu.VMEM((1, LANES), F32)],
        compiler_params=_cparams(2),
        name="fox_prep",
    )(proj, proj, proj, fl_proj, f_bias, tri, jnp.asarray(place, BF16), jnp.asarray(ones, F32))


def _fox_kernel(qi_ref, ki_ref, q_ref, k_ref, v_ref, o_ref, m_ref, acc_ref, va_ref, *, t, ts):
    step = pl.program_id(2)
    qi = qi_ref[step]
    ki = ki_ref[step]
    to_log2 = (FOX_HEAD_DIM ** -0.5) * math.log2(math.e)
    hd = FOX_HEAD_DIM

    @pl.when(ki == 0)
    def _():
        m_ref[...] = jnp.full_like(m_ref, NEG_INF)
        acc_ref[...] = jnp.zeros_like(acc_ref)
        va_ref[:, hd:2 * hd] = jnp.ones((t, hd), BF16)

    va_ref[:, 0:hd] = v_ref[...]

    def scores(rows, nk, r, diagonal):
        s = _dot_nt(q_ref[rows, :], k_ref[0:nk, :])
        if diagonal:
            ri = lax.broadcasted_iota(jnp.int32, (ts, nk), 0) + r * ts
            ci = lax.broadcasted_iota(jnp.int32, (ts, nk), 1)
            s = jnp.where(ci <= ri, s, NEG_INF)
        return s

    def update(diagonal):
        for r in range(t // ts):
            rows = slice(r * ts, (r + 1) * ts)
            nk = (r + 1) * ts if diagonal else t
            s = scores(rows, nk, r, diagonal)
            m_old = m_ref[rows, :]
            m_new = jnp.maximum(m_old, jnp.max(s, axis=1, keepdims=True))
            m_ref[rows, :] = m_new
            alpha = jnp.exp2((m_old - m_new) * to_log2)
            p = jnp.exp2((s - _widen(m_new, nk)) * to_log2)
            acc_ref[rows, :] = _widen(alpha, 2 * hd) * acc_ref[rows, :] + _dot(p.astype(BF16), va_ref[0:nk, :])

    @pl.when(ki < qi)
    def _():
        update(False)

    @pl.when(ki == qi)
    def _():
        update(True)
        o_ref[...] = (acc_ref[:, 0:hd] / acc_ref[:, hd:2 * hd]).astype(BF16)


def _fox_prompt(qa, ka, vb, *, bsz, seq_len, t, ts):
    m = qa.shape[0]
    nq = seq_len // t
    assert seq_len % t == 0 and t % ts == 0
    pairs = [(qi, ki) for qi in range(nq) for ki in range(qi + 1)]
    qi_tab = jnp.asarray([p[0] for p in pairs], jnp.int32)
    ki_tab = jnp.asarray([p[1] for p in pairs], jnp.int32)
    grid_spec = pltpu.PrefetchScalarGridSpec(
        num_scalar_prefetch=2,
        grid=(bsz, FOX_HEADS, len(pairs)),
        in_specs=[pl.BlockSpec((t, FOX_AUG), lambda b, h, s, qi, ki: (b * nq + qi[s], h)),
                  pl.BlockSpec((t, FOX_AUG), lambda b, h, s, qi, ki: (b * nq + ki[s], h)),
                  pl.BlockSpec((t, FOX_HEAD_DIM), lambda b, h, s, qi, ki: (b * nq + ki[s], h))],
        out_specs=pl.BlockSpec((t, FOX_HEAD_DIM), lambda b, h, s, qi, ki: (b * nq + qi[s], h)),
        scratch_shapes=[pltpu.VMEM((t, LANES), F32), pltpu.VMEM((t, 2 * FOX_HEAD_DIM), F32),
                        pltpu.VMEM((t, 2 * FOX_HEAD_DIM), BF16)],
    )
    return pl.pallas_call(
        functools.partial(_fox_kernel, t=t, ts=ts),
        grid_spec=grid_spec,
        out_shape=jax.ShapeDtypeStruct((m, FOX_WIDTH), BF16),
        compiler_params=_cparams(3),
        name="fox_attention",
    )(qi_tab, ki_tab, qa, ka, vb)


def _fox_decode_kernel(q_ref, kn_ref, vn_ref, kc_ref, vc_ref, cq_ref, ckc_ref, ckn_ref, o_ref,
                       m_ref, l_ref, acc_ref, *, lq, ncb):
    j = pl.program_id(1)
    nh, hd = FOX_HEADS, FOX_HEAD_DIM

    @pl.when(j == 0)
    def _():
        m_ref[...] = jnp.full_like(m_ref, NEG_INF)
        l_ref[...] = jnp.zeros_like(l_ref)
        acc_ref[...] = jnp.zeros_like(acc_ref)

    def attend(k_head, v_head, ck_head, causal):
        for h in range(nh):
            qh = q_ref[:, h * hd:(h + 1) * hd].astype(BF16)
            ck = ck_head(h)
            tk = ck.shape[1]
            s = _dot_nt(qh, k_head(h).astype(BF16)) * (hd ** -0.5)
            s = s + (_widen(cq_ref[h], tk) - ck)
            if causal:
                rows = lax.broadcasted_iota(jnp.int32, (lq, tk), 0)
                cols = lax.broadcasted_iota(jnp.int32, (lq, tk), 1)
                s = jnp.where(cols <= rows, s, NEG_INF)
            m_old = m_ref[h]
            m_new = jnp.maximum(m_old, jnp.max(s, axis=1, keepdims=True))
            alpha = jnp.exp(m_old - m_new)
            p = jnp.exp(s - _widen(m_new, tk))
            l_ref[h] = alpha * l_ref[h] + jnp.sum(p, axis=1, keepdims=True)
            acc_ref[h] = alpha * acc_ref[h] + _dot(p.astype(BF16), v_head(h).astype(BF16))
            m_ref[h] = m_new

    @pl.when(j < ncb)
    def _():
        k_hm = pltpu.einshape("mhd->hmd", kc_ref[0])
        v_hm = pltpu.einshape("mhd->hmd", vc_ref[0])
        attend(lambda h: k_hm[h], lambda h: v_hm[h], lambda h: ckc_ref[h], False)

    @pl.when(j == ncb)
    def _():
        attend(lambda h: kn_ref[:, h * hd:(h + 1) * hd], lambda h: vn_ref[:, h * hd:(h + 1) * hd],
               lambda h: ckn_ref[h][:, 0:lq], True)
        for h in range(nh):
            o_ref[:, h * hd:(h + 1) * hd] = (acc_ref[h] / l_ref[h]).astype(BF16)


def _fox_decode(proj, cache_k, cache_v, cq, ck_cache, ck_new, *, bsz, lq, tk):
    past = cache_k.shape[1]
    assert past % tk == 0 and tk % LANES == 0 and lq <= LANES
    ncb = past // tk
    nh, hd, w = FOX_HEADS, FOX_HEAD_DIM, FOX_WIDTH
    tile = lambda j: jnp.minimum(j, ncb - 1)
    cache_spec = pl.BlockSpec((1, tk, nh, hd), lambda b, j: (b, tile(j), 0, 0))
    return pl.pallas_call(
        functools.partial(_fox_decode_kernel, lq=lq, ncb=ncb),
        grid=(bsz, ncb + 1),
        in_specs=[pl.BlockSpec((lq, w), lambda b, j: (b, 0)),
                  pl.BlockSpec((lq, w), lambda b, j: (b, 1)),
                  pl.BlockSpec((lq, w), lambda b, j: (b, 2)),
                  cache_spec,
                  cache_spec,
                  pl.BlockSpec((nh, lq, LANES), lambda b, j: (b, 0, 0)),
                  pl.BlockSpec((nh, 1, tk), lambda b, j: (b, 0, tile(j))),
                  pl.BlockSpec((nh, 1, LANES), lambda b, j: (b, 0, 0))],
        out_specs=pl.BlockSpec((lq, w), lambda b, j: (b, 0)),
        out_shape=jax.ShapeDtypeStruct((bsz * lq, w), BF16),
        scratch_shapes=[pltpu.VMEM((nh, lq, LANES), F32), pltpu.VMEM((nh, lq, LANES), F32),
                        pltpu.VMEM((nh, lq, hd), F32)],
        compiler_params=_cparams(2),
        name="fox_decode",
    )(proj, proj, proj, cache_k, cache_v, cq, ck_cache, ck_new)


def _sconv_kernel(u_ref, bg_ref, cg_ref, cw_ref, hist_ref, y_ref, tail_ref, carry_ref, ext_ref,
                  *, tm, rows, spt, tpb):
    i = pl.program_id(0)
    w = cg_ref[...] * u_ref[...]
    first = (i % tpb) == 0
    for s in range(spt):
        w_s = w[s * rows:(s + 1) * rows]
        if tpb == 1:
            prev = hist_ref[s]
        else:
            prev = jnp.where(first, hist_ref[s], carry_ref[...])
        conv = _conv_rows(ext_ref, w_s, prev, cw_ref, SC_WIDTH, rows)
        y_ref[s * rows:(s + 1) * rows, :] = (bg_ref[s * rows:(s + 1) * rows, :] * conv).astype(BF16)
        tail_ref[s] = w_s[rows - SUBLANES:rows]
    if tpb > 1:
        carry_ref[...] = w[tm - SUBLANES:tm]


def _sconv(proj, conv_w, hist8, *, seq_len, tm):
    m = proj.shape[0]
    assert m % tm == 0
    rows, spt, tpb = _seq_tiling(seq_len, tm)
    nm = m // tm
    hist_map = (lambda i: (i // tpb, 0, 0)) if spt == 1 else (lambda i: (i, 0, 0))
    base = 3 * FOX_WIDTH // SC_DIM
    return pl.pallas_call(
        functools.partial(_sconv_kernel, tm=tm, rows=rows, spt=spt, tpb=tpb),
        grid=(nm,),
        in_specs=[pl.BlockSpec((tm, SC_DIM), lambda i: (i, base)),
                  pl.BlockSpec((tm, SC_DIM), lambda i: (i, base + 1)),
                  pl.BlockSpec((tm, SC_DIM), lambda i: (i, base + 2)),
                  pl.BlockSpec((SC_WIDTH, SC_DIM), lambda i: (0, 0)),
                  pl.BlockSpec((spt, SUBLANES, SC_DIM), hist_map)],
        out_specs=[pl.BlockSpec((tm, SC_DIM), lambda i: (i, 0)),
                   pl.BlockSpec((spt, SUBLANES, SC_DIM), lambda i: (i, 0, 0))],
        out_shape=[jax.ShapeDtypeStruct((m, SC_DIM), BF16),
                   jax.ShapeDtypeStruct((nm * spt, SUBLANES, SC_DIM), F32)],
        scratch_shapes=[pltpu.VMEM((SUBLANES, SC_DIM), F32),
                        pltpu.VMEM((rows + SUBLANES, SC_DIM), F32)],
        compiler_params=_cparams(1),
        name="gated_short_conv",
    )(proj, proj, proj, conv_w, hist8)


def _hist8(state):
    n, w1, c = state.shape
    return jnp.concatenate([jnp.zeros((n, SUBLANES - w1, c), F32), state.astype(F32)], axis=1)


def _tails(tails, n_seq, seq_len, tile_rows, keep):
    per_seq = max(1, seq_len // tile_rows)
    idx = (jnp.arange(n_seq) + 1) * per_seq - 1
    return tails[idx][:, SUBLANES - keep:, :]


def _rope_tables(pos0, length):
    half = RET_DK // 2
    inv = ROPE_BASE ** (-np.arange(half, dtype=np.float64) / half)
    ang = (pos0 + np.arange(length, dtype=np.float64))[:, None] * inv[None, :]
    cos, sin = np.cos(ang), np.sin(ang)
    return (jnp.asarray(np.concatenate([cos, cos], axis=1), F32),
            jnp.asarray(np.concatenate([-sin, sin], axis=1), F32))


def _prep_weights(p):
    d = D_MODEL
    ab_in = p['ab_w_in'][0]
    cd_in = p['cd_w_in'][0]
    f0 = 3 * FOX_WIDTH
    pad_cols = lambda w: jnp.pad(w, ((0, 0), (0, LANES - w.shape[1]))).astype(BF16)
    return dict(
        ab_in=ab_in.astype(BF16),
        ab_small=pad_cols(ab_in[:, AB_MAIN:]),
        cd_in=jnp.concatenate([cd_in[:, :f0], cd_in[:, f0 + FOX_HEADS:]], axis=1).astype(BF16),
        cd_small=pad_cols(cd_in[:, f0:f0 + FOX_HEADS]),
        ab_out=p['ab_w_out'][0].astype(BF16),
        cd_out=p['cd_w_out'][0].astype(BF16),
        ffn_gate=p['ffn_w_gate'].astype(BF16),
        ffn_up=p['ffn_w_up'].astype(BF16),
        ffn_down=p['ffn_w_down'].astype(BF16),
    )


def _trunk(x, pos0, st_ret, st_ssd, st_ssd_conv, c_k, c_v, c_logf, st_sconv, st_ffn, p, wb, t):
    bsz, length, d = x.shape
    m = bsz * length
    xf = x.reshape(m, d)
    zeros = lambda *shape: jnp.zeros(shape, F32)

    proj, dt_proj = _norm_matmul(xf, p['ab_norm_w'][0], wb['ab_in'], wb['ab_small'],
                                 n=AB_MAIN, tm=t['tm_proj'], tn=t['tn_ab'])
    cosf, sinf = _rope_tables(pos0, length)
    ret_state = zeros(bsz, RET_HEADS, RET_DK, RET_DV) if st_ret is None else st_ret
    y_ret, ret_new = _retention(proj, cosf, sinf, ret_state, p['ret_norm_w'][0],
                                bsz=bsz, seq_len=length, c=t['c_ret'])
    ssd_state = zeros(bsz, SSD_HEADS, SSD_HEADDIM, SSD_DSTATE) if st_ssd is None else st_ssd
    ssd_hist = zeros(bsz, SSD_CONV - 1, SSD_CONV_DIM) if st_ssd_conv is None else st_ssd_conv
    y_ssd, ssd_new = _ssd(proj, dt_proj, _hist8(ssd_hist), ssd_state, p['ssd_conv_w'][0], p['ssd_conv_b'][0],
                          p['ssd_dt_bias'][0], p['ssd_A_log'][0], p['ssd_D'][0], p['ssd_norm_w'][0],
                          bsz=bsz, seq_len=length, c=t['c_ssd'])
    xbc_lo = AB_MAIN - SSD_CONV_DIM
    ssd_conv_new = proj.reshape(bsz, length, -1)[:, length - (SSD_CONV - 1):, xbc_lo:AB_MAIN]
    xf = _proj_residual(xf, y_ret, y_ssd, wb['ab_out'], tm=t['tm_out'], tn=t['tn_out'])

    ffn_new = []
    ffn_hist0 = zeros(bsz, FFN_CONV - 1, D_FF) if st_ffn is None else st_ffn[0]
    xf, tails = _conv_ffn(xf, p['ffn_norm_w'][0], wb['ffn_gate'], wb['ffn_up'], wb['ffn_down'],
                          p['ffn_conv_w'][0], p['ffn_conv_b'][0], _hist8(ffn_hist0), p['final_norm_w'],
                          layer=0, seq_len=length, tm=t['tm_ffn'], tf=t['tf_ffn'], ts=t['ts_ffn'], final=False)
    ffn_new.append(_tails(tails, bsz, length, t['tm_ffn'], FFN_CONV - 1))

    proj, fl_proj = _norm_matmul(xf, p['cd_norm_w'][0], wb['cd_in'], wb['cd_small'],
                                 n=CD_MAIN, tm=t['tm_proj'], tn=t['tn_cd'])
    f_bias = jnp.pad(p['fox_f_bias'][0].astype(F32), (0, LANES - FOX_HEADS)).reshape(1, LANES)
    head_shape = (bsz, length, FOX_HEADS, FOX_HEAD_DIM)
    if c_k is None:
        qa, ka, vb, k32, v32, logf = _fox_prep(proj, fl_proj, f_bias, bsz=bsz, seq_len=length, tp=t['t_prep'])
        y_fox = _fox_prompt(qa, ka, vb, bsz=bsz, seq_len=length, t=t['t_fox'], ts=t['ts_fox'])
        logf_new = logf.reshape(bsz, length, LANES)[:, :, :FOX_HEADS]
        k_new, v_new = k32.reshape(head_shape), v32.reshape(head_shape)
    else:
        proj3 = proj.reshape(bsz, length, -1)
        k_new = proj3[:, :, FOX_WIDTH:2 * FOX_WIDTH].reshape(head_shape)
        v_new = proj3[:, :, 2 * FOX_WIDTH:3 * FOX_WIDTH].reshape(head_shape)
        past = c_k.shape[1]
        pairs = bsz * FOX_HEADS
        assert pairs <= LANES and length <= LANES
        to_lanes = lambda a, rows: jnp.pad(jnp.swapaxes(a, 0, 1).reshape(a.shape[1], pairs),
                                           ((0, rows - a.shape[1]), (0, LANES - pairs)))
        from_lanes = lambda a: jnp.swapaxes(a[:length, :pairs].reshape(length, bsz, FOX_HEADS), 0, 1)
        cache_lf = to_lanes(c_logf.astype(F32), past)
        fl_rows = to_lanes(fl_proj.reshape(bsz, length, LANES)[:, :, :FOX_HEADS], LANES)
        bias_lanes = jnp.pad(jnp.tile(p['fox_f_bias'][0].astype(F32), bsz), (0, LANES - pairs)).reshape(1, LANES)
        cum_t_cache, lf_rows, cum_rows, cum_t_new = _decode_cum(cache_lf, fl_rows, bias_lanes, c=t['c_cum'])
        logf_new = from_lanes(lf_rows)
        cq = jnp.broadcast_to(cum_rows[:length, :pairs].T[:, :, None], (pairs, length, LANES))
        y_fox = _fox_decode(proj, c_k, c_v, cq, cum_t_cache[:pairs, None, :], cum_t_new[:pairs, None, :],
                            bsz=bsz, lq=length, tk=t['tk_dec'])
    sc_hist = zeros(bsz, SC_WIDTH - 1, SC_DIM) if st_sconv is None else st_sconv
    y_sc, sc_tails = _sconv(proj, p['sconv_w'][0], _hist8(sc_hist), seq_len=length, tm=t['tm_sc'])
    sconv_new = _tails(sc_tails, bsz, length, t['tm_sc'], SC_WIDTH - 1)
    xf = _proj_residual(xf, y_fox, y_sc, wb['cd_out'], tm=t['tm_out'], tn=t['tn_out'])

    ffn_hist1 = zeros(bsz, FFN_CONV - 1, D_FF) if st_ffn is None else st_ffn[1]
    xf, tails = _conv_ffn(xf, p['ffn_norm_w'][1], wb['ffn_gate'], wb['ffn_up'], wb['ffn_down'],
                          p['ffn_conv_w'][1], p['ffn_conv_b'][1], _hist8(ffn_hist1), p['final_norm_w'],
                          layer=1, seq_len=length, tm=t['tm_ffn'], tf=t['tf_ffn'], ts=t['ts_ffn'], final=True)
    ffn_new.append(_tails(tails, bsz, length, t['tm_ffn'], FFN_CONV - 1))

    return (xf.reshape(bsz, length, d), ret_new[None], ssd_new[None], ssd_conv_new[None], k_new[None],
            v_new[None], logf_new[None], sconv_new[None], jnp.stack(ffn_new))


def _largest_divisor(n, cap, multiple=1):
    best = None
    for cand in range(multiple, min(n, cap) + 1, multiple):
        if n % cand == 0:
            best = cand
    assert best is not None, (n, cap, multiple)
    return best


def _tiles(bsz, length, past=None):
    m = bsz * length
    seq_tile = lambda cap: _largest_divisor(length, cap, SUBLANES)
    row_tile = lambda cap: (_largest_divisor(length, cap, SUBLANES) if length >= cap
                            else _largest_divisor(m, cap, length))
    t = dict(
        tm_proj=row_tile(1024), tn_ab=512, tn_cd=1024,
        tm_out=row_tile(1024), tn_out=1024,
        tm_ffn=row_tile(1024), tf_ffn=512, ts_ffn=512,
        tm_sc=row_tile(512),
        c_ret=seq_tile(256), c_ssd=seq_tile(256),
    )
    if past is None:
        t['t_fox'] = seq_tile(2048)
        t['ts_fox'] = _largest_divisor(t['t_fox'], 256, LANES)
        t['t_prep'] = seq_tile(512)
    else:
        t['tk_dec'] = _largest_divisor(past, 1024, LANES)
        t['c_cum'] = _largest_divisor(past, 256, LANES)
    return t


def kernel(x_prompt, x_sample, state_ret, state_ssd, state_ssd_conv, cache_fox_k, cache_fox_v, cache_fox_logf, state_sconv, state_ffn_conv, ab_norm_w, ab_w_in, ret_norm_w, ssd_conv_w, ssd_conv_b, ssd_dt_bias, ssd_A_log, ssd_D, ssd_norm_w, ab_w_out, cd_norm_w, cd_w_in, fox_f_bias, sconv_w, cd_w_out, ffn_norm_w, ffn_w_gate, ffn_w_up, ffn_conv_w, ffn_conv_b, ffn_w_down, final_norm_w):
    p = dict(ab_norm_w=ab_norm_w, ab_w_in=ab_w_in, ret_norm_w=ret_norm_w, ssd_conv_w=ssd_conv_w,
             ssd_conv_b=ssd_conv_b, ssd_dt_bias=ssd_dt_bias, ssd_A_log=ssd_A_log, ssd_D=ssd_D,
             ssd_norm_w=ssd_norm_w, ab_w_out=ab_w_out, cd_norm_w=cd_norm_w, cd_w_in=cd_w_in,
             fox_f_bias=fox_f_bias, sconv_w=sconv_w, cd_w_out=cd_w_out, ffn_norm_w=ffn_norm_w,
             ffn_w_gate=ffn_w_gate, ffn_w_up=ffn_w_up, ffn_conv_w=ffn_conv_w, ffn_conv_b=ffn_conv_b,
             ffn_w_down=ffn_w_down, final_norm_w=final_norm_w)
    assert x_prompt.shape[-1] == D_MODEL and ab_w_in.shape == (1, D_MODEL, AB_MAIN + SSD_HEADS)
    assert cd_w_in.shape == (1, D_MODEL, CD_MAIN + FOX_HEADS) and ffn_w_gate.shape == (2, D_MODEL, D_FF)
    wb = _prep_weights(p)
    bp, lp_, _ = x_prompt.shape
    bs, ls, _ = x_sample.shape
    past = cache_fox_k.shape[2]
    (y_prompt, p_ret, p_ssd, p_ssd_conv, p_fox_k, p_fox_v, p_fox_logf, p_sconv, p_ffn_conv) = _trunk(
        x_prompt, 0, None, None, None, None, None, None, None, None, p, wb, _tiles(bp, lp_))
    (y_sample, s_ret, s_ssd, s_ssd_conv, s_fox_k, s_fox_v, s_fox_logf, s_sconv, s_ffn_conv) = _trunk(
        x_sample, past, state_ret[0], state_ssd[0], state_ssd_conv[0], cache_fox_k[0], cache_fox_v[0],
        cache_fox_logf[0], state_sconv[0], state_ffn_conv, p, wb, _tiles(bs, ls, past))
    return (y_prompt, y_sample, p_ret, s_ret, p_ssd, s_ssd, p_ssd_conv, s_ssd_conv, p_fox_k, s_fox_k,
            p_fox_v, s_fox_v, p_fox_logf, s_fox_logf, p_sconv, s_sconv, p_ffn_conv, s_ffn_conv)
```

```python
import functools
import math

import numpy as np
import jax
import jax.numpy as jnp
from jax import lax
from jax.experimental import pallas as pl
from jax.experimental.pallas import tpu as pltpu

F32 = jnp.float32
BF16 = jnp.bfloat16
EPS = 1e-6
ROPE_BASE = 10000.0
NEG_INF = float("-inf")

D_MODEL = 2048
RET_HEADS, RET_DK, RET_DV = 4, 128, 256
SSD_DINNER, SSD_HEADDIM, SSD_HEADS, SSD_GROUPS, SSD_DSTATE, SSD_CONV = 1024, 64, 16, 2, 128, 4
SSD_CONV_DIM = SSD_DINNER + 2 * SSD_GROUPS * SSD_DSTATE
FOX_HEADS, FOX_HEAD_DIM = 8, 128
FOX_WIDTH = FOX_HEADS * FOX_HEAD_DIM
SC_DIM, SC_WIDTH = 1024, 3
D_FF, FFN_CONV = 5632, 3
AB_MAIN = 2 * RET_HEADS * RET_DK + 2 * RET_HEADS * RET_DV + SSD_DINNER + SSD_CONV_DIM
AB_PAD = AB_MAIN + 128
CD_MAIN = 3 * FOX_WIDTH + 3 * SC_DIM
CD_PAD = CD_MAIN + 128

LANES = 128
SUBLANES = 8
VMEM_LIMIT = 60 * 1024 * 1024
IN_PROJ_TN = 512


def _cparams(n_axes):
    return pltpu.CompilerParams(dimension_semantics=("arbitrary",) * n_axes,
                                vmem_limit_bytes=VMEM_LIMIT)


def _rms(xf, w):
    return xf * lax.rsqrt(jnp.mean(xf * xf, axis=-1, keepdims=True) + EPS) * w


def _softplus(x):
    return jnp.maximum(x, 0.0) + jnp.log1p(jnp.exp(-jnp.abs(x)))


def _split3(x):
    hi = x.astype(BF16)
    r1 = x - hi.astype(F32)
    mid = r1.astype(BF16)
    lo = (r1 - mid.astype(F32)).astype(BF16)
    return hi, mid, lo


def _widen(x, n):
    return x[:, 0:n] if n <= LANES else jnp.concatenate([x] * (n // LANES), axis=1)


def _dot(a, b):
    return jnp.dot(a, b, preferred_element_type=F32)


def _dot_nt(a, b):
    return lax.dot_general(a, b, (((1,), (1,)), ((), ())), preferred_element_type=F32)


def _dot_tn(a, b):
    return lax.dot_general(a, b, (((0,), (0,)), ((), ())), preferred_element_type=F32)


def _exact_lhs_dot(m_bf16, x):
    hi, mid, lo = _split3(x)
    return _dot(m_bf16, hi) + _dot(m_bf16, mid) + _dot(m_bf16, lo)


def _exact_rhs_dot(x, m_bf16):
    hi, mid, lo = _split3(x)
    return _dot(hi, m_bf16) + _dot(mid, m_bf16) + _dot(lo, m_bf16)


def _conv_rows(ext_ref, x, prev8, w_ref, width, rows, w_cols=slice(None)):
    ext_ref[0:SUBLANES, :] = prev8
    ext_ref[SUBLANES:SUBLANES + rows, :] = x
    out = None
    for j in range(width):
        off = SUBLANES - (width - 1) + j
        term = ext_ref[off:off + rows, :] * w_ref[j:j + 1, w_cols]
        out = term if out is None else out + term
    return out


def _seq_tiling(seq_len, tile_rows):
    if seq_len >= tile_rows:
        assert seq_len % tile_rows == 0
        return tile_rows, 1, seq_len // tile_rows
    assert tile_rows % seq_len == 0 and seq_len % SUBLANES == 0
    return seq_len, tile_rows // seq_len, 1


def _norm_matmul_kernel(x_ref, nw_ref, w_ref, ws_ref, o_ref, os_ref, h_ref):
    @pl.when(pl.program_id(1) == 0)
    def _():
        h_ref[...] = _rms(x_ref[...], nw_ref[...]).astype(BF16)
        os_ref[...] = _dot(h_ref[...], ws_ref[...])

    o_ref[...] = _dot(h_ref[...], w_ref[pl.program_id(1)])


def _norm_matmul(x, norm_w, w_tiles, w_small, *, tm):
    m, d = x.shape
    n_tiles, _, tn = w_tiles.shape
    n = n_tiles * tn
    assert m % tm == 0 and w_tiles.shape[1] == d and w_small.shape == (d, LANES)
    return pl.pallas_call(
        _norm_matmul_kernel,
        grid=(m // tm, n_tiles),
        in_specs=[pl.BlockSpec((tm, d), lambda i, j: (i, 0)),
                  pl.BlockSpec((1, d), lambda i, j: (0, 0)),
                  pl.BlockSpec((n_tiles, d, tn), lambda i, j: (0, 0, 0), pipeline_mode=pl.Buffered(1)),
                  pl.BlockSpec((d, LANES), lambda i, j: (0, 0))],
        out_specs=[pl.BlockSpec((tm, tn), lambda i, j: (i, j)),
                   pl.BlockSpec((tm, LANES), lambda i, j: (i, 0))],
        out_shape=[jax.ShapeDtypeStruct((m, n), F32),
                   jax.ShapeDtypeStruct((m, LANES), F32)],
        scratch_shapes=[pltpu.VMEM((tm, d), BF16)],
        compiler_params=_cparams(2),
        name="norm_in_proj",
    )(x, norm_w.reshape(1, d), w_tiles, w_small)


def _proj_res_kernel(x_ref, a_ref, b_ref, wa_ref, wb_ref, o_ref):
    acc = _dot(a_ref[...], wa_ref[...])
    acc = acc + _dot(b_ref[...], wb_ref[...])
    o_ref[...] = x_ref[...] + acc


def _proj_residual(x, a, b, w, *, tm, tn):
    m, d = x.shape
    ka, kb = a.shape[1], b.shape[1]
    assert m % tm == 0 and d % tn == 0 and ka == kb and w.shape == (ka + kb, d)
    return pl.pallas_call(
        _proj_res_kernel,
        grid=(m // tm, d // tn),
        in_specs=[pl.BlockSpec((tm, tn), lambda i, j: (i, j)),
                  pl.BlockSpec((tm, ka), lambda i, j: (i, 0)),
                  pl.BlockSpec((tm, kb), lambda i, j: (i, 0)),
                  pl.BlockSpec((ka, tn), lambda i, j: (0, j)),
                  pl.BlockSpec((kb, tn), lambda i, j: (1, j))],
        out_specs=pl.BlockSpec((tm, tn), lambda i, j: (i, j)),
        out_shape=jax.ShapeDtypeStruct((m, d), F32),
        compiler_params=_cparams(2),
        name="out_proj_residual",
    )(x, a, b, w, w)


def _ffn_kernel(x_ref, nw_ref, wg_ref, wu_ref, wd_ref, cw_ref, cb_ref, hist_ref, fw_ref,
                o_ref, tail_ref, h_ref, carry_ref, ext_ref,
                *, tm, rows, spt, tpb, nf, ts, nsub, nsub_last, final):
    i = pl.program_id(0)
    f = pl.program_id(1)

    @pl.when(f == 0)
    def _():
        xf = x_ref[...]
        h_ref[...] = _rms(xf, nw_ref[...]).astype(BF16)
        o_ref[...] = xf

    first = (i % tpb) == 0

    def sub_block(sb):
        cols = slice(sb * ts, (sb + 1) * ts)
        h = h_ref[...]
        a = _dot(h, wg_ref[:, cols])
        u = _dot(h, wu_ref[:, cols])
        convs = []
        for s in range(spt):
            a_s = a[s * rows:(s + 1) * rows]
            if tpb == 1:
                prev = hist_ref[s, :, cols]
            else:
                prev = jnp.where(first, hist_ref[s, :, cols], carry_ref[f * nsub + sb])
            convs.append(_conv_rows(ext_ref, a_s, prev, cw_ref, FFN_CONV, rows, cols))
            tail_ref[s, :, cols] = a_s[rows - SUBLANES:rows]
        if tpb > 1:
            carry_ref[f * nsub + sb] = a[tm - SUBLANES:tm]
        conv = convs[0] if spt == 1 else jnp.concatenate(convs, axis=0)
        act = (jax.nn.silu(conv + cb_ref[:, cols]) * u).astype(BF16)
        o_ref[...] += _dot(act, wd_ref[cols, :])

    if nsub_last == nsub:
        for sb in range(nsub):
            sub_block(sb)
    else:
        @pl.when(f < nf - 1)
        def _():
            for sb in range(nsub):
                sub_block(sb)

        @pl.when(f == nf - 1)
        def _():
            for sb in range(nsub_last):
                sub_block(sb)

    if final:
        @pl.when(f == nf - 1)
        def _():
            o_ref[...] = _rms(o_ref[...], fw_ref[...])


def _conv_ffn(x, norm_w, wg, wu, wd, conv_w, conv_b, hist8, final_w, *, layer, seq_len, tm, tf, ts, final):
    m, d = x.shape
    ff = wg.shape[2]
    assert m % tm == 0 and tf % ts == 0 and ff % ts == 0
    rows, spt, tpb = _seq_tiling(seq_len, tm)
    nm, nf = m // tm, pl.cdiv(ff, tf)
    nsub = tf // ts
    nsub_last = (ff - (nf - 1) * tf) // ts
    hist_map = (lambda i, f: (i // tpb, 0, f)) if spt == 1 else (lambda i, f: (i, 0, f))
    kern = functools.partial(_ffn_kernel, tm=tm, rows=rows, spt=spt, tpb=tpb, nf=nf, ts=ts, nsub=nsub,
                             nsub_last=nsub_last, final=final)
    out, tails = pl.pallas_call(
        kern,
        grid=(nm, nf),
        in_specs=[pl.BlockSpec((tm, d), lambda i, f: (i, 0)),
                  pl.BlockSpec((1, d), lambda i, f: (0, 0)),
                  pl.BlockSpec((None, d, tf), lambda i, f: (layer, 0, f)),
                  pl.BlockSpec((None, d, tf), lambda i, f: (layer, 0, f)),
                  pl.BlockSpec((None, tf, d), lambda i, f: (layer, f, 0)),
                  pl.BlockSpec((FFN_CONV, tf), lambda i, f: (0, f)),
                  pl.BlockSpec((1, tf), lambda i, f: (0, f)),
                  pl.BlockSpec((spt, SUBLANES, tf), hist_map),
                  pl.BlockSpec((1, d), lambda i, f: (0, 0))],
        out_specs=[pl.BlockSpec((tm, d), lambda i, f: (i, 0)),
                   pl.BlockSpec((spt, SUBLANES, tf), lambda i, f: (i, 0, f))],
        out_shape=[jax.ShapeDtypeStruct((m, d), F32),
                   jax.ShapeDtypeStruct((nm * spt, SUBLANES, ff), F32)],
        scratch_shapes=[pltpu.VMEM((tm, d), BF16),
                        pltpu.VMEM((nf * nsub, SUBLANES, ts), F32),
                        pltpu.VMEM((rows + SUBLANES, ts), F32)],
        compiler_params=_cparams(2),
        name="conv_ffn",
    )(x, norm_w.reshape(1, d), wg, wu, wd, conv_w, conv_b.reshape(1, ff), hist8, final_w.reshape(1, d))
    return out, tails


def _retention_kernel(q_ref, k_ref, v_ref, g_ref, cos_ref, sin_ref, st_ref, nw_ref,
                      y_ref, so_ref, *, c):
    ci = pl.program_id(1)

    @pl.when(ci == 0)
    def _():
        so_ref[...] = st_ref[...]

    cos = cos_ref[...]
    sin = sin_ref[...]
    ii = lax.broadcasted_iota(jnp.int32, (c, c), 0)
    jj = lax.broadcasted_iota(jnp.int32, (c, c), 1)
    diff = (ii - jj).astype(F32)
    causal = ii >= jj
    ridx = lax.broadcasted_iota(jnp.int32, (c, 1), 0).astype(F32)
    for h in range(RET_HEADS):
        lg = math.log1p(-(2.0 ** (-5.0 - h)))
        q = q_ref[:, h * RET_DK:(h + 1) * RET_DK]
        k = k_ref[:, h * RET_DK:(h + 1) * RET_DK]
        v = v_ref[:, h * RET_DV:(h + 1) * RET_DV]
        qr = q * cos + pltpu.roll(q, RET_DK // 2, 1) * sin
        kr = (k * cos + pltpu.roll(k, RET_DK // 2, 1) * sin) * (RET_DK ** -0.5)
        qb = qr.astype(BF16)
        kb = kr.astype(BF16)
        vb = v.astype(BF16)
        decay = jnp.exp(jnp.where(causal, diff * lg, NEG_INF))
        inner = jnp.exp((ridx + 1.0) * lg)
        sdecay = jnp.exp((c - 1.0 - ridx) * lg)
        s = so_ref[0, h]
        scores = _dot_nt(qb, kb) * decay
        y = _dot(scores.astype(BF16), vb)
        y = y + _dot(qb, s.astype(BF16)) * inner
        kd = (kr * sdecay).astype(BF16)
        so_ref[0, h] = math.exp(c * lg) * s + _dot_tn(kd, vb)
        mu = jnp.mean(y, axis=-1, keepdims=True)
        yc = y - mu
        var = jnp.mean(yc * yc, axis=-1, keepdims=True)
        yn = yc * lax.rsqrt(var + EPS) * nw_ref[:, h * RET_DV:(h + 1) * RET_DV]
        g = g_ref[:, h * RET_DV:(h + 1) * RET_DV]
        y_ref[:, h * RET_DV:(h + 1) * RET_DV] = (jax.nn.silu(g) * yn).astype(BF16)


def _retention(proj, cosf, sinf, state, norm_w, *, bsz, seq_len, c):
    m = proj.shape[0]
    nc = seq_len // c
    assert seq_len % c == 0
    qk_w = RET_HEADS * RET_DK
    v_w = RET_HEADS * RET_DV
    row = lambda b, ci: b * nc + ci
    y, s_new = pl.pallas_call(
        functools.partial(_retention_kernel, c=c),
        grid=(bsz, nc),
        in_specs=[pl.BlockSpec((c, qk_w), lambda b, ci: (row(b, ci), 0)),
                  pl.BlockSpec((c, qk_w), lambda b, ci: (row(b, ci), 1)),
                  pl.BlockSpec((c, v_w), lambda b, ci: (row(b, ci), 1)),
                  pl.BlockSpec((c, v_w), lambda b, ci: (row(b, ci), 2)),
                  pl.BlockSpec((c, RET_DK), lambda b, ci: (ci, 0)),
                  pl.BlockSpec((c, RET_DK), lambda b, ci: (ci, 0)),
                  pl.BlockSpec((1, RET_HEADS, RET_DK, RET_DV), lambda b, ci: (b, 0, 0, 0)),
                  pl.BlockSpec((1, v_w), lambda b, ci: (0, 0))],
        out_specs=[pl.BlockSpec((c, v_w), lambda b, ci: (row(b, ci), 0)),
                   pl.BlockSpec((1, RET_HEADS, RET_DK, RET_DV), lambda b, ci: (b, 0, 0, 0))],
        out_shape=[jax.ShapeDtypeStruct((m, v_w), BF16),
                   jax.ShapeDtypeStruct(state.shape, F32)],
        compiler_params=_cparams(2),
        name="retention",
    )(proj, proj, proj, proj, cosf, sinf, state, norm_w.reshape(1, v_w))
    return y, s_new


def _ssd_kernel(z_ref, xs_ref, bc_ref, dt_ref, hx_ref, hbc_ref, st_ref,
                cwx_ref, cwbc_ref, cbx_ref, cbbc_ref, dtb_ref, alog_ref, dsk_ref, nw_ref,
                tri_ref, exp_ref,
                y_ref, so_ref,
                st_scr, cx_scr, cbc_scr, extx_scr, extbc_scr, yh_scr, *, c, nc):
    ci = pl.program_id(1)
    gw = SSD_DINNER // SSD_GROUPS
    hpg = SSD_HEADS // SSD_GROUPS

    @pl.when(ci == 0)
    def _():
        st_scr[...] = st_ref[0].T
        cx_scr[...] = hx_ref[0]
        cbc_scr[...] = hbc_ref[0]

    xs_raw = xs_ref[...]
    bc_raw = bc_ref[...]
    xs = jax.nn.silu(_conv_rows(extx_scr, xs_raw, cx_scr[...], cwx_ref, SSD_CONV, c) + cbx_ref[...])
    bcm = jax.nn.silu(_conv_rows(extbc_scr, bc_raw, cbc_scr[...], cwbc_ref, SSD_CONV, c) + cbbc_ref[...])
    cx_scr[...] = xs_raw[c - SUBLANES:c]
    cbc_scr[...] = bc_raw[c - SUBLANES:c]

    tri = tri_ref[...]
    expand = exp_ref[...]
    dt = _softplus(dt_ref[...] + dtb_ref[...])
    a = -jnp.exp(alog_ref[...])
    acs = _exact_lhs_dot(tri, dt * a)
    acs_t = acs.T
    acs_last = acs[c - 1:c, :]
    exp_acs = jnp.exp(acs)
    to_end = jnp.exp(acs_last - acs)
    dt_e = _exact_rhs_dot(dt, expand)
    to_end_e = _exact_rhs_dot(to_end, expand)
    exp_acs_e = _exact_rhs_dot(exp_acs, expand)
    chunk_dec_e = _exact_rhs_dot(jnp.exp(acs_last), expand)

    xdt = xs * dt_e
    xdt_b = xdt.astype(BF16)
    xend_b = (xdt * to_end_e).astype(BF16)
    ii = lax.broadcasted_iota(jnp.int32, (c, c), 0)
    jj = lax.broadcasted_iota(jnp.int32, (c, c), 1)
    causal = ii >= jj
    nb = SSD_GROUPS * SSD_DSTATE
    for g in range(SSD_GROUPS):
        b_g = bcm[:, g * SSD_DSTATE:(g + 1) * SSD_DSTATE].astype(BF16)
        c_g = bcm[:, nb + g * SSD_DSTATE:nb + (g + 1) * SSD_DSTATE].astype(BF16)
        cb = _dot_nt(c_g, b_g)
        s_g = st_scr[:, g * gw:(g + 1) * gw]
        y_state = _dot(c_g, s_g.astype(BF16)) * exp_acs_e[:, g * gw:(g + 1) * gw]
        for r in range(hpg):
            hh = g * hpg + r
            seg = acs[:, hh:hh + 1] - acs_t[hh:hh + 1, :]
            lmat = jnp.exp(jnp.where(causal, seg, NEG_INF))
            mm = (cb * lmat).astype(BF16)
            lo = hh * SSD_HEADDIM
            yh_scr[:, lo:lo + SSD_HEADDIM] = (
                _dot(mm, xdt_b[:, lo:lo + SSD_HEADDIM]) + y_state[:, r * SSD_HEADDIM:(r + 1) * SSD_HEADDIM])
        upd = _dot_tn(b_g, xend_b[:, g * gw:(g + 1) * gw])
        st_scr[:, g * gw:(g + 1) * gw] = chunk_dec_e[:, g * gw:(g + 1) * gw] * s_g + upd

    y = yh_scr[...] + dsk_ref[...] * xs
    z = z_ref[...]
    y_ref[...] = _rms(y * jax.nn.silu(z), nw_ref[...]).astype(BF16)

    @pl.when(ci == nc - 1)
    def _():
        so_ref[0] = st_scr[...].T


def _ssd(proj, dt_proj, hist8, state, conv_w, conv_b, dt_bias, a_log, d_skip, norm_w, *, bsz, seq_len, c):
    m = proj.shape[0]
    nc = seq_len // c
    assert seq_len % c == 0
    row = lambda b, ci: b * nc + ci
    const2 = lambda b, ci: (0, 0)
    di, bcw = SSD_DINNER, 2 * SSD_GROUPS * SSD_DSTATE
    tri = jnp.asarray(np.tril(np.ones((c, c), np.float32)), BF16)
    expand = np.zeros((LANES, di), np.float32)
    for h in range(SSD_HEADS):
        expand[h, h * SSD_HEADDIM:(h + 1) * SSD_HEADDIM] = 1.0
    expand = jnp.asarray(expand, BF16)
    pad_row = lambda v: jnp.pad(v.astype(F32), (0, LANES - v.shape[0])).reshape(1, LANES)
    st2 = state.reshape(bsz, di, SSD_DSTATE)
    y, s_new = pl.pallas_call(
        functools.partial(_ssd_kernel, c=c, nc=nc),
        grid=(bsz, nc),
        in_specs=[pl.BlockSpec((c, di), lambda b, ci: (row(b, ci), 3)),
                  pl.BlockSpec((c, di), lambda b, ci: (row(b, ci), 4)),
                  pl.BlockSpec((c, bcw), lambda b, ci: (row(b, ci), 10)),
                  pl.BlockSpec((c, LANES), lambda b, ci: (row(b, ci), 0)),
                  pl.BlockSpec((1, SUBLANES, di), lambda b, ci: (b, 0, 0)),
                  pl.BlockSpec((1, SUBLANES, bcw), lambda b, ci: (b, 0, 2)),
                  pl.BlockSpec((1, di, SSD_DSTATE), lambda b, ci: (b, 0, 0)),
                  pl.BlockSpec((SSD_CONV, di), const2),
                  pl.BlockSpec((SSD_CONV, bcw), lambda b, ci: (0, 2)),
                  pl.BlockSpec((1, di), const2),
                  pl.BlockSpec((1, bcw), lambda b, ci: (0, 2)),
                  pl.BlockSpec((1, LANES), const2),
                  pl.BlockSpec((1, LANES), const2),
                  pl.BlockSpec((1, di), const2),
                  pl.BlockSpec((1, di), const2),
                  pl.BlockSpec((c, c), const2),
                  pl.BlockSpec((LANES, di), const2)],
        out_specs=[pl.BlockSpec((c, di), lambda b, ci: (row(b, ci), 0)),
                   pl.BlockSpec((1, di, SSD_DSTATE), lambda b, ci: (b, 0, 0))],
        out_shape=[jax.ShapeDtypeStruct((m, di), BF16),
                   jax.ShapeDtypeStruct(st2.shape, F32)],
        scratch_shapes=[pltpu.VMEM((SSD_DSTATE, di), F32),
                        pltpu.VMEM((SUBLANES, di), F32),
                        pltpu.VMEM((SUBLANES, bcw), F32),
                        pltpu.VMEM((c + SUBLANES, di), F32),
                        pltpu.VMEM((c + SUBLANES, bcw), F32),
                        pltpu.VMEM((c, di), F32)],
        compiler_params=_cparams(2),
        name="ssd",
    )(proj, proj, proj, dt_proj, hist8, hist8, st2,
      conv_w, conv_w, conv_b.reshape(1, -1), conv_b.reshape(1, -1),
      pad_row(dt_bias), pad_row(a_log), jnp.repeat(d_skip.astype(F32), SSD_HEADDIM).reshape(1, di),
      norm_w.reshape(1, di), tri, expand)
    return y, s_new.reshape(state.shape)


def _decode_cum_kernel(lfc_ref, fl_ref, b_ref, tri_ref, cumt_c_ref, lfn_ref, cumn_ref, cumt_n_ref,
                       carry_ref, *, c, ncb):
    j = pl.program_id(0)

    @pl.when(j == 0)
    def _():
        carry_ref[...] = jnp.zeros_like(carry_ref)

    @pl.when(j < ncb)
    def _():
        cum = _exact_lhs_dot(tri_ref[...], lfc_ref[...]) + carry_ref[...]
        carry_ref[...] = cum[c - 1:c, :]
        cumt_c_ref[...] = cum.T

    @pl.when(j == ncb)
    def _():
        lf = -_softplus(-(fl_ref[...] + b_ref[...]))
        lfn_ref[...] = lf
        cum = _exact_lhs_dot(tri_ref[0:LANES, 0:LANES], lf) + carry_ref[...]
        cumn_ref[...] = cum
        cumt_n_ref[...] = cum.T


def _decode_cum(cache_lf, fl_new, bias, *, c):
    past = cache_lf.shape[0]
    assert past % c == 0 and c % LANES == 0 and fl_new.shape == (LANES, LANES)
    ncb = past // c
    tri = jnp.asarray(np.tril(np.ones((c, c), np.float32)), BF16)
    blk = lambda j: jnp.minimum(j, ncb - 1)
    sq = jax.ShapeDtypeStruct((LANES, LANES), F32)
    return pl.pallas_call(
        functools.partial(_decode_cum_kernel, c=c, ncb=ncb),
        grid=(ncb + 1,),
        in_specs=[pl.BlockSpec((c, LANES), lambda j: (blk(j), 0)),
                  pl.BlockSpec((LANES, LANES), lambda j: (0, 0)),
                  pl.BlockSpec((1, LANES), lambda j: (0, 0)),
                  pl.BlockSpec((c, c), lambda j: (0, 0))],
        out_specs=[pl.BlockSpec((LANES, c), lambda j: (0, blk(j))),
                   pl.BlockSpec((LANES, LANES), lambda j: (0, 0)),
                   pl.BlockSpec((LANES, LANES), lambda j: (0, 0)),
                   pl.BlockSpec((LANES, LANES), lambda j: (0, 0))],
        out_shape=[jax.ShapeDtypeStruct((LANES, past), F32), sq, sq, sq],
        scratch_shapes=[pltpu.VMEM((1, LANES), F32)],
        compiler_params=_cparams(1),
        name="decode_logf_cumsum",
    )(cache_lf, fl_new, bias, tri)


FOX_AUG = 2 * FOX_HEAD_DIM
N_BIAS_PIECES = 3


def _fox_prep_kernel(q_ref, k_ref, v_ref, fl_ref, fb_ref, tri_ref, place_ref, ones_ref,
                     qa_ref, ka_ref, vb_ref, k32_ref, v32_ref, lf_ref, carry_ref, *, tp):
    @pl.when(pl.program_id(1) == 0)
    def _():
        carry_ref[...] = jnp.zeros_like(carry_ref)

    lf = -_softplus(-(fl_ref[...] + fb_ref[...]))
    lf_ref[...] = lf
    cum = _exact_lhs_dot(tri_ref[...], lf) + carry_ref[...]
    carry_ref[...] = cum[tp - 1:tp, :]
    pieces = _split3(cum * (FOX_HEAD_DIM ** 0.5))
    n = N_BIAS_PIECES
    aug_q = ones_ref[0:1, :] + sum(_dot(pieces[r], place_ref[r]) for r in range(n))
    aug_k = ones_ref[1:2, :] - sum(_dot(pieces[r], place_ref[n + r]) for r in range(n))
    for h in range(FOX_HEADS):
        src = slice(h * FOX_HEAD_DIM, (h + 1) * FOX_HEAD_DIM)
        feat = slice(h * FOX_AUG, h * FOX_AUG + FOX_HEAD_DIM)
        bias = slice(h * FOX_AUG + FOX_HEAD_DIM, (h + 1) * FOX_AUG)
        qa_ref[:, feat] = q_ref[:, src].astype(BF16)
        qa_ref[:, bias] = aug_q[:, src].astype(BF16)
        ka_ref[:, feat] = k_ref[:, src].astype(BF16)
        ka_ref[:, bias] = aug_k[:, src].astype(BF16)
    k = k_ref[...]
    v = v_ref[...]
    k32_ref[...] = pltpu.einshape("m(hd)->mhd", k, h=FOX_HEADS)
    v32_ref[...] = pltpu.einshape("m(hd)->mhd", v, h=FOX_HEADS)
    vb_ref[...] = v.astype(BF16)


def _fox_prep(proj, fl_proj, f_bias, *, bsz, seq_len, tp):
    m = proj.shape[0]
    nt = seq_len // tp
    assert seq_len % tp == 0
    w = FOX_WIDTH
    tri = jnp.asarray(np.tril(np.ones((tp, tp), np.float32)), BF16)
    n = N_BIAS_PIECES
    place = np.zeros((2 * n, LANES, w), np.float32)
    ones = np.zeros((SUBLANES, w), np.float32)
    for h in range(FOX_HEADS):
        for r in range(2 * n):
            place[r, h, h * FOX_HEAD_DIM + r] = 1.0
        ones[0, h * FOX_HEAD_DIM + n:h * FOX_HEAD_DIM + 2 * n] = 1.0
        ones[1, h * FOX_HEAD_DIM:h * FOX_HEAD_DIM + n] = 1.0
    row = lambda b, ti: (b * nt + ti, 0)
    const2 = lambda b, ti: (0, 0)
    return pl.pallas_call(
        functools.partial(_fox_prep_kernel, tp=tp),
        grid=(bsz, nt),
        in_specs=[pl.BlockSpec((tp, w), lambda b, ti: (b * nt + ti, 0)),
                  pl.BlockSpec((tp, w), lambda b, ti: (b * nt + ti, 1)),
                  pl.BlockSpec((tp, w), lambda b, ti: (b * nt + ti, 2)),
                  pl.BlockSpec((tp, LANES), row),
                  pl.BlockSpec((1, LANES), const2),
                  pl.BlockSpec((tp, tp), const2),
                  pl.BlockSpec((2 * n, LANES, w), lambda b, ti: (0, 0, 0)),
                  pl.BlockSpec((SUBLANES, w), const2)],
        out_specs=[pl.BlockSpec((tp, FOX_HEADS * FOX_AUG), row),
                   pl.BlockSpec((tp, FOX_HEADS * FOX_AUG), row),
                   pl.BlockSpec((tp, w), row),
                   pl.BlockSpec((tp, FOX_HEADS, FOX_HEAD_DIM), lambda b, ti: (b * nt + ti, 0, 0)),
                   pl.BlockSpec((tp, FOX_HEADS, FOX_HEAD_DIM), lambda b, ti: (b * nt + ti, 0, 0)),
                   pl.BlockSpec((tp, LANES), row)],
        out_shape=[jax.ShapeDtypeStruct((m, FOX_HEADS * FOX_AUG), BF16),
                   jax.ShapeDtypeStruct((m, FOX_HEADS * FOX_AUG), BF16),
                   jax.ShapeDtypeStruct((m, w), BF16),
                   jax.ShapeDtypeStruct((m, FOX_HEADS, FOX_HEAD_DIM), F32),
                   jax.ShapeDtypeStruct((m, FOX_HEADS, FOX_HEAD_DIM), F32),
                   jax.ShapeDtypeStruct((m, LANES), F32)],
        scratch_shapes=[pltpu.VMEM((1, LANES), F32)],
        compiler_params=_cparams(2),
        name="fox_prep",
    )(proj, proj, proj, fl_proj, f_bias, tri, jnp.asarray(place, BF16), jnp.asarray(ones, F32))


def _fox_kernel(qi_ref, ki_ref, q_ref, k_ref, v_ref, o_ref, m_ref, acc_ref, va_ref, *, t, ts):
    step = pl.program_id(2)
    qi = qi_ref[step]
    ki = ki_ref[step]
    to_log2 = (FOX_HEAD_DIM ** -0.5) * math.log2(math.e)
    hd = FOX_HEAD_DIM

    @pl.when(ki == 0)
    def _():
        m_ref[...] = jnp.full_like(m_ref, NEG_INF)
        acc_ref[...] = jnp.zeros_like(acc_ref)
        va_ref[:, hd:2 * hd] = jnp.ones((t, hd), BF16)

    va_ref[:, 0:hd] = v_ref[...]

    def scores(rows, nk, r, diagonal):
        s = _dot_nt(q_ref[rows, :], k_ref[0:nk, :])
        if diagonal:
            ri = lax.broadcasted_iota(jnp.int32, (ts, nk), 0) + r * ts
            ci = lax.broadcasted_iota(jnp.int32, (ts, nk), 1)
            s = jnp.where(ci <= ri, s, NEG_INF)
        return s

    def update(diagonal):
        for r in range(t // ts):
            rows = slice(r * ts, (r + 1) * ts)
            nk = (r + 1) * ts if diagonal else t
            s = scores(rows, nk, r, diagonal)
            m_old = m_ref[rows, :]
            m_new = jnp.maximum(m_old, jnp.max(s, axis=1, keepdims=True))
            m_ref[rows, :] = m_new
            alpha = jnp.exp2((m_old - m_new) * to_log2)
            p = jnp.exp2((s - _widen(m_new, nk)) * to_log2)
            acc_ref[rows, :] = _widen(alpha, 2 * hd) * acc_ref[rows, :] + _dot(p.astype(BF16), va_ref[0:nk, :])

    @pl.when(ki < qi)
    def _():
        update(False)

    @pl.when(ki == qi)
    def _():
        update(True)
        o_ref[...] = (acc_ref[:, 0:hd] / acc_ref[:, hd:2 * hd]).astype(BF16)


def _fox_prompt(qa, ka, vb, *, bsz, seq_len, t, ts):
    m = qa.shape[0]
    nq = seq_len // t
    assert seq_len % t == 0 and t % ts == 0
    pairs = [(qi, ki) for qi in range(nq) for ki in range(qi + 1)]
    qi_tab = jnp.asarray([p[0] for p in pairs], jnp.int32)
    ki_tab = jnp.asarray([p[1] for p in pairs], jnp.int32)
    grid_spec = pltpu.PrefetchScalarGridSpec(
        num_scalar_prefetch=2,
        grid=(bsz, FOX_HEADS, len(pairs)),
        in_specs=[pl.BlockSpec((t, FOX_AUG), lambda b, h, s, qi, ki: (b * nq + qi[s], h)),
                  pl.BlockSpec((t, FOX_AUG), lambda b, h, s, qi, ki: (b * nq + ki[s], h)),
                  pl.BlockSpec((t, FOX_HEAD_DIM), lambda b, h, s, qi, ki: (b * nq + ki[s], h))],
        out_specs=pl.BlockSpec((t, FOX_HEAD_DIM), lambda b, h, s, qi, ki: (b * nq + qi[s], h)),
        scratch_shapes=[pltpu.VMEM((t, LANES), F32), pltpu.VMEM((t, 2 * FOX_HEAD_DIM), F32),
                        pltpu.VMEM((t, 2 * FOX_HEAD_DIM), BF16)],
    )
    return pl.pallas_call(
        functools.partial(_fox_kernel, t=t, ts=ts),
        grid_spec=grid_spec,
        out_shape=jax.ShapeDtypeStruct((m, FOX_WIDTH), BF16),
        compiler_params=_cparams(3),
        name="fox_attention",
    )(qi_tab, ki_tab, qa, ka, vb)


def _fox_decode_kernel(q_ref, kn_ref, vn_ref, kc_ref, vc_ref, cq_ref, ckc_ref, ckn_ref, o_ref,
                       m_ref, l_ref, acc_ref, *, lq, ncb):
    j = pl.program_id(1)
    nh, hd = FOX_HEADS, FOX_HEAD_DIM

    @pl.when(j == 0)
    def _():
        m_ref[...] = jnp.full_like(m_ref, NEG_INF)
        l_ref[...] = jnp.zeros_like(l_ref)
        acc_ref[...] = jnp.zeros_like(acc_ref)

    def attend(k_head, v_head, ck_head, causal):
        for h in range(nh):
            qh = q_ref[:, h * hd:(h + 1) * hd].astype(BF16)
            ck = ck_head(h)
            tk = ck.shape[1]
            s = _dot_nt(qh, k_head(h).astype(BF16)) * (hd ** -0.5)
            s = s + (_widen(cq_ref[h], tk) - ck)
            if causal:
                rows = lax.broadcasted_iota(jnp.int32, (lq, tk), 0)
                cols = lax.broadcasted_iota(jnp.int32, (lq, tk), 1)
                s = jnp.where(cols <= rows, s, NEG_INF)
            m_old = m_ref[h]
            m_new = jnp.maximum(m_old, jnp.max(s, axis=1, keepdims=True))
            alpha = jnp.exp(m_old - m_new)
            p = jnp.exp(s - _widen(m_new, tk))
            l_ref[h] = alpha * l_ref[h] + jnp.sum(p, axis=1, keepdims=True)
            acc_ref[h] = alpha * acc_ref[h] + _dot(p.astype(BF16), v_head(h).astype(BF16))
            m_ref[h] = m_new

    @pl.when(j < ncb)
    def _():
        k_hm = pltpu.einshape("mhd->hmd", kc_ref[0])
        v_hm = pltpu.einshape("mhd->hmd", vc_ref[0])
        attend(lambda h: k_hm[h], lambda h: v_hm[h], lambda h: ckc_ref[h], False)

    @pl.when(j == ncb)
    def _():
        attend(lambda h: kn_ref[:, h * hd:(h + 1) * hd], lambda h: vn_ref[:, h * hd:(h + 1) * hd],
               lambda h: ckn_ref[h][:, 0:lq], True)
        for h in range(nh):
            o_ref[:, h * hd:(h + 1) * hd] = (acc_ref[h] / l_ref[h]).astype(BF16)


def _fox_decode(proj, cache_k, cache_v, cq, ck_cache, ck_new, *, bsz, lq, tk):
    past = cache_k.shape[1]
    assert past % tk == 0 and tk % LANES == 0 and lq <= LANES
    ncb = past // tk
    nh, hd, w = FOX_HEADS, FOX_HEAD_DIM, FOX_WIDTH
    tile = lambda j: jnp.minimum(j, ncb - 1)
    cache_spec = pl.BlockSpec((1, tk, nh, hd), lambda b, j: (b, tile(j), 0, 0))
    return pl.pallas_call(
        functools.partial(_fox_decode_kernel, lq=lq, ncb=ncb),
        grid=(bsz, ncb + 1),
        in_specs=[pl.BlockSpec((lq, w), lambda b, j: (b, 0)),
                  pl.BlockSpec((lq, w), lambda b, j: (b, 1)),
                  pl.BlockSpec((lq, w), lambda b, j: (b, 2)),
                  cache_spec,
                  cache_spec,
                  pl.BlockSpec((nh, lq, LANES), lambda b, j: (b, 0, 0)),
                  pl.BlockSpec((nh, 1, tk), lambda b, j: (b, 0, tile(j))),
                  pl.BlockSpec((nh, 1, LANES), lambda b, j: (b, 0, 0))],
        out_specs=pl.BlockSpec((lq, w), lambda b, j: (b, 0)),
        out_shape=jax.ShapeDtypeStruct((bsz * lq, w), BF16),
        scratch_shapes=[pltpu.VMEM((nh, lq, LANES), F32), pltpu.VMEM((nh, lq, LANES), F32),
                        pltpu.VMEM((nh, lq, hd), F32)],
        compiler_params=_cparams(2),
        name="fox_decode",
    )(proj, proj, proj, cache_k, cache_v, cq, ck_cache, ck_new)


def _sconv_kernel(u_ref, bg_ref, cg_ref, cw_ref, hist_ref, y_ref, tail_ref, carry_ref, ext_ref,
                  *, tm, rows, spt, tpb):
    i = pl.program_id(0)
    w = cg_ref[...] * u_ref[...]
    first = (i % tpb) == 0
    for s in range(spt):
        w_s = w[s * rows:(s + 1) * rows]
        if tpb == 1:
            prev = hist_ref[s]
        else:
            prev = jnp.where(first, hist_ref[s], carry_ref[...])
        conv = _conv_rows(ext_ref, w_s, prev, cw_ref, SC_WIDTH, rows)
        y_ref[s * rows:(s + 1) * rows, :] = (bg_ref[s * rows:(s + 1) * rows, :] * conv).astype(BF16)
        tail_ref[s] = w_s[rows - SUBLANES:rows]
    if tpb > 1:
        carry_ref[...] = w[tm - SUBLANES:tm]


def _sconv(proj, conv_w, hist8, *, seq_len, tm):
    m = proj.shape[0]
    assert m % tm == 0
    rows, spt, tpb = _seq_tiling(seq_len, tm)
    nm = m // tm
    hist_map = (lambda i: (i // tpb, 0, 0)) if spt == 1 else (lambda i: (i, 0, 0))
    base = 3 * FOX_WIDTH // SC_DIM
    return pl.pallas_call(
        functools.partial(_sconv_kernel, tm=tm, rows=rows, spt=spt, tpb=tpb),
        grid=(nm,),
        in_specs=[pl.BlockSpec((tm, SC_DIM), lambda i: (i, base)),
                  pl.BlockSpec((tm, SC_DIM), lambda i: (i, base + 1)),
                  pl.BlockSpec((tm, SC_DIM), lambda i: (i, base + 2)),
                  pl.BlockSpec((SC_WIDTH, SC_DIM), lambda i: (0, 0)),
                  pl.BlockSpec((spt, SUBLANES, SC_DIM), hist_map)],
        out_specs=[pl.BlockSpec((tm, SC_DIM), lambda i: (i, 0)),
                   pl.BlockSpec((spt, SUBLANES, SC_DIM), lambda i: (i, 0, 0))],
        out_shape=[jax.ShapeDtypeStruct((m, SC_DIM), BF16),
                   jax.ShapeDtypeStruct((nm * spt, SUBLANES, SC_DIM), F32)],
        scratch_shapes=[pltpu.VMEM((SUBLANES, SC_DIM), F32),
                        pltpu.VMEM((rows + SUBLANES, SC_DIM), F32)],
        compiler_params=_cparams(1),
        name="gated_short_conv",
    )(proj, proj, proj, conv_w, hist8)


def _hist8(state):
    n, w1, c = state.shape
    return jnp.concatenate([jnp.zeros((n, SUBLANES - w1, c), F32), state.astype(F32)], axis=1)


def _tails(tails, n_seq, seq_len, tile_rows, keep):
    per_seq = max(1, seq_len // tile_rows)
    idx = (jnp.arange(n_seq) + 1) * per_seq - 1
    return tails[idx][:, SUBLANES - keep:, :]


def _rope_tables(pos0, length):
    half = RET_DK // 2
    inv = ROPE_BASE ** (-np.arange(half, dtype=np.float64) / half)
    ang = (pos0 + np.arange(length, dtype=np.float64))[:, None] * inv[None, :]
    cos, sin = np.cos(ang), np.sin(ang)
    return (jnp.asarray(np.concatenate([cos, cos], axis=1), F32),
            jnp.asarray(np.concatenate([-sin, sin], axis=1), F32))


def _prep_weights(p):
    d = D_MODEL
    ab_in = p['ab_w_in'][0]
    cd_in = p['cd_w_in'][0]
    f0 = 3 * FOX_WIDTH
    pad_cols = lambda w: jnp.pad(w, ((0, 0), (0, LANES - w.shape[1]))).astype(BF16)
    col_tiles = lambda w: jnp.swapaxes(w.reshape(d, w.shape[1] // IN_PROJ_TN, IN_PROJ_TN), 0, 1).astype(BF16)
    return dict(
        ab_in=col_tiles(ab_in[:, :AB_MAIN]),
        ab_small=pad_cols(ab_in[:, AB_MAIN:]),
        cd_in=col_tiles(jnp.concatenate([cd_in[:, :f0], cd_in[:, f0 + FOX_HEADS:]], axis=1)),
        cd_small=pad_cols(cd_in[:, f0:f0 + FOX_HEADS]),
        ab_out=p['ab_w_out'][0].astype(BF16),
        cd_out=p['cd_w_out'][0].astype(BF16),
        ffn_gate=p['ffn_w_gate'].astype(BF16),
        ffn_up=p['ffn_w_up'].astype(BF16),
        ffn_down=p['ffn_w_down'].astype(BF16),
    )


def _trunk(x, pos0, st_ret, st_ssd, st_ssd_conv, c_k, c_v, c_logf, st_sconv, st_ffn, p, wb, t):
    bsz, length, d = x.shape
    m = bsz * length
    xf = x.reshape(m, d)
    zeros = lambda *shape: jnp.zeros(shape, F32)

    proj, dt_proj = _norm_matmul(xf, p['ab_norm_w'][0], wb['ab_in'], wb['ab_small'], tm=t['tm_proj'])
    cosf, sinf = _rope_tables(pos0, length)
    ret_state = zeros(bsz, RET_HEADS, RET_DK, RET_DV) if st_ret is None else st_ret
    y_ret, ret_new = _retention(proj, cosf, sinf, ret_state, p['ret_norm_w'][0],
                                bsz=bsz, seq_len=length, c=t['c_ret'])
    ssd_state = zeros(bsz, SSD_HEADS, SSD_HEADDIM, SSD_DSTATE) if st_ssd is None else st_ssd
    ssd_hist = zeros(bsz, SSD_CONV - 1, SSD_CONV_DIM) if st_ssd_conv is None else st_ssd_conv
    y_ssd, ssd_new = _ssd(proj, dt_proj, _hist8(ssd_hist), ssd_state, p['ssd_conv_w'][0], p['ssd_conv_b'][0],
                          p['ssd_dt_bias'][0], p['ssd_A_log'][0], p['ssd_D'][0], p['ssd_norm_w'][0],
                          bsz=bsz, seq_len=length, c=t['c_ssd'])
    xbc_lo = AB_MAIN - SSD_CONV_DIM
    ssd_conv_new = proj.reshape(bsz, length, -1)[:, length - (SSD_CONV - 1):, xbc_lo:AB_MAIN]
    xf = _proj_residual(xf, y_ret, y_ssd, wb['ab_out'], tm=t['tm_out'], tn=t['tn_out'])

    ffn_new = []
    ffn_hist0 = zeros(bsz, FFN_CONV - 1, D_FF) if st_ffn is None else st_ffn[0]
    xf, tails = _conv_ffn(xf, p['ffn_norm_w'][0], wb['ffn_gate'], wb['ffn_up'], wb['ffn_down'],
                          p['ffn_conv_w'][0], p['ffn_conv_b'][0], _hist8(ffn_hist0), p['final_norm_w'],
                          layer=0, seq_len=length, tm=t['tm_ffn'], tf=t['tf_ffn'], ts=t['ts_ffn'], final=False)
    ffn_new.append(_tails(tails, bsz, length, t['tm_ffn'], FFN_CONV - 1))

    proj, fl_proj = _norm_matmul(xf, p['cd_norm_w'][0], wb['cd_in'], wb['cd_small'], tm=t['tm_proj'])
    f_bias = jnp.pad(p['fox_f_bias'][0].astype(F32), (0, LANES - FOX_HEADS)).reshape(1, LANES)
    head_shape = (bsz, length, FOX_HEADS, FOX_HEAD_DIM)
    if c_k is None:
        qa, ka, vb, k32, v32, logf = _fox_prep(proj, fl_proj, f_bias, bsz=bsz, seq_len=length, tp=t['t_prep'])
        y_fox = _fox_prompt(qa, ka, vb, bsz=bsz, seq_len=length, t=t['t_fox'], ts=t['ts_fox'])
        logf_new = logf.reshape(bsz, length, LANES)[:, :, :FOX_HEADS]
        k_new, v_new = k32.reshape(head_shape), v32.reshape(head_shape)
    else:
        proj3 = proj.reshape(bsz, length, -1)
        k_new = proj3[:, :, FOX_WIDTH:2 * FOX_WIDTH].reshape(head_shape)
        v_new = proj3[:, :, 2 * FOX_WIDTH:3 * FOX_WIDTH].reshape(head_shape)
        past = c_k.shape[1]
        pairs = bsz * FOX_HEADS
        assert pairs <= LANES and length <= LANES
        to_lanes = lambda a, rows: jnp.pad(jnp.swapaxes(a, 0, 1).reshape(a.shape[1], pairs),
                                           ((0, rows - a.shape[1]), (0, LANES - pairs)))
        from_lanes = lambda a: jnp.swapaxes(a[:length, :pairs].reshape(length, bsz, FOX_HEADS), 0, 1)
        cache_lf = to_lanes(c_logf.astype(F32), past)
        fl_rows = to_lanes(fl_proj.reshape(bsz, length, LANES)[:, :, :FOX_HEADS], LANES)
        bias_lanes = jnp.pad(jnp.tile(p['fox_f_bias'][0].astype(F32), bsz), (0, LANES - pairs)).reshape(1, LANES)
        cum_t_cache, lf_rows, cum_rows, cum_t_new = _decode_cum(cache_lf, fl_rows, bias_lanes, c=t['c_cum'])
        logf_new = from_lanes(lf_rows)
        cq = jnp.broadcast_to(cum_rows[:length, :pairs].T[:, :, None], (pairs, length, LANES))
        y_fox = _fox_decode(proj, c_k, c_v, cq, cum_t_cache[:pairs, None, :], cum_t_new[:pairs, None, :],
                            bsz=bsz, lq=length, tk=t['tk_dec'])
    sc_hist = zeros(bsz, SC_WIDTH - 1, SC_DIM) if st_sconv is None else st_sconv
    y_sc, sc_tails = _sconv(proj, p['sconv_w'][0], _hist8(sc_hist), seq_len=length, tm=t['tm_sc'])
    sconv_new = _tails(sc_tails, bsz, length, t['tm_sc'], SC_WIDTH - 1)
    xf = _proj_residual(xf, y_fox, y_sc, wb['cd_out'], tm=t['tm_out'], tn=t['tn_out'])

    ffn_hist1 = zeros(bsz, FFN_CONV - 1, D_FF) if st_ffn is None else st_ffn[1]
    xf, tails = _conv_ffn(xf, p['ffn_norm_w'][1], wb['ffn_gate'], wb['ffn_up'], wb['ffn_down'],
                          p['ffn_conv_w'][1], p['ffn_conv_b'][1], _hist8(ffn_hist1), p['final_norm_w'],
                          layer=1, seq_len=length, tm=t['tm_ffn'], tf=t['tf_ffn'], ts=t['ts_ffn'], final=True)
    ffn_new.append(_tails(tails, bsz, length, t['tm_ffn'], FFN_CONV - 1))

    return (xf.reshape(bsz, length, d), ret_new[None], ssd_new[None], ssd_conv_new[None], k_new[None],
            v_new[None], logf_new[None], sconv_new[None], jnp.stack(ffn_new))


def _largest_divisor(n, cap, multiple=1):
    best = None
    for cand in range(multiple, min(n, cap) + 1, multiple):
        if n % cand == 0:
            best = cand
    assert best is not None, (n, cap, multiple)
    return best


def _tiles(bsz, length, past=None):
    m = bsz * length
    seq_tile = lambda cap: _largest_divisor(length, cap, SUBLANES)
    row_tile = lambda cap: (_largest_divisor(length, cap, SUBLANES) if length >= cap
                            else _largest_divisor(m, cap, length))
    t = dict(
        tm_proj=row_tile(1024),
        tm_out=row_tile(1024), tn_out=1024,
        tm_ffn=row_tile(1024), tf_ffn=512, ts_ffn=512,
        tm_sc=row_tile(512),
        c_ret=seq_tile(256), c_ssd=seq_tile(256),
    )
    if past is None:
        t['t_fox'] = seq_tile(2048)
        t['ts_fox'] = _largest_divisor(t['t_fox'], 256, LANES)
        t['t_prep'] = seq_tile(512)
    else:
        t['tk_dec'] = _largest_divisor(past, 1024, LANES)
        t['c_cum'] = _largest_divisor(past, 256, LANES)
    return t


def kernel(x_prompt, x_sample, state_ret, state_ssd, state_ssd_conv, cache_fox_k, cache_fox_v, cache_fox_logf, state_sconv, state_ffn_conv, ab_norm_w, ab_w_in, ret_norm_w, ssd_conv_w, ssd_conv_b, ssd_dt_bias, ssd_A_log, ssd_D, ssd_norm_w, ab_w_out, cd_norm_w, cd_w_in, fox_f_bias, sconv_w, cd_w_out, ffn_norm_w, ffn_w_gate, ffn_w_up, ffn_conv_w, ffn_conv_b, ffn_w_down, final_norm_w):
    p = dict(ab_norm_w=ab_norm_w, ab_w_in=ab_w_in, ret_norm_w=ret_norm_w, ssd_conv_w=ssd_conv_w,
             ssd_conv_b=ssd_conv_b, ssd_dt_bias=ssd_dt_bias, ssd_A_log=ssd_A_log, ssd_D=ssd_D,
             ssd_norm_w=ssd_norm_w, ab_w_out=ab_w_out, cd_norm_w=cd_norm_w, cd_w_in=cd_w_in,
             fox_f_bias=fox_f_bias, sconv_w=sconv_w, cd_w_out=cd_w_out, ffn_norm_w=ffn_norm_w,
             ffn_w_gate=ffn_w_gate, ffn_w_up=ffn_w_up, ffn_conv_w=ffn_conv_w, ffn_conv_b=ffn_conv_b,
             ffn_w_down=ffn_w_down, final_norm_w=final_norm_w)
    assert x_prompt.shape[-1] == D_MODEL and ab_w_in.shape == (1, D_MODEL, AB_MAIN + SSD_HEADS)
    assert cd_w_in.shape == (1, D_MODEL, CD_MAIN + FOX_HEADS) and ffn_w_gate.shape == (2, D_MODEL, D_FF)
    wb = _prep_weights(p)
    bp, lp_, _ = x_prompt.shape
    bs, ls, _ = x_sample.shape
    past = cache_fox_k.shape[2]
    (y_prompt, p_ret, p_ssd, p_ssd_conv, p_fox_k, p_fox_v, p_fox_logf, p_sconv, p_ffn_conv) = _trunk(
        x_prompt, 0, None, None, None, None, None, None, None, None, p, wb, _tiles(bp, lp_))
    (y_sample, s_ret, s_ssd, s_ssd_conv, s_fox_k, s_fox_v, s_fox_logf, s_sconv, s_ffn_conv) = _trunk(
        x_sample, past, state_ret[0], state_ssd[0], state_ssd_conv[0], cache_fox_k[0], cache_fox_v[0],
        cache_fox_logf[0], state_sconv[0], state_ffn_conv, p, wb, _tiles(bs, ls, past))
    return (y_prompt, y_sample, p_ret, s_ret, p_ssd, s_ssd, p_ssd_conv, s_ssd_conv, p_fox_k, s_fox_k,
            p_fox_v, s_fox_v, p_fox_logf, s_fox_logf, p_sconv, s_sconv, p_ffn_conv, s_ffn_conv)
```

```python
import functools
import math

import numpy as np
import jax
import jax.numpy as jnp
from jax import lax
from jax.experimental import pallas as pl
from jax.experimental.pallas import tpu as pltpu

F32 = jnp.float32
BF16 = jnp.bfloat16
EPS = 1e-6
ROPE_BASE = 10000.0
NEG_INF = float("-inf")

D_MODEL = 2048
RET_HEADS, RET_DK, RET_DV = 4, 128, 256
SSD_DINNER, SSD_HEADDIM, SSD_HEADS, SSD_GROUPS, SSD_DSTATE, SSD_CONV = 1024, 64, 16, 2, 128, 4
SSD_CONV_DIM = SSD_DINNER + 2 * SSD_GROUPS * SSD_DSTATE
FOX_HEADS, FOX_HEAD_DIM = 8, 128
FOX_WIDTH = FOX_HEADS * FOX_HEAD_DIM
SC_DIM, SC_WIDTH = 1024, 3
D_FF, FFN_CONV = 5632, 3
AB_MAIN = 2 * RET_HEADS * RET_DK + 2 * RET_HEADS * RET_DV + SSD_DINNER + SSD_CONV_DIM
AB_PAD = AB_MAIN + 128
CD_MAIN = 3 * FOX_WIDTH + 3 * SC_DIM
CD_PAD = CD_MAIN + 128

LANES = 128
SUBLANES = 8
VMEM_LIMIT = 60 * 1024 * 1024


def _cparams(n_axes):
    return pltpu.CompilerParams(dimension_semantics=("arbitrary",) * n_axes,
                                vmem_limit_bytes=VMEM_LIMIT)


def _rms(xf, w):
    return xf * lax.rsqrt(jnp.mean(xf * xf, axis=-1, keepdims=True) + EPS) * w


def _softplus(x):
    return jnp.maximum(x, 0.0) + jnp.log1p(jnp.exp(-jnp.abs(x)))


def _split3(x):
    hi = x.astype(BF16)
    r1 = x - hi.astype(F32)
    mid = r1.astype(BF16)
    lo = (r1 - mid.astype(F32)).astype(BF16)
    return hi, mid, lo


def _widen(x, n):
    return x[:, 0:n] if n <= LANES else jnp.concatenate([x] * (n // LANES), axis=1)


def _dot(a, b):
    return jnp.dot(a, b, preferred_element_type=F32)


def _dot_nt(a, b):
    return lax.dot_general(a, b, (((1,), (1,)), ((), ())), preferred_element_type=F32)


def _dot_tn(a, b):
    return lax.dot_general(a, b, (((0,), (0,)), ((), ())), preferred_element_type=F32)


def _exact_lhs_dot(m_bf16, x):
    hi, mid, lo = _split3(x)
    return _dot(m_bf16, hi) + _dot(m_bf16, mid) + _dot(m_bf16, lo)


def _exact_rhs_dot(x, m_bf16):
    hi, mid, lo = _split3(x)
    return _dot(hi, m_bf16) + _dot(mid, m_bf16) + _dot(lo, m_bf16)


def _conv_rows(ext_ref, x, prev8, w_ref, width, rows, w_cols=slice(None)):
    ext_ref[0:SUBLANES, :] = prev8
    ext_ref[SUBLANES:SUBLANES + rows, :] = x
    out = None
    for j in range(width):
        off = SUBLANES - (width - 1) + j
        term = ext_ref[off:off + rows, :] * w_ref[j:j + 1, w_cols]
        out = term if out is None else out + term
    return out


def _seq_tiling(seq_len, tile_rows):
    if seq_len >= tile_rows:
        assert seq_len % tile_rows == 0
        return tile_rows, 1, seq_len // tile_rows
    assert tile_rows % seq_len == 0 and seq_len % SUBLANES == 0
    return seq_len, tile_rows // seq_len, 1


def _norm_matmul_kernel(x_ref, nw_ref, w_ref, ws_ref, o_ref, os_ref, h_ref):
    @pl.when(pl.program_id(1) == 0)
    def _():
        h_ref[...] = _rms(x_ref[...], nw_ref[...]).astype(BF16)
        os_ref[...] = _dot(h_ref[...], ws_ref[...])

    o_ref[...] = _dot(h_ref[...], w_ref[...])


def _norm_matmul(x, norm_w, w, w_small, *, n, tm, tn):
    m, d = x.shape
    assert m % tm == 0 and n % tn == 0 and n <= w.shape[1] and w_small.shape == (d, LANES)
    return pl.pallas_call(
        _norm_matmul_kernel,
        grid=(m // tm, n // tn),
        in_specs=[pl.BlockSpec((tm, d), lambda i, j: (i, 0)),
                  pl.BlockSpec((1, d), lambda i, j: (0, 0)),
                  pl.BlockSpec((d, tn), lambda i, j: (0, j)),
                  pl.BlockSpec((d, LANES), lambda i, j: (0, 0))],
        out_specs=[pl.BlockSpec((tm, tn), lambda i, j: (i, j)),
                   pl.BlockSpec((tm, LANES), lambda i, j: (i, 0))],
        out_shape=[jax.ShapeDtypeStruct((m, n), F32),
                   jax.ShapeDtypeStruct((m, LANES), F32)],
        scratch_shapes=[pltpu.VMEM((tm, d), BF16)],
        compiler_params=_cparams(2),
        name="norm_in_proj",
    )(x, norm_w.reshape(1, d), w, w_small)


def _proj_res_kernel(x_ref, a_ref, b_ref, wa_ref, wb_ref, o_ref):
    acc = _dot(a_ref[...], wa_ref[...])
    acc = acc + _dot(b_ref[...], wb_ref[...])
    o_ref[...] = x_ref[...] + acc


def _proj_residual(x, a, b, w, *, tm, tn):
    m, d = x.shape
    ka, kb = a.shape[1], b.shape[1]
    assert m % tm == 0 and d % tn == 0 and ka == kb and w.shape == (ka + kb, d)
    return pl.pallas_call(
        _proj_res_kernel,
        grid=(m // tm, d // tn),
        in_specs=[pl.BlockSpec((tm, tn), lambda i, j: (i, j)),
                  pl.BlockSpec((tm, ka), lambda i, j: (i, 0)),
                  pl.BlockSpec((tm, kb), lambda i, j: (i, 0)),
                  pl.BlockSpec((ka, tn), lambda i, j: (0, j)),
                  pl.BlockSpec((kb, tn), lambda i, j: (1, j))],
        out_specs=pl.BlockSpec((tm, tn), lambda i, j: (i, j)),
        out_shape=jax.ShapeDtypeStruct((m, d), F32),
        compiler_params=_cparams(2),
        name="out_proj_residual",
    )(x, a, b, w, w)


def _ffn_kernel(x_ref, nw_ref, wg_ref, wu_ref, wd_ref, cw_ref, cb_ref, hist_ref, fw_ref,
                o_ref, tail_ref, h_ref, carry_ref, ext_ref,
                *, tm, rows, spt, tpb, nf, ts, nsub, nsub_last, final):
    i = pl.program_id(0)
    f = pl.program_id(1)

    @pl.when(f == 0)
    def _():
        xf = x_ref[...]
        h_ref[...] = _rms(xf, nw_ref[...]).astype(BF16)
        o_ref[...] = xf

    first = (i % tpb) == 0

    def sub_block(sb):
        cols = slice(sb * ts, (sb + 1) * ts)
        h = h_ref[...]
        a = _dot(h, wg_ref[:, cols])
        u = _dot(h, wu_ref[:, cols])
        convs = []
        for s in range(spt):
            a_s = a[s * rows:(s + 1) * rows]
            if tpb == 1:
                prev = hist_ref[s, :, cols]
            else:
                prev = jnp.where(first, hist_ref[s, :, cols], carry_ref[f * nsub + sb])
            convs.append(_conv_rows(ext_ref, a_s, prev, cw_ref, FFN_CONV, rows, cols))
            tail_ref[s, :, cols] = a_s[rows - SUBLANES:rows]
        if tpb > 1:
            carry_ref[f * nsub + sb] = a[tm - SUBLANES:tm]
        conv = convs[0] if spt == 1 else jnp.concatenate(convs, axis=0)
        act = (jax.nn.silu(conv + cb_ref[:, cols]) * u).astype(BF16)
        o_ref[...] += _dot(act, wd_ref[cols, :])

    if nsub_last == nsub:
        for sb in range(nsub):
            sub_block(sb)
    else:
        @pl.when(f < nf - 1)
        def _():
            for sb in range(nsub):
                sub_block(sb)

        @pl.when(f == nf - 1)
        def _():
            for sb in range(nsub_last):
                sub_block(sb)

    if final:
        @pl.when(f == nf - 1)
        def _():
            o_ref[...] = _rms(o_ref[...], fw_ref[...])


def _conv_ffn(x, norm_w, wg, wu, wd, conv_w, conv_b, hist8, final_w, *, layer, seq_len, tm, tf, ts, final):
    m, d = x.shape
    ff = wg.shape[2]
    assert m % tm == 0 and tf % ts == 0 and ff % ts == 0
    rows, spt, tpb = _seq_tiling(seq_len, tm)
    nm, nf = m // tm, pl.cdiv(ff, tf)
    nsub = tf // ts
    nsub_last = (ff - (nf - 1) * tf) // ts
    hist_map = (lambda i, f: (i // tpb, 0, f)) if spt == 1 else (lambda i, f: (i, 0, f))
    kern = functools.partial(_ffn_kernel, tm=tm, rows=rows, spt=spt, tpb=tpb, nf=nf, ts=ts, nsub=nsub,
                             nsub_last=nsub_last, final=final)
    out, tails = pl.pallas_call(
        kern,
        grid=(nm, nf),
        in_specs=[pl.BlockSpec((tm, d), lambda i, f: (i, 0)),
                  pl.BlockSpec((1, d), lambda i, f: (0, 0)),
                  pl.BlockSpec((None, d, tf), lambda i, f: (layer, 0, f)),
                  pl.BlockSpec((None, d, tf), lambda i, f: (layer, 0, f)),
                  pl.BlockSpec((None, tf, d), lambda i, f: (layer, f, 0)),
                  pl.BlockSpec((FFN_CONV, tf), lambda i, f: (0, f)),
                  pl.BlockSpec((1, tf), lambda i, f: (0, f)),
                  pl.BlockSpec((spt, SUBLANES, tf), hist_map),
                  pl.BlockSpec((1, d), lambda i, f: (0, 0))],
        out_specs=[pl.BlockSpec((tm, d), lambda i, f: (i, 0)),
                   pl.BlockSpec((spt, SUBLANES, tf), lambda i, f: (i, 0, f))],
        out_shape=[jax.ShapeDtypeStruct((m, d), F32),
                   jax.ShapeDtypeStruct((nm * spt, SUBLANES, ff), F32)],
        scratch_shapes=[pltpu.VMEM((tm, d), BF16),
                        pltpu.VMEM((nf * nsub, SUBLANES, ts), F32),
                        pltpu.VMEM((rows + SUBLANES, ts), F32)],
        compiler_params=_cparams(2),
        name="conv_ffn",
    )(x, norm_w.reshape(1, d), wg, wu, wd, conv_w, conv_b.reshape(1, ff), hist8, final_w.reshape(1, d))
    return out, tails


def _retention_kernel(q_ref, k_ref, v_ref, g_ref, cos_ref, sin_ref, st_ref, nw_ref,
                      y_ref, so_ref, *, c):
    ci = pl.program_id(1)

    @pl.when(ci == 0)
    def _():
        so_ref[...] = st_ref[...]

    cos = cos_ref[...]
    sin = sin_ref[...]
    ii = lax.broadcasted_iota(jnp.int32, (c, c), 0)
    jj = lax.broadcasted_iota(jnp.int32, (c, c), 1)
    diff = (ii - jj).astype(F32)
    causal = ii >= jj
    ridx = lax.broadcasted_iota(jnp.int32, (c, 1), 0).astype(F32)
    for h in range(RET_HEADS):
        lg = math.log1p(-(2.0 ** (-5.0 - h)))
        q = q_ref[:, h * RET_DK:(h + 1) * RET_DK]
        k = k_ref[:, h * RET_DK:(h + 1) * RET_DK]
        v = v_ref[:, h * RET_DV:(h + 1) * RET_DV]
        qr = q * cos + pltpu.roll(q, RET_DK // 2, 1) * sin
        kr = (k * cos + pltpu.roll(k, RET_DK // 2, 1) * sin) * (RET_DK ** -0.5)
        qb = qr.astype(BF16)
        kb = kr.astype(BF16)
        vb = v.astype(BF16)
        decay = jnp.exp(jnp.where(causal, diff * lg, NEG_INF))
        inner = jnp.exp((ridx + 1.0) * lg)
        sdecay = jnp.exp((c - 1.0 - ridx) * lg)
        s = so_ref[0, h]
        scores = _dot_nt(qb, kb) * decay
        y = _dot(scores.astype(BF16), vb)
        y = y + _dot(qb, s.astype(BF16)) * inner
        kd = (kr * sdecay).astype(BF16)
        so_ref[0, h] = math.exp(c * lg) * s + _dot_tn(kd, vb)
        mu = jnp.mean(y, axis=-1, keepdims=True)
        yc = y - mu
        var = jnp.mean(yc * yc, axis=-1, keepdims=True)
        yn = yc * lax.rsqrt(var + EPS) * nw_ref[:, h * RET_DV:(h + 1) * RET_DV]
        g = g_ref[:, h * RET_DV:(h + 1) * RET_DV]
        y_ref[:, h * RET_DV:(h + 1) * RET_DV] = (jax.nn.silu(g) * yn).astype(BF16)


def _retention(proj, cosf, sinf, state, norm_w, *, bsz, seq_len, c):
    m = proj.shape[0]
    nc = seq_len // c
    assert seq_len % c == 0
    qk_w = RET_HEADS * RET_DK
    v_w = RET_HEADS * RET_DV
    row = lambda b, ci: b * nc + ci
    y, s_new = pl.pallas_call(
        functools.partial(_retention_kernel, c=c),
        grid=(bsz, nc),
        in_specs=[pl.BlockSpec((c, qk_w), lambda b, ci: (row(b, ci), 0)),
                  pl.BlockSpec((c, qk_w), lambda b, ci: (row(b, ci), 1)),
                  pl.BlockSpec((c, v_w), lambda b, ci: (row(b, ci), 1)),
                  pl.BlockSpec((c, v_w), lambda b, ci: (row(b, ci), 2)),
                  pl.BlockSpec((c, RET_DK), lambda b, ci: (ci, 0)),
                  pl.BlockSpec((c, RET_DK), lambda b, ci: (ci, 0)),
                  pl.BlockSpec((1, RET_HEADS, RET_DK, RET_DV), lambda b, ci: (b, 0, 0, 0)),
                  pl.BlockSpec((1, v_w), lambda b, ci: (0, 0))],
        out_specs=[pl.BlockSpec((c, v_w), lambda b, ci: (row(b, ci), 0)),
                   pl.BlockSpec((1, RET_HEADS, RET_DK, RET_DV), lambda b, ci: (b, 0, 0, 0))],
        out_shape=[jax.ShapeDtypeStruct((m, v_w), BF16),
                   jax.ShapeDtypeStruct(state.shape, F32)],
        compiler_params=_cparams(2),
        name="retention",
    )(proj, proj, proj, proj, cosf, sinf, state, norm_w.reshape(1, v_w))
    return y, s_new


def _ssd_kernel(z_ref, xs_ref, bc_ref, dt_ref, hx_ref, hbc_ref, st_ref,
                cwx_ref, cwbc_ref, cbx_ref, cbbc_ref, dtb_ref, alog_ref, dsk_ref, nw_ref,
                tri_ref, exp_ref,
                y_ref, so_ref,
                st_scr, cx_scr, cbc_scr, extx_scr, extbc_scr, yh_scr, *, c, nc):
    ci = pl.program_id(1)
    gw = SSD_DINNER // SSD_GROUPS
    hpg = SSD_HEADS // SSD_GROUPS

    @pl.when(ci == 0)
    def _():
        st_scr[...] = st_ref[0].T
        cx_scr[...] = hx_ref[0]
        cbc_scr[...] = hbc_ref[0]

    xs_raw = xs_ref[...]
    bc_raw = bc_ref[...]
    xs = jax.nn.silu(_conv_rows(extx_scr, xs_raw, cx_scr[...], cwx_ref, SSD_CONV, c) + cbx_ref[...])
    bcm = jax.nn.silu(_conv_rows(extbc_scr, bc_raw, cbc_scr[...], cwbc_ref, SSD_CONV, c) + cbbc_ref[...])
    cx_scr[...] = xs_raw[c - SUBLANES:c]
    cbc_scr[...] = bc_raw[c - SUBLANES:c]

    tri = tri_ref[...]
    expand = exp_ref[...]
    dt = _softplus(dt_ref[...] + dtb_ref[...])
    a = -jnp.exp(alog_ref[...])
    acs = _exact_lhs_dot(tri, dt * a)
    acs_t = acs.T
    acs_last = acs[c - 1:c, :]
    exp_acs = jnp.exp(acs)
    to_end = jnp.exp(acs_last - acs)
    dt_e = _exact_rhs_dot(dt, expand)
    to_end_e = _exact_rhs_dot(to_end, expand)
    exp_acs_e = _exact_rhs_dot(exp_acs, expand)
    chunk_dec_e = _exact_rhs_dot(jnp.exp(acs_last), expand)

    xdt = xs * dt_e
    xdt_b = xdt.astype(BF16)
    xend_b = (xdt * to_end_e).astype(BF16)
    ii = lax.broadcasted_iota(jnp.int32, (c, c), 0)
    jj = lax.broadcasted_iota(jnp.int32, (c, c), 1)
    causal = ii >= jj
    nb = SSD_GROUPS * SSD_DSTATE
    for g in range(SSD_GROUPS):
        b_g = bcm[:, g * SSD_DSTATE:(g + 1) * SSD_DSTATE].astype(BF16)
        c_g = bcm[:, nb + g * SSD_DSTATE:nb + (g + 1) * SSD_DSTATE].astype(BF16)
        cb = _dot_nt(c_g, b_g)
        s_g = st_scr[:, g * gw:(g + 1) * gw]
        y_state = _dot(c_g, s_g.astype(BF16)) * exp_acs_e[:, g * gw:(g + 1) * gw]
        for r in range(hpg):
            hh = g * hpg + r
            seg = acs[:, hh:hh + 1] - acs_t[hh:hh + 1, :]
            lmat = jnp.exp(jnp.where(causal, seg, NEG_INF))
            mm = (cb * lmat).astype(BF16)
            lo = hh * SSD_HEADDIM
            yh_scr[:, lo:lo + SSD_HEADDIM] = (
                _dot(mm, xdt_b[:, lo:lo + SSD_HEADDIM]) + y_state[:, r * SSD_HEADDIM:(r + 1) * SSD_HEADDIM])
        upd = _dot_tn(b_g, xend_b[:, g * gw:(g + 1) * gw])
        st_scr[:, g * gw:(g + 1) * gw] = chunk_dec_e[:, g * gw:(g + 1) * gw] * s_g + upd

    y = yh_scr[...] + dsk_ref[...] * xs
    z = z_ref[...]
    y_ref[...] = _rms(y * jax.nn.silu(z), nw_ref[...]).astype(BF16)

    @pl.when(ci == nc - 1)
    def _():
        so_ref[0] = st_scr[...].T


def _ssd(proj, dt_proj, hist8, state, conv_w, conv_b, dt_bias, a_log, d_skip, norm_w, *, bsz, seq_len, c):
    m = proj.shape[0]
    nc = seq_len // c
    assert seq_len % c == 0
    row = lambda b, ci: b * nc + ci
    const2 = lambda b, ci: (0, 0)
    di, bcw = SSD_DINNER, 2 * SSD_GROUPS * SSD_DSTATE
    tri = jnp.asarray(np.tril(np.ones((c, c), np.float32)), BF16)
    expand = np.zeros((LANES, di), np.float32)
    for h in range(SSD_HEADS):
        expand[h, h * SSD_HEADDIM:(h + 1) * SSD_HEADDIM] = 1.0
    expand = jnp.asarray(expand, BF16)
    pad_row = lambda v: jnp.pad(v.astype(F32), (0, LANES - v.shape[0])).reshape(1, LANES)
    st2 = state.reshape(bsz, di, SSD_DSTATE)
    y, s_new = pl.pallas_call(
        functools.partial(_ssd_kernel, c=c, nc=nc),
        grid=(bsz, nc),
        in_specs=[pl.BlockSpec((c, di), lambda b, ci: (row(b, ci), 3)),
                  pl.BlockSpec((c, di), lambda b, ci: (row(b, ci), 4)),
                  pl.BlockSpec((c, bcw), lambda b, ci: (row(b, ci), 10)),
                  pl.BlockSpec((c, LANES), lambda b, ci: (row(b, ci), 0)),
                  pl.BlockSpec((1, SUBLANES, di), lambda b, ci: (b, 0, 0)),
                  pl.BlockSpec((1, SUBLANES, bcw), lambda b, ci: (b, 0, 2)),
                  pl.BlockSpec((1, di, SSD_DSTATE), lambda b, ci: (b, 0, 0)),
                  pl.BlockSpec((SSD_CONV, di), const2),
                  pl.BlockSpec((SSD_CONV, bcw), lambda b, ci: (0, 2)),
                  pl.BlockSpec((1, di), const2),
                  pl.BlockSpec((1, bcw), lambda b, ci: (0, 2)),
                  pl.BlockSpec((1, LANES), const2),
                  pl.BlockSpec((1, LANES), const2),
                  pl.BlockSpec((1, di), const2),
                  pl.BlockSpec((1, di), const2),
                  pl.BlockSpec((c, c), const2),
                  pl.BlockSpec((LANES, di), const2)],
        out_specs=[pl.BlockSpec((c, di), lambda b, ci: (row(b, ci), 0)),
                   pl.BlockSpec((1, di, SSD_DSTATE), lambda b, ci: (b, 0, 0))],
        out_shape=[jax.ShapeDtypeStruct((m, di), BF16),
                   jax.ShapeDtypeStruct(st2.shape, F32)],
        scratch_shapes=[pltpu.VMEM((SSD_DSTATE, di), F32),
                        pltpu.VMEM((SUBLANES, di), F32),
                        pltpu.VMEM((SUBLANES, bcw), F32),
                        pltpu.VMEM((c + SUBLANES, di), F32),
                        pltpu.VMEM((c + SUBLANES, bcw), F32),
                        pltpu.VMEM((c, di), F32)],
        compiler_params=_cparams(2),
        name="ssd",
    )(proj, proj, proj, dt_proj, hist8, hist8, st2,
      conv_w, conv_w, conv_b.reshape(1, -1), conv_b.reshape(1, -1),
      pad_row(dt_bias), pad_row(a_log), jnp.repeat(d_skip.astype(F32), SSD_HEADDIM).reshape(1, di),
      norm_w.reshape(1, di), tri, expand)
    return y, s_new.reshape(state.shape)


def _decode_cum_kernel(lfc_ref, fl_ref, b_ref, tri_ref, cumt_c_ref, lfn_ref, cumn_ref, cumt_n_ref,
                       carry_ref, *, c, ncb):
    j = pl.program_id(0)

    @pl.when(j == 0)
    def _():
        carry_ref[...] = jnp.zeros_like(carry_ref)

    @pl.when(j < ncb)
    def _():
        cum = _exact_lhs_dot(tri_ref[...], lfc_ref[...]) + carry_ref[...]
        carry_ref[...] = cum[c - 1:c, :]
        cumt_c_ref[...] = cum.T

    @pl.when(j == ncb)
    def _():
        lf = -_softplus(-(fl_ref[...] + b_ref[...]))
        lfn_ref[...] = lf
        cum = _exact_lhs_dot(tri_ref[0:LANES, 0:LANES], lf) + carry_ref[...]
        cumn_ref[...] = cum
        cumt_n_ref[...] = cum.T


def _decode_cum(cache_lf, fl_new, bias, *, c):
    past = cache_lf.shape[0]
    assert past % c == 0 and c % LANES == 0 and fl_new.shape == (LANES, LANES)
    ncb = past // c
    tri = jnp.asarray(np.tril(np.ones((c, c), np.float32)), BF16)
    blk = lambda j: jnp.minimum(j, ncb - 1)
    sq = jax.ShapeDtypeStruct((LANES, LANES), F32)
    return pl.pallas_call(
        functools.partial(_decode_cum_kernel, c=c, ncb=ncb),
        grid=(ncb + 1,),
        in_specs=[pl.BlockSpec((c, LANES), lambda j: (blk(j), 0)),
                  pl.BlockSpec((LANES, LANES), lambda j: (0, 0)),
                  pl.BlockSpec((1, LANES), lambda j: (0, 0)),
                  pl.BlockSpec((c, c), lambda j: (0, 0))],
        out_specs=[pl.BlockSpec((LANES, c), lambda j: (0, blk(j))),
                   pl.BlockSpec((LANES, LANES), lambda j: (0, 0)),
                   pl.BlockSpec((LANES, LANES), lambda j: (0, 0)),
                   pl.BlockSpec((LANES, LANES), lambda j: (0, 0))],
        out_shape=[jax.ShapeDtypeStruct((LANES, past), F32), sq, sq, sq],
        scratch_shapes=[pltpu.VMEM((1, LANES), F32)],
        compiler_params=_cparams(1),
        name="decode_logf_cumsum",
    )(cache_lf, fl_new, bias, tri)


FOX_AUG = 2 * FOX_HEAD_DIM
N_BIAS_PIECES = 3


def _fox_prep_kernel(q_ref, k_ref, v_ref, fl_ref, fb_ref, tri_ref, place_ref, ones_ref,
                     qa_ref, ka_ref, vb_ref, k32_ref, v32_ref, lf_ref, carry_ref, *, tp):
    @pl.when(pl.program_id(1) == 0)
    def _():
        carry_ref[...] = jnp.zeros_like(carry_ref)

    lf = -_softplus(-(fl_ref[...] + fb_ref[...]))
    lf_ref[...] = lf
    cum = _exact_lhs_dot(tri_ref[...], lf) + carry_ref[...]
    carry_ref[...] = cum[tp - 1:tp, :]
    pieces = _split3(cum * (FOX_HEAD_DIM ** 0.5))
    n = N_BIAS_PIECES
    aug_q = ones_ref[0:1, :] + sum(_dot(pieces[r], place_ref[r]) for r in range(n))
    aug_k = ones_ref[1:2, :] - sum(_dot(pieces[r], place_ref[n + r]) for r in range(n))
    for h in range(FOX_HEADS):
        src = slice(h * FOX_HEAD_DIM, (h + 1) * FOX_HEAD_DIM)
        feat = slice(h * FOX_AUG, h * FOX_AUG + FOX_HEAD_DIM)
        bias = slice(h * FOX_AUG + FOX_HEAD_DIM, (h + 1) * FOX_AUG)
        qa_ref[:, feat] = q_ref[:, src].astype(BF16)
        qa_ref[:, bias] = aug_q[:, src].astype(BF16)
        ka_ref[:, feat] = k_ref[:, src].astype(BF16)
        ka_ref[:, bias] = aug_k[:, src].astype(BF16)
    k = k_ref[...]
    v = v_ref[...]
    k32_ref[...] = pltpu.einshape("m(hd)->mhd", k, h=FOX_HEADS)
    v32_ref[...] = pltpu.einshape("m(hd)->mhd", v, h=FOX_HEADS)
    vb_ref[...] = v.astype(BF16)


def _fox_prep(proj, fl_proj, f_bias, *, bsz, seq_len, tp):
    m = proj.shape[0]
    nt = seq_len // tp
    assert seq_len % tp == 0
    w = FOX_WIDTH
    tri = jnp.asarray(np.tril(np.ones((tp, tp), np.float32)), BF16)
    n = N_BIAS_PIECES
    place = np.zeros((2 * n, LANES, w), np.float32)
    ones = np.zeros((SUBLANES, w), np.float32)
    for h in range(FOX_HEADS):
        for r in range(2 * n):
            place[r, h, h * FOX_HEAD_DIM + r] = 1.0
        ones[0, h * FOX_HEAD_DIM + n:h * FOX_HEAD_DIM + 2 * n] = 1.0
        ones[1, h * FOX_HEAD_DIM:h * FOX_HEAD_DIM + n] = 1.0
    row = lambda b, ti: (b * nt + ti, 0)
    const2 = lambda b, ti: (0, 0)
    return pl.pallas_call(
        functools.partial(_fox_prep_kernel, tp=tp),
        grid=(bsz, nt),
        in_specs=[pl.BlockSpec((tp, w), lambda b, ti: (b * nt + ti, 0)),
                  pl.BlockSpec((tp, w), lambda b, ti: (b * nt + ti, 1)),
                  pl.BlockSpec((tp, w), lambda b, ti: (b * nt + ti, 2)),
                  pl.BlockSpec((tp, LANES), row),
                  pl.BlockSpec((1, LANES), const2),
                  pl.BlockSpec((tp, tp), const2),
                  pl.BlockSpec((2 * n, LANES, w), lambda b, ti: (0, 0, 0)),
                  pl.BlockSpec((SUBLANES, w), const2)],
        out_specs=[pl.BlockSpec((tp, FOX_HEADS * FOX_AUG), row),
                   pl.BlockSpec((tp, FOX_HEADS * FOX_AUG), row),
                   pl.BlockSpec((tp, w), row),
                   pl.BlockSpec((tp, FOX_HEADS, FOX_HEAD_DIM), lambda b, ti: (b * nt + ti, 0, 0)),
                   pl.BlockSpec((tp, FOX_HEADS, FOX_HEAD_DIM), lambda b, ti: (b * nt + ti, 0, 0)),
                   pl.BlockSpec((tp, LANES), row)],
        out_shape=[jax.ShapeDtypeStruct((m, FOX_HEADS * FOX_AUG), BF16),
                   jax.ShapeDtypeStruct((m, FOX_HEADS * FOX_AUG), BF16),
                   jax.ShapeDtypeStruct((m, w), BF16),
                   jax.ShapeDtypeStruct((m, FOX_HEADS, FOX_HEAD_DIM), F32),
                   jax.ShapeDtypeStruct((m, FOX_HEADS, FOX_HEAD_DIM), F32),
                   jax.ShapeDtypeStruct((m, LANES), F32)],
        scratch_shapes=[pltpu.VMEM((1, LANES), F32)],
        compiler_params=_cparams(2),
        name="fox_prep",
    )(proj, proj, proj, fl_proj, f_bias, tri, jnp.asarray(place, BF16), jnp.asarray(ones, F32))


def _fox_kernel(qi_ref, ki_ref, q_ref, k_ref, v_ref, o_ref, m_ref, acc_ref, va_ref, *, t, ts):
    step = pl.program_id(2)
    qi = qi_ref[step]
    ki = ki_ref[step]
    to_log2 = (FOX_HEAD_DIM ** -0.5) * math.log2(math.e)
    hd = FOX_HEAD_DIM

    @pl.when(ki == 0)
    def _():
        m_ref[...] = jnp.full_like(m_ref, NEG_INF)
        acc_ref[...] = jnp.zeros_like(acc_ref)
        va_ref[:, hd:2 * hd] = jnp.ones((t, hd), BF16)

    va_ref[:, 0:hd] = v_ref[...]

    def scores(rows, nk, r, diagonal):
        s = _dot_nt(q_ref[rows, :], k_ref[0:nk, :])
        if diagonal:
            ri = lax.broadcasted_iota(jnp.int32, (ts, nk), 0) + r * ts
            ci = lax.broadcasted_iota(jnp.int32, (ts, nk), 1)
            s = jnp.where(ci <= ri, s, NEG_INF)
        return s

    def update(diagonal):
        for r in range(t // ts):
            rows = slice(r * ts, (r + 1) * ts)
            nk = (r + 1) * ts if diagonal else t
            s = scores(rows, nk, r, diagonal)
            m_old = m_ref[rows, :]
            m_new = jnp.maximum(m_old, jnp.max(s, axis=1, keepdims=True))
            m_ref[rows, :] = m_new
            alpha = jnp.exp2((m_old - m_new) * to_log2)
            p = jnp.exp2((s - _widen(m_new, nk)) * to_log2)
            acc_ref[rows, :] = _widen(alpha, 2 * hd) * acc_ref[rows, :] + _dot(p.astype(BF16), va_ref[0:nk, :])

    @pl.when(ki < qi)
    def _():
        update(False)

    @pl.when(ki == qi)
    def _():
        update(True)
        o_ref[...] = (acc_ref[:, 0:hd] / acc_ref[:, hd:2 * hd]).astype(BF16)


def _fox_prompt(qa, ka, vb, *, bsz, seq_len, t, ts):
    m = qa.shape[0]
    nq = seq_len // t
    assert seq_len % t == 0 and t % ts == 0
    pairs = [(qi, ki) for qi in range(nq) for ki in range(qi + 1)]
    qi_tab = jnp.asarray([p[0] for p in pairs], jnp.int32)
    ki_tab = jnp.asarray([p[1] for p in pairs], jnp.int32)
    grid_spec = pltpu.PrefetchScalarGridSpec(
        num_scalar_prefetch=2,
        grid=(bsz, FOX_HEADS, len(pairs)),
        in_specs=[pl.BlockSpec((t, FOX_AUG), lambda b, h, s, qi, ki: (b * nq + qi[s], h)),
                  pl.BlockSpec((t, FOX_AUG), lambda b, h, s, qi, ki: (b * nq + ki[s], h)),
                  pl.BlockSpec((t, FOX_HEAD_DIM), lambda b, h, s, qi, ki: (b * nq + ki[s], h))],
        out_specs=pl.BlockSpec((t, FOX_HEAD_DIM), lambda b, h, s, qi, ki: (b * nq + qi[s], h)),
        scratch_shapes=[pltpu.VMEM((t, LANES), F32), pltpu.VMEM((t, 2 * FOX_HEAD_DIM), F32),
                        pltpu.VMEM((t, 2 * FOX_HEAD_DIM), BF16)],
    )
    return pl.pallas_call(
        functools.partial(_fox_kernel, t=t, ts=ts),
        grid_spec=grid_spec,
        out_shape=jax.ShapeDtypeStruct((m, FOX_WIDTH), BF16),
        compiler_params=_cparams(3),
        name="fox_attention",
    )(qi_tab, ki_tab, qa, ka, vb)


def _fox_decode_kernel(q_ref, kn_ref, vn_ref, kc_ref, vc_ref, cq_ref, ckc_ref, ckn_ref, o_ref,
                       m_ref, l_ref, acc_ref, *, lq, ncb):
    j = pl.program_id(1)
    nh, hd = FOX_HEADS, FOX_HEAD_DIM

    @pl.when(j == 0)
    def _():
        m_ref[...] = jnp.full_like(m_ref, NEG_INF)
        l_ref[...] = jnp.zeros_like(l_ref)
        acc_ref[...] = jnp.zeros_like(acc_ref)

    def attend(k_head, v_head, ck_head, causal):
        for h in range(nh):
            qh = q_ref[:, h * hd:(h + 1) * hd].astype(BF16)
            ck = ck_head(h)
            tk = ck.shape[1]
            s = _dot_nt(qh, k_head(h).astype(BF16)) * (hd ** -0.5)
            s = s + (_widen(cq_ref[h], tk) - ck)
            if causal:
                rows = lax.broadcasted_iota(jnp.int32, (lq, tk), 0)
                cols = lax.broadcasted_iota(jnp.int32, (lq, tk), 1)
                s = jnp.where(cols <= rows, s, NEG_INF)
            m_old = m_ref[h]
            m_new = jnp.maximum(m_old, jnp.max(s, axis=1, keepdims=True))
            alpha = jnp.exp(m_old - m_new)
            p = jnp.exp(s - _widen(m_new, tk))
            l_ref[h] = alpha * l_ref[h] + jnp.sum(p, axis=1, keepdims=True)
            acc_ref[h] = alpha * acc_ref[h] + _dot(p.astype(BF16), v_head(h).astype(BF16))
            m_ref[h] = m_new

    @pl.when(j < ncb)
    def _():
        k_hm = pltpu.einshape("mhd->hmd", kc_ref[0])
        v_hm = pltpu.einshape("mhd->hmd", vc_ref[0])
        attend(lambda h: k_hm[h], lambda h: v_hm[h], lambda h: ckc_ref[h], False)

    @pl.when(j == ncb)
    def _():
        attend(lambda h: kn_ref[:, h * hd:(h + 1) * hd], lambda h: vn_ref[:, h * hd:(h + 1) * hd],
               lambda h: ckn_ref[h][:, 0:lq], True)
        for h in range(nh):
            o_ref[:, h * hd:(h + 1) * hd] = (acc_ref[h] / l_ref[h]).astype(BF16)


def _fox_decode(proj, cache_k, cache_v, cq, ck_cache, ck_new, *, bsz, lq, tk):
    past = cache_k.shape[1]
    assert past % tk == 0 and tk % LANES == 0 and lq <= LANES
    ncb = past // tk
    nh, hd, w = FOX_HEADS, FOX_HEAD_DIM, FOX_WIDTH
    tile = lambda j: jnp.minimum(j, ncb - 1)
    cache_spec = pl.BlockSpec((1, tk, nh, hd), lambda b, j: (b, tile(j), 0, 0))
    return pl.pallas_call(
        functools.partial(_fox_decode_kernel, lq=lq, ncb=ncb),
        grid=(bsz, ncb + 1),
        in_specs=[pl.BlockSpec((lq, w), lambda b, j: (b, 0)),
                  pl.BlockSpec((lq, w), lambda b, j: (b, 1)),
                  pl.BlockSpec((lq, w), lambda b, j: (b, 2)),
                  cache_spec,
                  cache_spec,
                  pl.BlockSpec((nh, lq, LANES), lambda b, j: (b, 0, 0)),
                  pl.BlockSpec((nh, 1, tk), lambda b, j: (b, 0, tile(j))),
                  pl.BlockSpec((nh, 1, LANES), lambda b, j: (b, 0, 0))],
        out_specs=pl.BlockSpec((lq, w), lambda b, j: (b, 0)),
        out_shape=jax.ShapeDtypeStruct((bsz * lq, w), BF16),
        scratch_shapes=[pltpu.VMEM((nh, lq, LANES), F32), pltpu.VMEM((nh, lq, LANES), F32),
                        pltpu.VMEM((nh, lq, hd), F32)],
        compiler_params=_cparams(2),
        name="fox_decode",
    )(proj, proj, proj, cache_k, cache_v, cq, ck_cache, ck_new)


def _sconv_kernel(u_ref, bg_ref, cg_ref, cw_ref, hist_ref, y_ref, tail_ref, carry_ref, ext_ref,
                  *, tm, rows, spt, tpb):
    i = pl.program_id(0)
    w = cg_ref[...] * u_ref[...]
    first = (i % tpb) == 0
    for s in range(spt):
        w_s = w[s * rows:(s + 1) * rows]
        if tpb == 1:
            prev = hist_ref[s]
        else:
            prev = jnp.where(first, hist_ref[s], carry_ref[...])
        conv = _conv_rows(ext_ref, w_s, prev, cw_ref, SC_WIDTH, rows)
        y_ref[s * rows:(s + 1) * rows, :] = (bg_ref[s * rows:(s + 1) * rows, :] * conv).astype(BF16)
        tail_ref[s] = w_s[rows - SUBLANES:rows]
    if tpb > 1:
        carry_ref[...] = w[tm - SUBLANES:tm]


def _sconv(proj, conv_w, hist8, *, seq_len, tm):
    m = proj.shape[0]
    assert m % tm == 0
    rows, spt, tpb = _seq_tiling(seq_len, tm)
    nm = m // tm
    hist_map = (lambda i: (i // tpb, 0, 0)) if spt == 1 else (lambda i: (i, 0, 0))
    base = 3 * FOX_WIDTH // SC_DIM
    return pl.pallas_call(
        functools.partial(_sconv_kernel, tm=tm, rows=rows, spt=spt, tpb=tpb),
        grid=(nm,),
        in_specs=[pl.BlockSpec((tm, SC_DIM), lambda i: (i, base)),
                  pl.BlockSpec((tm, SC_DIM), lambda i: (i, base + 1)),
                  pl.BlockSpec((tm, SC_DIM), lambda i: (i, base + 2)),
                  pl.BlockSpec((SC_WIDTH, SC_DIM), lambda i: (0, 0)),
                  pl.BlockSpec((spt, SUBLANES, SC_DIM), hist_map)],
        out_specs=[pl.BlockSpec((tm, SC_DIM), lambda i: (i, 0)),
                   pl.BlockSpec((spt, SUBLANES, SC_DIM), lambda i: (i, 0, 0))],
        out_shape=[jax.ShapeDtypeStruct((m, SC_DIM), BF16),
                   jax.ShapeDtypeStruct((nm * spt, SUBLANES, SC_DIM), F32)],
        scratch_shapes=[pltpu.VMEM((SUBLANES, SC_DIM), F32),
                        pltpu.VMEM((rows + SUBLANES, SC_DIM), F32)],
        compiler_params=_cparams(1),
        name="gated_short_conv",
    )(proj, proj, proj, conv_w, hist8)


def _hist8(state):
    n, w1, c = state.shape
    return jnp.concatenate([jnp.zeros((n, SUBLANES - w1, c), F32), state.astype(F32)], axis=1)


def _tails(tails, n_seq, seq_len, tile_rows, keep):
    per_seq = max(1, seq_len // tile_rows)
    idx = (jnp.arange(n_seq) + 1) * per_seq - 1
    return tails[idx][:, SUBLANES - keep:, :]


def _rope_tables(pos0, length):
    half = RET_DK // 2
    inv = ROPE_BASE ** (-np.arange(half, dtype=np.float64) / half)
    ang = (pos0 + np.arange(length, dtype=np.float64))[:, None] * inv[None, :]
    cos, sin = np.cos(ang), np.sin(ang)
    return (jnp.asarray(np.concatenate([cos, cos], axis=1), F32),
            jnp.asarray(np.concatenate([-sin, sin], axis=1), F32))


def _prep_weights(p):
    d = D_MODEL
    ab_in = p['ab_w_in'][0]
    cd_in = p['cd_w_in'][0]
    f0 = 3 * FOX_WIDTH
    pad_cols = lambda w: jnp.pad(w, ((0, 0), (0, LANES - w.shape[1]))).astype(BF16)
    return dict(
        ab_in=ab_in.astype(BF16),
        ab_small=pad_cols(ab_in[:, AB_MAIN:]),
        cd_in=jnp.concatenate([cd_in[:, :f0], cd_in[:, f0 + FOX_HEADS:]], axis=1).astype(BF16),
        cd_small=pad_cols(cd_in[:, f0:f0 + FOX_HEADS]),
        ab_out=p['ab_w_out'][0].astype(BF16),
        cd_out=p['cd_w_out'][0].astype(BF16),
        ffn_gate=p['ffn_w_gate'].astype(BF16),
        ffn_up=p['ffn_w_up'].astype(BF16),
        ffn_down=p['ffn_w_down'].astype(BF16),
    )


def _trunk(x, pos0, st_ret, st_ssd, st_ssd_conv, c_k, c_v, c_logf, st_sconv, st_ffn, p, wb, t):
    bsz, length, d = x.shape
    m = bsz * length
    xf = x.reshape(m, d)
    zeros = lambda *shape: jnp.zeros(shape, F32)

    proj, dt_proj = _norm_matmul(xf, p['ab_norm_w'][0], wb['ab_in'], wb['ab_small'],
                                 n=AB_MAIN, tm=t['tm_proj'], tn=t['tn_ab'])
    cosf, sinf = _rope_tables(pos0, length)
    ret_state = zeros(bsz, RET_HEADS, RET_DK, RET_DV) if st_ret is None else st_ret
    y_ret, ret_new = _retention(proj, cosf, sinf, ret_state, p['ret_norm_w'][0],
                                bsz=bsz, seq_len=length, c=t['c_ret'])
    ssd_state = zeros(bsz, SSD_HEADS, SSD_HEADDIM, SSD_DSTATE) if st_ssd is None else st_ssd
    ssd_hist = zeros(bsz, SSD_CONV - 1, SSD_CONV_DIM) if st_ssd_conv is None else st_ssd_conv
    y_ssd, ssd_new = _ssd(proj, dt_proj, _hist8(ssd_hist), ssd_state, p['ssd_conv_w'][0], p['ssd_conv_b'][0],
                          p['ssd_dt_bias'][0], p['ssd_A_log'][0], p['ssd_D'][0], p['ssd_norm_w'][0],
                          bsz=bsz, seq_len=length, c=t['c_ssd'])
    xbc_lo = AB_MAIN - SSD_CONV_DIM
    ssd_conv_new = proj.reshape(bsz, length, -1)[:, length - (SSD_CONV - 1):, xbc_lo:AB_MAIN]
    xf = _proj_residual(xf, y_ret, y_ssd, wb['ab_out'], tm=t['tm_out'], tn=t['tn_out'])

    ffn_new = []
    ffn_hist0 = zeros(bsz, FFN_CONV - 1, D_FF) if st_ffn is None else st_ffn[0]
    xf, tails = _conv_ffn(xf, p['ffn_norm_w'][0], wb['ffn_gate'], wb['ffn_up'], wb['ffn_down'],
                          p['ffn_conv_w'][0], p['ffn_conv_b'][0], _hist8(ffn_hist0), p['final_norm_w'],
                          layer=0, seq_len=length, tm=t['tm_ffn'], tf=t['tf_ffn'], ts=t['ts_ffn'], final=False)
    ffn_new.append(_tails(tails, bsz, length, t['tm_ffn'], FFN_CONV - 1))

    proj, fl_proj = _norm_matmul(xf, p['cd_norm_w'][0], wb['cd_in'], wb['cd_small'],
                                 n=CD_MAIN, tm=t['tm_proj'], tn=t['tn_cd'])
    f_bias = jnp.pad(p['fox_f_bias'][0].astype(F32), (0, LANES - FOX_HEADS)).reshape(1, LANES)
    head_shape = (bsz, length, FOX_HEADS, FOX_HEAD_DIM)
    if c_k is None:
        qa, ka, vb, k32, v32, logf = _fox_prep(proj, fl_proj, f_bias, bsz=bsz, seq_len=length, tp=t['t_prep'])
        y_fox = _fox_prompt(qa, ka, vb, bsz=bsz, seq_len=length, t=t['t_fox'], ts=t['ts_fox'])
        logf_new = logf.reshape(bsz, length, LANES)[:, :, :FOX_HEADS]
        k_new, v_new = k32.reshape(head_shape), v32.reshape(head_shape)
    else:
        proj3 = proj.reshape(bsz, length, -1)
        k_new = proj3[:, :, FOX_WIDTH:2 * FOX_WIDTH].reshape(head_shape)
        v_new = proj3[:, :, 2 * FOX_WIDTH:3 * FOX_WIDTH].reshape(head_shape)
        past = c_k.shape[1]
        pairs = bsz * FOX_HEADS
        assert pairs <= LANES and length <= LANES
        to_lanes = lambda a, rows: jnp.pad(jnp.swapaxes(a, 0, 1).reshape(a.shape[1], pairs),
                                           ((0, rows - a.shape[1]), (0, LANES - pairs)))
        from_lanes = lambda a: jnp.swapaxes(a[:length, :pairs].reshape(length, bsz, FOX_HEADS), 0, 1)
        cache_lf = to_lanes(c_logf.astype(F32), past)
        fl_rows = to_lanes(fl_proj.reshape(bsz, length, LANES)[:, :, :FOX_HEADS], LANES)
        bias_lanes = jnp.pad(jnp.tile(p['fox_f_bias'][0].astype(F32), bsz), (0, LANES - pairs)).reshape(1, LANES)
        cum_t_cache, lf_rows, cum_rows, cum_t_new = _decode_cum(cache_lf, fl_rows, bias_lanes, c=t['c_cum'])
        logf_new = from_lanes(lf_rows)
        cq = jnp.broadcast_to(cum_rows[:length, :pairs].T[:, :, None], (pairs, length, LANES))
        y_fox = _fox_decode(proj, c_k, c_v, cq, cum_t_cache[:pairs, None, :], cum_t_new[:pairs, None, :],
                            bsz=bsz, lq=length, tk=t['tk_dec'])
    sc_hist = zeros(bsz, SC_WIDTH - 1, SC_DIM) if st_sconv is None else st_sconv
    y_sc, sc_tails = _sconv(proj, p['sconv_w'][0], _hist8(sc_hist), seq_len=length, tm=t['tm_sc'])
    sconv_new = _tails(sc_tails, bsz, length, t['tm_sc'], SC_WIDTH - 1)
    xf = _proj_residual(xf, y_fox, y_sc, wb['cd_out'], tm=t['tm_out'], tn=t['tn_out'])

    ffn_hist1 = zeros(bsz, FFN_CONV - 1, D_FF) if st_ffn is None else st_ffn[1]
    xf, tails = _conv_ffn(xf, p['ffn_norm_w'][1], wb['ffn_gate'], wb['ffn_up'], wb['ffn_down'],
                          p['ffn_conv_w'][1], p['ffn_conv_b'][1], _hist8(ffn_hist1), p['final_norm_w'],
                          layer=1, seq_len=length, tm=t['tm_ffn'], tf=t['tf_ffn'], ts=t['ts_ffn'], final=True)
    ffn_new.append(_tails(tails, bsz, length, t['tm_ffn'], FFN_CONV - 1))

    return (xf.reshape(bsz, length, d), ret_new[None], ssd_new[None], ssd_conv_new[None], k_new[None],
            v_new[None], logf_new[None], sconv_new[None], jnp.stack(ffn_new))


def _largest_divisor(n, cap, multiple=1):
    best = None
    for cand in range(multiple, min(n, cap) + 1, multiple):
        if n % cand == 0:
            best = cand
    assert best is not None, (n, cap, multiple)
    return best


def _tiles(bsz, length, past=None):
    m = bsz * length
    seq_tile = lambda cap: _largest_divisor(length, cap, SUBLANES)
    row_tile = lambda cap: (_largest_divisor(length, cap, SUBLANES) if length >= cap
                            else _largest_divisor(m, cap, length))
    t = dict(
        tm_proj=row_tile(512), tn_ab=2816, tn_cd=3072,
        tm_out=row_tile(1024), tn_out=1024,
        tm_ffn=row_tile(1024), tf_ffn=512, ts_ffn=512,
        tm_sc=row_tile(512),
        c_ret=seq_tile(256), c_ssd=seq_tile(256),
    )
    if past is None:
        t['t_fox'] = seq_tile(2048)
        t['ts_fox'] = _largest_divisor(t['t_fox'], 256, LANES)
        t['t_prep'] = seq_tile(512)
    else:
        t['tk_dec'] = _largest_divisor(past, 1024, LANES)
        t['c_cum'] = _largest_divisor(past, 256, LANES)
    return t


def kernel(x_prompt, x_sample, state_ret, state_ssd, state_ssd_conv, cache_fox_k, cache_fox_v, cache_fox_logf, state_sconv, state_ffn_conv, ab_norm_w, ab_w_in, ret_norm_w, ssd_conv_w, ssd_conv_b, ssd_dt_bias, ssd_A_log, ssd_D, ssd_norm_w, ab_w_out, cd_norm_w, cd_w_in, fox_f_bias, sconv_w, cd_w_out, ffn_norm_w, ffn_w_gate, ffn_w_up, ffn_conv_w, ffn_conv_b, ffn_w_down, final_norm_w):
    p = dict(ab_norm_w=ab_norm_w, ab_w_in=ab_w_in, ret_norm_w=ret_norm_w, ssd_conv_w=ssd_conv_w,
             ssd_conv_b=ssd_conv_b, ssd_dt_bias=ssd_dt_bias, ssd_A_log=ssd_A_log, ssd_D=ssd_D,
             ssd_norm_w=ssd_norm_w, ab_w_out=ab_w_out, cd_norm_w=cd_norm_w, cd_w_in=cd_w_in,
             fox_f_bias=fox_f_bias, sconv_w=sconv_w, cd_w_out=cd_w_out, ffn_norm_w=ffn_norm_w,
             ffn_w_gate=ffn_w_gate, ffn_w_up=ffn_w_up, ffn_conv_w=ffn_conv_w, ffn_conv_b=ffn_conv_b,
             ffn_w_down=ffn_w_down, final_norm_w=final_norm_w)
    assert x_prompt.shape[-1] == D_MODEL and ab_w_in.shape == (1, D_MODEL, AB_MAIN + SSD_HEADS)
    assert cd_w_in.shape == (1, D_MODEL, CD_MAIN + FOX_HEADS) and ffn_w_gate.shape == (2, D_MODEL, D_FF)
    wb = _prep_weights(p)
    bp, lp_, _ = x_prompt.shape
    bs, ls, _ = x_sample.shape
    past = cache_fox_k.shape[2]
    (y_prompt, p_ret, p_ssd, p_ssd_conv, p_fox_k, p_fox_v, p_fox_logf, p_sconv, p_ffn_conv) = _trunk(
        x_prompt, 0, None, None, None, None, None, None, None, None, p, wb, _tiles(bp, lp_))
    (y_sample, s_ret, s_ssd, s_ssd_conv, s_fox_k, s_fox_v, s_fox_logf, s_sconv, s_ffn_conv) = _trunk(
        x_sample, past, state_ret[0], state_ssd[0], state_ssd_conv[0], cache_fox_k[0], cache_fox_v[0],
        cache_fox_logf[0], state_sconv[0], state_ffn_conv, p, wb, _tiles(bs, ls, past))
    return (y_prompt, y_sample, p_ret, s_ret, p_ssd, s_ssd, p_ssd_conv, s_ssd_conv, p_fox_k, s_fox_k,
            p_fox_v, s_fox_v, p_fox_logf, s_fox_logf, p_sconv, s_sconv, p_ffn_conv, s_ffn_conv)
```

```python
import functools
import math

import numpy as np
import jax
import jax.numpy as jnp
from jax import lax
from jax.experimental import pallas as pl
from jax.experimental.pallas import tpu as pltpu

F32 = jnp.float32
BF16 = jnp.bfloat16
EPS = 1e-6
ROPE_BASE = 10000.0
NEG_INF = float("-inf")

D_MODEL = 2048
RET_HEADS, RET_DK, RET_DV = 4, 128, 256
SSD_DINNER, SSD_HEADDIM, SSD_HEADS, SSD_GROUPS, SSD_DSTATE, SSD_CONV = 1024, 64, 16, 2, 128, 4
SSD_CONV_DIM = SSD_DINNER + 2 * SSD_GROUPS * SSD_DSTATE
FOX_HEADS, FOX_HEAD_DIM = 8, 128
FOX_WIDTH = FOX_HEADS * FOX_HEAD_DIM
SC_DIM, SC_WIDTH = 1024, 3
D_FF, FFN_CONV = 5632, 3
AB_MAIN = 2 * RET_HEADS * RET_DK + 2 * RET_HEADS * RET_DV + SSD_DINNER + SSD_CONV_DIM
AB_PAD = AB_MAIN + 128
CD_MAIN = 3 * FOX_WIDTH + 3 * SC_DIM
CD_PAD = CD_MAIN + 128

LANES = 128
SUBLANES = 8
VMEM_LIMIT = 60 * 1024 * 1024


def _cparams(n_axes):
    return pltpu.CompilerParams(dimension_semantics=("arbitrary",) * n_axes,
                                vmem_limit_bytes=VMEM_LIMIT)


def _rms(xf, w):
    return xf * lax.rsqrt(jnp.mean(xf * xf, axis=-1, keepdims=True) + EPS) * w


def _softplus(x):
    return jnp.maximum(x, 0.0) + jnp.log1p(jnp.exp(-jnp.abs(x)))


def _split3(x):
    hi = x.astype(BF16)
    r1 = x - hi.astype(F32)
    mid = r1.astype(BF16)
    lo = (r1 - mid.astype(F32)).astype(BF16)
    return hi, mid, lo


def _widen(x, n):
    return x[:, 0:n] if n <= LANES else jnp.concatenate([x] * (n // LANES), axis=1)


def _dot(a, b):
    return jnp.dot(a, b, preferred_element_type=F32)


def _dot_nt(a, b):
    return lax.dot_general(a, b, (((1,), (1,)), ((), ())), preferred_element_type=F32)


def _dot_tn(a, b):
    return lax.dot_general(a, b, (((0,), (0,)), ((), ())), preferred_element_type=F32)


def _exact_lhs_dot(m_bf16, x):
    hi, mid, lo = _split3(x)
    return _dot(m_bf16, hi) + _dot(m_bf16, mid) + _dot(m_bf16, lo)


def _exact_rhs_dot(x, m_bf16):
    hi, mid, lo = _split3(x)
    return _dot(hi, m_bf16) + _dot(mid, m_bf16) + _dot(lo, m_bf16)


def _conv_rows(ext_ref, x, prev8, w_ref, width, rows, w_cols=slice(None)):
    ext_ref[0:SUBLANES, :] = prev8
    ext_ref[SUBLANES:SUBLANES + rows, :] = x
    out = None
    for j in range(width):
        off = SUBLANES - (width - 1) + j
        term = ext_ref[off:off + rows, :] * w_ref[j:j + 1, w_cols]
        out = term if out is None else out + term
    return out


def _seq_tiling(seq_len, tile_rows):
    if seq_len >= tile_rows:
        assert seq_len % tile_rows == 0
        return tile_rows, 1, seq_len // tile_rows
    assert tile_rows % seq_len == 0 and seq_len % SUBLANES == 0
    return seq_len, tile_rows // seq_len, 1


def _norm_matmul_kernel(x_ref, nw_ref, w_ref, ws_ref, o_ref, os_ref, h_ref):
    @pl.when(pl.program_id(1) == 0)
    def _():
        h_ref[...] = _rms(x_ref[...], nw_ref[...]).astype(BF16)
        os_ref[...] = _dot(h_ref[...], ws_ref[...])

    o_ref[...] = _dot(h_ref[...], w_ref[...])


def _norm_matmul(x, norm_w, w, w_small, *, n, tm, tn):
    m, d = x.shape
    assert m % tm == 0 and n % tn == 0 and n <= w.shape[1] and w_small.shape == (d, LANES)
    return pl.pallas_call(
        _norm_matmul_kernel,
        grid=(m // tm, n // tn),
        in_specs=[pl.BlockSpec((tm, d), lambda i, j: (i, 0)),
                  pl.BlockSpec((1, d), lambda i, j: (0, 0)),
                  pl.BlockSpec((d, tn), lambda i, j: (0, j)),
                  pl.BlockSpec((d, LANES), lambda i, j: (0, 0))],
        out_specs=[pl.BlockSpec((tm, tn), lambda i, j: (i, j)),
                   pl.BlockSpec((tm, LANES), lambda i, j: (i, 0))],
        out_shape=[jax.ShapeDtypeStruct((m, n), F32),
                   jax.ShapeDtypeStruct((m, LANES), F32)],
        scratch_shapes=[pltpu.VMEM((tm, d), BF16)],
        compiler_params=_cparams(2),
        name="norm_in_proj",
    )(x, norm_w.reshape(1, d), w, w_small)


def _proj_res_kernel(x_ref, a_ref, b_ref, wa_ref, wb_ref, o_ref):
    acc = _dot(a_ref[...], wa_ref[...])
    acc = acc + _dot(b_ref[...], wb_ref[...])
    o_ref[...] = x_ref[...] + acc


def _proj_residual(x, a, b, w, *, tm, tn):
    m, d = x.shape
    ka, kb = a.shape[1], b.shape[1]
    assert m % tm == 0 and d % tn == 0 and ka == kb and w.shape == (ka + kb, d)
    return pl.pallas_call(
        _proj_res_kernel,
        grid=(m // tm, d // tn),
        in_specs=[pl.BlockSpec((tm, tn), lambda i, j: (i, j)),
                  pl.BlockSpec((tm, ka), lambda i, j: (i, 0)),
                  pl.BlockSpec((tm, kb), lambda i, j: (i, 0)),
                  pl.BlockSpec((ka, tn), lambda i, j: (0, j)),
                  pl.BlockSpec((kb, tn), lambda i, j: (1, j))],
        out_specs=pl.BlockSpec((tm, tn), lambda i, j: (i, j)),
        out_shape=jax.ShapeDtypeStruct((m, d), F32),
        compiler_params=_cparams(2),
        name="out_proj_residual",
    )(x, a, b, w, w)


def _ffn_kernel(x_ref, nw_ref, wg_ref, wu_ref, wd_ref, cw_ref, cb_ref, hist_ref, fw_ref,
                o_ref, tail_ref, h_ref, carry_ref, ext_ref,
                *, tm, rows, spt, tpb, nf, ts, nsub, nsub_last, final):
    i = pl.program_id(0)
    f = pl.program_id(1)

    @pl.when(f == 0)
    def _():
        xf = x_ref[...]
        h_ref[...] = _rms(xf, nw_ref[...]).astype(BF16)
        o_ref[...] = xf

    first = (i % tpb) == 0

    def sub_block(sb):
        cols = slice(sb * ts, (sb + 1) * ts)
        h = h_ref[...]
        a = _dot(h, wg_ref[:, cols])
        u = _dot(h, wu_ref[:, cols])
        convs = []
        for s in range(spt):
            a_s = a[s * rows:(s + 1) * rows]
            if tpb == 1:
                prev = hist_ref[s, :, cols]
            else:
                prev = jnp.where(first, hist_ref[s, :, cols], carry_ref[f * nsub + sb])
            convs.append(_conv_rows(ext_ref, a_s, prev, cw_ref, FFN_CONV, rows, cols))
            tail_ref[s, :, cols] = a_s[rows - SUBLANES:rows]
        if tpb > 1:
            carry_ref[f * nsub + sb] = a[tm - SUBLANES:tm]
        conv = convs[0] if spt == 1 else jnp.concatenate(convs, axis=0)
        act = (jax.nn.silu(conv + cb_ref[:, cols]) * u).astype(BF16)
        o_ref[...] += _dot(act, wd_ref[cols, :])

    if nsub_last == nsub:
        for sb in range(nsub):
            sub_block(sb)
    else:
        @pl.when(f < nf - 1)
        def _():
            for sb in range(nsub):
                sub_block(sb)

        @pl.when(f == nf - 1)
        def _():
            for sb in range(nsub_last):
                sub_block(sb)

    if final:
        @pl.when(f == nf - 1)
        def _():
            o_ref[...] = _rms(o_ref[...], fw_ref[...])


def _conv_ffn(x, norm_w, wg, wu, wd, conv_w, conv_b, hist8, final_w, *, layer, seq_len, tm, tf, ts, final):
    m, d = x.shape
    ff = wg.shape[2]
    assert m % tm == 0 and tf % ts == 0 and ff % ts == 0
    rows, spt, tpb = _seq_tiling(seq_len, tm)
    nm, nf = m // tm, pl.cdiv(ff, tf)
    nsub = tf // ts
    nsub_last = (ff - (nf - 1) * tf) // ts
    hist_map = (lambda i, f: (i // tpb, 0, f)) if spt == 1 else (lambda i, f: (i, 0, f))
    kern = functools.partial(_ffn_kernel, tm=tm, rows=rows, spt=spt, tpb=tpb, nf=nf, ts=ts, nsub=nsub,
                             nsub_last=nsub_last, final=final)
    out, tails = pl.pallas_call(
        kern,
        grid=(nm, nf),
        in_specs=[pl.BlockSpec((tm, d), lambda i, f: (i, 0)),
                  pl.BlockSpec((1, d), lambda i, f: (0, 0)),
                  pl.BlockSpec((None, d, tf), lambda i, f: (layer, 0, f)),
                  pl.BlockSpec((None, d, tf), lambda i, f: (layer, 0, f)),
                  pl.BlockSpec((None, tf, d), lambda i, f: (layer, f, 0)),
                  pl.BlockSpec((FFN_CONV, tf), lambda i, f: (0, f)),
                  pl.BlockSpec((1, tf), lambda i, f: (0, f)),
                  pl.BlockSpec((spt, SUBLANES, tf), hist_map),
                  pl.BlockSpec((1, d), lambda i, f: (0, 0))],
        out_specs=[pl.BlockSpec((tm, d), lambda i, f: (i, 0)),
                   pl.BlockSpec((spt, SUBLANES, tf), lambda i, f: (i, 0, f))],
        out_shape=[jax.ShapeDtypeStruct((m, d), F32),
                   jax.ShapeDtypeStruct((nm * spt, SUBLANES, ff), F32)],
        scratch_shapes=[pltpu.VMEM((tm, d), BF16),
                        pltpu.VMEM((nf * nsub, SUBLANES, ts), F32),
                        pltpu.VMEM((rows + SUBLANES, ts), F32)],
        compiler_params=_cparams(2),
        name="conv_ffn",
    )(x, norm_w.reshape(1, d), wg, wu, wd, conv_w, conv_b.reshape(1, ff), hist8, final_w.reshape(1, d))
    return out, tails


def _retention_kernel(q_ref, k_ref, v_ref, g_ref, cos_ref, sin_ref, st_ref, nw_ref,
                      y_ref, so_ref, *, c):
    ci = pl.program_id(1)

    @pl.when(ci == 0)
    def _():
        so_ref[...] = st_ref[...]

    cos = cos_ref[...]
    sin = sin_ref[...]
    ii = lax.broadcasted_iota(jnp.int32, (c, c), 0)
    jj = lax.broadcasted_iota(jnp.int32, (c, c), 1)
    diff = (ii - jj).astype(F32)
    causal = ii >= jj
    ridx = lax.broadcasted_iota(jnp.int32, (c, 1), 0).astype(F32)
    for h in range(RET_HEADS):
        lg = math.log1p(-(2.0 ** (-5.0 - h)))
        q = q_ref[:, h * RET_DK:(h + 1) * RET_DK]
        k = k_ref[:, h * RET_DK:(h + 1) * RET_DK]
        v = v_ref[:, h * RET_DV:(h + 1) * RET_DV]
        qr = q * cos + pltpu.roll(q, RET_DK // 2, 1) * sin
        kr = (k * cos + pltpu.roll(k, RET_DK // 2, 1) * sin) * (RET_DK ** -0.5)
        qb = qr.astype(BF16)
        kb = kr.astype(BF16)
        vb = v.astype(BF16)
        decay = jnp.exp(jnp.where(causal, diff * lg, NEG_INF))
        inner = jnp.exp((ridx + 1.0) * lg)
        sdecay = jnp.exp((c - 1.0 - ridx) * lg)
        s = so_ref[0, h]
        scores = _dot_nt(qb, kb) * decay
        y = _dot(scores.astype(BF16), vb)
        y = y + _dot(qb, s.astype(BF16)) * inner
        kd = (kr * sdecay).astype(BF16)
        so_ref[0, h] = math.exp(c * lg) * s + _dot_tn(kd, vb)
        mu = jnp.mean(y, axis=-1, keepdims=True)
        yc = y - mu
        var = jnp.mean(yc * yc, axis=-1, keepdims=True)
        yn = yc * lax.rsqrt(var + EPS) * nw_ref[:, h * RET_DV:(h + 1) * RET_DV]
        g = g_ref[:, h * RET_DV:(h + 1) * RET_DV]
        y_ref[:, h * RET_DV:(h + 1) * RET_DV] = (jax.nn.silu(g) * yn).astype(BF16)


def _retention(proj, cosf, sinf, state, norm_w, *, bsz, seq_len, c):
    m = proj.shape[0]
    nc = seq_len // c
    assert seq_len % c == 0
    qk_w = RET_HEADS * RET_DK
    v_w = RET_HEADS * RET_DV
    row = lambda b, ci: b * nc + ci
    y, s_new = pl.pallas_call(
        functools.partial(_retention_kernel, c=c),
        grid=(bsz, nc),
        in_specs=[pl.BlockSpec((c, qk_w), lambda b, ci: (row(b, ci), 0)),
                  pl.BlockSpec((c, qk_w), lambda b, ci: (row(b, ci), 1)),
                  pl.BlockSpec((c, v_w), lambda b, ci: (row(b, ci), 1)),
                  pl.BlockSpec((c, v_w), lambda b, ci: (row(b, ci), 2)),
                  pl.BlockSpec((c, RET_DK), lambda b, ci: (ci, 0)),
                  pl.BlockSpec((c, RET_DK), lambda b, ci: (ci, 0)),
                  pl.BlockSpec((1, RET_HEADS, RET_DK, RET_DV), lambda b, ci: (b, 0, 0, 0)),
                  pl.BlockSpec((1, v_w), lambda b, ci: (0, 0))],
        out_specs=[pl.BlockSpec((c, v_w), lambda b, ci: (row(b, ci), 0)),
                   pl.BlockSpec((1, RET_HEADS, RET_DK, RET_DV), lambda b, ci: (b, 0, 0, 0))],
        out_shape=[jax.ShapeDtypeStruct((m, v_w), BF16),
                   jax.ShapeDtypeStruct(state.shape, F32)],
        compiler_params=_cparams(2),
        name="retention",
    )(proj, proj, proj, proj, cosf, sinf, state, norm_w.reshape(1, v_w))
    return y, s_new


def _ssd_kernel(z_ref, xs_ref, bc_ref, dt_ref, hx_ref, hbc_ref, st_ref,
                cwx_ref, cwbc_ref, cbx_ref, cbbc_ref, dtb_ref, alog_ref, dsk_ref, nw_ref,
                tri_ref, exp_ref,
                y_ref, so_ref,
                st_scr, cx_scr, cbc_scr, extx_scr, extbc_scr, yh_scr, xs_scr, *, c, nc):
    ci = pl.program_id(1)
    gw = SSD_DINNER // SSD_GROUPS
    hpg = SSD_HEADS // SSD_GROUPS

    @pl.when(ci == 0)
    def _():
        st_scr[...] = st_ref[0].T
        cx_scr[...] = hx_ref[0]
        cbc_scr[...] = hbc_ref[0]

    xs_raw = xs_ref[...]
    bc_raw = bc_ref[...]
    xs_scr[...] = jax.nn.silu(_conv_rows(extx_scr, xs_raw, cx_scr[...], cwx_ref, SSD_CONV, c) + cbx_ref[...])
    bcm = jax.nn.silu(_conv_rows(extbc_scr, bc_raw, cbc_scr[...], cwbc_ref, SSD_CONV, c) + cbbc_ref[...])
    cx_scr[...] = xs_raw[c - SUBLANES:c]
    cbc_scr[...] = bc_raw[c - SUBLANES:c]

    tri = tri_ref[...]
    dt = _softplus(dt_ref[...] + dtb_ref[...])
    a = -jnp.exp(alog_ref[...])
    acs = _exact_lhs_dot(tri, dt * a)
    acs_t = acs.T
    acs_last = acs[c - 1:c, :]
    exp_acs = jnp.exp(acs)
    to_end = jnp.exp(acs_last - acs)
    chunk_dec = jnp.exp(acs_last)

    ii = lax.broadcasted_iota(jnp.int32, (c, c), 0)
    jj = lax.broadcasted_iota(jnp.int32, (c, c), 1)
    causal = ii >= jj
    nb = SSD_GROUPS * SSD_DSTATE
    for g in range(SSD_GROUPS):
        cols = slice(g * gw, (g + 1) * gw)
        expand = exp_ref[:, cols]
        xdt = xs_scr[:, cols] * _exact_rhs_dot(dt, expand)
        xdt_b = xdt.astype(BF16)
        xend_b = (xdt * _exact_rhs_dot(to_end, expand)).astype(BF16)
        b_g = bcm[:, g * SSD_DSTATE:(g + 1) * SSD_DSTATE].astype(BF16)
        c_g = bcm[:, nb + g * SSD_DSTATE:nb + (g + 1) * SSD_DSTATE].astype(BF16)
        cb = _dot_nt(c_g, b_g)
        s_g = st_scr[:, cols]
        y_state = _dot(c_g, s_g.astype(BF16)) * _exact_rhs_dot(exp_acs, expand)
        for r in range(hpg):
            hh = g * hpg + r
            seg = acs[:, hh:hh + 1] - acs_t[hh:hh + 1, :]
            lmat = jnp.exp(jnp.where(causal, seg, NEG_INF))
            mm = (cb * lmat).astype(BF16)
            head = slice(r * SSD_HEADDIM, (r + 1) * SSD_HEADDIM)
            yh_scr[:, hh * SSD_HEADDIM:(hh + 1) * SSD_HEADDIM] = _dot(mm, xdt_b[:, head]) + y_state[:, head]
        upd = _dot_tn(b_g, xend_b)
        st_scr[:, cols] = _exact_rhs_dot(chunk_dec, expand) * s_g + upd

    y = yh_scr[...] + dsk_ref[...] * xs_scr[...]
    z = z_ref[...]
    y_ref[...] = _rms(y * jax.nn.silu(z), nw_ref[...]).astype(BF16)

    @pl.when(ci == nc - 1)
    def _():
        so_ref[0] = st_scr[...].T


def _ssd(proj, dt_proj, hist8, state, conv_w, conv_b, dt_bias, a_log, d_skip, norm_w, *, bsz, seq_len, c):
    m = proj.shape[0]
    nc = seq_len // c
    assert seq_len % c == 0
    row = lambda b, ci: b * nc + ci
    const2 = lambda b, ci: (0, 0)
    di, bcw = SSD_DINNER, 2 * SSD_GROUPS * SSD_DSTATE
    tri = jnp.asarray(np.tril(np.ones((c, c), np.float32)), BF16)
    expand = np.zeros((LANES, di), np.float32)
    for h in range(SSD_HEADS):
        expand[h, h * SSD_HEADDIM:(h + 1) * SSD_HEADDIM] = 1.0
    expand = jnp.asarray(expand, BF16)
    pad_row = lambda v: jnp.pad(v.astype(F32), (0, LANES - v.shape[0])).reshape(1, LANES)
    st2 = state.reshape(bsz, di, SSD_DSTATE)
    y, s_new = pl.pallas_call(
        functools.partial(_ssd_kernel, c=c, nc=nc),
        grid=(bsz, nc),
        in_specs=[pl.BlockSpec((c, di), lambda b, ci: (row(b, ci), 3)),
                  pl.BlockSpec((c, di), lambda b, ci: (row(b, ci), 4)),
                  pl.BlockSpec((c, bcw), lambda b, ci: (row(b, ci), 10)),
                  pl.BlockSpec((c, LANES), lambda b, ci: (row(b, ci), 0)),
                  pl.BlockSpec((1, SUBLANES, di), lambda b, ci: (b, 0, 0)),
                  pl.BlockSpec((1, SUBLANES, bcw), lambda b, ci: (b, 0, 2)),
                  pl.BlockSpec((1, di, SSD_DSTATE), lambda b, ci: (b, 0, 0)),
                  pl.BlockSpec((SSD_CONV, di), const2),
                  pl.BlockSpec((SSD_CONV, bcw), lambda b, ci: (0, 2)),
                  pl.BlockSpec((1, di), const2),
                  pl.BlockSpec((1, bcw), lambda b, ci: (0, 2)),
                  pl.BlockSpec((1, LANES), const2),
                  pl.BlockSpec((1, LANES), const2),
                  pl.BlockSpec((1, di), const2),
                  pl.BlockSpec((1, di), const2),
                  pl.BlockSpec((c, c), const2),
                  pl.BlockSpec((LANES, di), const2)],
        out_specs=[pl.BlockSpec((c, di), lambda b, ci: (row(b, ci), 0)),
                   pl.BlockSpec((1, di, SSD_DSTATE), lambda b, ci: (b, 0, 0))],
        out_shape=[jax.ShapeDtypeStruct((m, di), BF16),
                   jax.ShapeDtypeStruct(st2.shape, F32)],
        scratch_shapes=[pltpu.VMEM((SSD_DSTATE, di), F32),
                        pltpu.VMEM((SUBLANES, di), F32),
                        pltpu.VMEM((SUBLANES, bcw), F32),
                        pltpu.VMEM((c + SUBLANES, di), F32),
                        pltpu.VMEM((c + SUBLANES, bcw), F32),
                        pltpu.VMEM((c, di), F32),
                        pltpu.VMEM((c, di), F32)],
        compiler_params=_cparams(2),
        name="ssd",
    )(proj, proj, proj, dt_proj, hist8, hist8, st2,
      conv_w, conv_w, conv_b.reshape(1, -1), conv_b.reshape(1, -1),
      pad_row(dt_bias), pad_row(a_log), jnp.repeat(d_skip.astype(F32), SSD_HEADDIM).reshape(1, di),
      norm_w.reshape(1, di), tri, expand)
    return y, s_new.reshape(state.shape)


def _decode_cum_kernel(lfc_ref, fl_ref, b_ref, tri_ref, cumt_c_ref, lfn_ref, cumn_ref, cumt_n_ref,
                       carry_ref, *, c, ncb):
    j = pl.program_id(0)

    @pl.when(j == 0)
    def _():
        carry_ref[...] = jnp.zeros_like(carry_ref)

    @pl.when(j < ncb)
    def _():
        cum = _exact_lhs_dot(tri_ref[...], lfc_ref[...]) + carry_ref[...]
        carry_ref[...] = cum[c - 1:c, :]
        cumt_c_ref[...] = cum.T

    @pl.when(j == ncb)
    def _():
        lf = -_softplus(-(fl_ref[...] + b_ref[...]))
        lfn_ref[...] = lf
        cum = _exact_lhs_dot(tri_ref[0:LANES, 0:LANES], lf) + carry_ref[...]
        cumn_ref[...] = cum
        cumt_n_ref[...] = cum.T


def _decode_cum(cache_lf, fl_new, bias, *, c):
    past = cache_lf.shape[0]
    assert past % c == 0 and c % LANES == 0 and fl_new.shape == (LANES, LANES)
    ncb = past // c
    tri = jnp.asarray(np.tril(np.ones((c, c), np.float32)), BF16)
    blk = lambda j: jnp.minimum(j, ncb - 1)
    sq = jax.ShapeDtypeStruct((LANES, LANES), F32)
    return pl.pallas_call(
        functools.partial(_decode_cum_kernel, c=c, ncb=ncb),
        grid=(ncb + 1,),
        in_specs=[pl.BlockSpec((c, LANES), lambda j: (blk(j), 0)),
                  pl.BlockSpec((LANES, LANES), lambda j: (0, 0)),
                  pl.BlockSpec((1, LANES), lambda j: (0, 0)),
                  pl.BlockSpec((c, c), lambda j: (0, 0))],
        out_specs=[pl.BlockSpec((LANES, c), lambda j: (0, blk(j))),
                   pl.BlockSpec((LANES, LANES), lambda j: (0, 0)),
                   pl.BlockSpec((LANES, LANES), lambda j: (0, 0)),
                   pl.BlockSpec((LANES, LANES), lambda j: (0, 0))],
        out_shape=[jax.ShapeDtypeStruct((LANES, past), F32), sq, sq, sq],
        scratch_shapes=[pltpu.VMEM((1, LANES), F32)],
        compiler_params=_cparams(1),
        name="decode_logf_cumsum",
    )(cache_lf, fl_new, bias, tri)


FOX_AUG = 2 * FOX_HEAD_DIM
N_BIAS_PIECES = 3


def _fox_prep_kernel(q_ref, k_ref, v_ref, fl_ref, fb_ref, tri_ref, place_ref, ones_ref,
                     qa_ref, ka_ref, vb_ref, k32_ref, v32_ref, lf_ref, carry_ref, *, tp):
    @pl.when(pl.program_id(1) == 0)
    def _():
        carry_ref[...] = jnp.zeros_like(carry_ref)

    lf = -_softplus(-(fl_ref[...] + fb_ref[...]))
    lf_ref[...] = lf
    cum = _exact_lhs_dot(tri_ref[...], lf) + carry_ref[...]
    carry_ref[...] = cum[tp - 1:tp, :]
    pieces = _split3(cum * (FOX_HEAD_DIM ** 0.5))
    n = N_BIAS_PIECES
    aug_q = ones_ref[0:1, :] + sum(_dot(pieces[r], place_ref[r]) for r in range(n))
    aug_k = ones_ref[1:2, :] - sum(_dot(pieces[r], place_ref[n + r]) for r in range(n))
    for h in range(FOX_HEADS):
        src = slice(h * FOX_HEAD_DIM, (h + 1) * FOX_HEAD_DIM)
        feat = slice(h * FOX_AUG, h * FOX_AUG + FOX_HEAD_DIM)
        bias = slice(h * FOX_AUG + FOX_HEAD_DIM, (h + 1) * FOX_AUG)
        qa_ref[:, feat] = q_ref[:, src].astype(BF16)
        qa_ref[:, bias] = aug_q[:, src].astype(BF16)
        ka_ref[:, feat] = k_ref[:, src].astype(BF16)
        ka_ref[:, bias] = aug_k[:, src].astype(BF16)
    k = k_ref[...]
    v = v_ref[...]
    k32_ref[...] = pltpu.einshape("m(hd)->mhd", k, h=FOX_HEADS)
    v32_ref[...] = pltpu.einshape("m(hd)->mhd", v, h=FOX_HEADS)
    vb_ref[...] = v.astype(BF16)


def _fox_prep(proj, fl_proj, f_bias, *, bsz, seq_len, tp):
    m = proj.shape[0]
    nt = seq_len // tp
    assert seq_len % tp == 0
    w = FOX_WIDTH
    tri = jnp.asarray(np.tril(np.ones((tp, tp), np.float32)), BF16)
    n = N_BIAS_PIECES
    place = np.zeros((2 * n, LANES, w), np.float32)
    ones = np.zeros((SUBLANES, w), np.float32)
    for h in range(FOX_HEADS):
        for r in range(2 * n):
            place[r, h, h * FOX_HEAD_DIM + r] = 1.0
        ones[0, h * FOX_HEAD_DIM + n:h * FOX_HEAD_DIM + 2 * n] = 1.0
        ones[1, h * FOX_HEAD_DIM:h * FOX_HEAD_DIM + n] = 1.0
    row = lambda b, ti: (b * nt + ti, 0)
    const2 = lambda b, ti: (0, 0)
    return pl.pallas_call(
        functools.partial(_fox_prep_kernel, tp=tp),
        grid=(bsz, nt),
        in_specs=[pl.BlockSpec((tp, w), lambda b, ti: (b * nt + ti, 0)),
                  pl.BlockSpec((tp, w), lambda b, ti: (b * nt + ti, 1)),
                  pl.BlockSpec((tp, w), lambda b, ti: (b * nt + ti, 2)),
                  pl.BlockSpec((tp, LANES), row),
                  pl.BlockSpec((1, LANES), const2),
                  pl.BlockSpec((tp, tp), const2),
                  pl.BlockSpec((2 * n, LANES, w), lambda b, ti: (0, 0, 0)),
                  pl.BlockSpec((SUBLANES, w), const2)],
        out_specs=[pl.BlockSpec((tp, FOX_HEADS * FOX_AUG), row),
                   pl.BlockSpec((tp, FOX_HEADS * FOX_AUG), row),
                   pl.BlockSpec((tp, w), row),
                   pl.BlockSpec((tp, FOX_HEADS, FOX_HEAD_DIM), lambda b, ti: (b * nt + ti, 0, 0)),
                   pl.BlockSpec((tp, FOX_HEADS, FOX_HEAD_DIM), lambda b, ti: (b * nt + ti, 0, 0)),
                   pl.BlockSpec((tp, LANES), row)],
        out_shape=[jax.ShapeDtypeStruct((m, FOX_HEADS * FOX_AUG), BF16),
                   jax.ShapeDtypeStruct((m, FOX_HEADS * FOX_AUG), BF16),
                   jax.ShapeDtypeStruct((m, w), BF16),
                   jax.ShapeDtypeStruct((m, FOX_HEADS, FOX_HEAD_DIM), F32),
                   jax.ShapeDtypeStruct((m, FOX_HEADS, FOX_HEAD_DIM), F32),
                   jax.ShapeDtypeStruct((m, LANES), F32)],
        scratch_shapes=[pltpu.VMEM((1, LANES), F32)],
        compiler_params=_cparams(2),
        name="fox_prep",
    )(proj, proj, proj, fl_proj, f_bias, tri, jnp.asarray(place, BF16), jnp.asarray(ones, F32))


def _fox_kernel(qi_ref, ki_ref, q_ref, k_ref, v_ref, o_ref, m_ref, acc_ref, va_ref, *, t, ts):
    step = pl.program_id(2)
    qi = qi_ref[step]
    ki = ki_ref[step]
    to_log2 = (FOX_HEAD_DIM ** -0.5) * math.log2(math.e)
    hd = FOX_HEAD_DIM

    @pl.when(ki == 0)
    def _():
        m_ref[...] = jnp.full_like(m_ref, NEG_INF)
        acc_ref[...] = jnp.zeros_like(acc_ref)
        va_ref[:, hd:2 * hd] = jnp.ones((t, hd), BF16)

    va_ref[:, 0:hd] = v_ref[...]

    def scores(rows, nk, r, diagonal):
        s = _dot_nt(q_ref[rows, :], k_ref[0:nk, :])
        if diagonal:
            ri = lax.broadcasted_iota(jnp.int32, (ts, nk), 0) + r * ts
            ci = lax.broadcasted_iota(jnp.int32, (ts, nk), 1)
            s = jnp.where(ci <= ri, s, NEG_INF)
        return s

    def update(diagonal):
        for r in range(t // ts):
            rows = slice(r * ts, (r + 1) * ts)
            nk = (r + 1) * ts if diagonal else t
            s = scores(rows, nk, r, diagonal)
            m_old = m_ref[rows, :]
            m_new = jnp.maximum(m_old, jnp.max(s, axis=1, keepdims=True))
            m_ref[rows, :] = m_new
            alpha = jnp.exp2((m_old - m_new) * to_log2)
            p = jnp.exp2((s - _widen(m_new, nk)) * to_log2)
            acc_ref[rows, :] = _widen(alpha, 2 * hd) * acc_ref[rows, :] + _dot(p.astype(BF16), va_ref[0:nk, :])

    @pl.when(ki < qi)
    def _():
        update(False)

    @pl.when(ki == qi)
    def _():
        update(True)
        o_ref[...] = (acc_ref[:, 0:hd] / acc_ref[:, hd:2 * hd]).astype(BF16)


def _fox_prompt(qa, ka, vb, *, bsz, seq_len, t, ts):
    m = qa.shape[0]
    nq = seq_len // t
    assert seq_len % t == 0 and t % ts == 0
    pairs = [(qi, ki) for qi in range(nq) for ki in range(qi + 1)]
    qi_tab = jnp.asarray([p[0] for p in pairs], jnp.int32)
    ki_tab = jnp.asarray([p[1] for p in pairs], jnp.int32)
    grid_spec = pltpu.PrefetchScalarGridSpec(
        num_scalar_prefetch=2,
        grid=(bsz, FOX_HEADS, len(pairs)),
        in_specs=[pl.BlockSpec((t, FOX_AUG), lambda b, h, s, qi, ki: (b * nq + qi[s], h)),
                  pl.BlockSpec((t, FOX_AUG), lambda b, h, s, qi, ki: (b * nq + ki[s], h)),
                  pl.BlockSpec((t, FOX_HEAD_DIM), lambda b, h, s, qi, ki: (b * nq + ki[s], h))],
        out_specs=pl.BlockSpec((t, FOX_HEAD_DIM), lambda b, h, s, qi, ki: (b * nq + qi[s], h)),
        scratch_shapes=[pltpu.VMEM((t, LANES), F32), pltpu.VMEM((t, 2 * FOX_HEAD_DIM), F32),
                        pltpu.VMEM((t, 2 * FOX_HEAD_DIM), BF16)],
    )
    return pl.pallas_call(
        functools.partial(_fox_kernel, t=t, ts=ts),
        grid_spec=grid_spec,
        out_shape=jax.ShapeDtypeStruct((m, FOX_WIDTH), BF16),
        compiler_params=_cparams(3),
        name="fox_attention",
    )(qi_tab, ki_tab, qa, ka, vb)


def _fox_decode_kernel(q_ref, kn_ref, vn_ref, kc_ref, vc_ref, cq_ref, ckc_ref, ckn_ref, o_ref,
                       m_ref, l_ref, acc_ref, *, lq, ncb):
    j = pl.program_id(1)
    nh, hd = FOX_HEADS, FOX_HEAD_DIM

    @pl.when(j == 0)
    def _():
        m_ref[...] = jnp.full_like(m_ref, NEG_INF)
        l_ref[...] = jnp.zeros_like(l_ref)
        acc_ref[...] = jnp.zeros_like(acc_ref)

    def attend(k_head, v_head, ck_head, causal):
        for h in range(nh):
            qh = q_ref[:, h * hd:(h + 1) * hd].astype(BF16)
            ck = ck_head(h)
            tk = ck.shape[1]
            s = _dot_nt(qh, k_head(h).astype(BF16)) * (hd ** -0.5)
            s = s + (_widen(cq_ref[h], tk) - ck)
            if causal:
                rows = lax.broadcasted_iota(jnp.int32, (lq, tk), 0)
                cols = lax.broadcasted_iota(jnp.int32, (lq, tk), 1)
                s = jnp.where(cols <= rows, s, NEG_INF)
            m_old = m_ref[h]
            m_new = jnp.maximum(m_old, jnp.max(s, axis=1, keepdims=True))
            alpha = jnp.exp(m_old - m_new)
            p = jnp.exp(s - _widen(m_new, tk))
            l_ref[h] = alpha * l_ref[h] + jnp.sum(p, axis=1, keepdims=True)
            acc_ref[h] = alpha * acc_ref[h] + _dot(p.astype(BF16), v_head(h).astype(BF16))
            m_ref[h] = m_new

    @pl.when(j < ncb)
    def _():
        k_hm = pltpu.einshape("mhd->hmd", kc_ref[0])
        v_hm = pltpu.einshape("mhd->hmd", vc_ref[0])
        attend(lambda h: k_hm[h], lambda h: v_hm[h], lambda h: ckc_ref[h], False)

    @pl.when(j == ncb)
    def _():
        attend(lambda h: kn_ref[:, h * hd:(h + 1) * hd], lambda h: vn_ref[:, h * hd:(h + 1) * hd],
               lambda h: ckn_ref[h][:, 0:lq], True)
        for h in range(nh):
            o_ref[:, h * hd:(h + 1) * hd] = (acc_ref[h] / l_ref[h]).astype(BF16)


def _fox_decode(proj, cache_k, cache_v, cq, ck_cache, ck_new, *, bsz, lq, tk):
    past = cache_k.shape[1]
    assert past % tk == 0 and tk % LANES == 0 and lq <= LANES
    ncb = past // tk
    nh, hd, w = FOX_HEADS, FOX_HEAD_DIM, FOX_WIDTH
    tile = lambda j: jnp.minimum(j, ncb - 1)
    cache_spec = pl.BlockSpec((1, tk, nh, hd), lambda b, j: (b, tile(j), 0, 0))
    return pl.pallas_call(
        functools.partial(_fox_decode_kernel, lq=lq, ncb=ncb),
        grid=(bsz, ncb + 1),
        in_specs=[pl.BlockSpec((lq, w), lambda b, j: (b, 0)),
                  pl.BlockSpec((lq, w), lambda b, j: (b, 1)),
                  pl.BlockSpec((lq, w), lambda b, j: (b, 2)),
                  cache_spec,
                  cache_spec,
                  pl.BlockSpec((nh, lq, LANES), lambda b, j: (b, 0, 0)),
                  pl.BlockSpec((nh, 1, tk), lambda b, j: (b, 0, tile(j))),
                  pl.BlockSpec((nh, 1, LANES), lambda b, j: (b, 0, 0))],
        out_specs=pl.BlockSpec((lq, w), lambda b, j: (b, 0)),
        out_shape=jax.ShapeDtypeStruct((bsz * lq, w), BF16),
        scratch_shapes=[pltpu.VMEM((nh, lq, LANES), F32), pltpu.VMEM((nh, lq, LANES), F32),
                        pltpu.VMEM((nh, lq, hd), F32)],
        compiler_params=_cparams(2),
        name="fox_decode",
    )(proj, proj, proj, cache_k, cache_v, cq, ck_cache, ck_new)


def _sconv_kernel(u_ref, bg_ref, cg_ref, cw_ref, hist_ref, y_ref, tail_ref, carry_ref, ext_ref,
                  *, tm, rows, spt, tpb):
    i = pl.program_id(0)
    w = cg_ref[...] * u_ref[...]
    first = (i % tpb) == 0
    for s in range(spt):
        w_s = w[s * rows:(s + 1) * rows]
        if tpb == 1:
            prev = hist_ref[s]
        else:
            prev = jnp.where(first, hist_ref[s], carry_ref[...])
        conv = _conv_rows(ext_ref, w_s, prev, cw_ref, SC_WIDTH, rows)
        y_ref[s * rows:(s + 1) * rows, :] = (bg_ref[s * rows:(s + 1) * rows, :] * conv).astype(BF16)
        tail_ref[s] = w_s[rows - SUBLANES:rows]
    if tpb > 1:
        carry_ref[...] = w[tm - SUBLANES:tm]


def _sconv(proj, conv_w, hist8, *, seq_len, tm):
    m = proj.shape[0]
    assert m % tm == 0
    rows, spt, tpb = _seq_tiling(seq_len, tm)
    nm = m // tm
    hist_map = (lambda i: (i // tpb, 0, 0)) if spt == 1 else (lambda i: (i, 0, 0))
    base = 3 * FOX_WIDTH // SC_DIM
    return pl.pallas_call(
        functools.partial(_sconv_kernel, tm=tm, rows=rows, spt=spt, tpb=tpb),
        grid=(nm,),
        in_specs=[pl.BlockSpec((tm, SC_DIM), lambda i: (i, base)),
                  pl.BlockSpec((tm, SC_DIM), lambda i: (i, base + 1)),
                  pl.BlockSpec((tm, SC_DIM), lambda i: (i, base + 2)),
                  pl.BlockSpec((SC_WIDTH, SC_DIM), lambda i: (0, 0)),
                  pl.BlockSpec((spt, SUBLANES, SC_DIM), hist_map)],
        out_specs=[pl.BlockSpec((tm, SC_DIM), lambda i: (i, 0)),
                   pl.BlockSpec((spt, SUBLANES, SC_DIM), lambda i: (i, 0, 0))],
        out_shape=[jax.ShapeDtypeStruct((m, SC_DIM), BF16),
                   jax.ShapeDtypeStruct((nm * spt, SUBLANES, SC_DIM), F32)],
        scratch_shapes=[pltpu.VMEM((SUBLANES, SC_DIM), F32),
                        pltpu.VMEM((rows + SUBLANES, SC_DIM), F32)],
        compiler_params=_cparams(1),
        name="gated_short_conv",
    )(proj, proj, proj, conv_w, hist8)


def _hist8(state):
    n, w1, c = state.shape
    return jnp.concatenate([jnp.zeros((n, SUBLANES - w1, c), F32), state.astype(F32)], axis=1)


def _tails(tails, n_seq, seq_len, tile_rows, keep):
    per_seq = max(1, seq_len // tile_rows)
    idx = (jnp.arange(n_seq) + 1) * per_seq - 1
    return tails[idx][:, SUBLANES - keep:, :]


def _rope_tables(pos0, length):
    half = RET_DK // 2
    inv = ROPE_BASE ** (-np.arange(half, dtype=np.float64) / half)
    ang = (pos0 + np.arange(length, dtype=np.float64))[:, None] * inv[None, :]
    cos, sin = np.cos(ang), np.sin(ang)
    return (jnp.asarray(np.concatenate([cos, cos], axis=1), F32),
            jnp.asarray(np.concatenate([-sin, sin], axis=1), F32))


def _prep_weights(p):
    d = D_MODEL
    ab_in = p['ab_w_in'][0]
    cd_in = p['cd_w_in'][0]
    f0 = 3 * FOX_WIDTH
    pad_cols = lambda w: jnp.pad(w, ((0, 0), (0, LANES - w.shape[1]))).astype(BF16)
    return dict(
        ab_in=ab_in.astype(BF16),
        ab_small=pad_cols(ab_in[:, AB_MAIN:]),
        cd_in=jnp.concatenate([cd_in[:, :f0], cd_in[:, f0 + FOX_HEADS:]], axis=1).astype(BF16),
        cd_small=pad_cols(cd_in[:, f0:f0 + FOX_HEADS]),
        ab_out=p['ab_w_out'][0].astype(BF16),
        cd_out=p['cd_w_out'][0].astype(BF16),
        ffn_gate=p['ffn_w_gate'].astype(BF16),
        ffn_up=p['ffn_w_up'].astype(BF16),
        ffn_down=p['ffn_w_down'].astype(BF16),
    )


def _trunk(x, pos0, st_ret, st_ssd, st_ssd_conv, c_k, c_v, c_logf, st_sconv, st_ffn, p, wb, t):
    bsz, length, d = x.shape
    m = bsz * length
    xf = x.reshape(m, d)
    zeros = lambda *shape: jnp.zeros(shape, F32)

    proj, dt_proj = _norm_matmul(xf, p['ab_norm_w'][0], wb['ab_in'], wb['ab_small'],
                                 n=AB_MAIN, tm=t['tm_ab'], tn=t['tn_ab'])
    cosf, sinf = _rope_tables(pos0, length)
    ret_state = zeros(bsz, RET_HEADS, RET_DK, RET_DV) if st_ret is None else st_ret
    y_ret, ret_new = _retention(proj, cosf, sinf, ret_state, p['ret_norm_w'][0],
                                bsz=bsz, seq_len=length, c=t['c_ret'])
    ssd_state = zeros(bsz, SSD_HEADS, SSD_HEADDIM, SSD_DSTATE) if st_ssd is None else st_ssd
    ssd_hist = zeros(bsz, SSD_CONV - 1, SSD_CONV_DIM) if st_ssd_conv is None else st_ssd_conv
    y_ssd, ssd_new = _ssd(proj, dt_proj, _hist8(ssd_hist), ssd_state, p['ssd_conv_w'][0], p['ssd_conv_b'][0],
                          p['ssd_dt_bias'][0], p['ssd_A_log'][0], p['ssd_D'][0], p['ssd_norm_w'][0],
                          bsz=bsz, seq_len=length, c=t['c_ssd'])
    xbc_lo = AB_MAIN - SSD_CONV_DIM
    ssd_conv_new = proj.reshape(bsz, length, -1)[:, length - (SSD_CONV - 1):, xbc_lo:AB_MAIN]
    xf = _proj_residual(xf, y_ret, y_ssd, wb['ab_out'], tm=t['tm_out'], tn=t['tn_out'])

    ffn_new = []
    ffn_hist0 = zeros(bsz, FFN_CONV - 1, D_FF) if st_ffn is None else st_ffn[0]
    xf, tails = _conv_ffn(xf, p['ffn_norm_w'][0], wb['ffn_gate'], wb['ffn_up'], wb['ffn_down'],
                          p['ffn_conv_w'][0], p['ffn_conv_b'][0], _hist8(ffn_hist0), p['final_norm_w'],
                          layer=0, seq_len=length, tm=t['tm_ffn'], tf=t['tf_ffn'], ts=t['ts_ffn'], final=False)
    ffn_new.append(_tails(tails, bsz, length, t['tm_ffn'], FFN_CONV - 1))

    proj, fl_proj = _norm_matmul(xf, p['cd_norm_w'][0], wb['cd_in'], wb['cd_small'],
                                 n=CD_MAIN, tm=t['tm_cd'], tn=t['tn_cd'])
    f_bias = jnp.pad(p['fox_f_bias'][0].astype(F32), (0, LANES - FOX_HEADS)).reshape(1, LANES)
    head_shape = (bsz, length, FOX_HEADS, FOX_HEAD_DIM)
    if c_k is None:
        qa, ka, vb, k32, v32, logf = _fox_prep(proj, fl_proj, f_bias, bsz=bsz, seq_len=length, tp=t['t_prep'])
        y_fox = _fox_prompt(qa, ka, vb, bsz=bsz, seq_len=length, t=t['t_fox'], ts=t['ts_fox'])
        logf_new = logf.reshape(bsz, length, LANES)[:, :, :FOX_HEADS]
        k_new, v_new = k32.reshape(head_shape), v32.reshape(head_shape)
    else:
        proj3 = proj.reshape(bsz, length, -1)
        k_new = proj3[:, :, FOX_WIDTH:2 * FOX_WIDTH].reshape(head_shape)
        v_new = proj3[:, :, 2 * FOX_WIDTH:3 * FOX_WIDTH].reshape(head_shape)
        past = c_k.shape[1]
        pairs = bsz * FOX_HEADS
        assert pairs <= LANES and length <= LANES
        to_lanes = lambda a, rows: jnp.pad(jnp.swapaxes(a, 0, 1).reshape(a.shape[1], pairs),
                                           ((0, rows - a.shape[1]), (0, LANES - pairs)))
        from_lanes = lambda a: jnp.swapaxes(a[:length, :pairs].reshape(length, bsz, FOX_HEADS), 0, 1)
        cache_lf = to_lanes(c_logf.astype(F32), past)
        fl_rows = to_lanes(fl_proj.reshape(bsz, length, LANES)[:, :, :FOX_HEADS], LANES)
        bias_lanes = jnp.pad(jnp.tile(p['fox_f_bias'][0].astype(F32), bsz), (0, LANES - pairs)).reshape(1, LANES)
        cum_t_cache, lf_rows, cum_rows, cum_t_new = _decode_cum(cache_lf, fl_rows, bias_lanes, c=t['c_cum'])
        logf_new = from_lanes(lf_rows)
        cq = jnp.broadcast_to(cum_rows[:length, :pairs].T[:, :, None], (pairs, length, LANES))
        y_fox = _fox_decode(proj, c_k, c_v, cq, cum_t_cache[:pairs, None, :], cum_t_new[:pairs, None, :],
                            bsz=bsz, lq=length, tk=t['tk_dec'])
    sc_hist = zeros(bsz, SC_WIDTH - 1, SC_DIM) if st_sconv is None else st_sconv
    y_sc, sc_tails = _sconv(proj, p['sconv_w'][0], _hist8(sc_hist), seq_len=length, tm=t['tm_sc'])
    sconv_new = _tails(sc_tails, bsz, length, t['tm_sc'], SC_WIDTH - 1)
    xf = _proj_residual(xf, y_fox, y_sc, wb['cd_out'], tm=t['tm_out'], tn=t['tn_out'])

    ffn_hist1 = zeros(bsz, FFN_CONV - 1, D_FF) if st_ffn is None else st_ffn[1]
    xf, tails = _conv_ffn(xf, p['ffn_norm_w'][1], wb['ffn_gate'], wb['ffn_up'], wb['ffn_down'],
                          p['ffn_conv_w'][1], p['ffn_conv_b'][1], _hist8(ffn_hist1), p['final_norm_w'],
                          layer=1, seq_len=length, tm=t['tm_ffn'], tf=t['tf_ffn'], ts=t['ts_ffn'], final=True)
    ffn_new.append(_tails(tails, bsz, length, t['tm_ffn'], FFN_CONV - 1))

    return (xf.reshape(bsz, length, d), ret_new[None], ssd_new[None], ssd_conv_new[None], k_new[None],
            v_new[None], logf_new[None], sconv_new[None], jnp.stack(ffn_new))


def _largest_divisor(n, cap, multiple=1):
    best = None
    for cand in range(multiple, min(n, cap) + 1, multiple):
        if n % cand == 0:
            best = cand
    assert best is not None, (n, cap, multiple)
    return best


def _tiles(bsz, length, past=None):
    m = bsz * length
    seq_tile = lambda cap: _largest_divisor(length, cap, SUBLANES)
    row_tile = lambda cap: (_largest_divisor(length, cap, SUBLANES) if length >= cap
                            else _largest_divisor(m, cap, length))
    t = dict(
        tm_ab=row_tile(512), tn_ab=2816, tm_cd=row_tile(1024), tn_cd=2048,
        tm_out=row_tile(1024), tn_out=1024,
        tm_ffn=row_tile(1024), tf_ffn=512, ts_ffn=512,
        tm_sc=row_tile(512),
        c_ret=seq_tile(256), c_ssd=seq_tile(256),
    )
    if past is None:
        t['t_fox'] = seq_tile(2048)
        t['ts_fox'] = _largest_divisor(t['t_fox'], 256, LANES)
        t['t_prep'] = seq_tile(512)
    else:
        t['tk_dec'] = _largest_divisor(past, 1024, LANES)
        t['c_cum'] = _largest_divisor(past, 256, LANES)
    return t


def kernel(x_prompt, x_sample, state_ret, state_ssd, state_ssd_conv, cache_fox_k, cache_fox_v, cache_fox_logf, state_sconv, state_ffn_conv, ab_norm_w, ab_w_in, ret_norm_w, ssd_conv_w, ssd_conv_b, ssd_dt_bias, ssd_A_log, ssd_D, ssd_norm_w, ab_w_out, cd_norm_w, cd_w_in, fox_f_bias, sconv_w, cd_w_out, ffn_norm_w, ffn_w_gate, ffn_w_up, ffn_conv_w, ffn_conv_b, ffn_w_down, final_norm_w):
    p = dict(ab_norm_w=ab_norm_w, ab_w_in=ab_w_in, ret_norm_w=ret_norm_w, ssd_conv_w=ssd_conv_w,
             ssd_conv_b=ssd_conv_b, ssd_dt_bias=ssd_dt_bias, ssd_A_log=ssd_A_log, ssd_D=ssd_D,
             ssd_norm_w=ssd_norm_w, ab_w_out=ab_w_out, cd_norm_w=cd_norm_w, cd_w_in=cd_w_in,
             fox_f_bias=fox_f_bias, sconv_w=sconv_w, cd_w_out=cd_w_out, ffn_norm_w=ffn_norm_w,
             ffn_w_gate=ffn_w_gate, ffn_w_up=ffn_w_up, ffn_conv_w=ffn_conv_w, ffn_conv_b=ffn_conv_b,
             ffn_w_down=ffn_w_down, final_norm_w=final_norm_w)
    assert x_prompt.shape[-1] == D_MODEL and ab_w_in.shape == (1, D_MODEL, AB_MAIN + SSD_HEADS)
    assert cd_w_in.shape == (1, D_MODEL, CD_MAIN + FOX_HEADS) and ffn_w_gate.shape == (2, D_MODEL, D_FF)
    wb = _prep_weights(p)
    bp, lp_, _ = x_prompt.shape
    bs, ls, _ = x_sample.shape
    past = cache_fox_k.shape[2]
    (y_prompt, p_ret, p_ssd, p_ssd_conv, p_fox_k, p_fox_v, p_fox_logf, p_sconv, p_ffn_conv) = _trunk(
        x_prompt, 0, None, None, None, None, None, None, None, None, p, wb, _tiles(bp, lp_))
    (y_sample, s_ret, s_ssd, s_ssd_conv, s_fox_k, s_fox_v, s_fox_logf, s_sconv, s_ffn_conv) = _trunk(
        x_sample, past, state_ret[0], state_ssd[0], state_ssd_conv[0], cache_fox_k[0], cache_fox_v[0],
        cache_fox_logf[0], state_sconv[0], state_ffn_conv, p, wb, _tiles(bs, ls, past))
    return (y_prompt, y_sample, p_ret, s_ret, p_ssd, s_ssd, p_ssd_conv, s_ssd_conv, p_fox_k, s_fox_k,
            p_fox_v, s_fox_v, p_fox_logf, s_fox_logf, p_sconv, s_sconv, p_ffn_conv, s_ffn_conv)
```

```python
import functools
import math

import numpy as np
import jax
import jax.numpy as jnp
from jax import lax
from jax.experimental import pallas as pl
from jax.experimental.pallas import tpu as pltpu

F32 = jnp.float32
BF16 = jnp.bfloat16
EPS = 1e-6
ROPE_BASE = 10000.0
NEG_INF = float("-inf")

D_MODEL = 2048
RET_HEADS, RET_DK, RET_DV = 4, 128, 256
SSD_DINNER, SSD_HEADDIM, SSD_HEADS, SSD_GROUPS, SSD_DSTATE, SSD_CONV = 1024, 64, 16, 2, 128, 4
SSD_CONV_DIM = SSD_DINNER + 2 * SSD_GROUPS * SSD_DSTATE
FOX_HEADS, FOX_HEAD_DIM = 8, 128
FOX_WIDTH = FOX_HEADS * FOX_HEAD_DIM
SC_DIM, SC_WIDTH = 1024, 3
D_FF, FFN_CONV = 5632, 3
AB_MAIN = 2 * RET_HEADS * RET_DK + 2 * RET_HEADS * RET_DV + SSD_DINNER + SSD_CONV_DIM
AB_PAD = AB_MAIN + 128
CD_MAIN = 3 * FOX_WIDTH + 3 * SC_DIM
CD_PAD = CD_MAIN + 128

LANES = 128
SUBLANES = 8
VMEM_LIMIT = 60 * 1024 * 1024


def _cparams(n_axes):
    return pltpu.CompilerParams(dimension_semantics=("arbitrary",) * n_axes,
                                vmem_limit_bytes=VMEM_LIMIT)


def _rms(xf, w):
    return xf * lax.rsqrt(jnp.mean(xf * xf, axis=-1, keepdims=True) + EPS) * w


def _softplus(x):
    return jnp.maximum(x, 0.0) + jnp.log1p(jnp.exp(-jnp.abs(x)))


def _split3(x):
    hi = x.astype(BF16)
    r1 = x - hi.astype(F32)
    mid = r1.astype(BF16)
    lo = (r1 - mid.astype(F32)).astype(BF16)
    return hi, mid, lo


def _widen(x, n):
    return x[:, 0:n] if n <= LANES else jnp.concatenate([x] * (n // LANES), axis=1)


def _dot(a, b):
    return jnp.dot(a, b, preferred_element_type=F32)


def _dot_nt(a, b):
    return lax.dot_general(a, b, (((1,), (1,)), ((), ())), preferred_element_type=F32)


def _dot_tn(a, b):
    return lax.dot_general(a, b, (((0,), (0,)), ((), ())), preferred_element_type=F32)


def _exact_lhs_dot(m_bf16, x):
    hi, mid, lo = _split3(x)
    return _dot(m_bf16, hi) + _dot(m_bf16, mid) + _dot(m_bf16, lo)


def _exact_rhs_dot(x, m_bf16):
    hi, mid, lo = _split3(x)
    return _dot(hi, m_bf16) + _dot(mid, m_bf16) + _dot(lo, m_bf16)


def _conv_rows(ext_ref, x, prev8, w_ref, width, rows, w_cols=slice(None)):
    ext_ref[0:SUBLANES, :] = prev8
    ext_ref[SUBLANES:SUBLANES + rows, :] = x
    out = None
    for j in range(width):
        off = SUBLANES - (width - 1) + j
        term = ext_ref[off:off + rows, :] * w_ref[j:j + 1, w_cols]
        out = term if out is None else out + term
    return out


def _seq_tiling(seq_len, tile_rows):
    if seq_len >= tile_rows:
        assert seq_len % tile_rows == 0
        return tile_rows, 1, seq_len // tile_rows
    assert tile_rows % seq_len == 0 and seq_len % SUBLANES == 0
    return seq_len, tile_rows // seq_len, 1


def _norm_matmul_kernel(x_ref, nw_ref, w_ref, ws_ref, o_ref, os_ref, h_ref, *, n_tiles, tn, last_w):
    j = pl.program_id(1)

    @pl.when(j == 0)
    def _():
        h_ref[...] = _rms(x_ref[...], nw_ref[...]).astype(BF16)
        os_ref[...] = _dot(h_ref[...], ws_ref[...])

    if last_w == tn:
        o_ref[...] = _dot(h_ref[...], w_ref[...])
    else:
        @pl.when(j < n_tiles - 1)
        def _():
            o_ref[...] = _dot(h_ref[...], w_ref[...])

        @pl.when(j == n_tiles - 1)
        def _():
            o_ref[:, 0:last_w] = _dot(h_ref[...], w_ref[:, 0:last_w])


def _norm_matmul(x, norm_w, w, w_small, *, n, tm, tn):
    m, d = x.shape
    n_tiles = pl.cdiv(n, tn)
    last_w = n - (n_tiles - 1) * tn
    assert m % tm == 0 and last_w % LANES == 0 and n <= w.shape[1] and w_small.shape == (d, LANES)
    return pl.pallas_call(
        functools.partial(_norm_matmul_kernel, n_tiles=n_tiles, tn=tn, last_w=last_w),
        grid=(m // tm, n_tiles),
        in_specs=[pl.BlockSpec((tm, d), lambda i, j: (i, 0)),
                  pl.BlockSpec((1, d), lambda i, j: (0, 0)),
                  pl.BlockSpec((d, tn), lambda i, j: (0, j)),
                  pl.BlockSpec((d, LANES), lambda i, j: (0, 0))],
        out_specs=[pl.BlockSpec((tm, tn), lambda i, j: (i, j)),
                   pl.BlockSpec((tm, LANES), lambda i, j: (i, 0))],
        out_shape=[jax.ShapeDtypeStruct((m, n), F32),
                   jax.ShapeDtypeStruct((m, LANES), F32)],
        scratch_shapes=[pltpu.VMEM((tm, d), BF16)],
        compiler_params=_cparams(2),
        name="norm_in_proj",
    )(x, norm_w.reshape(1, d), w, w_small)


def _proj_res_kernel(x_ref, a_ref, b_ref, wa_ref, wb_ref, o_ref):
    acc = _dot(a_ref[...], wa_ref[...])
    acc = acc + _dot(b_ref[...], wb_ref[...])
    o_ref[...] = x_ref[...] + acc


def _proj_residual(x, a, b, w, *, tm, tn):
    m, d = x.shape
    ka, kb = a.shape[1], b.shape[1]
    assert m % tm == 0 and d % tn == 0 and ka == kb and w.shape == (ka + kb, d)
    return pl.pallas_call(
        _proj_res_kernel,
        grid=(m // tm, d // tn),
        in_specs=[pl.BlockSpec((tm, tn), lambda i, j: (i, j)),
                  pl.BlockSpec((tm, ka), lambda i, j: (i, 0)),
                  pl.BlockSpec((tm, kb), lambda i, j: (i, 0)),
                  pl.BlockSpec((ka, tn), lambda i, j: (0, j)),
                  pl.BlockSpec((kb, tn), lambda i, j: (1, j))],
        out_specs=pl.BlockSpec((tm, tn), lambda i, j: (i, j)),
        out_shape=jax.ShapeDtypeStruct((m, d), F32),
        compiler_params=_cparams(2),
        name="out_proj_residual",
    )(x, a, b, w, w)


def _ffn_kernel(x_ref, nw_ref, wg_ref, wu_ref, wd_ref, cw_ref, cb_ref, hist_ref, fw_ref,
                o_ref, tail_ref, h_ref, carry_ref, ext_ref,
                *, tm, rows, spt, tpb, nf, ts, nsub, nsub_last, final):
    i = pl.program_id(0)
    f = pl.program_id(1)

    @pl.when(f == 0)
    def _():
        xf = x_ref[...]
        h_ref[...] = _rms(xf, nw_ref[...]).astype(BF16)
        o_ref[...] = xf

    first = (i % tpb) == 0

    def sub_block(sb):
        cols = slice(sb * ts, (sb + 1) * ts)
        h = h_ref[...]
        a = _dot(h, wg_ref[:, cols])
        u = _dot(h, wu_ref[:, cols])
        convs = []
        for s in range(spt):
            a_s = a[s * rows:(s + 1) * rows]
            if tpb == 1:
                prev = hist_ref[s, :, cols]
            else:
                prev = jnp.where(first, hist_ref[s, :, cols], carry_ref[f * nsub + sb])
            convs.append(_conv_rows(ext_ref, a_s, prev, cw_ref, FFN_CONV, rows, cols))
            tail_ref[s, :, cols] = a_s[rows - SUBLANES:rows]
        if tpb > 1:
            carry_ref[f * nsub + sb] = a[tm - SUBLANES:tm]
        conv = convs[0] if spt == 1 else jnp.concatenate(convs, axis=0)
        act = (jax.nn.silu(conv + cb_ref[:, cols]) * u).astype(BF16)
        o_ref[...] += _dot(act, wd_ref[cols, :])

    if nsub_last == nsub:
        for sb in range(nsub):
            sub_block(sb)
    else:
        @pl.when(f < nf - 1)
        def _():
            for sb in range(nsub):
                sub_block(sb)

        @pl.when(f == nf - 1)
        def _():
            for sb in range(nsub_last):
                sub_block(sb)

    if final:
        @pl.when(f == nf - 1)
        def _():
            o_ref[...] = _rms(o_ref[...], fw_ref[...])


def _conv_ffn(x, norm_w, wg, wu, wd, conv_w, conv_b, hist8, final_w, *, layer, seq_len, tm, tf, ts, final):
    m, d = x.shape
    ff = wg.shape[2]
    assert m % tm == 0 and tf % ts == 0 and ff % ts == 0
    rows, spt, tpb = _seq_tiling(seq_len, tm)
    nm, nf = m // tm, pl.cdiv(ff, tf)
    nsub = tf // ts
    nsub_last = (ff - (nf - 1) * tf) // ts
    hist_map = (lambda i, f: (i // tpb, 0, f)) if spt == 1 else (lambda i, f: (i, 0, f))
    kern = functools.partial(_ffn_kernel, tm=tm, rows=rows, spt=spt, tpb=tpb, nf=nf, ts=ts, nsub=nsub,
                             nsub_last=nsub_last, final=final)
    out, tails = pl.pallas_call(
        kern,
        grid=(nm, nf),
        in_specs=[pl.BlockSpec((tm, d), lambda i, f: (i, 0)),
                  pl.BlockSpec((1, d), lambda i, f: (0, 0)),
                  pl.BlockSpec((None, d, tf), lambda i, f: (layer, 0, f)),
                  pl.BlockSpec((None, d, tf), lambda i, f: (layer, 0, f)),
                  pl.BlockSpec((None, tf, d), lambda i, f: (layer, f, 0)),
                  pl.BlockSpec((FFN_CONV, tf), lambda i, f: (0, f)),
                  pl.BlockSpec((1, tf), lambda i, f: (0, f)),
                  pl.BlockSpec((spt, SUBLANES, tf), hist_map),
                  pl.BlockSpec((1, d), lambda i, f: (0, 0))],
        out_specs=[pl.BlockSpec((tm, d), lambda i, f: (i, 0)),
                   pl.BlockSpec((spt, SUBLANES, tf), lambda i, f: (i, 0, f))],
        out_shape=[jax.ShapeDtypeStruct((m, d), F32),
                   jax.ShapeDtypeStruct((nm * spt, SUBLANES, ff), F32)],
        scratch_shapes=[pltpu.VMEM((tm, d), BF16),
                        pltpu.VMEM((nf * nsub, SUBLANES, ts), F32),
                        pltpu.VMEM((rows + SUBLANES, ts), F32)],
        compiler_params=_cparams(2),
        name="conv_ffn",
    )(x, norm_w.reshape(1, d), wg, wu, wd, conv_w, conv_b.reshape(1, ff), hist8, final_w.reshape(1, d))
    return out, tails


def _retention_kernel(q_ref, k_ref, v_ref, g_ref, cos_ref, sin_ref, st_ref, nw_ref,
                      y_ref, so_ref, *, c):
    ci = pl.program_id(1)

    @pl.when(ci == 0)
    def _():
        so_ref[...] = st_ref[...]

    cos = cos_ref[...]
    sin = sin_ref[...]
    ii = lax.broadcasted_iota(jnp.int32, (c, c), 0)
    jj = lax.broadcasted_iota(jnp.int32, (c, c), 1)
    diff = (ii - jj).astype(F32)
    causal = ii >= jj
    ridx = lax.broadcasted_iota(jnp.int32, (c, 1), 0).astype(F32)
    for h in range(RET_HEADS):
        lg = math.log1p(-(2.0 ** (-5.0 - h)))
        q = q_ref[:, h * RET_DK:(h + 1) * RET_DK]
        k = k_ref[:, h * RET_DK:(h + 1) * RET_DK]
        v = v_ref[:, h * RET_DV:(h + 1) * RET_DV]
        qr = q * cos + pltpu.roll(q, RET_DK // 2, 1) * sin
        kr = (k * cos + pltpu.roll(k, RET_DK // 2, 1) * sin) * (RET_DK ** -0.5)
        qb = qr.astype(BF16)
        kb = kr.astype(BF16)
        vb = v.astype(BF16)
        decay = jnp.exp(jnp.where(causal, diff * lg, NEG_INF))
        inner = jnp.exp((ridx + 1.0) * lg)
        sdecay = jnp.exp((c - 1.0 - ridx) * lg)
        s = so_ref[0, h]
        scores = _dot_nt(qb, kb) * decay
        y = _dot(scores.astype(BF16), vb)
        y = y + _dot(qb, s.astype(BF16)) * inner
        kd = (kr * sdecay).astype(BF16)
        so_ref[0, h] = math.exp(c * lg) * s + _dot_tn(kd, vb)
        mu = jnp.mean(y, axis=-1, keepdims=True)
        yc = y - mu
        var = jnp.mean(yc * yc, axis=-1, keepdims=True)
        yn = yc * lax.rsqrt(var + EPS) * nw_ref[:, h * RET_DV:(h + 1) * RET_DV]
        g = g_ref[:, h * RET_DV:(h + 1) * RET_DV]
        y_ref[:, h * RET_DV:(h + 1) * RET_DV] = (jax.nn.silu(g) * yn).astype(BF16)


def _retention(proj, cosf, sinf, state, norm_w, *, bsz, seq_len, c):
    m = proj.shape[0]
    nc = seq_len // c
    assert seq_len % c == 0
    qk_w = RET_HEADS * RET_DK
    v_w = RET_HEADS * RET_DV
    row = lambda b, ci: b * nc + ci
    y, s_new = pl.pallas_call(
        functools.partial(_retention_kernel, c=c),
        grid=(bsz, nc),
        in_specs=[pl.BlockSpec((c, qk_w), lambda b, ci: (row(b, ci), 0)),
                  pl.BlockSpec((c, qk_w), lambda b, ci: (row(b, ci), 1)),
                  pl.BlockSpec((c, v_w), lambda b, ci: (row(b, ci), 1)),
                  pl.BlockSpec((c, v_w), lambda b, ci: (row(b, ci), 2)),
                  pl.BlockSpec((c, RET_DK), lambda b, ci: (ci, 0)),
                  pl.BlockSpec((c, RET_DK), lambda b, ci: (ci, 0)),
                  pl.BlockSpec((1, RET_HEADS, RET_DK, RET_DV), lambda b, ci: (b, 0, 0, 0)),
                  pl.BlockSpec((1, v_w), lambda b, ci: (0, 0))],
        out_specs=[pl.BlockSpec((c, v_w), lambda b, ci: (row(b, ci), 0)),
                   pl.BlockSpec((1, RET_HEADS, RET_DK, RET_DV), lambda b, ci: (b, 0, 0, 0))],
        out_shape=[jax.ShapeDtypeStruct((m, v_w), BF16),
                   jax.ShapeDtypeStruct(state.shape, F32)],
        compiler_params=_cparams(2),
        name="retention",
    )(proj, proj, proj, proj, cosf, sinf, state, norm_w.reshape(1, v_w))
    return y, s_new


def _ssd_kernel(z_ref, xs_ref, bc_ref, dt_ref, hx_ref, hbc_ref, st_ref,
                cwx_ref, cwbc_ref, cbx_ref, cbbc_ref, dtb_ref, alog_ref, dsk_ref, nw_ref,
                tri_ref, exp_ref,
                y_ref, so_ref,
                st_scr, cx_scr, cbc_scr, extx_scr, extbc_scr, yh_scr, xs_scr, *, c, nc):
    ci = pl.program_id(1)
    gw = SSD_DINNER // SSD_GROUPS
    hpg = SSD_HEADS // SSD_GROUPS

    @pl.when(ci == 0)
    def _():
        st_scr[...] = st_ref[0].T
        cx_scr[...] = hx_ref[0]
        cbc_scr[...] = hbc_ref[0]

    xs_raw = xs_ref[...]
    bc_raw = bc_ref[...]
    xs_scr[...] = jax.nn.silu(_conv_rows(extx_scr, xs_raw, cx_scr[...], cwx_ref, SSD_CONV, c) + cbx_ref[...])
    bcm = jax.nn.silu(_conv_rows(extbc_scr, bc_raw, cbc_scr[...], cwbc_ref, SSD_CONV, c) + cbbc_ref[...])
    cx_scr[...] = xs_raw[c - SUBLANES:c]
    cbc_scr[...] = bc_raw[c - SUBLANES:c]

    tri = tri_ref[...]
    dt = _softplus(dt_ref[...] + dtb_ref[...])
    a = -jnp.exp(alog_ref[...])
    acs = _exact_lhs_dot(tri, dt * a)
    acs_t = acs.T
    acs_last = acs[c - 1:c, :]
    exp_acs = jnp.exp(acs)
    to_end = jnp.exp(acs_last - acs)
    chunk_dec = jnp.exp(acs_last)

    ii = lax.broadcasted_iota(jnp.int32, (c, c), 0)
    jj = lax.broadcasted_iota(jnp.int32, (c, c), 1)
    causal = ii >= jj
    nb = SSD_GROUPS * SSD_DSTATE
    for g in range(SSD_GROUPS):
        cols = slice(g * gw, (g + 1) * gw)
        expand = exp_ref[:, cols]
        xdt = xs_scr[:, cols] * _exact_rhs_dot(dt, expand)
        xdt_b = xdt.astype(BF16)
        xend_b = (xdt * _exact_rhs_dot(to_end, expand)).astype(BF16)
        b_g = bcm[:, g * SSD_DSTATE:(g + 1) * SSD_DSTATE].astype(BF16)
        c_g = bcm[:, nb + g * SSD_DSTATE:nb + (g + 1) * SSD_DSTATE].astype(BF16)
        cb = _dot_nt(c_g, b_g)
        s_g = st_scr[:, cols]
        y_state = _dot(c_g, s_g.astype(BF16)) * _exact_rhs_dot(exp_acs, expand)
        for r in range(hpg):
            hh = g * hpg + r
            seg = acs[:, hh:hh + 1] - acs_t[hh:hh + 1, :]
            lmat = jnp.exp(jnp.where(causal, seg, NEG_INF))
            mm = (cb * lmat).astype(BF16)
            head = slice(r * SSD_HEADDIM, (r + 1) * SSD_HEADDIM)
            yh_scr[:, hh * SSD_HEADDIM:(hh + 1) * SSD_HEADDIM] = _dot(mm, xdt_b[:, head]) + y_state[:, head]
        upd = _dot_tn(b_g, xend_b)
        st_scr[:, cols] = _exact_rhs_dot(chunk_dec, expand) * s_g + upd

    y = yh_scr[...] + dsk_ref[...] * xs_scr[...]
    z = z_ref[...]
    y_ref[...] = _rms(y * jax.nn.silu(z), nw_ref[...]).astype(BF16)

    @pl.when(ci == nc - 1)
    def _():
        so_ref[0] = st_scr[...].T


def _ssd(proj, dt_proj, hist8, state, conv_w, conv_b, dt_bias, a_log, d_skip, norm_w, *, bsz, seq_len, c):
    m = proj.shape[0]
    nc = seq_len // c
    assert seq_len % c == 0
    row = lambda b, ci: b * nc + ci
    const2 = lambda b, ci: (0, 0)
    di, bcw = SSD_DINNER, 2 * SSD_GROUPS * SSD_DSTATE
    tri = jnp.asarray(np.tril(np.ones((c, c), np.float32)), BF16)
    expand = np.zeros((LANES, di), np.float32)
    for h in range(SSD_HEADS):
        expand[h, h * SSD_HEADDIM:(h + 1) * SSD_HEADDIM] = 1.0
    expand = jnp.asarray(expand, BF16)
    pad_row = lambda v: jnp.pad(v.astype(F32), (0, LANES - v.shape[0])).reshape(1, LANES)
    st2 = state.reshape(bsz, di, SSD_DSTATE)
    y, s_new = pl.pallas_call(
        functools.partial(_ssd_kernel, c=c, nc=nc),
        grid=(bsz, nc),
        in_specs=[pl.BlockSpec((c, di), lambda b, ci: (row(b, ci), 3)),
                  pl.BlockSpec((c, di), lambda b, ci: (row(b, ci), 4)),
                  pl.BlockSpec((c, bcw), lambda b, ci: (row(b, ci), 10)),
                  pl.BlockSpec((c, LANES), lambda b, ci: (row(b, ci), 0)),
                  pl.BlockSpec((1, SUBLANES, di), lambda b, ci: (b, 0, 0)),
                  pl.BlockSpec((1, SUBLANES, bcw), lambda b, ci: (b, 0, 2)),
                  pl.BlockSpec((1, di, SSD_DSTATE), lambda b, ci: (b, 0, 0)),
                  pl.BlockSpec((SSD_CONV, di), const2),
                  pl.BlockSpec((SSD_CONV, bcw), lambda b, ci: (0, 2)),
                  pl.BlockSpec((1, di), const2),
                  pl.BlockSpec((1, bcw), lambda b, ci: (0, 2)),
                  pl.BlockSpec((1, LANES), const2),
                  pl.BlockSpec((1, LANES), const2),
                  pl.BlockSpec((1, di), const2),
                  pl.BlockSpec((1, di), const2),
                  pl.BlockSpec((c, c), const2),
                  pl.BlockSpec((LANES, di), const2)],
        out_specs=[pl.BlockSpec((c, di), lambda b, ci: (row(b, ci), 0)),
                   pl.BlockSpec((1, di, SSD_DSTATE), lambda b, ci: (b, 0, 0))],
        out_shape=[jax.ShapeDtypeStruct((m, di), BF16),
                   jax.ShapeDtypeStruct(st2.shape, F32)],
        scratch_shapes=[pltpu.VMEM((SSD_DSTATE, di), F32),
                        pltpu.VMEM((SUBLANES, di), F32),
                        pltpu.VMEM((SUBLANES, bcw), F32),
                        pltpu.VMEM((c + SUBLANES, di), F32),
                        pltpu.VMEM((c + SUBLANES, bcw), F32),
                        pltpu.VMEM((c, di), F32),
                        pltpu.VMEM((c, di), F32)],
        compiler_params=_cparams(2),
        name="ssd",
    )(proj, proj, proj, dt_proj, hist8, hist8, st2,
      conv_w, conv_w, conv_b.reshape(1, -1), conv_b.reshape(1, -1),
      pad_row(dt_bias), pad_row(a_log), jnp.repeat(d_skip.astype(F32), SSD_HEADDIM).reshape(1, di),
      norm_w.reshape(1, di), tri, expand)
    return y, s_new.reshape(state.shape)


def _decode_cum_kernel(lfc_ref, fl_ref, b_ref, tri_ref, cumt_c_ref, lfn_ref, cumn_ref, cumt_n_ref,
                       carry_ref, *, c, ncb):
    j = pl.program_id(0)

    @pl.when(j == 0)
    def _():
        carry_ref[...] = jnp.zeros_like(carry_ref)

    @pl.when(j < ncb)
    def _():
        cum = _exact_lhs_dot(tri_ref[...], lfc_ref[...]) + carry_ref[...]
        carry_ref[...] = cum[c - 1:c, :]
        cumt_c_ref[...] = cum.T

    @pl.when(j == ncb)
    def _():
        lf = -_softplus(-(fl_ref[...] + b_ref[...]))
        lfn_ref[...] = lf
        cum = _exact_lhs_dot(tri_ref[0:LANES, 0:LANES], lf) + carry_ref[...]
        cumn_ref[...] = cum
        cumt_n_ref[...] = cum.T


def _decode_cum(cache_lf, fl_new, bias, *, c):
    past = cache_lf.shape[0]
    assert past % c == 0 and c % LANES == 0 and fl_new.shape == (LANES, LANES)
    ncb = past // c
    tri = jnp.asarray(np.tril(np.ones((c, c), np.float32)), BF16)
    blk = lambda j: jnp.minimum(j, ncb - 1)
    sq = jax.ShapeDtypeStruct((LANES, LANES), F32)
    return pl.pallas_call(
        functools.partial(_decode_cum_kernel, c=c, ncb=ncb),
        grid=(ncb + 1,),
        in_specs=[pl.BlockSpec((c, LANES), lambda j: (blk(j), 0)),
                  pl.BlockSpec((LANES, LANES), lambda j: (0, 0)),
                  pl.BlockSpec((1, LANES), lambda j: (0, 0)),
                  pl.BlockSpec((c, c), lambda j: (0, 0))],
        out_specs=[pl.BlockSpec((LANES, c), lambda j: (0, blk(j))),
                   pl.BlockSpec((LANES, LANES), lambda j: (0, 0)),
                   pl.BlockSpec((LANES, LANES), lambda j: (0, 0)),
                   pl.BlockSpec((LANES, LANES), lambda j: (0, 0))],
        out_shape=[jax.ShapeDtypeStruct((LANES, past), F32), sq, sq, sq],
        scratch_shapes=[pltpu.VMEM((1, LANES), F32)],
        compiler_params=_cparams(1),
        name="decode_logf_cumsum",
    )(cache_lf, fl_new, bias, tri)


FOX_AUG = 2 * FOX_HEAD_DIM
N_BIAS_PIECES = 3


def _fox_prep_kernel(q_ref, k_ref, v_ref, fl_ref, fb_ref, tri_ref, place_ref, ones_ref,
                     qa_ref, ka_ref, vb_ref, k32_ref, v32_ref, lf_ref, carry_ref, *, tp):
    @pl.when(pl.program_id(1) == 0)
    def _():
        carry_ref[...] = jnp.zeros_like(carry_ref)

    lf = -_softplus(-(fl_ref[...] + fb_ref[...]))
    lf_ref[...] = lf
    cum = _exact_lhs_dot(tri_ref[...], lf) + carry_ref[...]
    carry_ref[...] = cum[tp - 1:tp, :]
    pieces = _split3(cum * (FOX_HEAD_DIM ** 0.5))
    n = N_BIAS_PIECES
    aug_q = ones_ref[0:1, :] + sum(_dot(pieces[r], place_ref[r]) for r in range(n))
    aug_k = ones_ref[1:2, :] - sum(_dot(pieces[r], place_ref[n + r]) for r in range(n))
    for h in range(FOX_HEADS):
        src = slice(h * FOX_HEAD_DIM, (h + 1) * FOX_HEAD_DIM)
        feat = slice(h * FOX_AUG, h * FOX_AUG + FOX_HEAD_DIM)
        bias = slice(h * FOX_AUG + FOX_HEAD_DIM, (h + 1) * FOX_AUG)
        qa_ref[:, feat] = q_ref[:, src].astype(BF16)
        qa_ref[:, bias] = aug_q[:, src].astype(BF16)
        ka_ref[:, feat] = k_ref[:, src].astype(BF16)
        ka_ref[:, bias] = aug_k[:, src].astype(BF16)
    k = k_ref[...]
    v = v_ref[...]
    k32_ref[...] = pltpu.einshape("m(hd)->mhd", k, h=FOX_HEADS)
    v32_ref[...] = pltpu.einshape("m(hd)->mhd", v, h=FOX_HEADS)
    vb_ref[...] = v.astype(BF16)


def _fox_prep(proj, fl_proj, f_bias, *, bsz, seq_len, tp):
    m = proj.shape[0]
    nt = seq_len // tp
    assert seq_len % tp == 0
    w = FOX_WIDTH
    tri = jnp.asarray(np.tril(np.ones((tp, tp), np.float32)), BF16)
    n = N_BIAS_PIECES
    place = np.zeros((2 * n, LANES, w), np.float32)
    ones = np.zeros((SUBLANES, w), np.float32)
    for h in range(FOX_HEADS):
        for r in range(2 * n):
            place[r, h, h * FOX_HEAD_DIM + r] = 1.0
        ones[0, h * FOX_HEAD_DIM + n:h * FOX_HEAD_DIM + 2 * n] = 1.0
        ones[1, h * FOX_HEAD_DIM:h * FOX_HEAD_DIM + n] = 1.0
    row = lambda b, ti: (b * nt + ti, 0)
    const2 = lambda b, ti: (0, 0)
    return pl.pallas_call(
        functools.partial(_fox_prep_kernel, tp=tp),
        grid=(bsz, nt),
        in_specs=[pl.BlockSpec((tp, w), lambda b, ti: (b * nt + ti, 0)),
                  pl.BlockSpec((tp, w), lambda b, ti: (b * nt + ti, 1)),
                  pl.BlockSpec((tp, w), lambda b, ti: (b * nt + ti, 2)),
                  pl.BlockSpec((tp, LANES), row),
                  pl.BlockSpec((1, LANES), const2),
                  pl.BlockSpec((tp, tp), const2),
                  pl.BlockSpec((2 * n, LANES, w), lambda b, ti: (0, 0, 0)),
                  pl.BlockSpec((SUBLANES, w), const2)],
        out_specs=[pl.BlockSpec((tp, FOX_HEADS * FOX_AUG), row),
                   pl.BlockSpec((tp, FOX_HEADS * FOX_AUG), row),
                   pl.BlockSpec((tp, w), row),
                   pl.BlockSpec((tp, FOX_HEADS, FOX_HEAD_DIM), lambda b, ti: (b * nt + ti, 0, 0)),
                   pl.BlockSpec((tp, FOX_HEADS, FOX_HEAD_DIM), lambda b, ti: (b * nt + ti, 0, 0)),
                   pl.BlockSpec((tp, LANES), row)],
        out_shape=[jax.ShapeDtypeStruct((m, FOX_HEADS * FOX_AUG), BF16),
                   jax.ShapeDtypeStruct((m, FOX_HEADS * FOX_AUG), BF16),
                   jax.ShapeDtypeStruct((m, w), BF16),
                   jax.ShapeDtypeStruct((m, FOX_HEADS, FOX_HEAD_DIM), F32),
                   jax.ShapeDtypeStruct((m, FOX_HEADS, FOX_HEAD_DIM), F32),
                   jax.ShapeDtypeStruct((m, LANES), F32)],
        scratch_shapes=[pltpu.VMEM((1, LANES), F32)],
        compiler_params=_cparams(2),
        name="fox_prep",
    )(proj, proj, proj, fl_proj, f_bias, tri, jnp.asarray(place, BF16), jnp.asarray(ones, F32))


def _fox_kernel(qi_ref, ki_ref, q_ref, k_ref, v_ref, o_ref, m_ref, acc_ref, va_ref, *, t, ts):
    step = pl.program_id(2)
    qi = qi_ref[step]
    ki = ki_ref[step]
    to_log2 = (FOX_HEAD_DIM ** -0.5) * math.log2(math.e)
    hd = FOX_HEAD_DIM

    @pl.when(ki == 0)
    def _():
        m_ref[...] = jnp.full_like(m_ref, NEG_INF)
        acc_ref[...] = jnp.zeros_like(acc_ref)
        va_ref[:, hd:2 * hd] = jnp.ones((t, hd), BF16)

    va_ref[:, 0:hd] = v_ref[...]

    def scores(rows, nk, r, diagonal):
        s = _dot_nt(q_ref[rows, :], k_ref[0:nk, :])
        if diagonal:
            ri = lax.broadcasted_iota(jnp.int32, (ts, nk), 0) + r * ts
            ci = lax.broadcasted_iota(jnp.int32, (ts, nk), 1)
            s = jnp.where(ci <= ri, s, NEG_INF)
        return s

    def update(diagonal):
        for r in range(t // ts):
            rows = slice(r * ts, (r + 1) * ts)
            nk = (r + 1) * ts if diagonal else t
            s = scores(rows, nk, r, diagonal)
            m_old = m_ref[rows, :]
            m_new = jnp.maximum(m_old, jnp.max(s, axis=1, keepdims=True))
            m_ref[rows, :] = m_new
            alpha = jnp.exp2((m_old - m_new) * to_log2)
            p = jnp.exp2((s - _widen(m_new, nk)) * to_log2)
            acc_ref[rows, :] = _widen(alpha, 2 * hd) * acc_ref[rows, :] + _dot(p.astype(BF16), va_ref[0:nk, :])

    @pl.when(ki < qi)
    def _():
        update(False)

    @pl.when(ki == qi)
    def _():
        update(True)
        o_ref[...] = (acc_ref[:, 0:hd] / acc_ref[:, hd:2 * hd]).astype(BF16)


def _fox_prompt(qa, ka, vb, *, bsz, seq_len, t, ts):
    m = qa.shape[0]
    nq = seq_len // t
    assert seq_len % t == 0 and t % ts == 0
    pairs = [(qi, ki) for qi in range(nq) for ki in range(qi + 1)]
    qi_tab = jnp.asarray([p[0] for p in pairs], jnp.int32)
    ki_tab = jnp.asarray([p[1] for p in pairs], jnp.int32)
    grid_spec = pltpu.PrefetchScalarGridSpec(
        num_scalar_prefetch=2,
        grid=(bsz, FOX_HEADS, len(pairs)),
        in_specs=[pl.BlockSpec((t, FOX_AUG), lambda b, h, s, qi, ki: (b * nq + qi[s], h)),
                  pl.BlockSpec((t, FOX_AUG), lambda b, h, s, qi, ki: (b * nq + ki[s], h)),
                  pl.BlockSpec((t, FOX_HEAD_DIM), lambda b, h, s, qi, ki: (b * nq + ki[s], h))],
        out_specs=pl.BlockSpec((t, FOX_HEAD_DIM), lambda b, h, s, qi, ki: (b * nq + qi[s], h)),
        scratch_shapes=[pltpu.VMEM((t, LANES), F32), pltpu.VMEM((t, 2 * FOX_HEAD_DIM), F32),
                        pltpu.VMEM((t, 2 * FOX_HEAD_DIM), BF16)],
    )
    return pl.pallas_call(
        functools.partial(_fox_kernel, t=t, ts=ts),
        grid_spec=grid_spec,
        out_shape=jax.ShapeDtypeStruct((m, FOX_WIDTH), BF16),
        compiler_params=_cparams(3),
        name="fox_attention",
    )(qi_tab, ki_tab, qa, ka, vb)


def _fox_decode_kernel(q_ref, kn_ref, vn_ref, kc_ref, vc_ref, cq_ref, ckc_ref, ckn_ref, o_ref,
                       m_ref, l_ref, acc_ref, *, lq, ncb):
    j = pl.program_id(1)
    nh, hd = FOX_HEADS, FOX_HEAD_DIM

    @pl.when(j == 0)
    def _():
        m_ref[...] = jnp.full_like(m_ref, NEG_INF)
        l_ref[...] = jnp.zeros_like(l_ref)
        acc_ref[...] = jnp.zeros_like(acc_ref)

    def attend(k_head, v_head, ck_head, causal):
        for h in range(nh):
            qh = q_ref[:, h * hd:(h + 1) * hd].astype(BF16)
            ck = ck_head(h)
            tk = ck.shape[1]
            s = _dot_nt(qh, k_head(h).astype(BF16)) * (hd ** -0.5)
            s = s + (_widen(cq_ref[h], tk) - ck)
            if causal:
                rows = lax.broadcasted_iota(jnp.int32, (lq, tk), 0)
                cols = lax.broadcasted_iota(jnp.int32, (lq, tk), 1)
                s = jnp.where(cols <= rows, s, NEG_INF)
            m_old = m_ref[h]
            m_new = jnp.maximum(m_old, jnp.max(s, axis=1, keepdims=True))
            alpha = jnp.exp(m_old - m_new)
            p = jnp.exp(s - _widen(m_new, tk))
            l_ref[h] = alpha * l_ref[h] + jnp.sum(p, axis=1, keepdims=True)
            acc_ref[h] = alpha * acc_ref[h] + _dot(p.astype(BF16), v_head(h).astype(BF16))
            m_ref[h] = m_new

    @pl.when(j < ncb)
    def _():
        k_hm = pltpu.einshape("mhd->hmd", kc_ref[0])
        v_hm = pltpu.einshape("mhd->hmd", vc_ref[0])
        attend(lambda h: k_hm[h], lambda h: v_hm[h], lambda h: ckc_ref[h], False)

    @pl.when(j == ncb)
    def _():
        attend(lambda h: kn_ref[:, h * hd:(h + 1) * hd], lambda h: vn_ref[:, h * hd:(h + 1) * hd],
               lambda h: ckn_ref[h][:, 0:lq], True)
        for h in range(nh):
            o_ref[:, h * hd:(h + 1) * hd] = (acc_ref[h] / l_ref[h]).astype(BF16)


def _fox_decode(proj, cache_k, cache_v, cq, ck_cache, ck_new, *, bsz, lq, tk):
    past = cache_k.shape[1]
    assert past % tk == 0 and tk % LANES == 0 and lq <= LANES
    ncb = past // tk
    nh, hd, w = FOX_HEADS, FOX_HEAD_DIM, FOX_WIDTH
    tile = lambda j: jnp.minimum(j, ncb - 1)
    cache_spec = pl.BlockSpec((1, tk, nh, hd), lambda b, j: (b, tile(j), 0, 0))
    return pl.pallas_call(
        functools.partial(_fox_decode_kernel, lq=lq, ncb=ncb),
        grid=(bsz, ncb + 1),
        in_specs=[pl.BlockSpec((lq, w), lambda b, j: (b, 0)),
                  pl.BlockSpec((lq, w), lambda b, j: (b, 1)),
                  pl.BlockSpec((lq, w), lambda b, j: (b, 2)),
                  cache_spec,
                  cache_spec,
                  pl.BlockSpec((nh, lq, LANES), lambda b, j: (b, 0, 0)),
                  pl.BlockSpec((nh, 1, tk), lambda b, j: (b, 0, tile(j))),
                  pl.BlockSpec((nh, 1, LANES), lambda b, j: (b, 0, 0))],
        out_specs=pl.BlockSpec((lq, w), lambda b, j: (b, 0)),
        out_shape=jax.ShapeDtypeStruct((bsz * lq, w), BF16),
        scratch_shapes=[pltpu.VMEM((nh, lq, LANES), F32), pltpu.VMEM((nh, lq, LANES), F32),
                        pltpu.VMEM((nh, lq, hd), F32)],
        compiler_params=_cparams(2),
        name="fox_decode",
    )(proj, proj, proj, cache_k, cache_v, cq, ck_cache, ck_new)


def _sconv_kernel(u_ref, bg_ref, cg_ref, cw_ref, hist_ref, y_ref, tail_ref, carry_ref, ext_ref,
                  *, tm, rows, spt, tpb):
    i = pl.program_id(0)
    w = cg_ref[...] * u_ref[...]
    first = (i % tpb) == 0
    for s in range(spt):
        w_s = w[s * rows:(s + 1) * rows]
        if tpb == 1:
            prev = hist_ref[s]
        else:
            prev = jnp.where(first, hist_ref[s], carry_ref[...])
        conv = _conv_rows(ext_ref, w_s, prev, cw_ref, SC_WIDTH, rows)
        y_ref[s * rows:(s + 1) * rows, :] = (bg_ref[s * rows:(s + 1) * rows, :] * conv).astype(BF16)
        tail_ref[s] = w_s[rows - SUBLANES:rows]
    if tpb > 1:
        carry_ref[...] = w[tm - SUBLANES:tm]


def _sconv(proj, conv_w, hist8, *, seq_len, tm):
    m = proj.shape[0]
    assert m % tm == 0
    rows, spt, tpb = _seq_tiling(seq_len, tm)
    nm = m // tm
    hist_map = (lambda i: (i // tpb, 0, 0)) if spt == 1 else (lambda i: (i, 0, 0))
    base = 3 * FOX_WIDTH // SC_DIM
    return pl.pallas_call(
        functools.partial(_sconv_kernel, tm=tm, rows=rows, spt=spt, tpb=tpb),
        grid=(nm,),
        in_specs=[pl.BlockSpec((tm, SC_DIM), lambda i: (i, base)),
                  pl.BlockSpec((tm, SC_DIM), lambda i: (i, base + 1)),
                  pl.BlockSpec((tm, SC_DIM), lambda i: (i, base + 2)),
                  pl.BlockSpec((SC_WIDTH, SC_DIM), lambda i: (0, 0)),
                  pl.BlockSpec((spt, SUBLANES, SC_DIM), hist_map)],
        out_specs=[pl.BlockSpec((tm, SC_DIM), lambda i: (i, 0)),
                   pl.BlockSpec((spt, SUBLANES, SC_DIM), lambda i: (i, 0, 0))],
        out_shape=[jax.ShapeDtypeStruct((m, SC_DIM), BF16),
                   jax.ShapeDtypeStruct((nm * spt, SUBLANES, SC_DIM), F32)],
        scratch_shapes=[pltpu.VMEM((SUBLANES, SC_DIM), F32),
                        pltpu.VMEM((rows + SUBLANES, SC_DIM), F32)],
        compiler_params=_cparams(1),
        name="gated_short_conv",
    )(proj, proj, proj, conv_w, hist8)


def _hist8(state):
    n, w1, c = state.shape
    return jnp.concatenate([jnp.zeros((n, SUBLANES - w1, c), F32), state.astype(F32)], axis=1)


def _tails(tails, n_seq, seq_len, tile_rows, keep):
    per_seq = max(1, seq_len // tile_rows)
    idx = (jnp.arange(n_seq) + 1) * per_seq - 1
    return tails[idx][:, SUBLANES - keep:, :]


def _rope_tables(pos0, length):
    half = RET_DK // 2
    inv = ROPE_BASE ** (-np.arange(half, dtype=np.float64) / half)
    ang = (pos0 + np.arange(length, dtype=np.float64))[:, None] * inv[None, :]
    cos, sin = np.cos(ang), np.sin(ang)
    return (jnp.asarray(np.concatenate([cos, cos], axis=1), F32),
            jnp.asarray(np.concatenate([-sin, sin], axis=1), F32))


def _prep_weights(p):
    d = D_MODEL
    ab_in = p['ab_w_in'][0]
    cd_in = p['cd_w_in'][0]
    f0 = 3 * FOX_WIDTH
    pad_cols = lambda w: jnp.pad(w, ((0, 0), (0, LANES - w.shape[1]))).astype(BF16)
    return dict(
        ab_in=ab_in.astype(BF16),
        ab_small=pad_cols(ab_in[:, AB_MAIN:]),
        cd_in=jnp.concatenate([cd_in[:, :f0], cd_in[:, f0 + FOX_HEADS:]], axis=1).astype(BF16),
        cd_small=pad_cols(cd_in[:, f0:f0 + FOX_HEADS]),
        ab_out=p['ab_w_out'][0].astype(BF16),
        cd_out=p['cd_w_out'][0].astype(BF16),
        ffn_gate=p['ffn_w_gate'].astype(BF16),
        ffn_up=p['ffn_w_up'].astype(BF16),
        ffn_down=p['ffn_w_down'].astype(BF16),
    )


def _trunk(x, pos0, st_ret, st_ssd, st_ssd_conv, c_k, c_v, c_logf, st_sconv, st_ffn, p, wb, t):
    bsz, length, d = x.shape
    m = bsz * length
    xf = x.reshape(m, d)
    zeros = lambda *shape: jnp.zeros(shape, F32)

    proj, dt_proj = _norm_matmul(xf, p['ab_norm_w'][0], wb['ab_in'], wb['ab_small'],
                                 n=AB_MAIN, tm=t['tm_ab'], tn=t['tn_ab'])
    cosf, sinf = _rope_tables(pos0, length)
    ret_state = zeros(bsz, RET_HEADS, RET_DK, RET_DV) if st_ret is None else st_ret
    y_ret, ret_new = _retention(proj, cosf, sinf, ret_state, p['ret_norm_w'][0],
                                bsz=bsz, seq_len=length, c=t['c_ret'])
    ssd_state = zeros(bsz, SSD_HEADS, SSD_HEADDIM, SSD_DSTATE) if st_ssd is None else st_ssd
    ssd_hist = zeros(bsz, SSD_CONV - 1, SSD_CONV_DIM) if st_ssd_conv is None else st_ssd_conv
    y_ssd, ssd_new = _ssd(proj, dt_proj, _hist8(ssd_hist), ssd_state, p['ssd_conv_w'][0], p['ssd_conv_b'][0],
                          p['ssd_dt_bias'][0], p['ssd_A_log'][0], p['ssd_D'][0], p['ssd_norm_w'][0],
                          bsz=bsz, seq_len=length, c=t['c_ssd'])
    xbc_lo = AB_MAIN - SSD_CONV_DIM
    ssd_conv_new = proj.reshape(bsz, length, -1)[:, length - (SSD_CONV - 1):, xbc_lo:AB_MAIN]
    xf = _proj_residual(xf, y_ret, y_ssd, wb['ab_out'], tm=t['tm_out'], tn=t['tn_out'])

    ffn_new = []
    ffn_hist0 = zeros(bsz, FFN_CONV - 1, D_FF) if st_ffn is None else st_ffn[0]
    xf, tails = _conv_ffn(xf, p['ffn_norm_w'][0], wb['ffn_gate'], wb['ffn_up'], wb['ffn_down'],
                          p['ffn_conv_w'][0], p['ffn_conv_b'][0], _hist8(ffn_hist0), p['final_norm_w'],
                          layer=0, seq_len=length, tm=t['tm_ffn'], tf=t['tf_ffn'], ts=t['ts_ffn'], final=False)
    ffn_new.append(_tails(tails, bsz, length, t['tm_ffn'], FFN_CONV - 1))

    proj, fl_proj = _norm_matmul(xf, p['cd_norm_w'][0], wb['cd_in'], wb['cd_small'],
                                 n=CD_MAIN, tm=t['tm_cd'], tn=t['tn_cd'])
    f_bias = jnp.pad(p['fox_f_bias'][0].astype(F32), (0, LANES - FOX_HEADS)).reshape(1, LANES)
    head_shape = (bsz, length, FOX_HEADS, FOX_HEAD_DIM)
    if c_k is None:
        qa, ka, vb, k32, v32, logf = _fox_prep(proj, fl_proj, f_bias, bsz=bsz, seq_len=length, tp=t['t_prep'])
        y_fox = _fox_prompt(qa, ka, vb, bsz=bsz, seq_len=length, t=t['t_fox'], ts=t['ts_fox'])
        logf_new = logf.reshape(bsz, length, LANES)[:, :, :FOX_HEADS]
        k_new, v_new = k32.reshape(head_shape), v32.reshape(head_shape)
    else:
        proj3 = proj.reshape(bsz, length, -1)
        k_new = proj3[:, :, FOX_WIDTH:2 * FOX_WIDTH].reshape(head_shape)
        v_new = proj3[:, :, 2 * FOX_WIDTH:3 * FOX_WIDTH].reshape(head_shape)
        past = c_k.shape[1]
        pairs = bsz * FOX_HEADS
        assert pairs <= LANES and length <= LANES
        to_lanes = lambda a, rows: jnp.pad(jnp.swapaxes(a, 0, 1).reshape(a.shape[1], pairs),
                                           ((0, rows - a.shape[1]), (0, LANES - pairs)))
        from_lanes = lambda a: jnp.swapaxes(a[:length, :pairs].reshape(length, bsz, FOX_HEADS), 0, 1)
        cache_lf = to_lanes(c_logf.astype(F32), past)
        fl_rows = to_lanes(fl_proj.reshape(bsz, length, LANES)[:, :, :FOX_HEADS], LANES)
        bias_lanes = jnp.pad(jnp.tile(p['fox_f_bias'][0].astype(F32), bsz), (0, LANES - pairs)).reshape(1, LANES)
        cum_t_cache, lf_rows, cum_rows, cum_t_new = _decode_cum(cache_lf, fl_rows, bias_lanes, c=t['c_cum'])
        logf_new = from_lanes(lf_rows)
        cq = jnp.broadcast_to(cum_rows[:length, :pairs].T[:, :, None], (pairs, length, LANES))
        y_fox = _fox_decode(proj, c_k, c_v, cq, cum_t_cache[:pairs, None, :], cum_t_new[:pairs, None, :],
                            bsz=bsz, lq=length, tk=t['tk_dec'])
    sc_hist = zeros(bsz, SC_WIDTH - 1, SC_DIM) if st_sconv is None else st_sconv
    y_sc, sc_tails = _sconv(proj, p['sconv_w'][0], _hist8(sc_hist), seq_len=length, tm=t['tm_sc'])
    sconv_new = _tails(sc_tails, bsz, length, t['tm_sc'], SC_WIDTH - 1)
    xf = _proj_residual(xf, y_fox, y_sc, wb['cd_out'], tm=t['tm_out'], tn=t['tn_out'])

    ffn_hist1 = zeros(bsz, FFN_CONV - 1, D_FF) if st_ffn is None else st_ffn[1]
    xf, tails = _conv_ffn(xf, p['ffn_norm_w'][1], wb['ffn_gate'], wb['ffn_up'], wb['ffn_down'],
                          p['ffn_conv_w'][1], p['ffn_conv_b'][1], _hist8(ffn_hist1), p['final_norm_w'],
                          layer=1, seq_len=length, tm=t['tm_ffn'], tf=t['tf_ffn'], ts=t['ts_ffn'], final=True)
    ffn_new.append(_tails(tails, bsz, length, t['tm_ffn'], FFN_CONV - 1))

    return (xf.reshape(bsz, length, d), ret_new[None], ssd_new[None], ssd_conv_new[None], k_new[None],
            v_new[None], logf_new[None], sconv_new[None], jnp.stack(ffn_new))


def _largest_divisor(n, cap, multiple=1):
    best = None
    for cand in range(multiple, min(n, cap) + 1, multiple):
        if n % cand == 0:
            best = cand
    assert best is not None, (n, cap, multiple)
    return best


def _tiles(bsz, length, past=None):
    m = bsz * length
    seq_tile = lambda cap: _largest_divisor(length, cap, SUBLANES)
    row_tile = lambda cap: (_largest_divisor(length, cap, SUBLANES) if length >= cap
                            else _largest_divisor(m, cap, length))
    t = dict(
        tm_ab=row_tile(1024), tn_ab=2048, tm_cd=row_tile(1024), tn_cd=2048,
        tm_out=row_tile(1024), tn_out=1024,
        tm_ffn=row_tile(1024), tf_ffn=512, ts_ffn=512,
        tm_sc=row_tile(512),
        c_ret=seq_tile(256), c_ssd=seq_tile(256),
    )
    if past is None:
        t['t_fox'] = seq_tile(2048)
        t['ts_fox'] = _largest_divisor(t['t_fox'], 256, LANES)
        t['t_prep'] = seq_tile(512)
    else:
        t['tk_dec'] = _largest_divisor(past, 1024, LANES)
        t['c_cum'] = _largest_divisor(past, 256, LANES)
    return t


def kernel(x_prompt, x_sample, state_ret, state_ssd, state_ssd_conv, cache_fox_k, cache_fox_v, cache_fox_logf, state_sconv, state_ffn_conv, ab_norm_w, ab_w_in, ret_norm_w, ssd_conv_w, ssd_conv_b, ssd_dt_bias, ssd_A_log, ssd_D, ssd_norm_w, ab_w_out, cd_norm_w, cd_w_in, fox_f_bias, sconv_w, cd_w_out, ffn_norm_w, ffn_w_gate, ffn_w_up, ffn_conv_w, ffn_conv_b, ffn_w_down, final_norm_w):
    p = dict(ab_norm_w=ab_norm_w, ab_w_in=ab_w_in, ret_norm_w=ret_norm_w, ssd_conv_w=ssd_conv_w,
             ssd_conv_b=ssd_conv_b, ssd_dt_bias=ssd_dt_bias, ssd_A_log=ssd_A_log, ssd_D=ssd_D,
             ssd_norm_w=ssd_norm_w, ab_w_out=ab_w_out, cd_norm_w=cd_norm_w, cd_w_in=cd_w_in,
             fox_f_bias=fox_f_bias, sconv_w=sconv_w, cd_w_out=cd_w_out, ffn_norm_w=ffn_norm_w,
             ffn_w_gate=ffn_w_gate, ffn_w_up=ffn_w_up, ffn_conv_w=ffn_conv_w, ffn_conv_b=ffn_conv_b,
             ffn_w_down=ffn_w_down, final_norm_w=final_norm_w)
    assert x_prompt.shape[-1] == D_MODEL and ab_w_in.shape == (1, D_MODEL, AB_MAIN + SSD_HEADS)
    assert cd_w_in.shape == (1, D_MODEL, CD_MAIN + FOX_HEADS) and ffn_w_gate.shape == (2, D_MODEL, D_FF)
    wb = _prep_weights(p)
    bp, lp_, _ = x_prompt.shape
    bs, ls, _ = x_sample.shape
    past = cache_fox_k.shape[2]
    (y_prompt, p_ret, p_ssd, p_ssd_conv, p_fox_k, p_fox_v, p_fox_logf, p_sconv, p_ffn_conv) = _trunk(
        x_prompt, 0, None, None, None, None, None, None, None, None, p, wb, _tiles(bp, lp_))
    (y_sample, s_ret, s_ssd, s_ssd_conv, s_fox_k, s_fox_v, s_fox_logf, s_sconv, s_ffn_conv) = _trunk(
        x_sample, past, state_ret[0], state_ssd[0], state_ssd_conv[0], cache_fox_k[0], cache_fox_v[0],
        cache_fox_logf[0], state_sconv[0], state_ffn_conv, p, wb, _tiles(bs, ls, past))
    return (y_prompt, y_sample, p_ret, s_ret, p_ssd, s_ssd, p_ssd_conv, s_ssd_conv, p_fox_k, s_fox_k,
            p_fox_v, s_fox_v, p_fox_logf, s_fox_logf, p_sconv, s_sconv, p_ffn_conv, s_ffn_conv)
```

```python
import functools
import math

import numpy as np
import jax
import jax.numpy as jnp
from jax import lax
from jax.experimental import pallas as pl
from jax.experimental.pallas import tpu as pltpu

F32 = jnp.float32
BF16 = jnp.bfloat16
EPS = 1e-6
ROPE_BASE = 10000.0
NEG_INF = float("-inf")

D_MODEL = 2048
RET_HEADS, RET_DK, RET_DV = 4, 128, 256
SSD_DINNER, SSD_HEADDIM, SSD_HEADS, SSD_GROUPS, SSD_DSTATE, SSD_CONV = 1024, 64, 16, 2, 128, 4
SSD_CONV_DIM = SSD_DINNER + 2 * SSD_GROUPS * SSD_DSTATE
FOX_HEADS, FOX_HEAD_DIM = 8, 128
FOX_WIDTH = FOX_HEADS * FOX_HEAD_DIM
SC_DIM, SC_WIDTH = 1024, 3
D_FF, FFN_CONV = 5632, 3
AB_MAIN = 2 * RET_HEADS * RET_DK + 2 * RET_HEADS * RET_DV + SSD_DINNER + SSD_CONV_DIM
AB_PAD = AB_MAIN + 128
CD_MAIN = 3 * FOX_WIDTH + 3 * SC_DIM
CD_PAD = CD_MAIN + 128

LANES = 128
SUBLANES = 8
VMEM_LIMIT = 60 * 1024 * 1024


def _cparams(n_axes):
    return pltpu.CompilerParams(dimension_semantics=("arbitrary",) * n_axes,
                                vmem_limit_bytes=VMEM_LIMIT)


def _rms(xf, w):
    return xf * lax.rsqrt(jnp.mean(xf * xf, axis=-1, keepdims=True) + EPS) * w


def _softplus(x):
    return jnp.maximum(x, 0.0) + jnp.log1p(jnp.exp(-jnp.abs(x)))


def _split3(x):
    hi = x.astype(BF16)
    r1 = x - hi.astype(F32)
    mid = r1.astype(BF16)
    lo = (r1 - mid.astype(F32)).astype(BF16)
    return hi, mid, lo


def _widen(x, n):
    return x[:, 0:n] if n <= LANES else jnp.concatenate([x] * (n // LANES), axis=1)


def _dot(a, b):
    return jnp.dot(a, b, preferred_element_type=F32)


def _dot_nt(a, b):
    return lax.dot_general(a, b, (((1,), (1,)), ((), ())), preferred_element_type=F32)


def _dot_tn(a, b):
    return lax.dot_general(a, b, (((0,), (0,)), ((), ())), preferred_element_type=F32)


def _exact_lhs_dot(m_bf16, x):
    hi, mid, lo = _split3(x)
    return _dot(m_bf16, hi) + _dot(m_bf16, mid) + _dot(m_bf16, lo)


def _exact_rhs_dot(x, m_bf16):
    hi, mid, lo = _split3(x)
    return _dot(hi, m_bf16) + _dot(mid, m_bf16) + _dot(lo, m_bf16)


def _conv_rows(ext_ref, x, prev8, w_ref, width, rows, w_cols=slice(None)):
    ext_ref[0:SUBLANES, :] = prev8
    ext_ref[SUBLANES:SUBLANES + rows, :] = x
    out = None
    for j in range(width):
        off = SUBLANES - (width - 1) + j
        term = ext_ref[off:off + rows, :] * w_ref[j:j + 1, w_cols]
        out = term if out is None else out + term
    return out


def _seq_tiling(seq_len, tile_rows):
    if seq_len >= tile_rows:
        assert seq_len % tile_rows == 0
        return tile_rows, 1, seq_len // tile_rows
    assert tile_rows % seq_len == 0 and seq_len % SUBLANES == 0
    return seq_len, tile_rows // seq_len, 1


def _norm_matmul_kernel(x_ref, nw_ref, w_ref, ws_ref, o_ref, os_ref, h_ref, *, n_tiles, tn, last_w):
    j = pl.program_id(1)

    @pl.when(j == 0)
    def _():
        h_ref[...] = _rms(x_ref[...], nw_ref[...]).astype(BF16)
        os_ref[...] = _dot(h_ref[...], ws_ref[...])

    if last_w == tn:
        o_ref[...] = _dot(h_ref[...], w_ref[...])
    else:
        @pl.when(j < n_tiles - 1)
        def _():
            o_ref[...] = _dot(h_ref[...], w_ref[...])

        @pl.when(j == n_tiles - 1)
        def _():
            o_ref[:, 0:last_w] = _dot(h_ref[...], w_ref[:, 0:last_w])


def _norm_matmul(x, norm_w, w, w_small, *, n, tm, tn):
    m, d = x.shape
    n_tiles = pl.cdiv(n, tn)
    last_w = n - (n_tiles - 1) * tn
    assert m % tm == 0 and last_w % LANES == 0 and n <= w.shape[1] and w_small.shape == (d, LANES)
    return pl.pallas_call(
        functools.partial(_norm_matmul_kernel, n_tiles=n_tiles, tn=tn, last_w=last_w),
        grid=(m // tm, n_tiles),
        in_specs=[pl.BlockSpec((tm, d), lambda i, j: (i, 0)),
                  pl.BlockSpec((1, d), lambda i, j: (0, 0)),
                  pl.BlockSpec((d, tn), lambda i, j: (0, j)),
                  pl.BlockSpec((d, LANES), lambda i, j: (0, 0))],
        out_specs=[pl.BlockSpec((tm, tn), lambda i, j: (i, j)),
                   pl.BlockSpec((tm, LANES), lambda i, j: (i, 0))],
        out_shape=[jax.ShapeDtypeStruct((m, n), F32),
                   jax.ShapeDtypeStruct((m, LANES), F32)],
        scratch_shapes=[pltpu.VMEM((tm, d), BF16)],
        compiler_params=_cparams(2),
        name="norm_in_proj",
    )(x, norm_w.reshape(1, d), w, w_small)


def _proj_res_kernel(x_ref, a_ref, b_ref, wa_ref, wb_ref, o_ref):
    acc = _dot(a_ref[...], wa_ref[...])
    acc = acc + _dot(b_ref[...], wb_ref[...])
    o_ref[...] = x_ref[...] + acc


def _proj_residual(x, a, b, w, *, tm, tn):
    m, d = x.shape
    ka, kb = a.shape[1], b.shape[1]
    assert m % tm == 0 and d % tn == 0 and ka == kb and w.shape == (ka + kb, d)
    return pl.pallas_call(
        _proj_res_kernel,
        grid=(m // tm, d // tn),
        in_specs=[pl.BlockSpec((tm, tn), lambda i, j: (i, j)),
                  pl.BlockSpec((tm, ka), lambda i, j: (i, 0)),
                  pl.BlockSpec((tm, kb), lambda i, j: (i, 0)),
                  pl.BlockSpec((ka, tn), lambda i, j: (0, j)),
                  pl.BlockSpec((kb, tn), lambda i, j: (1, j))],
        out_specs=pl.BlockSpec((tm, tn), lambda i, j: (i, j)),
        out_shape=jax.ShapeDtypeStruct((m, d), F32),
        compiler_params=_cparams(2),
        name="out_proj_residual",
    )(x, a, b, w, w)


def _ffn_kernel(x_ref, nw_ref, wg_ref, wu_ref, wd_ref, cw_ref, cb_ref, hist_ref, fw_ref,
                o_ref, tail_ref, h_ref, carry_ref, ext_ref,
                *, tm, rows, spt, tpb, nf, ts, nsub, nsub_last, final):
    i = pl.program_id(0)
    f = pl.program_id(1)

    @pl.when(f == 0)
    def _():
        xf = x_ref[...]
        h_ref[...] = _rms(xf, nw_ref[...]).astype(BF16)
        o_ref[...] = xf

    first = (i % tpb) == 0

    def sub_block(sb):
        cols = slice(sb * ts, (sb + 1) * ts)
        h = h_ref[...]
        a = _dot(h, wg_ref[:, cols])
        u = _dot(h, wu_ref[:, cols])
        convs = []
        for s in range(spt):
            a_s = a[s * rows:(s + 1) * rows]
            if tpb == 1:
                prev = hist_ref[s, :, cols]
            else:
                prev = jnp.where(first, hist_ref[s, :, cols], carry_ref[f * nsub + sb])
            convs.append(_conv_rows(ext_ref, a_s, prev, cw_ref, FFN_CONV, rows, cols))
            tail_ref[s, :, cols] = a_s[rows - SUBLANES:rows]
        if tpb > 1:
            carry_ref[f * nsub + sb] = a[tm - SUBLANES:tm]
        conv = convs[0] if spt == 1 else jnp.concatenate(convs, axis=0)
        act = (jax.nn.silu(conv + cb_ref[:, cols]) * u).astype(BF16)
        o_ref[...] += _dot(act, wd_ref[cols, :])

    if nsub_last == nsub:
        for sb in range(nsub):
            sub_block(sb)
    else:
        @pl.when(f < nf - 1)
        def _():
            for sb in range(nsub):
                sub_block(sb)

        @pl.when(f == nf - 1)
        def _():
            for sb in range(nsub_last):
                sub_block(sb)

    if final:
        @pl.when(f == nf - 1)
        def _():
            o_ref[...] = _rms(o_ref[...], fw_ref[...])


def _conv_ffn(x, norm_w, wg, wu, wd, conv_w, conv_b, hist8, final_w, *, layer, seq_len, tm, tf, ts, final):
    m, d = x.shape
    ff = wg.shape[2]
    assert m % tm == 0 and tf % ts == 0 and ff % ts == 0
    rows, spt, tpb = _seq_tiling(seq_len, tm)
    nm, nf = m // tm, pl.cdiv(ff, tf)
    nsub = tf // ts
    nsub_last = (ff - (nf - 1) * tf) // ts
    hist_map = (lambda i, f: (i // tpb, 0, f)) if spt == 1 else (lambda i, f: (i, 0, f))
    kern = functools.partial(_ffn_kernel, tm=tm, rows=rows, spt=spt, tpb=tpb, nf=nf, ts=ts, nsub=nsub,
                             nsub_last=nsub_last, final=final)
    out, tails = pl.pallas_call(
        kern,
        grid=(nm, nf),
        in_specs=[pl.BlockSpec((tm, d), lambda i, f: (i, 0)),
                  pl.BlockSpec((1, d), lambda i, f: (0, 0)),
                  pl.BlockSpec((None, d, tf), lambda i, f: (layer, 0, f)),
                  pl.BlockSpec((None, d, tf), lambda i, f: (layer, 0, f)),
                  pl.BlockSpec((None, tf, d), lambda i, f: (layer, f, 0)),
                  pl.BlockSpec((FFN_CONV, tf), lambda i, f: (0, f)),
                  pl.BlockSpec((1, tf), lambda i, f: (0, f)),
                  pl.BlockSpec((spt, SUBLANES, tf), hist_map),
                  pl.BlockSpec((1, d), lambda i, f: (0, 0))],
        out_specs=[pl.BlockSpec((tm, d), lambda i, f: (i, 0)),
                   pl.BlockSpec((spt, SUBLANES, tf), lambda i, f: (i, 0, f))],
        out_shape=[jax.ShapeDtypeStruct((m, d), F32),
                   jax.ShapeDtypeStruct((nm * spt, SUBLANES, ff), F32)],
        scratch_shapes=[pltpu.VMEM((tm, d), BF16),
                        pltpu.VMEM((nf * nsub, SUBLANES, ts), F32),
                        pltpu.VMEM((rows + SUBLANES, ts), F32)],
        compiler_params=_cparams(2),
        name="conv_ffn",
    )(x, norm_w.reshape(1, d), wg, wu, wd, conv_w, conv_b.reshape(1, ff), hist8, final_w.reshape(1, d))
    return out, tails


def _retention_kernel(q_ref, k_ref, v_ref, g_ref, cos_ref, sin_ref, st_ref, nw_ref,
                      y_ref, so_ref, *, c):
    ci = pl.program_id(1)

    @pl.when(ci == 0)
    def _():
        so_ref[...] = st_ref[...]

    cos = cos_ref[...]
    sin = sin_ref[...]
    ii = lax.broadcasted_iota(jnp.int32, (c, c), 0)
    jj = lax.broadcasted_iota(jnp.int32, (c, c), 1)
    diff = (ii - jj).astype(F32)
    causal = ii >= jj
    ridx = lax.broadcasted_iota(jnp.int32, (c, 1), 0).astype(F32)
    for h in range(RET_HEADS):
        lg = math.log1p(-(2.0 ** (-5.0 - h)))
        q = q_ref[:, h * RET_DK:(h + 1) * RET_DK]
        k = k_ref[:, h * RET_DK:(h + 1) * RET_DK]
        v = v_ref[:, h * RET_DV:(h + 1) * RET_DV]
        qr = q * cos + pltpu.roll(q, RET_DK // 2, 1) * sin
        kr = (k * cos + pltpu.roll(k, RET_DK // 2, 1) * sin) * (RET_DK ** -0.5)
        qb = qr.astype(BF16)
        kb = kr.astype(BF16)
        vb = v.astype(BF16)
        decay = jnp.exp(jnp.where(causal, diff * lg, NEG_INF))
        inner = jnp.exp((ridx + 1.0) * lg)
        sdecay = jnp.exp((c - 1.0 - ridx) * lg)
        s = so_ref[0, h]
        scores = _dot_nt(qb, kb) * decay
        y = _dot(scores.astype(BF16), vb)
        y = y + _dot(qb, s.astype(BF16)) * inner
        kd = (kr * sdecay).astype(BF16)
        so_ref[0, h] = math.exp(c * lg) * s + _dot_tn(kd, vb)
        mu = jnp.mean(y, axis=-1, keepdims=True)
        yc = y - mu
        var = jnp.mean(yc * yc, axis=-1, keepdims=True)
        yn = yc * lax.rsqrt(var + EPS) * nw_ref[:, h * RET_DV:(h + 1) * RET_DV]
        g = g_ref[:, h * RET_DV:(h + 1) * RET_DV]
        y_ref[:, h * RET_DV:(h + 1) * RET_DV] = (jax.nn.silu(g) * yn).astype(BF16)


def _retention(proj, cosf, sinf, state, norm_w, *, bsz, seq_len, c):
    m = proj.shape[0]
    nc = seq_len // c
    assert seq_len % c == 0
    qk_w = RET_HEADS * RET_DK
    v_w = RET_HEADS * RET_DV
    row = lambda b, ci: b * nc + ci
    y, s_new = pl.pallas_call(
        functools.partial(_retention_kernel, c=c),
        grid=(bsz, nc),
        in_specs=[pl.BlockSpec((c, qk_w), lambda b, ci: (row(b, ci), 0)),
                  pl.BlockSpec((c, qk_w), lambda b, ci: (row(b, ci), 1)),
                  pl.BlockSpec((c, v_w), lambda b, ci: (row(b, ci), 1)),
                  pl.BlockSpec((c, v_w), lambda b, ci: (row(b, ci), 2)),
                  pl.BlockSpec((c, RET_DK), lambda b, ci: (ci, 0)),
                  pl.BlockSpec((c, RET_DK), lambda b, ci: (ci, 0)),
                  pl.BlockSpec((1, RET_HEADS, RET_DK, RET_DV), lambda b, ci: (b, 0, 0, 0)),
                  pl.BlockSpec((1, v_w), lambda b, ci: (0, 0))],
        out_specs=[pl.BlockSpec((c, v_w), lambda b, ci: (row(b, ci), 0)),
                   pl.BlockSpec((1, RET_HEADS, RET_DK, RET_DV), lambda b, ci: (b, 0, 0, 0))],
        out_shape=[jax.ShapeDtypeStruct((m, v_w), BF16),
                   jax.ShapeDtypeStruct(state.shape, F32)],
        compiler_params=_cparams(2),
        name="retention",
    )(proj, proj, proj, proj, cosf, sinf, state, norm_w.reshape(1, v_w))
    return y, s_new


def _ssd_kernel(z_ref, xs_ref, bc_ref, dt_ref, hx_ref, hbc_ref, st_ref,
                cwx_ref, cwbc_ref, cbx_ref, cbbc_ref, dtb_ref, alog_ref, dsk_ref, nw_ref,
                tri_ref, exp_ref,
                y_ref, so_ref,
                st_scr, cx_scr, cbc_scr, extx_scr, extbc_scr, yh_scr, xs_scr, *, c, nc):
    ci = pl.program_id(1)
    gw = SSD_DINNER // SSD_GROUPS
    hpg = SSD_HEADS // SSD_GROUPS

    @pl.when(ci == 0)
    def _():
        st_scr[...] = st_ref[0].T
        cx_scr[...] = hx_ref[0]
        cbc_scr[...] = hbc_ref[0]

    xs_raw = xs_ref[...]
    bc_raw = bc_ref[...]
    xs_scr[...] = jax.nn.silu(_conv_rows(extx_scr, xs_raw, cx_scr[...], cwx_ref, SSD_CONV, c) + cbx_ref[...])
    bcm = jax.nn.silu(_conv_rows(extbc_scr, bc_raw, cbc_scr[...], cwbc_ref, SSD_CONV, c) + cbbc_ref[...])
    cx_scr[...] = xs_raw[c - SUBLANES:c]
    cbc_scr[...] = bc_raw[c - SUBLANES:c]

    tri = tri_ref[...]
    dt = _softplus(dt_ref[...] + dtb_ref[...])
    a = -jnp.exp(alog_ref[...])
    acs = _exact_lhs_dot(tri, dt * a)
    acs_t = acs.T
    acs_last = acs[c - 1:c, :]
    exp_acs = jnp.exp(acs)
    to_end = jnp.exp(acs_last - acs)
    chunk_dec = jnp.exp(acs_last)

    ii = lax.broadcasted_iota(jnp.int32, (c, c), 0)
    jj = lax.broadcasted_iota(jnp.int32, (c, c), 1)
    causal = ii >= jj
    nb = SSD_GROUPS * SSD_DSTATE
    for g in range(SSD_GROUPS):
        cols = slice(g * gw, (g + 1) * gw)
        expand = exp_ref[:, cols]
        xdt = xs_scr[:, cols] * _exact_rhs_dot(dt, expand)
        xdt_b = xdt.astype(BF16)
        xend_b = (xdt * _exact_rhs_dot(to_end, expand)).astype(BF16)
        b_g = bcm[:, g * SSD_DSTATE:(g + 1) * SSD_DSTATE].astype(BF16)
        c_g = bcm[:, nb + g * SSD_DSTATE:nb + (g + 1) * SSD_DSTATE].astype(BF16)
        cb = _dot_nt(c_g, b_g)
        s_g = st_scr[:, cols]
        y_state = _dot(c_g, s_g.astype(BF16)) * _exact_rhs_dot(exp_acs, expand)
        for r in range(hpg):
            hh = g * hpg + r
            seg = acs[:, hh:hh + 1] - acs_t[hh:hh + 1, :]
            lmat = jnp.exp(jnp.where(causal, seg, NEG_INF))
            mm = (cb * lmat).astype(BF16)
            head = slice(r * SSD_HEADDIM, (r + 1) * SSD_HEADDIM)
            yh_scr[:, hh * SSD_HEADDIM:(hh + 1) * SSD_HEADDIM] = _dot(mm, xdt_b[:, head]) + y_state[:, head]
        upd = _dot_tn(b_g, xend_b)
        st_scr[:, cols] = _exact_rhs_dot(chunk_dec, expand) * s_g + upd

    y = yh_scr[...] + dsk_ref[...] * xs_scr[...]
    z = z_ref[...]
    y_ref[...] = _rms(y * jax.nn.silu(z), nw_ref[...]).astype(BF16)

    @pl.when(ci == nc - 1)
    def _():
        so_ref[0] = st_scr[...].T


def _ssd(proj, dt_proj, hist8, state, conv_w, conv_b, dt_bias, a_log, d_skip, norm_w, *, bsz, seq_len, c):
    m = proj.shape[0]
    nc = seq_len // c
    assert seq_len % c == 0
    row = lambda b, ci: b * nc + ci
    const2 = lambda b, ci: (0, 0)
    di, bcw = SSD_DINNER, 2 * SSD_GROUPS * SSD_DSTATE
    tri = jnp.asarray(np.tril(np.ones((c, c), np.float32)), BF16)
    expand = np.zeros((LANES, di), np.float32)
    for h in range(SSD_HEADS):
        expand[h, h * SSD_HEADDIM:(h + 1) * SSD_HEADDIM] = 1.0
    expand = jnp.asarray(expand, BF16)
    pad_row = lambda v: jnp.pad(v.astype(F32), (0, LANES - v.shape[0])).reshape(1, LANES)
    st2 = state.reshape(bsz, di, SSD_DSTATE)
    y, s_new = pl.pallas_call(
        functools.partial(_ssd_kernel, c=c, nc=nc),
        grid=(bsz, nc),
        in_specs=[pl.BlockSpec((c, di), lambda b, ci: (row(b, ci), 3)),
                  pl.BlockSpec((c, di), lambda b, ci: (row(b, ci), 4)),
                  pl.BlockSpec((c, bcw), lambda b, ci: (row(b, ci), 10)),
                  pl.BlockSpec((c, LANES), lambda b, ci: (row(b, ci), 0)),
                  pl.BlockSpec((1, SUBLANES, di), lambda b, ci: (b, 0, 0)),
                  pl.BlockSpec((1, SUBLANES, bcw), lambda b, ci: (b, 0, 2)),
                  pl.BlockSpec((1, di, SSD_DSTATE), lambda b, ci: (b, 0, 0)),
                  pl.BlockSpec((SSD_CONV, di), const2),
                  pl.BlockSpec((SSD_CONV, bcw), lambda b, ci: (0, 2)),
                  pl.BlockSpec((1, di), const2),
                  pl.BlockSpec((1, bcw), lambda b, ci: (0, 2)),
                  pl.BlockSpec((1, LANES), const2),
                  pl.BlockSpec((1, LANES), const2),
                  pl.BlockSpec((1, di), const2),
                  pl.BlockSpec((1, di), const2),
                  pl.BlockSpec((c, c), const2),
                  pl.BlockSpec((LANES, di), const2)],
        out_specs=[pl.BlockSpec((c, di), lambda b, ci: (row(b, ci), 0)),
                   pl.BlockSpec((1, di, SSD_DSTATE), lambda b, ci: (b, 0, 0))],
        out_shape=[jax.ShapeDtypeStruct((m, di), BF16),
                   jax.ShapeDtypeStruct(st2.shape, F32)],
        scratch_shapes=[pltpu.VMEM((SSD_DSTATE, di), F32),
                        pltpu.VMEM((SUBLANES, di), F32),
                        pltpu.VMEM((SUBLANES, bcw), F32),
                        pltpu.VMEM((c + SUBLANES, di), F32),
                        pltpu.VMEM((c + SUBLANES, bcw), F32),
                        pltpu.VMEM((c, di), F32),
                        pltpu.VMEM((c, di), F32)],
        compiler_params=_cparams(2),
        name="ssd",
    )(proj, proj, proj, dt_proj, hist8, hist8, st2,
      conv_w, conv_w, conv_b.reshape(1, -1), conv_b.reshape(1, -1),
      pad_row(dt_bias), pad_row(a_log), jnp.repeat(d_skip.astype(F32), SSD_HEADDIM).reshape(1, di),
      norm_w.reshape(1, di), tri, expand)
    return y, s_new.reshape(state.shape)


def _decode_cum_kernel(lfc_ref, fl_ref, b_ref, tri_ref, cumt_c_ref, lfn_ref, cumn_ref, cumt_n_ref,
                       carry_ref, *, c, ncb):
    j = pl.program_id(0)

    @pl.when(j == 0)
    def _():
        carry_ref[...] = jnp.zeros_like(carry_ref)

    @pl.when(j < ncb)
    def _():
        cum = _exact_lhs_dot(tri_ref[...], lfc_ref[...]) + carry_ref[...]
        carry_ref[...] = cum[c - 1:c, :]
        cumt_c_ref[...] = cum.T

    @pl.when(j == ncb)
    def _():
        lf = -_softplus(-(fl_ref[...] + b_ref[...]))
        lfn_ref[...] = lf
        cum = _exact_lhs_dot(tri_ref[0:LANES, 0:LANES], lf) + carry_ref[...]
        cumn_ref[...] = cum
        cumt_n_ref[...] = cum.T


def _decode_cum(cache_lf, fl_new, bias, *, c):
    past = cache_lf.shape[0]
    assert past % c == 0 and c % LANES == 0 and fl_new.shape == (LANES, LANES)
    ncb = past // c
    tri = jnp.asarray(np.tril(np.ones((c, c), np.float32)), BF16)
    blk = lambda j: jnp.minimum(j, ncb - 1)
    sq = jax.ShapeDtypeStruct((LANES, LANES), F32)
    return pl.pallas_call(
        functools.partial(_decode_cum_kernel, c=c, ncb=ncb),
        grid=(ncb + 1,),
        in_specs=[pl.BlockSpec((c, LANES), lambda j: (blk(j), 0)),
                  pl.BlockSpec((LANES, LANES), lambda j: (0, 0)),
                  pl.BlockSpec((1, LANES), lambda j: (0, 0)),
                  pl.BlockSpec((c, c), lambda j: (0, 0))],
        out_specs=[pl.BlockSpec((LANES, c), lambda j: (0, blk(j))),
                   pl.BlockSpec((LANES, LANES), lambda j: (0, 0)),
                   pl.BlockSpec((LANES, LANES), lambda j: (0, 0)),
                   pl.BlockSpec((LANES, LANES), lambda j: (0, 0))],
        out_shape=[jax.ShapeDtypeStruct((LANES, past), F32), sq, sq, sq],
        scratch_shapes=[pltpu.VMEM((1, LANES), F32)],
        compiler_params=_cparams(1),
        name="decode_logf_cumsum",
    )(cache_lf, fl_new, bias, tri)


FOX_AUG = 2 * FOX_HEAD_DIM
N_BIAS_PIECES = 3


def _fox_prep_kernel(q_ref, k_ref, v_ref, fl_ref, fb_ref, tri_ref, place_ref, ones_ref,
                     qa_ref, ka_ref, vb_ref, k32_ref, v32_ref, lf_ref, carry_ref, *, tp):
    @pl.when(pl.program_id(1) == 0)
    def _():
        carry_ref[...] = jnp.zeros_like(carry_ref)

    lf = -_softplus(-(fl_ref[...] + fb_ref[...]))
    lf_ref[...] = lf
    cum = _exact_lhs_dot(tri_ref[...], lf) + carry_ref[...]
    carry_ref[...] = cum[tp - 1:tp, :]
    pieces = _split3(cum * (FOX_HEAD_DIM ** 0.5))
    n = N_BIAS_PIECES
    aug_q = ones_ref[0:1, :] + sum(_dot(pieces[r], place_ref[r]) for r in range(n))
    aug_k = ones_ref[1:2, :] - sum(_dot(pieces[r], place_ref[n + r]) for r in range(n))
    for h in range(FOX_HEADS):
        src = slice(h * FOX_HEAD_DIM, (h + 1) * FOX_HEAD_DIM)
        feat = slice(h * FOX_AUG, h * FOX_AUG + FOX_HEAD_DIM)
        bias = slice(h * FOX_AUG + FOX_HEAD_DIM, (h + 1) * FOX_AUG)
        qa_ref[:, feat] = q_ref[:, src].astype(BF16)
        qa_ref[:, bias] = aug_q[:, src].astype(BF16)
        ka_ref[:, feat] = k_ref[:, src].astype(BF16)
        ka_ref[:, bias] = aug_k[:, src].astype(BF16)
    k = k_ref[...]
    v = v_ref[...]
    k32_ref[...] = pltpu.einshape("m(hd)->mhd", k, h=FOX_HEADS)
    v32_ref[...] = pltpu.einshape("m(hd)->mhd", v, h=FOX_HEADS)
    vb_ref[...] = v.astype(BF16)


def _fox_prep(proj, fl_proj, f_bias, *, bsz, seq_len, tp):
    m = proj.shape[0]
    nt = seq_len // tp
    assert seq_len % tp == 0
    w = FOX_WIDTH
    tri = jnp.asarray(np.tril(np.ones((tp, tp), np.float32)), BF16)
    n = N_BIAS_PIECES
    place = np.zeros((2 * n, LANES, w), np.float32)
    ones = np.zeros((SUBLANES, w), np.float32)
    for h in range(FOX_HEADS):
        for r in range(2 * n):
            place[r, h, h * FOX_HEAD_DIM + r] = 1.0
        ones[0, h * FOX_HEAD_DIM + n:h * FOX_HEAD_DIM + 2 * n] = 1.0
        ones[1, h * FOX_HEAD_DIM:h * FOX_HEAD_DIM + n] = 1.0
    row = lambda b, ti: (b * nt + ti, 0)
    const2 = lambda b, ti: (0, 0)
    return pl.pallas_call(
        functools.partial(_fox_prep_kernel, tp=tp),
        grid=(bsz, nt),
        in_specs=[pl.BlockSpec((tp, w), lambda b, ti: (b * nt + ti, 0)),
                  pl.BlockSpec((tp, w), lambda b, ti: (b * nt + ti, 1)),
                  pl.BlockSpec((tp, w), lambda b, ti: (b * nt + ti, 2)),
                  pl.BlockSpec((tp, LANES), row),
                  pl.BlockSpec((1, LANES), const2),
                  pl.BlockSpec((tp, tp), const2),
                  pl.BlockSpec((2 * n, LANES, w), lambda b, ti: (0, 0, 0)),
                  pl.BlockSpec((SUBLANES, w), const2)],
        out_specs=[pl.BlockSpec((tp, FOX_HEADS * FOX_AUG), row),
                   pl.BlockSpec((tp, FOX_HEADS * FOX_AUG), row),
                   pl.BlockSpec((tp, w), row),
                   pl.BlockSpec((tp, FOX_HEADS, FOX_HEAD_DIM), lambda b, ti: (b * nt + ti, 0, 0)),
                   pl.BlockSpec((tp, FOX_HEADS, FOX_HEAD_DIM), lambda b, ti: (b * nt + ti, 0, 0)),
                   pl.BlockSpec((tp, LANES), row)],
        out_shape=[jax.ShapeDtypeStruct((m, FOX_HEADS * FOX_AUG), BF16),
                   jax.ShapeDtypeStruct((m, FOX_HEADS * FOX_AUG), BF16),
                   jax.ShapeDtypeStruct((m, w), BF16),
                   jax.ShapeDtypeStruct((m, FOX_HEADS, FOX_HEAD_DIM), F32),
                   jax.ShapeDtypeStruct((m, FOX_HEADS, FOX_HEAD_DIM), F32),
                   jax.ShapeDtypeStruct((m, LANES), F32)],
        scratch_shapes=[pltpu.VMEM((1, LANES), F32)],
        compiler_params=_cparams(2),
        name="fox_prep",
    )(proj, proj, proj, fl_proj, f_bias, tri, jnp.asarray(place, BF16), jnp.asarray(ones, F32))


def _fox_kernel(qi_ref, ki_ref, q_ref, k_ref, v_ref, o_ref, m_ref, acc_ref, va_ref, *, t, ts, ts_diag):
    step = pl.program_id(2)
    qi = qi_ref[step]
    ki = ki_ref[step]
    to_log2 = (FOX_HEAD_DIM ** -0.5) * math.log2(math.e)
    hd = FOX_HEAD_DIM

    @pl.when(ki == 0)
    def _():
        m_ref[...] = jnp.full_like(m_ref, NEG_INF)
        acc_ref[...] = jnp.zeros_like(acc_ref)
        va_ref[:, hd:2 * hd] = jnp.ones((t, hd), BF16)

    va_ref[:, 0:hd] = v_ref[...]

    def update(diagonal):
        tc = ts_diag if diagonal else ts
        for r in range(t // tc):
            rows = slice(r * tc, (r + 1) * tc)
            nk = (r + 1) * tc if diagonal else t
            s = _dot_nt(q_ref[rows, :], k_ref[0:nk, :])
            if diagonal:
                ri = lax.broadcasted_iota(jnp.int32, (tc, nk), 0) + r * tc
                ci = lax.broadcasted_iota(jnp.int32, (tc, nk), 1)
                s = jnp.where(ci <= ri, s, NEG_INF)
            m_old = m_ref[rows, :]
            m_new = jnp.maximum(m_old, jnp.max(s, axis=1, keepdims=True))
            m_ref[rows, :] = m_new
            alpha = jnp.exp2((m_old - m_new) * to_log2)
            p = jnp.exp2((s - _widen(m_new, nk)) * to_log2)
            acc_ref[rows, :] = _widen(alpha, 2 * hd) * acc_ref[rows, :] + _dot(p.astype(BF16), va_ref[0:nk, :])

    @pl.when(ki < qi)
    def _():
        update(False)

    @pl.when(ki == qi)
    def _():
        update(True)
        o_ref[...] = (acc_ref[:, 0:hd] / acc_ref[:, hd:2 * hd]).astype(BF16)


def _fox_prompt(qa, ka, vb, *, bsz, seq_len, t, ts, ts_diag):
    m = qa.shape[0]
    nq = seq_len // t
    assert seq_len % t == 0 and t % ts == 0 and t % ts_diag == 0
    pairs = [(qi, ki) for qi in range(nq) for ki in range(qi + 1)]
    qi_tab = jnp.asarray([p[0] for p in pairs], jnp.int32)
    ki_tab = jnp.asarray([p[1] for p in pairs], jnp.int32)
    grid_spec = pltpu.PrefetchScalarGridSpec(
        num_scalar_prefetch=2,
        grid=(bsz, FOX_HEADS, len(pairs)),
        in_specs=[pl.BlockSpec((t, FOX_AUG), lambda b, h, s, qi, ki: (b * nq + qi[s], h)),
                  pl.BlockSpec((t, FOX_AUG), lambda b, h, s, qi, ki: (b * nq + ki[s], h)),
                  pl.BlockSpec((t, FOX_HEAD_DIM), lambda b, h, s, qi, ki: (b * nq + ki[s], h))],
        out_specs=pl.BlockSpec((t, FOX_HEAD_DIM), lambda b, h, s, qi, ki: (b * nq + qi[s], h)),
        scratch_shapes=[pltpu.VMEM((t, LANES), F32), pltpu.VMEM((t, 2 * FOX_HEAD_DIM), F32),
                        pltpu.VMEM((t, 2 * FOX_HEAD_DIM), BF16)],
    )
    return pl.pallas_call(
        functools.partial(_fox_kernel, t=t, ts=ts, ts_diag=ts_diag),
        grid_spec=grid_spec,
        out_shape=jax.ShapeDtypeStruct((m, FOX_WIDTH), BF16),
        compiler_params=_cparams(3),
        name="fox_attention",
    )(qi_tab, ki_tab, qa, ka, vb)


def _fox_decode_kernel(q_ref, kn_ref, vn_ref, kc_ref, vc_ref, cq_ref, ckc_ref, ckn_ref, o_ref,
                       m_ref, l_ref, acc_ref, *, lq, ncb):
    j = pl.program_id(1)
    nh, hd = FOX_HEADS, FOX_HEAD_DIM

    @pl.when(j == 0)
    def _():
        m_ref[...] = jnp.full_like(m_ref, NEG_INF)
        l_ref[...] = jnp.zeros_like(l_ref)
        acc_ref[...] = jnp.zeros_like(acc_ref)

    def attend(k_head, v_head, ck_head, causal):
        for h in range(nh):
            qh = q_ref[:, h * hd:(h + 1) * hd].astype(BF16)
            ck = ck_head(h)
            tk = ck.shape[1]
            s = _dot_nt(qh, k_head(h).astype(BF16)) * (hd ** -0.5)
            s = s + (_widen(cq_ref[h], tk) - ck)
            if causal:
                rows = lax.broadcasted_iota(jnp.int32, (lq, tk), 0)
                cols = lax.broadcasted_iota(jnp.int32, (lq, tk), 1)
                s = jnp.where(cols <= rows, s, NEG_INF)
            m_old = m_ref[h]
            m_new = jnp.maximum(m_old, jnp.max(s, axis=1, keepdims=True))
            alpha = jnp.exp(m_old - m_new)
            p = jnp.exp(s - _widen(m_new, tk))
            l_ref[h] = alpha * l_ref[h] + jnp.sum(p, axis=1, keepdims=True)
            acc_ref[h] = alpha * acc_ref[h] + _dot(p.astype(BF16), v_head(h).astype(BF16))
            m_ref[h] = m_new

    @pl.when(j < ncb)
    def _():
        k_hm = pltpu.einshape("mhd->hmd", kc_ref[0])
        v_hm = pltpu.einshape("mhd->hmd", vc_ref[0])
        attend(lambda h: k_hm[h], lambda h: v_hm[h], lambda h: ckc_ref[h], False)

    @pl.when(j == ncb)
    def _():
        attend(lambda h: kn_ref[:, h * hd:(h + 1) * hd], lambda h: vn_ref[:, h * hd:(h + 1) * hd],
               lambda h: ckn_ref[h][:, 0:lq], True)
        for h in range(nh):
            o_ref[:, h * hd:(h + 1) * hd] = (acc_ref[h] / l_ref[h]).astype(BF16)


def _fox_decode(proj, cache_k, cache_v, cq, ck_cache, ck_new, *, bsz, lq, tk):
    past = cache_k.shape[1]
    assert past % tk == 0 and tk % LANES == 0 and lq <= LANES
    ncb = past // tk
    nh, hd, w = FOX_HEADS, FOX_HEAD_DIM, FOX_WIDTH
    tile = lambda j: jnp.minimum(j, ncb - 1)
    cache_spec = pl.BlockSpec((1, tk, nh, hd), lambda b, j: (b, tile(j), 0, 0))
    return pl.pallas_call(
        functools.partial(_fox_decode_kernel, lq=lq, ncb=ncb),
        grid=(bsz, ncb + 1),
        in_specs=[pl.BlockSpec((lq, w), lambda b, j: (b, 0)),
                  pl.BlockSpec((lq, w), lambda b, j: (b, 1)),
                  pl.BlockSpec((lq, w), lambda b, j: (b, 2)),
                  cache_spec,
                  cache_spec,
                  pl.BlockSpec((nh, lq, LANES), lambda b, j: (b, 0, 0)),
                  pl.BlockSpec((nh, 1, tk), lambda b, j: (b, 0, tile(j))),
                  pl.BlockSpec((nh, 1, LANES), lambda b, j: (b, 0, 0))],
        out_specs=pl.BlockSpec((lq, w), lambda b, j: (b, 0)),
        out_shape=jax.ShapeDtypeStruct((bsz * lq, w), BF16),
        scratch_shapes=[pltpu.VMEM((nh, lq, LANES), F32), pltpu.VMEM((nh, lq, LANES), F32),
                        pltpu.VMEM((nh, lq, hd), F32)],
        compiler_params=_cparams(2),
        name="fox_decode",
    )(proj, proj, proj, cache_k, cache_v, cq, ck_cache, ck_new)


def _sconv_kernel(u_ref, bg_ref, cg_ref, cw_ref, hist_ref, y_ref, tail_ref, carry_ref, ext_ref,
                  *, tm, rows, spt, tpb):
    i = pl.program_id(0)
    w = cg_ref[...] * u_ref[...]
    first = (i % tpb) == 0
    for s in range(spt):
        w_s = w[s * rows:(s + 1) * rows]
        if tpb == 1:
            prev = hist_ref[s]
        else:
            prev = jnp.where(first, hist_ref[s], carry_ref[...])
        conv = _conv_rows(ext_ref, w_s, prev, cw_ref, SC_WIDTH, rows)
        y_ref[s * rows:(s + 1) * rows, :] = (bg_ref[s * rows:(s + 1) * rows, :] * conv).astype(BF16)
        tail_ref[s] = w_s[rows - SUBLANES:rows]
    if tpb > 1:
        carry_ref[...] = w[tm - SUBLANES:tm]


def _sconv(proj, conv_w, hist8, *, seq_len, tm):
    m = proj.shape[0]
    assert m % tm == 0
    rows, spt, tpb = _seq_tiling(seq_len, tm)
    nm = m // tm
    hist_map = (lambda i: (i // tpb, 0, 0)) if spt == 1 else (lambda i: (i, 0, 0))
    base = 3 * FOX_WIDTH // SC_DIM
    return pl.pallas_call(
        functools.partial(_sconv_kernel, tm=tm, rows=rows, spt=spt, tpb=tpb),
        grid=(nm,),
        in_specs=[pl.BlockSpec((tm, SC_DIM), lambda i: (i, base)),
                  pl.BlockSpec((tm, SC_DIM), lambda i: (i, base + 1)),
                  pl.BlockSpec((tm, SC_DIM), lambda i: (i, base + 2)),
                  pl.BlockSpec((SC_WIDTH, SC_DIM), lambda i: (0, 0)),
                  pl.BlockSpec((spt, SUBLANES, SC_DIM), hist_map)],
        out_specs=[pl.BlockSpec((tm, SC_DIM), lambda i: (i, 0)),
                   pl.BlockSpec((spt, SUBLANES, SC_DIM), lambda i: (i, 0, 0))],
        out_shape=[jax.ShapeDtypeStruct((m, SC_DIM), BF16),
                   jax.ShapeDtypeStruct((nm * spt, SUBLANES, SC_DIM), F32)],
        scratch_shapes=[pltpu.VMEM((SUBLANES, SC_DIM), F32),
                        pltpu.VMEM((rows + SUBLANES, SC_DIM), F32)],
        compiler_params=_cparams(1),
        name="gated_short_conv",
    )(proj, proj, proj, conv_w, hist8)


def _hist8(state):
    n, w1, c = state.shape
    return jnp.concatenate([jnp.zeros((n, SUBLANES - w1, c), F32), state.astype(F32)], axis=1)


def _tails(tails, n_seq, seq_len, tile_rows, keep):
    per_seq = max(1, seq_len // tile_rows)
    idx = (jnp.arange(n_seq) + 1) * per_seq - 1
    return tails[idx][:, SUBLANES - keep:, :]


def _rope_tables(pos0, length):
    half = RET_DK // 2
    inv = ROPE_BASE ** (-np.arange(half, dtype=np.float64) / half)
    ang = (pos0 + np.arange(length, dtype=np.float64))[:, None] * inv[None, :]
    cos, sin = np.cos(ang), np.sin(ang)
    return (jnp.asarray(np.concatenate([cos, cos], axis=1), F32),
            jnp.asarray(np.concatenate([-sin, sin], axis=1), F32))


def _prep_weights(p):
    d = D_MODEL
    ab_in = p['ab_w_in'][0]
    cd_in = p['cd_w_in'][0]
    f0 = 3 * FOX_WIDTH
    pad_cols = lambda w: jnp.pad(w, ((0, 0), (0, LANES - w.shape[1]))).astype(BF16)
    return dict(
        ab_in=ab_in.astype(BF16),
        ab_small=pad_cols(ab_in[:, AB_MAIN:]),
        cd_in=jnp.concatenate([cd_in[:, :f0], cd_in[:, f0 + FOX_HEADS:]], axis=1).astype(BF16),
        cd_small=pad_cols(cd_in[:, f0:f0 + FOX_HEADS]),
        ab_out=p['ab_w_out'][0].astype(BF16),
        cd_out=p['cd_w_out'][0].astype(BF16),
        ffn_gate=p['ffn_w_gate'].astype(BF16),
        ffn_up=p['ffn_w_up'].astype(BF16),
        ffn_down=p['ffn_w_down'].astype(BF16),
    )


def _trunk(x, pos0, st_ret, st_ssd, st_ssd_conv, c_k, c_v, c_logf, st_sconv, st_ffn, p, wb, t):
    bsz, length, d = x.shape
    m = bsz * length
    xf = x.reshape(m, d)
    zeros = lambda *shape: jnp.zeros(shape, F32)

    proj, dt_proj = _norm_matmul(xf, p['ab_norm_w'][0], wb['ab_in'], wb['ab_small'],
                                 n=AB_MAIN, tm=t['tm_ab'], tn=t['tn_ab'])
    cosf, sinf = _rope_tables(pos0, length)
    ret_state = zeros(bsz, RET_HEADS, RET_DK, RET_DV) if st_ret is None else st_ret
    y_ret, ret_new = _retention(proj, cosf, sinf, ret_state, p['ret_norm_w'][0],
                                bsz=bsz, seq_len=length, c=t['c_ret'])
    ssd_state = zeros(bsz, SSD_HEADS, SSD_HEADDIM, SSD_DSTATE) if st_ssd is None else st_ssd
    ssd_hist = zeros(bsz, SSD_CONV - 1, SSD_CONV_DIM) if st_ssd_conv is None else st_ssd_conv
    y_ssd, ssd_new = _ssd(proj, dt_proj, _hist8(ssd_hist), ssd_state, p['ssd_conv_w'][0], p['ssd_conv_b'][0],
                          p['ssd_dt_bias'][0], p['ssd_A_log'][0], p['ssd_D'][0], p['ssd_norm_w'][0],
                          bsz=bsz, seq_len=length, c=t['c_ssd'])
    xbc_lo = AB_MAIN - SSD_CONV_DIM
    ssd_conv_new = proj.reshape(bsz, length, -1)[:, length - (SSD_CONV - 1):, xbc_lo:AB_MAIN]
    xf = _proj_residual(xf, y_ret, y_ssd, wb['ab_out'], tm=t['tm_out'], tn=t['tn_out'])

    ffn_new = []
    ffn_hist0 = zeros(bsz, FFN_CONV - 1, D_FF) if st_ffn is None else st_ffn[0]
    xf, tails = _conv_ffn(xf, p['ffn_norm_w'][0], wb['ffn_gate'], wb['ffn_up'], wb['ffn_down'],
                          p['ffn_conv_w'][0], p['ffn_conv_b'][0], _hist8(ffn_hist0), p['final_norm_w'],
                          layer=0, seq_len=length, tm=t['tm_ffn'], tf=t['tf_ffn'], ts=t['ts_ffn'], final=False)
    ffn_new.append(_tails(tails, bsz, length, t['tm_ffn'], FFN_CONV - 1))

    proj, fl_proj = _norm_matmul(xf, p['cd_norm_w'][0], wb['cd_in'], wb['cd_small'],
                                 n=CD_MAIN, tm=t['tm_cd'], tn=t['tn_cd'])
    f_bias = jnp.pad(p['fox_f_bias'][0].astype(F32), (0, LANES - FOX_HEADS)).reshape(1, LANES)
    head_shape = (bsz, length, FOX_HEADS, FOX_HEAD_DIM)
    if c_k is None:
        qa, ka, vb, k32, v32, logf = _fox_prep(proj, fl_proj, f_bias, bsz=bsz, seq_len=length, tp=t['t_prep'])
        y_fox = _fox_prompt(qa, ka, vb, bsz=bsz, seq_len=length, t=t['t_fox'], ts=t['ts_fox'],
                            ts_diag=t['ts_fox_diag'])
        logf_new = logf.reshape(bsz, length, LANES)[:, :, :FOX_HEADS]
        k_new, v_new = k32.reshape(head_shape), v32.reshape(head_shape)
    else:
        proj3 = proj.reshape(bsz, length, -1)
        k_new = proj3[:, :, FOX_WIDTH:2 * FOX_WIDTH].reshape(head_shape)
        v_new = proj3[:, :, 2 * FOX_WIDTH:3 * FOX_WIDTH].reshape(head_shape)
        past = c_k.shape[1]
        pairs = bsz * FOX_HEADS
        assert pairs <= LANES and length <= LANES
        to_lanes = lambda a, rows: jnp.pad(jnp.swapaxes(a, 0, 1).reshape(a.shape[1], pairs),
                                           ((0, rows - a.shape[1]), (0, LANES - pairs)))
        from_lanes = lambda a: jnp.swapaxes(a[:length, :pairs].reshape(length, bsz, FOX_HEADS), 0, 1)
        cache_lf = to_lanes(c_logf.astype(F32), past)
        fl_rows = to_lanes(fl_proj.reshape(bsz, length, LANES)[:, :, :FOX_HEADS], LANES)
        bias_lanes = jnp.pad(jnp.tile(p['fox_f_bias'][0].astype(F32), bsz), (0, LANES - pairs)).reshape(1, LANES)
        cum_t_cache, lf_rows, cum_rows, cum_t_new = _decode_cum(cache_lf, fl_rows, bias_lanes, c=t['c_cum'])
        logf_new = from_lanes(lf_rows)
        cq = jnp.broadcast_to(cum_rows[:length, :pairs].T[:, :, None], (pairs, length, LANES))
        y_fox = _fox_decode(proj, c_k, c_v, cq, cum_t_cache[:pairs, None, :], cum_t_new[:pairs, None, :],
                            bsz=bsz, lq=length, tk=t['tk_dec'])
    sc_hist = zeros(bsz, SC_WIDTH - 1, SC_DIM) if st_sconv is None else st_sconv
    y_sc, sc_tails = _sconv(proj, p['sconv_w'][0], _hist8(sc_hist), seq_len=length, tm=t['tm_sc'])
    sconv_new = _tails(sc_tails, bsz, length, t['tm_sc'], SC_WIDTH - 1)
    xf = _proj_residual(xf, y_fox, y_sc, wb['cd_out'], tm=t['tm_out'], tn=t['tn_out'])

    ffn_hist1 = zeros(bsz, FFN_CONV - 1, D_FF) if st_ffn is None else st_ffn[1]
    xf, tails = _conv_ffn(xf, p['ffn_norm_w'][1], wb['ffn_gate'], wb['ffn_up'], wb['ffn_down'],
                          p['ffn_conv_w'][1], p['ffn_conv_b'][1], _hist8(ffn_hist1), p['final_norm_w'],
                          layer=1, seq_len=length, tm=t['tm_ffn'], tf=t['tf_ffn'], ts=t['ts_ffn'], final=True)
    ffn_new.append(_tails(tails, bsz, length, t['tm_ffn'], FFN_CONV - 1))

    return (xf.reshape(bsz, length, d), ret_new[None], ssd_new[None], ssd_conv_new[None], k_new[None],
            v_new[None], logf_new[None], sconv_new[None], jnp.stack(ffn_new))


def _largest_divisor(n, cap, multiple=1):
    best = None
    for cand in range(multiple, min(n, cap) + 1, multiple):
        if n % cand == 0:
            best = cand
    assert best is not None, (n, cap, multiple)
    return best


def _tiles(bsz, length, past=None):
    m = bsz * length
    seq_tile = lambda cap: _largest_divisor(length, cap, SUBLANES)
    row_tile = lambda cap: (_largest_divisor(length, cap, SUBLANES) if length >= cap
                            else _largest_divisor(m, cap, length))
    t = dict(
        tm_ab=row_tile(1024), tn_ab=2048, tm_cd=row_tile(1024), tn_cd=2048,
        tm_out=row_tile(1024), tn_out=2048,
        tm_ffn=row_tile(1024), tf_ffn=512, ts_ffn=512,
        tm_sc=row_tile(512),
        c_ret=seq_tile(256), c_ssd=seq_tile(256),
    )
    if past is None:
        t['t_fox'] = seq_tile(2048)
        t['ts_fox'] = _largest_divisor(t['t_fox'], 256, LANES)
        t['ts_fox_diag'] = _largest_divisor(t['t_fox'], 512, LANES)
        t['t_prep'] = seq_tile(512)
    else:
        t['tk_dec'] = _largest_divisor(past, 1024, LANES)
        t['c_cum'] = _largest_divisor(past, 256, LANES)
    return t


def kernel(x_prompt, x_sample, state_ret, state_ssd, state_ssd_conv, cache_fox_k, cache_fox_v, cache_fox_logf, state_sconv, state_ffn_conv, ab_norm_w, ab_w_in, ret_norm_w, ssd_conv_w, ssd_conv_b, ssd_dt_bias, ssd_A_log, ssd_D, ssd_norm_w, ab_w_out, cd_norm_w, cd_w_in, fox_f_bias, sconv_w, cd_w_out, ffn_norm_w, ffn_w_gate, ffn_w_up, ffn_conv_w, ffn_conv_b, ffn_w_down, final_norm_w):
    p = dict(ab_norm_w=ab_norm_w, ab_w_in=ab_w_in, ret_norm_w=ret_norm_w, ssd_conv_w=ssd_conv_w,
             ssd_conv_b=ssd_conv_b, ssd_dt_bias=ssd_dt_bias, ssd_A_log=ssd_A_log, ssd_D=ssd_D,
             ssd_norm_w=ssd_norm_w, ab_w_out=ab_w_out, cd_norm_w=cd_norm_w, cd_w_in=cd_w_in,
             fox_f_bias=fox_f_bias, sconv_w=sconv_w, cd_w_out=cd_w_out, ffn_norm_w=ffn_norm_w,
             ffn_w_gate=ffn_w_gate, ffn_w_up=ffn_w_up, ffn_conv_w=ffn_conv_w, ffn_conv_b=ffn_conv_b,
             ffn_w_down=ffn_w_down, final_norm_w=final_norm_w)
    assert x_prompt.shape[-1] == D_MODEL and ab_w_in.shape == (1, D_MODEL, AB_MAIN + SSD_HEADS)
    assert cd_w_in.shape == (1, D_MODEL, CD_MAIN + FOX_HEADS) and ffn_w_gate.shape == (2, D_MODEL, D_FF)
    wb = _prep_weights(p)
    bp, lp_, _ = x_prompt.shape
    bs, ls, _ = x_sample.shape
    past = cache_fox_k.shape[2]
    (y_prompt, p_ret, p_ssd, p_ssd_conv, p_fox_k, p_fox_v, p_fox_logf, p_sconv, p_ffn_conv) = _trunk(
        x_prompt, 0, None, None, None, None, None, None, None, None, p, wb, _tiles(bp, lp_))
    (y_sample, s_ret, s_ssd, s_ssd_conv, s_fox_k, s_fox_v, s_fox_logf, s_sconv, s_ffn_conv) = _trunk(
        x_sample, past, state_ret[0], state_ssd[0], state_ssd_conv[0], cache_fox_k[0], cache_fox_v[0],
        cache_fox_logf[0], state_sconv[0], state_ffn_conv, p, wb, _tiles(bs, ls, past))
    return (y_prompt, y_sample, p_ret, s_ret, p_ssd, s_ssd, p_ssd_conv, s_ssd_conv, p_fox_k, s_fox_k,
            p_fox_v, s_fox_v, p_fox_logf, s_fox_logf, p_sconv, s_sconv, p_ffn_conv, s_ffn_conv)
```

```python
import functools
import math

import numpy as np
import jax
import jax.numpy as jnp
from jax import lax
from jax.experimental import pallas as pl
from jax.experimental.pallas import tpu as pltpu

F32 = jnp.float32
BF16 = jnp.bfloat16
EPS = 1e-6
ROPE_BASE = 10000.0
NEG_INF = float("-inf")

D_MODEL = 2048
RET_HEADS, RET_DK, RET_DV = 4, 128, 256
SSD_DINNER, SSD_HEADDIM, SSD_HEADS, SSD_GROUPS, SSD_DSTATE, SSD_CONV = 1024, 64, 16, 2, 128, 4
SSD_CONV_DIM = SSD_DINNER + 2 * SSD_GROUPS * SSD_DSTATE
FOX_HEADS, FOX_HEAD_DIM = 8, 128
FOX_WIDTH = FOX_HEADS * FOX_HEAD_DIM
SC_DIM, SC_WIDTH = 1024, 3
D_FF, FFN_CONV = 5632, 3
AB_MAIN = 2 * RET_HEADS * RET_DK + 2 * RET_HEADS * RET_DV + SSD_DINNER + SSD_CONV_DIM
AB_PAD = AB_MAIN + 128
CD_MAIN = 3 * FOX_WIDTH + 3 * SC_DIM
CD_PAD = CD_MAIN + 128

LANES = 128
SUBLANES = 8
VMEM_LIMIT = 60 * 1024 * 1024


def _cparams(n_axes):
    return pltpu.CompilerParams(dimension_semantics=("arbitrary",) * n_axes,
                                vmem_limit_bytes=VMEM_LIMIT)


def _rms(xf, w):
    return xf * lax.rsqrt(jnp.mean(xf * xf, axis=-1, keepdims=True) + EPS) * w


def _softplus(x):
    return jnp.maximum(x, 0.0) + jnp.log1p(jnp.exp(-jnp.abs(x)))


def _split3(x):
    hi = x.astype(BF16)
    r1 = x - hi.astype(F32)
    mid = r1.astype(BF16)
    lo = (r1 - mid.astype(F32)).astype(BF16)
    return hi, mid, lo


def _widen(x, n):
    return x[:, 0:n] if n <= LANES else jnp.concatenate([x] * (n // LANES), axis=1)


def _dot(a, b):
    return jnp.dot(a, b, preferred_element_type=F32)


def _dot_nt(a, b):
    return lax.dot_general(a, b, (((1,), (1,)), ((), ())), preferred_element_type=F32)


def _dot_tn(a, b):
    return lax.dot_general(a, b, (((0,), (0,)), ((), ())), preferred_element_type=F32)


def _exact_lhs_dot(m_bf16, x):
    hi, mid, lo = _split3(x)
    return _dot(m_bf16, hi) + _dot(m_bf16, mid) + _dot(m_bf16, lo)


def _exact_rhs_dot(x, m_bf16):
    hi, mid, lo = _split3(x)
    return _dot(hi, m_bf16) + _dot(mid, m_bf16) + _dot(lo, m_bf16)


def _conv_rows(ext_ref, x, prev8, w_ref, width, rows, w_cols=slice(None)):
    ext_ref[0:SUBLANES, :] = prev8
    ext_ref[SUBLANES:SUBLANES + rows, :] = x
    out = None
    for j in range(width):
        off = SUBLANES - (width - 1) + j
        term = ext_ref[off:off + rows, :] * w_ref[j:j + 1, w_cols]
        out = term if out is None else out + term
    return out


def _seq_tiling(seq_len, tile_rows):
    if seq_len >= tile_rows:
        assert seq_len % tile_rows == 0
        return tile_rows, 1, seq_len // tile_rows
    assert tile_rows % seq_len == 0 and seq_len % SUBLANES == 0
    return seq_len, tile_rows // seq_len, 1


def _norm_matmul_kernel(x_ref, nw_ref, w_ref, ws_ref, o_ref, os_ref, h_ref, *, n_tiles, tn, last_w):
    j = pl.program_id(1)

    @pl.when(j == 0)
    def _():
        h_ref[...] = _rms(x_ref[...], nw_ref[...]).astype(BF16)
        os_ref[...] = _dot(h_ref[...], ws_ref[...])

    if last_w == tn:
        o_ref[...] = _dot(h_ref[...], w_ref[...])
    else:
        @pl.when(j < n_tiles - 1)
        def _():
            o_ref[...] = _dot(h_ref[...], w_ref[...])

        @pl.when(j == n_tiles - 1)
        def _():
            o_ref[:, 0:last_w] = _dot(h_ref[...], w_ref[:, 0:last_w])


def _norm_matmul(x, norm_w, w, w_small, *, n, tm, tn):
    m, d = x.shape
    n_tiles = pl.cdiv(n, tn)
    last_w = n - (n_tiles - 1) * tn
    assert m % tm == 0 and last_w % LANES == 0 and n <= w.shape[1] and w_small.shape == (d, LANES)
    return pl.pallas_call(
        functools.partial(_norm_matmul_kernel, n_tiles=n_tiles, tn=tn, last_w=last_w),
        grid=(m // tm, n_tiles),
        in_specs=[pl.BlockSpec((tm, d), lambda i, j: (i, 0)),
                  pl.BlockSpec((1, d), lambda i, j: (0, 0)),
                  pl.BlockSpec((d, tn), lambda i, j: (0, j)),
                  pl.BlockSpec((d, LANES), lambda i, j: (0, 0))],
        out_specs=[pl.BlockSpec((tm, tn), lambda i, j: (i, j)),
                   pl.BlockSpec((tm, LANES), lambda i, j: (i, 0))],
        out_shape=[jax.ShapeDtypeStruct((m, n), F32),
                   jax.ShapeDtypeStruct((m, LANES), F32)],
        scratch_shapes=[pltpu.VMEM((tm, d), BF16)],
        compiler_params=_cparams(2),
        name="norm_in_proj",
    )(x, norm_w.reshape(1, d), w, w_small)


def _proj_res_kernel(x_ref, a_ref, b_ref, wa_ref, wb_ref, o_ref):
    acc = _dot(a_ref[...], wa_ref[...])
    acc = acc + _dot(b_ref[...], wb_ref[...])
    o_ref[...] = x_ref[...] + acc


def _proj_residual(x, a, b, w, *, tm, tn):
    m, d = x.shape
    ka, kb = a.shape[1], b.shape[1]
    assert m % tm == 0 and d % tn == 0 and ka == kb and w.shape == (ka + kb, d)
    return pl.pallas_call(
        _proj_res_kernel,
        grid=(m // tm, d // tn),
        in_specs=[pl.BlockSpec((tm, tn), lambda i, j: (i, j)),
                  pl.BlockSpec((tm, ka), lambda i, j: (i, 0)),
                  pl.BlockSpec((tm, kb), lambda i, j: (i, 0)),
                  pl.BlockSpec((ka, tn), lambda i, j: (0, j)),
                  pl.BlockSpec((kb, tn), lambda i, j: (1, j))],
        out_specs=pl.BlockSpec((tm, tn), lambda i, j: (i, j)),
        out_shape=jax.ShapeDtypeStruct((m, d), F32),
        compiler_params=_cparams(2),
        name="out_proj_residual",
    )(x, a, b, w, w)


def _ffn_kernel(x_ref, nw_ref, wg_ref, wu_ref, wd_ref, cw_ref, cb_ref, hist_ref, fw_ref,
                o_ref, tail_ref, h_ref, carry_ref, ext_ref,
                *, tm, rows, spt, tpb, nf, ts, nsub, nsub_last, final):
    i = pl.program_id(0)
    f = pl.program_id(1)

    @pl.when(f == 0)
    def _():
        xf = x_ref[...]
        h_ref[...] = _rms(xf, nw_ref[...]).astype(BF16)
        o_ref[...] = xf

    first = (i % tpb) == 0

    def sub_block(sb):
        cols = slice(sb * ts, (sb + 1) * ts)
        h = h_ref[...]
        a = _dot(h, wg_ref[:, cols])
        u = _dot(h, wu_ref[:, cols])
        convs = []
        for s in range(spt):
            a_s = a[s * rows:(s + 1) * rows]
            if tpb == 1:
                prev = hist_ref[s, :, cols]
            else:
                prev = jnp.where(first, hist_ref[s, :, cols], carry_ref[f * nsub + sb])
            convs.append(_conv_rows(ext_ref, a_s, prev, cw_ref, FFN_CONV, rows, cols))
            tail_ref[s, :, cols] = a_s[rows - SUBLANES:rows]
        if tpb > 1:
            carry_ref[f * nsub + sb] = a[tm - SUBLANES:tm]
        conv = convs[0] if spt == 1 else jnp.concatenate(convs, axis=0)
        act = (jax.nn.silu(conv + cb_ref[:, cols]) * u).astype(BF16)
        o_ref[...] += _dot(act, wd_ref[cols, :])

    if nsub_last == nsub:
        for sb in range(nsub):
            sub_block(sb)
    else:
        @pl.when(f < nf - 1)
        def _():
            for sb in range(nsub):
                sub_block(sb)

        @pl.when(f == nf - 1)
        def _():
            for sb in range(nsub_last):
                sub_block(sb)

    if final:
        @pl.when(f == nf - 1)
        def _():
            o_ref[...] = _rms(o_ref[...], fw_ref[...])


def _conv_ffn(x, norm_w, wg, wu, wd, conv_w, conv_b, hist8, final_w, *, layer, seq_len, tm, tf, ts, final):
    m, d = x.shape
    ff = wg.shape[2]
    assert m % tm == 0 and tf % ts == 0 and ff % ts == 0
    rows, spt, tpb = _seq_tiling(seq_len, tm)
    nm, nf = m // tm, pl.cdiv(ff, tf)
    nsub = tf // ts
    nsub_last = (ff - (nf - 1) * tf) // ts
    hist_map = (lambda i, f: (i // tpb, 0, f)) if spt == 1 else (lambda i, f: (i, 0, f))
    kern = functools.partial(_ffn_kernel, tm=tm, rows=rows, spt=spt, tpb=tpb, nf=nf, ts=ts, nsub=nsub,
                             nsub_last=nsub_last, final=final)
    out, tails = pl.pallas_call(
        kern,
        grid=(nm, nf),
        in_specs=[pl.BlockSpec((tm, d), lambda i, f: (i, 0)),
                  pl.BlockSpec((1, d), lambda i, f: (0, 0)),
                  pl.BlockSpec((None, d, tf), lambda i, f: (layer, 0, f)),
                  pl.BlockSpec((None, d, tf), lambda i, f: (layer, 0, f)),
                  pl.BlockSpec((None, tf, d), lambda i, f: (layer, f, 0)),
                  pl.BlockSpec((FFN_CONV, tf), lambda i, f: (0, f)),
                  pl.BlockSpec((1, tf), lambda i, f: (0, f)),
                  pl.BlockSpec((spt, SUBLANES, tf), hist_map),
                  pl.BlockSpec((1, d), lambda i, f: (0, 0))],
        out_specs=[pl.BlockSpec((tm, d), lambda i, f: (i, 0)),
                   pl.BlockSpec((spt, SUBLANES, tf), lambda i, f: (i, 0, f))],
        out_shape=[jax.ShapeDtypeStruct((m, d), F32),
                   jax.ShapeDtypeStruct((nm * spt, SUBLANES, ff), F32)],
        scratch_shapes=[pltpu.VMEM((tm, d), BF16),
                        pltpu.VMEM((nf * nsub, SUBLANES, ts), F32),
                        pltpu.VMEM((rows + SUBLANES, ts), F32)],
        compiler_params=_cparams(2),
        name="conv_ffn",
    )(x, norm_w.reshape(1, d), wg, wu, wd, conv_w, conv_b.reshape(1, ff), hist8, final_w.reshape(1, d))
    return out, tails


def _retention_kernel(q_ref, k_ref, v_ref, g_ref, cos_ref, sin_ref, st_ref, nw_ref,
                      y_ref, so_ref, *, c):
    ci = pl.program_id(1)

    @pl.when(ci == 0)
    def _():
        so_ref[...] = st_ref[...]

    cos = cos_ref[...]
    sin = sin_ref[...]
    ii = lax.broadcasted_iota(jnp.int32, (c, c), 0)
    jj = lax.broadcasted_iota(jnp.int32, (c, c), 1)
    diff = (ii - jj).astype(F32)
    causal = ii >= jj
    ridx = lax.broadcasted_iota(jnp.int32, (c, 1), 0).astype(F32)
    for h in range(RET_HEADS):
        lg = math.log1p(-(2.0 ** (-5.0 - h)))
        q = q_ref[:, h * RET_DK:(h + 1) * RET_DK]
        k = k_ref[:, h * RET_DK:(h + 1) * RET_DK]
        v = v_ref[:, h * RET_DV:(h + 1) * RET_DV]
        qr = q * cos + pltpu.roll(q, RET_DK // 2, 1) * sin
        kr = (k * cos + pltpu.roll(k, RET_DK // 2, 1) * sin) * (RET_DK ** -0.5)
        qb = qr.astype(BF16)
        kb = kr.astype(BF16)
        vb = v.astype(BF16)
        decay = jnp.exp(jnp.where(causal, diff * lg, NEG_INF))
        inner = jnp.exp((ridx + 1.0) * lg)
        sdecay = jnp.exp((c - 1.0 - ridx) * lg)
        s = so_ref[0, h]
        scores = _dot_nt(qb, kb) * decay
        y = _dot(scores.astype(BF16), vb)
        y = y + _dot(qb, s.astype(BF16)) * inner
        kd = (kr * sdecay).astype(BF16)
        so_ref[0, h] = math.exp(c * lg) * s + _dot_tn(kd, vb)
        mu = jnp.mean(y, axis=-1, keepdims=True)
        yc = y - mu
        var = jnp.mean(yc * yc, axis=-1, keepdims=True)
        yn = yc * lax.rsqrt(var + EPS) * nw_ref[:, h * RET_DV:(h + 1) * RET_DV]
        g = g_ref[:, h * RET_DV:(h + 1) * RET_DV]
        y_ref[:, h * RET_DV:(h + 1) * RET_DV] = (jax.nn.silu(g) * yn).astype(BF16)


def _retention(proj, cosf, sinf, state, norm_w, *, bsz, seq_len, c):
    m = proj.shape[0]
    nc = seq_len // c
    assert seq_len % c == 0
    qk_w = RET_HEADS * RET_DK
    v_w = RET_HEADS * RET_DV
    row = lambda b, ci: b * nc + ci
    y, s_new = pl.pallas_call(
        functools.partial(_retention_kernel, c=c),
        grid=(bsz, nc),
        in_specs=[pl.BlockSpec((c, qk_w), lambda b, ci: (row(b, ci), 0)),
                  pl.BlockSpec((c, qk_w), lambda b, ci: (row(b, ci), 1)),
                  pl.BlockSpec((c, v_w), lambda b, ci: (row(b, ci), 1)),
                  pl.BlockSpec((c, v_w), lambda b, ci: (row(b, ci), 2)),
                  pl.BlockSpec((c, RET_DK), lambda b, ci: (ci, 0)),
                  pl.BlockSpec((c, RET_DK), lambda b, ci: (ci, 0)),
                  pl.BlockSpec((1, RET_HEADS, RET_DK, RET_DV), lambda b, ci: (b, 0, 0, 0)),
                  pl.BlockSpec((1, v_w), lambda b, ci: (0, 0))],
        out_specs=[pl.BlockSpec((c, v_w), lambda b, ci: (row(b, ci), 0)),
                   pl.BlockSpec((1, RET_HEADS, RET_DK, RET_DV), lambda b, ci: (b, 0, 0, 0))],
        out_shape=[jax.ShapeDtypeStruct((m, v_w), BF16),
                   jax.ShapeDtypeStruct(state.shape, F32)],
        compiler_params=_cparams(2),
        name="retention",
    )(proj, proj, proj, proj, cosf, sinf, state, norm_w.reshape(1, v_w))
    return y, s_new


def _ssd_kernel(z_ref, xs_ref, bc_ref, dt_ref, hx_ref, hbc_ref, st_ref,
                cwx_ref, cwbc_ref, cbx_ref, cbbc_ref, dtb_ref, alog_ref, dsk_ref, nw_ref,
                tri_ref, exp_ref,
                y_ref, so_ref,
                st_scr, cx_scr, cbc_scr, extx_scr, extbc_scr, yh_scr, xs_scr, *, c, nc):
    ci = pl.program_id(1)
    gw = SSD_DINNER // SSD_GROUPS
    hpg = SSD_HEADS // SSD_GROUPS

    @pl.when(ci == 0)
    def _():
        st_scr[...] = st_ref[0].T
        cx_scr[...] = hx_ref[0]
        cbc_scr[...] = hbc_ref[0]

    xs_raw = xs_ref[...]
    bc_raw = bc_ref[...]
    xs_scr[...] = jax.nn.silu(_conv_rows(extx_scr, xs_raw, cx_scr[...], cwx_ref, SSD_CONV, c) + cbx_ref[...])
    bcm = jax.nn.silu(_conv_rows(extbc_scr, bc_raw, cbc_scr[...], cwbc_ref, SSD_CONV, c) + cbbc_ref[...])
    cx_scr[...] = xs_raw[c - SUBLANES:c]
    cbc_scr[...] = bc_raw[c - SUBLANES:c]

    tri = tri_ref[...]
    dt = _softplus(dt_ref[...] + dtb_ref[...])
    a = -jnp.exp(alog_ref[...])
    acs = _exact_lhs_dot(tri, dt * a)
    acs_t = acs.T
    acs_last = acs[c - 1:c, :]
    exp_acs = jnp.exp(acs)
    to_end = jnp.exp(acs_last - acs)
    chunk_dec = jnp.exp(acs_last)

    ii = lax.broadcasted_iota(jnp.int32, (c, c), 0)
    jj = lax.broadcasted_iota(jnp.int32, (c, c), 1)
    causal = ii >= jj
    nb = SSD_GROUPS * SSD_DSTATE
    for g in range(SSD_GROUPS):
        cols = slice(g * gw, (g + 1) * gw)
        expand = exp_ref[:, cols]
        xdt = xs_scr[:, cols] * _exact_rhs_dot(dt, expand)
        xdt_b = xdt.astype(BF16)
        xend_b = (xdt * _exact_rhs_dot(to_end, expand)).astype(BF16)
        b_g = bcm[:, g * SSD_DSTATE:(g + 1) * SSD_DSTATE].astype(BF16)
        c_g = bcm[:, nb + g * SSD_DSTATE:nb + (g + 1) * SSD_DSTATE].astype(BF16)
        cb = _dot_nt(c_g, b_g)
        s_g = st_scr[:, cols]
        y_state = _dot(c_g, s_g.astype(BF16)) * _exact_rhs_dot(exp_acs, expand)
        for r in range(hpg):
            hh = g * hpg + r
            seg = acs[:, hh:hh + 1] - acs_t[hh:hh + 1, :]
            lmat = jnp.exp(jnp.where(causal, seg, NEG_INF))
            mm = (cb * lmat).astype(BF16)
            head = slice(r * SSD_HEADDIM, (r + 1) * SSD_HEADDIM)
            yh_scr[:, hh * SSD_HEADDIM:(hh + 1) * SSD_HEADDIM] = _dot(mm, xdt_b[:, head]) + y_state[:, head]
        upd = _dot_tn(b_g, xend_b)
        st_scr[:, cols] = _exact_rhs_dot(chunk_dec, expand) * s_g + upd

    y = yh_scr[...] + dsk_ref[...] * xs_scr[...]
    z = z_ref[...]
    y_ref[...] = _rms(y * jax.nn.silu(z), nw_ref[...]).astype(BF16)

    @pl.when(ci == nc - 1)
    def _():
        so_ref[0] = st_scr[...].T


def _ssd(proj, dt_proj, hist8, state, conv_w, conv_b, dt_bias, a_log, d_skip, norm_w, *, bsz, seq_len, c):
    m = proj.shape[0]
    nc = seq_len // c
    assert seq_len % c == 0
    row = lambda b, ci: b * nc + ci
    const2 = lambda b, ci: (0, 0)
    di, bcw = SSD_DINNER, 2 * SSD_GROUPS * SSD_DSTATE
    tri = jnp.asarray(np.tril(np.ones((c, c), np.float32)), BF16)
    expand = np.zeros((LANES, di), np.float32)
    for h in range(SSD_HEADS):
        expand[h, h * SSD_HEADDIM:(h + 1) * SSD_HEADDIM] = 1.0
    expand = jnp.asarray(expand, BF16)
    pad_row = lambda v: jnp.pad(v.astype(F32), (0, LANES - v.shape[0])).reshape(1, LANES)
    st2 = state.reshape(bsz, di, SSD_DSTATE)
    y, s_new = pl.pallas_call(
        functools.partial(_ssd_kernel, c=c, nc=nc),
        grid=(bsz, nc),
        in_specs=[pl.BlockSpec((c, di), lambda b, ci: (row(b, ci), 3)),
                  pl.BlockSpec((c, di), lambda b, ci: (row(b, ci), 4)),
                  pl.BlockSpec((c, bcw), lambda b, ci: (row(b, ci), 10)),
                  pl.BlockSpec((c, LANES), lambda b, ci: (row(b, ci), 0)),
                  pl.BlockSpec((1, SUBLANES, di), lambda b, ci: (b, 0, 0)),
                  pl.BlockSpec((1, SUBLANES, bcw), lambda b, ci: (b, 0, 2)),
                  pl.BlockSpec((1, di, SSD_DSTATE), lambda b, ci: (b, 0, 0)),
                  pl.BlockSpec((SSD_CONV, di), const2),
                  pl.BlockSpec((SSD_CONV, bcw), lambda b, ci: (0, 2)),
                  pl.BlockSpec((1, di), const2),
                  pl.BlockSpec((1, bcw), lambda b, ci: (0, 2)),
                  pl.BlockSpec((1, LANES), const2),
                  pl.BlockSpec((1, LANES), const2),
                  pl.BlockSpec((1, di), const2),
                  pl.BlockSpec((1, di), const2),
                  pl.BlockSpec((c, c), const2),
                  pl.BlockSpec((LANES, di), const2)],
        out_specs=[pl.BlockSpec((c, di), lambda b, ci: (row(b, ci), 0)),
                   pl.BlockSpec((1, di, SSD_DSTATE), lambda b, ci: (b, 0, 0))],
        out_shape=[jax.ShapeDtypeStruct((m, di), BF16),
                   jax.ShapeDtypeStruct(st2.shape, F32)],
        scratch_shapes=[pltpu.VMEM((SSD_DSTATE, di), F32),
                        pltpu.VMEM((SUBLANES, di), F32),
                        pltpu.VMEM((SUBLANES, bcw), F32),
                        pltpu.VMEM((c + SUBLANES, di), F32),
                        pltpu.VMEM((c + SUBLANES, bcw), F32),
                        pltpu.VMEM((c, di), F32),
                        pltpu.VMEM((c, di), F32)],
        compiler_params=_cparams(2),
        name="ssd",
    )(proj, proj, proj, dt_proj, hist8, hist8, st2,
      conv_w, conv_w, conv_b.reshape(1, -1), conv_b.reshape(1, -1),
      pad_row(dt_bias), pad_row(a_log), jnp.repeat(d_skip.astype(F32), SSD_HEADDIM).reshape(1, di),
      norm_w.reshape(1, di), tri, expand)
    return y, s_new.reshape(state.shape)


def _decode_cum_kernel(lfc_ref, fl_ref, b_ref, tri_ref, cumt_c_ref, lfn_ref, cumn_ref, cumt_n_ref,
                       carry_ref, *, c, ncb):
    j = pl.program_id(0)

    @pl.when(j == 0)
    def _():
        carry_ref[...] = jnp.zeros_like(carry_ref)

    @pl.when(j < ncb)
    def _():
        cum = _exact_lhs_dot(tri_ref[...], lfc_ref[...]) + carry_ref[...]
        carry_ref[...] = cum[c - 1:c, :]
        cumt_c_ref[...] = cum.T

    @pl.when(j == ncb)
    def _():
        lf = -_softplus(-(fl_ref[...] + b_ref[...]))
        lfn_ref[...] = lf
        cum = _exact_lhs_dot(tri_ref[0:LANES, 0:LANES], lf) + carry_ref[...]
        cumn_ref[...] = cum
        cumt_n_ref[...] = cum.T


def _decode_cum(cache_lf, fl_new, bias, *, c):
    past = cache_lf.shape[0]
    assert past % c == 0 and c % LANES == 0 and fl_new.shape == (LANES, LANES)
    ncb = past // c
    tri = jnp.asarray(np.tril(np.ones((c, c), np.float32)), BF16)
    blk = lambda j: jnp.minimum(j, ncb - 1)
    sq = jax.ShapeDtypeStruct((LANES, LANES), F32)
    return pl.pallas_call(
        functools.partial(_decode_cum_kernel, c=c, ncb=ncb),
        grid=(ncb + 1,),
        in_specs=[pl.BlockSpec((c, LANES), lambda j: (blk(j), 0)),
                  pl.BlockSpec((LANES, LANES), lambda j: (0, 0)),
                  pl.BlockSpec((1, LANES), lambda j: (0, 0)),
                  pl.BlockSpec((c, c), lambda j: (0, 0))],
        out_specs=[pl.BlockSpec((LANES, c), lambda j: (0, blk(j))),
                   pl.BlockSpec((LANES, LANES), lambda j: (0, 0)),
                   pl.BlockSpec((LANES, LANES), lambda j: (0, 0)),
                   pl.BlockSpec((LANES, LANES), lambda j: (0, 0))],
        out_shape=[jax.ShapeDtypeStruct((LANES, past), F32), sq, sq, sq],
        scratch_shapes=[pltpu.VMEM((1, LANES), F32)],
        compiler_params=_cparams(1),
        name="decode_logf_cumsum",
    )(cache_lf, fl_new, bias, tri)


FOX_AUG = 2 * FOX_HEAD_DIM
N_BIAS_PIECES = 3


def _fox_prep_kernel(q_ref, k_ref, v_ref, fl_ref, fb_ref, tri_ref, place_ref, ones_ref,
                     qa_ref, ka_ref, vb_ref, k32_ref, v32_ref, lf_ref, carry_ref, *, tp):
    @pl.when(pl.program_id(1) == 0)
    def _():
        carry_ref[...] = jnp.zeros_like(carry_ref)

    lf = -_softplus(-(fl_ref[...] + fb_ref[...]))
    lf_ref[...] = lf
    cum = _exact_lhs_dot(tri_ref[...], lf) + carry_ref[...]
    carry_ref[...] = cum[tp - 1:tp, :]
    pieces = _split3(cum * (FOX_HEAD_DIM ** 0.5))
    n = N_BIAS_PIECES
    aug_q = ones_ref[0:1, :] + sum(_dot(pieces[r], place_ref[r]) for r in range(n))
    aug_k = ones_ref[1:2, :] - sum(_dot(pieces[r], place_ref[n + r]) for r in range(n))
    for h in range(FOX_HEADS):
        src = slice(h * FOX_HEAD_DIM, (h + 1) * FOX_HEAD_DIM)
        feat = slice(h * FOX_AUG, h * FOX_AUG + FOX_HEAD_DIM)
        bias = slice(h * FOX_AUG + FOX_HEAD_DIM, (h + 1) * FOX_AUG)
        qa_ref[:, feat] = q_ref[:, src].astype(BF16)
        qa_ref[:, bias] = aug_q[:, src].astype(BF16)
        ka_ref[h, 0:FOX_HEAD_DIM, :] = k_ref[:, src].T.astype(BF16)
        ka_ref[h, FOX_HEAD_DIM:FOX_AUG, :] = aug_k[:, src].T.astype(BF16)
    k = k_ref[...]
    v = v_ref[...]
    k32_ref[...] = pltpu.einshape("m(hd)->mhd", k, h=FOX_HEADS)
    v32_ref[...] = pltpu.einshape("m(hd)->mhd", v, h=FOX_HEADS)
    vb_ref[...] = v.astype(BF16)


def _fox_prep(proj, fl_proj, f_bias, *, bsz, seq_len, tp):
    m = proj.shape[0]
    nt = seq_len // tp
    assert seq_len % tp == 0
    w = FOX_WIDTH
    tri = jnp.asarray(np.tril(np.ones((tp, tp), np.float32)), BF16)
    n = N_BIAS_PIECES
    place = np.zeros((2 * n, LANES, w), np.float32)
    ones = np.zeros((SUBLANES, w), np.float32)
    for h in range(FOX_HEADS):
        for r in range(2 * n):
            place[r, h, h * FOX_HEAD_DIM + r] = 1.0
        ones[0, h * FOX_HEAD_DIM + n:h * FOX_HEAD_DIM + 2 * n] = 1.0
        ones[1, h * FOX_HEAD_DIM:h * FOX_HEAD_DIM + n] = 1.0
    row = lambda b, ti: (b * nt + ti, 0)
    const2 = lambda b, ti: (0, 0)
    return pl.pallas_call(
        functools.partial(_fox_prep_kernel, tp=tp),
        grid=(bsz, nt),
        in_specs=[pl.BlockSpec((tp, w), lambda b, ti: (b * nt + ti, 0)),
                  pl.BlockSpec((tp, w), lambda b, ti: (b * nt + ti, 1)),
                  pl.BlockSpec((tp, w), lambda b, ti: (b * nt + ti, 2)),
                  pl.BlockSpec((tp, LANES), row),
                  pl.BlockSpec((1, LANES), const2),
                  pl.BlockSpec((tp, tp), const2),
                  pl.BlockSpec((2 * n, LANES, w), lambda b, ti: (0, 0, 0)),
                  pl.BlockSpec((SUBLANES, w), const2)],
        out_specs=[pl.BlockSpec((tp, FOX_HEADS * FOX_AUG), row),
                   pl.BlockSpec((FOX_HEADS, FOX_AUG, tp), lambda b, ti: (b, 0, ti)),
                   pl.BlockSpec((tp, w), row),
                   pl.BlockSpec((tp, FOX_HEADS, FOX_HEAD_DIM), lambda b, ti: (b * nt + ti, 0, 0)),
                   pl.BlockSpec((tp, FOX_HEADS, FOX_HEAD_DIM), lambda b, ti: (b * nt + ti, 0, 0)),
                   pl.BlockSpec((tp, LANES), row)],
        out_shape=[jax.ShapeDtypeStruct((m, FOX_HEADS * FOX_AUG), BF16),
                   jax.ShapeDtypeStruct((bsz * FOX_HEADS, FOX_AUG, seq_len), BF16),
                   jax.ShapeDtypeStruct((m, w), BF16),
                   jax.ShapeDtypeStruct((m, FOX_HEADS, FOX_HEAD_DIM), F32),
                   jax.ShapeDtypeStruct((m, FOX_HEADS, FOX_HEAD_DIM), F32),
                   jax.ShapeDtypeStruct((m, LANES), F32)],
        scratch_shapes=[pltpu.VMEM((1, LANES), F32)],
        compiler_params=_cparams(2),
        name="fox_prep",
    )(proj, proj, proj, fl_proj, f_bias, tri, jnp.asarray(place, BF16), jnp.asarray(ones, F32))


def _fox_kernel(qi_ref, ki_ref, q_ref, k_ref, v_ref, o_ref, m_ref, acc_ref, va_ref, *, t, ts, ts_diag):
    step = pl.program_id(2)
    qi = qi_ref[step]
    ki = ki_ref[step]
    to_log2 = (FOX_HEAD_DIM ** -0.5) * math.log2(math.e)
    hd = FOX_HEAD_DIM

    @pl.when(ki == 0)
    def _():
        m_ref[...] = jnp.full_like(m_ref, NEG_INF)
        acc_ref[...] = jnp.zeros_like(acc_ref)
        va_ref[:, hd:2 * hd] = jnp.ones((t, hd), BF16)

    va_ref[:, 0:hd] = v_ref[...]

    def update(diagonal):
        tc = ts_diag if diagonal else ts
        for r in range(t // tc):
            rows = slice(r * tc, (r + 1) * tc)
            nk = (r + 1) * tc if diagonal else t
            s = _dot(q_ref[rows, :], k_ref[:, 0:nk])
            if diagonal:
                ri = lax.broadcasted_iota(jnp.int32, (tc, nk), 0) + r * tc
                ci = lax.broadcasted_iota(jnp.int32, (tc, nk), 1)
                s = jnp.where(ci <= ri, s, NEG_INF)
            m_old = m_ref[rows, :]
            m_new = jnp.maximum(m_old, jnp.max(s, axis=1, keepdims=True))
            m_ref[rows, :] = m_new
            alpha = jnp.exp2((m_old - m_new) * to_log2)
            p = jnp.exp2((s - _widen(m_new, nk)) * to_log2)
            acc_ref[rows, :] = _widen(alpha, 2 * hd) * acc_ref[rows, :] + _dot(p.astype(BF16), va_ref[0:nk, :])

    @pl.when(ki < qi)
    def _():
        update(False)

    @pl.when(ki == qi)
    def _():
        update(True)
        o_ref[...] = (acc_ref[:, 0:hd] / acc_ref[:, hd:2 * hd]).astype(BF16)


def _fox_prompt(qa, ka, vb, *, bsz, seq_len, t, ts, ts_diag):
    m = qa.shape[0]
    nq = seq_len // t
    assert seq_len % t == 0 and t % ts == 0 and t % ts_diag == 0
    pairs = [(qi, ki) for qi in range(nq) for ki in range(qi + 1)]
    qi_tab = jnp.asarray([p[0] for p in pairs], jnp.int32)
    ki_tab = jnp.asarray([p[1] for p in pairs], jnp.int32)
    grid_spec = pltpu.PrefetchScalarGridSpec(
        num_scalar_prefetch=2,
        grid=(bsz, FOX_HEADS, len(pairs)),
        in_specs=[pl.BlockSpec((t, FOX_AUG), lambda b, h, s, qi, ki: (b * nq + qi[s], h)),
                  pl.BlockSpec((None, FOX_AUG, t), lambda b, h, s, qi, ki: (b * FOX_HEADS + h, 0, ki[s])),
                  pl.BlockSpec((t, FOX_HEAD_DIM), lambda b, h, s, qi, ki: (b * nq + ki[s], h))],
        out_specs=pl.BlockSpec((t, FOX_HEAD_DIM), lambda b, h, s, qi, ki: (b * nq + qi[s], h)),
        scratch_shapes=[pltpu.VMEM((t, LANES), F32), pltpu.VMEM((t, 2 * FOX_HEAD_DIM), F32),
                        pltpu.VMEM((t, 2 * FOX_HEAD_DIM), BF16)],
    )
    return pl.pallas_call(
        functools.partial(_fox_kernel, t=t, ts=ts, ts_diag=ts_diag),
        grid_spec=grid_spec,
        out_shape=jax.ShapeDtypeStruct((m, FOX_WIDTH), BF16),
        compiler_params=_cparams(3),
        name="fox_attention",
    )(qi_tab, ki_tab, qa, ka, vb)


def _fox_decode_kernel(q_ref, kn_ref, vn_ref, kc_ref, vc_ref, cq_ref, ckc_ref, ckn_ref, o_ref,
                       m_ref, l_ref, acc_ref, *, lq, ncb):
    j = pl.program_id(1)
    nh, hd = FOX_HEADS, FOX_HEAD_DIM

    @pl.when(j == 0)
    def _():
        m_ref[...] = jnp.full_like(m_ref, NEG_INF)
        l_ref[...] = jnp.zeros_like(l_ref)
        acc_ref[...] = jnp.zeros_like(acc_ref)

    def attend(k_head, v_head, ck_head, causal):
        for h in range(nh):
            qh = q_ref[:, h * hd:(h + 1) * hd].astype(BF16)
            ck = ck_head(h)
            tk = ck.shape[1]
            s = _dot_nt(qh, k_head(h).astype(BF16)) * (hd ** -0.5)
            s = s + (_widen(cq_ref[h], tk) - ck)
            if causal:
                rows = lax.broadcasted_iota(jnp.int32, (lq, tk), 0)
                cols = lax.broadcasted_iota(jnp.int32, (lq, tk), 1)
                s = jnp.where(cols <= rows, s, NEG_INF)
            m_old = m_ref[h]
            m_new = jnp.maximum(m_old, jnp.max(s, axis=1, keepdims=True))
            alpha = jnp.exp(m_old - m_new)
            p = jnp.exp(s - _widen(m_new, tk))
            l_ref[h] = alpha * l_ref[h] + jnp.sum(p, axis=1, keepdims=True)
            acc_ref[h] = alpha * acc_ref[h] + _dot(p.astype(BF16), v_head(h).astype(BF16))
            m_ref[h] = m_new

    @pl.when(j < ncb)
    def _():
        k_hm = pltpu.einshape("mhd->hmd", kc_ref[0])
        v_hm = pltpu.einshape("mhd->hmd", vc_ref[0])
        attend(lambda h: k_hm[h], lambda h: v_hm[h], lambda h: ckc_ref[h], False)

    @pl.when(j == ncb)
    def _():
        attend(lambda h: kn_ref[:, h * hd:(h + 1) * hd], lambda h: vn_ref[:, h * hd:(h + 1) * hd],
               lambda h: ckn_ref[h][:, 0:lq], True)
        for h in range(nh):
            o_ref[:, h * hd:(h + 1) * hd] = (acc_ref[h] / l_ref[h]).astype(BF16)


def _fox_decode(proj, cache_k, cache_v, cq, ck_cache, ck_new, *, bsz, lq, tk):
    past = cache_k.shape[1]
    assert past % tk == 0 and tk % LANES == 0 and lq <= LANES
    ncb = past // tk
    nh, hd, w = FOX_HEADS, FOX_HEAD_DIM, FOX_WIDTH
    tile = lambda j: jnp.minimum(j, ncb - 1)
    cache_spec = pl.BlockSpec((1, tk, nh, hd), lambda b, j: (b, tile(j), 0, 0))
    return pl.pallas_call(
        functools.partial(_fox_decode_kernel, lq=lq, ncb=ncb),
        grid=(bsz, ncb + 1),
        in_specs=[pl.BlockSpec((lq, w), lambda b, j: (b, 0)),
                  pl.BlockSpec((lq, w), lambda b, j: (b, 1)),
                  pl.BlockSpec((lq, w), lambda b, j: (b, 2)),
                  cache_spec,
                  cache_spec,
                  pl.BlockSpec((nh, lq, LANES), lambda b, j: (b, 0, 0)),
                  pl.BlockSpec((nh, 1, tk), lambda b, j: (b, 0, tile(j))),
                  pl.BlockSpec((nh, 1, LANES), lambda b, j: (b, 0, 0))],
        out_specs=pl.BlockSpec((lq, w), lambda b, j: (b, 0)),
        out_shape=jax.ShapeDtypeStruct((bsz * lq, w), BF16),
        scratch_shapes=[pltpu.VMEM((nh, lq, LANES), F32), pltpu.VMEM((nh, lq, LANES), F32),
                        pltpu.VMEM((nh, lq, hd), F32)],
        compiler_params=_cparams(2),
        name="fox_decode",
    )(proj, proj, proj, cache_k, cache_v, cq, ck_cache, ck_new)


def _sconv_kernel(u_ref, bg_ref, cg_ref, cw_ref, hist_ref, y_ref, tail_ref, carry_ref, ext_ref,
                  *, tm, rows, spt, tpb):
    i = pl.program_id(0)
    w = cg_ref[...] * u_ref[...]
    first = (i % tpb) == 0
    for s in range(spt):
        w_s = w[s * rows:(s + 1) * rows]
        if tpb == 1:
            prev = hist_ref[s]
        else:
            prev = jnp.where(first, hist_ref[s], carry_ref[...])
        conv = _conv_rows(ext_ref, w_s, prev, cw_ref, SC_WIDTH, rows)
        y_ref[s * rows:(s + 1) * rows, :] = (bg_ref[s * rows:(s + 1) * rows, :] * conv).astype(BF16)
        tail_ref[s] = w_s[rows - SUBLANES:rows]
    if tpb > 1:
        carry_ref[...] = w[tm - SUBLANES:tm]


def _sconv(proj, conv_w, hist8, *, seq_len, tm):
    m = proj.shape[0]
    assert m % tm == 0
    rows, spt, tpb = _seq_tiling(seq_len, tm)
    nm = m // tm
    hist_map = (lambda i: (i // tpb, 0, 0)) if spt == 1 else (lambda i: (i, 0, 0))
    base = 3 * FOX_WIDTH // SC_DIM
    return pl.pallas_call(
        functools.partial(_sconv_kernel, tm=tm, rows=rows, spt=spt, tpb=tpb),
        grid=(nm,),
        in_specs=[pl.BlockSpec((tm, SC_DIM), lambda i: (i, base)),
                  pl.BlockSpec((tm, SC_DIM), lambda i: (i, base + 1)),
                  pl.BlockSpec((tm, SC_DIM), lambda i: (i, base + 2)),
                  pl.BlockSpec((SC_WIDTH, SC_DIM), lambda i: (0, 0)),
                  pl.BlockSpec((spt, SUBLANES, SC_DIM), hist_map)],
        out_specs=[pl.BlockSpec((tm, SC_DIM), lambda i: (i, 0)),
                   pl.BlockSpec((spt, SUBLANES, SC_DIM), lambda i: (i, 0, 0))],
        out_shape=[jax.ShapeDtypeStruct((m, SC_DIM), BF16),
                   jax.ShapeDtypeStruct((nm * spt, SUBLANES, SC_DIM), F32)],
        scratch_shapes=[pltpu.VMEM((SUBLANES, SC_DIM), F32),
                        pltpu.VMEM((rows + SUBLANES, SC_DIM), F32)],
        compiler_params=_cparams(1),
        name="gated_short_conv",
    )(proj, proj, proj, conv_w, hist8)


def _hist8(state):
    n, w1, c = state.shape
    return jnp.concatenate([jnp.zeros((n, SUBLANES - w1, c), F32), state.astype(F32)], axis=1)


def _tails(tails, n_seq, seq_len, tile_rows, keep):
    per_seq = max(1, seq_len // tile_rows)
    idx = (jnp.arange(n_seq) + 1) * per_seq - 1
    return tails[idx][:, SUBLANES - keep:, :]


def _rope_tables(pos0, length):
    half = RET_DK // 2
    inv = ROPE_BASE ** (-np.arange(half, dtype=np.float64) / half)
    ang = (pos0 + np.arange(length, dtype=np.float64))[:, None] * inv[None, :]
    cos, sin = np.cos(ang), np.sin(ang)
    return (jnp.asarray(np.concatenate([cos, cos], axis=1), F32),
            jnp.asarray(np.concatenate([-sin, sin], axis=1), F32))


def _prep_weights(p):
    d = D_MODEL
    ab_in = p['ab_w_in'][0]
    cd_in = p['cd_w_in'][0]
    f0 = 3 * FOX_WIDTH
    pad_cols = lambda w: jnp.pad(w, ((0, 0), (0, LANES - w.shape[1]))).astype(BF16)
    return dict(
        ab_in=ab_in.astype(BF16),
        ab_small=pad_cols(ab_in[:, AB_MAIN:]),
        cd_in=jnp.concatenate([cd_in[:, :f0], cd_in[:, f0 + FOX_HEADS:]], axis=1).astype(BF16),
        cd_small=pad_cols(cd_in[:, f0:f0 + FOX_HEADS]),
        ab_out=p['ab_w_out'][0].astype(BF16),
        cd_out=p['cd_w_out'][0].astype(BF16),
        ffn_gate=p['ffn_w_gate'].astype(BF16),
        ffn_up=p['ffn_w_up'].astype(BF16),
        ffn_down=p['ffn_w_down'].astype(BF16),
    )


def _trunk(x, pos0, st_ret, st_ssd, st_ssd_conv, c_k, c_v, c_logf, st_sconv, st_ffn, p, wb, t):
    bsz, length, d = x.shape
    m = bsz * length
    xf = x.reshape(m, d)
    zeros = lambda *shape: jnp.zeros(shape, F32)

    proj, dt_proj = _norm_matmul(xf, p['ab_norm_w'][0], wb['ab_in'], wb['ab_small'],
                                 n=AB_MAIN, tm=t['tm_ab'], tn=t['tn_ab'])
    cosf, sinf = _rope_tables(pos0, length)
    ret_state = zeros(bsz, RET_HEADS, RET_DK, RET_DV) if st_ret is None else st_ret
    y_ret, ret_new = _retention(proj, cosf, sinf, ret_state, p['ret_norm_w'][0],
                                bsz=bsz, seq_len=length, c=t['c_ret'])
    ssd_state = zeros(bsz, SSD_HEADS, SSD_HEADDIM, SSD_DSTATE) if st_ssd is None else st_ssd
    ssd_hist = zeros(bsz, SSD_CONV - 1, SSD_CONV_DIM) if st_ssd_conv is None else st_ssd_conv
    y_ssd, ssd_new = _ssd(proj, dt_proj, _hist8(ssd_hist), ssd_state, p['ssd_conv_w'][0], p['ssd_conv_b'][0],
                          p['ssd_dt_bias'][0], p['ssd_A_log'][0], p['ssd_D'][0], p['ssd_norm_w'][0],
                          bsz=bsz, seq_len=length, c=t['c_ssd'])
    xbc_lo = AB_MAIN - SSD_CONV_DIM
    ssd_conv_new = proj.reshape(bsz, length, -1)[:, length - (SSD_CONV - 1):, xbc_lo:AB_MAIN]
    xf = _proj_residual(xf, y_ret, y_ssd, wb['ab_out'], tm=t['tm_out'], tn=t['tn_out'])

    ffn_new = []
    ffn_hist0 = zeros(bsz, FFN_CONV - 1, D_FF) if st_ffn is None else st_ffn[0]
    xf, tails = _conv_ffn(xf, p['ffn_norm_w'][0], wb['ffn_gate'], wb['ffn_up'], wb['ffn_down'],
                          p['ffn_conv_w'][0], p['ffn_conv_b'][0], _hist8(ffn_hist0), p['final_norm_w'],
                          layer=0, seq_len=length, tm=t['tm_ffn'], tf=t['tf_ffn'], ts=t['ts_ffn'], final=False)
    ffn_new.append(_tails(tails, bsz, length, t['tm_ffn'], FFN_CONV - 1))

    proj, fl_proj = _norm_matmul(xf, p['cd_norm_w'][0], wb['cd_in'], wb['cd_small'],
                                 n=CD_MAIN, tm=t['tm_cd'], tn=t['tn_cd'])
    f_bias = jnp.pad(p['fox_f_bias'][0].astype(F32), (0, LANES - FOX_HEADS)).reshape(1, LANES)
    head_shape = (bsz, length, FOX_HEADS, FOX_HEAD_DIM)
    if c_k is None:
        qa, ka, vb, k32, v32, logf = _fox_prep(proj, fl_proj, f_bias, bsz=bsz, seq_len=length, tp=t['t_prep'])
        y_fox = _fox_prompt(qa, ka, vb, bsz=bsz, seq_len=length, t=t['t_fox'], ts=t['ts_fox'],
                            ts_diag=t['ts_fox_diag'])
        logf_new = logf.reshape(bsz, length, LANES)[:, :, :FOX_HEADS]
        k_new, v_new = k32.reshape(head_shape), v32.reshape(head_shape)
    else:
        proj3 = proj.reshape(bsz, length, -1)
        k_new = proj3[:, :, FOX_WIDTH:2 * FOX_WIDTH].reshape(head_shape)
        v_new = proj3[:, :, 2 * FOX_WIDTH:3 * FOX_WIDTH].reshape(head_shape)
        past = c_k.shape[1]
        pairs = bsz * FOX_HEADS
        assert pairs <= LANES and length <= LANES
        to_lanes = lambda a, rows: jnp.pad(jnp.swapaxes(a, 0, 1).reshape(a.shape[1], pairs),
                                           ((0, rows - a.shape[1]), (0, LANES - pairs)))
        from_lanes = lambda a: jnp.swapaxes(a[:length, :pairs].reshape(length, bsz, FOX_HEADS), 0, 1)
        cache_lf = to_lanes(c_logf.astype(F32), past)
        fl_rows = to_lanes(fl_proj.reshape(bsz, length, LANES)[:, :, :FOX_HEADS], LANES)
        bias_lanes = jnp.pad(jnp.tile(p['fox_f_bias'][0].astype(F32), bsz), (0, LANES - pairs)).reshape(1, LANES)
        cum_t_cache, lf_rows, cum_rows, cum_t_new = _decode_cum(cache_lf, fl_rows, bias_lanes, c=t['c_cum'])
        logf_new = from_lanes(lf_rows)
        cq = jnp.broadcast_to(cum_rows[:length, :pairs].T[:, :, None], (pairs, length, LANES))
        y_fox = _fox_decode(proj, c_k, c_v, cq, cum_t_cache[:pairs, None, :], cum_t_new[:pairs, None, :],
                            bsz=bsz, lq=length, tk=t['tk_dec'])
    sc_hist = zeros(bsz, SC_WIDTH - 1, SC_DIM) if st_sconv is None else st_sconv
    y_sc, sc_tails = _sconv(proj, p['sconv_w'][0], _hist8(sc_hist), seq_len=length, tm=t['tm_sc'])
    sconv_new = _tails(sc_tails, bsz, length, t['tm_sc'], SC_WIDTH - 1)
    xf = _proj_residual(xf, y_fox, y_sc, wb['cd_out'], tm=t['tm_out'], tn=t['tn_out'])

    ffn_hist1 = zeros(bsz, FFN_CONV - 1, D_FF) if st_ffn is None else st_ffn[1]
    xf, tails = _conv_ffn(xf, p['ffn_norm_w'][1], wb['ffn_gate'], wb['ffn_up'], wb['ffn_down'],
                          p['ffn_conv_w'][1], p['ffn_conv_b'][1], _hist8(ffn_hist1), p['final_norm_w'],
                          layer=1, seq_len=length, tm=t['tm_ffn'], tf=t['tf_ffn'], ts=t['ts_ffn'], final=True)
    ffn_new.append(_tails(tails, bsz, length, t['tm_ffn'], FFN_CONV - 1))

    return (xf.reshape(bsz, length, d), ret_new[None], ssd_new[None], ssd_conv_new[None], k_new[None],
            v_new[None], logf_new[None], sconv_new[None], jnp.stack(ffn_new))


def _largest_divisor(n, cap, multiple=1):
    best = None
    for cand in range(multiple, min(n, cap) + 1, multiple):
        if n % cand == 0:
            best = cand
    assert best is not None, (n, cap, multiple)
    return best


def _tiles(bsz, length, past=None):
    m = bsz * length
    seq_tile = lambda cap: _largest_divisor(length, cap, SUBLANES)
    row_tile = lambda cap: (_largest_divisor(length, cap, SUBLANES) if length >= cap
                            else _largest_divisor(m, cap, length))
    t = dict(
        tm_ab=row_tile(1024), tn_ab=2048, tm_cd=row_tile(1024), tn_cd=2048,
        tm_out=row_tile(1024), tn_out=2048,
        tm_ffn=row_tile(1024), tf_ffn=512, ts_ffn=512,
        tm_sc=row_tile(512),
        c_ret=seq_tile(256), c_ssd=seq_tile(256),
    )
    if past is None:
        t['t_fox'] = seq_tile(2048)
        t['ts_fox'] = _largest_divisor(t['t_fox'], 256, LANES)
        t['ts_fox_diag'] = _largest_divisor(t['t_fox'], 512, LANES)
        t['t_prep'] = seq_tile(512)
    else:
        t['tk_dec'] = _largest_divisor(past, 1024, LANES)
        t['c_cum'] = _largest_divisor(past, 256, LANES)
    return t


def kernel(x_prompt, x_sample, state_ret, state_ssd, state_ssd_conv, cache_fox_k, cache_fox_v, cache_fox_logf, state_sconv, state_ffn_conv, ab_norm_w, ab_w_in, ret_norm_w, ssd_conv_w, ssd_conv_b, ssd_dt_bias, ssd_A_log, ssd_D, ssd_norm_w, ab_w_out, cd_norm_w, cd_w_in, fox_f_bias, sconv_w, cd_w_out, ffn_norm_w, ffn_w_gate, ffn_w_up, ffn_conv_w, ffn_conv_b, ffn_w_down, final_norm_w):
    p = dict(ab_norm_w=ab_norm_w, ab_w_in=ab_w_in, ret_norm_w=ret_norm_w, ssd_conv_w=ssd_conv_w,
             ssd_conv_b=ssd_conv_b, ssd_dt_bias=ssd_dt_bias, ssd_A_log=ssd_A_log, ssd_D=ssd_D,
             ssd_norm_w=ssd_norm_w, ab_w_out=ab_w_out, cd_norm_w=cd_norm_w, cd_w_in=cd_w_in,
             fox_f_bias=fox_f_bias, sconv_w=sconv_w, cd_w_out=cd_w_out, ffn_norm_w=ffn_norm_w,
             ffn_w_gate=ffn_w_gate, ffn_w_up=ffn_w_up, ffn_conv_w=ffn_conv_w, ffn_conv_b=ffn_conv_b,
             ffn_w_down=ffn_w_down, final_norm_w=final_norm_w)
    assert x_prompt.shape[-1] == D_MODEL and ab_w_in.shape == (1, D_MODEL, AB_MAIN + SSD_HEADS)
    assert cd_w_in.shape == (1, D_MODEL, CD_MAIN + FOX_HEADS) and ffn_w_gate.shape == (2, D_MODEL, D_FF)
    wb = _prep_weights(p)
    bp, lp_, _ = x_prompt.shape
    bs, ls, _ = x_sample.shape
    past = cache_fox_k.shape[2]
    (y_prompt, p_ret, p_ssd, p_ssd_conv, p_fox_k, p_fox_v, p_fox_logf, p_sconv, p_ffn_conv) = _trunk(
        x_prompt, 0, None, None, None, None, None, None, None, None, p, wb, _tiles(bp, lp_))
    (y_sample, s_ret, s_ssd, s_ssd_conv, s_fox_k, s_fox_v, s_fox_logf, s_sconv, s_ffn_conv) = _trunk(
        x_sample, past, state_ret[0], state_ssd[0], state_ssd_conv[0], cache_fox_k[0], cache_fox_v[0],
        cache_fox_logf[0], state_sconv[0], state_ffn_conv, p, wb, _tiles(bs, ls, past))
    return (y_prompt, y_sample, p_ret, s_ret, p_ssd, s_ssd, p_ssd_conv, s_ssd_conv, p_fox_k, s_fox_k,
            p_fox_v, s_fox_v, p_fox_logf, s_fox_logf, p_sconv, s_sconv, p_ffn_conv, s_ffn_conv)
```

```python
import functools
import math

import numpy as np
import jax
import jax.numpy as jnp
from jax import lax
from jax.experimental import pallas as pl
from jax.experimental.pallas import tpu as pltpu

F32 = jnp.float32
BF16 = jnp.bfloat16
EPS = 1e-6
ROPE_BASE = 10000.0
NEG_INF = float("-inf")

D_MODEL = 2048
RET_HEADS, RET_DK, RET_DV = 4, 128, 256
SSD_DINNER, SSD_HEADDIM, SSD_HEADS, SSD_GROUPS, SSD_DSTATE, SSD_CONV = 1024, 64, 16, 2, 128, 4
SSD_CONV_DIM = SSD_DINNER + 2 * SSD_GROUPS * SSD_DSTATE
FOX_HEADS, FOX_HEAD_DIM = 8, 128
FOX_WIDTH = FOX_HEADS * FOX_HEAD_DIM
SC_DIM, SC_WIDTH = 1024, 3
D_FF, FFN_CONV = 5632, 3
AB_MAIN = 2 * RET_HEADS * RET_DK + 2 * RET_HEADS * RET_DV + SSD_DINNER + SSD_CONV_DIM
AB_PAD = AB_MAIN + 128
CD_MAIN = 3 * FOX_WIDTH + 3 * SC_DIM
CD_PAD = CD_MAIN + 128

LANES = 128
SUBLANES = 8
VMEM_LIMIT = 60 * 1024 * 1024


def _cparams(n_axes):
    return pltpu.CompilerParams(dimension_semantics=("arbitrary",) * n_axes,
                                vmem_limit_bytes=VMEM_LIMIT)


def _rms(xf, w):
    return xf * lax.rsqrt(jnp.mean(xf * xf, axis=-1, keepdims=True) + EPS) * w


def _softplus(x):
    return jnp.maximum(x, 0.0) + jnp.log1p(jnp.exp(-jnp.abs(x)))


def _split3(x):
    hi = x.astype(BF16)
    r1 = x - hi.astype(F32)
    mid = r1.astype(BF16)
    lo = (r1 - mid.astype(F32)).astype(BF16)
    return hi, mid, lo


def _widen(x, n):
    return x[:, 0:n] if n <= LANES else jnp.concatenate([x] * (n // LANES), axis=1)


def _dot(a, b):
    return jnp.dot(a, b, preferred_element_type=F32)


def _dot_nt(a, b):
    return lax.dot_general(a, b, (((1,), (1,)), ((), ())), preferred_element_type=F32)


def _dot_tn(a, b):
    return lax.dot_general(a, b, (((0,), (0,)), ((), ())), preferred_element_type=F32)


def _exact_lhs_dot(m_bf16, x):
    hi, mid, lo = _split3(x)
    return _dot(m_bf16, hi) + _dot(m_bf16, mid) + _dot(m_bf16, lo)


def _exact_rhs_dot(x, m_bf16):
    hi, mid, lo = _split3(x)
    return _dot(hi, m_bf16) + _dot(mid, m_bf16) + _dot(lo, m_bf16)


def _conv_rows(ext_ref, x, prev8, w_ref, width, rows, w_cols=slice(None)):
    ext_ref[0:SUBLANES, :] = prev8
    ext_ref[SUBLANES:SUBLANES + rows, :] = x
    out = None
    for j in range(width):
        off = SUBLANES - (width - 1) + j
        term = ext_ref[off:off + rows, :] * w_ref[j:j + 1, w_cols]
        out = term if out is None else out + term
    return out


def _seq_tiling(seq_len, tile_rows):
    if seq_len >= tile_rows:
        assert seq_len % tile_rows == 0
        return tile_rows, 1, seq_len // tile_rows
    assert tile_rows % seq_len == 0 and seq_len % SUBLANES == 0
    return seq_len, tile_rows // seq_len, 1


def _norm_matmul_kernel(x_ref, nw_ref, w_ref, ws_ref, o_ref, os_ref, h_ref, *, n_tiles, tn, last_w):
    j = pl.program_id(1)

    @pl.when(j == 0)
    def _():
        h_ref[...] = _rms(x_ref[...], nw_ref[...]).astype(BF16)
        os_ref[...] = _dot(h_ref[...], ws_ref[...])

    if last_w == tn:
        o_ref[...] = _dot(h_ref[...], w_ref[...])
    else:
        @pl.when(j < n_tiles - 1)
        def _():
            o_ref[...] = _dot(h_ref[...], w_ref[...])

        @pl.when(j == n_tiles - 1)
        def _():
            o_ref[:, 0:last_w] = _dot(h_ref[...], w_ref[:, 0:last_w])


def _norm_matmul(x, norm_w, w, w_small, *, n, tm, tn):
    m, d = x.shape
    n_tiles = pl.cdiv(n, tn)
    last_w = n - (n_tiles - 1) * tn
    assert m % tm == 0 and last_w % LANES == 0 and n <= w.shape[1] and w_small.shape == (d, LANES)
    return pl.pallas_call(
        functools.partial(_norm_matmul_kernel, n_tiles=n_tiles, tn=tn, last_w=last_w),
        grid=(m // tm, n_tiles),
        in_specs=[pl.BlockSpec((tm, d), lambda i, j: (i, 0)),
                  pl.BlockSpec((1, d), lambda i, j: (0, 0)),
                  pl.BlockSpec((d, tn), lambda i, j: (0, j)),
                  pl.BlockSpec((d, LANES), lambda i, j: (0, 0))],
        out_specs=[pl.BlockSpec((tm, tn), lambda i, j: (i, j)),
                   pl.BlockSpec((tm, LANES), lambda i, j: (i, 0))],
        out_shape=[jax.ShapeDtypeStruct((m, n), F32),
                   jax.ShapeDtypeStruct((m, LANES), F32)],
        scratch_shapes=[pltpu.VMEM((tm, d), BF16)],
        compiler_params=_cparams(2),
        name="norm_in_proj",
    )(x, norm_w.reshape(1, d), w, w_small)


def _proj_res_kernel(x_ref, a_ref, b_ref, wa_ref, wb_ref, o_ref):
    acc = _dot(a_ref[...], wa_ref[...])
    acc = acc + _dot(b_ref[...], wb_ref[...])
    o_ref[...] = x_ref[...] + acc


def _proj_residual(x, a, b, w, *, tm, tn):
    m, d = x.shape
    ka, kb = a.shape[1], b.shape[1]
    assert m % tm == 0 and d % tn == 0 and ka == kb and w.shape == (ka + kb, d)
    return pl.pallas_call(
        _proj_res_kernel,
        grid=(m // tm, d // tn),
        in_specs=[pl.BlockSpec((tm, tn), lambda i, j: (i, j)),
                  pl.BlockSpec((tm, ka), lambda i, j: (i, 0)),
                  pl.BlockSpec((tm, kb), lambda i, j: (i, 0)),
                  pl.BlockSpec((ka, tn), lambda i, j: (0, j)),
                  pl.BlockSpec((kb, tn), lambda i, j: (1, j))],
        out_specs=pl.BlockSpec((tm, tn), lambda i, j: (i, j)),
        out_shape=jax.ShapeDtypeStruct((m, d), F32),
        compiler_params=_cparams(2),
        name="out_proj_residual",
    )(x, a, b, w, w)


def _ffn_kernel(x_ref, nw_ref, wg_ref, wu_ref, wd_ref, cw_ref, cb_ref, hist_ref, fw_ref,
                o_ref, tail_ref, h_ref, carry_ref, ext_ref,
                *, tm, rows, spt, tpb, nf, ts, nsub, nsub_last, final):
    i = pl.program_id(0)
    f = pl.program_id(1)

    @pl.when(f == 0)
    def _():
        xf = x_ref[...]
        h_ref[...] = _rms(xf, nw_ref[...]).astype(BF16)
        o_ref[...] = xf

    first = (i % tpb) == 0

    def sub_block(sb):
        cols = slice(sb * ts, (sb + 1) * ts)
        h = h_ref[...]
        a = _dot(h, wg_ref[:, cols])
        u = _dot(h, wu_ref[:, cols])
        convs = []
        for s in range(spt):
            a_s = a[s * rows:(s + 1) * rows]
            if tpb == 1:
                prev = hist_ref[s, :, cols]
            else:
                prev = jnp.where(first, hist_ref[s, :, cols], carry_ref[f * nsub + sb])
            convs.append(_conv_rows(ext_ref, a_s, prev, cw_ref, FFN_CONV, rows, cols))
            tail_ref[s, :, cols] = a_s[rows - SUBLANES:rows]
        if tpb > 1:
            carry_ref[f * nsub + sb] = a[tm - SUBLANES:tm]
        conv = convs[0] if spt == 1 else jnp.concatenate(convs, axis=0)
        act = (jax.nn.silu(conv + cb_ref[:, cols]) * u).astype(BF16)
        o_ref[...] += _dot(act, wd_ref[cols, :])

    if nsub_last == nsub:
        for sb in range(nsub):
            sub_block(sb)
    else:
        @pl.when(f < nf - 1)
        def _():
            for sb in range(nsub):
                sub_block(sb)

        @pl.when(f == nf - 1)
        def _():
            for sb in range(nsub_last):
                sub_block(sb)

    if final:
        @pl.when(f == nf - 1)
        def _():
            o_ref[...] = _rms(o_ref[...], fw_ref[...])


def _conv_ffn(x, norm_w, wg, wu, wd, conv_w, conv_b, hist8, final_w, *, layer, seq_len, tm, tf, ts, final):
    m, d = x.shape
    ff = wg.shape[2]
    assert m % tm == 0 and tf % ts == 0 and ff % ts == 0
    rows, spt, tpb = _seq_tiling(seq_len, tm)
    nm, nf = m // tm, pl.cdiv(ff, tf)
    nsub = tf // ts
    nsub_last = (ff - (nf - 1) * tf) // ts
    hist_map = (lambda i, f: (i // tpb, 0, f)) if spt == 1 else (lambda i, f: (i, 0, f))
    kern = functools.partial(_ffn_kernel, tm=tm, rows=rows, spt=spt, tpb=tpb, nf=nf, ts=ts, nsub=nsub,
                             nsub_last=nsub_last, final=final)
    out, tails = pl.pallas_call(
        kern,
        grid=(nm, nf),
        in_specs=[pl.BlockSpec((tm, d), lambda i, f: (i, 0)),
                  pl.BlockSpec((1, d), lambda i, f: (0, 0)),
                  pl.BlockSpec((None, d, tf), lambda i, f: (layer, 0, f)),
                  pl.BlockSpec((None, d, tf), lambda i, f: (layer, 0, f)),
                  pl.BlockSpec((None, tf, d), lambda i, f: (layer, f, 0)),
                  pl.BlockSpec((FFN_CONV, tf), lambda i, f: (0, f)),
                  pl.BlockSpec((1, tf), lambda i, f: (0, f)),
                  pl.BlockSpec((spt, SUBLANES, tf), hist_map),
                  pl.BlockSpec((1, d), lambda i, f: (0, 0))],
        out_specs=[pl.BlockSpec((tm, d), lambda i, f: (i, 0)),
                   pl.BlockSpec((spt, SUBLANES, tf), lambda i, f: (i, 0, f))],
        out_shape=[jax.ShapeDtypeStruct((m, d), F32),
                   jax.ShapeDtypeStruct((nm * spt, SUBLANES, ff), F32)],
        scratch_shapes=[pltpu.VMEM((tm, d), BF16),
                        pltpu.VMEM((nf * nsub, SUBLANES, ts), F32),
                        pltpu.VMEM((rows + SUBLANES, ts), F32)],
        compiler_params=_cparams(2),
        name="conv_ffn",
    )(x, norm_w.reshape(1, d), wg, wu, wd, conv_w, conv_b.reshape(1, ff), hist8, final_w.reshape(1, d))
    return out, tails


def _retention_kernel(q_ref, k_ref, v_ref, g_ref, cos_ref, sin_ref, st_ref, nw_ref,
                      y_ref, so_ref, *, c):
    ci = pl.program_id(1)

    @pl.when(ci == 0)
    def _():
        so_ref[...] = st_ref[...]

    cos = cos_ref[...]
    sin = sin_ref[...]
    ii = lax.broadcasted_iota(jnp.int32, (c, c), 0)
    jj = lax.broadcasted_iota(jnp.int32, (c, c), 1)
    diff = (ii - jj).astype(F32)
    causal = ii >= jj
    ridx = lax.broadcasted_iota(jnp.int32, (c, 1), 0).astype(F32)
    for h in range(RET_HEADS):
        lg = math.log1p(-(2.0 ** (-5.0 - h)))
        q = q_ref[:, h * RET_DK:(h + 1) * RET_DK]
        k = k_ref[:, h * RET_DK:(h + 1) * RET_DK]
        v = v_ref[:, h * RET_DV:(h + 1) * RET_DV]
        qr = q * cos + pltpu.roll(q, RET_DK // 2, 1) * sin
        kr = (k * cos + pltpu.roll(k, RET_DK // 2, 1) * sin) * (RET_DK ** -0.5)
        qb = qr.astype(BF16)
        kb = kr.astype(BF16)
        vb = v.astype(BF16)
        decay = jnp.exp(jnp.where(causal, diff * lg, NEG_INF))
        inner = jnp.exp((ridx + 1.0) * lg)
        sdecay = jnp.exp((c - 1.0 - ridx) * lg)
        s = so_ref[0, h]
        scores = _dot_nt(qb, kb) * decay
        y = _dot(scores.astype(BF16), vb)
        y = y + _dot(qb, s.astype(BF16)) * inner
        kd = (kr * sdecay).astype(BF16)
        so_ref[0, h] = math.exp(c * lg) * s + _dot_tn(kd, vb)
        mu = jnp.mean(y, axis=-1, keepdims=True)
        yc = y - mu
        var = jnp.mean(yc * yc, axis=-1, keepdims=True)
        yn = yc * lax.rsqrt(var + EPS) * nw_ref[:, h * RET_DV:(h + 1) * RET_DV]
        g = g_ref[:, h * RET_DV:(h + 1) * RET_DV]
        y_ref[:, h * RET_DV:(h + 1) * RET_DV] = (jax.nn.silu(g) * yn).astype(BF16)


def _retention(proj, cosf, sinf, state, norm_w, *, bsz, seq_len, c):
    m = proj.shape[0]
    nc = seq_len // c
    assert seq_len % c == 0
    qk_w = RET_HEADS * RET_DK
    v_w = RET_HEADS * RET_DV
    row = lambda b, ci: b * nc + ci
    y, s_new = pl.pallas_call(
        functools.partial(_retention_kernel, c=c),
        grid=(bsz, nc),
        in_specs=[pl.BlockSpec((c, qk_w), lambda b, ci: (row(b, ci), 0)),
                  pl.BlockSpec((c, qk_w), lambda b, ci: (row(b, ci), 1)),
                  pl.BlockSpec((c, v_w), lambda b, ci: (row(b, ci), 1)),
                  pl.BlockSpec((c, v_w), lambda b, ci: (row(b, ci), 2)),
                  pl.BlockSpec((c, RET_DK), lambda b, ci: (ci, 0)),
                  pl.BlockSpec((c, RET_DK), lambda b, ci: (ci, 0)),
                  pl.BlockSpec((1, RET_HEADS, RET_DK, RET_DV), lambda b, ci: (b, 0, 0, 0)),
                  pl.BlockSpec((1, v_w), lambda b, ci: (0, 0))],
        out_specs=[pl.BlockSpec((c, v_w), lambda b, ci: (row(b, ci), 0)),
                   pl.BlockSpec((1, RET_HEADS, RET_DK, RET_DV), lambda b, ci: (b, 0, 0, 0))],
        out_shape=[jax.ShapeDtypeStruct((m, v_w), BF16),
                   jax.ShapeDtypeStruct(state.shape, F32)],
        compiler_params=_cparams(2),
        name="retention",
    )(proj, proj, proj, proj, cosf, sinf, state, norm_w.reshape(1, v_w))
    return y, s_new


def _ssd_kernel(z_ref, xs_ref, bc_ref, dt_ref, hx_ref, hbc_ref, st_ref,
                cwx_ref, cwbc_ref, cbx_ref, cbbc_ref, dtb_ref, alog_ref, dsk_ref, nw_ref,
                tri_ref, exp_ref,
                y_ref, so_ref,
                st_scr, cx_scr, cbc_scr, extx_scr, extbc_scr, yh_scr, xs_scr, *, c, nc):
    ci = pl.program_id(1)
    gw = SSD_DINNER // SSD_GROUPS
    hpg = SSD_HEADS // SSD_GROUPS

    @pl.when(ci == 0)
    def _():
        st_scr[...] = st_ref[0].T
        cx_scr[...] = hx_ref[0]
        cbc_scr[...] = hbc_ref[0]

    xs_raw = xs_ref[...]
    bc_raw = bc_ref[...]
    xs_scr[...] = jax.nn.silu(_conv_rows(extx_scr, xs_raw, cx_scr[...], cwx_ref, SSD_CONV, c) + cbx_ref[...])
    bcm = jax.nn.silu(_conv_rows(extbc_scr, bc_raw, cbc_scr[...], cwbc_ref, SSD_CONV, c) + cbbc_ref[...])
    cx_scr[...] = xs_raw[c - SUBLANES:c]
    cbc_scr[...] = bc_raw[c - SUBLANES:c]

    tri = tri_ref[...]
    dt = _softplus(dt_ref[...] + dtb_ref[...])
    a = -jnp.exp(alog_ref[...])
    acs = _exact_lhs_dot(tri, dt * a)
    acs_t = acs.T
    acs_last = acs[c - 1:c, :]
    exp_acs = jnp.exp(acs)
    to_end = jnp.exp(acs_last - acs)
    chunk_dec = jnp.exp(acs_last)

    ii = lax.broadcasted_iota(jnp.int32, (c, c), 0)
    jj = lax.broadcasted_iota(jnp.int32, (c, c), 1)
    causal = ii >= jj
    nb = SSD_GROUPS * SSD_DSTATE
    for g in range(SSD_GROUPS):
        cols = slice(g * gw, (g + 1) * gw)
        expand = exp_ref[:, cols]
        xdt = xs_scr[:, cols] * _exact_rhs_dot(dt, expand)
        xdt_b = xdt.astype(BF16)
        xend_b = (xdt * _exact_rhs_dot(to_end, expand)).astype(BF16)
        b_g = bcm[:, g * SSD_DSTATE:(g + 1) * SSD_DSTATE].astype(BF16)
        c_g = bcm[:, nb + g * SSD_DSTATE:nb + (g + 1) * SSD_DSTATE].astype(BF16)
        cb = _dot_nt(c_g, b_g)
        s_g = st_scr[:, cols]
        y_state = _dot(c_g, s_g.astype(BF16)) * _exact_rhs_dot(exp_acs, expand)
        for r in range(hpg):
            hh = g * hpg + r
            seg = acs[:, hh:hh + 1] - acs_t[hh:hh + 1, :]
            lmat = jnp.exp(jnp.where(causal, seg, NEG_INF))
            mm = (cb * lmat).astype(BF16)
            head = slice(r * SSD_HEADDIM, (r + 1) * SSD_HEADDIM)
            yh_scr[:, hh * SSD_HEADDIM:(hh + 1) * SSD_HEADDIM] = _dot(mm, xdt_b[:, head]) + y_state[:, head]
        upd = _dot_tn(b_g, xend_b)
        st_scr[:, cols] = _exact_rhs_dot(chunk_dec, expand) * s_g + upd

    y = yh_scr[...] + dsk_ref[...] * xs_scr[...]
    z = z_ref[...]
    y_ref[...] = _rms(y * jax.nn.silu(z), nw_ref[...]).astype(BF16)

    @pl.when(ci == nc - 1)
    def _():
        so_ref[0] = st_scr[...].T


def _ssd(proj, dt_proj, hist8, state, conv_w, conv_b, dt_bias, a_log, d_skip, norm_w, *, bsz, seq_len, c):
    m = proj.shape[0]
    nc = seq_len // c
    assert seq_len % c == 0
    row = lambda b, ci: b * nc + ci
    const2 = lambda b, ci: (0, 0)
    di, bcw = SSD_DINNER, 2 * SSD_GROUPS * SSD_DSTATE
    tri = jnp.asarray(np.tril(np.ones((c, c), np.float32)), BF16)
    expand = np.zeros((LANES, di), np.float32)
    for h in range(SSD_HEADS):
        expand[h, h * SSD_HEADDIM:(h + 1) * SSD_HEADDIM] = 1.0
    expand = jnp.asarray(expand, BF16)
    pad_row = lambda v: jnp.pad(v.astype(F32), (0, LANES - v.shape[0])).reshape(1, LANES)
    st2 = state.reshape(bsz, di, SSD_DSTATE)
    y, s_new = pl.pallas_call(
        functools.partial(_ssd_kernel, c=c, nc=nc),
        grid=(bsz, nc),
        in_specs=[pl.BlockSpec((c, di), lambda b, ci: (row(b, ci), 3)),
                  pl.BlockSpec((c, di), lambda b, ci: (row(b, ci), 4)),
                  pl.BlockSpec((c, bcw), lambda b, ci: (row(b, ci), 10)),
                  pl.BlockSpec((c, LANES), lambda b, ci: (row(b, ci), 0)),
                  pl.BlockSpec((1, SUBLANES, di), lambda b, ci: (b, 0, 0)),
                  pl.BlockSpec((1, SUBLANES, bcw), lambda b, ci: (b, 0, 2)),
                  pl.BlockSpec((1, di, SSD_DSTATE), lambda b, ci: (b, 0, 0)),
                  pl.BlockSpec((SSD_CONV, di), const2),
                  pl.BlockSpec((SSD_CONV, bcw), lambda b, ci: (0, 2)),
                  pl.BlockSpec((1, di), const2),
                  pl.BlockSpec((1, bcw), lambda b, ci: (0, 2)),
                  pl.BlockSpec((1, LANES), const2),
                  pl.BlockSpec((1, LANES), const2),
                  pl.BlockSpec((1, di), const2),
                  pl.BlockSpec((1, di), const2),
                  pl.BlockSpec((c, c), const2),
                  pl.BlockSpec((LANES, di), const2)],
        out_specs=[pl.BlockSpec((c, di), lambda b, ci: (row(b, ci), 0)),
                   pl.BlockSpec((1, di, SSD_DSTATE), lambda b, ci: (b, 0, 0))],
        out_shape=[jax.ShapeDtypeStruct((m, di), BF16),
                   jax.ShapeDtypeStruct(st2.shape, F32)],
        scratch_shapes=[pltpu.VMEM((SSD_DSTATE, di), F32),
                        pltpu.VMEM((SUBLANES, di), F32),
                        pltpu.VMEM((SUBLANES, bcw), F32),
                        pltpu.VMEM((c + SUBLANES, di), F32),
                        pltpu.VMEM((c + SUBLANES, bcw), F32),
                        pltpu.VMEM((c, di), F32),
                        pltpu.VMEM((c, di), F32)],
        compiler_params=_cparams(2),
        name="ssd",
    )(proj, proj, proj, dt_proj, hist8, hist8, st2,
      conv_w, conv_w, conv_b.reshape(1, -1), conv_b.reshape(1, -1),
      pad_row(dt_bias), pad_row(a_log), jnp.repeat(d_skip.astype(F32), SSD_HEADDIM).reshape(1, di),
      norm_w.reshape(1, di), tri, expand)
    return y, s_new.reshape(state.shape)


def _decode_cum_kernel(lfc_ref, fl_ref, b_ref, tri_ref, cumt_c_ref, lfn_ref, cumn_ref, cumt_n_ref,
                       carry_ref, *, c, ncb):
    j = pl.program_id(0)

    @pl.when(j == 0)
    def _():
        carry_ref[...] = jnp.zeros_like(carry_ref)

    @pl.when(j < ncb)
    def _():
        cum = _exact_lhs_dot(tri_ref[...], lfc_ref[...]) + carry_ref[...]
        carry_ref[...] = cum[c - 1:c, :]
        cumt_c_ref[...] = cum.T

    @pl.when(j == ncb)
    def _():
        lf = -_softplus(-(fl_ref[...] + b_ref[...]))
        lfn_ref[...] = lf
        cum = _exact_lhs_dot(tri_ref[0:LANES, 0:LANES], lf) + carry_ref[...]
        cumn_ref[...] = cum
        cumt_n_ref[...] = cum.T


def _decode_cum(cache_lf, fl_new, bias, *, c):
    past = cache_lf.shape[0]
    assert past % c == 0 and c % LANES == 0 and fl_new.shape == (LANES, LANES)
    ncb = past // c
    tri = jnp.asarray(np.tril(np.ones((c, c), np.float32)), BF16)
    blk = lambda j: jnp.minimum(j, ncb - 1)
    sq = jax.ShapeDtypeStruct((LANES, LANES), F32)
    return pl.pallas_call(
        functools.partial(_decode_cum_kernel, c=c, ncb=ncb),
        grid=(ncb + 1,),
        in_specs=[pl.BlockSpec((c, LANES), lambda j: (blk(j), 0)),
                  pl.BlockSpec((LANES, LANES), lambda j: (0, 0)),
                  pl.BlockSpec((1, LANES), lambda j: (0, 0)),
                  pl.BlockSpec((c, c), lambda j: (0, 0))],
        out_specs=[pl.BlockSpec((LANES, c), lambda j: (0, blk(j))),
                   pl.BlockSpec((LANES, LANES), lambda j: (0, 0)),
                   pl.BlockSpec((LANES, LANES), lambda j: (0, 0)),
                   pl.BlockSpec((LANES, LANES), lambda j: (0, 0))],
        out_shape=[jax.ShapeDtypeStruct((LANES, past), F32), sq, sq, sq],
        scratch_shapes=[pltpu.VMEM((1, LANES), F32)],
        compiler_params=_cparams(1),
        name="decode_logf_cumsum",
    )(cache_lf, fl_new, bias, tri)


FOX_AUG = 2 * FOX_HEAD_DIM
N_BIAS_PIECES = 3


def _fox_prep_kernel(q_ref, k_ref, v_ref, fl_ref, fb_ref, tri_ref, place_ref, ones_ref,
                     qa_ref, ka_ref, vb_ref, k32_ref, v32_ref, lf_ref, carry_ref, *, tp):
    @pl.when(pl.program_id(1) == 0)
    def _():
        carry_ref[...] = jnp.zeros_like(carry_ref)

    lf = -_softplus(-(fl_ref[...] + fb_ref[...]))
    lf_ref[...] = lf
    cum = _exact_lhs_dot(tri_ref[...], lf) + carry_ref[...]
    carry_ref[...] = cum[tp - 1:tp, :]
    pieces = _split3(cum * (FOX_HEAD_DIM ** 0.5))
    n = N_BIAS_PIECES
    aug_q = ones_ref[0:1, :] + sum(_dot(pieces[r], place_ref[r]) for r in range(n))
    aug_k = ones_ref[1:2, :] - sum(_dot(pieces[r], place_ref[n + r]) for r in range(n))
    for h in range(FOX_HEADS):
        src = slice(h * FOX_HEAD_DIM, (h + 1) * FOX_HEAD_DIM)
        feat = slice(h * FOX_AUG, h * FOX_AUG + FOX_HEAD_DIM)
        bias = slice(h * FOX_AUG + FOX_HEAD_DIM, (h + 1) * FOX_AUG)
        qa_ref[:, feat] = q_ref[:, src].astype(BF16)
        qa_ref[:, bias] = aug_q[:, src].astype(BF16)
        ka_ref[:, feat] = k_ref[:, src].astype(BF16)
        ka_ref[:, bias] = aug_k[:, src].astype(BF16)
    k = k_ref[...]
    v = v_ref[...]
    k32_ref[...] = pltpu.einshape("m(hd)->mhd", k, h=FOX_HEADS)
    v32_ref[...] = pltpu.einshape("m(hd)->mhd", v, h=FOX_HEADS)
    vb_ref[...] = v.astype(BF16)


def _fox_prep(proj, fl_proj, f_bias, *, bsz, seq_len, tp):
    m = proj.shape[0]
    nt = seq_len // tp
    assert seq_len % tp == 0
    w = FOX_WIDTH
    tri = jnp.asarray(np.tril(np.ones((tp, tp), np.float32)), BF16)
    n = N_BIAS_PIECES
    place = np.zeros((2 * n, LANES, w), np.float32)
    ones = np.zeros((SUBLANES, w), np.float32)
    for h in range(FOX_HEADS):
        for r in range(2 * n):
            place[r, h, h * FOX_HEAD_DIM + r] = 1.0
        ones[0, h * FOX_HEAD_DIM + n:h * FOX_HEAD_DIM + 2 * n] = 1.0
        ones[1, h * FOX_HEAD_DIM:h * FOX_HEAD_DIM + n] = 1.0
    row = lambda b, ti: (b * nt + ti, 0)
    const2 = lambda b, ti: (0, 0)
    return pl.pallas_call(
        functools.partial(_fox_prep_kernel, tp=tp),
        grid=(bsz, nt),
        in_specs=[pl.BlockSpec((tp, w), lambda b, ti: (b * nt + ti, 0)),
                  pl.BlockSpec((tp, w), lambda b, ti: (b * nt + ti, 1)),
                  pl.BlockSpec((tp, w), lambda b, ti: (b * nt + ti, 2)),
                  pl.BlockSpec((tp, LANES), row),
                  pl.BlockSpec((1, LANES), const2),
                  pl.BlockSpec((tp, tp), const2),
                  pl.BlockSpec((2 * n, LANES, w), lambda b, ti: (0, 0, 0)),
                  pl.BlockSpec((SUBLANES, w), const2)],
        out_specs=[pl.BlockSpec((tp, FOX_HEADS * FOX_AUG), row),
                   pl.BlockSpec((tp, FOX_HEADS * FOX_AUG), row),
                   pl.BlockSpec((tp, w), row),
                   pl.BlockSpec((tp, FOX_HEADS, FOX_HEAD_DIM), lambda b, ti: (b * nt + ti, 0, 0)),
                   pl.BlockSpec((tp, FOX_HEADS, FOX_HEAD_DIM), lambda b, ti: (b * nt + ti, 0, 0)),
                   pl.BlockSpec((tp, LANES), row)],
        out_shape=[jax.ShapeDtypeStruct((m, FOX_HEADS * FOX_AUG), BF16),
                   jax.ShapeDtypeStruct((m, FOX_HEADS * FOX_AUG), BF16),
                   jax.ShapeDtypeStruct((m, w), BF16),
                   jax.ShapeDtypeStruct((m, FOX_HEADS, FOX_HEAD_DIM), F32),
                   jax.ShapeDtypeStruct((m, FOX_HEADS, FOX_HEAD_DIM), F32),
                   jax.ShapeDtypeStruct((m, LANES), F32)],
        scratch_shapes=[pltpu.VMEM((1, LANES), F32)],
        compiler_params=_cparams(2),
        name="fox_prep",
    )(proj, proj, proj, fl_proj, f_bias, tri, jnp.asarray(place, BF16), jnp.asarray(ones, F32))


def _fox_kernel(qi_ref, ki_ref, q_ref, k_ref, v_ref, o_ref, m_ref, acc_ref, va_ref, *, t, ts, ts_diag):
    step = pl.program_id(2)
    qi = qi_ref[step]
    ki = ki_ref[step]
    to_log2 = (FOX_HEAD_DIM ** -0.5) * math.log2(math.e)
    hd = FOX_HEAD_DIM

    @pl.when(ki == 0)
    def _():
        m_ref[...] = jnp.full_like(m_ref, NEG_INF)
        acc_ref[...] = jnp.zeros_like(acc_ref)
        va_ref[:, hd:2 * hd] = jnp.ones((t, hd), BF16)

    va_ref[:, 0:hd] = v_ref[...]

    def update(diagonal):
        tc = ts_diag if diagonal else ts
        for r in range(t // tc):
            rows = slice(r * tc, (r + 1) * tc)
            nk = (r + 1) * tc if diagonal else t
            s = _dot_nt(q_ref[rows, :], k_ref[0:nk, :])
            if diagonal:
                ri = lax.broadcasted_iota(jnp.int32, (tc, nk), 0) + r * tc
                ci = lax.broadcasted_iota(jnp.int32, (tc, nk), 1)
                s = jnp.where(ci <= ri, s, NEG_INF)
            m_old = m_ref[rows, :]
            m_new = jnp.maximum(m_old, jnp.max(s, axis=1, keepdims=True))
            m_ref[rows, :] = m_new
            alpha = jnp.exp2((m_old - m_new) * to_log2)
            p = jnp.exp2((s - _widen(m_new, nk)) * to_log2)
            acc_ref[rows, :] = _widen(alpha, 2 * hd) * acc_ref[rows, :] + _dot(p.astype(BF16), va_ref[0:nk, :])

    @pl.when(ki < qi)
    def _():
        update(False)

    @pl.when(ki == qi)
    def _():
        update(True)
        o_ref[...] = (acc_ref[:, 0:hd] / acc_ref[:, hd:2 * hd]).astype(BF16)


def _fox_prompt(qa, ka, vb, *, bsz, seq_len, t, ts, ts_diag):
    m = qa.shape[0]
    nq = seq_len // t
    assert seq_len % t == 0 and t % ts == 0 and t % ts_diag == 0
    pairs = [(qi, ki) for qi in range(nq) for ki in range(qi + 1)]
    qi_tab = jnp.asarray([p[0] for p in pairs], jnp.int32)
    ki_tab = jnp.asarray([p[1] for p in pairs], jnp.int32)
    grid_spec = pltpu.PrefetchScalarGridSpec(
        num_scalar_prefetch=2,
        grid=(bsz, FOX_HEADS, len(pairs)),
        in_specs=[pl.BlockSpec((t, FOX_AUG), lambda b, h, s, qi, ki: (b * nq + qi[s], h)),
                  pl.BlockSpec((t, FOX_AUG), lambda b, h, s, qi, ki: (b * nq + ki[s], h)),
                  pl.BlockSpec((t, FOX_HEAD_DIM), lambda b, h, s, qi, ki: (b * nq + ki[s], h))],
        out_specs=pl.BlockSpec((t, FOX_HEAD_DIM), lambda b, h, s, qi, ki: (b * nq + qi[s], h)),
        scratch_shapes=[pltpu.VMEM((t, LANES), F32), pltpu.VMEM((t, 2 * FOX_HEAD_DIM), F32),
                        pltpu.VMEM((t, 2 * FOX_HEAD_DIM), BF16)],
    )
    return pl.pallas_call(
        functools.partial(_fox_kernel, t=t, ts=ts, ts_diag=ts_diag),
        grid_spec=grid_spec,
        out_shape=jax.ShapeDtypeStruct((m, FOX_WIDTH), BF16),
        compiler_params=_cparams(3),
        name="fox_attention",
    )(qi_tab, ki_tab, qa, ka, vb)


def _fox_decode_kernel(q_ref, kn_ref, vn_ref, kc_ref, vc_ref, cq_ref, ckc_ref, ckn_ref, o_ref,
                       m_ref, l_ref, acc_ref, *, lq, ncb):
    j = pl.program_id(1)
    nh, hd = FOX_HEADS, FOX_HEAD_DIM

    @pl.when(j == 0)
    def _():
        m_ref[...] = jnp.full_like(m_ref, NEG_INF)
        l_ref[...] = jnp.zeros_like(l_ref)
        acc_ref[...] = jnp.zeros_like(acc_ref)

    def attend(k_head, v_head, ck_head, causal):
        for h in range(nh):
            qh = q_ref[:, h * hd:(h + 1) * hd].astype(BF16)
            ck = ck_head(h)
            tk = ck.shape[1]
            s = _dot_nt(qh, k_head(h).astype(BF16)) * (hd ** -0.5)
            s = s + (_widen(cq_ref[h], tk) - ck)
            if causal:
                rows = lax.broadcasted_iota(jnp.int32, (lq, tk), 0)
                cols = lax.broadcasted_iota(jnp.int32, (lq, tk), 1)
                s = jnp.where(cols <= rows, s, NEG_INF)
            m_old = m_ref[h]
            m_new = jnp.maximum(m_old, jnp.max(s, axis=1, keepdims=True))
            alpha = jnp.exp(m_old - m_new)
            p = jnp.exp(s - _widen(m_new, tk))
            l_ref[h] = alpha * l_ref[h] + jnp.sum(p, axis=1, keepdims=True)
            acc_ref[h] = alpha * acc_ref[h] + _dot(p.astype(BF16), v_head(h).astype(BF16))
            m_ref[h] = m_new

    @pl.when(j < ncb)
    def _():
        k_hm = pltpu.einshape("mhd->hmd", kc_ref[0])
        v_hm = pltpu.einshape("mhd->hmd", vc_ref[0])
        attend(lambda h: k_hm[h], lambda h: v_hm[h], lambda h: ckc_ref[h], False)

    @pl.when(j == ncb)
    def _():
        attend(lambda h: kn_ref[:, h * hd:(h + 1) * hd], lambda h: vn_ref[:, h * hd:(h + 1) * hd],
               lambda h: ckn_ref[h][:, 0:lq], True)
        for h in range(nh):
            o_ref[:, h * hd:(h + 1) * hd] = (acc_ref[h] / l_ref[h]).astype(BF16)


def _fox_decode(proj, cache_k, cache_v, cq, ck_cache, ck_new, *, bsz, lq, tk):
    past = cache_k.shape[1]
    assert past % tk == 0 and tk % LANES == 0 and lq <= LANES
    ncb = past // tk
    nh, hd, w = FOX_HEADS, FOX_HEAD_DIM, FOX_WIDTH
    tile = lambda j: jnp.minimum(j, ncb - 1)
    cache_spec = pl.BlockSpec((1, tk, nh, hd), lambda b, j: (b, tile(j), 0, 0))
    return pl.pallas_call(
        functools.partial(_fox_decode_kernel, lq=lq, ncb=ncb),
        grid=(bsz, ncb + 1),
        in_specs=[pl.BlockSpec((lq, w), lambda b, j: (b, 0)),
                  pl.BlockSpec((lq, w), lambda b, j: (b, 1)),
                  pl.BlockSpec((lq, w), lambda b, j: (b, 2)),
                  cache_spec,
                  cache_spec,
                  pl.BlockSpec((nh, lq, LANES), lambda b, j: (b, 0, 0)),
                  pl.BlockSpec((nh, 1, tk), lambda b, j: (b, 0, tile(j))),
                  pl.BlockSpec((nh, 1, LANES), lambda b, j: (b, 0, 0))],
        out_specs=pl.BlockSpec((lq, w), lambda b, j: (b, 0)),
        out_shape=jax.ShapeDtypeStruct((bsz * lq, w), BF16),
        scratch_shapes=[pltpu.VMEM((nh, lq, LANES), F32), pltpu.VMEM((nh, lq, LANES), F32),
                        pltpu.VMEM((nh, lq, hd), F32)],
        compiler_params=_cparams(2),
        name="fox_decode",
    )(proj, proj, proj, cache_k, cache_v, cq, ck_cache, ck_new)


def _sconv_kernel(u_ref, bg_ref, cg_ref, cw_ref, hist_ref, y_ref, tail_ref, carry_ref, ext_ref,
                  *, tm, rows, spt, tpb):
    i = pl.program_id(0)
    w = cg_ref[...] * u_ref[...]
    first = (i % tpb) == 0
    for s in range(spt):
        w_s = w[s * rows:(s + 1) * rows]
        if tpb == 1:
            prev = hist_ref[s]
        else:
            prev = jnp.where(first, hist_ref[s], carry_ref[...])
        conv = _conv_rows(ext_ref, w_s, prev, cw_ref, SC_WIDTH, rows)
        y_ref[s * rows:(s + 1) * rows, :] = (bg_ref[s * rows:(s + 1) * rows, :] * conv).astype(BF16)
        tail_ref[s] = w_s[rows - SUBLANES:rows]
    if tpb > 1:
        carry_ref[...] = w[tm - SUBLANES:tm]


def _sconv(proj, conv_w, hist8, *, seq_len, tm):
    m = proj.shape[0]
    assert m % tm == 0
    rows, spt, tpb = _seq_tiling(seq_len, tm)
    nm = m // tm
    hist_map = (lambda i: (i // tpb, 0, 0)) if spt == 1 else (lambda i: (i, 0, 0))
    base = 3 * FOX_WIDTH // SC_DIM
    return pl.pallas_call(
        functools.partial(_sconv_kernel, tm=tm, rows=rows, spt=spt, tpb=tpb),
        grid=(nm,),
        in_specs=[pl.BlockSpec((tm, SC_DIM), lambda i: (i, base)),
                  pl.BlockSpec((tm, SC_DIM), lambda i: (i, base + 1)),
                  pl.BlockSpec((tm, SC_DIM), lambda i: (i, base + 2)),
                  pl.BlockSpec((SC_WIDTH, SC_DIM), lambda i: (0, 0)),
                  pl.BlockSpec((spt, SUBLANES, SC_DIM), hist_map)],
        out_specs=[pl.BlockSpec((tm, SC_DIM), lambda i: (i, 0)),
                   pl.BlockSpec((spt, SUBLANES, SC_DIM), lambda i: (i, 0, 0))],
        out_shape=[jax.ShapeDtypeStruct((m, SC_DIM), BF16),
                   jax.ShapeDtypeStruct((nm * spt, SUBLANES, SC_DIM), F32)],
        scratch_shapes=[pltpu.VMEM((SUBLANES, SC_DIM), F32),
                        pltpu.VMEM((rows + SUBLANES, SC_DIM), F32)],
        compiler_params=_cparams(1),
        name="gated_short_conv",
    )(proj, proj, proj, conv_w, hist8)


def _hist8(state):
    n, w1, c = state.shape
    return jnp.concatenate([jnp.zeros((n, SUBLANES - w1, c), F32), state.astype(F32)], axis=1)


def _tails(tails, n_seq, seq_len, tile_rows, keep):
    per_seq = max(1, seq_len // tile_rows)
    pieces = tails.reshape(n_seq, per_seq, SUBLANES, tails.shape[-1])
    return pieces[:, per_seq - 1, SUBLANES - keep:, :]


def _rope_tables(pos0, length):
    half = RET_DK // 2
    inv = ROPE_BASE ** (-np.arange(half, dtype=np.float64) / half)
    ang = (pos0 + np.arange(length, dtype=np.float64))[:, None] * inv[None, :]
    cos, sin = np.cos(ang), np.sin(ang)
    return (jnp.asarray(np.concatenate([cos, cos], axis=1), F32),
            jnp.asarray(np.concatenate([-sin, sin], axis=1), F32))


def _prep_weights(p):
    d = D_MODEL
    ab_in = p['ab_w_in'][0]
    cd_in = p['cd_w_in'][0]
    f0 = 3 * FOX_WIDTH
    pad_cols = lambda w: jnp.pad(w, ((0, 0), (0, LANES - w.shape[1]))).astype(BF16)
    return dict(
        ab_in=ab_in.astype(BF16),
        ab_small=pad_cols(ab_in[:, AB_MAIN:]),
        cd_in=jnp.concatenate([cd_in[:, :f0], cd_in[:, f0 + FOX_HEADS:]], axis=1).astype(BF16),
        cd_small=pad_cols(cd_in[:, f0:f0 + FOX_HEADS]),
        ab_out=p['ab_w_out'][0].astype(BF16),
        cd_out=p['cd_w_out'][0].astype(BF16),
        ffn_gate=p['ffn_w_gate'].astype(BF16),
        ffn_up=p['ffn_w_up'].astype(BF16),
        ffn_down=p['ffn_w_down'].astype(BF16),
    )


def _trunk(x, pos0, st_ret, st_ssd, st_ssd_conv, c_k, c_v, c_logf, st_sconv, st_ffn, p, wb, t):
    bsz, length, d = x.shape
    m = bsz * length
    xf = x.reshape(m, d)
    zeros = lambda *shape: jnp.zeros(shape, F32)

    proj, dt_proj = _norm_matmul(xf, p['ab_norm_w'][0], wb['ab_in'], wb['ab_small'],
                                 n=AB_MAIN, tm=t['tm_ab'], tn=t['tn_ab'])
    cosf, sinf = _rope_tables(pos0, length)
    ret_state = zeros(bsz, RET_HEADS, RET_DK, RET_DV) if st_ret is None else st_ret
    y_ret, ret_new = _retention(proj, cosf, sinf, ret_state, p['ret_norm_w'][0],
                                bsz=bsz, seq_len=length, c=t['c_ret'])
    ssd_state = zeros(bsz, SSD_HEADS, SSD_HEADDIM, SSD_DSTATE) if st_ssd is None else st_ssd
    ssd_hist = zeros(bsz, SSD_CONV - 1, SSD_CONV_DIM) if st_ssd_conv is None else st_ssd_conv
    y_ssd, ssd_new = _ssd(proj, dt_proj, _hist8(ssd_hist), ssd_state, p['ssd_conv_w'][0], p['ssd_conv_b'][0],
                          p['ssd_dt_bias'][0], p['ssd_A_log'][0], p['ssd_D'][0], p['ssd_norm_w'][0],
                          bsz=bsz, seq_len=length, c=t['c_ssd'])
    xbc_lo = AB_MAIN - SSD_CONV_DIM
    ssd_conv_new = proj.reshape(bsz, length, -1)[:, length - (SSD_CONV - 1):, xbc_lo:AB_MAIN]
    xf = _proj_residual(xf, y_ret, y_ssd, wb['ab_out'], tm=t['tm_out'], tn=t['tn_out'])

    ffn_new = []
    ffn_hist0 = zeros(bsz, FFN_CONV - 1, D_FF) if st_ffn is None else st_ffn[0]
    xf, tails = _conv_ffn(xf, p['ffn_norm_w'][0], wb['ffn_gate'], wb['ffn_up'], wb['ffn_down'],
                          p['ffn_conv_w'][0], p['ffn_conv_b'][0], _hist8(ffn_hist0), p['final_norm_w'],
                          layer=0, seq_len=length, tm=t['tm_ffn'], tf=t['tf_ffn'], ts=t['ts_ffn'], final=False)
    ffn_new.append(_tails(tails, bsz, length, t['tm_ffn'], FFN_CONV - 1))

    proj, fl_proj = _norm_matmul(xf, p['cd_norm_w'][0], wb['cd_in'], wb['cd_small'],
                                 n=CD_MAIN, tm=t['tm_cd'], tn=t['tn_cd'])
    f_bias = jnp.pad(p['fox_f_bias'][0].astype(F32), (0, LANES - FOX_HEADS)).reshape(1, LANES)
    head_shape = (bsz, length, FOX_HEADS, FOX_HEAD_DIM)
    if c_k is None:
        qa, ka, vb, k32, v32, logf = _fox_prep(proj, fl_proj, f_bias, bsz=bsz, seq_len=length, tp=t['t_prep'])
        y_fox = _fox_prompt(qa, ka, vb, bsz=bsz, seq_len=length, t=t['t_fox'], ts=t['ts_fox'],
                            ts_diag=t['ts_fox_diag'])
        logf_new = logf.reshape(bsz, length, LANES)[:, :, :FOX_HEADS]
        k_new, v_new = k32.reshape(head_shape), v32.reshape(head_shape)
    else:
        proj3 = proj.reshape(bsz, length, -1)
        k_new = proj3[:, :, FOX_WIDTH:2 * FOX_WIDTH].reshape(head_shape)
        v_new = proj3[:, :, 2 * FOX_WIDTH:3 * FOX_WIDTH].reshape(head_shape)
        past = c_k.shape[1]
        pairs = bsz * FOX_HEADS
        assert pairs <= LANES and length <= LANES
        to_lanes = lambda a, rows: jnp.pad(jnp.swapaxes(a, 0, 1).reshape(a.shape[1], pairs),
                                           ((0, rows - a.shape[1]), (0, LANES - pairs)))
        from_lanes = lambda a: jnp.swapaxes(a[:length, :pairs].reshape(length, bsz, FOX_HEADS), 0, 1)
        cache_lf = to_lanes(c_logf.astype(F32), past)
        fl_rows = to_lanes(fl_proj.reshape(bsz, length, LANES)[:, :, :FOX_HEADS], LANES)
        bias_lanes = jnp.pad(jnp.tile(p['fox_f_bias'][0].astype(F32), bsz), (0, LANES - pairs)).reshape(1, LANES)
        cum_t_cache, lf_rows, cum_rows, cum_t_new = _decode_cum(cache_lf, fl_rows, bias_lanes, c=t['c_cum'])
        logf_new = from_lanes(lf_rows)
        cq = jnp.broadcast_to(cum_rows[:length, :pairs].T[:, :, None], (pairs, length, LANES))
        y_fox = _fox_decode(proj, c_k, c_v, cq, cum_t_cache[:pairs, None, :], cum_t_new[:pairs, None, :],
                            bsz=bsz, lq=length, tk=t['tk_dec'])
    sc_hist = zeros(bsz, SC_WIDTH - 1, SC_DIM) if st_sconv is None else st_sconv
    y_sc, sc_tails = _sconv(proj, p['sconv_w'][0], _hist8(sc_hist), seq_len=length, tm=t['tm_sc'])
    sconv_new = _tails(sc_tails, bsz, length, t['tm_sc'], SC_WIDTH - 1)
    xf = _proj_residual(xf, y_fox, y_sc, wb['cd_out'], tm=t['tm_out'], tn=t['tn_out'])

    ffn_hist1 = zeros(bsz, FFN_CONV - 1, D_FF) if st_ffn is None else st_ffn[1]
    xf, tails = _conv_ffn(xf, p['ffn_norm_w'][1], wb['ffn_gate'], wb['ffn_up'], wb['ffn_down'],
                          p['ffn_conv_w'][1], p['ffn_conv_b'][1], _hist8(ffn_hist1), p['final_norm_w'],
                          layer=1, seq_len=length, tm=t['tm_ffn'], tf=t['tf_ffn'], ts=t['ts_ffn'], final=True)
    ffn_new.append(_tails(tails, bsz, length, t['tm_ffn'], FFN_CONV - 1))

    return (xf.reshape(bsz, length, d), ret_new[None], ssd_new[None], ssd_conv_new[None], k_new[None],
            v_new[None], logf_new[None], sconv_new[None], jnp.stack(ffn_new))


def _largest_divisor(n, cap, multiple=1):
    best = None
    for cand in range(multiple, min(n, cap) + 1, multiple):
        if n % cand == 0:
            best = cand
    assert best is not None, (n, cap, multiple)
    return best


def _tiles(bsz, length, past=None):
    m = bsz * length
    seq_tile = lambda cap: _largest_divisor(length, cap, SUBLANES)
    row_tile = lambda cap: (_largest_divisor(length, cap, SUBLANES) if length >= cap
                            else _largest_divisor(m, cap, length))
    t = dict(
        tm_ab=row_tile(1024), tn_ab=2048, tm_cd=row_tile(1024), tn_cd=2048,
        tm_out=row_tile(1024), tn_out=2048,
        tm_ffn=row_tile(1024), tf_ffn=512, ts_ffn=512,
        tm_sc=row_tile(1024),
        c_ret=seq_tile(256), c_ssd=seq_tile(256),
    )
    if past is None:
        t['t_fox'] = seq_tile(2048)
        t['ts_fox'] = _largest_divisor(t['t_fox'], 256, LANES)
        t['ts_fox_diag'] = _largest_divisor(t['t_fox'], 512, LANES)
        t['t_prep'] = seq_tile(512)
    else:
        t['tk_dec'] = _largest_divisor(past, 1024, LANES)
        t['c_cum'] = _largest_divisor(past, 256, LANES)
    return t


def kernel(x_prompt, x_sample, state_ret, state_ssd, state_ssd_conv, cache_fox_k, cache_fox_v, cache_fox_logf, state_sconv, state_ffn_conv, ab_norm_w, ab_w_in, ret_norm_w, ssd_conv_w, ssd_conv_b, ssd_dt_bias, ssd_A_log, ssd_D, ssd_norm_w, ab_w_out, cd_norm_w, cd_w_in, fox_f_bias, sconv_w, cd_w_out, ffn_norm_w, ffn_w_gate, ffn_w_up, ffn_conv_w, ffn_conv_b, ffn_w_down, final_norm_w):
    p = dict(ab_norm_w=ab_norm_w, ab_w_in=ab_w_in, ret_norm_w=ret_norm_w, ssd_conv_w=ssd_conv_w,
             ssd_conv_b=ssd_conv_b, ssd_dt_bias=ssd_dt_bias, ssd_A_log=ssd_A_log, ssd_D=ssd_D,
             ssd_norm_w=ssd_norm_w, ab_w_out=ab_w_out, cd_norm_w=cd_norm_w, cd_w_in=cd_w_in,
             fox_f_bias=fox_f_bias, sconv_w=sconv_w, cd_w_out=cd_w_out, ffn_norm_w=ffn_norm_w,
             ffn_w_gate=ffn_w_gate, ffn_w_up=ffn_w_up, ffn_conv_w=ffn_conv_w, ffn_conv_b=ffn_conv_b,
             ffn_w_down=ffn_w_down, final_norm_w=final_norm_w)
    assert x_prompt.shape[-1] == D_MODEL and ab_w_in.shape == (1, D_MODEL, AB_MAIN + SSD_HEADS)
    assert cd_w_in.shape == (1, D_MODEL, CD_MAIN + FOX_HEADS) and ffn_w_gate.shape == (2, D_MODEL, D_FF)
    wb = _prep_weights(p)
    bp, lp_, _ = x_prompt.shape
    bs, ls, _ = x_sample.shape
    past = cache_fox_k.shape[2]
    (y_prompt, p_ret, p_ssd, p_ssd_conv, p_fox_k, p_fox_v, p_fox_logf, p_sconv, p_ffn_conv) = _trunk(
        x_prompt, 0, None, None, None, None, None, None, None, None, p, wb, _tiles(bp, lp_))
    (y_sample, s_ret, s_ssd, s_ssd_conv, s_fox_k, s_fox_v, s_fox_logf, s_sconv, s_ffn_conv) = _trunk(
        x_sample, past, state_ret[0], state_ssd[0], state_ssd_conv[0], cache_fox_k[0], cache_fox_v[0],
        cache_fox_logf[0], state_sconv[0], state_ffn_conv, p, wb, _tiles(bs, ls, past))
    return (y_prompt, y_sample, p_ret, s_ret, p_ssd, s_ssd, p_ssd_conv, s_ssd_conv, p_fox_k, s_fox_k,
            p_fox_v, s_fox_v, p_fox_logf, s_fox_logf, p_sconv, s_sconv, p_ffn_conv, s_ffn_conv)
```

```python
import functools
import math

import numpy as np
import jax
import jax.numpy as jnp
from jax import lax
from jax.experimental import pallas as pl
from jax.experimental.pallas import tpu as pltpu

F32 = jnp.float32
BF16 = jnp.bfloat16
EPS = 1e-6
ROPE_BASE = 10000.0
NEG_INF = float("-inf")

D_MODEL = 2048
RET_HEADS, RET_DK, RET_DV = 4, 128, 256
SSD_DINNER, SSD_HEADDIM, SSD_HEADS, SSD_GROUPS, SSD_DSTATE, SSD_CONV = 1024, 64, 16, 2, 128, 4
SSD_CONV_DIM = SSD_DINNER + 2 * SSD_GROUPS * SSD_DSTATE
FOX_HEADS, FOX_HEAD_DIM = 8, 128
FOX_WIDTH = FOX_HEADS * FOX_HEAD_DIM
SC_DIM, SC_WIDTH = 1024, 3
D_FF, FFN_CONV = 5632, 3
AB_MAIN = 2 * RET_HEADS * RET_DK + 2 * RET_HEADS * RET_DV + SSD_DINNER + SSD_CONV_DIM
AB_PAD = AB_MAIN + 128
CD_MAIN = 3 * FOX_WIDTH + 3 * SC_DIM
CD_PAD = CD_MAIN + 128

LANES = 128
SUBLANES = 8
VMEM_LIMIT = 60 * 1024 * 1024


def _cparams(n_axes):
    return pltpu.CompilerParams(dimension_semantics=("arbitrary",) * n_axes,
                                vmem_limit_bytes=VMEM_LIMIT)


def _rms(xf, w):
    return xf * lax.rsqrt(jnp.mean(xf * xf, axis=-1, keepdims=True) + EPS) * w


def _softplus(x):
    return jnp.maximum(x, 0.0) + jnp.log1p(jnp.exp(-jnp.abs(x)))


def _split3(x):
    hi = x.astype(BF16)
    r1 = x - hi.astype(F32)
    mid = r1.astype(BF16)
    lo = (r1 - mid.astype(F32)).astype(BF16)
    return hi, mid, lo


def _widen(x, n):
    return x[:, 0:n] if n <= LANES else jnp.concatenate([x] * (n // LANES), axis=1)


def _dot(a, b):
    return jnp.dot(a, b, preferred_element_type=F32)


def _dot_nt(a, b):
    return lax.dot_general(a, b, (((1,), (1,)), ((), ())), preferred_element_type=F32)


def _dot_tn(a, b):
    return lax.dot_general(a, b, (((0,), (0,)), ((), ())), preferred_element_type=F32)


def _exact_lhs_dot(m_bf16, x):
    hi, mid, lo = _split3(x)
    return _dot(m_bf16, hi) + _dot(m_bf16, mid) + _dot(m_bf16, lo)


def _exact_rhs_dot(x, m_bf16):
    hi, mid, lo = _split3(x)
    return _dot(hi, m_bf16) + _dot(mid, m_bf16) + _dot(lo, m_bf16)


def _conv_rows(ext_ref, x, prev8, w_ref, width, rows, w_cols=slice(None)):
    ext_ref[0:SUBLANES, :] = prev8
    ext_ref[SUBLANES:SUBLANES + rows, :] = x
    out = None
    for j in range(width):
        off = SUBLANES - (width - 1) + j
        term = ext_ref[off:off + rows, :] * w_ref[j:j + 1, w_cols]
        out = term if out is None else out + term
    return out


def _seq_tiling(seq_len, tile_rows):
    if seq_len >= tile_rows:
        assert seq_len % tile_rows == 0
        return tile_rows, 1, seq_len // tile_rows
    assert tile_rows % seq_len == 0 and seq_len % SUBLANES == 0
    return seq_len, tile_rows // seq_len, 1


def _norm_matmul_kernel(x_ref, nw_ref, w_ref, ws_ref, o_ref, os_ref, h_ref, *, n_tiles, tn, last_w):
    j = pl.program_id(1)

    @pl.when(j == 0)
    def _():
        h_ref[...] = _rms(x_ref[...], nw_ref[...]).astype(BF16)
        os_ref[...] = _dot(h_ref[...], ws_ref[...])

    if last_w == tn:
        o_ref[...] = _dot(h_ref[...], w_ref[...])
    else:
        @pl.when(j < n_tiles - 1)
        def _():
            o_ref[...] = _dot(h_ref[...], w_ref[...])

        @pl.when(j == n_tiles - 1)
        def _():
            o_ref[:, 0:last_w] = _dot(h_ref[...], w_ref[:, 0:last_w])


def _norm_matmul(x, norm_w, w, w_small, *, n, tm, tn):
    m, d = x.shape
    n_tiles = pl.cdiv(n, tn)
    last_w = n - (n_tiles - 1) * tn
    assert m % tm == 0 and last_w % LANES == 0 and n <= w.shape[1] and w_small.shape == (d, LANES)
    return pl.pallas_call(
        functools.partial(_norm_matmul_kernel, n_tiles=n_tiles, tn=tn, last_w=last_w),
        grid=(m // tm, n_tiles),
        in_specs=[pl.BlockSpec((tm, d), lambda i, j: (i, 0)),
                  pl.BlockSpec((1, d), lambda i, j: (0, 0)),
                  pl.BlockSpec((d, tn), lambda i, j: (0, j)),
                  pl.BlockSpec((d, LANES), lambda i, j: (0, 0))],
        out_specs=[pl.BlockSpec((tm, tn), lambda i, j: (i, j)),
                   pl.BlockSpec((tm, LANES), lambda i, j: (i, 0))],
        out_shape=[jax.ShapeDtypeStruct((m, n), F32),
                   jax.ShapeDtypeStruct((m, LANES), F32)],
        scratch_shapes=[pltpu.VMEM((tm, d), BF16)],
        compiler_params=_cparams(2),
        name="norm_in_proj",
    )(x, norm_w.reshape(1, d), w, w_small)


def _proj_res_kernel(x_ref, a_ref, b_ref, wa_ref, wb_ref, o_ref):
    acc = _dot(a_ref[...], wa_ref[...])
    acc = acc + _dot(b_ref[...], wb_ref[...])
    o_ref[...] = x_ref[...] + acc


def _proj_residual(x, a, b, w, *, tm, tn):
    m, d = x.shape
    ka, kb = a.shape[1], b.shape[1]
    assert m % tm == 0 and d % tn == 0 and ka == kb and w.shape == (ka + kb, d)
    return pl.pallas_call(
        _proj_res_kernel,
        grid=(m // tm, d // tn),
        in_specs=[pl.BlockSpec((tm, tn), lambda i, j: (i, j)),
                  pl.BlockSpec((tm, ka), lambda i, j: (i, 0)),
                  pl.BlockSpec((tm, kb), lambda i, j: (i, 0)),
                  pl.BlockSpec((ka, tn), lambda i, j: (0, j)),
                  pl.BlockSpec((kb, tn), lambda i, j: (1, j))],
        out_specs=pl.BlockSpec((tm, tn), lambda i, j: (i, j)),
        out_shape=jax.ShapeDtypeStruct((m, d), F32),
        compiler_params=_cparams(2),
        name="out_proj_residual",
    )(x, a, b, w, w)


def _ffn_kernel(*refs, tm, rows, spt, tpb, nf, ts, nsub, nsub_last, final, n_cast):
    x_ref, nw_ref, wg_ref, wu_ref, wd_ref, cw_ref, cb_ref, hist_ref, fw_ref = refs[:9]
    cast_in = refs[9:9 + n_cast]
    o_ref, tail_ref = refs[9 + n_cast:11 + n_cast]
    cast_out = refs[11 + n_cast:11 + 2 * n_cast]
    h_ref, carry_ref, ext_ref = refs[11 + 2 * n_cast:]
    i = pl.program_id(0)
    f = pl.program_id(1)

    for src_ref, dst_ref in zip(cast_in, cast_out):
        dst_ref[...] = src_ref[...].astype(BF16)

    @pl.when(f == 0)
    def _():
        xf = x_ref[...]
        h_ref[...] = _rms(xf, nw_ref[...]).astype(BF16)
        o_ref[...] = xf

    first = (i % tpb) == 0

    def sub_block(sb):
        cols = slice(sb * ts, (sb + 1) * ts)
        h = h_ref[...]
        a = _dot(h, wg_ref[:, cols])
        u = _dot(h, wu_ref[:, cols])
        convs = []
        for s in range(spt):
            a_s = a[s * rows:(s + 1) * rows]
            if tpb == 1:
                prev = hist_ref[s, :, cols]
            else:
                prev = jnp.where(first, hist_ref[s, :, cols], carry_ref[f * nsub + sb])
            convs.append(_conv_rows(ext_ref, a_s, prev, cw_ref, FFN_CONV, rows, cols))
            tail_ref[s, :, cols] = a_s[rows - SUBLANES:rows]
        if tpb > 1:
            carry_ref[f * nsub + sb] = a[tm - SUBLANES:tm]
        conv = convs[0] if spt == 1 else jnp.concatenate(convs, axis=0)
        act = (jax.nn.silu(conv + cb_ref[:, cols]) * u).astype(BF16)
        o_ref[...] += _dot(act, wd_ref[cols, :])

    if nsub_last == nsub:
        for sb in range(nsub):
            sub_block(sb)
    else:
        @pl.when(f < nf - 1)
        def _():
            for sb in range(nsub):
                sub_block(sb)

        @pl.when(f == nf - 1)
        def _():
            for sb in range(nsub_last):
                sub_block(sb)

    if final:
        @pl.when(f == nf - 1)
        def _():
            o_ref[...] = _rms(o_ref[...], fw_ref[...])


def _conv_ffn(x, norm_w, wg, wu, wd, conv_w, conv_b, hist8, final_w, *, layer, seq_len, tm, tf, ts, final,
              cast_next=None):
    m, d = x.shape
    ff = wg.shape[2]
    assert m % tm == 0 and tf % ts == 0 and ff % ts == 0
    rows, spt, tpb = _seq_tiling(seq_len, tm)
    nm, nf = m // tm, pl.cdiv(ff, tf)
    nsub = tf // ts
    nsub_last = (ff - (nf - 1) * tf) // ts
    hist_map = (lambda i, f: (i // tpb, 0, f)) if spt == 1 else (lambda i, f: (i, 0, f))
    cast_in, cast_specs = [], []
    if cast_next is not None:
        assert d % nm == 0 and (d // nm) % LANES == 0 and ff % nf == 0 and (ff // nf) % LANES == 0
        dr, fc = d // nm, ff // nf
        cast_in = list(cast_next)
        cast_specs = [pl.BlockSpec((dr, fc), lambda i, f: (i, f)),
                      pl.BlockSpec((dr, fc), lambda i, f: (i, f)),
                      pl.BlockSpec((fc, dr), lambda i, f: (f, i))]
    kern = functools.partial(_ffn_kernel, tm=tm, rows=rows, spt=spt, tpb=tpb, nf=nf, ts=ts, nsub=nsub,
                             nsub_last=nsub_last, final=final, n_cast=len(cast_in))
    results = pl.pallas_call(
        kern,
        grid=(nm, nf),
        in_specs=[pl.BlockSpec((tm, d), lambda i, f: (i, 0)),
                  pl.BlockSpec((1, d), lambda i, f: (0, 0)),
                  pl.BlockSpec((None, d, tf), lambda i, f: (layer, 0, f)),
                  pl.BlockSpec((None, d, tf), lambda i, f: (layer, 0, f)),
                  pl.BlockSpec((None, tf, d), lambda i, f: (layer, f, 0)),
                  pl.BlockSpec((FFN_CONV, tf), lambda i, f: (0, f)),
                  pl.BlockSpec((1, tf), lambda i, f: (0, f)),
                  pl.BlockSpec((spt, SUBLANES, tf), hist_map),
                  pl.BlockSpec((1, d), lambda i, f: (0, 0))] + cast_specs,
        out_specs=[pl.BlockSpec((tm, d), lambda i, f: (i, 0)),
                   pl.BlockSpec((spt, SUBLANES, tf), lambda i, f: (i, 0, f))] + cast_specs,
        out_shape=[jax.ShapeDtypeStruct((m, d), F32),
                   jax.ShapeDtypeStruct((nm * spt, SUBLANES, ff), F32)]
                  + [jax.ShapeDtypeStruct(w.shape, BF16) for w in cast_in],
        scratch_shapes=[pltpu.VMEM((tm, d), BF16),
                        pltpu.VMEM((nf * nsub, SUBLANES, ts), F32),
                        pltpu.VMEM((rows + SUBLANES, ts), F32)],
        compiler_params=_cparams(2),
        name="conv_ffn",
    )(x, norm_w.reshape(1, d), wg, wu, wd, conv_w, conv_b.reshape(1, ff), hist8, final_w.reshape(1, d), *cast_in)
    return results[0], results[1], tuple(results[2:])


def _retention_kernel(q_ref, k_ref, v_ref, g_ref, cos_ref, sin_ref, st_ref, nw_ref,
                      y_ref, so_ref, *, c):
    ci = pl.program_id(1)

    @pl.when(ci == 0)
    def _():
        so_ref[...] = st_ref[...]

    cos = cos_ref[...]
    sin = sin_ref[...]
    ii = lax.broadcasted_iota(jnp.int32, (c, c), 0)
    jj = lax.broadcasted_iota(jnp.int32, (c, c), 1)
    diff = (ii - jj).astype(F32)
    causal = ii >= jj
    ridx = lax.broadcasted_iota(jnp.int32, (c, 1), 0).astype(F32)
    for h in range(RET_HEADS):
        lg = math.log1p(-(2.0 ** (-5.0 - h)))
        q = q_ref[:, h * RET_DK:(h + 1) * RET_DK]
        k = k_ref[:, h * RET_DK:(h + 1) * RET_DK]
        v = v_ref[:, h * RET_DV:(h + 1) * RET_DV]
        qr = q * cos + pltpu.roll(q, RET_DK // 2, 1) * sin
        kr = (k * cos + pltpu.roll(k, RET_DK // 2, 1) * sin) * (RET_DK ** -0.5)
        qb = qr.astype(BF16)
        kb = kr.astype(BF16)
        vb = v.astype(BF16)
        decay = jnp.exp(jnp.where(causal, diff * lg, NEG_INF))
        inner = jnp.exp((ridx + 1.0) * lg)
        sdecay = jnp.exp((c - 1.0 - ridx) * lg)
        s = so_ref[0, h]
        scores = _dot_nt(qb, kb) * decay
        y = _dot(scores.astype(BF16), vb)
        y = y + _dot(qb, s.astype(BF16)) * inner
        kd = (kr * sdecay).astype(BF16)
        so_ref[0, h] = math.exp(c * lg) * s + _dot_tn(kd, vb)
        mu = jnp.mean(y, axis=-1, keepdims=True)
        yc = y - mu
        var = jnp.mean(yc * yc, axis=-1, keepdims=True)
        yn = yc * lax.rsqrt(var + EPS) * nw_ref[:, h * RET_DV:(h + 1) * RET_DV]
        g = g_ref[:, h * RET_DV:(h + 1) * RET_DV]
        y_ref[:, h * RET_DV:(h + 1) * RET_DV] = (jax.nn.silu(g) * yn).astype(BF16)


def _retention(proj, cosf, sinf, state, norm_w, *, bsz, seq_len, c):
    m = proj.shape[0]
    nc = seq_len // c
    assert seq_len % c == 0
    qk_w = RET_HEADS * RET_DK
    v_w = RET_HEADS * RET_DV
    row = lambda b, ci: b * nc + ci
    y, s_new = pl.pallas_call(
        functools.partial(_retention_kernel, c=c),
        grid=(bsz, nc),
        in_specs=[pl.BlockSpec((c, qk_w), lambda b, ci: (row(b, ci), 0)),
                  pl.BlockSpec((c, qk_w), lambda b, ci: (row(b, ci), 1)),
                  pl.BlockSpec((c, v_w), lambda b, ci: (row(b, ci), 1)),
                  pl.BlockSpec((c, v_w), lambda b, ci: (row(b, ci), 2)),
                  pl.BlockSpec((c, RET_DK), lambda b, ci: (ci, 0)),
                  pl.BlockSpec((c, RET_DK), lambda b, ci: (ci, 0)),
                  pl.BlockSpec((1, RET_HEADS, RET_DK, RET_DV), lambda b, ci: (b, 0, 0, 0)),
                  pl.BlockSpec((1, v_w), lambda b, ci: (0, 0))],
        out_specs=[pl.BlockSpec((c, v_w), lambda b, ci: (row(b, ci), 0)),
                   pl.BlockSpec((1, RET_HEADS, RET_DK, RET_DV), lambda b, ci: (b, 0, 0, 0))],
        out_shape=[jax.ShapeDtypeStruct((m, v_w), BF16),
                   jax.ShapeDtypeStruct(state.shape, F32)],
        compiler_params=_cparams(2),
        name="retention",
    )(proj, proj, proj, proj, cosf, sinf, state, norm_w.reshape(1, v_w))
    return y, s_new


def _ssd_kernel(z_ref, xs_ref, bc_ref, dt_ref, hx_ref, hbc_ref, st_ref,
                cwx_ref, cwbc_ref, cbx_ref, cbbc_ref, dtb_ref, alog_ref, dsk_ref, nw_ref,
                tri_ref, exp_ref,
                y_ref, so_ref,
                st_scr, cx_scr, cbc_scr, extx_scr, extbc_scr, yh_scr, xs_scr, *, c, nc):
    ci = pl.program_id(1)
    gw = SSD_DINNER // SSD_GROUPS
    hpg = SSD_HEADS // SSD_GROUPS

    @pl.when(ci == 0)
    def _():
        st_scr[...] = st_ref[0].T
        cx_scr[...] = hx_ref[0]
        cbc_scr[...] = hbc_ref[0]

    xs_raw = xs_ref[...]
    bc_raw = bc_ref[...]
    xs_scr[...] = jax.nn.silu(_conv_rows(extx_scr, xs_raw, cx_scr[...], cwx_ref, SSD_CONV, c) + cbx_ref[...])
    bcm = jax.nn.silu(_conv_rows(extbc_scr, bc_raw, cbc_scr[...], cwbc_ref, SSD_CONV, c) + cbbc_ref[...])
    cx_scr[...] = xs_raw[c - SUBLANES:c]
    cbc_scr[...] = bc_raw[c - SUBLANES:c]

    tri = tri_ref[...]
    dt = _softplus(dt_ref[...] + dtb_ref[...])
    a = -jnp.exp(alog_ref[...])
    acs = _exact_lhs_dot(tri, dt * a)
    acs_t = acs.T
    acs_last = acs[c - 1:c, :]
    exp_acs = jnp.exp(acs)
    to_end = jnp.exp(acs_last - acs)
    chunk_dec = jnp.exp(acs_last)

    ii = lax.broadcasted_iota(jnp.int32, (c, c), 0)
    jj = lax.broadcasted_iota(jnp.int32, (c, c), 1)
    causal = ii >= jj
    nb = SSD_GROUPS * SSD_DSTATE
    for g in range(SSD_GROUPS):
        cols = slice(g * gw, (g + 1) * gw)
        expand = exp_ref[:, cols]
        xdt = xs_scr[:, cols] * _exact_rhs_dot(dt, expand)
        xdt_b = xdt.astype(BF16)
        xend_b = (xdt * _exact_rhs_dot(to_end, expand)).astype(BF16)
        b_g = bcm[:, g * SSD_DSTATE:(g + 1) * SSD_DSTATE].astype(BF16)
        c_g = bcm[:, nb + g * SSD_DSTATE:nb + (g + 1) * SSD_DSTATE].astype(BF16)
        cb = _dot_nt(c_g, b_g)
        s_g = st_scr[:, cols]
        y_state = _dot(c_g, s_g.astype(BF16)) * _exact_rhs_dot(exp_acs, expand)
        for r in range(hpg):
            hh = g * hpg + r
            seg = acs[:, hh:hh + 1] - acs_t[hh:hh + 1, :]
            lmat = jnp.exp(jnp.where(causal, seg, NEG_INF))
            mm = (cb * lmat).astype(BF16)
            head = slice(r * SSD_HEADDIM, (r + 1) * SSD_HEADDIM)
            yh_scr[:, hh * SSD_HEADDIM:(hh + 1) * SSD_HEADDIM] = _dot(mm, xdt_b[:, head]) + y_state[:, head]
        upd = _dot_tn(b_g, xend_b)
        st_scr[:, cols] = _exact_rhs_dot(chunk_dec, expand) * s_g + upd

    y = yh_scr[...] + dsk_ref[...] * xs_scr[...]
    z = z_ref[...]
    y_ref[...] = _rms(y * jax.nn.silu(z), nw_ref[...]).astype(BF16)

    @pl.when(ci == nc - 1)
    def _():
        so_ref[0] = st_scr[...].T


def _ssd(proj, dt_proj, hist8, state, conv_w, conv_b, dt_bias, a_log, d_skip, norm_w, *, bsz, seq_len, c):
    m = proj.shape[0]
    nc = seq_len // c
    assert seq_len % c == 0
    row = lambda b, ci: b * nc + ci
    const2 = lambda b, ci: (0, 0)
    di, bcw = SSD_DINNER, 2 * SSD_GROUPS * SSD_DSTATE
    tri = jnp.asarray(np.tril(np.ones((c, c), np.float32)), BF16)
    expand = np.zeros((LANES, di), np.float32)
    for h in range(SSD_HEADS):
        expand[h, h * SSD_HEADDIM:(h + 1) * SSD_HEADDIM] = 1.0
    expand = jnp.asarray(expand, BF16)
    pad_row = lambda v: jnp.pad(v.astype(F32), (0, LANES - v.shape[0])).reshape(1, LANES)
    st2 = state.reshape(bsz, di, SSD_DSTATE)
    y, s_new = pl.pallas_call(
        functools.partial(_ssd_kernel, c=c, nc=nc),
        grid=(bsz, nc),
        in_specs=[pl.BlockSpec((c, di), lambda b, ci: (row(b, ci), 3)),
                  pl.BlockSpec((c, di), lambda b, ci: (row(b, ci), 4)),
                  pl.BlockSpec((c, bcw), lambda b, ci: (row(b, ci), 10)),
                  pl.BlockSpec((c, LANES), lambda b, ci: (row(b, ci), 0)),
                  pl.BlockSpec((1, SUBLANES, di), lambda b, ci: (b, 0, 0)),
                  pl.BlockSpec((1, SUBLANES, bcw), lambda b, ci: (b, 0, 2)),
                  pl.BlockSpec((1, di, SSD_DSTATE), lambda b, ci: (b, 0, 0)),
                  pl.BlockSpec((SSD_CONV, di), const2),
                  pl.BlockSpec((SSD_CONV, bcw), lambda b, ci: (0, 2)),
                  pl.BlockSpec((1, di), const2),
                  pl.BlockSpec((1, bcw), lambda b, ci: (0, 2)),
                  pl.BlockSpec((1, LANES), const2),
                  pl.BlockSpec((1, LANES), const2),
                  pl.BlockSpec((1, di), const2),
                  pl.BlockSpec((1, di), const2),
                  pl.BlockSpec((c, c), const2),
                  pl.BlockSpec((LANES, di), const2)],
        out_specs=[pl.BlockSpec((c, di), lambda b, ci: (row(b, ci), 0)),
                   pl.BlockSpec((1, di, SSD_DSTATE), lambda b, ci: (b, 0, 0))],
        out_shape=[jax.ShapeDtypeStruct((m, di), BF16),
                   jax.ShapeDtypeStruct(st2.shape, F32)],
        scratch_shapes=[pltpu.VMEM((SSD_DSTATE, di), F32),
                        pltpu.VMEM((SUBLANES, di), F32),
                        pltpu.VMEM((SUBLANES, bcw), F32),
                        pltpu.VMEM((c + SUBLANES, di), F32),
                        pltpu.VMEM((c + SUBLANES, bcw), F32),
                        pltpu.VMEM((c, di), F32),
                        pltpu.VMEM((c, di), F32)],
        compiler_params=_cparams(2),
        name="ssd",
    )(proj, proj, proj, dt_proj, hist8, hist8, st2,
      conv_w, conv_w, conv_b.reshape(1, -1), conv_b.reshape(1, -1),
      pad_row(dt_bias), pad_row(a_log), jnp.repeat(d_skip.astype(F32), SSD_HEADDIM).reshape(1, di),
      norm_w.reshape(1, di), tri, expand)
    return y, s_new.reshape(state.shape)


def _decode_cum_kernel(lfc_ref, fl_ref, b_ref, tri_ref, cumt_c_ref, lfn_ref, cumn_ref, cumt_n_ref,
                       carry_ref, *, c, ncb):
    j = pl.program_id(0)

    @pl.when(j == 0)
    def _():
        carry_ref[...] = jnp.zeros_like(carry_ref)

    @pl.when(j < ncb)
    def _():
        cum = _exact_lhs_dot(tri_ref[...], lfc_ref[...]) + carry_ref[...]
        carry_ref[...] = cum[c - 1:c, :]
        cumt_c_ref[...] = cum.T

    @pl.when(j == ncb)
    def _():
        lf = -_softplus(-(fl_ref[...] + b_ref[...]))
        lfn_ref[...] = lf
        cum = _exact_lhs_dot(tri_ref[0:LANES, 0:LANES], lf) + carry_ref[...]
        cumn_ref[...] = cum
        cumt_n_ref[...] = cum.T


def _decode_cum(cache_lf, fl_new, bias, *, c):
    past = cache_lf.shape[0]
    assert past % c == 0 and c % LANES == 0 and fl_new.shape == (LANES, LANES)
    ncb = past // c
    tri = jnp.asarray(np.tril(np.ones((c, c), np.float32)), BF16)
    blk = lambda j: jnp.minimum(j, ncb - 1)
    sq = jax.ShapeDtypeStruct((LANES, LANES), F32)
    return pl.pallas_call(
        functools.partial(_decode_cum_kernel, c=c, ncb=ncb),
        grid=(ncb + 1,),
        in_specs=[pl.BlockSpec((c, LANES), lambda j: (blk(j), 0)),
                  pl.BlockSpec((LANES, LANES), lambda j: (0, 0)),
                  pl.BlockSpec((1, LANES), lambda j: (0, 0)),
                  pl.BlockSpec((c, c), lambda j: (0, 0))],
        out_specs=[pl.BlockSpec((LANES, c), lambda j: (0, blk(j))),
                   pl.BlockSpec((LANES, LANES), lambda j: (0, 0)),
                   pl.BlockSpec((LANES, LANES), lambda j: (0, 0)),
                   pl.BlockSpec((LANES, LANES), lambda j: (0, 0))],
        out_shape=[jax.ShapeDtypeStruct((LANES, past), F32), sq, sq, sq],
        scratch_shapes=[pltpu.VMEM((1, LANES), F32)],
        compiler_params=_cparams(1),
        name="decode_logf_cumsum",
    )(cache_lf, fl_new, bias, tri)


FOX_AUG = 2 * FOX_HEAD_DIM
N_BIAS_PIECES = 3


def _fox_prep_kernel(q_ref, k_ref, v_ref, fl_ref, fb_ref, tri_ref, place_ref, ones_ref,
                     qa_ref, ka_ref, vb_ref, k32_ref, v32_ref, lf_ref, carry_ref, *, tp):
    @pl.when(pl.program_id(1) == 0)
    def _():
        carry_ref[...] = jnp.zeros_like(carry_ref)

    lf = -_softplus(-(fl_ref[...] + fb_ref[...]))
    lf_ref[...] = lf
    cum = _exact_lhs_dot(tri_ref[...], lf) + carry_ref[...]
    carry_ref[...] = cum[tp - 1:tp, :]
    pieces = _split3(cum * (FOX_HEAD_DIM ** 0.5))
    n = N_BIAS_PIECES
    aug_q = ones_ref[0:1, :] + sum(_dot(pieces[r], place_ref[r]) for r in range(n))
    aug_k = ones_ref[1:2, :] - sum(_dot(pieces[r], place_ref[n + r]) for r in range(n))
    for h in range(FOX_HEADS):
        src = slice(h * FOX_HEAD_DIM, (h + 1) * FOX_HEAD_DIM)
        feat = slice(h * FOX_AUG, h * FOX_AUG + FOX_HEAD_DIM)
        bias = slice(h * FOX_AUG + FOX_HEAD_DIM, (h + 1) * FOX_AUG)
        qa_ref[:, feat] = q_ref[:, src].astype(BF16)
        qa_ref[:, bias] = aug_q[:, src].astype(BF16)
        ka_ref[:, feat] = k_ref[:, src].astype(BF16)
        ka_ref[:, bias] = aug_k[:, src].astype(BF16)
    k = k_ref[...]
    v = v_ref[...]
    k32_ref[...] = pltpu.einshape("m(hd)->mhd", k, h=FOX_HEADS)
    v32_ref[...] = pltpu.einshape("m(hd)->mhd", v, h=FOX_HEADS)
    vb_ref[...] = v.astype(BF16)


def _fox_prep(proj, fl_proj, f_bias, *, bsz, seq_len, tp):
    m = proj.shape[0]
    nt = seq_len // tp
    assert seq_len % tp == 0
    w = FOX_WIDTH
    tri = jnp.asarray(np.tril(np.ones((tp, tp), np.float32)), BF16)
    n = N_BIAS_PIECES
    place = np.zeros((2 * n, LANES, w), np.float32)
    ones = np.zeros((SUBLANES, w), np.float32)
    for h in range(FOX_HEADS):
        for r in range(2 * n):
            place[r, h, h * FOX_HEAD_DIM + r] = 1.0
        ones[0, h * FOX_HEAD_DIM + n:h * FOX_HEAD_DIM + 2 * n] = 1.0
        ones[1, h * FOX_HEAD_DIM:h * FOX_HEAD_DIM + n] = 1.0
    row = lambda b, ti: (b * nt + ti, 0)
    const2 = lambda b, ti: (0, 0)
    return pl.pallas_call(
        functools.partial(_fox_prep_kernel, tp=tp),
        grid=(bsz, nt),
        in_specs=[pl.BlockSpec((tp, w), lambda b, ti: (b * nt + ti, 0)),
                  pl.BlockSpec((tp, w), lambda b, ti: (b * nt + ti, 1)),
                  pl.BlockSpec((tp, w), lambda b, ti: (b * nt + ti, 2)),
                  pl.BlockSpec((tp, LANES), row),
                  pl.BlockSpec((1, LANES), const2),
                  pl.BlockSpec((tp, tp), const2),
                  pl.BlockSpec((2 * n, LANES, w), lambda b, ti: (0, 0, 0)),
                  pl.BlockSpec((SUBLANES, w), const2)],
        out_specs=[pl.BlockSpec((tp, FOX_HEADS * FOX_AUG), row),
                   pl.BlockSpec((tp, FOX_HEADS * FOX_AUG), row),
                   pl.BlockSpec((tp, w), row),
                   pl.BlockSpec((tp, FOX_HEADS, FOX_HEAD_DIM), lambda b, ti: (b * nt + ti, 0, 0)),
                   pl.BlockSpec((tp, FOX_HEADS, FOX_HEAD_DIM), lambda b, ti: (b * nt + ti, 0, 0)),
                   pl.BlockSpec((tp, LANES), row)],
        out_shape=[jax.ShapeDtypeStruct((m, FOX_HEADS * FOX_AUG), BF16),
                   jax.ShapeDtypeStruct((m, FOX_HEADS * FOX_AUG), BF16),
                   jax.ShapeDtypeStruct((m, w), BF16),
                   jax.ShapeDtypeStruct((m, FOX_HEADS, FOX_HEAD_DIM), F32),
                   jax.ShapeDtypeStruct((m, FOX_HEADS, FOX_HEAD_DIM), F32),
                   jax.ShapeDtypeStruct((m, LANES), F32)],
        scratch_shapes=[pltpu.VMEM((1, LANES), F32)],
        compiler_params=_cparams(2),
        name="fox_prep",
    )(proj, proj, proj, fl_proj, f_bias, tri, jnp.asarray(place, BF16), jnp.asarray(ones, F32))


def _fox_kernel(qi_ref, ki_ref, q_ref, k_ref, v_ref, o_ref, m_ref, acc_ref, va_ref, *, t, ts, ts_diag):
    step = pl.program_id(2)
    qi = qi_ref[step]
    ki = ki_ref[step]
    to_log2 = (FOX_HEAD_DIM ** -0.5) * math.log2(math.e)
    hd = FOX_HEAD_DIM

    @pl.when(ki == 0)
    def _():
        m_ref[...] = jnp.full_like(m_ref, NEG_INF)
        acc_ref[...] = jnp.zeros_like(acc_ref)
        va_ref[:, hd:2 * hd] = jnp.ones((t, hd), BF16)

    va_ref[:, 0:hd] = v_ref[...]

    def update(diagonal):
        tc = ts_diag if diagonal else ts
        for r in range(t // tc):
            rows = slice(r * tc, (r + 1) * tc)
            nk = (r + 1) * tc if diagonal else t
            s = _dot_nt(q_ref[rows, :], k_ref[0:nk, :])
            if diagonal:
                ri = lax.broadcasted_iota(jnp.int32, (tc, nk), 0) + r * tc
                ci = lax.broadcasted_iota(jnp.int32, (tc, nk), 1)
                s = jnp.where(ci <= ri, s, NEG_INF)
            m_old = m_ref[rows, :]
            m_new = jnp.maximum(m_old, jnp.max(s, axis=1, keepdims=True))
            m_ref[rows, :] = m_new
            alpha = jnp.exp2((m_old - m_new) * to_log2)
            p = jnp.exp2((s - _widen(m_new, nk)) * to_log2)
            acc_ref[rows, :] = _widen(alpha, 2 * hd) * acc_ref[rows, :] + _dot(p.astype(BF16), va_ref[0:nk, :])

    @pl.when(ki < qi)
    def _():
        update(False)

    @pl.when(ki == qi)
    def _():
        update(True)
        o_ref[...] = (acc_ref[:, 0:hd] / acc_ref[:, hd:2 * hd]).astype(BF16)


def _fox_prompt(qa, ka, vb, *, bsz, seq_len, t, ts, ts_diag):
    m = qa.shape[0]
    nq = seq_len // t
    assert seq_len % t == 0 and t % ts == 0 and t % ts_diag == 0
    pairs = [(qi, ki) for qi in range(nq) for ki in range(qi + 1)]
    qi_tab = jnp.asarray([p[0] for p in pairs], jnp.int32)
    ki_tab = jnp.asarray([p[1] for p in pairs], jnp.int32)
    grid_spec = pltpu.PrefetchScalarGridSpec(
        num_scalar_prefetch=2,
        grid=(bsz, FOX_HEADS, len(pairs)),
        in_specs=[pl.BlockSpec((t, FOX_AUG), lambda b, h, s, qi, ki: (b * nq + qi[s], h)),
                  pl.BlockSpec((t, FOX_AUG), lambda b, h, s, qi, ki: (b * nq + ki[s], h)),
                  pl.BlockSpec((t, FOX_HEAD_DIM), lambda b, h, s, qi, ki: (b * nq + ki[s], h))],
        out_specs=pl.BlockSpec((t, FOX_HEAD_DIM), lambda b, h, s, qi, ki: (b * nq + qi[s], h)),
        scratch_shapes=[pltpu.VMEM((t, LANES), F32), pltpu.VMEM((t, 2 * FOX_HEAD_DIM), F32),
                        pltpu.VMEM((t, 2 * FOX_HEAD_DIM), BF16)],
    )
    return pl.pallas_call(
        functools.partial(_fox_kernel, t=t, ts=ts, ts_diag=ts_diag),
        grid_spec=grid_spec,
        out_shape=jax.ShapeDtypeStruct((m, FOX_WIDTH), BF16),
        compiler_params=_cparams(3),
        name="fox_attention",
    )(qi_tab, ki_tab, qa, ka, vb)


def _fox_decode_kernel(q_ref, kn_ref, vn_ref, kc_ref, vc_ref, cq_ref, ckc_ref, ckn_ref, o_ref,
                       m_ref, l_ref, acc_ref, *, lq, ncb):
    j = pl.program_id(1)
    nh, hd = FOX_HEADS, FOX_HEAD_DIM

    @pl.when(j == 0)
    def _():
        m_ref[...] = jnp.full_like(m_ref, NEG_INF)
        l_ref[...] = jnp.zeros_like(l_ref)
        acc_ref[...] = jnp.zeros_like(acc_ref)

    def attend(k_head, v_head, ck_head, causal):
        for h in range(nh):
            qh = q_ref[:, h * hd:(h + 1) * hd].astype(BF16)
            ck = ck_head(h)
            tk = ck.shape[1]
            s = _dot_nt(qh, k_head(h).astype(BF16)) * (hd ** -0.5)
            s = s + (_widen(cq_ref[h], tk) - ck)
            if causal:
                rows = lax.broadcasted_iota(jnp.int32, (lq, tk), 0)
                cols = lax.broadcasted_iota(jnp.int32, (lq, tk), 1)
                s = jnp.where(cols <= rows, s, NEG_INF)
            m_old = m_ref[h]
            m_new = jnp.maximum(m_old, jnp.max(s, axis=1, keepdims=True))
            alpha = jnp.exp(m_old - m_new)
            p = jnp.exp(s - _widen(m_new, tk))
            l_ref[h] = alpha * l_ref[h] + jnp.sum(p, axis=1, keepdims=True)
            acc_ref[h] = alpha * acc_ref[h] + _dot(p.astype(BF16), v_head(h).astype(BF16))
            m_ref[h] = m_new

    @pl.when(j < ncb)
    def _():
        k_hm = pltpu.einshape("mhd->hmd", kc_ref[0])
        v_hm = pltpu.einshape("mhd->hmd", vc_ref[0])
        attend(lambda h: k_hm[h], lambda h: v_hm[h], lambda h: ckc_ref[h], False)

    @pl.when(j == ncb)
    def _():
        attend(lambda h: kn_ref[:, h * hd:(h + 1) * hd], lambda h: vn_ref[:, h * hd:(h + 1) * hd],
               lambda h: ckn_ref[h][:, 0:lq], True)
        for h in range(nh):
            o_ref[:, h * hd:(h + 1) * hd] = (acc_ref[h] / l_ref[h]).astype(BF16)


def _fox_decode(proj, cache_k, cache_v, cq, ck_cache, ck_new, *, bsz, lq, tk):
    past = cache_k.shape[1]
    assert past % tk == 0 and tk % LANES == 0 and lq <= LANES
    ncb = past // tk
    nh, hd, w = FOX_HEADS, FOX_HEAD_DIM, FOX_WIDTH
    tile = lambda j: jnp.minimum(j, ncb - 1)
    cache_spec = pl.BlockSpec((1, tk, nh, hd), lambda b, j: (b, tile(j), 0, 0))
    return pl.pallas_call(
        functools.partial(_fox_decode_kernel, lq=lq, ncb=ncb),
        grid=(bsz, ncb + 1),
        in_specs=[pl.BlockSpec((lq, w), lambda b, j: (b, 0)),
                  pl.BlockSpec((lq, w), lambda b, j: (b, 1)),
                  pl.BlockSpec((lq, w), lambda b, j: (b, 2)),
                  cache_spec,
                  cache_spec,
                  pl.BlockSpec((nh, lq, LANES), lambda b, j: (b, 0, 0)),
                  pl.BlockSpec((nh, 1, tk), lambda b, j: (b, 0, tile(j))),
                  pl.BlockSpec((nh, 1, LANES), lambda b, j: (b, 0, 0))],
        out_specs=pl.BlockSpec((lq, w), lambda b, j: (b, 0)),
        out_shape=jax.ShapeDtypeStruct((bsz * lq, w), BF16),
        scratch_shapes=[pltpu.VMEM((nh, lq, LANES), F32), pltpu.VMEM((nh, lq, LANES), F32),
                        pltpu.VMEM((nh, lq, hd), F32)],
        compiler_params=_cparams(2),
        name="fox_decode",
    )(proj, proj, proj, cache_k, cache_v, cq, ck_cache, ck_new)


def _sconv_kernel(u_ref, bg_ref, cg_ref, cw_ref, hist_ref, y_ref, tail_ref, carry_ref, ext_ref,
                  *, tm, rows, spt, tpb):
    i = pl.program_id(0)
    w = cg_ref[...] * u_ref[...]
    first = (i % tpb) == 0
    for s in range(spt):
        w_s = w[s * rows:(s + 1) * rows]
        if tpb == 1:
            prev = hist_ref[s]
        else:
            prev = jnp.where(first, hist_ref[s], carry_ref[...])
        conv = _conv_rows(ext_ref, w_s, prev, cw_ref, SC_WIDTH, rows)
        y_ref[s * rows:(s + 1) * rows, :] = (bg_ref[s * rows:(s + 1) * rows, :] * conv).astype(BF16)
        tail_ref[s] = w_s[rows - SUBLANES:rows]
    if tpb > 1:
        carry_ref[...] = w[tm - SUBLANES:tm]


def _sconv(proj, conv_w, hist8, *, seq_len, tm):
    m = proj.shape[0]
    assert m % tm == 0
    rows, spt, tpb = _seq_tiling(seq_len, tm)
    nm = m // tm
    hist_map = (lambda i: (i // tpb, 0, 0)) if spt == 1 else (lambda i: (i, 0, 0))
    base = 3 * FOX_WIDTH // SC_DIM
    return pl.pallas_call(
        functools.partial(_sconv_kernel, tm=tm, rows=rows, spt=spt, tpb=tpb),
        grid=(nm,),
        in_specs=[pl.BlockSpec((tm, SC_DIM), lambda i: (i, base)),
                  pl.BlockSpec((tm, SC_DIM), lambda i: (i, base + 1)),
                  pl.BlockSpec((tm, SC_DIM), lambda i: (i, base + 2)),
                  pl.BlockSpec((SC_WIDTH, SC_DIM), lambda i: (0, 0)),
                  pl.BlockSpec((spt, SUBLANES, SC_DIM), hist_map)],
        out_specs=[pl.BlockSpec((tm, SC_DIM), lambda i: (i, 0)),
                   pl.BlockSpec((spt, SUBLANES, SC_DIM), lambda i: (i, 0, 0))],
        out_shape=[jax.ShapeDtypeStruct((m, SC_DIM), BF16),
                   jax.ShapeDtypeStruct((nm * spt, SUBLANES, SC_DIM), F32)],
        scratch_shapes=[pltpu.VMEM((SUBLANES, SC_DIM), F32),
                        pltpu.VMEM((rows + SUBLANES, SC_DIM), F32)],
        compiler_params=_cparams(1),
        name="gated_short_conv",
    )(proj, proj, proj, conv_w, hist8)


def _hist8(state):
    n, w1, c = state.shape
    return jnp.concatenate([jnp.zeros((n, SUBLANES - w1, c), F32), state.astype(F32)], axis=1)


def _tails(tails, n_seq, seq_len, tile_rows, keep):
    per_seq = max(1, seq_len // tile_rows)
    pieces = tails.reshape(n_seq, per_seq, SUBLANES, tails.shape[-1])
    return pieces[:, per_seq - 1, SUBLANES - keep:, :]


def _rope_tables(pos0, length):
    half = RET_DK // 2
    inv = ROPE_BASE ** (-np.arange(half, dtype=np.float64) / half)
    ang = (pos0 + np.arange(length, dtype=np.float64))[:, None] * inv[None, :]
    cos, sin = np.cos(ang), np.sin(ang)
    return (jnp.asarray(np.concatenate([cos, cos], axis=1), F32),
            jnp.asarray(np.concatenate([-sin, sin], axis=1), F32))


def _prep_weights(p):
    d = D_MODEL
    ab_in = p['ab_w_in'][0]
    cd_in = p['cd_w_in'][0]
    f0 = 3 * FOX_WIDTH
    pad_cols = lambda w: jnp.pad(w, ((0, 0), (0, LANES - w.shape[1]))).astype(BF16)
    return dict(
        ab_in=ab_in.astype(BF16),
        ab_small=pad_cols(ab_in[:, AB_MAIN:]),
        cd_in=jnp.concatenate([cd_in[:, :f0], cd_in[:, f0 + FOX_HEADS:]], axis=1).astype(BF16),
        cd_small=pad_cols(cd_in[:, f0:f0 + FOX_HEADS]),
        ab_out=p['ab_w_out'][0].astype(BF16),
        cd_out=p['cd_w_out'][0].astype(BF16),
        ffn0=tuple(p[name][0:1].astype(BF16) for name in ('ffn_w_gate', 'ffn_w_up', 'ffn_w_down')),
        ffn1=None,
    )


def _trunk(x, pos0, st_ret, st_ssd, st_ssd_conv, c_k, c_v, c_logf, st_sconv, st_ffn, p, wb, t):
    bsz, length, d = x.shape
    m = bsz * length
    xf = x.reshape(m, d)
    zeros = lambda *shape: jnp.zeros(shape, F32)

    proj, dt_proj = _norm_matmul(xf, p['ab_norm_w'][0], wb['ab_in'], wb['ab_small'],
                                 n=AB_MAIN, tm=t['tm_ab'], tn=t['tn_ab'])
    cosf, sinf = _rope_tables(pos0, length)
    ret_state = zeros(bsz, RET_HEADS, RET_DK, RET_DV) if st_ret is None else st_ret
    y_ret, ret_new = _retention(proj, cosf, sinf, ret_state, p['ret_norm_w'][0],
                                bsz=bsz, seq_len=length, c=t['c_ret'])
    ssd_state = zeros(bsz, SSD_HEADS, SSD_HEADDIM, SSD_DSTATE) if st_ssd is None else st_ssd
    ssd_hist = zeros(bsz, SSD_CONV - 1, SSD_CONV_DIM) if st_ssd_conv is None else st_ssd_conv
    y_ssd, ssd_new = _ssd(proj, dt_proj, _hist8(ssd_hist), ssd_state, p['ssd_conv_w'][0], p['ssd_conv_b'][0],
                          p['ssd_dt_bias'][0], p['ssd_A_log'][0], p['ssd_D'][0], p['ssd_norm_w'][0],
                          bsz=bsz, seq_len=length, c=t['c_ssd'])
    xbc_lo = AB_MAIN - SSD_CONV_DIM
    ssd_conv_new = proj.reshape(bsz, length, -1)[:, length - (SSD_CONV - 1):, xbc_lo:AB_MAIN]
    xf = _proj_residual(xf, y_ret, y_ssd, wb['ab_out'], tm=t['tm_out'], tn=t['tn_out'])

    ffn_new = []
    ffn_hist0 = zeros(bsz, FFN_CONV - 1, D_FF) if st_ffn is None else st_ffn[0]
    next_f32 = tuple(p[name][1] for name in ('ffn_w_gate', 'ffn_w_up', 'ffn_w_down'))
    n_row_tiles = m // t['tm_ffn']
    if wb['ffn1'] is None and not (d % (n_row_tiles * LANES) == 0 and D_FF % t['tf_ffn'] == 0):
        wb['ffn1'] = tuple(w[None].astype(BF16) for w in next_f32)
    xf, tails, next_bf16 = _conv_ffn(
        xf, p['ffn_norm_w'][0], *wb['ffn0'], p['ffn_conv_w'][0], p['ffn_conv_b'][0], _hist8(ffn_hist0),
        p['final_norm_w'], layer=0, seq_len=length, tm=t['tm_ffn'], tf=t['tf_ffn'], ts=t['ts_ffn'], final=False,
        cast_next=next_f32 if wb['ffn1'] is None else None)
    if wb['ffn1'] is None:
        wb['ffn1'] = tuple(w[None] for w in next_bf16)
    ffn_new.append(_tails(tails, bsz, length, t['tm_ffn'], FFN_CONV - 1))

    proj, fl_proj = _norm_matmul(xf, p['cd_norm_w'][0], wb['cd_in'], wb['cd_small'],
                                 n=CD_MAIN, tm=t['tm_cd'], tn=t['tn_cd'])
    f_bias = jnp.pad(p['fox_f_bias'][0].astype(F32), (0, LANES - FOX_HEADS)).reshape(1, LANES)
    head_shape = (bsz, length, FOX_HEADS, FOX_HEAD_DIM)
    if c_k is None:
        qa, ka, vb, k32, v32, logf = _fox_prep(proj, fl_proj, f_bias, bsz=bsz, seq_len=length, tp=t['t_prep'])
        y_fox = _fox_prompt(qa, ka, vb, bsz=bsz, seq_len=length, t=t['t_fox'], ts=t['ts_fox'],
                            ts_diag=t['ts_fox_diag'])
        logf_new = logf.reshape(bsz, length, LANES)[:, :, :FOX_HEADS]
        k_new, v_new = k32.reshape(head_shape), v32.reshape(head_shape)
    else:
        proj3 = proj.reshape(bsz, length, -1)
        k_new = proj3[:, :, FOX_WIDTH:2 * FOX_WIDTH].reshape(head_shape)
        v_new = proj3[:, :, 2 * FOX_WIDTH:3 * FOX_WIDTH].reshape(head_shape)
        past = c_k.shape[1]
        pairs = bsz * FOX_HEADS
        assert pairs <= LANES and length <= LANES
        to_lanes = lambda a, rows: jnp.pad(jnp.swapaxes(a, 0, 1).reshape(a.shape[1], pairs),
                                           ((0, rows - a.shape[1]), (0, LANES - pairs)))
        from_lanes = lambda a: jnp.swapaxes(a[:length, :pairs].reshape(length, bsz, FOX_HEADS), 0, 1)
        cache_lf = to_lanes(c_logf.astype(F32), past)
        fl_rows = to_lanes(fl_proj.reshape(bsz, length, LANES)[:, :, :FOX_HEADS], LANES)
        bias_lanes = jnp.pad(jnp.tile(p['fox_f_bias'][0].astype(F32), bsz), (0, LANES - pairs)).reshape(1, LANES)
        cum_t_cache, lf_rows, cum_rows, cum_t_new = _decode_cum(cache_lf, fl_rows, bias_lanes, c=t['c_cum'])
        logf_new = from_lanes(lf_rows)
        cq = jnp.broadcast_to(cum_rows[:length, :pairs].T[:, :, None], (pairs, length, LANES))
        y_fox = _fox_decode(proj, c_k, c_v, cq, cum_t_cache[:pairs, None, :], cum_t_new[:pairs, None, :],
                            bsz=bsz, lq=length, tk=t['tk_dec'])
    sc_hist = zeros(bsz, SC_WIDTH - 1, SC_DIM) if st_sconv is None else st_sconv
    y_sc, sc_tails = _sconv(proj, p['sconv_w'][0], _hist8(sc_hist), seq_len=length, tm=t['tm_sc'])
    sconv_new = _tails(sc_tails, bsz, length, t['tm_sc'], SC_WIDTH - 1)
    xf = _proj_residual(xf, y_fox, y_sc, wb['cd_out'], tm=t['tm_out'], tn=t['tn_out'])

    ffn_hist1 = zeros(bsz, FFN_CONV - 1, D_FF) if st_ffn is None else st_ffn[1]
    xf, tails, _ = _conv_ffn(xf, p['ffn_norm_w'][1], *wb['ffn1'], p['ffn_conv_w'][1], p['ffn_conv_b'][1],
                             _hist8(ffn_hist1), p['final_norm_w'], layer=0, seq_len=length, tm=t['tm_ffn'],
                             tf=t['tf_ffn'], ts=t['ts_ffn'], final=True)
    ffn_new.append(_tails(tails, bsz, length, t['tm_ffn'], FFN_CONV - 1))

    return (xf.reshape(bsz, length, d), ret_new[None], ssd_new[None], ssd_conv_new[None], k_new[None],
            v_new[None], logf_new[None], sconv_new[None], jnp.stack(ffn_new))


def _largest_divisor(n, cap, multiple=1):
    best = None
    for cand in range(multiple, min(n, cap) + 1, multiple):
        if n % cand == 0:
            best = cand
    assert best is not None, (n, cap, multiple)
    return best


def _tiles(bsz, length, past=None):
    m = bsz * length
    seq_tile = lambda cap: _largest_divisor(length, cap, SUBLANES)
    row_tile = lambda cap: (_largest_divisor(length, cap, SUBLANES) if length >= cap
                            else _largest_divisor(m, cap, length))
    t = dict(
        tm_ab=row_tile(1024), tn_ab=2048, tm_cd=row_tile(1024), tn_cd=2048,
        tm_out=row_tile(1024), tn_out=2048,
        tm_ffn=row_tile(1024), tf_ffn=512, ts_ffn=512,
        tm_sc=row_tile(1024),
        c_ret=seq_tile(256), c_ssd=seq_tile(256),
    )
    if past is None:
        t['t_fox'] = seq_tile(2048)
        t['ts_fox'] = _largest_divisor(t['t_fox'], 256, LANES)
        t['ts_fox_diag'] = _largest_divisor(t['t_fox'], 512, LANES)
        t['t_prep'] = seq_tile(512)
    else:
        t['tk_dec'] = _largest_divisor(past, 1024, LANES)
        t['c_cum'] = _largest_divisor(past, 256, LANES)
    return t


def kernel(x_prompt, x_sample, state_ret, state_ssd, state_ssd_conv, cache_fox_k, cache_fox_v, cache_fox_logf, state_sconv, state_ffn_conv, ab_norm_w, ab_w_in, ret_norm_w, ssd_conv_w, ssd_conv_b, ssd_dt_bias, ssd_A_log, ssd_D, ssd_norm_w, ab_w_out, cd_norm_w, cd_w_in, fox_f_bias, sconv_w, cd_w_out, ffn_norm_w, ffn_w_gate, ffn_w_up, ffn_conv_w, ffn_conv_b, ffn_w_down, final_norm_w):
    p = dict(ab_norm_w=ab_norm_w, ab_w_in=ab_w_in, ret_norm_w=ret_norm_w, ssd_conv_w=ssd_conv_w,
             ssd_conv_b=ssd_conv_b, ssd_dt_bias=ssd_dt_bias, ssd_A_log=ssd_A_log, ssd_D=ssd_D,
             ssd_norm_w=ssd_norm_w, ab_w_out=ab_w_out, cd_norm_w=cd_norm_w, cd_w_in=cd_w_in,
             fox_f_bias=fox_f_bias, sconv_w=sconv_w, cd_w_out=cd_w_out, ffn_norm_w=ffn_norm_w,
             ffn_w_gate=ffn_w_gate, ffn_w_up=ffn_w_up, ffn_conv_w=ffn_conv_w, ffn_conv_b=ffn_conv_b,
             ffn_w_down=ffn_w_down, final_norm_w=final_norm_w)
    assert x_prompt.shape[-1] == D_MODEL and ab_w_in.shape == (1, D_MODEL, AB_MAIN + SSD_HEADS)
    assert cd_w_in.shape == (1, D_MODEL, CD_MAIN + FOX_HEADS) and ffn_w_gate.shape == (2, D_MODEL, D_FF)
    wb = _prep_weights(p)
    bp, lp_, _ = x_prompt.shape
    bs, ls, _ = x_sample.shape
    past = cache_fox_k.shape[2]
    (y_prompt, p_ret, p_ssd, p_ssd_conv, p_fox_k, p_fox_v, p_fox_logf, p_sconv, p_ffn_conv) = _trunk(
        x_prompt, 0, None, None, None, None, None, None, None, None, p, wb, _tiles(bp, lp_))
    (y_sample, s_ret, s_ssd, s_ssd_conv, s_fox_k, s_fox_v, s_fox_logf, s_sconv, s_ffn_conv) = _trunk(
        x_sample, past, state_ret[0], state_ssd[0], state_ssd_conv[0], cache_fox_k[0], cache_fox_v[0],
        cache_fox_logf[0], state_sconv[0], state_ffn_conv, p, wb, _tiles(bs, ls, past))
    return (y_prompt, y_sample, p_ret, s_ret, p_ssd, s_ssd, p_ssd_conv, s_ssd_conv, p_fox_k, s_fox_k,
            p_fox_v, s_fox_v, p_fox_logf, s_fox_logf, p_sconv, s_sconv, p_ffn_conv, s_ffn_conv)
```

```python
import functools
import math

import numpy as np
import jax
import jax.numpy as jnp
from jax import lax
from jax.experimental import pallas as pl
from jax.experimental.pallas import tpu as pltpu

F32 = jnp.float32
BF16 = jnp.bfloat16
EPS = 1e-6
ROPE_BASE = 10000.0
NEG_INF = float("-inf")

D_MODEL = 2048
RET_HEADS, RET_DK, RET_DV = 4, 128, 256
SSD_DINNER, SSD_HEADDIM, SSD_HEADS, SSD_GROUPS, SSD_DSTATE, SSD_CONV = 1024, 64, 16, 2, 128, 4
SSD_CONV_DIM = SSD_DINNER + 2 * SSD_GROUPS * SSD_DSTATE
FOX_HEADS, FOX_HEAD_DIM = 8, 128
FOX_WIDTH = FOX_HEADS * FOX_HEAD_DIM
SC_DIM, SC_WIDTH = 1024, 3
D_FF, FFN_CONV = 5632, 3
AB_MAIN = 2 * RET_HEADS * RET_DK + 2 * RET_HEADS * RET_DV + SSD_DINNER + SSD_CONV_DIM
AB_PAD = AB_MAIN + 128
CD_MAIN = 3 * FOX_WIDTH + 3 * SC_DIM
CD_PAD = CD_MAIN + 128

LANES = 128
SUBLANES = 8
VMEM_LIMIT = 60 * 1024 * 1024


def _cparams(n_axes):
    return pltpu.CompilerParams(dimension_semantics=("arbitrary",) * n_axes,
                                vmem_limit_bytes=VMEM_LIMIT)


def _rms(xf, w):
    return xf * lax.rsqrt(jnp.mean(xf * xf, axis=-1, keepdims=True) + EPS) * w


def _softplus(x):
    return jnp.maximum(x, 0.0) + jnp.log1p(jnp.exp(-jnp.abs(x)))


def _split3(x):
    hi = x.astype(BF16)
    r1 = x - hi.astype(F32)
    mid = r1.astype(BF16)
    lo = (r1 - mid.astype(F32)).astype(BF16)
    return hi, mid, lo


def _widen(x, n):
    return x[:, 0:n] if n <= LANES else jnp.concatenate([x] * (n // LANES), axis=1)


def _dot(a, b):
    return jnp.dot(a, b, preferred_element_type=F32)


def _dot_nt(a, b):
    return lax.dot_general(a, b, (((1,), (1,)), ((), ())), preferred_element_type=F32)


def _dot_tn(a, b):
    return lax.dot_general(a, b, (((0,), (0,)), ((), ())), preferred_element_type=F32)


def _exact_lhs_dot(m_bf16, x):
    hi, mid, lo = _split3(x)
    return _dot(m_bf16, hi) + _dot(m_bf16, mid) + _dot(m_bf16, lo)


def _exact_rhs_dot(x, m_bf16):
    hi, mid, lo = _split3(x)
    return _dot(hi, m_bf16) + _dot(mid, m_bf16) + _dot(lo, m_bf16)


def _conv_rows(ext_ref, x, prev8, w_ref, width, rows, w_cols=slice(None)):
    ext_ref[0:SUBLANES, :] = prev8
    ext_ref[SUBLANES:SUBLANES + rows, :] = x
    out = None
    for j in range(width):
        off = SUBLANES - (width - 1) + j
        term = ext_ref[off:off + rows, :] * w_ref[j:j + 1, w_cols]
        out = term if out is None else out + term
    return out


def _seq_tiling(seq_len, tile_rows):
    if seq_len >= tile_rows:
        assert seq_len % tile_rows == 0
        return tile_rows, 1, seq_len // tile_rows
    assert tile_rows % seq_len == 0 and seq_len % SUBLANES == 0
    return seq_len, tile_rows // seq_len, 1


def _norm_matmul_kernel(x_ref, nw_ref, w_ref, ws_ref, o_ref, os_ref, h_ref, *, n_tiles, tn, last_w):
    j = pl.program_id(1)

    @pl.when(j == 0)
    def _():
        h_ref[...] = _rms(x_ref[...], nw_ref[...]).astype(BF16)
        os_ref[...] = _dot(h_ref[...], ws_ref[...])

    if last_w == tn:
        o_ref[...] = _dot(h_ref[...], w_ref[...])
    else:
        @pl.when(j < n_tiles - 1)
        def _():
            o_ref[...] = _dot(h_ref[...], w_ref[...])

        @pl.when(j == n_tiles - 1)
        def _():
            o_ref[:, 0:last_w] = _dot(h_ref[...], w_ref[:, 0:last_w])


def _norm_matmul(x, norm_w, w, w_small, *, n, tm, tn):
    m, d = x.shape
    n_tiles = pl.cdiv(n, tn)
    last_w = n - (n_tiles - 1) * tn
    assert m % tm == 0 and last_w % LANES == 0 and n <= w.shape[1] and w_small.shape == (d, LANES)
    return pl.pallas_call(
        functools.partial(_norm_matmul_kernel, n_tiles=n_tiles, tn=tn, last_w=last_w),
        grid=(m // tm, n_tiles),
        in_specs=[pl.BlockSpec((tm, d), lambda i, j: (i, 0)),
                  pl.BlockSpec((1, d), lambda i, j: (0, 0)),
                  pl.BlockSpec((d, tn), lambda i, j: (0, j)),
                  pl.BlockSpec((d, LANES), lambda i, j: (0, 0))],
        out_specs=[pl.BlockSpec((tm, tn), lambda i, j: (i, j)),
                   pl.BlockSpec((tm, LANES), lambda i, j: (i, 0))],
        out_shape=[jax.ShapeDtypeStruct((m, n), F32),
                   jax.ShapeDtypeStruct((m, LANES), F32)],
        scratch_shapes=[pltpu.VMEM((tm, d), BF16)],
        compiler_params=_cparams(2),
        name="norm_in_proj",
    )(x, norm_w.reshape(1, d), w, w_small)


def _proj_res_kernel(x_ref, a_ref, b_ref, wa_ref, wb_ref, o_ref):
    acc = _dot(a_ref[...], wa_ref[...])
    acc = acc + _dot(b_ref[...], wb_ref[...])
    o_ref[...] = x_ref[...] + acc


def _proj_residual(x, a, b, w, *, tm, tn):
    m, d = x.shape
    ka, kb = a.shape[1], b.shape[1]
    assert m % tm == 0 and d % tn == 0 and ka == kb and w.shape == (ka + kb, d)
    return pl.pallas_call(
        _proj_res_kernel,
        grid=(m // tm, d // tn),
        in_specs=[pl.BlockSpec((tm, tn), lambda i, j: (i, j)),
                  pl.BlockSpec((tm, ka), lambda i, j: (i, 0)),
                  pl.BlockSpec((tm, kb), lambda i, j: (i, 0)),
                  pl.BlockSpec((ka, tn), lambda i, j: (0, j)),
                  pl.BlockSpec((kb, tn), lambda i, j: (1, j))],
        out_specs=pl.BlockSpec((tm, tn), lambda i, j: (i, j)),
        out_shape=jax.ShapeDtypeStruct((m, d), F32),
        compiler_params=_cparams(2),
        name="out_proj_residual",
    )(x, a, b, w, w)


def _ffn_kernel(*refs, tm, rows, spt, tpb, nf, ts, nsub, nsub_last, final, n_cast):
    x_ref, nw_ref, wg_ref, wu_ref, wd_ref, cw_ref, cb_ref, hist_ref, fw_ref = refs[:9]
    cast_in = refs[9:9 + n_cast]
    o_ref, tail_ref = refs[9 + n_cast:11 + n_cast]
    cast_out = refs[11 + n_cast:11 + 2 * n_cast]
    h_ref, carry_ref, ext_ref = refs[11 + 2 * n_cast:]
    i = pl.program_id(0)
    f = pl.program_id(1)

    for src_ref, dst_ref in zip(cast_in, cast_out):
        dst_ref[...] = src_ref[...].astype(BF16)

    @pl.when(f == 0)
    def _():
        xf = x_ref[...]
        h_ref[...] = _rms(xf, nw_ref[...]).astype(BF16)
        o_ref[...] = xf

    first = (i % tpb) == 0

    def sub_block(sb):
        cols = slice(sb * ts, (sb + 1) * ts)
        h = h_ref[...]
        a = _dot(h, wg_ref[:, cols])
        u = _dot(h, wu_ref[:, cols])
        convs = []
        for s in range(spt):
            a_s = a[s * rows:(s + 1) * rows]
            if tpb == 1:
                prev = hist_ref[s, :, cols]
            else:
                prev = jnp.where(first, hist_ref[s, :, cols], carry_ref[f * nsub + sb])
            convs.append(_conv_rows(ext_ref, a_s, prev, cw_ref, FFN_CONV, rows, cols))
            tail_ref[s, :, cols] = a_s[rows - SUBLANES:rows]
        if tpb > 1:
            carry_ref[f * nsub + sb] = a[tm - SUBLANES:tm]
        conv = convs[0] if spt == 1 else jnp.concatenate(convs, axis=0)
        act = (jax.nn.silu(conv + cb_ref[:, cols]) * u).astype(BF16)
        o_ref[...] += _dot(act, wd_ref[cols, :])

    if nsub_last == nsub:
        for sb in range(nsub):
            sub_block(sb)
    else:
        @pl.when(f < nf - 1)
        def _():
            for sb in range(nsub):
                sub_block(sb)

        @pl.when(f == nf - 1)
        def _():
            for sb in range(nsub_last):
                sub_block(sb)

    if final:
        @pl.when(f == nf - 1)
        def _():
            o_ref[...] = _rms(o_ref[...], fw_ref[...])


def _conv_ffn(x, norm_w, wg, wu, wd, conv_w, conv_b, hist8, final_w, *, layer, seq_len, tm, tf, ts, final,
              cast_next=None):
    m, d = x.shape
    ff = wg.shape[2]
    assert m % tm == 0 and tf % ts == 0 and ff % ts == 0
    rows, spt, tpb = _seq_tiling(seq_len, tm)
    nm, nf = m // tm, pl.cdiv(ff, tf)
    nsub = tf // ts
    nsub_last = (ff - (nf - 1) * tf) // ts
    hist_map = (lambda i, f: (i // tpb, 0, f)) if spt == 1 else (lambda i, f: (i, 0, f))
    cast_in, cast_specs, cast_src_specs = [], [], []
    if cast_next is not None:
        assert d % nm == 0 and (d // nm) % LANES == 0 and ff % nf == 0 and (ff // nf) % LANES == 0
        dr, fc = d // nm, ff // nf
        cast_layer, cast_in = cast_next[0], list(cast_next[1:])
        cast_specs = [pl.BlockSpec((dr, fc), lambda i, f: (i, f)),
                      pl.BlockSpec((dr, fc), lambda i, f: (i, f)),
                      pl.BlockSpec((fc, dr), lambda i, f: (f, i))]
        cast_src_specs = [pl.BlockSpec((None, dr, fc), lambda i, f: (cast_layer, i, f)),
                          pl.BlockSpec((None, dr, fc), lambda i, f: (cast_layer, i, f)),
                          pl.BlockSpec((None, fc, dr), lambda i, f: (cast_layer, f, i))]
    kern = functools.partial(_ffn_kernel, tm=tm, rows=rows, spt=spt, tpb=tpb, nf=nf, ts=ts, nsub=nsub,
                             nsub_last=nsub_last, final=final, n_cast=len(cast_in))
    results = pl.pallas_call(
        kern,
        grid=(nm, nf),
        in_specs=[pl.BlockSpec((tm, d), lambda i, f: (i, 0)),
                  pl.BlockSpec((1, d), lambda i, f: (0, 0)),
                  pl.BlockSpec((None, d, tf), lambda i, f: (layer, 0, f)),
                  pl.BlockSpec((None, d, tf), lambda i, f: (layer, 0, f)),
                  pl.BlockSpec((None, tf, d), lambda i, f: (layer, f, 0)),
                  pl.BlockSpec((FFN_CONV, tf), lambda i, f: (0, f)),
                  pl.BlockSpec((1, tf), lambda i, f: (0, f)),
                  pl.BlockSpec((spt, SUBLANES, tf), hist_map),
                  pl.BlockSpec((1, d), lambda i, f: (0, 0))] + cast_src_specs,
        out_specs=[pl.BlockSpec((tm, d), lambda i, f: (i, 0)),
                   pl.BlockSpec((spt, SUBLANES, tf), lambda i, f: (i, 0, f))] + cast_specs,
        out_shape=[jax.ShapeDtypeStruct((m, d), F32),
                   jax.ShapeDtypeStruct((nm * spt, SUBLANES, ff), F32)]
                  + [jax.ShapeDtypeStruct(w.shape[1:], BF16) for w in cast_in],
        scratch_shapes=[pltpu.VMEM((tm, d), BF16),
                        pltpu.VMEM((nf * nsub, SUBLANES, ts), F32),
                        pltpu.VMEM((rows + SUBLANES, ts), F32)],
        compiler_params=_cparams(2),
        name="conv_ffn",
    )(x, norm_w.reshape(1, d), wg, wu, wd, conv_w, conv_b.reshape(1, ff), hist8, final_w.reshape(1, d), *cast_in)
    return results[0], results[1], tuple(results[2:])


def _retention_kernel(q_ref, k_ref, v_ref, g_ref, cos_ref, sin_ref, st_ref, nw_ref,
                      y_ref, so_ref, *, c):
    ci = pl.program_id(1)

    @pl.when(ci == 0)
    def _():
        so_ref[...] = st_ref[...]

    cos = cos_ref[...]
    sin = sin_ref[...]
    ii = lax.broadcasted_iota(jnp.int32, (c, c), 0)
    jj = lax.broadcasted_iota(jnp.int32, (c, c), 1)
    diff = (ii - jj).astype(F32)
    causal = ii >= jj
    ridx = lax.broadcasted_iota(jnp.int32, (c, 1), 0).astype(F32)
    for h in range(RET_HEADS):
        lg = math.log1p(-(2.0 ** (-5.0 - h)))
        q = q_ref[:, h * RET_DK:(h + 1) * RET_DK]
        k = k_ref[:, h * RET_DK:(h + 1) * RET_DK]
        v = v_ref[:, h * RET_DV:(h + 1) * RET_DV]
        qr = q * cos + pltpu.roll(q, RET_DK // 2, 1) * sin
        kr = (k * cos + pltpu.roll(k, RET_DK // 2, 1) * sin) * (RET_DK ** -0.5)
        qb = qr.astype(BF16)
        kb = kr.astype(BF16)
        vb = v.astype(BF16)
        decay = jnp.exp(jnp.where(causal, diff * lg, NEG_INF))
        inner = jnp.exp((ridx + 1.0) * lg)
        sdecay = jnp.exp((c - 1.0 - ridx) * lg)
        s = so_ref[0, h]
        scores = _dot_nt(qb, kb) * decay
        y = _dot(scores.astype(BF16), vb)
        y = y + _dot(qb, s.astype(BF16)) * inner
        kd = (kr * sdecay).astype(BF16)
        so_ref[0, h] = math.exp(c * lg) * s + _dot_tn(kd, vb)
        mu = jnp.mean(y, axis=-1, keepdims=True)
        yc = y - mu
        var = jnp.mean(yc * yc, axis=-1, keepdims=True)
        yn = yc * lax.rsqrt(var + EPS) * nw_ref[:, h * RET_DV:(h + 1) * RET_DV]
        g = g_ref[:, h * RET_DV:(h + 1) * RET_DV]
        y_ref[:, h * RET_DV:(h + 1) * RET_DV] = (jax.nn.silu(g) * yn).astype(BF16)


def _retention(proj, cosf, sinf, state, norm_w, *, bsz, seq_len, c):
    m = proj.shape[0]
    nc = seq_len // c
    assert seq_len % c == 0
    qk_w = RET_HEADS * RET_DK
    v_w = RET_HEADS * RET_DV
    row = lambda b, ci: b * nc + ci
    y, s_new = pl.pallas_call(
        functools.partial(_retention_kernel, c=c),
        grid=(bsz, nc),
        in_specs=[pl.BlockSpec((c, qk_w), lambda b, ci: (row(b, ci), 0)),
                  pl.BlockSpec((c, qk_w), lambda b, ci: (row(b, ci), 1)),
                  pl.BlockSpec((c, v_w), lambda b, ci: (row(b, ci), 1)),
                  pl.BlockSpec((c, v_w), lambda b, ci: (row(b, ci), 2)),
                  pl.BlockSpec((c, RET_DK), lambda b, ci: (ci, 0)),
                  pl.BlockSpec((c, RET_DK), lambda b, ci: (ci, 0)),
                  pl.BlockSpec((1, RET_HEADS, RET_DK, RET_DV), lambda b, ci: (b, 0, 0, 0)),
                  pl.BlockSpec((1, v_w), lambda b, ci: (0, 0))],
        out_specs=[pl.BlockSpec((c, v_w), lambda b, ci: (row(b, ci), 0)),
                   pl.BlockSpec((1, RET_HEADS, RET_DK, RET_DV), lambda b, ci: (b, 0, 0, 0))],
        out_shape=[jax.ShapeDtypeStruct((m, v_w), BF16),
                   jax.ShapeDtypeStruct(state.shape, F32)],
        compiler_params=_cparams(2),
        name="retention",
    )(proj, proj, proj, proj, cosf, sinf, state, norm_w.reshape(1, v_w))
    return y, s_new


def _ssd_kernel(z_ref, xs_ref, bc_ref, dt_ref, hx_ref, hbc_ref, st_ref,
                cwx_ref, cwbc_ref, cbx_ref, cbbc_ref, dtb_ref, alog_ref, dsk_ref, nw_ref,
                tri_ref, exp_ref,
                y_ref, so_ref,
                st_scr, cx_scr, cbc_scr, extx_scr, extbc_scr, yh_scr, xs_scr, *, c, nc):
    ci = pl.program_id(1)
    gw = SSD_DINNER // SSD_GROUPS
    hpg = SSD_HEADS // SSD_GROUPS

    @pl.when(ci == 0)
    def _():
        st_scr[...] = st_ref[0].T
        cx_scr[...] = hx_ref[0]
        cbc_scr[...] = hbc_ref[0]

    xs_raw = xs_ref[...]
    bc_raw = bc_ref[...]
    xs_scr[...] = jax.nn.silu(_conv_rows(extx_scr, xs_raw, cx_scr[...], cwx_ref, SSD_CONV, c) + cbx_ref[...])
    bcm = jax.nn.silu(_conv_rows(extbc_scr, bc_raw, cbc_scr[...], cwbc_ref, SSD_CONV, c) + cbbc_ref[...])
    cx_scr[...] = xs_raw[c - SUBLANES:c]
    cbc_scr[...] = bc_raw[c - SUBLANES:c]

    tri = tri_ref[...]
    dt = _softplus(dt_ref[...] + dtb_ref[...])
    a = -jnp.exp(alog_ref[...])
    acs = _exact_lhs_dot(tri, dt * a)
    acs_t = acs.T
    acs_last = acs[c - 1:c, :]
    exp_acs = jnp.exp(acs)
    to_end = jnp.exp(acs_last - acs)
    chunk_dec = jnp.exp(acs_last)

    ii = lax.broadcasted_iota(jnp.int32, (c, c), 0)
    jj = lax.broadcasted_iota(jnp.int32, (c, c), 1)
    causal = ii >= jj
    nb = SSD_GROUPS * SSD_DSTATE
    for g in range(SSD_GROUPS):
        cols = slice(g * gw, (g + 1) * gw)
        expand = exp_ref[:, cols]
        xdt = xs_scr[:, cols] * _exact_rhs_dot(dt, expand)
        xdt_b = xdt.astype(BF16)
        xend_b = (xdt * _exact_rhs_dot(to_end, expand)).astype(BF16)
        b_g = bcm[:, g * SSD_DSTATE:(g + 1) * SSD_DSTATE].astype(BF16)
        c_g = bcm[:, nb + g * SSD_DSTATE:nb + (g + 1) * SSD_DSTATE].astype(BF16)
        cb = _dot_nt(c_g, b_g)
        s_g = st_scr[:, cols]
        y_state = _dot(c_g, s_g.astype(BF16)) * _exact_rhs_dot(exp_acs, expand)
        for r in range(hpg):
            hh = g * hpg + r
            seg = acs[:, hh:hh + 1] - acs_t[hh:hh + 1, :]
            lmat = jnp.exp(jnp.where(causal, seg, NEG_INF))
            mm = (cb * lmat).astype(BF16)
            head = slice(r * SSD_HEADDIM, (r + 1) * SSD_HEADDIM)
            yh_scr[:, hh * SSD_HEADDIM:(hh + 1) * SSD_HEADDIM] = _dot(mm, xdt_b[:, head]) + y_state[:, head]
        upd = _dot_tn(b_g, xend_b)
        st_scr[:, cols] = _exact_rhs_dot(chunk_dec, expand) * s_g + upd

    y = yh_scr[...] + dsk_ref[...] * xs_scr[...]
    z = z_ref[...]
    y_ref[...] = _rms(y * jax.nn.silu(z), nw_ref[...]).astype(BF16)

    @pl.when(ci == nc - 1)
    def _():
        so_ref[0] = st_scr[...].T


def _ssd(proj, dt_proj, hist8, state, conv_w, conv_b, dt_bias, a_log, d_skip, norm_w, *, bsz, seq_len, c):
    m = proj.shape[0]
    nc = seq_len // c
    assert seq_len % c == 0
    row = lambda b, ci: b * nc + ci
    const2 = lambda b, ci: (0, 0)
    di, bcw = SSD_DINNER, 2 * SSD_GROUPS * SSD_DSTATE
    tri = jnp.asarray(np.tril(np.ones((c, c), np.float32)), BF16)
    expand = np.zeros((LANES, di), np.float32)
    for h in range(SSD_HEADS):
        expand[h, h * SSD_HEADDIM:(h + 1) * SSD_HEADDIM] = 1.0
    expand = jnp.asarray(expand, BF16)
    pad_row = lambda v: jnp.pad(v.astype(F32), (0, LANES - v.shape[0])).reshape(1, LANES)
    st2 = state.reshape(bsz, di, SSD_DSTATE)
    y, s_new = pl.pallas_call(
        functools.partial(_ssd_kernel, c=c, nc=nc),
        grid=(bsz, nc),
        in_specs=[pl.BlockSpec((c, di), lambda b, ci: (row(b, ci), 3)),
                  pl.BlockSpec((c, di), lambda b, ci: (row(b, ci), 4)),
                  pl.BlockSpec((c, bcw), lambda b, ci: (row(b, ci), 10)),
                  pl.BlockSpec((c, LANES), lambda b, ci: (row(b, ci), 0)),
                  pl.BlockSpec((1, SUBLANES, di), lambda b, ci: (b, 0, 0)),
                  pl.BlockSpec((1, SUBLANES, bcw), lambda b, ci: (b, 0, 2)),
                  pl.BlockSpec((1, di, SSD_DSTATE), lambda b, ci: (b, 0, 0)),
                  pl.BlockSpec((SSD_CONV, di), const2),
                  pl.BlockSpec((SSD_CONV, bcw), lambda b, ci: (0, 2)),
                  pl.BlockSpec((1, di), const2),
                  pl.BlockSpec((1, bcw), lambda b, ci: (0, 2)),
                  pl.BlockSpec((1, LANES), const2),
                  pl.BlockSpec((1, LANES), const2),
                  pl.BlockSpec((1, di), const2),
                  pl.BlockSpec((1, di), const2),
                  pl.BlockSpec((c, c), const2),
                  pl.BlockSpec((LANES, di), const2)],
        out_specs=[pl.BlockSpec((c, di), lambda b, ci: (row(b, ci), 0)),
                   pl.BlockSpec((1, di, SSD_DSTATE), lambda b, ci: (b, 0, 0))],
        out_shape=[jax.ShapeDtypeStruct((m, di), BF16),
                   jax.ShapeDtypeStruct(st2.shape, F32)],
        scratch_shapes=[pltpu.VMEM((SSD_DSTATE, di), F32),
                        pltpu.VMEM((SUBLANES, di), F32),
                        pltpu.VMEM((SUBLANES, bcw), F32),
                        pltpu.VMEM((c + SUBLANES, di), F32),
                        pltpu.VMEM((c + SUBLANES, bcw), F32),
                        pltpu.VMEM((c, di), F32),
                        pltpu.VMEM((c, di), F32)],
        compiler_params=_cparams(2),
        name="ssd",
    )(proj, proj, proj, dt_proj, hist8, hist8, st2,
      conv_w, conv_w, conv_b.reshape(1, -1), conv_b.reshape(1, -1),
      pad_row(dt_bias), pad_row(a_log), jnp.repeat(d_skip.astype(F32), SSD_HEADDIM).reshape(1, di),
      norm_w.reshape(1, di), tri, expand)
    return y, s_new.reshape(state.shape)


def _decode_cum_kernel(lfc_ref, fl_ref, b_ref, tri_ref, cumt_c_ref, lfn_ref, cumn_ref, cumt_n_ref,
                       carry_ref, *, c, ncb):
    j = pl.program_id(0)

    @pl.when(j == 0)
    def _():
        carry_ref[...] = jnp.zeros_like(carry_ref)

    @pl.when(j < ncb)
    def _():
        cum = _exact_lhs_dot(tri_ref[...], lfc_ref[...]) + carry_ref[...]
        carry_ref[...] = cum[c - 1:c, :]
        cumt_c_ref[...] = cum.T

    @pl.when(j == ncb)
    def _():
        lf = -_softplus(-(fl_ref[...] + b_ref[...]))
        lfn_ref[...] = lf
        cum = _exact_lhs_dot(tri_ref[0:LANES, 0:LANES], lf) + carry_ref[...]
        cumn_ref[...] = cum
        cumt_n_ref[...] = cum.T


def _decode_cum(cache_lf, fl_new, bias, *, c):
    past = cache_lf.shape[0]
    assert past % c == 0 and c % LANES == 0 and fl_new.shape == (LANES, LANES)
    ncb = past // c
    tri = jnp.asarray(np.tril(np.ones((c, c), np.float32)), BF16)
    blk = lambda j: jnp.minimum(j, ncb - 1)
    sq = jax.ShapeDtypeStruct((LANES, LANES), F32)
    return pl.pallas_call(
        functools.partial(_decode_cum_kernel, c=c, ncb=ncb),
        grid=(ncb + 1,),
        in_specs=[pl.BlockSpec((c, LANES), lambda j: (blk(j), 0)),
                  pl.BlockSpec((LANES, LANES), lambda j: (0, 0)),
                  pl.BlockSpec((1, LANES), lambda j: (0, 0)),
                  pl.BlockSpec((c, c), lambda j: (0, 0))],
        out_specs=[pl.BlockSpec((LANES, c), lambda j: (0, blk(j))),
                   pl.BlockSpec((LANES, LANES), lambda j: (0, 0)),
                   pl.BlockSpec((LANES, LANES), lambda j: (0, 0)),
                   pl.BlockSpec((LANES, LANES), lambda j: (0, 0))],
        out_shape=[jax.ShapeDtypeStruct((LANES, past), F32), sq, sq, sq],
        scratch_shapes=[pltpu.VMEM((1, LANES), F32)],
        compiler_params=_cparams(1),
        name="decode_logf_cumsum",
    )(cache_lf, fl_new, bias, tri)


FOX_AUG = 2 * FOX_HEAD_DIM
N_BIAS_PIECES = 3


def _fox_prep_kernel(q_ref, k_ref, v_ref, fl_ref, fb_ref, tri_ref, place_ref, ones_ref,
                     qa_ref, ka_ref, vb_ref, k32_ref, v32_ref, lf_ref, carry_ref, *, tp):
    @pl.when(pl.program_id(1) == 0)
    def _():
        carry_ref[...] = jnp.zeros_like(carry_ref)

    lf = -_softplus(-(fl_ref[...] + fb_ref[...]))
    lf_ref[...] = lf
    cum = _exact_lhs_dot(tri_ref[...], lf) + carry_ref[...]
    carry_ref[...] = cum[tp - 1:tp, :]
    pieces = _split3(cum * (FOX_HEAD_DIM ** 0.5))
    n = N_BIAS_PIECES
    aug_q = ones_ref[0:1, :] + sum(_dot(pieces[r], place_ref[r]) for r in range(n))
    aug_k = ones_ref[1:2, :] - sum(_dot(pieces[r], place_ref[n + r]) for r in range(n))
    for h in range(FOX_HEADS):
        src = slice(h * FOX_HEAD_DIM, (h + 1) * FOX_HEAD_DIM)
        feat = slice(h * FOX_AUG, h * FOX_AUG + FOX_HEAD_DIM)
        bias = slice(h * FOX_AUG + FOX_HEAD_DIM, (h + 1) * FOX_AUG)
        qa_ref[:, feat] = q_ref[:, src].astype(BF16)
        qa_ref[:, bias] = aug_q[:, src].astype(BF16)
        ka_ref[:, feat] = k_ref[:, src].astype(BF16)
        ka_ref[:, bias] = aug_k[:, src].astype(BF16)
    k = k_ref[...]
    v = v_ref[...]
    k32_ref[...] = pltpu.einshape("m(hd)->mhd", k, h=FOX_HEADS)
    v32_ref[...] = pltpu.einshape("m(hd)->mhd", v, h=FOX_HEADS)
    vb_ref[...] = v.astype(BF16)


def _fox_prep(proj, fl_proj, f_bias, *, bsz, seq_len, tp):
    m = proj.shape[0]
    nt = seq_len // tp
    assert seq_len % tp == 0
    w = FOX_WIDTH
    tri = jnp.asarray(np.tril(np.ones((tp, tp), np.float32)), BF16)
    n = N_BIAS_PIECES
    place = np.zeros((2 * n, LANES, w), np.float32)
    ones = np.zeros((SUBLANES, w), np.float32)
    for h in range(FOX_HEADS):
        for r in range(2 * n):
            place[r, h, h * FOX_HEAD_DIM + r] = 1.0
        ones[0, h * FOX_HEAD_DIM + n:h * FOX_HEAD_DIM + 2 * n] = 1.0
        ones[1, h * FOX_HEAD_DIM:h * FOX_HEAD_DIM + n] = 1.0
    row = lambda b, ti: (b * nt + ti, 0)
    const2 = lambda b, ti: (0, 0)
    return pl.pallas_call(
        functools.partial(_fox_prep_kernel, tp=tp),
        grid=(bsz, nt),
        in_specs=[pl.BlockSpec((tp, w), lambda b, ti: (b * nt + ti, 0)),
                  pl.BlockSpec((tp, w), lambda b, ti: (b * nt + ti, 1)),
                  pl.BlockSpec((tp, w), lambda b, ti: (b * nt + ti, 2)),
                  pl.BlockSpec((tp, LANES), row),
                  pl.BlockSpec((1, LANES), const2),
                  pl.BlockSpec((tp, tp), const2),
                  pl.BlockSpec((2 * n, LANES, w), lambda b, ti: (0, 0, 0)),
                  pl.BlockSpec((SUBLANES, w), const2)],
        out_specs=[pl.BlockSpec((tp, FOX_HEADS * FOX_AUG), row),
                   pl.BlockSpec((tp, FOX_HEADS * FOX_AUG), row),
                   pl.BlockSpec((tp, w), row),
                   pl.BlockSpec((tp, FOX_HEADS, FOX_HEAD_DIM), lambda b, ti: (b * nt + ti, 0, 0)),
                   pl.BlockSpec((tp, FOX_HEADS, FOX_HEAD_DIM), lambda b, ti: (b * nt + ti, 0, 0)),
                   pl.BlockSpec((tp, LANES), row)],
        out_shape=[jax.ShapeDtypeStruct((m, FOX_HEADS * FOX_AUG), BF16),
                   jax.ShapeDtypeStruct((m, FOX_HEADS * FOX_AUG), BF16),
                   jax.ShapeDtypeStruct((m, w), BF16),
                   jax.ShapeDtypeStruct((m, FOX_HEADS, FOX_HEAD_DIM), F32),
                   jax.ShapeDtypeStruct((m, FOX_HEADS, FOX_HEAD_DIM), F32),
                   jax.ShapeDtypeStruct((m, LANES), F32)],
        scratch_shapes=[pltpu.VMEM((1, LANES), F32)],
        compiler_params=_cparams(2),
        name="fox_prep",
    )(proj, proj, proj, fl_proj, f_bias, tri, jnp.asarray(place, BF16), jnp.asarray(ones, F32))


def _fox_kernel(qi_ref, ki_ref, q_ref, k_ref, v_ref, o_ref, m_ref, acc_ref, va_ref, *, t, ts, ts_diag):
    step = pl.program_id(2)
    qi = qi_ref[step]
    ki = ki_ref[step]
    to_log2 = (FOX_HEAD_DIM ** -0.5) * math.log2(math.e)
    hd = FOX_HEAD_DIM

    @pl.when(ki == 0)
    def _():
        m_ref[...] = jnp.full_like(m_ref, NEG_INF)
        acc_ref[...] = jnp.zeros_like(acc_ref)
        va_ref[:, hd:2 * hd] = jnp.ones((t, hd), BF16)

    va_ref[:, 0:hd] = v_ref[...]

    def update(diagonal):
        tc = ts_diag if diagonal else ts
        for r in range(t // tc):
            rows = slice(r * tc, (r + 1) * tc)
            nk = (r + 1) * tc if diagonal else t
            s = _dot_nt(q_ref[rows, :], k_ref[0:nk, :])
            if diagonal:
                ri = lax.broadcasted_iota(jnp.int32, (tc, nk), 0) + r * tc
                ci = lax.broadcasted_iota(jnp.int32, (tc, nk), 1)
                s = jnp.where(ci <= ri, s, NEG_INF)
            m_old = m_ref[rows, :]
            m_new = jnp.maximum(m_old, jnp.max(s, axis=1, keepdims=True))
            m_ref[rows, :] = m_new
            alpha = jnp.exp2((m_old - m_new) * to_log2)
            p = jnp.exp2((s - _widen(m_new, nk)) * to_log2)
            acc_ref[rows, :] = _widen(alpha, 2 * hd) * acc_ref[rows, :] + _dot(p.astype(BF16), va_ref[0:nk, :])

    @pl.when(ki < qi)
    def _():
        update(False)

    @pl.when(ki == qi)
    def _():
        update(True)
        o_ref[...] = (acc_ref[:, 0:hd] / acc_ref[:, hd:2 * hd]).astype(BF16)


def _fox_prompt(qa, ka, vb, *, bsz, seq_len, t, ts, ts_diag):
    m = qa.shape[0]
    nq = seq_len // t
    assert seq_len % t == 0 and t % ts == 0 and t % ts_diag == 0
    pairs = [(qi, ki) for qi in range(nq) for ki in range(qi + 1)]
    qi_tab = jnp.asarray([p[0] for p in pairs], jnp.int32)
    ki_tab = jnp.asarray([p[1] for p in pairs], jnp.int32)
    grid_spec = pltpu.PrefetchScalarGridSpec(
        num_scalar_prefetch=2,
        grid=(bsz, FOX_HEADS, len(pairs)),
        in_specs=[pl.BlockSpec((t, FOX_AUG), lambda b, h, s, qi, ki: (b * nq + qi[s], h)),
                  pl.BlockSpec((t, FOX_AUG), lambda b, h, s, qi, ki: (b * nq + ki[s], h)),
                  pl.BlockSpec((t, FOX_HEAD_DIM), lambda b, h, s, qi, ki: (b * nq + ki[s], h))],
        out_specs=pl.BlockSpec((t, FOX_HEAD_DIM), lambda b, h, s, qi, ki: (b * nq + qi[s], h)),
        scratch_shapes=[pltpu.VMEM((t, LANES), F32), pltpu.VMEM((t, 2 * FOX_HEAD_DIM), F32),
                        pltpu.VMEM((t, 2 * FOX_HEAD_DIM), BF16)],
    )
    return pl.pallas_call(
        functools.partial(_fox_kernel, t=t, ts=ts, ts_diag=ts_diag),
        grid_spec=grid_spec,
        out_shape=jax.ShapeDtypeStruct((m, FOX_WIDTH), BF16),
        compiler_params=_cparams(3),
        name="fox_attention",
    )(qi_tab, ki_tab, qa, ka, vb)


def _fox_decode_kernel(q_ref, kn_ref, vn_ref, kc_ref, vc_ref, cq_ref, ckc_ref, ckn_ref, o_ref,
                       m_ref, l_ref, acc_ref, *, lq, ncb):
    j = pl.program_id(1)
    nh, hd = FOX_HEADS, FOX_HEAD_DIM

    @pl.when(j == 0)
    def _():
        m_ref[...] = jnp.full_like(m_ref, NEG_INF)
        l_ref[...] = jnp.zeros_like(l_ref)
        acc_ref[...] = jnp.zeros_like(acc_ref)

    def attend(k_head, v_head, ck_head, causal):
        for h in range(nh):
            qh = q_ref[:, h * hd:(h + 1) * hd].astype(BF16)
            ck = ck_head(h)
            tk = ck.shape[1]
            s = _dot_nt(qh, k_head(h).astype(BF16)) * (hd ** -0.5)
            s = s + (_widen(cq_ref[h], tk) - ck)
            if causal:
                rows = lax.broadcasted_iota(jnp.int32, (lq, tk), 0)
                cols = lax.broadcasted_iota(jnp.int32, (lq, tk), 1)
                s = jnp.where(cols <= rows, s, NEG_INF)
            m_old = m_ref[h]
            m_new = jnp.maximum(m_old, jnp.max(s, axis=1, keepdims=True))
            alpha = jnp.exp(m_old - m_new)
            p = jnp.exp(s - _widen(m_new, tk))
            l_ref[h] = alpha * l_ref[h] + jnp.sum(p, axis=1, keepdims=True)
            acc_ref[h] = alpha * acc_ref[h] + _dot(p.astype(BF16), v_head(h).astype(BF16))
            m_ref[h] = m_new

    @pl.when(j < ncb)
    def _():
        k_hm = pltpu.einshape("mhd->hmd", kc_ref[0])
        v_hm = pltpu.einshape("mhd->hmd", vc_ref[0])
        attend(lambda h: k_hm[h], lambda h: v_hm[h], lambda h: ckc_ref[h], False)

    @pl.when(j == ncb)
    def _():
        attend(lambda h: kn_ref[:, h * hd:(h + 1) * hd], lambda h: vn_ref[:, h * hd:(h + 1) * hd],
               lambda h: ckn_ref[h][:, 0:lq], True)
        for h in range(nh):
            o_ref[:, h * hd:(h + 1) * hd] = (acc_ref[h] / l_ref[h]).astype(BF16)


def _fox_decode(proj, cache_k, cache_v, cq, ck_cache, ck_new, *, bsz, lq, tk):
    past = cache_k.shape[1]
    assert past % tk == 0 and tk % LANES == 0 and lq <= LANES
    ncb = past // tk
    nh, hd, w = FOX_HEADS, FOX_HEAD_DIM, FOX_WIDTH
    tile = lambda j: jnp.minimum(j, ncb - 1)
    cache_spec = pl.BlockSpec((1, tk, nh, hd), lambda b, j: (b, tile(j), 0, 0))
    return pl.pallas_call(
        functools.partial(_fox_decode_kernel, lq=lq, ncb=ncb),
        grid=(bsz, ncb + 1),
        in_specs=[pl.BlockSpec((lq, w), lambda b, j: (b, 0)),
                  pl.BlockSpec((lq, w), lambda b, j: (b, 1)),
                  pl.BlockSpec((lq, w), lambda b, j: (b, 2)),
                  cache_spec,
                  cache_spec,
                  pl.BlockSpec((nh, lq, LANES), lambda b, j: (b, 0, 0)),
                  pl.BlockSpec((nh, 1, tk), lambda b, j: (b, 0, tile(j))),
                  pl.BlockSpec((nh, 1, LANES), lambda b, j: (b, 0, 0))],
        out_specs=pl.BlockSpec((lq, w), lambda b, j: (b, 0)),
        out_shape=jax.ShapeDtypeStruct((bsz * lq, w), BF16),
        scratch_shapes=[pltpu.VMEM((nh, lq, LANES), F32), pltpu.VMEM((nh, lq, LANES), F32),
                        pltpu.VMEM((nh, lq, hd), F32)],
        compiler_params=_cparams(2),
        name="fox_decode",
    )(proj, proj, proj, cache_k, cache_v, cq, ck_cache, ck_new)


def _sconv_kernel(u_ref, bg_ref, cg_ref, cw_ref, hist_ref, y_ref, tail_ref, carry_ref, ext_ref,
                  *, tm, rows, spt, tpb):
    i = pl.program_id(0)
    w = cg_ref[...] * u_ref[...]
    first = (i % tpb) == 0
    for s in range(spt):
        w_s = w[s * rows:(s + 1) * rows]
        if tpb == 1:
            prev = hist_ref[s]
        else:
            prev = jnp.where(first, hist_ref[s], carry_ref[...])
        conv = _conv_rows(ext_ref, w_s, prev, cw_ref, SC_WIDTH, rows)
        y_ref[s * rows:(s + 1) * rows, :] = (bg_ref[s * rows:(s + 1) * rows, :] * conv).astype(BF16)
        tail_ref[s] = w_s[rows - SUBLANES:rows]
    if tpb > 1:
        carry_ref[...] = w[tm - SUBLANES:tm]


def _sconv(proj, conv_w, hist8, *, seq_len, tm):
    m = proj.shape[0]
    assert m % tm == 0
    rows, spt, tpb = _seq_tiling(seq_len, tm)
    nm = m // tm
    hist_map = (lambda i: (i // tpb, 0, 0)) if spt == 1 else (lambda i: (i, 0, 0))
    base = 3 * FOX_WIDTH // SC_DIM
    return pl.pallas_call(
        functools.partial(_sconv_kernel, tm=tm, rows=rows, spt=spt, tpb=tpb),
        grid=(nm,),
        in_specs=[pl.BlockSpec((tm, SC_DIM), lambda i: (i, base)),
                  pl.BlockSpec((tm, SC_DIM), lambda i: (i, base + 1)),
                  pl.BlockSpec((tm, SC_DIM), lambda i: (i, base + 2)),
                  pl.BlockSpec((SC_WIDTH, SC_DIM), lambda i: (0, 0)),
                  pl.BlockSpec((spt, SUBLANES, SC_DIM), hist_map)],
        out_specs=[pl.BlockSpec((tm, SC_DIM), lambda i: (i, 0)),
                   pl.BlockSpec((spt, SUBLANES, SC_DIM), lambda i: (i, 0, 0))],
        out_shape=[jax.ShapeDtypeStruct((m, SC_DIM), BF16),
                   jax.ShapeDtypeStruct((nm * spt, SUBLANES, SC_DIM), F32)],
        scratch_shapes=[pltpu.VMEM((SUBLANES, SC_DIM), F32),
                        pltpu.VMEM((rows + SUBLANES, SC_DIM), F32)],
        compiler_params=_cparams(1),
        name="gated_short_conv",
    )(proj, proj, proj, conv_w, hist8)


def _hist8(state):
    n, w1, c = state.shape
    return jnp.concatenate([jnp.zeros((n, SUBLANES - w1, c), F32), state.astype(F32)], axis=1)


def _tails(tails, n_seq, seq_len, tile_rows, keep):
    per_seq = max(1, seq_len // tile_rows)
    pieces = tails.reshape(n_seq, per_seq, SUBLANES, tails.shape[-1])
    return pieces[:, per_seq - 1, SUBLANES - keep:, :]


def _rope_tables(pos0, length):
    half = RET_DK // 2
    inv = ROPE_BASE ** (-np.arange(half, dtype=np.float64) / half)
    ang = (pos0 + np.arange(length, dtype=np.float64))[:, None] * inv[None, :]
    cos, sin = np.cos(ang), np.sin(ang)
    return (jnp.asarray(np.concatenate([cos, cos], axis=1), F32),
            jnp.asarray(np.concatenate([-sin, sin], axis=1), F32))


def _prep_weights(p):
    d = D_MODEL
    ab_in = p['ab_w_in'][0]
    cd_in = p['cd_w_in'][0]
    f0 = 3 * FOX_WIDTH
    pad_cols = lambda w: jnp.pad(w, ((0, 0), (0, LANES - w.shape[1]))).astype(BF16)
    return dict(
        ab_in=ab_in.astype(BF16),
        ab_small=pad_cols(ab_in[:, AB_MAIN:]),
        cd_in=jnp.concatenate([cd_in[:, :f0], cd_in[:, f0 + FOX_HEADS:]], axis=1).astype(BF16),
        cd_small=pad_cols(cd_in[:, f0:f0 + FOX_HEADS]),
        ab_out=p['ab_w_out'][0].astype(BF16),
        cd_out=p['cd_w_out'][0].astype(BF16),
        ffn0=tuple(p[name][0:1].astype(BF16) for name in ('ffn_w_gate', 'ffn_w_up', 'ffn_w_down')),
        ffn1=None,
    )


def _trunk(x, pos0, st_ret, st_ssd, st_ssd_conv, c_k, c_v, c_logf, st_sconv, st_ffn, p, wb, t):
    bsz, length, d = x.shape
    m = bsz * length
    xf = x.reshape(m, d)
    zeros = lambda *shape: jnp.zeros(shape, F32)

    proj, dt_proj = _norm_matmul(xf, p['ab_norm_w'][0], wb['ab_in'], wb['ab_small'],
                                 n=AB_MAIN, tm=t['tm_ab'], tn=t['tn_ab'])
    cosf, sinf = _rope_tables(pos0, length)
    ret_state = zeros(bsz, RET_HEADS, RET_DK, RET_DV) if st_ret is None else st_ret
    y_ret, ret_new = _retention(proj, cosf, sinf, ret_state, p['ret_norm_w'][0],
                                bsz=bsz, seq_len=length, c=t['c_ret'])
    ssd_state = zeros(bsz, SSD_HEADS, SSD_HEADDIM, SSD_DSTATE) if st_ssd is None else st_ssd
    ssd_hist = zeros(bsz, SSD_CONV - 1, SSD_CONV_DIM) if st_ssd_conv is None else st_ssd_conv
    y_ssd, ssd_new = _ssd(proj, dt_proj, _hist8(ssd_hist), ssd_state, p['ssd_conv_w'][0], p['ssd_conv_b'][0],
                          p['ssd_dt_bias'][0], p['ssd_A_log'][0], p['ssd_D'][0], p['ssd_norm_w'][0],
                          bsz=bsz, seq_len=length, c=t['c_ssd'])
    xbc_lo = AB_MAIN - SSD_CONV_DIM
    ssd_conv_new = proj.reshape(bsz, length, -1)[:, length - (SSD_CONV - 1):, xbc_lo:AB_MAIN]
    xf = _proj_residual(xf, y_ret, y_ssd, wb['ab_out'], tm=t['tm_out'], tn=t['tn_out'])

    ffn_new = []
    ffn_hist0 = zeros(bsz, FFN_CONV - 1, D_FF) if st_ffn is None else st_ffn[0]
    next_f32 = (1, p['ffn_w_gate'], p['ffn_w_up'], p['ffn_w_down'])
    n_row_tiles = m // t['tm_ffn']
    if wb['ffn1'] is None and not (d % (n_row_tiles * LANES) == 0 and D_FF % t['tf_ffn'] == 0):
        wb['ffn1'] = tuple(w[1:2].astype(BF16) for w in next_f32[1:])
    xf, tails, next_bf16 = _conv_ffn(
        xf, p['ffn_norm_w'][0], *wb['ffn0'], p['ffn_conv_w'][0], p['ffn_conv_b'][0], _hist8(ffn_hist0),
        p['final_norm_w'], layer=0, seq_len=length, tm=t['tm_ffn'], tf=t['tf_ffn'], ts=t['ts_ffn'], final=False,
        cast_next=next_f32 if wb['ffn1'] is None else None)
    if wb['ffn1'] is None:
        wb['ffn1'] = tuple(w[None] for w in next_bf16)
    ffn_new.append(_tails(tails, bsz, length, t['tm_ffn'], FFN_CONV - 1))

    proj, fl_proj = _norm_matmul(xf, p['cd_norm_w'][0], wb['cd_in'], wb['cd_small'],
                                 n=CD_MAIN, tm=t['tm_cd'], tn=t['tn_cd'])
    f_bias = jnp.pad(p['fox_f_bias'][0].astype(F32), (0, LANES - FOX_HEADS)).reshape(1, LANES)
    head_shape = (bsz, length, FOX_HEADS, FOX_HEAD_DIM)
    if c_k is None:
        qa, ka, vb, k32, v32, logf = _fox_prep(proj, fl_proj, f_bias, bsz=bsz, seq_len=length, tp=t['t_prep'])
        y_fox = _fox_prompt(qa, ka, vb, bsz=bsz, seq_len=length, t=t['t_fox'], ts=t['ts_fox'],
                            ts_diag=t['ts_fox_diag'])
        logf_new = logf.reshape(bsz, length, LANES)[:, :, :FOX_HEADS]
        k_new, v_new = k32.reshape(head_shape), v32.reshape(head_shape)
    else:
        proj3 = proj.reshape(bsz, length, -1)
        k_new = proj3[:, :, FOX_WIDTH:2 * FOX_WIDTH].reshape(head_shape)
        v_new = proj3[:, :, 2 * FOX_WIDTH:3 * FOX_WIDTH].reshape(head_shape)
        past = c_k.shape[1]
        pairs = bsz * FOX_HEADS
        assert pairs <= LANES and length <= LANES
        to_lanes = lambda a, rows: jnp.pad(jnp.swapaxes(a, 0, 1).reshape(a.shape[1], pairs),
                                           ((0, rows - a.shape[1]), (0, LANES - pairs)))
        from_lanes = lambda a: jnp.swapaxes(a[:length, :pairs].reshape(length, bsz, FOX_HEADS), 0, 1)
        cache_lf = to_lanes(c_logf.astype(F32), past)
        fl_rows = to_lanes(fl_proj.reshape(bsz, length, LANES)[:, :, :FOX_HEADS], LANES)
        bias_lanes = jnp.pad(jnp.tile(p['fox_f_bias'][0].astype(F32), bsz), (0, LANES - pairs)).reshape(1, LANES)
        cum_t_cache, lf_rows, cum_rows, cum_t_new = _decode_cum(cache_lf, fl_rows, bias_lanes, c=t['c_cum'])
        logf_new = from_lanes(lf_rows)
        cq = jnp.broadcast_to(cum_rows[:length, :pairs].T[:, :, None], (pairs, length, LANES))
        y_fox = _fox_decode(proj, c_k, c_v, cq, cum_t_cache[:pairs, None, :], cum_t_new[:pairs, None, :],
                            bsz=bsz, lq=length, tk=t['tk_dec'])
    sc_hist = zeros(bsz, SC_WIDTH - 1, SC_DIM) if st_sconv is None else st_sconv
    y_sc, sc_tails = _sconv(proj, p['sconv_w'][0], _hist8(sc_hist), seq_len=length, tm=t['tm_sc'])
    sconv_new = _tails(sc_tails, bsz, length, t['tm_sc'], SC_WIDTH - 1)
    xf = _proj_residual(xf, y_fox, y_sc, wb['cd_out'], tm=t['tm_out'], tn=t['tn_out'])

    ffn_hist1 = zeros(bsz, FFN_CONV - 1, D_FF) if st_ffn is None else st_ffn[1]
    xf, tails, _ = _conv_ffn(xf, p['ffn_norm_w'][1], *wb['ffn1'], p['ffn_conv_w'][1], p['ffn_conv_b'][1],
                             _hist8(ffn_hist1), p['final_norm_w'], layer=0, seq_len=length, tm=t['tm_ffn'],
                             tf=t['tf_ffn'], ts=t['ts_ffn'], final=True)
    ffn_new.append(_tails(tails, bsz, length, t['tm_ffn'], FFN_CONV - 1))

    return (xf.reshape(bsz, length, d), ret_new[None], ssd_new[None], ssd_conv_new[None], k_new[None],
            v_new[None], logf_new[None], sconv_new[None], jnp.stack(ffn_new))


def _largest_divisor(n, cap, multiple=1):
    best = None
    for cand in range(multiple, min(n, cap) + 1, multiple):
        if n % cand == 0:
            best = cand
    assert best is not None, (n, cap, multiple)
    return best


def _tiles(bsz, length, past=None):
    m = bsz * length
    seq_tile = lambda cap: _largest_divisor(length, cap, SUBLANES)
    row_tile = lambda cap: (_largest_divisor(length, cap, SUBLANES) if length >= cap
                            else _largest_divisor(m, cap, length))
    t = dict(
        tm_ab=row_tile(1024), tn_ab=2048, tm_cd=row_tile(1024), tn_cd=2048,
        tm_out=row_tile(1024), tn_out=2048,
        tm_ffn=row_tile(1024), tf_ffn=512, ts_ffn=512,
        tm_sc=row_tile(1024),
        c_ret=seq_tile(256), c_ssd=seq_tile(256),
    )
    if past is None:
        t['t_fox'] = seq_tile(2048)
        t['ts_fox'] = _largest_divisor(t['t_fox'], 256, LANES)
        t['ts_fox_diag'] = _largest_divisor(t['t_fox'], 512, LANES)
        t['t_prep'] = seq_tile(512)
    else:
        t['tk_dec'] = _largest_divisor(past, 1024, LANES)
        t['c_cum'] = _largest_divisor(past, 256, LANES)
    return t


def kernel(x_prompt, x_sample, state_ret, state_ssd, state_ssd_conv, cache_fox_k, cache_fox_v, cache_fox_logf, state_sconv, state_ffn_conv, ab_norm_w, ab_w_in, ret_norm_w, ssd_conv_w, ssd_conv_b, ssd_dt_bias, ssd_A_log, ssd_D, ssd_norm_w, ab_w_out, cd_norm_w, cd_w_in, fox_f_bias, sconv_w, cd_w_out, ffn_norm_w, ffn_w_gate, ffn_w_up, ffn_conv_w, ffn_conv_b, ffn_w_down, final_norm_w):
    p = dict(ab_norm_w=ab_norm_w, ab_w_in=ab_w_in, ret_norm_w=ret_norm_w, ssd_conv_w=ssd_conv_w,
             ssd_conv_b=ssd_conv_b, ssd_dt_bias=ssd_dt_bias, ssd_A_log=ssd_A_log, ssd_D=ssd_D,
             ssd_norm_w=ssd_norm_w, ab_w_out=ab_w_out, cd_norm_w=cd_norm_w, cd_w_in=cd_w_in,
             fox_f_bias=fox_f_bias, sconv_w=sconv_w, cd_w_out=cd_w_out, ffn_norm_w=ffn_norm_w,
             ffn_w_gate=ffn_w_gate, ffn_w_up=ffn_w_up, ffn_conv_w=ffn_conv_w, ffn_conv_b=ffn_conv_b,
             ffn_w_down=ffn_w_down, final_norm_w=final_norm_w)
    assert x_prompt.shape[-1] == D_MODEL and ab_w_in.shape == (1, D_MODEL, AB_MAIN + SSD_HEADS)
    assert cd_w_in.shape == (1, D_MODEL, CD_MAIN + FOX_HEADS) and ffn_w_gate.shape == (2, D_MODEL, D_FF)
    wb = _prep_weights(p)
    bp, lp_, _ = x_prompt.shape
    bs, ls, _ = x_sample.shape
    past = cache_fox_k.shape[2]
    (y_prompt, p_ret, p_ssd, p_ssd_conv, p_fox_k, p_fox_v, p_fox_logf, p_sconv, p_ffn_conv) = _trunk(
        x_prompt, 0, None, None, None, None, None, None, None, None, p, wb, _tiles(bp, lp_))
    (y_sample, s_ret, s_ssd, s_ssd_conv, s_fox_k, s_fox_v, s_fox_logf, s_sconv, s_ffn_conv) = _trunk(
        x_sample, past, state_ret[0], state_ssd[0], state_ssd_conv[0], cache_fox_k[0], cache_fox_v[0],
        cache_fox_logf[0], state_sconv[0], state_ffn_conv, p, wb, _tiles(bs, ls, past))
    return (y_prompt, y_sample, p_ret, s_ret, p_ssd, s_ssd, p_ssd_conv, s_ssd_conv, p_fox_k, s_fox_k,
            p_fox_v, s_fox_v, p_fox_logf, s_fox_logf, p_sconv, s_sconv, p_ffn_conv, s_ffn_conv)
```

```python
import functools
import math

import numpy as np
import jax
import jax.numpy as jnp
from jax import lax
from jax.experimental import pallas as pl
from jax.experimental.pallas import tpu as pltpu

F32 = jnp.float32
BF16 = jnp.bfloat16
EPS = 1e-6
ROPE_BASE = 10000.0
NEG_INF = float("-inf")

D_MODEL = 2048
RET_HEADS, RET_DK, RET_DV = 4, 128, 256
SSD_DINNER, SSD_HEADDIM, SSD_HEADS, SSD_GROUPS, SSD_DSTATE, SSD_CONV = 1024, 64, 16, 2, 128, 4
SSD_CONV_DIM = SSD_DINNER + 2 * SSD_GROUPS * SSD_DSTATE
FOX_HEADS, FOX_HEAD_DIM = 8, 128
FOX_WIDTH = FOX_HEADS * FOX_HEAD_DIM
SC_DIM, SC_WIDTH = 1024, 3
D_FF, FFN_CONV = 5632, 3
AB_MAIN = 2 * RET_HEADS * RET_DK + 2 * RET_HEADS * RET_DV + SSD_DINNER + SSD_CONV_DIM
AB_PAD = AB_MAIN + 128
CD_MAIN = 3 * FOX_WIDTH + 3 * SC_DIM
CD_PAD = CD_MAIN + 128

LANES = 128
SUBLANES = 8
VMEM_LIMIT = 60 * 1024 * 1024


def _cparams(n_axes):
    return pltpu.CompilerParams(dimension_semantics=("arbitrary",) * n_axes,
                                vmem_limit_bytes=VMEM_LIMIT)


def _rms(xf, w):
    return xf * lax.rsqrt(jnp.mean(xf * xf, axis=-1, keepdims=True) + EPS) * w


def _softplus(x):
    return jnp.maximum(x, 0.0) + jnp.log1p(jnp.exp(-jnp.abs(x)))


def _split3(x):
    hi = x.astype(BF16)
    r1 = x - hi.astype(F32)
    mid = r1.astype(BF16)
    lo = (r1 - mid.astype(F32)).astype(BF16)
    return hi, mid, lo


def _widen(x, n):
    return x[:, 0:n] if n <= LANES else jnp.concatenate([x] * (n // LANES), axis=1)


def _dot(a, b):
    return jnp.dot(a, b, preferred_element_type=F32)


def _dot_nt(a, b):
    return lax.dot_general(a, b, (((1,), (1,)), ((), ())), preferred_element_type=F32)


def _dot_tn(a, b):
    return lax.dot_general(a, b, (((0,), (0,)), ((), ())), preferred_element_type=F32)


def _exact_lhs_dot(m_bf16, x):
    hi, mid, lo = _split3(x)
    return _dot(m_bf16, hi) + _dot(m_bf16, mid) + _dot(m_bf16, lo)


def _exact_rhs_dot(x, m_bf16):
    hi, mid, lo = _split3(x)
    return _dot(hi, m_bf16) + _dot(mid, m_bf16) + _dot(lo, m_bf16)


def _conv_rows(ext_ref, x, prev8, w_ref, width, rows, w_cols=slice(None)):
    ext_ref[0:SUBLANES, :] = prev8
    ext_ref[SUBLANES:SUBLANES + rows, :] = x
    out = None
    for j in range(width):
        off = SUBLANES - (width - 1) + j
        term = ext_ref[off:off + rows, :] * w_ref[j:j + 1, w_cols]
        out = term if out is None else out + term
    return out


def _seq_tiling(seq_len, tile_rows):
    if seq_len >= tile_rows:
        assert seq_len % tile_rows == 0
        return tile_rows, 1, seq_len // tile_rows
    assert tile_rows % seq_len == 0 and seq_len % SUBLANES == 0
    return seq_len, tile_rows // seq_len, 1


def _norm_matmul_kernel(x_ref, nw_ref, w_ref, ws_ref, o_ref, os_ref, h_ref, *, n_tiles, tn, last_w):
    j = pl.program_id(1)

    @pl.when(j == 0)
    def _():
        h_ref[...] = _rms(x_ref[...], nw_ref[...]).astype(BF16)
        os_ref[...] = _dot(h_ref[...], ws_ref[...])

    if last_w == tn:
        o_ref[...] = _dot(h_ref[...], w_ref[...])
    else:
        @pl.when(j < n_tiles - 1)
        def _():
            o_ref[...] = _dot(h_ref[...], w_ref[...])

        @pl.when(j == n_tiles - 1)
        def _():
            o_ref[:, 0:last_w] = _dot(h_ref[...], w_ref[:, 0:last_w])


def _norm_matmul(x, norm_w, w, w_small, *, n, tm, tn):
    m, d = x.shape
    n_tiles = pl.cdiv(n, tn)
    last_w = n - (n_tiles - 1) * tn
    assert m % tm == 0 and last_w % LANES == 0 and n <= w.shape[1] and w_small.shape == (d, LANES)
    return pl.pallas_call(
        functools.partial(_norm_matmul_kernel, n_tiles=n_tiles, tn=tn, last_w=last_w),
        grid=(m // tm, n_tiles),
        in_specs=[pl.BlockSpec((tm, d), lambda i, j: (i, 0)),
                  pl.BlockSpec((1, d), lambda i, j: (0, 0)),
                  pl.BlockSpec((d, tn), lambda i, j: (0, j)),
                  pl.BlockSpec((d, LANES), lambda i, j: (0, 0))],
        out_specs=[pl.BlockSpec((tm, tn), lambda i, j: (i, j)),
                   pl.BlockSpec((tm, LANES), lambda i, j: (i, 0))],
        out_shape=[jax.ShapeDtypeStruct((m, n), F32),
                   jax.ShapeDtypeStruct((m, LANES), F32)],
        scratch_shapes=[pltpu.VMEM((tm, d), BF16)],
        compiler_params=_cparams(2),
        name="norm_in_proj",
    )(x, norm_w.reshape(1, d), w, w_small)


def _proj_res_kernel(x_ref, a_ref, b_ref, wa_ref, wb_ref, o_ref):
    acc = _dot(a_ref[...], wa_ref[...])
    acc = acc + _dot(b_ref[...], wb_ref[...])
    o_ref[...] = x_ref[...] + acc


def _proj_residual(x, a, b, w, *, tm, tn):
    m, d = x.shape
    ka, kb = a.shape[1], b.shape[1]
    assert m % tm == 0 and d % tn == 0 and ka == kb and w.shape == (ka + kb, d)
    return pl.pallas_call(
        _proj_res_kernel,
        grid=(m // tm, d // tn),
        in_specs=[pl.BlockSpec((tm, tn), lambda i, j: (i, j)),
                  pl.BlockSpec((tm, ka), lambda i, j: (i, 0)),
                  pl.BlockSpec((tm, kb), lambda i, j: (i, 0)),
                  pl.BlockSpec((ka, tn), lambda i, j: (0, j)),
                  pl.BlockSpec((kb, tn), lambda i, j: (1, j))],
        out_specs=pl.BlockSpec((tm, tn), lambda i, j: (i, j)),
        out_shape=jax.ShapeDtypeStruct((m, d), F32),
        compiler_params=_cparams(2),
        name="out_proj_residual",
    )(x, a, b, w, w)


def _ffn_kernel(*refs, tm, rows, spt, tpb, nf, ts, nsub, nsub_last, final, n_cast):
    x_ref, nw_ref, wg_ref, wu_ref, wd_ref, cw_ref, cb_ref, hist_ref, fw_ref = refs[:9]
    cast_in = refs[9:9 + n_cast]
    o_ref, tail_ref = refs[9 + n_cast:11 + n_cast]
    cast_out = refs[11 + n_cast:11 + 2 * n_cast]
    h_ref, carry_ref, ext_ref = refs[11 + 2 * n_cast:]
    i = pl.program_id(0)
    f = pl.program_id(1)

    for src_ref, dst_ref in zip(cast_in, cast_out):
        dst_ref[...] = src_ref[...].astype(BF16)

    @pl.when(f == 0)
    def _():
        xf = x_ref[...]
        h_ref[...] = _rms(xf, nw_ref[...]).astype(BF16)
        o_ref[...] = xf

    first = (i % tpb) == 0

    def sub_block(sb):
        cols = slice(sb * ts, (sb + 1) * ts)
        h = h_ref[...]
        a = _dot(h, wg_ref[:, cols])
        u = _dot(h, wu_ref[:, cols])
        convs = []
        for s in range(spt):
            a_s = a[s * rows:(s + 1) * rows]
            if tpb == 1:
                prev = hist_ref[s, :, cols]
            else:
                prev = jnp.where(first, hist_ref[s, :, cols], carry_ref[f * nsub + sb])
            convs.append(_conv_rows(ext_ref, a_s, prev, cw_ref, FFN_CONV, rows, cols))
            tail_ref[s, :, cols] = a_s[rows - SUBLANES:rows]
        if tpb > 1:
            carry_ref[f * nsub + sb] = a[tm - SUBLANES:tm]
        conv = convs[0] if spt == 1 else jnp.concatenate(convs, axis=0)
        act = (jax.nn.silu(conv + cb_ref[:, cols]) * u).astype(BF16)
        o_ref[...] += _dot(act, wd_ref[cols, :])

    if nsub_last == nsub:
        for sb in range(nsub):
            sub_block(sb)
    else:
        @pl.when(f < nf - 1)
        def _():
            for sb in range(nsub):
                sub_block(sb)

        @pl.when(f == nf - 1)
        def _():
            for sb in range(nsub_last):
                sub_block(sb)

    if final:
        @pl.when(f == nf - 1)
        def _():
            o_ref[...] = _rms(o_ref[...], fw_ref[...])


def _conv_ffn(x, norm_w, wg, wu, wd, conv_w, conv_b, hist8, final_w, *, layer, seq_len, tm, tf, ts, final,
              cast_next=None):
    m, d = x.shape
    ff = wg.shape[2]
    assert m % tm == 0 and tf % ts == 0 and ff % ts == 0
    rows, spt, tpb = _seq_tiling(seq_len, tm)
    nm, nf = m // tm, pl.cdiv(ff, tf)
    nsub = tf // ts
    nsub_last = (ff - (nf - 1) * tf) // ts
    hist_map = (lambda i, f: (i // tpb, 0, f)) if spt == 1 else (lambda i, f: (i, 0, f))
    cast_in, cast_specs, cast_src_specs = [], [], []
    if cast_next is not None:
        assert d % nm == 0 and (d // nm) % LANES == 0 and ff % nf == 0 and (ff // nf) % LANES == 0
        dr, fc = d // nm, ff // nf
        cast_layer, cast_in = cast_next[0], list(cast_next[1:])
        cast_specs = [pl.BlockSpec((dr, fc), lambda i, f: (i, f)),
                      pl.BlockSpec((dr, fc), lambda i, f: (i, f)),
                      pl.BlockSpec((fc, dr), lambda i, f: (f, i))]
        cast_src_specs = [pl.BlockSpec((None, dr, fc), lambda i, f: (cast_layer, i, f)),
                          pl.BlockSpec((None, dr, fc), lambda i, f: (cast_layer, i, f)),
                          pl.BlockSpec((None, fc, dr), lambda i, f: (cast_layer, f, i))]
    kern = functools.partial(_ffn_kernel, tm=tm, rows=rows, spt=spt, tpb=tpb, nf=nf, ts=ts, nsub=nsub,
                             nsub_last=nsub_last, final=final, n_cast=len(cast_in))
    results = pl.pallas_call(
        kern,
        grid=(nm, nf),
        in_specs=[pl.BlockSpec((tm, d), lambda i, f: (i, 0)),
                  pl.BlockSpec((1, d), lambda i, f: (0, 0)),
                  pl.BlockSpec((None, d, tf), lambda i, f: (layer, 0, f)),
                  pl.BlockSpec((None, d, tf), lambda i, f: (layer, 0, f)),
                  pl.BlockSpec((None, tf, d), lambda i, f: (layer, f, 0)),
                  pl.BlockSpec((FFN_CONV, tf), lambda i, f: (0, f)),
                  pl.BlockSpec((1, tf), lambda i, f: (0, f)),
                  pl.BlockSpec((spt, SUBLANES, tf), hist_map),
                  pl.BlockSpec((1, d), lambda i, f: (0, 0))] + cast_src_specs,
        out_specs=[pl.BlockSpec((tm, d), lambda i, f: (i, 0)),
                   pl.BlockSpec((spt, SUBLANES, tf), lambda i, f: (i, 0, f))] + cast_specs,
        out_shape=[jax.ShapeDtypeStruct((m, d), F32),
                   jax.ShapeDtypeStruct((nm * spt, SUBLANES, ff), F32)]
                  + [jax.ShapeDtypeStruct(w.shape[1:], BF16) for w in cast_in],
        scratch_shapes=[pltpu.VMEM((tm, d), BF16),
                        pltpu.VMEM((nf * nsub, SUBLANES, ts), F32),
                        pltpu.VMEM((rows + SUBLANES, ts), F32)],
        compiler_params=_cparams(2),
        name="conv_ffn",
    )(x, norm_w.reshape(1, d), wg, wu, wd, conv_w, conv_b.reshape(1, ff), hist8, final_w.reshape(1, d), *cast_in)
    return results[0], results[1], tuple(results[2:])


def _retention_kernel(q_ref, k_ref, v_ref, g_ref, cos_ref, sin_ref, st_ref, nw_ref,
                      y_ref, so_ref, *, c):
    ci = pl.program_id(1)

    @pl.when(ci == 0)
    def _():
        so_ref[...] = st_ref[...]

    cos = cos_ref[...]
    sin = sin_ref[...]
    ii = lax.broadcasted_iota(jnp.int32, (c, c), 0)
    jj = lax.broadcasted_iota(jnp.int32, (c, c), 1)
    diff = (ii - jj).astype(F32)
    causal = ii >= jj
    ridx = lax.broadcasted_iota(jnp.int32, (c, 1), 0).astype(F32)
    for h in range(RET_HEADS):
        lg = math.log1p(-(2.0 ** (-5.0 - h)))
        q = q_ref[:, h * RET_DK:(h + 1) * RET_DK]
        k = k_ref[:, h * RET_DK:(h + 1) * RET_DK]
        v = v_ref[:, h * RET_DV:(h + 1) * RET_DV]
        qr = q * cos + pltpu.roll(q, RET_DK // 2, 1) * sin
        kr = (k * cos + pltpu.roll(k, RET_DK // 2, 1) * sin) * (RET_DK ** -0.5)
        qb = qr.astype(BF16)
        kb = kr.astype(BF16)
        vb = v.astype(BF16)
        decay = jnp.exp(jnp.where(causal, diff * lg, NEG_INF))
        inner = jnp.exp((ridx + 1.0) * lg)
        sdecay = jnp.exp((c - 1.0 - ridx) * lg)
        s = so_ref[0, h]
        scores = _dot_nt(qb, kb) * decay
        y = _dot(scores.astype(BF16), vb)
        y = y + _dot(qb, s.astype(BF16)) * inner
        kd = (kr * sdecay).astype(BF16)
        so_ref[0, h] = math.exp(c * lg) * s + _dot_tn(kd, vb)
        mu = jnp.mean(y, axis=-1, keepdims=True)
        yc = y - mu
        var = jnp.mean(yc * yc, axis=-1, keepdims=True)
        yn = yc * lax.rsqrt(var + EPS) * nw_ref[:, h * RET_DV:(h + 1) * RET_DV]
        g = g_ref[:, h * RET_DV:(h + 1) * RET_DV]
        y_ref[:, h * RET_DV:(h + 1) * RET_DV] = (jax.nn.silu(g) * yn).astype(BF16)


def _retention(proj, cosf, sinf, state, norm_w, *, bsz, seq_len, c):
    m = proj.shape[0]
    nc = seq_len // c
    assert seq_len % c == 0
    qk_w = RET_HEADS * RET_DK
    v_w = RET_HEADS * RET_DV
    row = lambda b, ci: b * nc + ci
    y, s_new = pl.pallas_call(
        functools.partial(_retention_kernel, c=c),
        grid=(bsz, nc),
        in_specs=[pl.BlockSpec((c, qk_w), lambda b, ci: (row(b, ci), 0)),
                  pl.BlockSpec((c, qk_w), lambda b, ci: (row(b, ci), 1)),
                  pl.BlockSpec((c, v_w), lambda b, ci: (row(b, ci), 1)),
                  pl.BlockSpec((c, v_w), lambda b, ci: (row(b, ci), 2)),
                  pl.BlockSpec((c, RET_DK), lambda b, ci: (ci, 0)),
                  pl.BlockSpec((c, RET_DK), lambda b, ci: (ci, 0)),
                  pl.BlockSpec((1, RET_HEADS, RET_DK, RET_DV), lambda b, ci: (b, 0, 0, 0)),
                  pl.BlockSpec((1, v_w), lambda b, ci: (0, 0))],
        out_specs=[pl.BlockSpec((c, v_w), lambda b, ci: (row(b, ci), 0)),
                   pl.BlockSpec((1, RET_HEADS, RET_DK, RET_DV), lambda b, ci: (b, 0, 0, 0))],
        out_shape=[jax.ShapeDtypeStruct((m, v_w), BF16),
                   jax.ShapeDtypeStruct(state.shape, F32)],
        compiler_params=_cparams(2),
        name="retention",
    )(proj, proj, proj, proj, cosf, sinf, state, norm_w.reshape(1, v_w))
    return y, s_new


def _ssd_kernel(*refs, c, nc, n_cast):
    (z_ref, xs_ref, bc_ref, dt_ref, hx_ref, hbc_ref, st_ref, cwx_ref, cwbc_ref, cbx_ref, cbbc_ref,
     dtb_ref, alog_ref, dsk_ref, nw_ref, tri_ref, exp_ref) = refs[:17]
    cast_in = refs[17:17 + n_cast]
    y_ref, so_ref = refs[17 + n_cast:19 + n_cast]
    cast_out = refs[19 + n_cast:19 + 2 * n_cast]
    st_scr, cx_scr, cbc_scr, extx_scr, extbc_scr, yh_scr, xs_scr = refs[19 + 2 * n_cast:]
    ci = pl.program_id(1)

    for src_ref, dst_ref in zip(cast_in, cast_out):
        dst_ref[...] = src_ref[...].astype(BF16)
    gw = SSD_DINNER // SSD_GROUPS
    hpg = SSD_HEADS // SSD_GROUPS

    @pl.when(ci == 0)
    def _():
        st_scr[...] = st_ref[0].T
        cx_scr[...] = hx_ref[0]
        cbc_scr[...] = hbc_ref[0]

    xs_raw = xs_ref[...]
    bc_raw = bc_ref[...]
    xs_scr[...] = jax.nn.silu(_conv_rows(extx_scr, xs_raw, cx_scr[...], cwx_ref, SSD_CONV, c) + cbx_ref[...])
    bcm = jax.nn.silu(_conv_rows(extbc_scr, bc_raw, cbc_scr[...], cwbc_ref, SSD_CONV, c) + cbbc_ref[...])
    cx_scr[...] = xs_raw[c - SUBLANES:c]
    cbc_scr[...] = bc_raw[c - SUBLANES:c]

    tri = tri_ref[...]
    dt = _softplus(dt_ref[...] + dtb_ref[...])
    a = -jnp.exp(alog_ref[...])
    acs = _exact_lhs_dot(tri, dt * a)
    acs_t = acs.T
    acs_last = acs[c - 1:c, :]
    exp_acs = jnp.exp(acs)
    to_end = jnp.exp(acs_last - acs)
    chunk_dec = jnp.exp(acs_last)

    ii = lax.broadcasted_iota(jnp.int32, (c, c), 0)
    jj = lax.broadcasted_iota(jnp.int32, (c, c), 1)
    causal = ii >= jj
    nb = SSD_GROUPS * SSD_DSTATE
    for g in range(SSD_GROUPS):
        cols = slice(g * gw, (g + 1) * gw)
        expand = exp_ref[:, cols]
        xdt = xs_scr[:, cols] * _exact_rhs_dot(dt, expand)
        xdt_b = xdt.astype(BF16)
        xend_b = (xdt * _exact_rhs_dot(to_end, expand)).astype(BF16)
        b_g = bcm[:, g * SSD_DSTATE:(g + 1) * SSD_DSTATE].astype(BF16)
        c_g = bcm[:, nb + g * SSD_DSTATE:nb + (g + 1) * SSD_DSTATE].astype(BF16)
        cb = _dot_nt(c_g, b_g)
        s_g = st_scr[:, cols]
        y_state = _dot(c_g, s_g.astype(BF16)) * _exact_rhs_dot(exp_acs, expand)
        for r in range(hpg):
            hh = g * hpg + r
            seg = acs[:, hh:hh + 1] - acs_t[hh:hh + 1, :]
            lmat = jnp.exp(jnp.where(causal, seg, NEG_INF))
            mm = (cb * lmat).astype(BF16)
            head = slice(r * SSD_HEADDIM, (r + 1) * SSD_HEADDIM)
            yh_scr[:, hh * SSD_HEADDIM:(hh + 1) * SSD_HEADDIM] = _dot(mm, xdt_b[:, head]) + y_state[:, head]
        upd = _dot_tn(b_g, xend_b)
        st_scr[:, cols] = _exact_rhs_dot(chunk_dec, expand) * s_g + upd

    y = yh_scr[...] + dsk_ref[...] * xs_scr[...]
    z = z_ref[...]
    y_ref[...] = _rms(y * jax.nn.silu(z), nw_ref[...]).astype(BF16)

    @pl.when(ci == nc - 1)
    def _():
        so_ref[0] = st_scr[...].T


def _row_block_cast_specs(w3, layer, n_steps, step_of):
    _, r, cols = w3.shape
    for per in (1, 2, 4, 8):
        if (r * per) % n_steps == 0 and (r * per // n_steps) % 16 == 0:
            rb = r * per // n_steps
            src = pl.BlockSpec((None, rb, cols), lambda *g: (layer, step_of(*g) // per, 0))
            dst = pl.BlockSpec((rb, cols), lambda *g: (step_of(*g) // per, 0))
            return src, dst, jax.ShapeDtypeStruct((r, cols), BF16)
    return None


def _ssd(proj, dt_proj, hist8, state, conv_w, conv_b, dt_bias, a_log, d_skip, norm_w, *, bsz, seq_len, c,
         cast=None):
    m = proj.shape[0]
    nc = seq_len // c
    assert seq_len % c == 0
    row = lambda b, ci: b * nc + ci
    cast_in, cast_src, cast_dst, cast_shapes = [], [], [], []
    if cast is not None:
        specs = [_row_block_cast_specs(w, cast[0], bsz * nc, row) for w in cast[1]]
        if all(s is not None for s in specs):
            cast_in = list(cast[1])
            cast_src, cast_dst, cast_shapes = (list(x) for x in zip(*specs))
    const2 = lambda b, ci: (0, 0)
    di, bcw = SSD_DINNER, 2 * SSD_GROUPS * SSD_DSTATE
    tri = jnp.asarray(np.tril(np.ones((c, c), np.float32)), BF16)
    expand = np.zeros((LANES, di), np.float32)
    for h in range(SSD_HEADS):
        expand[h, h * SSD_HEADDIM:(h + 1) * SSD_HEADDIM] = 1.0
    expand = jnp.asarray(expand, BF16)
    pad_row = lambda v: jnp.pad(v.astype(F32), (0, LANES - v.shape[0])).reshape(1, LANES)
    st2 = state.reshape(bsz, di, SSD_DSTATE)
    results = pl.pallas_call(
        functools.partial(_ssd_kernel, c=c, nc=nc, n_cast=len(cast_in)),
        grid=(bsz, nc),
        in_specs=[pl.BlockSpec((c, di), lambda b, ci: (row(b, ci), 3)),
                  pl.BlockSpec((c, di), lambda b, ci: (row(b, ci), 4)),
                  pl.BlockSpec((c, bcw), lambda b, ci: (row(b, ci), 10)),
                  pl.BlockSpec((c, LANES), lambda b, ci: (row(b, ci), 0)),
                  pl.BlockSpec((1, SUBLANES, di), lambda b, ci: (b, 0, 0)),
                  pl.BlockSpec((1, SUBLANES, bcw), lambda b, ci: (b, 0, 2)),
                  pl.BlockSpec((1, di, SSD_DSTATE), lambda b, ci: (b, 0, 0)),
                  pl.BlockSpec((SSD_CONV, di), const2),
                  pl.BlockSpec((SSD_CONV, bcw), lambda b, ci: (0, 2)),
                  pl.BlockSpec((1, di), const2),
                  pl.BlockSpec((1, bcw), lambda b, ci: (0, 2)),
                  pl.BlockSpec((1, LANES), const2),
                  pl.BlockSpec((1, LANES), const2),
                  pl.BlockSpec((1, di), const2),
                  pl.BlockSpec((1, di), const2),
                  pl.BlockSpec((c, c), const2),
                  pl.BlockSpec((LANES, di), const2)] + cast_src,
        out_specs=[pl.BlockSpec((c, di), lambda b, ci: (row(b, ci), 0)),
                   pl.BlockSpec((1, di, SSD_DSTATE), lambda b, ci: (b, 0, 0))] + cast_dst,
        out_shape=[jax.ShapeDtypeStruct((m, di), BF16),
                   jax.ShapeDtypeStruct(st2.shape, F32)] + cast_shapes,
        scratch_shapes=[pltpu.VMEM((SSD_DSTATE, di), F32),
                        pltpu.VMEM((SUBLANES, di), F32),
                        pltpu.VMEM((SUBLANES, bcw), F32),
                        pltpu.VMEM((c + SUBLANES, di), F32),
                        pltpu.VMEM((c + SUBLANES, bcw), F32),
                        pltpu.VMEM((c, di), F32),
                        pltpu.VMEM((c, di), F32)],
        compiler_params=_cparams(2),
        name="ssd",
    )(proj, proj, proj, dt_proj, hist8, hist8, st2,
      conv_w, conv_w, conv_b.reshape(1, -1), conv_b.reshape(1, -1),
      pad_row(dt_bias), pad_row(a_log), jnp.repeat(d_skip.astype(F32), SSD_HEADDIM).reshape(1, di),
      norm_w.reshape(1, di), tri, expand, *cast_in)
    return results[0], results[1].reshape(state.shape), (list(results[2:]) if cast_in else None)


def _decode_cum_kernel(lfc_ref, fl_ref, b_ref, tri_ref, cumt_c_ref, lfn_ref, cumn_ref, cumt_n_ref,
                       carry_ref, *, c, ncb):
    j = pl.program_id(0)

    @pl.when(j == 0)
    def _():
        carry_ref[...] = jnp.zeros_like(carry_ref)

    @pl.when(j < ncb)
    def _():
        cum = _exact_lhs_dot(tri_ref[...], lfc_ref[...]) + carry_ref[...]
        carry_ref[...] = cum[c - 1:c, :]
        cumt_c_ref[...] = cum.T

    @pl.when(j == ncb)
    def _():
        lf = -_softplus(-(fl_ref[...] + b_ref[...]))
        lfn_ref[...] = lf
        cum = _exact_lhs_dot(tri_ref[0:LANES, 0:LANES], lf) + carry_ref[...]
        cumn_ref[...] = cum
        cumt_n_ref[...] = cum.T


def _decode_cum(cache_lf, fl_new, bias, *, c):
    past = cache_lf.shape[0]
    assert past % c == 0 and c % LANES == 0 and fl_new.shape == (LANES, LANES)
    ncb = past // c
    tri = jnp.asarray(np.tril(np.ones((c, c), np.float32)), BF16)
    blk = lambda j: jnp.minimum(j, ncb - 1)
    sq = jax.ShapeDtypeStruct((LANES, LANES), F32)
    return pl.pallas_call(
        functools.partial(_decode_cum_kernel, c=c, ncb=ncb),
        grid=(ncb + 1,),
        in_specs=[pl.BlockSpec((c, LANES), lambda j: (blk(j), 0)),
                  pl.BlockSpec((LANES, LANES), lambda j: (0, 0)),
                  pl.BlockSpec((1, LANES), lambda j: (0, 0)),
                  pl.BlockSpec((c, c), lambda j: (0, 0))],
        out_specs=[pl.BlockSpec((LANES, c), lambda j: (0, blk(j))),
                   pl.BlockSpec((LANES, LANES), lambda j: (0, 0)),
                   pl.BlockSpec((LANES, LANES), lambda j: (0, 0)),
                   pl.BlockSpec((LANES, LANES), lambda j: (0, 0))],
        out_shape=[jax.ShapeDtypeStruct((LANES, past), F32), sq, sq, sq],
        scratch_shapes=[pltpu.VMEM((1, LANES), F32)],
        compiler_params=_cparams(1),
        name="decode_logf_cumsum",
    )(cache_lf, fl_new, bias, tri)


FOX_AUG = 2 * FOX_HEAD_DIM
N_BIAS_PIECES = 3


def _fox_prep_kernel(q_ref, k_ref, v_ref, fl_ref, fb_ref, tri_ref, place_ref, ones_ref,
                     qa_ref, ka_ref, vb_ref, k32_ref, v32_ref, lf_ref, carry_ref, *, tp):
    @pl.when(pl.program_id(1) == 0)
    def _():
        carry_ref[...] = jnp.zeros_like(carry_ref)

    lf = -_softplus(-(fl_ref[...] + fb_ref[...]))
    lf_ref[...] = lf
    cum = _exact_lhs_dot(tri_ref[...], lf) + carry_ref[...]
    carry_ref[...] = cum[tp - 1:tp, :]
    pieces = _split3(cum * (FOX_HEAD_DIM ** 0.5))
    n = N_BIAS_PIECES
    aug_q = ones_ref[0:1, :] + sum(_dot(pieces[r], place_ref[r]) for r in range(n))
    aug_k = ones_ref[1:2, :] - sum(_dot(pieces[r], place_ref[n + r]) for r in range(n))
    for h in range(FOX_HEADS):
        src = slice(h * FOX_HEAD_DIM, (h + 1) * FOX_HEAD_DIM)
        feat = slice(h * FOX_AUG, h * FOX_AUG + FOX_HEAD_DIM)
        bias = slice(h * FOX_AUG + FOX_HEAD_DIM, (h + 1) * FOX_AUG)
        qa_ref[:, feat] = q_ref[:, src].astype(BF16)
        qa_ref[:, bias] = aug_q[:, src].astype(BF16)
        ka_ref[:, feat] = k_ref[:, src].astype(BF16)
        ka_ref[:, bias] = aug_k[:, src].astype(BF16)
    k = k_ref[...]
    v = v_ref[...]
    k32_ref[...] = pltpu.einshape("m(hd)->mhd", k, h=FOX_HEADS)
    v32_ref[...] = pltpu.einshape("m(hd)->mhd", v, h=FOX_HEADS)
    vb_ref[...] = v.astype(BF16)


def _fox_prep(proj, fl_proj, f_bias, *, bsz, seq_len, tp):
    m = proj.shape[0]
    nt = seq_len // tp
    assert seq_len % tp == 0
    w = FOX_WIDTH
    tri = jnp.asarray(np.tril(np.ones((tp, tp), np.float32)), BF16)
    n = N_BIAS_PIECES
    place = np.zeros((2 * n, LANES, w), np.float32)
    ones = np.zeros((SUBLANES, w), np.float32)
    for h in range(FOX_HEADS):
        for r in range(2 * n):
            place[r, h, h * FOX_HEAD_DIM + r] = 1.0
        ones[0, h * FOX_HEAD_DIM + n:h * FOX_HEAD_DIM + 2 * n] = 1.0
        ones[1, h * FOX_HEAD_DIM:h * FOX_HEAD_DIM + n] = 1.0
    row = lambda b, ti: (b * nt + ti, 0)
    const2 = lambda b, ti: (0, 0)
    return pl.pallas_call(
        functools.partial(_fox_prep_kernel, tp=tp),
        grid=(bsz, nt),
        in_specs=[pl.BlockSpec((tp, w), lambda b, ti: (b * nt + ti, 0)),
                  pl.BlockSpec((tp, w), lambda b, ti: (b * nt + ti, 1)),
                  pl.BlockSpec((tp, w), lambda b, ti: (b * nt + ti, 2)),
                  pl.BlockSpec((tp, LANES), row),
                  pl.BlockSpec((1, LANES), const2),
                  pl.BlockSpec((tp, tp), const2),
                  pl.BlockSpec((2 * n, LANES, w), lambda b, ti: (0, 0, 0)),
                  pl.BlockSpec((SUBLANES, w), const2)],
        out_specs=[pl.BlockSpec((tp, FOX_HEADS * FOX_AUG), row),
                   pl.BlockSpec((tp, FOX_HEADS * FOX_AUG), row),
                   pl.BlockSpec((tp, w), row),
                   pl.BlockSpec((tp, FOX_HEADS, FOX_HEAD_DIM), lambda b, ti: (b * nt + ti, 0, 0)),
                   pl.BlockSpec((tp, FOX_HEADS, FOX_HEAD_DIM), lambda b, ti: (b * nt + ti, 0, 0)),
                   pl.BlockSpec((tp, LANES), row)],
        out_shape=[jax.ShapeDtypeStruct((m, FOX_HEADS * FOX_AUG), BF16),
                   jax.ShapeDtypeStruct((m, FOX_HEADS * FOX_AUG), BF16),
                   jax.ShapeDtypeStruct((m, w), BF16),
                   jax.ShapeDtypeStruct((m, FOX_HEADS, FOX_HEAD_DIM), F32),
                   jax.ShapeDtypeStruct((m, FOX_HEADS, FOX_HEAD_DIM), F32),
                   jax.ShapeDtypeStruct((m, LANES), F32)],
        scratch_shapes=[pltpu.VMEM((1, LANES), F32)],
        compiler_params=_cparams(2),
        name="fox_prep",
    )(proj, proj, proj, fl_proj, f_bias, tri, jnp.asarray(place, BF16), jnp.asarray(ones, F32))


def _fox_kernel(qi_ref, ki_ref, q_ref, k_ref, v_ref, o_ref, m_ref, acc_ref, va_ref, *, t, ts, ts_diag):
    step = pl.program_id(2)
    qi = qi_ref[step]
    ki = ki_ref[step]
    to_log2 = (FOX_HEAD_DIM ** -0.5) * math.log2(math.e)
    hd = FOX_HEAD_DIM

    @pl.when(ki == 0)
    def _():
        m_ref[...] = jnp.full_like(m_ref, NEG_INF)
        acc_ref[...] = jnp.zeros_like(acc_ref)
        va_ref[:, hd:2 * hd] = jnp.ones((t, hd), BF16)

    va_ref[:, 0:hd] = v_ref[...]

    def update(diagonal):
        tc = ts_diag if diagonal else ts
        for r in range(t // tc):
            rows = slice(r * tc, (r + 1) * tc)
            nk = (r + 1) * tc if diagonal else t
            s = _dot_nt(q_ref[rows, :], k_ref[0:nk, :])
            if diagonal:
                ri = lax.broadcasted_iota(jnp.int32, (tc, nk), 0) + r * tc
                ci = lax.broadcasted_iota(jnp.int32, (tc, nk), 1)
                s = jnp.where(ci <= ri, s, NEG_INF)
            m_old = m_ref[rows, :]
            m_new = jnp.maximum(m_old, jnp.max(s, axis=1, keepdims=True))
            m_ref[rows, :] = m_new
            alpha = jnp.exp2((m_old - m_new) * to_log2)
            p = jnp.exp2((s - _widen(m_new, nk)) * to_log2)
            acc_ref[rows, :] = _widen(alpha, 2 * hd) * acc_ref[rows, :] + _dot(p.astype(BF16), va_ref[0:nk, :])

    @pl.when(ki < qi)
    def _():
        update(False)

    @pl.when(ki == qi)
    def _():
        update(True)
        o_ref[...] = (acc_ref[:, 0:hd] / acc_ref[:, hd:2 * hd]).astype(BF16)


def _fox_prompt(qa, ka, vb, *, bsz, seq_len, t, ts, ts_diag):
    m = qa.shape[0]
    nq = seq_len // t
    assert seq_len % t == 0 and t % ts == 0 and t % ts_diag == 0
    pairs = [(qi, ki) for qi in range(nq) for ki in range(qi + 1)]
    qi_tab = jnp.asarray([p[0] for p in pairs], jnp.int32)
    ki_tab = jnp.asarray([p[1] for p in pairs], jnp.int32)
    grid_spec = pltpu.PrefetchScalarGridSpec(
        num_scalar_prefetch=2,
        grid=(bsz, FOX_HEADS, len(pairs)),
        in_specs=[pl.BlockSpec((t, FOX_AUG), lambda b, h, s, qi, ki: (b * nq + qi[s], h)),
                  pl.BlockSpec((t, FOX_AUG), lambda b, h, s, qi, ki: (b * nq + ki[s], h)),
                  pl.BlockSpec((t, FOX_HEAD_DIM), lambda b, h, s, qi, ki: (b * nq + ki[s], h))],
        out_specs=pl.BlockSpec((t, FOX_HEAD_DIM), lambda b, h, s, qi, ki: (b * nq + qi[s], h)),
        scratch_shapes=[pltpu.VMEM((t, LANES), F32), pltpu.VMEM((t, 2 * FOX_HEAD_DIM), F32),
                        pltpu.VMEM((t, 2 * FOX_HEAD_DIM), BF16)],
    )
    return pl.pallas_call(
        functools.partial(_fox_kernel, t=t, ts=ts, ts_diag=ts_diag),
        grid_spec=grid_spec,
        out_shape=jax.ShapeDtypeStruct((m, FOX_WIDTH), BF16),
        compiler_params=_cparams(3),
        name="fox_attention",
    )(qi_tab, ki_tab, qa, ka, vb)


def _fox_decode_kernel(q_ref, kn_ref, vn_ref, kc_ref, vc_ref, cq_ref, ckc_ref, ckn_ref, o_ref,
                       m_ref, l_ref, acc_ref, *, lq, ncb):
    j = pl.program_id(1)
    nh, hd = FOX_HEADS, FOX_HEAD_DIM

    @pl.when(j == 0)
    def _():
        m_ref[...] = jnp.full_like(m_ref, NEG_INF)
        l_ref[...] = jnp.zeros_like(l_ref)
        acc_ref[...] = jnp.zeros_like(acc_ref)

    def attend(k_head, v_head, ck_head, causal):
        for h in range(nh):
            qh = q_ref[:, h * hd:(h + 1) * hd].astype(BF16)
            ck = ck_head(h)
            tk = ck.shape[1]
            s = _dot_nt(qh, k_head(h).astype(BF16)) * (hd ** -0.5)
            s = s + (_widen(cq_ref[h], tk) - ck)
            if causal:
                rows = lax.broadcasted_iota(jnp.int32, (lq, tk), 0)
                cols = lax.broadcasted_iota(jnp.int32, (lq, tk), 1)
                s = jnp.where(cols <= rows, s, NEG_INF)
            m_old = m_ref[h]
            m_new = jnp.maximum(m_old, jnp.max(s, axis=1, keepdims=True))
            alpha = jnp.exp(m_old - m_new)
            p = jnp.exp(s - _widen(m_new, tk))
            l_ref[h] = alpha * l_ref[h] + jnp.sum(p, axis=1, keepdims=True)
            acc_ref[h] = alpha * acc_ref[h] + _dot(p.astype(BF16), v_head(h).astype(BF16))
            m_ref[h] = m_new

    @pl.when(j < ncb)
    def _():
        k_hm = pltpu.einshape("mhd->hmd", kc_ref[0])
        v_hm = pltpu.einshape("mhd->hmd", vc_ref[0])
        attend(lambda h: k_hm[h], lambda h: v_hm[h], lambda h: ckc_ref[h], False)

    @pl.when(j == ncb)
    def _():
        attend(lambda h: kn_ref[:, h * hd:(h + 1) * hd], lambda h: vn_ref[:, h * hd:(h + 1) * hd],
               lambda h: ckn_ref[h][:, 0:lq], True)
        for h in range(nh):
            o_ref[:, h * hd:(h + 1) * hd] = (acc_ref[h] / l_ref[h]).astype(BF16)


def _fox_decode(proj, cache_k, cache_v, cq, ck_cache, ck_new, *, bsz, lq, tk):
    past = cache_k.shape[1]
    assert past % tk == 0 and tk % LANES == 0 and lq <= LANES
    ncb = past // tk
    nh, hd, w = FOX_HEADS, FOX_HEAD_DIM, FOX_WIDTH
    tile = lambda j: jnp.minimum(j, ncb - 1)
    cache_spec = pl.BlockSpec((1, tk, nh, hd), lambda b, j: (b, tile(j), 0, 0))
    return pl.pallas_call(
        functools.partial(_fox_decode_kernel, lq=lq, ncb=ncb),
        grid=(bsz, ncb + 1),
        in_specs=[pl.BlockSpec((lq, w), lambda b, j: (b, 0)),
                  pl.BlockSpec((lq, w), lambda b, j: (b, 1)),
                  pl.BlockSpec((lq, w), lambda b, j: (b, 2)),
                  cache_spec,
                  cache_spec,
                  pl.BlockSpec((nh, lq, LANES), lambda b, j: (b, 0, 0)),
                  pl.BlockSpec((nh, 1, tk), lambda b, j: (b, 0, tile(j))),
                  pl.BlockSpec((nh, 1, LANES), lambda b, j: (b, 0, 0))],
        out_specs=pl.BlockSpec((lq, w), lambda b, j: (b, 0)),
        out_shape=jax.ShapeDtypeStruct((bsz * lq, w), BF16),
        scratch_shapes=[pltpu.VMEM((nh, lq, LANES), F32), pltpu.VMEM((nh, lq, LANES), F32),
                        pltpu.VMEM((nh, lq, hd), F32)],
        compiler_params=_cparams(2),
        name="fox_decode",
    )(proj, proj, proj, cache_k, cache_v, cq, ck_cache, ck_new)


def _sconv_kernel(u_ref, bg_ref, cg_ref, cw_ref, hist_ref, y_ref, tail_ref, carry_ref, ext_ref,
                  *, tm, rows, spt, tpb):
    i = pl.program_id(0)
    w = cg_ref[...] * u_ref[...]
    first = (i % tpb) == 0
    for s in range(spt):
        w_s = w[s * rows:(s + 1) * rows]
        if tpb == 1:
            prev = hist_ref[s]
        else:
            prev = jnp.where(first, hist_ref[s], carry_ref[...])
        conv = _conv_rows(ext_ref, w_s, prev, cw_ref, SC_WIDTH, rows)
        y_ref[s * rows:(s + 1) * rows, :] = (bg_ref[s * rows:(s + 1) * rows, :] * conv).astype(BF16)
        tail_ref[s] = w_s[rows - SUBLANES:rows]
    if tpb > 1:
        carry_ref[...] = w[tm - SUBLANES:tm]


def _sconv(proj, conv_w, hist8, *, seq_len, tm):
    m = proj.shape[0]
    assert m % tm == 0
    rows, spt, tpb = _seq_tiling(seq_len, tm)
    nm = m // tm
    hist_map = (lambda i: (i // tpb, 0, 0)) if spt == 1 else (lambda i: (i, 0, 0))
    base = 3 * FOX_WIDTH // SC_DIM
    return pl.pallas_call(
        functools.partial(_sconv_kernel, tm=tm, rows=rows, spt=spt, tpb=tpb),
        grid=(nm,),
        in_specs=[pl.BlockSpec((tm, SC_DIM), lambda i: (i, base)),
                  pl.BlockSpec((tm, SC_DIM), lambda i: (i, base + 1)),
                  pl.BlockSpec((tm, SC_DIM), lambda i: (i, base + 2)),
                  pl.BlockSpec((SC_WIDTH, SC_DIM), lambda i: (0, 0)),
                  pl.BlockSpec((spt, SUBLANES, SC_DIM), hist_map)],
        out_specs=[pl.BlockSpec((tm, SC_DIM), lambda i: (i, 0)),
                   pl.BlockSpec((spt, SUBLANES, SC_DIM), lambda i: (i, 0, 0))],
        out_shape=[jax.ShapeDtypeStruct((m, SC_DIM), BF16),
                   jax.ShapeDtypeStruct((nm * spt, SUBLANES, SC_DIM), F32)],
        scratch_shapes=[pltpu.VMEM((SUBLANES, SC_DIM), F32),
                        pltpu.VMEM((rows + SUBLANES, SC_DIM), F32)],
        compiler_params=_cparams(1),
        name="gated_short_conv",
    )(proj, proj, proj, conv_w, hist8)


def _hist8(state):
    n, w1, c = state.shape
    return jnp.concatenate([jnp.zeros((n, SUBLANES - w1, c), F32), state.astype(F32)], axis=1)


def _tails(tails, n_seq, seq_len, tile_rows, keep):
    per_seq = max(1, seq_len // tile_rows)
    pieces = tails.reshape(n_seq, per_seq, SUBLANES, tails.shape[-1])
    return pieces[:, per_seq - 1, SUBLANES - keep:, :]


def _rope_tables(pos0, length):
    half = RET_DK // 2
    inv = ROPE_BASE ** (-np.arange(half, dtype=np.float64) / half)
    ang = (pos0 + np.arange(length, dtype=np.float64))[:, None] * inv[None, :]
    cos, sin = np.cos(ang), np.sin(ang)
    return (jnp.asarray(np.concatenate([cos, cos], axis=1), F32),
            jnp.asarray(np.concatenate([-sin, sin], axis=1), F32))


def _prep_weights(p):
    d = D_MODEL
    ab_in = p['ab_w_in'][0]
    cd_in = p['cd_w_in'][0]
    f0 = 3 * FOX_WIDTH
    pad_cols = lambda w: jnp.pad(w, ((0, 0), (0, LANES - w.shape[1]))).astype(BF16)
    return dict(
        ab_in=ab_in.astype(BF16),
        ab_small=pad_cols(ab_in[:, AB_MAIN:]),
        cd_in=jnp.concatenate([cd_in[:, :f0], cd_in[:, f0 + FOX_HEADS:]], axis=1).astype(BF16),
        cd_small=pad_cols(cd_in[:, f0:f0 + FOX_HEADS]),
        ab_out=p['ab_w_out'][0].astype(BF16),
        cd_out=p['cd_w_out'][0].astype(BF16),
        ffn0=None,
        ffn1=None,
    )


def _trunk(x, pos0, st_ret, st_ssd, st_ssd_conv, c_k, c_v, c_logf, st_sconv, st_ffn, p, wb, t):
    bsz, length, d = x.shape
    m = bsz * length
    xf = x.reshape(m, d)
    zeros = lambda *shape: jnp.zeros(shape, F32)

    proj, dt_proj = _norm_matmul(xf, p['ab_norm_w'][0], wb['ab_in'], wb['ab_small'],
                                 n=AB_MAIN, tm=t['tm_ab'], tn=t['tn_ab'])
    cosf, sinf = _rope_tables(pos0, length)
    ret_state = zeros(bsz, RET_HEADS, RET_DK, RET_DV) if st_ret is None else st_ret
    y_ret, ret_new = _retention(proj, cosf, sinf, ret_state, p['ret_norm_w'][0],
                                bsz=bsz, seq_len=length, c=t['c_ret'])
    ssd_state = zeros(bsz, SSD_HEADS, SSD_HEADDIM, SSD_DSTATE) if st_ssd is None else st_ssd
    ssd_hist = zeros(bsz, SSD_CONV - 1, SSD_CONV_DIM) if st_ssd_conv is None else st_ssd_conv
    ffn_f32 = [p['ffn_w_gate'], p['ffn_w_up'], p['ffn_w_down']]
    y_ssd, ssd_new, ffn0_bf16 = _ssd(proj, dt_proj, _hist8(ssd_hist), ssd_state, p['ssd_conv_w'][0],
                                     p['ssd_conv_b'][0], p['ssd_dt_bias'][0], p['ssd_A_log'][0], p['ssd_D'][0],
                                     p['ssd_norm_w'][0], bsz=bsz, seq_len=length, c=t['c_ssd'],
                                     cast=(0, ffn_f32) if wb['ffn0'] is None else None)
    if wb['ffn0'] is None:
        wb['ffn0'] = (tuple(w[None] for w in ffn0_bf16) if ffn0_bf16 is not None
                      else tuple(w[0:1].astype(BF16) for w in ffn_f32))
    xbc_lo = AB_MAIN - SSD_CONV_DIM
    ssd_conv_new = proj.reshape(bsz, length, -1)[:, length - (SSD_CONV - 1):, xbc_lo:AB_MAIN]
    xf = _proj_residual(xf, y_ret, y_ssd, wb['ab_out'], tm=t['tm_out'], tn=t['tn_out'])

    ffn_new = []
    ffn_hist0 = zeros(bsz, FFN_CONV - 1, D_FF) if st_ffn is None else st_ffn[0]
    next_f32 = (1, p['ffn_w_gate'], p['ffn_w_up'], p['ffn_w_down'])
    n_row_tiles = m // t['tm_ffn']
    if wb['ffn1'] is None and not (d % (n_row_tiles * LANES) == 0 and D_FF % t['tf_ffn'] == 0):
        wb['ffn1'] = tuple(w[1:2].astype(BF16) for w in next_f32[1:])
    xf, tails, next_bf16 = _conv_ffn(
        xf, p['ffn_norm_w'][0], *wb['ffn0'], p['ffn_conv_w'][0], p['ffn_conv_b'][0], _hist8(ffn_hist0),
        p['final_norm_w'], layer=0, seq_len=length, tm=t['tm_ffn'], tf=t['tf_ffn'], ts=t['ts_ffn'], final=False,
        cast_next=next_f32 if wb['ffn1'] is None else None)
    if wb['ffn1'] is None:
        wb['ffn1'] = tuple(w[None] for w in next_bf16)
    ffn_new.append(_tails(tails, bsz, length, t['tm_ffn'], FFN_CONV - 1))

    proj, fl_proj = _norm_matmul(xf, p['cd_norm_w'][0], wb['cd_in'], wb['cd_small'],
                                 n=CD_MAIN, tm=t['tm_cd'], tn=t['tn_cd'])
    f_bias = jnp.pad(p['fox_f_bias'][0].astype(F32), (0, LANES - FOX_HEADS)).reshape(1, LANES)
    head_shape = (bsz, length, FOX_HEADS, FOX_HEAD_DIM)
    if c_k is None:
        qa, ka, vb, k32, v32, logf = _fox_prep(proj, fl_proj, f_bias, bsz=bsz, seq_len=length, tp=t['t_prep'])
        y_fox = _fox_prompt(qa, ka, vb, bsz=bsz, seq_len=length, t=t['t_fox'], ts=t['ts_fox'],
                            ts_diag=t['ts_fox_diag'])
        logf_new = logf.reshape(bsz, length, LANES)[:, :, :FOX_HEADS]
        k_new, v_new = k32.reshape(head_shape), v32.reshape(head_shape)
    else:
        proj3 = proj.reshape(bsz, length, -1)
        k_new = proj3[:, :, FOX_WIDTH:2 * FOX_WIDTH].reshape(head_shape)
        v_new = proj3[:, :, 2 * FOX_WIDTH:3 * FOX_WIDTH].reshape(head_shape)
        past = c_k.shape[1]
        pairs = bsz * FOX_HEADS
        assert pairs <= LANES and length <= LANES
        to_lanes = lambda a, rows: jnp.pad(jnp.swapaxes(a, 0, 1).reshape(a.shape[1], pairs),
                                           ((0, rows - a.shape[1]), (0, LANES - pairs)))
        from_lanes = lambda a: jnp.swapaxes(a[:length, :pairs].reshape(length, bsz, FOX_HEADS), 0, 1)
        cache_lf = to_lanes(c_logf.astype(F32), past)
        fl_rows = to_lanes(fl_proj.reshape(bsz, length, LANES)[:, :, :FOX_HEADS], LANES)
        bias_lanes = jnp.pad(jnp.tile(p['fox_f_bias'][0].astype(F32), bsz), (0, LANES - pairs)).reshape(1, LANES)
        cum_t_cache, lf_rows, cum_rows, cum_t_new = _decode_cum(cache_lf, fl_rows, bias_lanes, c=t['c_cum'])
        logf_new = from_lanes(lf_rows)
        cq = jnp.broadcast_to(cum_rows[:length, :pairs].T[:, :, None], (pairs, length, LANES))
        y_fox = _fox_decode(proj, c_k, c_v, cq, cum_t_cache[:pairs, None, :], cum_t_new[:pairs, None, :],
                            bsz=bsz, lq=length, tk=t['tk_dec'])
    sc_hist = zeros(bsz, SC_WIDTH - 1, SC_DIM) if st_sconv is None else st_sconv
    y_sc, sc_tails = _sconv(proj, p['sconv_w'][0], _hist8(sc_hist), seq_len=length, tm=t['tm_sc'])
    sconv_new = _tails(sc_tails, bsz, length, t['tm_sc'], SC_WIDTH - 1)
    xf = _proj_residual(xf, y_fox, y_sc, wb['cd_out'], tm=t['tm_out'], tn=t['tn_out'])

    ffn_hist1 = zeros(bsz, FFN_CONV - 1, D_FF) if st_ffn is None else st_ffn[1]
    xf, tails, _ = _conv_ffn(xf, p['ffn_norm_w'][1], *wb['ffn1'], p['ffn_conv_w'][1], p['ffn_conv_b'][1],
                             _hist8(ffn_hist1), p['final_norm_w'], layer=0, seq_len=length, tm=t['tm_ffn'],
                             tf=t['tf_ffn'], ts=t['ts_ffn'], final=True)
    ffn_new.append(_tails(tails, bsz, length, t['tm_ffn'], FFN_CONV - 1))

    return (xf.reshape(bsz, length, d), ret_new[None], ssd_new[None], ssd_conv_new[None], k_new[None],
            v_new[None], logf_new[None], sconv_new[None], jnp.stack(ffn_new))


def _largest_divisor(n, cap, multiple=1):
    best = None
    for cand in range(multiple, min(n, cap) + 1, multiple):
        if n % cand == 0:
            best = cand
    assert best is not None, (n, cap, multiple)
    return best


def _tiles(bsz, length, past=None):
    m = bsz * length
    seq_tile = lambda cap: _largest_divisor(length, cap, SUBLANES)
    row_tile = lambda cap: (_largest_divisor(length, cap, SUBLANES) if length >= cap
                            else _largest_divisor(m, cap, length))
    t = dict(
        tm_ab=row_tile(1024), tn_ab=2048, tm_cd=row_tile(1024), tn_cd=2048,
        tm_out=row_tile(1024), tn_out=2048,
        tm_ffn=row_tile(1024), tf_ffn=512, ts_ffn=512,
        tm_sc=row_tile(1024),
        c_ret=seq_tile(256), c_ssd=seq_tile(256),
    )
    if past is None:
        t['t_fox'] = seq_tile(2048)
        t['ts_fox'] = _largest_divisor(t['t_fox'], 256, LANES)
        t['ts_fox_diag'] = _largest_divisor(t['t_fox'], 512, LANES)
        t['t_prep'] = seq_tile(512)
    else:
        t['tk_dec'] = _largest_divisor(past, 1024, LANES)
        t['c_cum'] = _largest_divisor(past, 256, LANES)
    return t


def kernel(x_prompt, x_sample, state_ret, state_ssd, state_ssd_conv, cache_fox_k, cache_fox_v, cache_fox_logf, state_sconv, state_ffn_conv, ab_norm_w, ab_w_in, ret_norm_w, ssd_conv_w, ssd_conv_b, ssd_dt_bias, ssd_A_log, ssd_D, ssd_norm_w, ab_w_out, cd_norm_w, cd_w_in, fox_f_bias, sconv_w, cd_w_out, ffn_norm_w, ffn_w_gate, ffn_w_up, ffn_conv_w, ffn_conv_b, ffn_w_down, final_norm_w):
    p = dict(ab_norm_w=ab_norm_w, ab_w_in=ab_w_in, ret_norm_w=ret_norm_w, ssd_conv_w=ssd_conv_w,
             ssd_conv_b=ssd_conv_b, ssd_dt_bias=ssd_dt_bias, ssd_A_log=ssd_A_log, ssd_D=ssd_D,
             ssd_norm_w=ssd_norm_w, ab_w_out=ab_w_out, cd_norm_w=cd_norm_w, cd_w_in=cd_w_in,
             fox_f_bias=fox_f_bias, sconv_w=sconv_w, cd_w_out=cd_w_out, ffn_norm_w=ffn_norm_w,
             ffn_w_gate=ffn_w_gate, ffn_w_up=ffn_w_up, ffn_conv_w=ffn_conv_w, ffn_conv_b=ffn_conv_b,
             ffn_w_down=ffn_w_down, final_norm_w=final_norm_w)
    assert x_prompt.shape[-1] == D_MODEL and ab_w_in.shape == (1, D_MODEL, AB_MAIN + SSD_HEADS)
    assert cd_w_in.shape == (1, D_MODEL, CD_MAIN + FOX_HEADS) and ffn_w_gate.shape == (2, D_MODEL, D_FF)
    wb = _prep_weights(p)
    bp, lp_, _ = x_prompt.shape
    bs, ls, _ = x_sample.shape
    past = cache_fox_k.shape[2]
    (y_prompt, p_ret, p_ssd, p_ssd_conv, p_fox_k, p_fox_v, p_fox_logf, p_sconv, p_ffn_conv) = _trunk(
        x_prompt, 0, None, None, None, None, None, None, None, None, p, wb, _tiles(bp, lp_))
    (y_sample, s_ret, s_ssd, s_ssd_conv, s_fox_k, s_fox_v, s_fox_logf, s_sconv, s_ffn_conv) = _trunk(
        x_sample, past, state_ret[0], state_ssd[0], state_ssd_conv[0], cache_fox_k[0], cache_fox_v[0],
        cache_fox_logf[0], state_sconv[0], state_ffn_conv, p, wb, _tiles(bs, ls, past))
    return (y_prompt, y_sample, p_ret, s_ret, p_ssd, s_ssd, p_ssd_conv, s_ssd_conv, p_fox_k, s_fox_k,
            p_fox_v, s_fox_v, p_fox_logf, s_fox_logf, p_sconv, s_sconv, p_ffn_conv, s_ffn_conv)
```

```python
import functools
import math

import numpy as np
import jax
import jax.numpy as jnp
from jax import lax
from jax.experimental import pallas as pl
from jax.experimental.pallas import tpu as pltpu

F32 = jnp.float32
BF16 = jnp.bfloat16
EPS = 1e-6
ROPE_BASE = 10000.0
NEG_INF = float("-inf")

D_MODEL = 2048
RET_HEADS, RET_DK, RET_DV = 4, 128, 256
SSD_DINNER, SSD_HEADDIM, SSD_HEADS, SSD_GROUPS, SSD_DSTATE, SSD_CONV = 1024, 64, 16, 2, 128, 4
SSD_CONV_DIM = SSD_DINNER + 2 * SSD_GROUPS * SSD_DSTATE
FOX_HEADS, FOX_HEAD_DIM = 8, 128
FOX_WIDTH = FOX_HEADS * FOX_HEAD_DIM
SC_DIM, SC_WIDTH = 1024, 3
D_FF, FFN_CONV = 5632, 3
AB_MAIN = 2 * RET_HEADS * RET_DK + 2 * RET_HEADS * RET_DV + SSD_DINNER + SSD_CONV_DIM
AB_PAD = AB_MAIN + 128
CD_MAIN = 3 * FOX_WIDTH + 3 * SC_DIM
CD_PAD = CD_MAIN + 128

LANES = 128
SUBLANES = 8
VMEM_LIMIT = 60 * 1024 * 1024


def _cparams(n_axes):
    return pltpu.CompilerParams(dimension_semantics=("arbitrary",) * n_axes,
                                vmem_limit_bytes=VMEM_LIMIT)


def _rms(xf, w):
    return xf * lax.rsqrt(jnp.mean(xf * xf, axis=-1, keepdims=True) + EPS) * w


def _softplus(x):
    return jnp.maximum(x, 0.0) + jnp.log1p(jnp.exp(-jnp.abs(x)))


def _split3(x):
    hi = x.astype(BF16)
    r1 = x - hi.astype(F32)
    mid = r1.astype(BF16)
    lo = (r1 - mid.astype(F32)).astype(BF16)
    return hi, mid, lo


def _widen(x, n):
    return x[:, 0:n] if n <= LANES else jnp.concatenate([x] * (n // LANES), axis=1)


def _dot(a, b):
    return jnp.dot(a, b, preferred_element_type=F32)


def _dot_nt(a, b):
    return lax.dot_general(a, b, (((1,), (1,)), ((), ())), preferred_element_type=F32)


def _dot_tn(a, b):
    return lax.dot_general(a, b, (((0,), (0,)), ((), ())), preferred_element_type=F32)


def _exact_lhs_dot(m_bf16, x):
    hi, mid, lo = _split3(x)
    return _dot(m_bf16, hi) + _dot(m_bf16, mid) + _dot(m_bf16, lo)


def _exact_rhs_dot(x, m_bf16):
    hi, mid, lo = _split3(x)
    return _dot(hi, m_bf16) + _dot(mid, m_bf16) + _dot(lo, m_bf16)


def _conv_rows(ext_ref, x, prev8, w_ref, width, rows, w_cols=slice(None)):
    ext_ref[0:SUBLANES, :] = prev8
    ext_ref[SUBLANES:SUBLANES + rows, :] = x
    out = None
    for j in range(width):
        off = SUBLANES - (width - 1) + j
        term = ext_ref[off:off + rows, :] * w_ref[j:j + 1, w_cols]
        out = term if out is None else out + term
    return out


def _seq_tiling(seq_len, tile_rows):
    if seq_len >= tile_rows:
        assert seq_len % tile_rows == 0
        return tile_rows, 1, seq_len // tile_rows
    assert tile_rows % seq_len == 0 and seq_len % SUBLANES == 0
    return seq_len, tile_rows // seq_len, 1


def _norm_matmul_kernel(x_ref, nw_ref, w_ref, ws_ref, o_ref, os_ref, h_ref, *, n_tiles, tn, last_w):
    j = pl.program_id(1)

    @pl.when(j == 0)
    def _():
        h_ref[...] = _rms(x_ref[...], nw_ref[...]).astype(BF16)
        os_ref[...] = _dot(h_ref[...], ws_ref[...])

    if last_w == tn:
        o_ref[...] = _dot(h_ref[...], w_ref[...])
    else:
        @pl.when(j < n_tiles - 1)
        def _():
            o_ref[...] = _dot(h_ref[...], w_ref[...])

        @pl.when(j == n_tiles - 1)
        def _():
            o_ref[:, 0:last_w] = _dot(h_ref[...], w_ref[:, 0:last_w])


def _norm_matmul(x, norm_w, w, w_small, *, n, tm, tn):
    m, d = x.shape
    n_tiles = pl.cdiv(n, tn)
    last_w = n - (n_tiles - 1) * tn
    assert m % tm == 0 and last_w % LANES == 0 and n <= w.shape[1] and w_small.shape == (d, LANES)
    return pl.pallas_call(
        functools.partial(_norm_matmul_kernel, n_tiles=n_tiles, tn=tn, last_w=last_w),
        grid=(m // tm, n_tiles),
        in_specs=[pl.BlockSpec((tm, d), lambda i, j: (i, 0)),
                  pl.BlockSpec((1, d), lambda i, j: (0, 0)),
                  pl.BlockSpec((d, tn), lambda i, j: (0, j)),
                  pl.BlockSpec((d, LANES), lambda i, j: (0, 0))],
        out_specs=[pl.BlockSpec((tm, tn), lambda i, j: (i, j)),
                   pl.BlockSpec((tm, LANES), lambda i, j: (i, 0))],
        out_shape=[jax.ShapeDtypeStruct((m, n), F32),
                   jax.ShapeDtypeStruct((m, LANES), F32)],
        scratch_shapes=[pltpu.VMEM((tm, d), BF16)],
        compiler_params=_cparams(2),
        name="norm_in_proj",
    )(x, norm_w.reshape(1, d), w, w_small)


def _proj_res_kernel(x_ref, a_ref, b_ref, wa_ref, wb_ref, o_ref):
    acc = _dot(a_ref[...], wa_ref[...])
    acc = acc + _dot(b_ref[...], wb_ref[...])
    o_ref[...] = x_ref[...] + acc


def _proj_residual(x, a, b, w, *, tm, tn):
    m, d = x.shape
    ka, kb = a.shape[1], b.shape[1]
    assert m % tm == 0 and d % tn == 0 and ka == kb and w.shape == (ka + kb, d)
    return pl.pallas_call(
        _proj_res_kernel,
        grid=(m // tm, d // tn),
        in_specs=[pl.BlockSpec((tm, tn), lambda i, j: (i, j)),
                  pl.BlockSpec((tm, ka), lambda i, j: (i, 0)),
                  pl.BlockSpec((tm, kb), lambda i, j: (i, 0)),
                  pl.BlockSpec((ka, tn), lambda i, j: (0, j)),
                  pl.BlockSpec((kb, tn), lambda i, j: (1, j))],
        out_specs=pl.BlockSpec((tm, tn), lambda i, j: (i, j)),
        out_shape=jax.ShapeDtypeStruct((m, d), F32),
        compiler_params=_cparams(2),
        name="out_proj_residual",
    )(x, a, b, w, w)


def _ffn_kernel(*refs, tm, rows, spt, tpb, nf, ts, nsub, nsub_last, final, n_cast):
    x_ref, nw_ref, wg_ref, wu_ref, wd_ref, cw_ref, cb_ref, hist_ref, fw_ref = refs[:9]
    cast_in = refs[9:9 + n_cast]
    o_ref, tail_ref = refs[9 + n_cast:11 + n_cast]
    cast_out = refs[11 + n_cast:11 + 2 * n_cast]
    h_ref, carry_ref, ext_ref = refs[11 + 2 * n_cast:]
    i = pl.program_id(0)
    f = pl.program_id(1)

    for src_ref, dst_ref in zip(cast_in, cast_out):
        dst_ref[...] = src_ref[...].astype(BF16)

    @pl.when(f == 0)
    def _():
        xf = x_ref[...]
        h_ref[...] = _rms(xf, nw_ref[...]).astype(BF16)
        o_ref[...] = xf

    first = (i % tpb) == 0

    def sub_block(sb):
        cols = slice(sb * ts, (sb + 1) * ts)
        h = h_ref[...]
        a = _dot(h, wg_ref[:, cols])
        u = _dot(h, wu_ref[:, cols])
        convs = []
        for s in range(spt):
            a_s = a[s * rows:(s + 1) * rows]
            if tpb == 1:
                prev = hist_ref[s, :, cols]
            else:
                prev = jnp.where(first, hist_ref[s, :, cols], carry_ref[f * nsub + sb])
            convs.append(_conv_rows(ext_ref, a_s, prev, cw_ref, FFN_CONV, rows, cols))
            tail_ref[s, :, cols] = a_s[rows - SUBLANES:rows]
        if tpb > 1:
            carry_ref[f * nsub + sb] = a[tm - SUBLANES:tm]
        conv = convs[0] if spt == 1 else jnp.concatenate(convs, axis=0)
        act = (jax.nn.silu(conv + cb_ref[:, cols]) * u).astype(BF16)
        o_ref[...] += _dot(act, wd_ref[cols, :])

    if nsub_last == nsub:
        for sb in range(nsub):
            sub_block(sb)
    else:
        @pl.when(f < nf - 1)
        def _():
            for sb in range(nsub):
                sub_block(sb)

        @pl.when(f == nf - 1)
        def _():
            for sb in range(nsub_last):
                sub_block(sb)

    if final:
        @pl.when(f == nf - 1)
        def _():
            o_ref[...] = _rms(o_ref[...], fw_ref[...])


def _conv_ffn(x, norm_w, wg, wu, wd, conv_w, conv_b, hist8, final_w, *, layer, seq_len, tm, tf, ts, final,
              cast_next=None):
    m, d = x.shape
    ff = wg.shape[2]
    assert m % tm == 0 and tf % ts == 0 and ff % ts == 0
    rows, spt, tpb = _seq_tiling(seq_len, tm)
    nm, nf = m // tm, pl.cdiv(ff, tf)
    nsub = tf // ts
    nsub_last = (ff - (nf - 1) * tf) // ts
    hist_map = (lambda i, f: (i // tpb, 0, f)) if spt == 1 else (lambda i, f: (i, 0, f))
    cast_in, cast_specs, cast_src_specs = [], [], []
    if cast_next is not None:
        assert d % nm == 0 and (d // nm) % LANES == 0 and ff % nf == 0 and (ff // nf) % LANES == 0
        dr, fc = d // nm, ff // nf
        cast_layer, cast_in = cast_next[0], list(cast_next[1:])
        cast_specs = [pl.BlockSpec((dr, fc), lambda i, f: (i, f)),
                      pl.BlockSpec((dr, fc), lambda i, f: (i, f)),
                      pl.BlockSpec((fc, dr), lambda i, f: (f, i))]
        cast_src_specs = [pl.BlockSpec((None, dr, fc), lambda i, f: (cast_layer, i, f)),
                          pl.BlockSpec((None, dr, fc), lambda i, f: (cast_layer, i, f)),
                          pl.BlockSpec((None, fc, dr), lambda i, f: (cast_layer, f, i))]
    kern = functools.partial(_ffn_kernel, tm=tm, rows=rows, spt=spt, tpb=tpb, nf=nf, ts=ts, nsub=nsub,
                             nsub_last=nsub_last, final=final, n_cast=len(cast_in))
    results = pl.pallas_call(
        kern,
        grid=(nm, nf),
        in_specs=[pl.BlockSpec((tm, d), lambda i, f: (i, 0)),
                  pl.BlockSpec((1, d), lambda i, f: (0, 0)),
                  pl.BlockSpec((None, d, tf), lambda i, f: (layer, 0, f)),
                  pl.BlockSpec((None, d, tf), lambda i, f: (layer, 0, f)),
                  pl.BlockSpec((None, tf, d), lambda i, f: (layer, f, 0)),
                  pl.BlockSpec((FFN_CONV, tf), lambda i, f: (0, f)),
                  pl.BlockSpec((1, tf), lambda i, f: (0, f)),
                  pl.BlockSpec((spt, SUBLANES, tf), hist_map),
                  pl.BlockSpec((1, d), lambda i, f: (0, 0))] + cast_src_specs,
        out_specs=[pl.BlockSpec((tm, d), lambda i, f: (i, 0)),
                   pl.BlockSpec((spt, SUBLANES, tf), lambda i, f: (i, 0, f))] + cast_specs,
        out_shape=[jax.ShapeDtypeStruct((m, d), F32),
                   jax.ShapeDtypeStruct((nm * spt, SUBLANES, ff), F32)]
                  + [jax.ShapeDtypeStruct(w.shape[1:], BF16) for w in cast_in],
        scratch_shapes=[pltpu.VMEM((tm, d), BF16),
                        pltpu.VMEM((nf * nsub, SUBLANES, ts), F32),
                        pltpu.VMEM((rows + SUBLANES, ts), F32)],
        compiler_params=_cparams(2),
        name="conv_ffn",
    )(x, norm_w.reshape(1, d), wg, wu, wd, conv_w, conv_b.reshape(1, ff), hist8, final_w.reshape(1, d), *cast_in)
    return results[0], results[1], tuple(results[2:])


def _retention_kernel(*refs, c, side_cast):
    q_ref, k_ref, v_ref, g_ref, cos_ref, sin_ref, st_ref, nw_ref = refs[:8]
    y_ref, so_ref = refs[8 + side_cast:10 + side_cast]
    ci = pl.program_id(1)

    if side_cast:
        w = refs[8][...]
        w_main_ref, w_gate_ref = refs[11:13]
        f0 = 3 * FOX_WIDTH
        w_main_ref[:, 0:f0] = w[:, 0:f0].astype(BF16)
        w_main_ref[:, f0:CD_MAIN] = w[:, f0 + FOX_HEADS:CD_MAIN + FOX_HEADS].astype(BF16)
        gate = jnp.concatenate([w[:, f0:f0 + FOX_HEADS], jnp.zeros((w.shape[0], LANES - FOX_HEADS), F32)], axis=1)
        w_gate_ref[...] = gate.astype(BF16)

    @pl.when(ci == 0)
    def _():
        so_ref[...] = st_ref[...]

    cos = cos_ref[...]
    sin = sin_ref[...]
    ii = lax.broadcasted_iota(jnp.int32, (c, c), 0)
    jj = lax.broadcasted_iota(jnp.int32, (c, c), 1)
    diff = (ii - jj).astype(F32)
    causal = ii >= jj
    ridx = lax.broadcasted_iota(jnp.int32, (c, 1), 0).astype(F32)
    for h in range(RET_HEADS):
        lg = math.log1p(-(2.0 ** (-5.0 - h)))
        q = q_ref[:, h * RET_DK:(h + 1) * RET_DK]
        k = k_ref[:, h * RET_DK:(h + 1) * RET_DK]
        v = v_ref[:, h * RET_DV:(h + 1) * RET_DV]
        qr = q * cos + pltpu.roll(q, RET_DK // 2, 1) * sin
        kr = (k * cos + pltpu.roll(k, RET_DK // 2, 1) * sin) * (RET_DK ** -0.5)
        qb = qr.astype(BF16)
        kb = kr.astype(BF16)
        vb = v.astype(BF16)
        decay = jnp.exp(jnp.where(causal, diff * lg, NEG_INF))
        inner = jnp.exp((ridx + 1.0) * lg)
        sdecay = jnp.exp((c - 1.0 - ridx) * lg)
        s = so_ref[0, h]
        scores = _dot_nt(qb, kb) * decay
        y = _dot(scores.astype(BF16), vb)
        y = y + _dot(qb, s.astype(BF16)) * inner
        kd = (kr * sdecay).astype(BF16)
        so_ref[0, h] = math.exp(c * lg) * s + _dot_tn(kd, vb)
        mu = jnp.mean(y, axis=-1, keepdims=True)
        yc = y - mu
        var = jnp.mean(yc * yc, axis=-1, keepdims=True)
        yn = yc * lax.rsqrt(var + EPS) * nw_ref[:, h * RET_DV:(h + 1) * RET_DV]
        g = g_ref[:, h * RET_DV:(h + 1) * RET_DV]
        y_ref[:, h * RET_DV:(h + 1) * RET_DV] = (jax.nn.silu(g) * yn).astype(BF16)


def _retention(proj, cosf, sinf, state, norm_w, *, bsz, seq_len, c, cd_w_in=None):
    m = proj.shape[0]
    nc = seq_len // c
    assert seq_len % c == 0
    qk_w = RET_HEADS * RET_DK
    v_w = RET_HEADS * RET_DV
    row = lambda b, ci: b * nc + ci
    side_in, side_src, side_dst, side_shapes = [], [], [], []
    if cd_w_in is not None and cd_w_in.shape[1] % (bsz * nc * 16) == 0:
        d, wide = cd_w_in.shape[1:]
        rb = d // (bsz * nc)
        side_in = [cd_w_in]
        side_src = [pl.BlockSpec((None, rb, wide), lambda b, ci: (0, row(b, ci), 0))]
        side_dst = [pl.BlockSpec((rb, CD_MAIN), lambda b, ci: (row(b, ci), 0)),
                    pl.BlockSpec((rb, LANES), lambda b, ci: (row(b, ci), 0))]
        side_shapes = [jax.ShapeDtypeStruct((d, CD_MAIN), BF16), jax.ShapeDtypeStruct((d, LANES), BF16)]
    results = pl.pallas_call(
        functools.partial(_retention_kernel, c=c, side_cast=len(side_in)),
        grid=(bsz, nc),
        in_specs=[pl.BlockSpec((c, qk_w), lambda b, ci: (row(b, ci), 0)),
                  pl.BlockSpec((c, qk_w), lambda b, ci: (row(b, ci), 1)),
                  pl.BlockSpec((c, v_w), lambda b, ci: (row(b, ci), 1)),
                  pl.BlockSpec((c, v_w), lambda b, ci: (row(b, ci), 2)),
                  pl.BlockSpec((c, RET_DK), lambda b, ci: (ci, 0)),
                  pl.BlockSpec((c, RET_DK), lambda b, ci: (ci, 0)),
                  pl.BlockSpec((1, RET_HEADS, RET_DK, RET_DV), lambda b, ci: (b, 0, 0, 0)),
                  pl.BlockSpec((1, v_w), lambda b, ci: (0, 0))] + side_src,
        out_specs=[pl.BlockSpec((c, v_w), lambda b, ci: (row(b, ci), 0)),
                   pl.BlockSpec((1, RET_HEADS, RET_DK, RET_DV), lambda b, ci: (b, 0, 0, 0))] + side_dst,
        out_shape=[jax.ShapeDtypeStruct((m, v_w), BF16),
                   jax.ShapeDtypeStruct(state.shape, F32)] + side_shapes,
        compiler_params=_cparams(2),
        name="retention",
    )(proj, proj, proj, proj, cosf, sinf, state, norm_w.reshape(1, v_w), *side_in)
    return results[0], results[1], (tuple(results[2:]) if side_in else None)


def _ssd_kernel(*refs, c, nc, n_cast):
    (z_ref, xs_ref, bc_ref, dt_ref, hx_ref, hbc_ref, st_ref, cwx_ref, cwbc_ref, cbx_ref, cbbc_ref,
     dtb_ref, alog_ref, dsk_ref, nw_ref, tri_ref, exp_ref) = refs[:17]
    cast_in = refs[17:17 + n_cast]
    y_ref, so_ref = refs[17 + n_cast:19 + n_cast]
    cast_out = refs[19 + n_cast:19 + 2 * n_cast]
    st_scr, cx_scr, cbc_scr, extx_scr, extbc_scr, yh_scr, xs_scr = refs[19 + 2 * n_cast:]
    ci = pl.program_id(1)

    for src_ref, dst_ref in zip(cast_in, cast_out):
        dst_ref[...] = src_ref[...].astype(BF16)
    gw = SSD_DINNER // SSD_GROUPS
    hpg = SSD_HEADS // SSD_GROUPS

    @pl.when(ci == 0)
    def _():
        st_scr[...] = st_ref[0].T
        cx_scr[...] = hx_ref[0]
        cbc_scr[...] = hbc_ref[0]

    xs_raw = xs_ref[...]
    bc_raw = bc_ref[...]
    xs_scr[...] = jax.nn.silu(_conv_rows(extx_scr, xs_raw, cx_scr[...], cwx_ref, SSD_CONV, c) + cbx_ref[...])
    bcm = jax.nn.silu(_conv_rows(extbc_scr, bc_raw, cbc_scr[...], cwbc_ref, SSD_CONV, c) + cbbc_ref[...])
    cx_scr[...] = xs_raw[c - SUBLANES:c]
    cbc_scr[...] = bc_raw[c - SUBLANES:c]

    tri = tri_ref[...]
    dt = _softplus(dt_ref[...] + dtb_ref[...])
    a = -jnp.exp(alog_ref[...])
    acs = _exact_lhs_dot(tri, dt * a)
    acs_t = acs.T
    acs_last = acs[c - 1:c, :]
    exp_acs = jnp.exp(acs)
    to_end = jnp.exp(acs_last - acs)
    chunk_dec = jnp.exp(acs_last)

    ii = lax.broadcasted_iota(jnp.int32, (c, c), 0)
    jj = lax.broadcasted_iota(jnp.int32, (c, c), 1)
    causal = ii >= jj
    nb = SSD_GROUPS * SSD_DSTATE
    for g in range(SSD_GROUPS):
        cols = slice(g * gw, (g + 1) * gw)
        expand = exp_ref[:, cols]
        xdt = xs_scr[:, cols] * _exact_rhs_dot(dt, expand)
        xdt_b = xdt.astype(BF16)
        xend_b = (xdt * _exact_rhs_dot(to_end, expand)).astype(BF16)
        b_g = bcm[:, g * SSD_DSTATE:(g + 1) * SSD_DSTATE].astype(BF16)
        c_g = bcm[:, nb + g * SSD_DSTATE:nb + (g + 1) * SSD_DSTATE].astype(BF16)
        cb = _dot_nt(c_g, b_g)
        s_g = st_scr[:, cols]
        y_state = _dot(c_g, s_g.astype(BF16)) * _exact_rhs_dot(exp_acs, expand)
        for r in range(hpg):
            hh = g * hpg + r
            seg = acs[:, hh:hh + 1] - acs_t[hh:hh + 1, :]
            lmat = jnp.exp(jnp.where(causal, seg, NEG_INF))
            mm = (cb * lmat).astype(BF16)
            head = slice(r * SSD_HEADDIM, (r + 1) * SSD_HEADDIM)
            yh_scr[:, hh * SSD_HEADDIM:(hh + 1) * SSD_HEADDIM] = _dot(mm, xdt_b[:, head]) + y_state[:, head]
        upd = _dot_tn(b_g, xend_b)
        st_scr[:, cols] = _exact_rhs_dot(chunk_dec, expand) * s_g + upd

    y = yh_scr[...] + dsk_ref[...] * xs_scr[...]
    z = z_ref[...]
    y_ref[...] = _rms(y * jax.nn.silu(z), nw_ref[...]).astype(BF16)

    @pl.when(ci == nc - 1)
    def _():
        so_ref[0] = st_scr[...].T


def _row_block_cast_specs(w3, layer, n_steps, step_of):
    _, r, cols = w3.shape
    for per in (1, 2, 4, 8):
        if (r * per) % n_steps == 0 and (r * per // n_steps) % 16 == 0:
            rb = r * per // n_steps
            src = pl.BlockSpec((None, rb, cols), lambda *g: (layer, step_of(*g) // per, 0))
            dst = pl.BlockSpec((rb, cols), lambda *g: (step_of(*g) // per, 0))
            return src, dst, jax.ShapeDtypeStruct((r, cols), BF16)
    return None


def _ssd(proj, dt_proj, hist8, state, conv_w, conv_b, dt_bias, a_log, d_skip, norm_w, *, bsz, seq_len, c,
         cast=None):
    m = proj.shape[0]
    nc = seq_len // c
    assert seq_len % c == 0
    row = lambda b, ci: b * nc + ci
    cast_in, cast_src, cast_dst, cast_shapes = [], [], [], []
    if cast is not None:
        specs = [_row_block_cast_specs(w, cast[0], bsz * nc, row) for w in cast[1]]
        if all(s is not None for s in specs):
            cast_in = list(cast[1])
            cast_src, cast_dst, cast_shapes = (list(x) for x in zip(*specs))
    const2 = lambda b, ci: (0, 0)
    di, bcw = SSD_DINNER, 2 * SSD_GROUPS * SSD_DSTATE
    tri = jnp.asarray(np.tril(np.ones((c, c), np.float32)), BF16)
    expand = np.zeros((LANES, di), np.float32)
    for h in range(SSD_HEADS):
        expand[h, h * SSD_HEADDIM:(h + 1) * SSD_HEADDIM] = 1.0
    expand = jnp.asarray(expand, BF16)
    pad_row = lambda v: jnp.pad(v.astype(F32), (0, LANES - v.shape[0])).reshape(1, LANES)
    st2 = state.reshape(bsz, di, SSD_DSTATE)
    results = pl.pallas_call(
        functools.partial(_ssd_kernel, c=c, nc=nc, n_cast=len(cast_in)),
        grid=(bsz, nc),
        in_specs=[pl.BlockSpec((c, di), lambda b, ci: (row(b, ci), 3)),
                  pl.BlockSpec((c, di), lambda b, ci: (row(b, ci), 4)),
                  pl.BlockSpec((c, bcw), lambda b, ci: (row(b, ci), 10)),
                  pl.BlockSpec((c, LANES), lambda b, ci: (row(b, ci), 0)),
                  pl.BlockSpec((1, SUBLANES, di), lambda b, ci: (b, 0, 0)),
                  pl.BlockSpec((1, SUBLANES, bcw), lambda b, ci: (b, 0, 2)),
                  pl.BlockSpec((1, di, SSD_DSTATE), lambda b, ci: (b, 0, 0)),
                  pl.BlockSpec((SSD_CONV, di), const2),
                  pl.BlockSpec((SSD_CONV, bcw), lambda b, ci: (0, 2)),
                  pl.BlockSpec((1, di), const2),
                  pl.BlockSpec((1, bcw), lambda b, ci: (0, 2)),
                  pl.BlockSpec((1, LANES), const2),
                  pl.BlockSpec((1, LANES), const2),
                  pl.BlockSpec((1, di), const2),
                  pl.BlockSpec((1, di), const2),
                  pl.BlockSpec((c, c), const2),
                  pl.BlockSpec((LANES, di), const2)] + cast_src,
        out_specs=[pl.BlockSpec((c, di), lambda b, ci: (row(b, ci), 0)),
                   pl.BlockSpec((1, di, SSD_DSTATE), lambda b, ci: (b, 0, 0))] + cast_dst,
        out_shape=[jax.ShapeDtypeStruct((m, di), BF16),
                   jax.ShapeDtypeStruct(st2.shape, F32)] + cast_shapes,
        scratch_shapes=[pltpu.VMEM((SSD_DSTATE, di), F32),
                        pltpu.VMEM((SUBLANES, di), F32),
                        pltpu.VMEM((SUBLANES, bcw), F32),
                        pltpu.VMEM((c + SUBLANES, di), F32),
                        pltpu.VMEM((c + SUBLANES, bcw), F32),
                        pltpu.VMEM((c, di), F32),
                        pltpu.VMEM((c, di), F32)],
        compiler_params=_cparams(2),
        name="ssd",
    )(proj, proj, proj, dt_proj, hist8, hist8, st2,
      conv_w, conv_w, conv_b.reshape(1, -1), conv_b.reshape(1, -1),
      pad_row(dt_bias), pad_row(a_log), jnp.repeat(d_skip.astype(F32), SSD_HEADDIM).reshape(1, di),
      norm_w.reshape(1, di), tri, expand, *cast_in)
    return results[0], results[1].reshape(state.shape), (list(results[2:]) if cast_in else None)


def _decode_cum_kernel(lfc_ref, fl_ref, b_ref, tri_ref, cumt_c_ref, lfn_ref, cumn_ref, cumt_n_ref,
                       carry_ref, *, c, ncb):
    j = pl.program_id(0)

    @pl.when(j == 0)
    def _():
        carry_ref[...] = jnp.zeros_like(carry_ref)

    @pl.when(j < ncb)
    def _():
        cum = _exact_lhs_dot(tri_ref[...], lfc_ref[...]) + carry_ref[...]
        carry_ref[...] = cum[c - 1:c, :]
        cumt_c_ref[...] = cum.T

    @pl.when(j == ncb)
    def _():
        lf = -_softplus(-(fl_ref[...] + b_ref[...]))
        lfn_ref[...] = lf
        cum = _exact_lhs_dot(tri_ref[0:LANES, 0:LANES], lf) + carry_ref[...]
        cumn_ref[...] = cum
        cumt_n_ref[...] = cum.T


def _decode_cum(cache_lf, fl_new, bias, *, c):
    past = cache_lf.shape[0]
    assert past % c == 0 and c % LANES == 0 and fl_new.shape == (LANES, LANES)
    ncb = past // c
    tri = jnp.asarray(np.tril(np.ones((c, c), np.float32)), BF16)
    blk = lambda j: jnp.minimum(j, ncb - 1)
    sq = jax.ShapeDtypeStruct((LANES, LANES), F32)
    return pl.pallas_call(
        functools.partial(_decode_cum_kernel, c=c, ncb=ncb),
        grid=(ncb + 1,),
        in_specs=[pl.BlockSpec((c, LANES), lambda j: (blk(j), 0)),
                  pl.BlockSpec((LANES, LANES), lambda j: (0, 0)),
                  pl.BlockSpec((1, LANES), lambda j: (0, 0)),
                  pl.BlockSpec((c, c), lambda j: (0, 0))],
        out_specs=[pl.BlockSpec((LANES, c), lambda j: (0, blk(j))),
                   pl.BlockSpec((LANES, LANES), lambda j: (0, 0)),
                   pl.BlockSpec((LANES, LANES), lambda j: (0, 0)),
                   pl.BlockSpec((LANES, LANES), lambda j: (0, 0))],
        out_shape=[jax.ShapeDtypeStruct((LANES, past), F32), sq, sq, sq],
        scratch_shapes=[pltpu.VMEM((1, LANES), F32)],
        compiler_params=_cparams(1),
        name="decode_logf_cumsum",
    )(cache_lf, fl_new, bias, tri)


FOX_AUG = 2 * FOX_HEAD_DIM
N_BIAS_PIECES = 3


def _fox_prep_kernel(q_ref, k_ref, v_ref, fl_ref, fb_ref, tri_ref, place_ref, ones_ref,
                     qa_ref, ka_ref, vb_ref, k32_ref, v32_ref, lf_ref, carry_ref, *, tp):
    @pl.when(pl.program_id(1) == 0)
    def _():
        carry_ref[...] = jnp.zeros_like(carry_ref)

    lf = -_softplus(-(fl_ref[...] + fb_ref[...]))
    lf_ref[...] = lf
    cum = _exact_lhs_dot(tri_ref[...], lf) + carry_ref[...]
    carry_ref[...] = cum[tp - 1:tp, :]
    pieces = _split3(cum * (FOX_HEAD_DIM ** 0.5))
    n = N_BIAS_PIECES
    aug_q = ones_ref[0:1, :] + sum(_dot(pieces[r], place_ref[r]) for r in range(n))
    aug_k = ones_ref[1:2, :] - sum(_dot(pieces[r], place_ref[n + r]) for r in range(n))
    for h in range(FOX_HEADS):
        src = slice(h * FOX_HEAD_DIM, (h + 1) * FOX_HEAD_DIM)
        feat = slice(h * FOX_AUG, h * FOX_AUG + FOX_HEAD_DIM)
        bias = slice(h * FOX_AUG + FOX_HEAD_DIM, (h + 1) * FOX_AUG)
        qa_ref[:, feat] = q_ref[:, src].astype(BF16)
        qa_ref[:, bias] = aug_q[:, src].astype(BF16)
        ka_ref[:, feat] = k_ref[:, src].astype(BF16)
        ka_ref[:, bias] = aug_k[:, src].astype(BF16)
    k = k_ref[...]
    v = v_ref[...]
    k32_ref[...] = pltpu.einshape("m(hd)->mhd", k, h=FOX_HEADS)
    v32_ref[...] = pltpu.einshape("m(hd)->mhd", v, h=FOX_HEADS)
    vb_ref[...] = v.astype(BF16)


def _fox_prep(proj, fl_proj, f_bias, *, bsz, seq_len, tp):
    m = proj.shape[0]
    nt = seq_len // tp
    assert seq_len % tp == 0
    w = FOX_WIDTH
    tri = jnp.asarray(np.tril(np.ones((tp, tp), np.float32)), BF16)
    n = N_BIAS_PIECES
    place = np.zeros((2 * n, LANES, w), np.float32)
    ones = np.zeros((SUBLANES, w), np.float32)
    for h in range(FOX_HEADS):
        for r in range(2 * n):
            place[r, h, h * FOX_HEAD_DIM + r] = 1.0
        ones[0, h * FOX_HEAD_DIM + n:h * FOX_HEAD_DIM + 2 * n] = 1.0
        ones[1, h * FOX_HEAD_DIM:h * FOX_HEAD_DIM + n] = 1.0
    row = lambda b, ti: (b * nt + ti, 0)
    const2 = lambda b, ti: (0, 0)
    return pl.pallas_call(
        functools.partial(_fox_prep_kernel, tp=tp),
        grid=(bsz, nt),
        in_specs=[pl.BlockSpec((tp, w), lambda b, ti: (b * nt + ti, 0)),
                  pl.BlockSpec((tp, w), lambda b, ti: (b * nt + ti, 1)),
                  pl.BlockSpec((tp, w), lambda b, ti: (b * nt + ti, 2)),
                  pl.BlockSpec((tp, LANES), row),
                  pl.BlockSpec((1, LANES), const2),
                  pl.BlockSpec((tp, tp), const2),
                  pl.BlockSpec((2 * n, LANES, w), lambda b, ti: (0, 0, 0)),
                  pl.BlockSpec((SUBLANES, w), const2)],
        out_specs=[pl.BlockSpec((tp, FOX_HEADS * FOX_AUG), row),
                   pl.BlockSpec((tp, FOX_HEADS * FOX_AUG), row),
                   pl.BlockSpec((tp, w), row),
                   pl.BlockSpec((tp, FOX_HEADS, FOX_HEAD_DIM), lambda b, ti: (b * nt + ti, 0, 0)),
                   pl.BlockSpec((tp, FOX_HEADS, FOX_HEAD_DIM), lambda b, ti: (b * nt + ti, 0, 0)),
                   pl.BlockSpec((tp, LANES), row)],
        out_shape=[jax.ShapeDtypeStruct((m, FOX_HEADS * FOX_AUG), BF16),
                   jax.ShapeDtypeStruct((m, FOX_HEADS * FOX_AUG), BF16),
                   jax.ShapeDtypeStruct((m, w), BF16),
                   jax.ShapeDtypeStruct((m, FOX_HEADS, FOX_HEAD_DIM), F32),
                   jax.ShapeDtypeStruct((m, FOX_HEADS, FOX_HEAD_DIM), F32),
                   jax.ShapeDtypeStruct((m, LANES), F32)],
        scratch_shapes=[pltpu.VMEM((1, LANES), F32)],
        compiler_params=_cparams(2),
        name="fox_prep",
    )(proj, proj, proj, fl_proj, f_bias, tri, jnp.asarray(place, BF16), jnp.asarray(ones, F32))


def _fox_kernel(qi_ref, ki_ref, q_ref, k_ref, v_ref, o_ref, m_ref, acc_ref, va_ref, *, t, ts, ts_diag):
    step = pl.program_id(2)
    qi = qi_ref[step]
    ki = ki_ref[step]
    to_log2 = (FOX_HEAD_DIM ** -0.5) * math.log2(math.e)
    hd = FOX_HEAD_DIM

    @pl.when(ki == 0)
    def _():
        m_ref[...] = jnp.full_like(m_ref, NEG_INF)
        acc_ref[...] = jnp.zeros_like(acc_ref)
        va_ref[:, hd:2 * hd] = jnp.ones((t, hd), BF16)

    va_ref[:, 0:hd] = v_ref[...]

    def update(diagonal):
        tc = ts_diag if diagonal else ts
        for r in range(t // tc):
            rows = slice(r * tc, (r + 1) * tc)
            nk = (r + 1) * tc if diagonal else t
            s = _dot_nt(q_ref[rows, :], k_ref[0:nk, :])
            if diagonal:
                ri = lax.broadcasted_iota(jnp.int32, (tc, nk), 0) + r * tc
                ci = lax.broadcasted_iota(jnp.int32, (tc, nk), 1)
                s = jnp.where(ci <= ri, s, NEG_INF)
            m_old = m_ref[rows, :]
            m_new = jnp.maximum(m_old, jnp.max(s, axis=1, keepdims=True))
            m_ref[rows, :] = m_new
            alpha = jnp.exp2((m_old - m_new) * to_log2)
            p = jnp.exp2((s - _widen(m_new, nk)) * to_log2)
            acc_ref[rows, :] = _widen(alpha, 2 * hd) * acc_ref[rows, :] + _dot(p.astype(BF16), va_ref[0:nk, :])

    @pl.when(ki < qi)
    def _():
        update(False)

    @pl.when(ki == qi)
    def _():
        update(True)
        o_ref[...] = (acc_ref[:, 0:hd] / acc_ref[:, hd:2 * hd]).astype(BF16)


def _fox_prompt(qa, ka, vb, *, bsz, seq_len, t, ts, ts_diag):
    m = qa.shape[0]
    nq = seq_len // t
    assert seq_len % t == 0 and t % ts == 0 and t % ts_diag == 0
    pairs = [(qi, ki) for qi in range(nq) for ki in range(qi + 1)]
    qi_tab = jnp.asarray([p[0] for p in pairs], jnp.int32)
    ki_tab = jnp.asarray([p[1] for p in pairs], jnp.int32)
    grid_spec = pltpu.PrefetchScalarGridSpec(
        num_scalar_prefetch=2,
        grid=(bsz, FOX_HEADS, len(pairs)),
        in_specs=[pl.BlockSpec((t, FOX_AUG), lambda b, h, s, qi, ki: (b * nq + qi[s], h)),
                  pl.BlockSpec((t, FOX_AUG), lambda b, h, s, qi, ki: (b * nq + ki[s], h)),
                  pl.BlockSpec((t, FOX_HEAD_DIM), lambda b, h, s, qi, ki: (b * nq + ki[s], h))],
        out_specs=pl.BlockSpec((t, FOX_HEAD_DIM), lambda b, h, s, qi, ki: (b * nq + qi[s], h)),
        scratch_shapes=[pltpu.VMEM((t, LANES), F32), pltpu.VMEM((t, 2 * FOX_HEAD_DIM), F32),
                        pltpu.VMEM((t, 2 * FOX_HEAD_DIM), BF16)],
    )
    return pl.pallas_call(
        functools.partial(_fox_kernel, t=t, ts=ts, ts_diag=ts_diag),
        grid_spec=grid_spec,
        out_shape=jax.ShapeDtypeStruct((m, FOX_WIDTH), BF16),
        compiler_params=_cparams(3),
        name="fox_attention",
    )(qi_tab, ki_tab, qa, ka, vb)


def _fox_decode_kernel(q_ref, kn_ref, vn_ref, kc_ref, vc_ref, cq_ref, ckc_ref, ckn_ref, o_ref,
                       m_ref, l_ref, acc_ref, *, lq, ncb):
    j = pl.program_id(1)
    nh, hd = FOX_HEADS, FOX_HEAD_DIM

    @pl.when(j == 0)
    def _():
        m_ref[...] = jnp.full_like(m_ref, NEG_INF)
        l_ref[...] = jnp.zeros_like(l_ref)
        acc_ref[...] = jnp.zeros_like(acc_ref)

    def attend(k_head, v_head, ck_head, causal):
        for h in range(nh):
            qh = q_ref[:, h * hd:(h + 1) * hd].astype(BF16)
            ck = ck_head(h)
            tk = ck.shape[1]
            s = _dot_nt(qh, k_head(h).astype(BF16)) * (hd ** -0.5)
            s = s + (_widen(cq_ref[h], tk) - ck)
            if causal:
                rows = lax.broadcasted_iota(jnp.int32, (lq, tk), 0)
                cols = lax.broadcasted_iota(jnp.int32, (lq, tk), 1)
                s = jnp.where(cols <= rows, s, NEG_INF)
            m_old = m_ref[h]
            m_new = jnp.maximum(m_old, jnp.max(s, axis=1, keepdims=True))
            alpha = jnp.exp(m_old - m_new)
            p = jnp.exp(s - _widen(m_new, tk))
            l_ref[h] = alpha * l_ref[h] + jnp.sum(p, axis=1, keepdims=True)
            acc_ref[h] = alpha * acc_ref[h] + _dot(p.astype(BF16), v_head(h).astype(BF16))
            m_ref[h] = m_new

    @pl.when(j < ncb)
    def _():
        k_hm = pltpu.einshape("mhd->hmd", kc_ref[0])
        v_hm = pltpu.einshape("mhd->hmd", vc_ref[0])
        attend(lambda h: k_hm[h], lambda h: v_hm[h], lambda h: ckc_ref[h], False)

    @pl.when(j == ncb)
    def _():
        attend(lambda h: kn_ref[:, h * hd:(h + 1) * hd], lambda h: vn_ref[:, h * hd:(h + 1) * hd],
               lambda h: ckn_ref[h][:, 0:lq], True)
        for h in range(nh):
            o_ref[:, h * hd:(h + 1) * hd] = (acc_ref[h] / l_ref[h]).astype(BF16)


def _fox_decode(proj, cache_k, cache_v, cq, ck_cache, ck_new, *, bsz, lq, tk):
    past = cache_k.shape[1]
    assert past % tk == 0 and tk % LANES == 0 and lq <= LANES
    ncb = past // tk
    nh, hd, w = FOX_HEADS, FOX_HEAD_DIM, FOX_WIDTH
    tile = lambda j: jnp.minimum(j, ncb - 1)
    cache_spec = pl.BlockSpec((1, tk, nh, hd), lambda b, j: (b, tile(j), 0, 0))
    return pl.pallas_call(
        functools.partial(_fox_decode_kernel, lq=lq, ncb=ncb),
        grid=(bsz, ncb + 1),
        in_specs=[pl.BlockSpec((lq, w), lambda b, j: (b, 0)),
                  pl.BlockSpec((lq, w), lambda b, j: (b, 1)),
                  pl.BlockSpec((lq, w), lambda b, j: (b, 2)),
                  cache_spec,
                  cache_spec,
                  pl.BlockSpec((nh, lq, LANES), lambda b, j: (b, 0, 0)),
                  pl.BlockSpec((nh, 1, tk), lambda b, j: (b, 0, tile(j))),
                  pl.BlockSpec((nh, 1, LANES), lambda b, j: (b, 0, 0))],
        out_specs=pl.BlockSpec((lq, w), lambda b, j: (b, 0)),
        out_shape=jax.ShapeDtypeStruct((bsz * lq, w), BF16),
        scratch_shapes=[pltpu.VMEM((nh, lq, LANES), F32), pltpu.VMEM((nh, lq, LANES), F32),
                        pltpu.VMEM((nh, lq, hd), F32)],
        compiler_params=_cparams(2),
        name="fox_decode",
    )(proj, proj, proj, cache_k, cache_v, cq, ck_cache, ck_new)


def _sconv_kernel(u_ref, bg_ref, cg_ref, cw_ref, hist_ref, y_ref, tail_ref, carry_ref, ext_ref,
                  *, tm, rows, spt, tpb):
    i = pl.program_id(0)
    w = cg_ref[...] * u_ref[...]
    first = (i % tpb) == 0
    for s in range(spt):
        w_s = w[s * rows:(s + 1) * rows]
        if tpb == 1:
            prev = hist_ref[s]
        else:
            prev = jnp.where(first, hist_ref[s], carry_ref[...])
        conv = _conv_rows(ext_ref, w_s, prev, cw_ref, SC_WIDTH, rows)
        y_ref[s * rows:(s + 1) * rows, :] = (bg_ref[s * rows:(s + 1) * rows, :] * conv).astype(BF16)
        tail_ref[s] = w_s[rows - SUBLANES:rows]
    if tpb > 1:
        carry_ref[...] = w[tm - SUBLANES:tm]


def _sconv(proj, conv_w, hist8, *, seq_len, tm):
    m = proj.shape[0]
    assert m % tm == 0
    rows, spt, tpb = _seq_tiling(seq_len, tm)
    nm = m // tm
    hist_map = (lambda i: (i // tpb, 0, 0)) if spt == 1 else (lambda i: (i, 0, 0))
    base = 3 * FOX_WIDTH // SC_DIM
    return pl.pallas_call(
        functools.partial(_sconv_kernel, tm=tm, rows=rows, spt=spt, tpb=tpb),
        grid=(nm,),
        in_specs=[pl.BlockSpec((tm, SC_DIM), lambda i: (i, base)),
                  pl.BlockSpec((tm, SC_DIM), lambda i: (i, base + 1)),
                  pl.BlockSpec((tm, SC_DIM), lambda i: (i, base + 2)),
                  pl.BlockSpec((SC_WIDTH, SC_DIM), lambda i: (0, 0)),
                  pl.BlockSpec((spt, SUBLANES, SC_DIM), hist_map)],
        out_specs=[pl.BlockSpec((tm, SC_DIM), lambda i: (i, 0)),
                   pl.BlockSpec((spt, SUBLANES, SC_DIM), lambda i: (i, 0, 0))],
        out_shape=[jax.ShapeDtypeStruct((m, SC_DIM), BF16),
                   jax.ShapeDtypeStruct((nm * spt, SUBLANES, SC_DIM), F32)],
        scratch_shapes=[pltpu.VMEM((SUBLANES, SC_DIM), F32),
                        pltpu.VMEM((rows + SUBLANES, SC_DIM), F32)],
        compiler_params=_cparams(1),
        name="gated_short_conv",
    )(proj, proj, proj, conv_w, hist8)


def _hist8(state):
    n, w1, c = state.shape
    return jnp.concatenate([jnp.zeros((n, SUBLANES - w1, c), F32), state.astype(F32)], axis=1)


def _tails(tails, n_seq, seq_len, tile_rows, keep):
    per_seq = max(1, seq_len // tile_rows)
    pieces = tails.reshape(n_seq, per_seq, SUBLANES, tails.shape[-1])
    return pieces[:, per_seq - 1, SUBLANES - keep:, :]


def _rope_tables(pos0, length):
    half = RET_DK // 2
    inv = ROPE_BASE ** (-np.arange(half, dtype=np.float64) / half)
    ang = (pos0 + np.arange(length, dtype=np.float64))[:, None] * inv[None, :]
    cos, sin = np.cos(ang), np.sin(ang)
    return (jnp.asarray(np.concatenate([cos, cos], axis=1), F32),
            jnp.asarray(np.concatenate([-sin, sin], axis=1), F32))


def _pad_cols_bf16(w):
    return jnp.pad(w, ((0, 0), (0, LANES - w.shape[1]))).astype(BF16)


def _cd_weights_xla(cd_in):
    f0 = 3 * FOX_WIDTH
    main = jnp.concatenate([cd_in[:, :f0], cd_in[:, f0 + FOX_HEADS:]], axis=1).astype(BF16)
    return main, _pad_cols_bf16(cd_in[:, f0:f0 + FOX_HEADS])


def _prep_weights(p):
    ab_in = p['ab_w_in'][0]
    return dict(
        ab_in=ab_in.astype(BF16),
        ab_small=_pad_cols_bf16(ab_in[:, AB_MAIN:]),
        cd_in=None,
        cd_small=None,
        ab_out=p['ab_w_out'][0].astype(BF16),
        cd_out=p['cd_w_out'][0].astype(BF16),
        ffn0=None,
        ffn1=None,
    )


def _trunk(x, pos0, st_ret, st_ssd, st_ssd_conv, c_k, c_v, c_logf, st_sconv, st_ffn, p, wb, t):
    bsz, length, d = x.shape
    m = bsz * length
    xf = x.reshape(m, d)
    zeros = lambda *shape: jnp.zeros(shape, F32)

    proj, dt_proj = _norm_matmul(xf, p['ab_norm_w'][0], wb['ab_in'], wb['ab_small'],
                                 n=AB_MAIN, tm=t['tm_ab'], tn=t['tn_ab'])
    cosf, sinf = _rope_tables(pos0, length)
    ret_state = zeros(bsz, RET_HEADS, RET_DK, RET_DV) if st_ret is None else st_ret
    y_ret, ret_new, cd_bf16 = _retention(proj, cosf, sinf, ret_state, p['ret_norm_w'][0],
                                         bsz=bsz, seq_len=length, c=t['c_ret'],
                                         cd_w_in=p['cd_w_in'] if wb['cd_in'] is None else None)
    if wb['cd_in'] is None:
        wb['cd_in'], wb['cd_small'] = cd_bf16 if cd_bf16 is not None else _cd_weights_xla(p['cd_w_in'][0])
    ssd_state = zeros(bsz, SSD_HEADS, SSD_HEADDIM, SSD_DSTATE) if st_ssd is None else st_ssd
    ssd_hist = zeros(bsz, SSD_CONV - 1, SSD_CONV_DIM) if st_ssd_conv is None else st_ssd_conv
    ffn_f32 = [p['ffn_w_gate'], p['ffn_w_up'], p['ffn_w_down']]
    y_ssd, ssd_new, ffn0_bf16 = _ssd(proj, dt_proj, _hist8(ssd_hist), ssd_state, p['ssd_conv_w'][0],
                                     p['ssd_conv_b'][0], p['ssd_dt_bias'][0], p['ssd_A_log'][0], p['ssd_D'][0],
                                     p['ssd_norm_w'][0], bsz=bsz, seq_len=length, c=t['c_ssd'],
                                     cast=(0, ffn_f32) if wb['ffn0'] is None else None)
    if wb['ffn0'] is None:
        wb['ffn0'] = (tuple(w[None] for w in ffn0_bf16) if ffn0_bf16 is not None
                      else tuple(w[0:1].astype(BF16) for w in ffn_f32))
    xbc_lo = AB_MAIN - SSD_CONV_DIM
    ssd_conv_new = proj.reshape(bsz, length, -1)[:, length - (SSD_CONV - 1):, xbc_lo:AB_MAIN]
    xf = _proj_residual(xf, y_ret, y_ssd, wb['ab_out'], tm=t['tm_out'], tn=t['tn_out'])

    ffn_new = []
    ffn_hist0 = zeros(bsz, FFN_CONV - 1, D_FF) if st_ffn is None else st_ffn[0]
    next_f32 = (1, p['ffn_w_gate'], p['ffn_w_up'], p['ffn_w_down'])
    n_row_tiles = m // t['tm_ffn']
    if wb['ffn1'] is None and not (d % (n_row_tiles * LANES) == 0 and D_FF % t['tf_ffn'] == 0):
        wb['ffn1'] = tuple(w[1:2].astype(BF16) for w in next_f32[1:])
    xf, tails, next_bf16 = _conv_ffn(
        xf, p['ffn_norm_w'][0], *wb['ffn0'], p['ffn_conv_w'][0], p['ffn_conv_b'][0], _hist8(ffn_hist0),
        p['final_norm_w'], layer=0, seq_len=length, tm=t['tm_ffn'], tf=t['tf_ffn'], ts=t['ts_ffn'], final=False,
        cast_next=next_f32 if wb['ffn1'] is None else None)
    if wb['ffn1'] is None:
        wb['ffn1'] = tuple(w[None] for w in next_bf16)
    ffn_new.append(_tails(tails, bsz, length, t['tm_ffn'], FFN_CONV - 1))

    proj, fl_proj = _norm_matmul(xf, p['cd_norm_w'][0], wb['cd_in'], wb['cd_small'],
                                 n=CD_MAIN, tm=t['tm_cd'], tn=t['tn_cd'])
    f_bias = jnp.pad(p['fox_f_bias'][0].astype(F32), (0, LANES - FOX_HEADS)).reshape(1, LANES)
    head_shape = (bsz, length, FOX_HEADS, FOX_HEAD_DIM)
    if c_k is None:
        qa, ka, vb, k32, v32, logf = _fox_prep(proj, fl_proj, f_bias, bsz=bsz, seq_len=length, tp=t['t_prep'])
        y_fox = _fox_prompt(qa, ka, vb, bsz=bsz, seq_len=length, t=t['t_fox'], ts=t['ts_fox'],
                            ts_diag=t['ts_fox_diag'])
        logf_new = logf.reshape(bsz, length, LANES)[:, :, :FOX_HEADS]
        k_new, v_new = k32.reshape(head_shape), v32.reshape(head_shape)
    else:
        proj3 = proj.reshape(bsz, length, -1)
        k_new = proj3[:, :, FOX_WIDTH:2 * FOX_WIDTH].reshape(head_shape)
        v_new = proj3[:, :, 2 * FOX_WIDTH:3 * FOX_WIDTH].reshape(head_shape)
        past = c_k.shape[1]
        pairs = bsz * FOX_HEADS
        assert pairs <= LANES and length <= LANES
        to_lanes = lambda a, rows: jnp.pad(jnp.swapaxes(a, 0, 1).reshape(a.shape[1], pairs),
                                           ((0, rows - a.shape[1]), (0, LANES - pairs)))
        from_lanes = lambda a: jnp.swapaxes(a[:length, :pairs].reshape(length, bsz, FOX_HEADS), 0, 1)
        cache_lf = to_lanes(c_logf.astype(F32), past)
        fl_rows = to_lanes(fl_proj.reshape(bsz, length, LANES)[:, :, :FOX_HEADS], LANES)
        bias_lanes = jnp.pad(jnp.tile(p['fox_f_bias'][0].astype(F32), bsz), (0, LANES - pairs)).reshape(1, LANES)
        cum_t_cache, lf_rows, cum_rows, cum_t_new = _decode_cum(cache_lf, fl_rows, bias_lanes, c=t['c_cum'])
        logf_new = from_lanes(lf_rows)
        cq = jnp.broadcast_to(cum_rows[:length, :pairs].T[:, :, None], (pairs, length, LANES))
        y_fox = _fox_decode(proj, c_k, c_v, cq, cum_t_cache[:pairs, None, :], cum_t_new[:pairs, None, :],
                            bsz=bsz, lq=length, tk=t['tk_dec'])
    sc_hist = zeros(bsz, SC_WIDTH - 1, SC_DIM) if st_sconv is None else st_sconv
    y_sc, sc_tails = _sconv(proj, p['sconv_w'][0], _hist8(sc_hist), seq_len=length, tm=t['tm_sc'])
    sconv_new = _tails(sc_tails, bsz, length, t['tm_sc'], SC_WIDTH - 1)
    xf = _proj_residual(xf, y_fox, y_sc, wb['cd_out'], tm=t['tm_out'], tn=t['tn_out'])

    ffn_hist1 = zeros(bsz, FFN_CONV - 1, D_FF) if st_ffn is None else st_ffn[1]
    xf, tails, _ = _conv_ffn(xf, p['ffn_norm_w'][1], *wb['ffn1'], p['ffn_conv_w'][1], p['ffn_conv_b'][1],
                             _hist8(ffn_hist1), p['final_norm_w'], layer=0, seq_len=length, tm=t['tm_ffn'],
                             tf=t['tf_ffn'], ts=t['ts_ffn'], final=True)
    ffn_new.append(_tails(tails, bsz, length, t['tm_ffn'], FFN_CONV - 1))

    return (xf.reshape(bsz, length, d), ret_new[None], ssd_new[None], ssd_conv_new[None], k_new[None],
            v_new[None], logf_new[None], sconv_new[None], jnp.stack(ffn_new))


def _largest_divisor(n, cap, multiple=1):
    best = None
    for cand in range(multiple, min(n, cap) + 1, multiple):
        if n % cand == 0:
            best = cand
    assert best is not None, (n, cap, multiple)
    return best


def _tiles(bsz, length, past=None):
    m = bsz * length
    seq_tile = lambda cap: _largest_divisor(length, cap, SUBLANES)
    row_tile = lambda cap: (_largest_divisor(length, cap, SUBLANES) if length >= cap
                            else _largest_divisor(m, cap, length))
    t = dict(
        tm_ab=row_tile(1024), tn_ab=2048, tm_cd=row_tile(1024), tn_cd=2048,
        tm_out=row_tile(1024), tn_out=2048,
        tm_ffn=row_tile(1024), tf_ffn=512, ts_ffn=512,
        tm_sc=row_tile(1024),
        c_ret=seq_tile(256), c_ssd=seq_tile(256),
    )
    if past is None:
        t['t_fox'] = seq_tile(2048)
        t['ts_fox'] = _largest_divisor(t['t_fox'], 256, LANES)
        t['ts_fox_diag'] = _largest_divisor(t['t_fox'], 512, LANES)
        t['t_prep'] = seq_tile(512)
    else:
        t['tk_dec'] = _largest_divisor(past, 1024, LANES)
        t['c_cum'] = _largest_divisor(past, 256, LANES)
    return t


def kernel(x_prompt, x_sample, state_ret, state_ssd, state_ssd_conv, cache_fox_k, cache_fox_v, cache_fox_logf, state_sconv, state_ffn_conv, ab_norm_w, ab_w_in, ret_norm_w, ssd_conv_w, ssd_conv_b, ssd_dt_bias, ssd_A_log, ssd_D, ssd_norm_w, ab_w_out, cd_norm_w, cd_w_in, fox_f_bias, sconv_w, cd_w_out, ffn_norm_w, ffn_w_gate, ffn_w_up, ffn_conv_w, ffn_conv_b, ffn_w_down, final_norm_w):
    p = dict(ab_norm_w=ab_norm_w, ab_w_in=ab_w_in, ret_norm_w=ret_norm_w, ssd_conv_w=ssd_conv_w,
             ssd_conv_b=ssd_conv_b, ssd_dt_bias=ssd_dt_bias, ssd_A_log=ssd_A_log, ssd_D=ssd_D,
             ssd_norm_w=ssd_norm_w, ab_w_out=ab_w_out, cd_norm_w=cd_norm_w, cd_w_in=cd_w_in,
             fox_f_bias=fox_f_bias, sconv_w=sconv_w, cd_w_out=cd_w_out, ffn_norm_w=ffn_norm_w,
             ffn_w_gate=ffn_w_gate, ffn_w_up=ffn_w_up, ffn_conv_w=ffn_conv_w, ffn_conv_b=ffn_conv_b,
             ffn_w_down=ffn_w_down, final_norm_w=final_norm_w)
    assert x_prompt.shape[-1] == D_MODEL and ab_w_in.shape == (1, D_MODEL, AB_MAIN + SSD_HEADS)
    assert cd_w_in.shape == (1, D_MODEL, CD_MAIN + FOX_HEADS) and ffn_w_gate.shape == (2, D_MODEL, D_FF)
    wb = _prep_weights(p)
    bp, lp_, _ = x_prompt.shape
    bs, ls, _ = x_sample.shape
    past = cache_fox_k.shape[2]
    (y_prompt, p_ret, p_ssd, p_ssd_conv, p_fox_k, p_fox_v, p_fox_logf, p_sconv, p_ffn_conv) = _trunk(
        x_prompt, 0, None, None, None, None, None, None, None, None, p, wb, _tiles(bp, lp_))
    (y_sample, s_ret, s_ssd, s_ssd_conv, s_fox_k, s_fox_v, s_fox_logf, s_sconv, s_ffn_conv) = _trunk(
        x_sample, past, state_ret[0], state_ssd[0], state_ssd_conv[0], cache_fox_k[0], cache_fox_v[0],
        cache_fox_logf[0], state_sconv[0], state_ffn_conv, p, wb, _tiles(bs, ls, past))
    return (y_prompt, y_sample, p_ret, s_ret, p_ssd, s_ssd, p_ssd_conv, s_ssd_conv, p_fox_k, s_fox_k,
            p_fox_v, s_fox_v, p_fox_logf, s_fox_logf, p_sconv, s_sconv, p_ffn_conv, s_ffn_conv)
```

```python
import functools
import math

import numpy as np
import jax
import jax.numpy as jnp
from jax import lax
from jax.experimental import pallas as pl
from jax.experimental.pallas import tpu as pltpu

F32 = jnp.float32
BF16 = jnp.bfloat16
EPS = 1e-6
ROPE_BASE = 10000.0
NEG_INF = float("-inf")

D_MODEL = 2048
RET_HEADS, RET_DK, RET_DV = 4, 128, 256
SSD_DINNER, SSD_HEADDIM, SSD_HEADS, SSD_GROUPS, SSD_DSTATE, SSD_CONV = 1024, 64, 16, 2, 128, 4
SSD_CONV_DIM = SSD_DINNER + 2 * SSD_GROUPS * SSD_DSTATE
FOX_HEADS, FOX_HEAD_DIM = 8, 128
FOX_WIDTH = FOX_HEADS * FOX_HEAD_DIM
SC_DIM, SC_WIDTH = 1024, 3
D_FF, FFN_CONV = 5632, 3
AB_MAIN = 2 * RET_HEADS * RET_DK + 2 * RET_HEADS * RET_DV + SSD_DINNER + SSD_CONV_DIM
CD_MAIN = 3 * FOX_WIDTH + 3 * SC_DIM

LANES = 128
SUBLANES = 8
VMEM_LIMIT = 60 * 1024 * 1024


def _cparams(n_axes):
    return pltpu.CompilerParams(dimension_semantics=("arbitrary",) * n_axes,
                                vmem_limit_bytes=VMEM_LIMIT)


def _rms(xf, w):
    return xf * lax.rsqrt(jnp.mean(xf * xf, axis=-1, keepdims=True) + EPS) * w


def _softplus(x):
    return jnp.maximum(x, 0.0) + jnp.log1p(jnp.exp(-jnp.abs(x)))


def _split3(x):
    hi = x.astype(BF16)
    r1 = x - hi.astype(F32)
    mid = r1.astype(BF16)
    lo = (r1 - mid.astype(F32)).astype(BF16)
    return hi, mid, lo


def _widen(x, n):
    return x[:, 0:n] if n <= LANES else jnp.concatenate([x] * (n // LANES), axis=1)


def _dot(a, b):
    return jnp.dot(a, b, preferred_element_type=F32)


def _dot_nt(a, b):
    return lax.dot_general(a, b, (((1,), (1,)), ((), ())), preferred_element_type=F32)


def _dot_tn(a, b):
    return lax.dot_general(a, b, (((0,), (0,)), ((), ())), preferred_element_type=F32)


def _exact_lhs_dot(m_bf16, x):
    hi, mid, lo = _split3(x)
    return _dot(m_bf16, hi) + _dot(m_bf16, mid) + _dot(m_bf16, lo)


def _exact_rhs_dot(x, m_bf16):
    hi, mid, lo = _split3(x)
    return _dot(hi, m_bf16) + _dot(mid, m_bf16) + _dot(lo, m_bf16)


def _conv_rows(ext_ref, x, prev8, w_ref, width, rows, w_cols=slice(None)):
    ext_ref[0:SUBLANES, :] = prev8
    ext_ref[SUBLANES:SUBLANES + rows, :] = x
    out = None
    for j in range(width):
        off = SUBLANES - (width - 1) + j
        term = ext_ref[off:off + rows, :] * w_ref[j:j + 1, w_cols]
        out = term if out is None else out + term
    return out


def _seq_tiling(seq_len, tile_rows):
    if seq_len >= tile_rows:
        assert seq_len % tile_rows == 0
        return tile_rows, 1, seq_len // tile_rows
    assert tile_rows % seq_len == 0 and seq_len % SUBLANES == 0
    return seq_len, tile_rows // seq_len, 1


def _norm_matmul_kernel(x_ref, nw_ref, w_ref, ws_ref, o_ref, os_ref, h_ref, *, n_tiles, tn, last_w):
    j = pl.program_id(1)

    @pl.when(j == 0)
    def _():
        h_ref[...] = _rms(x_ref[...], nw_ref[...]).astype(BF16)
        os_ref[...] = _dot(h_ref[...], ws_ref[...])

    if last_w == tn:
        o_ref[...] = _dot(h_ref[...], w_ref[...])
    else:
        @pl.when(j < n_tiles - 1)
        def _():
            o_ref[...] = _dot(h_ref[...], w_ref[...])

        @pl.when(j == n_tiles - 1)
        def _():
            o_ref[:, 0:last_w] = _dot(h_ref[...], w_ref[:, 0:last_w])


def _norm_matmul(x, norm_w, w, w_small, *, n, tm, tn):
    m, d = x.shape
    n_tiles = pl.cdiv(n, tn)
    last_w = n - (n_tiles - 1) * tn
    assert m % tm == 0 and last_w % LANES == 0 and n <= w.shape[1] and w_small.shape == (d, LANES)
    return pl.pallas_call(
        functools.partial(_norm_matmul_kernel, n_tiles=n_tiles, tn=tn, last_w=last_w),
        grid=(m // tm, n_tiles),
        in_specs=[pl.BlockSpec((tm, d), lambda i, j: (i, 0)),
                  pl.BlockSpec((1, d), lambda i, j: (0, 0)),
                  pl.BlockSpec((d, tn), lambda i, j: (0, j)),
                  pl.BlockSpec((d, LANES), lambda i, j: (0, 0))],
        out_specs=[pl.BlockSpec((tm, tn), lambda i, j: (i, j)),
                   pl.BlockSpec((tm, LANES), lambda i, j: (i, 0))],
        out_shape=[jax.ShapeDtypeStruct((m, n), F32),
                   jax.ShapeDtypeStruct((m, LANES), F32)],
        scratch_shapes=[pltpu.VMEM((tm, d), BF16)],
        compiler_params=_cparams(2),
        name="norm_in_proj",
    )(x, norm_w.reshape(1, d), w, w_small)


def _proj_res_kernel(x_ref, a_ref, b_ref, wa_ref, wb_ref, o_ref):
    acc = _dot(a_ref[...], wa_ref[...])
    acc = acc + _dot(b_ref[...], wb_ref[...])
    o_ref[...] = x_ref[...] + acc


def _proj_residual(x, a, b, w, *, tm, tn):
    m, d = x.shape
    ka, kb = a.shape[1], b.shape[1]
    assert m % tm == 0 and d % tn == 0 and ka == kb and w.shape == (ka + kb, d)
    return pl.pallas_call(
        _proj_res_kernel,
        grid=(m // tm, d // tn),
        in_specs=[pl.BlockSpec((tm, tn), lambda i, j: (i, j)),
                  pl.BlockSpec((tm, ka), lambda i, j: (i, 0)),
                  pl.BlockSpec((tm, kb), lambda i, j: (i, 0)),
                  pl.BlockSpec((ka, tn), lambda i, j: (0, j)),
                  pl.BlockSpec((kb, tn), lambda i, j: (1, j))],
        out_specs=pl.BlockSpec((tm, tn), lambda i, j: (i, j)),
        out_shape=jax.ShapeDtypeStruct((m, d), F32),
        compiler_params=_cparams(2),
        name="out_proj_residual",
    )(x, a, b, w, w)


def _ffn_kernel(*refs, tm, rows, spt, tpb, nf, ts, nsub, nsub_last, final, n_cast):
    x_ref, nw_ref, wg_ref, wu_ref, wd_ref, cw_ref, cb_ref, hist_ref, fw_ref = refs[:9]
    cast_in = refs[9:9 + n_cast]
    o_ref, tail_ref = refs[9 + n_cast:11 + n_cast]
    cast_out = refs[11 + n_cast:11 + 2 * n_cast]
    h_ref, carry_ref, ext_ref = refs[11 + 2 * n_cast:]
    i = pl.program_id(0)
    f = pl.program_id(1)

    for src_ref, dst_ref in zip(cast_in, cast_out):
        dst_ref[...] = src_ref[...].astype(BF16)

    @pl.when(f == 0)
    def _():
        xf = x_ref[...]
        h_ref[...] = _rms(xf, nw_ref[...]).astype(BF16)
        o_ref[...] = xf

    first = (i % tpb) == 0

    def sub_block(sb):
        cols = slice(sb * ts, (sb + 1) * ts)
        h = h_ref[...]
        a = _dot(h, wg_ref[:, cols])
        u = _dot(h, wu_ref[:, cols])
        convs = []
        for s in range(spt):
            a_s = a[s * rows:(s + 1) * rows]
            if tpb == 1:
                prev = hist_ref[s, :, cols]
            else:
                prev = jnp.where(first, hist_ref[s, :, cols], carry_ref[f * nsub + sb])
            convs.append(_conv_rows(ext_ref, a_s, prev, cw_ref, FFN_CONV, rows, cols))
            tail_ref[s, :, cols] = a_s[rows - SUBLANES:rows]
        if tpb > 1:
            carry_ref[f * nsub + sb] = a[tm - SUBLANES:tm]
        conv = convs[0] if spt == 1 else jnp.concatenate(convs, axis=0)
        act = (jax.nn.silu(conv + cb_ref[:, cols]) * u).astype(BF16)
        o_ref[...] += _dot(act, wd_ref[cols, :])

    if nsub_last == nsub:
        for sb in range(nsub):
            sub_block(sb)
    else:
        @pl.when(f < nf - 1)
        def _():
            for sb in range(nsub):
                sub_block(sb)

        @pl.when(f == nf - 1)
        def _():
            for sb in range(nsub_last):
                sub_block(sb)

    if final:
        @pl.when(f == nf - 1)
        def _():
            o_ref[...] = _rms(o_ref[...], fw_ref[...])


def _conv_ffn(x, norm_w, wg, wu, wd, conv_w, conv_b, hist8, final_w, *, layer, seq_len, tm, tf, ts, final,
              cast_next=None):
    m, d = x.shape
    ff = wg.shape[2]
    assert m % tm == 0 and tf % ts == 0 and ff % ts == 0
    rows, spt, tpb = _seq_tiling(seq_len, tm)
    nm, nf = m // tm, pl.cdiv(ff, tf)
    nsub = tf // ts
    nsub_last = (ff - (nf - 1) * tf) // ts
    hist_map = (lambda i, f: (i // tpb, 0, f)) if spt == 1 else (lambda i, f: (i, 0, f))
    cast_in, cast_specs, cast_src_specs = [], [], []
    if cast_next is not None:
        assert d % nm == 0 and (d // nm) % LANES == 0 and ff % nf == 0 and (ff // nf) % LANES == 0
        dr, fc = d // nm, ff // nf
        cast_layer, cast_in = cast_next[0], list(cast_next[1:])
        cast_specs = [pl.BlockSpec((dr, fc), lambda i, f: (i, f)),
                      pl.BlockSpec((dr, fc), lambda i, f: (i, f)),
                      pl.BlockSpec((fc, dr), lambda i, f: (f, i))]
        cast_src_specs = [pl.BlockSpec((None, dr, fc), lambda i, f: (cast_layer, i, f)),
                          pl.BlockSpec((None, dr, fc), lambda i, f: (cast_layer, i, f)),
                          pl.BlockSpec((None, fc, dr), lambda i, f: (cast_layer, f, i))]
    kern = functools.partial(_ffn_kernel, tm=tm, rows=rows, spt=spt, tpb=tpb, nf=nf, ts=ts, nsub=nsub,
                             nsub_last=nsub_last, final=final, n_cast=len(cast_in))
    results = pl.pallas_call(
        kern,
        grid=(nm, nf),
        in_specs=[pl.BlockSpec((tm, d), lambda i, f: (i, 0)),
                  pl.BlockSpec((1, d), lambda i, f: (0, 0)),
                  pl.BlockSpec((None, d, tf), lambda i, f: (layer, 0, f)),
                  pl.BlockSpec((None, d, tf), lambda i, f: (layer, 0, f)),
                  pl.BlockSpec((None, tf, d), lambda i, f: (layer, f, 0)),
                  pl.BlockSpec((FFN_CONV, tf), lambda i, f: (0, f)),
                  pl.BlockSpec((1, tf), lambda i, f: (0, f)),
                  pl.BlockSpec((spt, SUBLANES, tf), hist_map),
                  pl.BlockSpec((1, d), lambda i, f: (0, 0))] + cast_src_specs,
        out_specs=[pl.BlockSpec((tm, d), lambda i, f: (i, 0)),
                   pl.BlockSpec((spt, SUBLANES, tf), lambda i, f: (i, 0, f))] + cast_specs,
        out_shape=[jax.ShapeDtypeStruct((m, d), F32),
                   jax.ShapeDtypeStruct((nm * spt, SUBLANES, ff), F32)]
                  + [jax.ShapeDtypeStruct(w.shape[1:], BF16) for w in cast_in],
        scratch_shapes=[pltpu.VMEM((tm, d), BF16),
                        pltpu.VMEM((nf * nsub, SUBLANES, ts), F32),
                        pltpu.VMEM((rows + SUBLANES, ts), F32)],
        compiler_params=_cparams(2),
        name="conv_ffn",
    )(x, norm_w.reshape(1, d), wg, wu, wd, conv_w, conv_b.reshape(1, ff), hist8, final_w.reshape(1, d), *cast_in)
    return results[0], results[1], tuple(results[2:])


def _retention_kernel(*refs, c, side_cast):
    q_ref, k_ref, v_ref, g_ref, cos_ref, sin_ref, st_ref, nw_ref = refs[:8]
    y_ref, so_ref = refs[8 + side_cast:10 + side_cast]
    ci = pl.program_id(1)

    if side_cast:
        w = refs[8][...]
        w_main_ref, w_gate_ref = refs[11:13]
        f0 = 3 * FOX_WIDTH
        w_main_ref[:, 0:f0] = w[:, 0:f0].astype(BF16)
        w_main_ref[:, f0:CD_MAIN] = w[:, f0 + FOX_HEADS:CD_MAIN + FOX_HEADS].astype(BF16)
        gate = jnp.concatenate([w[:, f0:f0 + FOX_HEADS], jnp.zeros((w.shape[0], LANES - FOX_HEADS), F32)], axis=1)
        w_gate_ref[...] = gate.astype(BF16)

    @pl.when(ci == 0)
    def _():
        so_ref[...] = st_ref[...]

    cos = cos_ref[...]
    sin = sin_ref[...]
    ii = lax.broadcasted_iota(jnp.int32, (c, c), 0)
    jj = lax.broadcasted_iota(jnp.int32, (c, c), 1)
    diff = (ii - jj).astype(F32)
    causal = ii >= jj
    ridx = lax.broadcasted_iota(jnp.int32, (c, 1), 0).astype(F32)
    for h in range(RET_HEADS):
        lg = math.log1p(-(2.0 ** (-5.0 - h)))
        q = q_ref[:, h * RET_DK:(h + 1) * RET_DK]
        k = k_ref[:, h * RET_DK:(h + 1) * RET_DK]
        v = v_ref[:, h * RET_DV:(h + 1) * RET_DV]
        qr = q * cos + pltpu.roll(q, RET_DK // 2, 1) * sin
        kr = (k * cos + pltpu.roll(k, RET_DK // 2, 1) * sin) * (RET_DK ** -0.5)
        qb = qr.astype(BF16)
        kb = kr.astype(BF16)
        vb = v.astype(BF16)
        decay = jnp.exp(jnp.where(causal, diff * lg, NEG_INF))
        inner = jnp.exp((ridx + 1.0) * lg)
        sdecay = jnp.exp((c - 1.0 - ridx) * lg)
        s = so_ref[0, h]
        scores = _dot_nt(qb, kb) * decay
        y = _dot(scores.astype(BF16), vb)
        y = y + _dot(qb, s.astype(BF16)) * inner
        kd = (kr * sdecay).astype(BF16)
        so_ref[0, h] = math.exp(c * lg) * s + _dot_tn(kd, vb)
        mu = jnp.mean(y, axis=-1, keepdims=True)
        yc = y - mu
        var = jnp.mean(yc * yc, axis=-1, keepdims=True)
        yn = yc * lax.rsqrt(var + EPS) * nw_ref[:, h * RET_DV:(h + 1) * RET_DV]
        g = g_ref[:, h * RET_DV:(h + 1) * RET_DV]
        y_ref[:, h * RET_DV:(h + 1) * RET_DV] = (jax.nn.silu(g) * yn).astype(BF16)


def _retention(proj, cosf, sinf, state, norm_w, *, bsz, seq_len, c, cd_w_in=None):
    m = proj.shape[0]
    nc = seq_len // c
    assert seq_len % c == 0
    qk_w = RET_HEADS * RET_DK
    v_w = RET_HEADS * RET_DV
    row = lambda b, ci: b * nc + ci
    side_in, side_src, side_dst, side_shapes = [], [], [], []
    if cd_w_in is not None and cd_w_in.shape[0] % (bsz * nc * 16) == 0:
        d, wide = cd_w_in.shape
        rb = d // (bsz * nc)
        side_in = [cd_w_in]
        side_src = [pl.BlockSpec((rb, wide), lambda b, ci: (row(b, ci), 0))]
        side_dst = [pl.BlockSpec((rb, CD_MAIN), lambda b, ci: (row(b, ci), 0)),
                    pl.BlockSpec((rb, LANES), lambda b, ci: (row(b, ci), 0))]
        side_shapes = [jax.ShapeDtypeStruct((d, CD_MAIN), BF16), jax.ShapeDtypeStruct((d, LANES), BF16)]
    results = pl.pallas_call(
        functools.partial(_retention_kernel, c=c, side_cast=len(side_in)),
        grid=(bsz, nc),
        in_specs=[pl.BlockSpec((c, qk_w), lambda b, ci: (row(b, ci), 0)),
                  pl.BlockSpec((c, qk_w), lambda b, ci: (row(b, ci), 1)),
                  pl.BlockSpec((c, v_w), lambda b, ci: (row(b, ci), 1)),
                  pl.BlockSpec((c, v_w), lambda b, ci: (row(b, ci), 2)),
                  pl.BlockSpec((c, RET_DK), lambda b, ci: (ci, 0)),
                  pl.BlockSpec((c, RET_DK), lambda b, ci: (ci, 0)),
                  pl.BlockSpec((1, RET_HEADS, RET_DK, RET_DV), lambda b, ci: (b, 0, 0, 0)),
                  pl.BlockSpec((1, v_w), lambda b, ci: (0, 0))] + side_src,
        out_specs=[pl.BlockSpec((c, v_w), lambda b, ci: (row(b, ci), 0)),
                   pl.BlockSpec((1, RET_HEADS, RET_DK, RET_DV), lambda b, ci: (b, 0, 0, 0))] + side_dst,
        out_shape=[jax.ShapeDtypeStruct((m, v_w), BF16),
                   jax.ShapeDtypeStruct(state.shape, F32)] + side_shapes,
        compiler_params=_cparams(2),
        name="retention",
    )(proj, proj, proj, proj, cosf, sinf, state, norm_w.reshape(1, v_w), *side_in)
    return results[0], results[1], (tuple(results[2:]) if side_in else None)


def _ssd_kernel(*refs, c, nc, n_cast):
    (z_ref, xs_ref, bc_ref, dt_ref, hx_ref, hbc_ref, st_ref, cwx_ref, cwbc_ref, cbx_ref, cbbc_ref,
     dtb_ref, alog_ref, dsk_ref, nw_ref, tri_ref, exp_ref) = refs[:17]
    cast_in = refs[17:17 + n_cast]
    y_ref, so_ref = refs[17 + n_cast:19 + n_cast]
    cast_out = refs[19 + n_cast:19 + 2 * n_cast]
    st_scr, cx_scr, cbc_scr, extx_scr, extbc_scr, yh_scr, xs_scr = refs[19 + 2 * n_cast:]
    ci = pl.program_id(1)

    for src_ref, dst_ref in zip(cast_in, cast_out):
        dst_ref[...] = src_ref[...].astype(BF16)
    gw = SSD_DINNER // SSD_GROUPS
    hpg = SSD_HEADS // SSD_GROUPS

    @pl.when(ci == 0)
    def _():
        st_scr[...] = st_ref[0].T
        cx_scr[...] = hx_ref[0]
        cbc_scr[...] = hbc_ref[0]

    xs_raw = xs_ref[...]
    bc_raw = bc_ref[...]
    xs_scr[...] = jax.nn.silu(_conv_rows(extx_scr, xs_raw, cx_scr[...], cwx_ref, SSD_CONV, c) + cbx_ref[...])
    bcm = jax.nn.silu(_conv_rows(extbc_scr, bc_raw, cbc_scr[...], cwbc_ref, SSD_CONV, c) + cbbc_ref[...])
    cx_scr[...] = xs_raw[c - SUBLANES:c]
    cbc_scr[...] = bc_raw[c - SUBLANES:c]

    tri = tri_ref[...]
    dt = _softplus(dt_ref[...] + dtb_ref[...])
    a = -jnp.exp(alog_ref[...])
    acs = _exact_lhs_dot(tri, dt * a)
    acs_t = acs.T
    acs_last = acs[c - 1:c, :]
    exp_acs = jnp.exp(acs)
    to_end = jnp.exp(acs_last - acs)
    chunk_dec = jnp.exp(acs_last)

    ii = lax.broadcasted_iota(jnp.int32, (c, c), 0)
    jj = lax.broadcasted_iota(jnp.int32, (c, c), 1)
    causal = ii >= jj
    nb = SSD_GROUPS * SSD_DSTATE
    for g in range(SSD_GROUPS):
        cols = slice(g * gw, (g + 1) * gw)
        expand = exp_ref[:, cols]
        xdt = xs_scr[:, cols] * _exact_rhs_dot(dt, expand)
        xdt_b = xdt.astype(BF16)
        xend_b = (xdt * _exact_rhs_dot(to_end, expand)).astype(BF16)
        b_g = bcm[:, g * SSD_DSTATE:(g + 1) * SSD_DSTATE].astype(BF16)
        c_g = bcm[:, nb + g * SSD_DSTATE:nb + (g + 1) * SSD_DSTATE].astype(BF16)
        cb = _dot_nt(c_g, b_g)
        s_g = st_scr[:, cols]
        y_state = _dot(c_g, s_g.astype(BF16)) * _exact_rhs_dot(exp_acs, expand)
        for r in range(hpg):
            hh = g * hpg + r
            seg = acs[:, hh:hh + 1] - acs_t[hh:hh + 1, :]
            lmat = jnp.exp(jnp.where(causal, seg, NEG_INF))
            mm = (cb * lmat).astype(BF16)
            head = slice(r * SSD_HEADDIM, (r + 1) * SSD_HEADDIM)
            yh_scr[:, hh * SSD_HEADDIM:(hh + 1) * SSD_HEADDIM] = _dot(mm, xdt_b[:, head]) + y_state[:, head]
        upd = _dot_tn(b_g, xend_b)
        st_scr[:, cols] = _exact_rhs_dot(chunk_dec, expand) * s_g + upd

    y = yh_scr[...] + dsk_ref[...] * xs_scr[...]
    z = z_ref[...]
    y_ref[...] = _rms(y * jax.nn.silu(z), nw_ref[...]).astype(BF16)

    @pl.when(ci == nc - 1)
    def _():
        so_ref[0] = st_scr[...].T


def _row_block_cast_specs(w3, layer, n_steps, step_of):
    _, r, cols = w3.shape
    for per in (1, 2, 4, 8):
        if (r * per) % n_steps == 0 and (r * per // n_steps) % 16 == 0:
            rb = r * per // n_steps
            src = pl.BlockSpec((None, rb, cols), lambda *g: (layer, step_of(*g) // per, 0))
            dst = pl.BlockSpec((rb, cols), lambda *g: (step_of(*g) // per, 0))
            return src, dst, jax.ShapeDtypeStruct((r, cols), BF16)
    return None


def _ssd(proj, dt_proj, hist8, state, conv_w, conv_b, dt_bias, a_log, d_skip, norm_w, *, bsz, seq_len, c,
         cast=None):
    m = proj.shape[0]
    nc = seq_len // c
    assert seq_len % c == 0
    row = lambda b, ci: b * nc + ci
    cast_in, cast_src, cast_dst, cast_shapes = [], [], [], []
    if cast is not None:
        specs = [_row_block_cast_specs(w, cast[0], bsz * nc, row) for w in cast[1]]
        if all(s is not None for s in specs):
            cast_in = list(cast[1])
            cast_src, cast_dst, cast_shapes = (list(x) for x in zip(*specs))
    const2 = lambda b, ci: (0, 0)
    di, bcw = SSD_DINNER, 2 * SSD_GROUPS * SSD_DSTATE
    tri = jnp.asarray(np.tril(np.ones((c, c), np.float32)), BF16)
    expand = np.zeros((LANES, di), np.float32)
    for h in range(SSD_HEADS):
        expand[h, h * SSD_HEADDIM:(h + 1) * SSD_HEADDIM] = 1.0
    expand = jnp.asarray(expand, BF16)
    pad_row = lambda v: jnp.pad(v.astype(F32), (0, LANES - v.shape[0])).reshape(1, LANES)
    st2 = state.reshape(bsz, di, SSD_DSTATE)
    results = pl.pallas_call(
        functools.partial(_ssd_kernel, c=c, nc=nc, n_cast=len(cast_in)),
        grid=(bsz, nc),
        in_specs=[pl.BlockSpec((c, di), lambda b, ci: (row(b, ci), 3)),
                  pl.BlockSpec((c, di), lambda b, ci: (row(b, ci), 4)),
                  pl.BlockSpec((c, bcw), lambda b, ci: (row(b, ci), 10)),
                  pl.BlockSpec((c, LANES), lambda b, ci: (row(b, ci), 0)),
                  pl.BlockSpec((1, SUBLANES, di), lambda b, ci: (b, 0, 0)),
                  pl.BlockSpec((1, SUBLANES, bcw), lambda b, ci: (b, 0, 2)),
                  pl.BlockSpec((1, di, SSD_DSTATE), lambda b, ci: (b, 0, 0)),
                  pl.BlockSpec((SSD_CONV, di), const2),
                  pl.BlockSpec((SSD_CONV, bcw), lambda b, ci: (0, 2)),
                  pl.BlockSpec((1, di), const2),
                  pl.BlockSpec((1, bcw), lambda b, ci: (0, 2)),
                  pl.BlockSpec((1, LANES), const2),
                  pl.BlockSpec((1, LANES), const2),
                  pl.BlockSpec((1, di), const2),
                  pl.BlockSpec((1, di), const2),
                  pl.BlockSpec((c, c), const2),
                  pl.BlockSpec((LANES, di), const2)] + cast_src,
        out_specs=[pl.BlockSpec((c, di), lambda b, ci: (row(b, ci), 0)),
                   pl.BlockSpec((1, di, SSD_DSTATE), lambda b, ci: (b, 0, 0))] + cast_dst,
        out_shape=[jax.ShapeDtypeStruct((m, di), BF16),
                   jax.ShapeDtypeStruct(st2.shape, F32)] + cast_shapes,
        scratch_shapes=[pltpu.VMEM((SSD_DSTATE, di), F32),
                        pltpu.VMEM((SUBLANES, di), F32),
                        pltpu.VMEM((SUBLANES, bcw), F32),
                        pltpu.VMEM((c + SUBLANES, di), F32),
                        pltpu.VMEM((c + SUBLANES, bcw), F32),
                        pltpu.VMEM((c, di), F32),
                        pltpu.VMEM((c, di), F32)],
        compiler_params=_cparams(2),
        name="ssd",
    )(proj, proj, proj, dt_proj, hist8, hist8, st2,
      conv_w, conv_w, conv_b.reshape(1, -1), conv_b.reshape(1, -1),
      pad_row(dt_bias), pad_row(a_log), jnp.repeat(d_skip.astype(F32), SSD_HEADDIM).reshape(1, di),
      norm_w.reshape(1, di), tri, expand, *cast_in)
    return results[0], results[1].reshape(state.shape), (list(results[2:]) if cast_in else None)


def _decode_cum_kernel(lfc_ref, fl_ref, b_ref, tri_ref, cumt_c_ref, lfn_ref, cumn_ref, cumt_n_ref,
                       carry_ref, *, c, ncb):
    j = pl.program_id(0)

    @pl.when(j == 0)
    def _():
        carry_ref[...] = jnp.zeros_like(carry_ref)

    @pl.when(j < ncb)
    def _():
        cum = _exact_lhs_dot(tri_ref[...], lfc_ref[...]) + carry_ref[...]
        carry_ref[...] = cum[c - 1:c, :]
        cumt_c_ref[...] = cum.T

    @pl.when(j == ncb)
    def _():
        lf = -_softplus(-(fl_ref[...] + b_ref[...]))
        lfn_ref[...] = lf
        cum = _exact_lhs_dot(tri_ref[0:LANES, 0:LANES], lf) + carry_ref[...]
        cumn_ref[...] = cum
        cumt_n_ref[...] = cum.T


def _decode_cum(cache_lf, fl_new, bias, *, c):
    past = cache_lf.shape[0]
    assert past % c == 0 and c % LANES == 0 and fl_new.shape == (LANES, LANES)
    ncb = past // c
    tri = jnp.asarray(np.tril(np.ones((c, c), np.float32)), BF16)
    blk = lambda j: jnp.minimum(j, ncb - 1)
    sq = jax.ShapeDtypeStruct((LANES, LANES), F32)
    return pl.pallas_call(
        functools.partial(_decode_cum_kernel, c=c, ncb=ncb),
        grid=(ncb + 1,),
        in_specs=[pl.BlockSpec((c, LANES), lambda j: (blk(j), 0)),
                  pl.BlockSpec((LANES, LANES), lambda j: (0, 0)),
                  pl.BlockSpec((1, LANES), lambda j: (0, 0)),
                  pl.BlockSpec((c, c), lambda j: (0, 0))],
        out_specs=[pl.BlockSpec((LANES, c), lambda j: (0, blk(j))),
                   pl.BlockSpec((LANES, LANES), lambda j: (0, 0)),
                   pl.BlockSpec((LANES, LANES), lambda j: (0, 0)),
                   pl.BlockSpec((LANES, LANES), lambda j: (0, 0))],
        out_shape=[jax.ShapeDtypeStruct((LANES, past), F32), sq, sq, sq],
        scratch_shapes=[pltpu.VMEM((1, LANES), F32)],
        compiler_params=_cparams(1),
        name="decode_logf_cumsum",
    )(cache_lf, fl_new, bias, tri)


FOX_AUG = 2 * FOX_HEAD_DIM
N_BIAS_PIECES = 3


def _fox_prep_kernel(q_ref, k_ref, v_ref, fl_ref, fb_ref, tri_ref, place_ref, ones_ref,
                     qa_ref, ka_ref, vb_ref, k32_ref, v32_ref, lf_ref, carry_ref, *, tp):
    @pl.when(pl.program_id(1) == 0)
    def _():
        carry_ref[...] = jnp.zeros_like(carry_ref)

    lf = -_softplus(-(fl_ref[...] + fb_ref[...]))
    lf_ref[...] = lf
    cum = _exact_lhs_dot(tri_ref[...], lf) + carry_ref[...]
    carry_ref[...] = cum[tp - 1:tp, :]
    pieces = _split3(cum * (FOX_HEAD_DIM ** 0.5))
    n = N_BIAS_PIECES
    aug_q = ones_ref[0:1, :] + sum(_dot(pieces[r], place_ref[r]) for r in range(n))
    aug_k = ones_ref[1:2, :] - sum(_dot(pieces[r], place_ref[n + r]) for r in range(n))
    for h in range(FOX_HEADS):
        src = slice(h * FOX_HEAD_DIM, (h + 1) * FOX_HEAD_DIM)
        feat = slice(h * FOX_AUG, h * FOX_AUG + FOX_HEAD_DIM)
        bias = slice(h * FOX_AUG + FOX_HEAD_DIM, (h + 1) * FOX_AUG)
        qa_ref[:, feat] = q_ref[:, src].astype(BF16)
        qa_ref[:, bias] = aug_q[:, src].astype(BF16)
        ka_ref[:, feat] = k_ref[:, src].astype(BF16)
        ka_ref[:, bias] = aug_k[:, src].astype(BF16)
    k = k_ref[...]
    v = v_ref[...]
    k32_ref[...] = pltpu.einshape("m(hd)->mhd", k, h=FOX_HEADS)
    v32_ref[...] = pltpu.einshape("m(hd)->mhd", v, h=FOX_HEADS)
    vb_ref[...] = v.astype(BF16)


def _fox_prep(proj, fl_proj, f_bias, *, bsz, seq_len, tp):
    m = proj.shape[0]
    nt = seq_len // tp
    assert seq_len % tp == 0
    w = FOX_WIDTH
    tri = jnp.asarray(np.tril(np.ones((tp, tp), np.float32)), BF16)
    n = N_BIAS_PIECES
    place = np.zeros((2 * n, LANES, w), np.float32)
    ones = np.zeros((SUBLANES, w), np.float32)
    for h in range(FOX_HEADS):
        for r in range(2 * n):
            place[r, h, h * FOX_HEAD_DIM + r] = 1.0
        ones[0, h * FOX_HEAD_DIM + n:h * FOX_HEAD_DIM + 2 * n] = 1.0
        ones[1, h * FOX_HEAD_DIM:h * FOX_HEAD_DIM + n] = 1.0
    row = lambda b, ti: (b * nt + ti, 0)
    const2 = lambda b, ti: (0, 0)
    return pl.pallas_call(
        functools.partial(_fox_prep_kernel, tp=tp),
        grid=(bsz, nt),
        in_specs=[pl.BlockSpec((tp, w), lambda b, ti: (b * nt + ti, 0)),
                  pl.BlockSpec((tp, w), lambda b, ti: (b * nt + ti, 1)),
                  pl.BlockSpec((tp, w), lambda b, ti: (b * nt + ti, 2)),
                  pl.BlockSpec((tp, LANES), row),
                  pl.BlockSpec((1, LANES), const2),
                  pl.BlockSpec((tp, tp), const2),
                  pl.BlockSpec((2 * n, LANES, w), lambda b, ti: (0, 0, 0)),
                  pl.BlockSpec((SUBLANES, w), const2)],
        out_specs=[pl.BlockSpec((tp, FOX_HEADS * FOX_AUG), row),
                   pl.BlockSpec((tp, FOX_HEADS * FOX_AUG), row),
                   pl.BlockSpec((tp, w), row),
                   pl.BlockSpec((tp, FOX_HEADS, FOX_HEAD_DIM), lambda b, ti: (b * nt + ti, 0, 0)),
                   pl.BlockSpec((tp, FOX_HEADS, FOX_HEAD_DIM), lambda b, ti: (b * nt + ti, 0, 0)),
                   pl.BlockSpec((tp, LANES), row)],
        out_shape=[jax.ShapeDtypeStruct((m, FOX_HEADS * FOX_AUG), BF16),
                   jax.ShapeDtypeStruct((m, FOX_HEADS * FOX_AUG), BF16),
                   jax.ShapeDtypeStruct((m, w), BF16),
                   jax.ShapeDtypeStruct((m, FOX_HEADS, FOX_HEAD_DIM), F32),
                   jax.ShapeDtypeStruct((m, FOX_HEADS, FOX_HEAD_DIM), F32),
                   jax.ShapeDtypeStruct((m, LANES), F32)],
        scratch_shapes=[pltpu.VMEM((1, LANES), F32)],
        compiler_params=_cparams(2),
        name="fox_prep",
    )(proj, proj, proj, fl_proj, f_bias, tri, jnp.asarray(place, BF16), jnp.asarray(ones, F32))


def _fox_kernel(qi_ref, ki_ref, q_ref, k_ref, v_ref, o_ref, m_ref, acc_ref, va_ref, *, t, ts, ts_diag):
    step = pl.program_id(2)
    qi = qi_ref[step]
    ki = ki_ref[step]
    to_log2 = (FOX_HEAD_DIM ** -0.5) * math.log2(math.e)
    hd = FOX_HEAD_DIM

    @pl.when(ki == 0)
    def _():
        m_ref[...] = jnp.full_like(m_ref, NEG_INF)
        acc_ref[...] = jnp.zeros_like(acc_ref)
        va_ref[:, hd:2 * hd] = jnp.ones((t, hd), BF16)

    va_ref[:, 0:hd] = v_ref[...]

    def update(diagonal):
        tc = ts_diag if diagonal else ts
        for r in range(t // tc):
            rows = slice(r * tc, (r + 1) * tc)
            nk = (r + 1) * tc if diagonal else t
            s = _dot_nt(q_ref[rows, :], k_ref[0:nk, :])
            if diagonal:
                ri = lax.broadcasted_iota(jnp.int32, (tc, nk), 0) + r * tc
                ci = lax.broadcasted_iota(jnp.int32, (tc, nk), 1)
                s = jnp.where(ci <= ri, s, NEG_INF)
            m_old = m_ref[rows, :]
            m_new = jnp.maximum(m_old, jnp.max(s, axis=1, keepdims=True))
            m_ref[rows, :] = m_new
            alpha = jnp.exp2((m_old - m_new) * to_log2)
            p = jnp.exp2((s - _widen(m_new, nk)) * to_log2)
            acc_ref[rows, :] = _widen(alpha, 2 * hd) * acc_ref[rows, :] + _dot(p.astype(BF16), va_ref[0:nk, :])

    @pl.when(ki < qi)
    def _():
        update(False)

    @pl.when(ki == qi)
    def _():
        update(True)
        o_ref[...] = (acc_ref[:, 0:hd] / acc_ref[:, hd:2 * hd]).astype(BF16)


def _fox_prompt(qa, ka, vb, *, bsz, seq_len, t, ts, ts_diag):
    m = qa.shape[0]
    nq = seq_len // t
    assert seq_len % t == 0 and t % ts == 0 and t % ts_diag == 0
    pairs = [(qi, ki) for qi in range(nq) for ki in range(qi + 1)]
    qi_tab = jnp.asarray([p[0] for p in pairs], jnp.int32)
    ki_tab = jnp.asarray([p[1] for p in pairs], jnp.int32)
    grid_spec = pltpu.PrefetchScalarGridSpec(
        num_scalar_prefetch=2,
        grid=(bsz, FOX_HEADS, len(pairs)),
        in_specs=[pl.BlockSpec((t, FOX_AUG), lambda b, h, s, qi, ki: (b * nq + qi[s], h)),
                  pl.BlockSpec((t, FOX_AUG), lambda b, h, s, qi, ki: (b * nq + ki[s], h)),
                  pl.BlockSpec((t, FOX_HEAD_DIM), lambda b, h, s, qi, ki: (b * nq + ki[s], h))],
        out_specs=pl.BlockSpec((t, FOX_HEAD_DIM), lambda b, h, s, qi, ki: (b * nq + qi[s], h)),
        scratch_shapes=[pltpu.VMEM((t, LANES), F32), pltpu.VMEM((t, 2 * FOX_HEAD_DIM), F32),
                        pltpu.VMEM((t, 2 * FOX_HEAD_DIM), BF16)],
    )
    return pl.pallas_call(
        functools.partial(_fox_kernel, t=t, ts=ts, ts_diag=ts_diag),
        grid_spec=grid_spec,
        out_shape=jax.ShapeDtypeStruct((m, FOX_WIDTH), BF16),
        compiler_params=_cparams(3),
        name="fox_attention",
    )(qi_tab, ki_tab, qa, ka, vb)


def _fox_decode_kernel(q_ref, kn_ref, vn_ref, kc_ref, vc_ref, cq_ref, ckc_ref, ckn_ref, o_ref,
                       m_ref, l_ref, acc_ref, *, lq, ncb):
    j = pl.program_id(1)
    nh, hd = FOX_HEADS, FOX_HEAD_DIM

    @pl.when(j == 0)
    def _():
        m_ref[...] = jnp.full_like(m_ref, NEG_INF)
        l_ref[...] = jnp.zeros_like(l_ref)
        acc_ref[...] = jnp.zeros_like(acc_ref)

    def attend(k_head, v_head, ck_head, causal):
        for h in range(nh):
            qh = q_ref[:, h * hd:(h + 1) * hd].astype(BF16)
            ck = ck_head(h)
            tk = ck.shape[1]
            s = _dot_nt(qh, k_head(h).astype(BF16)) * (hd ** -0.5)
            s = s + (_widen(cq_ref[h], tk) - ck)
            if causal:
                rows = lax.broadcasted_iota(jnp.int32, (lq, tk), 0)
                cols = lax.broadcasted_iota(jnp.int32, (lq, tk), 1)
                s = jnp.where(cols <= rows, s, NEG_INF)
            m_old = m_ref[h]
            m_new = jnp.maximum(m_old, jnp.max(s, axis=1, keepdims=True))
            alpha = jnp.exp(m_old - m_new)
            p = jnp.exp(s - _widen(m_new, tk))
            l_ref[h] = alpha * l_ref[h] + jnp.sum(p, axis=1, keepdims=True)
            acc_ref[h] = alpha * acc_ref[h] + _dot(p.astype(BF16), v_head(h).astype(BF16))
            m_ref[h] = m_new

    @pl.when(j < ncb)
    def _():
        k_hm = pltpu.einshape("mhd->hmd", kc_ref[0])
        v_hm = pltpu.einshape("mhd->hmd", vc_ref[0])
        attend(lambda h: k_hm[h], lambda h: v_hm[h], lambda h: ckc_ref[h], False)

    @pl.when(j == ncb)
    def _():
        attend(lambda h: kn_ref[:, h * hd:(h + 1) * hd], lambda h: vn_ref[:, h * hd:(h + 1) * hd],
               lambda h: ckn_ref[h][:, 0:lq], True)
        for h in range(nh):
            o_ref[:, h * hd:(h + 1) * hd] = (acc_ref[h] / l_ref[h]).astype(BF16)


def _fox_decode(proj, cache_k, cache_v, cq, ck_cache, ck_new, *, bsz, lq, tk):
    past = cache_k.shape[1]
    assert past % tk == 0 and tk % LANES == 0 and lq <= LANES
    ncb = past // tk
    nh, hd, w = FOX_HEADS, FOX_HEAD_DIM, FOX_WIDTH
    tile = lambda j: jnp.minimum(j, ncb - 1)
    cache_spec = pl.BlockSpec((1, tk, nh, hd), lambda b, j: (b, tile(j), 0, 0))
    return pl.pallas_call(
        functools.partial(_fox_decode_kernel, lq=lq, ncb=ncb),
        grid=(bsz, ncb + 1),
        in_specs=[pl.BlockSpec((lq, w), lambda b, j: (b, 0)),
                  pl.BlockSpec((lq, w), lambda b, j: (b, 1)),
                  pl.BlockSpec((lq, w), lambda b, j: (b, 2)),
                  cache_spec,
                  cache_spec,
                  pl.BlockSpec((nh, lq, LANES), lambda b, j: (b, 0, 0)),
                  pl.BlockSpec((nh, 1, tk), lambda b, j: (b, 0, tile(j))),
                  pl.BlockSpec((nh, 1, LANES), lambda b, j: (b, 0, 0))],
        out_specs=pl.BlockSpec((lq, w), lambda b, j: (b, 0)),
        out_shape=jax.ShapeDtypeStruct((bsz * lq, w), BF16),
        scratch_shapes=[pltpu.VMEM((nh, lq, LANES), F32), pltpu.VMEM((nh, lq, LANES), F32),
                        pltpu.VMEM((nh, lq, hd), F32)],
        compiler_params=_cparams(2),
        name="fox_decode",
    )(proj, proj, proj, cache_k, cache_v, cq, ck_cache, ck_new)


def _sconv_kernel(u_ref, bg_ref, cg_ref, cw_ref, hist_ref, y_ref, tail_ref, carry_ref, ext_ref,
                  *, tm, rows, spt, tpb):
    i = pl.program_id(0)
    w = cg_ref[...] * u_ref[...]
    first = (i % tpb) == 0
    for s in range(spt):
        w_s = w[s * rows:(s + 1) * rows]
        if tpb == 1:
            prev = hist_ref[s]
        else:
            prev = jnp.where(first, hist_ref[s], carry_ref[...])
        conv = _conv_rows(ext_ref, w_s, prev, cw_ref, SC_WIDTH, rows)
        y_ref[s * rows:(s + 1) * rows, :] = (bg_ref[s * rows:(s + 1) * rows, :] * conv).astype(BF16)
        tail_ref[s] = w_s[rows - SUBLANES:rows]
    if tpb > 1:
        carry_ref[...] = w[tm - SUBLANES:tm]


def _sconv(proj, conv_w, hist8, *, seq_len, tm):
    m = proj.shape[0]
    assert m % tm == 0
    rows, spt, tpb = _seq_tiling(seq_len, tm)
    nm = m // tm
    hist_map = (lambda i: (i // tpb, 0, 0)) if spt == 1 else (lambda i: (i, 0, 0))
    base = 3 * FOX_WIDTH // SC_DIM
    return pl.pallas_call(
        functools.partial(_sconv_kernel, tm=tm, rows=rows, spt=spt, tpb=tpb),
        grid=(nm,),
        in_specs=[pl.BlockSpec((tm, SC_DIM), lambda i: (i, base)),
                  pl.BlockSpec((tm, SC_DIM), lambda i: (i, base + 1)),
                  pl.BlockSpec((tm, SC_DIM), lambda i: (i, base + 2)),
                  pl.BlockSpec((SC_WIDTH, SC_DIM), lambda i: (0, 0)),
                  pl.BlockSpec((spt, SUBLANES, SC_DIM), hist_map)],
        out_specs=[pl.BlockSpec((tm, SC_DIM), lambda i: (i, 0)),
                   pl.BlockSpec((spt, SUBLANES, SC_DIM), lambda i: (i, 0, 0))],
        out_shape=[jax.ShapeDtypeStruct((m, SC_DIM), BF16),
                   jax.ShapeDtypeStruct((nm * spt, SUBLANES, SC_DIM), F32)],
        scratch_shapes=[pltpu.VMEM((SUBLANES, SC_DIM), F32),
                        pltpu.VMEM((rows + SUBLANES, SC_DIM), F32)],
        compiler_params=_cparams(1),
        name="gated_short_conv",
    )(proj, proj, proj, conv_w, hist8)


def _hist8(state):
    n, w1, c = state.shape
    return jnp.concatenate([jnp.zeros((n, SUBLANES - w1, c), F32), state.astype(F32)], axis=1)


def _tails(tails, n_seq, seq_len, tile_rows, keep):
    per_seq = max(1, seq_len // tile_rows)
    pieces = tails.reshape(n_seq, per_seq, SUBLANES, tails.shape[-1])
    return pieces[:, per_seq - 1, SUBLANES - keep:, :]


def _rope_tables(pos0, length):
    half = RET_DK // 2
    inv = ROPE_BASE ** (-np.arange(half, dtype=np.float64) / half)
    ang = (pos0 + np.arange(length, dtype=np.float64))[:, None] * inv[None, :]
    cos, sin = np.cos(ang), np.sin(ang)
    return (jnp.asarray(np.concatenate([cos, cos], axis=1), F32),
            jnp.asarray(np.concatenate([-sin, sin], axis=1), F32))


def _pad_cols_bf16(w):
    return jnp.pad(w, ((0, 0), (0, LANES - w.shape[1]))).astype(BF16)


def _cd_weights_xla(cd_in):
    f0 = 3 * FOX_WIDTH
    main = jnp.concatenate([cd_in[:, :f0], cd_in[:, f0 + FOX_HEADS:]], axis=1).astype(BF16)
    return main, _pad_cols_bf16(cd_in[:, f0:f0 + FOX_HEADS])


def _prep_weights(p):
    ab_in = p['ab_w_in'][0]
    return dict(
        ab_in=ab_in.astype(BF16),
        ab_small=_pad_cols_bf16(ab_in[:, AB_MAIN:]),
        cd_in=None,
        cd_small=None,
        ab_out=p['ab_w_out'][0].astype(BF16),
        cd_out=p['cd_w_out'][0].astype(BF16),
        ffn0=None,
        ffn1=None,
    )


def _trunk(x, pos0, st_ret, st_ssd, st_ssd_conv, c_k, c_v, c_logf, st_sconv, st_ffn, p, wb, t):
    bsz, length, d = x.shape
    m = bsz * length
    xf = x.reshape(m, d)
    zeros = lambda *shape: jnp.zeros(shape, F32)

    proj, dt_proj = _norm_matmul(xf, p['ab_norm_w'][0], wb['ab_in'], wb['ab_small'],
                                 n=AB_MAIN, tm=t['tm_ab'], tn=t['tn_ab'])
    cosf, sinf = _rope_tables(pos0, length)
    ret_state = zeros(bsz, RET_HEADS, RET_DK, RET_DV) if st_ret is None else st_ret
    y_ret, ret_new, cd_bf16 = _retention(proj, cosf, sinf, ret_state, p['ret_norm_w'][0],
                                         bsz=bsz, seq_len=length, c=t['c_ret'],
                                         cd_w_in=p['cd_w_in'][0] if wb['cd_in'] is None else None)
    if wb['cd_in'] is None:
        wb['cd_in'], wb['cd_small'] = cd_bf16 if cd_bf16 is not None else _cd_weights_xla(p['cd_w_in'][0])
    ssd_state = zeros(bsz, SSD_HEADS, SSD_HEADDIM, SSD_DSTATE) if st_ssd is None else st_ssd
    ssd_hist = zeros(bsz, SSD_CONV - 1, SSD_CONV_DIM) if st_ssd_conv is None else st_ssd_conv
    ffn_f32 = [p['ffn_w_gate'], p['ffn_w_up'], p['ffn_w_down']]
    y_ssd, ssd_new, ffn0_bf16 = _ssd(proj, dt_proj, _hist8(ssd_hist), ssd_state, p['ssd_conv_w'][0],
                                     p['ssd_conv_b'][0], p['ssd_dt_bias'][0], p['ssd_A_log'][0], p['ssd_D'][0],
                                     p['ssd_norm_w'][0], bsz=bsz, seq_len=length, c=t['c_ssd'],
                                     cast=(0, ffn_f32) if wb['ffn0'] is None else None)
    if wb['ffn0'] is None:
        wb['ffn0'] = (tuple(w[None] for w in ffn0_bf16) if ffn0_bf16 is not None
                      else tuple(w[0:1].astype(BF16) for w in ffn_f32))
    xbc_lo = AB_MAIN - SSD_CONV_DIM
    ssd_conv_new = proj.reshape(bsz, length, -1)[:, length - (SSD_CONV - 1):, xbc_lo:AB_MAIN]
    xf = _proj_residual(xf, y_ret, y_ssd, wb['ab_out'], tm=t['tm_out'], tn=t['tn_out'])

    ffn_new = []
    ffn_hist0 = zeros(bsz, FFN_CONV - 1, D_FF) if st_ffn is None else st_ffn[0]
    next_f32 = (1, p['ffn_w_gate'], p['ffn_w_up'], p['ffn_w_down'])
    n_row_tiles = m // t['tm_ffn']
    if wb['ffn1'] is None and not (d % (n_row_tiles * LANES) == 0 and D_FF % t['tf_ffn'] == 0):
        wb['ffn1'] = tuple(w[1:2].astype(BF16) for w in next_f32[1:])
    xf, tails, next_bf16 = _conv_ffn(
        xf, p['ffn_norm_w'][0], *wb['ffn0'], p['ffn_conv_w'][0], p['ffn_conv_b'][0], _hist8(ffn_hist0),
        p['final_norm_w'], layer=0, seq_len=length, tm=t['tm_ffn'], tf=t['tf_ffn'], ts=t['ts_ffn'], final=False,
        cast_next=next_f32 if wb['ffn1'] is None else None)
    if wb['ffn1'] is None:
        wb['ffn1'] = tuple(w[None] for w in next_bf16)
    ffn_new.append(_tails(tails, bsz, length, t['tm_ffn'], FFN_CONV - 1))

    proj, fl_proj = _norm_matmul(xf, p['cd_norm_w'][0], wb['cd_in'], wb['cd_small'],
                                 n=CD_MAIN, tm=t['tm_cd'], tn=t['tn_cd'])
    f_bias = jnp.pad(p['fox_f_bias'][0].astype(F32), (0, LANES - FOX_HEADS)).reshape(1, LANES)
    head_shape = (bsz, length, FOX_HEADS, FOX_HEAD_DIM)
    if c_k is None:
        qa, ka, vb, k32, v32, logf = _fox_prep(proj, fl_proj, f_bias, bsz=bsz, seq_len=length, tp=t['t_prep'])
        y_fox = _fox_prompt(qa, ka, vb, bsz=bsz, seq_len=length, t=t['t_fox'], ts=t['ts_fox'],
                            ts_diag=t['ts_fox_diag'])
        logf_new = logf.reshape(bsz, length, LANES)[:, :, :FOX_HEADS]
        k_new, v_new = k32.reshape(head_shape), v32.reshape(head_shape)
    else:
        proj3 = proj.reshape(bsz, length, -1)
        k_new = proj3[:, :, FOX_WIDTH:2 * FOX_WIDTH].reshape(head_shape)
        v_new = proj3[:, :, 2 * FOX_WIDTH:3 * FOX_WIDTH].reshape(head_shape)
        past = c_k.shape[1]
        pairs = bsz * FOX_HEADS
        assert pairs <= LANES and length <= LANES
        to_lanes = lambda a, rows: jnp.pad(jnp.swapaxes(a, 0, 1).reshape(a.shape[1], pairs),
                                           ((0, rows - a.shape[1]), (0, LANES - pairs)))
        from_lanes = lambda a: jnp.swapaxes(a[:length, :pairs].reshape(length, bsz, FOX_HEADS), 0, 1)
        cache_lf = to_lanes(c_logf.astype(F32), past)
        fl_rows = to_lanes(fl_proj.reshape(bsz, length, LANES)[:, :, :FOX_HEADS], LANES)
        bias_lanes = jnp.pad(jnp.tile(p['fox_f_bias'][0].astype(F32), bsz), (0, LANES - pairs)).reshape(1, LANES)
        cum_t_cache, lf_rows, cum_rows, cum_t_new = _decode_cum(cache_lf, fl_rows, bias_lanes, c=t['c_cum'])
        logf_new = from_lanes(lf_rows)
        cq = jnp.broadcast_to(cum_rows[:length, :pairs].T[:, :, None], (pairs, length, LANES))
        y_fox = _fox_decode(proj, c_k, c_v, cq, cum_t_cache[:pairs, None, :], cum_t_new[:pairs, None, :],
                            bsz=bsz, lq=length, tk=t['tk_dec'])
    sc_hist = zeros(bsz, SC_WIDTH - 1, SC_DIM) if st_sconv is None else st_sconv
    y_sc, sc_tails = _sconv(proj, p['sconv_w'][0], _hist8(sc_hist), seq_len=length, tm=t['tm_sc'])
    sconv_new = _tails(sc_tails, bsz, length, t['tm_sc'], SC_WIDTH - 1)
    xf = _proj_residual(xf, y_fox, y_sc, wb['cd_out'], tm=t['tm_out'], tn=t['tn_out'])

    ffn_hist1 = zeros(bsz, FFN_CONV - 1, D_FF) if st_ffn is None else st_ffn[1]
    xf, tails, _ = _conv_ffn(xf, p['ffn_norm_w'][1], *wb['ffn1'], p['ffn_conv_w'][1], p['ffn_conv_b'][1],
                             _hist8(ffn_hist1), p['final_norm_w'], layer=0, seq_len=length, tm=t['tm_ffn'],
                             tf=t['tf_ffn'], ts=t['ts_ffn'], final=True)
    ffn_new.append(_tails(tails, bsz, length, t['tm_ffn'], FFN_CONV - 1))

    return (xf.reshape(bsz, length, d), ret_new[None], ssd_new[None], ssd_conv_new[None], k_new[None],
            v_new[None], logf_new[None], sconv_new[None], jnp.stack(ffn_new))


def _largest_divisor(n, cap, multiple=1):
    best = None
    for cand in range(multiple, min(n, cap) + 1, multiple):
        if n % cand == 0:
            best = cand
    assert best is not None, (n, cap, multiple)
    return best


def _tiles(bsz, length, past=None):
    m = bsz * length
    seq_tile = lambda cap: _largest_divisor(length, cap, SUBLANES)
    row_tile = lambda cap: (_largest_divisor(length, cap, SUBLANES) if length >= cap
                            else _largest_divisor(m, cap, length))
    t = dict(
        tm_ab=row_tile(1024), tn_ab=2048, tm_cd=row_tile(1024), tn_cd=2048,
        tm_out=row_tile(1024), tn_out=2048,
        tm_ffn=row_tile(1024), tf_ffn=512, ts_ffn=512,
        tm_sc=row_tile(1024),
        c_ret=seq_tile(256), c_ssd=seq_tile(256),
    )
    if past is None:
        t['t_fox'] = seq_tile(2048)
        t['ts_fox'] = _largest_divisor(t['t_fox'], 256, LANES)
        t['ts_fox_diag'] = _largest_divisor(t['t_fox'], 512, LANES)
        t['t_prep'] = seq_tile(512)
    else:
        t['tk_dec'] = _largest_divisor(past, 1024, LANES)
        t['c_cum'] = _largest_divisor(past, 256, LANES)
    return t


def kernel(x_prompt, x_sample, state_ret, state_ssd, state_ssd_conv, cache_fox_k, cache_fox_v, cache_fox_logf, state_sconv, state_ffn_conv, ab_norm_w, ab_w_in, ret_norm_w, ssd_conv_w, ssd_conv_b, ssd_dt_bias, ssd_A_log, ssd_D, ssd_norm_w, ab_w_out, cd_norm_w, cd_w_in, fox_f_bias, sconv_w, cd_w_out, ffn_norm_w, ffn_w_gate, ffn_w_up, ffn_conv_w, ffn_conv_b, ffn_w_down, final_norm_w):
    p = dict(ab_norm_w=ab_norm_w, ab_w_in=ab_w_in, ret_norm_w=ret_norm_w, ssd_conv_w=ssd_conv_w,
             ssd_conv_b=ssd_conv_b, ssd_dt_bias=ssd_dt_bias, ssd_A_log=ssd_A_log, ssd_D=ssd_D,
             ssd_norm_w=ssd_norm_w, ab_w_out=ab_w_out, cd_norm_w=cd_norm_w, cd_w_in=cd_w_in,
             fox_f_bias=fox_f_bias, sconv_w=sconv_w, cd_w_out=cd_w_out, ffn_norm_w=ffn_norm_w,
             ffn_w_gate=ffn_w_gate, ffn_w_up=ffn_w_up, ffn_conv_w=ffn_conv_w, ffn_conv_b=ffn_conv_b,
             ffn_w_down=ffn_w_down, final_norm_w=final_norm_w)
    assert x_prompt.shape[-1] == D_MODEL and ab_w_in.shape == (1, D_MODEL, AB_MAIN + SSD_HEADS)
    assert cd_w_in.shape == (1, D_MODEL, CD_MAIN + FOX_HEADS) and ffn_w_gate.shape == (2, D_MODEL, D_FF)
    wb = _prep_weights(p)
    bp, lp_, _ = x_prompt.shape
    bs, ls, _ = x_sample.shape
    past = cache_fox_k.shape[2]
    (y_prompt, p_ret, p_ssd, p_ssd_conv, p_fox_k, p_fox_v, p_fox_logf, p_sconv, p_ffn_conv) = _trunk(
        x_prompt, 0, None, None, None, None, None, None, None, None, p, wb, _tiles(bp, lp_))
    (y_sample, s_ret, s_ssd, s_ssd_conv, s_fox_k, s_fox_v, s_fox_logf, s_sconv, s_ffn_conv) = _trunk(
        x_sample, past, state_ret[0], state_ssd[0], state_ssd_conv[0], cache_fox_k[0], cache_fox_v[0],
        cache_fox_logf[0], state_sconv[0], state_ffn_conv, p, wb, _tiles(bs, ls, past))
    return (y_prompt, y_sample, p_ret, s_ret, p_ssd, s_ssd, p_ssd_conv, s_ssd_conv, p_fox_k, s_fox_k,
            p_fox_v, s_fox_v, p_fox_logf, s_fox_logf, p_sconv, s_sconv, p_ffn_conv, s_ffn_conv)
```

```python
import functools
import math

import numpy as np
import jax
import jax.numpy as jnp
from jax import lax
from jax.experimental import pallas as pl
from jax.experimental.pallas import tpu as pltpu

F32 = jnp.float32
BF16 = jnp.bfloat16
EPS = 1e-6
ROPE_BASE = 10000.0
NEG_INF = float("-inf")

D_MODEL = 2048
RET_HEADS, RET_DK, RET_DV = 4, 128, 256
SSD_DINNER, SSD_HEADDIM, SSD_HEADS, SSD_GROUPS, SSD_DSTATE, SSD_CONV = 1024, 64, 16, 2, 128, 4
SSD_CONV_DIM = SSD_DINNER + 2 * SSD_GROUPS * SSD_DSTATE
FOX_HEADS, FOX_HEAD_DIM = 8, 128
FOX_WIDTH = FOX_HEADS * FOX_HEAD_DIM
SC_DIM, SC_WIDTH = 1024, 3
D_FF, FFN_CONV = 5632, 3
AB_MAIN = 2 * RET_HEADS * RET_DK + 2 * RET_HEADS * RET_DV + SSD_DINNER + SSD_CONV_DIM
CD_MAIN = 3 * FOX_WIDTH + 3 * SC_DIM

LANES = 128
SUBLANES = 8
VMEM_LIMIT = 60 * 1024 * 1024


def _cparams(n_axes):
    return pltpu.CompilerParams(dimension_semantics=("arbitrary",) * n_axes,
                                vmem_limit_bytes=VMEM_LIMIT)


def _rms(xf, w):
    return xf * lax.rsqrt(jnp.mean(xf * xf, axis=-1, keepdims=True) + EPS) * w


def _softplus(x):
    return jnp.maximum(x, 0.0) + jnp.log1p(jnp.exp(-jnp.abs(x)))


def _split3(x):
    hi = x.astype(BF16)
    r1 = x - hi.astype(F32)
    mid = r1.astype(BF16)
    lo = (r1 - mid.astype(F32)).astype(BF16)
    return hi, mid, lo


def _widen(x, n):
    return x[:, 0:n] if n <= LANES else jnp.concatenate([x] * (n // LANES), axis=1)


def _dot(a, b):
    return jnp.dot(a, b, preferred_element_type=F32)


def _dot_nt(a, b):
    return lax.dot_general(a, b, (((1,), (1,)), ((), ())), preferred_element_type=F32)


def _dot_tn(a, b):
    return lax.dot_general(a, b, (((0,), (0,)), ((), ())), preferred_element_type=F32)


def _exact_lhs_dot(m_bf16, x):
    hi, mid, lo = _split3(x)
    return _dot(m_bf16, hi) + _dot(m_bf16, mid) + _dot(m_bf16, lo)


def _exact_rhs_dot(x, m_bf16):
    hi, mid, lo = _split3(x)
    return _dot(hi, m_bf16) + _dot(mid, m_bf16) + _dot(lo, m_bf16)


def _conv_rows(ext_ref, x, prev8, w_ref, width, rows, w_cols=slice(None)):
    ext_ref[0:SUBLANES, :] = prev8
    ext_ref[SUBLANES:SUBLANES + rows, :] = x
    out = None
    for j in range(width):
        off = SUBLANES - (width - 1) + j
        term = ext_ref[off:off + rows, :] * w_ref[j:j + 1, w_cols]
        out = term if out is None else out + term
    return out


def _seq_tiling(seq_len, tile_rows):
    if seq_len >= tile_rows:
        assert seq_len % tile_rows == 0
        return tile_rows, 1, seq_len // tile_rows
    assert tile_rows % seq_len == 0 and seq_len % SUBLANES == 0
    return seq_len, tile_rows // seq_len, 1


def _norm_matmul_kernel(x_ref, nw_ref, w_ref, ws_ref, o_ref, os_ref, h_ref, *, n_tiles, tn, last_w):
    j = pl.program_id(1)

    @pl.when(j == 0)
    def _():
        h_ref[...] = _rms(x_ref[...], nw_ref[...]).astype(BF16)
        os_ref[...] = _dot(h_ref[...], ws_ref[...])

    if last_w == tn:
        o_ref[...] = _dot(h_ref[...], w_ref[...])
    else:
        @pl.when(j < n_tiles - 1)
        def _():
            o_ref[...] = _dot(h_ref[...], w_ref[...])

        @pl.when(j == n_tiles - 1)
        def _():
            o_ref[:, 0:last_w] = _dot(h_ref[...], w_ref[:, 0:last_w])


def _norm_matmul(x, norm_w, w, w_small, *, n, tm, tn):
    m, d = x.shape
    n_tiles = pl.cdiv(n, tn)
    last_w = n - (n_tiles - 1) * tn
    assert m % tm == 0 and last_w % LANES == 0 and n <= w.shape[1] and w_small.shape == (d, LANES)
    return pl.pallas_call(
        functools.partial(_norm_matmul_kernel, n_tiles=n_tiles, tn=tn, last_w=last_w),
        grid=(m // tm, n_tiles),
        in_specs=[pl.BlockSpec((tm, d), lambda i, j: (i, 0)),
                  pl.BlockSpec((1, d), lambda i, j: (0, 0)),
                  pl.BlockSpec((d, tn), lambda i, j: (0, j)),
                  pl.BlockSpec((d, LANES), lambda i, j: (0, 0))],
        out_specs=[pl.BlockSpec((tm, tn), lambda i, j: (i, j)),
                   pl.BlockSpec((tm, LANES), lambda i, j: (i, 0))],
        out_shape=[jax.ShapeDtypeStruct((m, n), F32),
                   jax.ShapeDtypeStruct((m, LANES), F32)],
        scratch_shapes=[pltpu.VMEM((tm, d), BF16)],
        compiler_params=_cparams(2),
        name="norm_in_proj",
    )(x, norm_w.reshape(1, d), w, w_small)


def _proj_res_kernel(x_ref, a_ref, b_ref, wa_ref, wb_ref, o_ref):
    acc = _dot(a_ref[...], wa_ref[...])
    acc = acc + _dot(b_ref[...], wb_ref[...])
    o_ref[...] = x_ref[...] + acc


def _proj_residual(x, a, b, w, *, tm, tn):
    m, d = x.shape
    ka, kb = a.shape[1], b.shape[1]
    assert m % tm == 0 and d % tn == 0 and ka == kb and w.shape == (ka + kb, d)
    return pl.pallas_call(
        _proj_res_kernel,
        grid=(m // tm, d // tn),
        in_specs=[pl.BlockSpec((tm, tn), lambda i, j: (i, j)),
                  pl.BlockSpec((tm, ka), lambda i, j: (i, 0)),
                  pl.BlockSpec((tm, kb), lambda i, j: (i, 0)),
                  pl.BlockSpec((ka, tn), lambda i, j: (0, j)),
                  pl.BlockSpec((kb, tn), lambda i, j: (1, j))],
        out_specs=pl.BlockSpec((tm, tn), lambda i, j: (i, j)),
        out_shape=jax.ShapeDtypeStruct((m, d), F32),
        compiler_params=_cparams(2),
        name="out_proj_residual",
    )(x, a, b, w, w)


def _ffn_kernel(*refs, tm, rows, spt, tpb, nf, ts, nsub, nsub_last, final, n_cast):
    x_ref, nw_ref, wg_ref, wu_ref, wd_ref, cw_ref, cb_ref, hist_ref, fw_ref = refs[:9]
    cast_in = refs[9:9 + n_cast]
    o_ref, tail_ref = refs[9 + n_cast:11 + n_cast]
    cast_out = refs[11 + n_cast:11 + 2 * n_cast]
    h_ref, carry_ref, ext_ref = refs[11 + 2 * n_cast:]
    i = pl.program_id(0)
    f = pl.program_id(1)

    for src_ref, dst_ref in zip(cast_in, cast_out):
        dst_ref[...] = src_ref[...].astype(BF16)

    @pl.when(f == 0)
    def _():
        xf = x_ref[...]
        h_ref[...] = _rms(xf, nw_ref[...]).astype(BF16)
        o_ref[...] = xf

    first = (i % tpb) == 0

    def sub_block(sb):
        cols = slice(sb * ts, (sb + 1) * ts)
        h = h_ref[...]
        a = _dot(h, wg_ref[:, cols])
        u = _dot(h, wu_ref[:, cols])
        convs = []
        for s in range(spt):
            a_s = a[s * rows:(s + 1) * rows]
            if tpb == 1:
                prev = hist_ref[s, :, cols]
            else:
                prev = jnp.where(first, hist_ref[s, :, cols], carry_ref[f * nsub + sb])
            convs.append(_conv_rows(ext_ref, a_s, prev, cw_ref, FFN_CONV, rows, cols))
            tail_ref[s, :, cols] = a_s[rows - SUBLANES:rows]
        if tpb > 1:
            carry_ref[f * nsub + sb] = a[tm - SUBLANES:tm]
        conv = convs[0] if spt == 1 else jnp.concatenate(convs, axis=0)
        act = (jax.nn.silu(conv + cb_ref[:, cols]) * u).astype(BF16)
        o_ref[...] += _dot(act, wd_ref[cols, :])

    if nsub_last == nsub:
        for sb in range(nsub):
            sub_block(sb)
    else:
        @pl.when(f < nf - 1)
        def _():
            for sb in range(nsub):
                sub_block(sb)

        @pl.when(f == nf - 1)
        def _():
            for sb in range(nsub_last):
                sub_block(sb)

    if final:
        @pl.when(f == nf - 1)
        def _():
            o_ref[...] = _rms(o_ref[...], fw_ref[...])


def _conv_ffn(x, norm_w, wg, wu, wd, conv_w, conv_b, hist8, final_w, *, layer, seq_len, tm, tf, ts, final,
              cast_next=None):
    m, d = x.shape
    ff = wg.shape[2]
    assert m % tm == 0 and tf % ts == 0 and ff % ts == 0
    rows, spt, tpb = _seq_tiling(seq_len, tm)
    nm, nf = m // tm, pl.cdiv(ff, tf)
    nsub = tf // ts
    nsub_last = (ff - (nf - 1) * tf) // ts
    hist_map = (lambda i, f: (i // tpb, 0, f)) if spt == 1 else (lambda i, f: (i, 0, f))
    cast_in, cast_specs, cast_src_specs = [], [], []
    if cast_next is not None:
        assert d % nm == 0 and (d // nm) % LANES == 0 and ff % nf == 0 and (ff // nf) % LANES == 0
        dr, fc = d // nm, ff // nf
        cast_layer, cast_in = cast_next[0], list(cast_next[1:])
        cast_specs = [pl.BlockSpec((dr, fc), lambda i, f: (i, f)),
                      pl.BlockSpec((dr, fc), lambda i, f: (i, f)),
                      pl.BlockSpec((fc, dr), lambda i, f: (f, i))]
        cast_src_specs = [pl.BlockSpec((None, dr, fc), lambda i, f: (cast_layer, i, f)),
                          pl.BlockSpec((None, dr, fc), lambda i, f: (cast_layer, i, f)),
                          pl.BlockSpec((None, fc, dr), lambda i, f: (cast_layer, f, i))]
    kern = functools.partial(_ffn_kernel, tm=tm, rows=rows, spt=spt, tpb=tpb, nf=nf, ts=ts, nsub=nsub,
                             nsub_last=nsub_last, final=final, n_cast=len(cast_in))
    results = pl.pallas_call(
        kern,
        grid=(nm, nf),
        in_specs=[pl.BlockSpec((tm, d), lambda i, f: (i, 0)),
                  pl.BlockSpec((1, d), lambda i, f: (0, 0)),
                  pl.BlockSpec((None, d, tf), lambda i, f: (layer, 0, f)),
                  pl.BlockSpec((None, d, tf), lambda i, f: (layer, 0, f)),
                  pl.BlockSpec((None, tf, d), lambda i, f: (layer, f, 0)),
                  pl.BlockSpec((FFN_CONV, tf), lambda i, f: (0, f)),
                  pl.BlockSpec((1, tf), lambda i, f: (0, f)),
                  pl.BlockSpec((spt, SUBLANES, tf), hist_map),
                  pl.BlockSpec((1, d), lambda i, f: (0, 0))] + cast_src_specs,
        out_specs=[pl.BlockSpec((tm, d), lambda i, f: (i, 0)),
                   pl.BlockSpec((spt, SUBLANES, tf), lambda i, f: (i, 0, f))] + cast_specs,
        out_shape=[jax.ShapeDtypeStruct((m, d), F32),
                   jax.ShapeDtypeStruct((nm * spt, SUBLANES, ff), F32)]
                  + [jax.ShapeDtypeStruct(w.shape[1:], BF16) for w in cast_in],
        scratch_shapes=[pltpu.VMEM((tm, d), BF16),
                        pltpu.VMEM((nf * nsub, SUBLANES, ts), F32),
                        pltpu.VMEM((rows + SUBLANES, ts), F32)],
        compiler_params=_cparams(2),
        name="conv_ffn",
    )(x, norm_w.reshape(1, d), wg, wu, wd, conv_w, conv_b.reshape(1, ff), hist8, final_w.reshape(1, d), *cast_in)
    return results[0], results[1], tuple(results[2:])


def _retention_kernel(*refs, c, side_cast):
    q_ref, k_ref, v_ref, g_ref, cos_ref, sin_ref, st_ref, nw_ref = refs[:8]
    y_ref, so_ref = refs[8 + side_cast:10 + side_cast]
    ci = pl.program_id(1)

    if side_cast:
        w = refs[8][...]
        w_main_ref, w_gate_ref = refs[11:13]
        f0 = 3 * FOX_WIDTH
        w_main_ref[:, 0:f0] = w[:, 0:f0].astype(BF16)
        w_main_ref[:, f0:CD_MAIN] = w[:, f0 + FOX_HEADS:CD_MAIN + FOX_HEADS].astype(BF16)
        gate = jnp.concatenate([w[:, f0:f0 + FOX_HEADS], jnp.zeros((w.shape[0], LANES - FOX_HEADS), F32)], axis=1)
        w_gate_ref[...] = gate.astype(BF16)

    @pl.when(ci == 0)
    def _():
        so_ref[...] = st_ref[...]

    cos = cos_ref[...]
    sin = sin_ref[...]
    ii = lax.broadcasted_iota(jnp.int32, (c, c), 0)
    jj = lax.broadcasted_iota(jnp.int32, (c, c), 1)
    diff = (ii - jj).astype(F32)
    causal = ii >= jj
    ridx = lax.broadcasted_iota(jnp.int32, (c, 1), 0).astype(F32)
    for h in range(RET_HEADS):
        lg = math.log1p(-(2.0 ** (-5.0 - h)))
        q = q_ref[:, h * RET_DK:(h + 1) * RET_DK]
        k = k_ref[:, h * RET_DK:(h + 1) * RET_DK]
        v = v_ref[:, h * RET_DV:(h + 1) * RET_DV]
        qr = q * cos + pltpu.roll(q, RET_DK // 2, 1) * sin
        kr = (k * cos + pltpu.roll(k, RET_DK // 2, 1) * sin) * (RET_DK ** -0.5)
        qb = qr.astype(BF16)
        kb = kr.astype(BF16)
        vb = v.astype(BF16)
        decay = jnp.exp(jnp.where(causal, diff * lg, NEG_INF))
        inner = jnp.exp((ridx + 1.0) * lg)
        sdecay = jnp.exp((c - 1.0 - ridx) * lg)
        s = so_ref[0, h]
        scores = _dot_nt(qb, kb) * decay
        y = _dot(scores.astype(BF16), vb)
        y = y + _dot(qb, s.astype(BF16)) * inner
        kd = (kr * sdecay).astype(BF16)
        so_ref[0, h] = math.exp(c * lg) * s + _dot_tn(kd, vb)
        mu = jnp.mean(y, axis=-1, keepdims=True)
        yc = y - mu
        var = jnp.mean(yc * yc, axis=-1, keepdims=True)
        yn = yc * lax.rsqrt(var + EPS) * nw_ref[:, h * RET_DV:(h + 1) * RET_DV]
        g = g_ref[:, h * RET_DV:(h + 1) * RET_DV]
        y_ref[:, h * RET_DV:(h + 1) * RET_DV] = (jax.nn.silu(g) * yn).astype(BF16)


def _retention(proj, cosf, sinf, state, norm_w, *, bsz, seq_len, c, cd_w_in=None):
    m = proj.shape[0]
    nc = seq_len // c
    assert seq_len % c == 0
    qk_w = RET_HEADS * RET_DK
    v_w = RET_HEADS * RET_DV
    row = lambda b, ci: b * nc + ci
    side_in, side_src, side_dst, side_shapes = [], [], [], []
    if cd_w_in is not None and cd_w_in.shape[0] % (bsz * nc * 16) == 0:
        d, wide = cd_w_in.shape
        rb = d // (bsz * nc)
        side_in = [cd_w_in]
        side_src = [pl.BlockSpec((rb, wide), lambda b, ci: (row(b, ci), 0))]
        side_dst = [pl.BlockSpec((rb, CD_MAIN), lambda b, ci: (row(b, ci), 0)),
                    pl.BlockSpec((rb, LANES), lambda b, ci: (row(b, ci), 0))]
        side_shapes = [jax.ShapeDtypeStruct((d, CD_MAIN), BF16), jax.ShapeDtypeStruct((d, LANES), BF16)]
    results = pl.pallas_call(
        functools.partial(_retention_kernel, c=c, side_cast=len(side_in)),
        grid=(bsz, nc),
        in_specs=[pl.BlockSpec((c, qk_w), lambda b, ci: (row(b, ci), 0)),
                  pl.BlockSpec((c, qk_w), lambda b, ci: (row(b, ci), 1)),
                  pl.BlockSpec((c, v_w), lambda b, ci: (row(b, ci), 1)),
                  pl.BlockSpec((c, v_w), lambda b, ci: (row(b, ci), 2)),
                  pl.BlockSpec((c, RET_DK), lambda b, ci: (ci, 0)),
                  pl.BlockSpec((c, RET_DK), lambda b, ci: (ci, 0)),
                  pl.BlockSpec((1, RET_HEADS, RET_DK, RET_DV), lambda b, ci: (b, 0, 0, 0)),
                  pl.BlockSpec((1, v_w), lambda b, ci: (0, 0))] + side_src,
        out_specs=[pl.BlockSpec((c, v_w), lambda b, ci: (row(b, ci), 0)),
                   pl.BlockSpec((1, RET_HEADS, RET_DK, RET_DV), lambda b, ci: (b, 0, 0, 0))] + side_dst,
        out_shape=[jax.ShapeDtypeStruct((m, v_w), BF16),
                   jax.ShapeDtypeStruct(state.shape, F32)] + side_shapes,
        compiler_params=_cparams(2),
        name="retention",
    )(proj, proj, proj, proj, cosf, sinf, state, norm_w.reshape(1, v_w), *side_in)
    return results[0], results[1], (tuple(results[2:]) if side_in else None)


def _ssd_kernel(*refs, c, nc, n_cast):
    (z_ref, xs_ref, bc_ref, dt_ref, hx_ref, hbc_ref, st_ref, cwx_ref, cwbc_ref, cbx_ref, cbbc_ref,
     dtb_ref, alog_ref, dsk_ref, nw_ref, tri_ref, exp_ref) = refs[:17]
    cast_in = refs[17:17 + n_cast]
    y_ref, so_ref = refs[17 + n_cast:19 + n_cast]
    cast_out = refs[19 + n_cast:19 + 2 * n_cast]
    st_scr, cx_scr, cbc_scr, extx_scr, extbc_scr, yh_scr, xs_scr = refs[19 + 2 * n_cast:]
    ci = pl.program_id(1)

    for src_ref, dst_ref in zip(cast_in, cast_out):
        dst_ref[...] = src_ref[...].astype(BF16)
    gw = SSD_DINNER // SSD_GROUPS
    hpg = SSD_HEADS // SSD_GROUPS

    @pl.when(ci == 0)
    def _():
        st_scr[...] = st_ref[0].T
        cx_scr[...] = hx_ref[0]
        cbc_scr[...] = hbc_ref[0]

    xs_raw = xs_ref[...]
    bc_raw = bc_ref[...]
    xs_scr[...] = jax.nn.silu(_conv_rows(extx_scr, xs_raw, cx_scr[...], cwx_ref, SSD_CONV, c) + cbx_ref[...])
    bcm = jax.nn.silu(_conv_rows(extbc_scr, bc_raw, cbc_scr[...], cwbc_ref, SSD_CONV, c) + cbbc_ref[...])
    cx_scr[...] = xs_raw[c - SUBLANES:c]
    cbc_scr[...] = bc_raw[c - SUBLANES:c]

    tri = tri_ref[...]
    dt = _softplus(dt_ref[...] + dtb_ref[...])
    a = -jnp.exp(alog_ref[...])
    acs = _exact_lhs_dot(tri, dt * a)
    acs_t = acs.T
    acs_last = acs[c - 1:c, :]
    exp_acs = jnp.exp(acs)
    to_end = jnp.exp(acs_last - acs)
    chunk_dec = jnp.exp(acs_last)

    ii = lax.broadcasted_iota(jnp.int32, (c, c), 0)
    jj = lax.broadcasted_iota(jnp.int32, (c, c), 1)
    causal = ii >= jj
    nb = SSD_GROUPS * SSD_DSTATE
    for g in range(SSD_GROUPS):
        cols = slice(g * gw, (g + 1) * gw)
        expand = exp_ref[:, cols]
        xdt = xs_scr[:, cols] * _exact_rhs_dot(dt, expand)
        xdt_b = xdt.astype(BF16)
        xend_b = (xdt * _exact_rhs_dot(to_end, expand)).astype(BF16)
        b_g = bcm[:, g * SSD_DSTATE:(g + 1) * SSD_DSTATE].astype(BF16)
        c_g = bcm[:, nb + g * SSD_DSTATE:nb + (g + 1) * SSD_DSTATE].astype(BF16)
        cb = _dot_nt(c_g, b_g)
        s_g = st_scr[:, cols]
        y_state = _dot(c_g, s_g.astype(BF16)) * _exact_rhs_dot(exp_acs, expand)
        for r in range(hpg):
            hh = g * hpg + r
            seg = acs[:, hh:hh + 1] - acs_t[hh:hh + 1, :]
            lmat = jnp.exp(jnp.where(causal, seg, NEG_INF))
            mm = (cb * lmat).astype(BF16)
            head = slice(r * SSD_HEADDIM, (r + 1) * SSD_HEADDIM)
            yh_scr[:, hh * SSD_HEADDIM:(hh + 1) * SSD_HEADDIM] = _dot(mm, xdt_b[:, head]) + y_state[:, head]
        upd = _dot_tn(b_g, xend_b)
        st_scr[:, cols] = _exact_rhs_dot(chunk_dec, expand) * s_g + upd

    y = yh_scr[...] + dsk_ref[...] * xs_scr[...]
    z = z_ref[...]
    y_ref[...] = _rms(y * jax.nn.silu(z), nw_ref[...]).astype(BF16)

    @pl.when(ci == nc - 1)
    def _():
        so_ref[0] = st_scr[...].T


def _row_block_cast_specs(w3, layer, n_steps, step_of):
    _, r, cols = w3.shape
    for per in (1, 2, 4, 8):
        if (r * per) % n_steps == 0 and (r * per // n_steps) % 16 == 0:
            rb = r * per // n_steps
            src = pl.BlockSpec((None, rb, cols), lambda *g: (layer, step_of(*g) // per, 0))
            dst = pl.BlockSpec((rb, cols), lambda *g: (step_of(*g) // per, 0))
            return src, dst, jax.ShapeDtypeStruct((r, cols), BF16)
    return None


def _ssd(proj, dt_proj, hist8, state, conv_w, conv_b, dt_bias, a_log, d_skip, norm_w, *, bsz, seq_len, c,
         cast=None):
    m = proj.shape[0]
    nc = seq_len // c
    assert seq_len % c == 0
    row = lambda b, ci: b * nc + ci
    cast_in, cast_src, cast_dst, cast_shapes = [], [], [], []
    if cast is not None:
        specs = [_row_block_cast_specs(w, cast[0], bsz * nc, row) for w in cast[1]]
        if all(s is not None for s in specs):
            cast_in = list(cast[1])
            cast_src, cast_dst, cast_shapes = (list(x) for x in zip(*specs))
    const2 = lambda b, ci: (0, 0)
    di, bcw = SSD_DINNER, 2 * SSD_GROUPS * SSD_DSTATE
    tri = jnp.asarray(np.tril(np.ones((c, c), np.float32)), BF16)
    expand = np.zeros((LANES, di), np.float32)
    for h in range(SSD_HEADS):
        expand[h, h * SSD_HEADDIM:(h + 1) * SSD_HEADDIM] = 1.0
    expand = jnp.asarray(expand, BF16)
    pad_row = lambda v: jnp.pad(v.astype(F32), (0, LANES - v.shape[0])).reshape(1, LANES)
    st2 = state.reshape(bsz, di, SSD_DSTATE)
    results = pl.pallas_call(
        functools.partial(_ssd_kernel, c=c, nc=nc, n_cast=len(cast_in)),
        grid=(bsz, nc),
        in_specs=[pl.BlockSpec((c, di), lambda b, ci: (row(b, ci), 3)),
                  pl.BlockSpec((c, di), lambda b, ci: (row(b, ci), 4)),
                  pl.BlockSpec((c, bcw), lambda b, ci: (row(b, ci), 10)),
                  pl.BlockSpec((c, LANES), lambda b, ci: (row(b, ci), 0)),
                  pl.BlockSpec((1, SUBLANES, di), lambda b, ci: (b, 0, 0)),
                  pl.BlockSpec((1, SUBLANES, bcw), lambda b, ci: (b, 0, 2)),
                  pl.BlockSpec((1, di, SSD_DSTATE), lambda b, ci: (b, 0, 0)),
                  pl.BlockSpec((SSD_CONV, di), const2),
                  pl.BlockSpec((SSD_CONV, bcw), lambda b, ci: (0, 2)),
                  pl.BlockSpec((1, di), const2),
                  pl.BlockSpec((1, bcw), lambda b, ci: (0, 2)),
                  pl.BlockSpec((1, LANES), const2),
                  pl.BlockSpec((1, LANES), const2),
                  pl.BlockSpec((1, di), const2),
                  pl.BlockSpec((1, di), const2),
                  pl.BlockSpec((c, c), const2),
                  pl.BlockSpec((LANES, di), const2)] + cast_src,
        out_specs=[pl.BlockSpec((c, di), lambda b, ci: (row(b, ci), 0)),
                   pl.BlockSpec((1, di, SSD_DSTATE), lambda b, ci: (b, 0, 0))] + cast_dst,
        out_shape=[jax.ShapeDtypeStruct((m, di), BF16),
                   jax.ShapeDtypeStruct(st2.shape, F32)] + cast_shapes,
        scratch_shapes=[pltpu.VMEM((SSD_DSTATE, di), F32),
                        pltpu.VMEM((SUBLANES, di), F32),
                        pltpu.VMEM((SUBLANES, bcw), F32),
                        pltpu.VMEM((c + SUBLANES, di), F32),
                        pltpu.VMEM((c + SUBLANES, bcw), F32),
                        pltpu.VMEM((c, di), F32),
                        pltpu.VMEM((c, di), F32)],
        compiler_params=_cparams(2),
        name="ssd",
    )(proj, proj, proj, dt_proj, hist8, hist8, st2,
      conv_w, conv_w, conv_b.reshape(1, -1), conv_b.reshape(1, -1),
      pad_row(dt_bias), pad_row(a_log), jnp.repeat(d_skip.astype(F32), SSD_HEADDIM).reshape(1, di),
      norm_w.reshape(1, di), tri, expand, *cast_in)
    return results[0], results[1].reshape(state.shape), (list(results[2:]) if cast_in else None)


def _decode_cum_kernel(lfc_ref, fl_ref, b_ref, tri_ref, cumt_c_ref, lfn_ref, cumn_ref, cumt_n_ref,
                       carry_ref, *, c, ncb):
    j = pl.program_id(0)

    @pl.when(j == 0)
    def _():
        carry_ref[...] = jnp.zeros_like(carry_ref)

    @pl.when(j < ncb)
    def _():
        cum = _exact_lhs_dot(tri_ref[...], lfc_ref[...]) + carry_ref[...]
        carry_ref[...] = cum[c - 1:c, :]
        cumt_c_ref[...] = cum.T

    @pl.when(j == ncb)
    def _():
        lf = -_softplus(-(fl_ref[...] + b_ref[...]))
        lfn_ref[...] = lf
        cum = _exact_lhs_dot(tri_ref[0:LANES, 0:LANES], lf) + carry_ref[...]
        cumn_ref[...] = cum
        cumt_n_ref[...] = cum.T


def _decode_cum(cache_lf, fl_new, bias, *, c):
    past = cache_lf.shape[0]
    assert past % c == 0 and c % LANES == 0 and fl_new.shape == (LANES, LANES)
    ncb = past // c
    tri = jnp.asarray(np.tril(np.ones((c, c), np.float32)), BF16)
    blk = lambda j: jnp.minimum(j, ncb - 1)
    sq = jax.ShapeDtypeStruct((LANES, LANES), F32)
    return pl.pallas_call(
        functools.partial(_decode_cum_kernel, c=c, ncb=ncb),
        grid=(ncb + 1,),
        in_specs=[pl.BlockSpec((c, LANES), lambda j: (blk(j), 0)),
                  pl.BlockSpec((LANES, LANES), lambda j: (0, 0)),
                  pl.BlockSpec((1, LANES), lambda j: (0, 0)),
                  pl.BlockSpec((c, c), lambda j: (0, 0))],
        out_specs=[pl.BlockSpec((LANES, c), lambda j: (0, blk(j))),
                   pl.BlockSpec((LANES, LANES), lambda j: (0, 0)),
                   pl.BlockSpec((LANES, LANES), lambda j: (0, 0)),
                   pl.BlockSpec((LANES, LANES), lambda j: (0, 0))],
        out_shape=[jax.ShapeDtypeStruct((LANES, past), F32), sq, sq, sq],
        scratch_shapes=[pltpu.VMEM((1, LANES), F32)],
        compiler_params=_cparams(1),
        name="decode_logf_cumsum",
    )(cache_lf, fl_new, bias, tri)


FOX_AUG = 2 * FOX_HEAD_DIM
N_BIAS_PIECES = 3


def _fox_prep_kernel(q_ref, k_ref, v_ref, fl_ref, fb_ref, tri_ref, place_ref, ones_ref,
                     qa_ref, ka_ref, k32_ref, v32_ref, lf_ref, carry_ref, *, tp):
    @pl.when(pl.program_id(1) == 0)
    def _():
        carry_ref[...] = jnp.zeros_like(carry_ref)

    lf = -_softplus(-(fl_ref[...] + fb_ref[...]))
    lf_ref[...] = lf
    cum = _exact_lhs_dot(tri_ref[...], lf) + carry_ref[...]
    carry_ref[...] = cum[tp - 1:tp, :]
    pieces = _split3(cum * (FOX_HEAD_DIM ** 0.5))
    n = N_BIAS_PIECES
    aug_q = ones_ref[0:1, :] + sum(_dot(pieces[r], place_ref[r]) for r in range(n))
    aug_k = ones_ref[1:2, :] - sum(_dot(pieces[r], place_ref[n + r]) for r in range(n))
    for h in range(FOX_HEADS):
        src = slice(h * FOX_HEAD_DIM, (h + 1) * FOX_HEAD_DIM)
        feat = slice(h * FOX_AUG, h * FOX_AUG + FOX_HEAD_DIM)
        bias = slice(h * FOX_AUG + FOX_HEAD_DIM, (h + 1) * FOX_AUG)
        qa_ref[:, feat] = q_ref[:, src].astype(BF16)
        qa_ref[:, bias] = aug_q[:, src].astype(BF16)
        ka_ref[:, feat] = k_ref[:, src].astype(BF16)
        ka_ref[:, bias] = aug_k[:, src].astype(BF16)
    k = k_ref[...]
    v = v_ref[...]
    k32_ref[...] = pltpu.einshape("m(hd)->mhd", k, h=FOX_HEADS)
    v32_ref[...] = pltpu.einshape("m(hd)->mhd", v, h=FOX_HEADS)


def _fox_prep(proj, fl_proj, f_bias, *, bsz, seq_len, tp):
    m = proj.shape[0]
    nt = seq_len // tp
    assert seq_len % tp == 0
    w = FOX_WIDTH
    tri = jnp.asarray(np.tril(np.ones((tp, tp), np.float32)), BF16)
    n = N_BIAS_PIECES
    place = np.zeros((2 * n, LANES, w), np.float32)
    ones = np.zeros((SUBLANES, w), np.float32)
    for h in range(FOX_HEADS):
        for r in range(2 * n):
            place[r, h, h * FOX_HEAD_DIM + r] = 1.0
        ones[0, h * FOX_HEAD_DIM + n:h * FOX_HEAD_DIM + 2 * n] = 1.0
        ones[1, h * FOX_HEAD_DIM:h * FOX_HEAD_DIM + n] = 1.0
    row = lambda b, ti: (b * nt + ti, 0)
    const2 = lambda b, ti: (0, 0)
    return pl.pallas_call(
        functools.partial(_fox_prep_kernel, tp=tp),
        grid=(bsz, nt),
        in_specs=[pl.BlockSpec((tp, w), lambda b, ti: (b * nt + ti, 0)),
                  pl.BlockSpec((tp, w), lambda b, ti: (b * nt + ti, 1)),
                  pl.BlockSpec((tp, w), lambda b, ti: (b * nt + ti, 2)),
                  pl.BlockSpec((tp, LANES), row),
                  pl.BlockSpec((1, LANES), const2),
                  pl.BlockSpec((tp, tp), const2),
                  pl.BlockSpec((2 * n, LANES, w), lambda b, ti: (0, 0, 0)),
                  pl.BlockSpec((SUBLANES, w), const2)],
        out_specs=[pl.BlockSpec((tp, FOX_HEADS * FOX_AUG), row),
                   pl.BlockSpec((tp, FOX_HEADS * FOX_AUG), row),
                   pl.BlockSpec((tp, FOX_HEADS, FOX_HEAD_DIM), lambda b, ti: (b * nt + ti, 0, 0)),
                   pl.BlockSpec((tp, FOX_HEADS, FOX_HEAD_DIM), lambda b, ti: (b * nt + ti, 0, 0)),
                   pl.BlockSpec((tp, LANES), row)],
        out_shape=[jax.ShapeDtypeStruct((m, FOX_HEADS * FOX_AUG), BF16),
                   jax.ShapeDtypeStruct((m, FOX_HEADS * FOX_AUG), BF16),
                   jax.ShapeDtypeStruct((m, FOX_HEADS, FOX_HEAD_DIM), F32),
                   jax.ShapeDtypeStruct((m, FOX_HEADS, FOX_HEAD_DIM), F32),
                   jax.ShapeDtypeStruct((m, LANES), F32)],
        scratch_shapes=[pltpu.VMEM((1, LANES), F32)],
        compiler_params=_cparams(2),
        name="fox_prep",
    )(proj, proj, proj, fl_proj, f_bias, tri, jnp.asarray(place, BF16), jnp.asarray(ones, F32))


def _fox_kernel(qi_ref, ki_ref, q_ref, k_ref, v_ref, o_ref, m_ref, acc_ref, va_ref, *, t, ts, ts_diag):
    step = pl.program_id(2)
    qi = qi_ref[step]
    ki = ki_ref[step]
    to_log2 = (FOX_HEAD_DIM ** -0.5) * math.log2(math.e)
    hd = FOX_HEAD_DIM

    @pl.when(ki == 0)
    def _():
        m_ref[...] = jnp.full_like(m_ref, NEG_INF)
        acc_ref[...] = jnp.zeros_like(acc_ref)
        va_ref[:, hd:2 * hd] = jnp.ones((t, hd), BF16)

    va_ref[:, 0:hd] = v_ref[...].astype(BF16)

    def update(diagonal):
        tc = ts_diag if diagonal else ts
        for r in range(t // tc):
            rows = slice(r * tc, (r + 1) * tc)
            nk = (r + 1) * tc if diagonal else t
            s = _dot_nt(q_ref[rows, :], k_ref[0:nk, :])
            if diagonal:
                ri = lax.broadcasted_iota(jnp.int32, (tc, nk), 0) + r * tc
                ci = lax.broadcasted_iota(jnp.int32, (tc, nk), 1)
                s = jnp.where(ci <= ri, s, NEG_INF)
            m_old = m_ref[rows, :]
            m_new = jnp.maximum(m_old, jnp.max(s, axis=1, keepdims=True))
            m_ref[rows, :] = m_new
            alpha = jnp.exp2((m_old - m_new) * to_log2)
            p = jnp.exp2((s - _widen(m_new, nk)) * to_log2)
            acc_ref[rows, :] = _widen(alpha, 2 * hd) * acc_ref[rows, :] + _dot(p.astype(BF16), va_ref[0:nk, :])

    @pl.when(ki < qi)
    def _():
        update(False)

    @pl.when(ki == qi)
    def _():
        update(True)
        o_ref[...] = (acc_ref[:, 0:hd] / acc_ref[:, hd:2 * hd]).astype(BF16)


def _fox_prompt(qa, ka, proj, *, bsz, seq_len, t, ts, ts_diag):
    m = qa.shape[0]
    nq = seq_len // t
    assert seq_len % t == 0 and t % ts == 0 and t % ts_diag == 0
    pairs = [(qi, ki) for qi in range(nq) for ki in range(qi + 1)]
    qi_tab = jnp.asarray([p[0] for p in pairs], jnp.int32)
    ki_tab = jnp.asarray([p[1] for p in pairs], jnp.int32)
    grid_spec = pltpu.PrefetchScalarGridSpec(
        num_scalar_prefetch=2,
        grid=(bsz, FOX_HEADS, len(pairs)),
        in_specs=[pl.BlockSpec((t, FOX_AUG), lambda b, h, s, qi, ki: (b * nq + qi[s], h)),
                  pl.BlockSpec((t, FOX_AUG), lambda b, h, s, qi, ki: (b * nq + ki[s], h)),
                  pl.BlockSpec((t, FOX_HEAD_DIM), lambda b, h, s, qi, ki: (b * nq + ki[s], 2 * FOX_HEADS + h))],
        out_specs=pl.BlockSpec((t, FOX_HEAD_DIM), lambda b, h, s, qi, ki: (b * nq + qi[s], h)),
        scratch_shapes=[pltpu.VMEM((t, LANES), F32), pltpu.VMEM((t, 2 * FOX_HEAD_DIM), F32),
                        pltpu.VMEM((t, 2 * FOX_HEAD_DIM), BF16)],
    )
    return pl.pallas_call(
        functools.partial(_fox_kernel, t=t, ts=ts, ts_diag=ts_diag),
        grid_spec=grid_spec,
        out_shape=jax.ShapeDtypeStruct((m, FOX_WIDTH), BF16),
        compiler_params=_cparams(3),
        name="fox_attention",
    )(qi_tab, ki_tab, qa, ka, proj)


def _fox_decode_kernel(q_ref, kn_ref, vn_ref, kc_ref, vc_ref, cq_ref, ckc_ref, ckn_ref, o_ref,
                       m_ref, l_ref, acc_ref, *, lq, ncb):
    j = pl.program_id(1)
    nh, hd = FOX_HEADS, FOX_HEAD_DIM

    @pl.when(j == 0)
    def _():
        m_ref[...] = jnp.full_like(m_ref, NEG_INF)
        l_ref[...] = jnp.zeros_like(l_ref)
        acc_ref[...] = jnp.zeros_like(acc_ref)

    def attend(k_head, v_head, ck_head, causal):
        for h in range(nh):
            qh = q_ref[:, h * hd:(h + 1) * hd].astype(BF16)
            ck = ck_head(h)
            tk = ck.shape[1]
            s = _dot_nt(qh, k_head(h).astype(BF16)) * (hd ** -0.5)
            s = s + (_widen(cq_ref[h], tk) - ck)
            if causal:
                rows = lax.broadcasted_iota(jnp.int32, (lq, tk), 0)
                cols = lax.broadcasted_iota(jnp.int32, (lq, tk), 1)
                s = jnp.where(cols <= rows, s, NEG_INF)
            m_old = m_ref[h]
            m_new = jnp.maximum(m_old, jnp.max(s, axis=1, keepdims=True))
            alpha = jnp.exp(m_old - m_new)
            p = jnp.exp(s - _widen(m_new, tk))
            l_ref[h] = alpha * l_ref[h] + jnp.sum(p, axis=1, keepdims=True)
            acc_ref[h] = alpha * acc_ref[h] + _dot(p.astype(BF16), v_head(h).astype(BF16))
            m_ref[h] = m_new

    @pl.when(j < ncb)
    def _():
        k_hm = pltpu.einshape("mhd->hmd", kc_ref[0])
        v_hm = pltpu.einshape("mhd->hmd", vc_ref[0])
        attend(lambda h: k_hm[h], lambda h: v_hm[h], lambda h: ckc_ref[h], False)

    @pl.when(j == ncb)
    def _():
        attend(lambda h: kn_ref[:, h * hd:(h + 1) * hd], lambda h: vn_ref[:, h * hd:(h + 1) * hd],
               lambda h: ckn_ref[h][:, 0:lq], True)
        for h in range(nh):
            o_ref[:, h * hd:(h + 1) * hd] = (acc_ref[h] / l_ref[h]).astype(BF16)


def _fox_decode(proj, cache_k, cache_v, cq, ck_cache, ck_new, *, bsz, lq, tk):
    past = cache_k.shape[1]
    assert past % tk == 0 and tk % LANES == 0 and lq <= LANES
    ncb = past // tk
    nh, hd, w = FOX_HEADS, FOX_HEAD_DIM, FOX_WIDTH
    tile = lambda j: jnp.minimum(j, ncb - 1)
    cache_spec = pl.BlockSpec((1, tk, nh, hd), lambda b, j: (b, tile(j), 0, 0))
    return pl.pallas_call(
        functools.partial(_fox_decode_kernel, lq=lq, ncb=ncb),
        grid=(bsz, ncb + 1),
        in_specs=[pl.BlockSpec((lq, w), lambda b, j: (b, 0)),
                  pl.BlockSpec((lq, w), lambda b, j: (b, 1)),
                  pl.BlockSpec((lq, w), lambda b, j: (b, 2)),
                  cache_spec,
                  cache_spec,
                  pl.BlockSpec((nh, lq, LANES), lambda b, j: (b, 0, 0)),
                  pl.BlockSpec((nh, 1, tk), lambda b, j: (b, 0, tile(j))),
                  pl.BlockSpec((nh, 1, LANES), lambda b, j: (b, 0, 0))],
        out_specs=pl.BlockSpec((lq, w), lambda b, j: (b, 0)),
        out_shape=jax.ShapeDtypeStruct((bsz * lq, w), BF16),
        scratch_shapes=[pltpu.VMEM((nh, lq, LANES), F32), pltpu.VMEM((nh, lq, LANES), F32),
                        pltpu.VMEM((nh, lq, hd), F32)],
        compiler_params=_cparams(2),
        name="fox_decode",
    )(proj, proj, proj, cache_k, cache_v, cq, ck_cache, ck_new)


def _sconv_kernel(u_ref, bg_ref, cg_ref, cw_ref, hist_ref, y_ref, tail_ref, carry_ref, ext_ref,
                  *, tm, rows, spt, tpb):
    i = pl.program_id(0)
    w = cg_ref[...] * u_ref[...]
    first = (i % tpb) == 0
    for s in range(spt):
        w_s = w[s * rows:(s + 1) * rows]
        if tpb == 1:
            prev = hist_ref[s]
        else:
            prev = jnp.where(first, hist_ref[s], carry_ref[...])
        conv = _conv_rows(ext_ref, w_s, prev, cw_ref, SC_WIDTH, rows)
        y_ref[s * rows:(s + 1) * rows, :] = (bg_ref[s * rows:(s + 1) * rows, :] * conv).astype(BF16)
        tail_ref[s] = w_s[rows - SUBLANES:rows]
    if tpb > 1:
        carry_ref[...] = w[tm - SUBLANES:tm]


def _sconv(proj, conv_w, hist8, *, seq_len, tm):
    m = proj.shape[0]
    assert m % tm == 0
    rows, spt, tpb = _seq_tiling(seq_len, tm)
    nm = m // tm
    hist_map = (lambda i: (i // tpb, 0, 0)) if spt == 1 else (lambda i: (i, 0, 0))
    base = 3 * FOX_WIDTH // SC_DIM
    return pl.pallas_call(
        functools.partial(_sconv_kernel, tm=tm, rows=rows, spt=spt, tpb=tpb),
        grid=(nm,),
        in_specs=[pl.BlockSpec((tm, SC_DIM), lambda i: (i, base)),
                  pl.BlockSpec((tm, SC_DIM), lambda i: (i, base + 1)),
                  pl.BlockSpec((tm, SC_DIM), lambda i: (i, base + 2)),
                  pl.BlockSpec((SC_WIDTH, SC_DIM), lambda i: (0, 0)),
                  pl.BlockSpec((spt, SUBLANES, SC_DIM), hist_map)],
        out_specs=[pl.BlockSpec((tm, SC_DIM), lambda i: (i, 0)),
                   pl.BlockSpec((spt, SUBLANES, SC_DIM), lambda i: (i, 0, 0))],
        out_shape=[jax.ShapeDtypeStruct((m, SC_DIM), BF16),
                   jax.ShapeDtypeStruct((nm * spt, SUBLANES, SC_DIM), F32)],
        scratch_shapes=[pltpu.VMEM((SUBLANES, SC_DIM), F32),
                        pltpu.VMEM((rows + SUBLANES, SC_DIM), F32)],
        compiler_params=_cparams(1),
        name="gated_short_conv",
    )(proj, proj, proj, conv_w, hist8)


def _hist8(state):
    n, w1, c = state.shape
    return jnp.concatenate([jnp.zeros((n, SUBLANES - w1, c), F32), state.astype(F32)], axis=1)


def _tails(tails, n_seq, seq_len, tile_rows, keep):
    per_seq = max(1, seq_len // tile_rows)
    pieces = tails.reshape(n_seq, per_seq, SUBLANES, tails.shape[-1])
    return pieces[:, per_seq - 1, SUBLANES - keep:, :]


def _rope_tables(pos0, length):
    half = RET_DK // 2
    inv = ROPE_BASE ** (-np.arange(half, dtype=np.float64) / half)
    ang = (pos0 + np.arange(length, dtype=np.float64))[:, None] * inv[None, :]
    cos, sin = np.cos(ang), np.sin(ang)
    return (jnp.asarray(np.concatenate([cos, cos], axis=1), F32),
            jnp.asarray(np.concatenate([-sin, sin], axis=1), F32))


def _pad_cols_bf16(w):
    return jnp.pad(w, ((0, 0), (0, LANES - w.shape[1]))).astype(BF16)


def _cd_weights_xla(cd_in):
    f0 = 3 * FOX_WIDTH
    main = jnp.concatenate([cd_in[:, :f0], cd_in[:, f0 + FOX_HEADS:]], axis=1).astype(BF16)
    return main, _pad_cols_bf16(cd_in[:, f0:f0 + FOX_HEADS])


def _prep_weights(p):
    ab_in = p['ab_w_in'][0]
    return dict(
        ab_in=ab_in.astype(BF16),
        ab_small=_pad_cols_bf16(ab_in[:, AB_MAIN:]),
        cd_in=None,
        cd_small=None,
        ab_out=p['ab_w_out'][0].astype(BF16),
        cd_out=p['cd_w_out'][0].astype(BF16),
        ffn0=None,
        ffn1=None,
    )


def _trunk(x, pos0, st_ret, st_ssd, st_ssd_conv, c_k, c_v, c_logf, st_sconv, st_ffn, p, wb, t):
    bsz, length, d = x.shape
    m = bsz * length
    xf = x.reshape(m, d)
    zeros = lambda *shape: jnp.zeros(shape, F32)

    proj, dt_proj = _norm_matmul(xf, p['ab_norm_w'][0], wb['ab_in'], wb['ab_small'],
                                 n=AB_MAIN, tm=t['tm_ab'], tn=t['tn_ab'])
    cosf, sinf = _rope_tables(pos0, length)
    ret_state = zeros(bsz, RET_HEADS, RET_DK, RET_DV) if st_ret is None else st_ret
    y_ret, ret_new, cd_bf16 = _retention(proj, cosf, sinf, ret_state, p['ret_norm_w'][0],
                                         bsz=bsz, seq_len=length, c=t['c_ret'],
                                         cd_w_in=p['cd_w_in'][0] if wb['cd_in'] is None else None)
    if wb['cd_in'] is None:
        wb['cd_in'], wb['cd_small'] = cd_bf16 if cd_bf16 is not None else _cd_weights_xla(p['cd_w_in'][0])
    ssd_state = zeros(bsz, SSD_HEADS, SSD_HEADDIM, SSD_DSTATE) if st_ssd is None else st_ssd
    ssd_hist = zeros(bsz, SSD_CONV - 1, SSD_CONV_DIM) if st_ssd_conv is None else st_ssd_conv
    ffn_f32 = [p['ffn_w_gate'], p['ffn_w_up'], p['ffn_w_down']]
    y_ssd, ssd_new, ffn0_bf16 = _ssd(proj, dt_proj, _hist8(ssd_hist), ssd_state, p['ssd_conv_w'][0],
                                     p['ssd_conv_b'][0], p['ssd_dt_bias'][0], p['ssd_A_log'][0], p['ssd_D'][0],
                                     p['ssd_norm_w'][0], bsz=bsz, seq_len=length, c=t['c_ssd'],
                                     cast=(0, ffn_f32) if wb['ffn0'] is None else None)
    if wb['ffn0'] is None:
        wb['ffn0'] = (tuple(w[None] for w in ffn0_bf16) if ffn0_bf16 is not None
                      else tuple(w[0:1].astype(BF16) for w in ffn_f32))
    xbc_lo = AB_MAIN - SSD_CONV_DIM
    ssd_conv_new = proj.reshape(bsz, length, -1)[:, length - (SSD_CONV - 1):, xbc_lo:AB_MAIN]
    xf = _proj_residual(xf, y_ret, y_ssd, wb['ab_out'], tm=t['tm_out'], tn=t['tn_out'])

    ffn_new = []
    ffn_hist0 = zeros(bsz, FFN_CONV - 1, D_FF) if st_ffn is None else st_ffn[0]
    next_f32 = (1, p['ffn_w_gate'], p['ffn_w_up'], p['ffn_w_down'])
    n_row_tiles = m // t['tm_ffn']
    if wb['ffn1'] is None and not (d % (n_row_tiles * LANES) == 0 and D_FF % t['tf_ffn'] == 0):
        wb['ffn1'] = tuple(w[1:2].astype(BF16) for w in next_f32[1:])
    xf, tails, next_bf16 = _conv_ffn(
        xf, p['ffn_norm_w'][0], *wb['ffn0'], p['ffn_conv_w'][0], p['ffn_conv_b'][0], _hist8(ffn_hist0),
        p['final_norm_w'], layer=0, seq_len=length, tm=t['tm_ffn'], tf=t['tf_ffn'], ts=t['ts_ffn'], final=False,
        cast_next=next_f32 if wb['ffn1'] is None else None)
    if wb['ffn1'] is None:
        wb['ffn1'] = tuple(w[None] for w in next_bf16)
    ffn_new.append(_tails(tails, bsz, length, t['tm_ffn'], FFN_CONV - 1))

    proj, fl_proj = _norm_matmul(xf, p['cd_norm_w'][0], wb['cd_in'], wb['cd_small'],
                                 n=CD_MAIN, tm=t['tm_cd'], tn=t['tn_cd'])
    f_bias = jnp.pad(p['fox_f_bias'][0].astype(F32), (0, LANES - FOX_HEADS)).reshape(1, LANES)
    head_shape = (bsz, length, FOX_HEADS, FOX_HEAD_DIM)
    if c_k is None:
        qa, ka, k32, v32, logf = _fox_prep(proj, fl_proj, f_bias, bsz=bsz, seq_len=length, tp=t['t_prep'])
        y_fox = _fox_prompt(qa, ka, proj, bsz=bsz, seq_len=length, t=t['t_fox'], ts=t['ts_fox'],
                            ts_diag=t['ts_fox_diag'])
        logf_new = logf.reshape(bsz, length, LANES)[:, :, :FOX_HEADS]
        k_new, v_new = k32.reshape(head_shape), v32.reshape(head_shape)
    else:
        proj3 = proj.reshape(bsz, length, -1)
        k_new = proj3[:, :, FOX_WIDTH:2 * FOX_WIDTH].reshape(head_shape)
        v_new = proj3[:, :, 2 * FOX_WIDTH:3 * FOX_WIDTH].reshape(head_shape)
        past = c_k.shape[1]
        pairs = bsz * FOX_HEADS
        assert pairs <= LANES and length <= LANES
        to_lanes = lambda a, rows: jnp.pad(jnp.swapaxes(a, 0, 1).reshape(a.shape[1], pairs),
                                           ((0, rows - a.shape[1]), (0, LANES - pairs)))
        from_lanes = lambda a: jnp.swapaxes(a[:length, :pairs].reshape(length, bsz, FOX_HEADS), 0, 1)
        cache_lf = to_lanes(c_logf.astype(F32), past)
        fl_rows = to_lanes(fl_proj.reshape(bsz, length, LANES)[:, :, :FOX_HEADS], LANES)
        bias_lanes = jnp.pad(jnp.tile(p['fox_f_bias'][0].astype(F32), bsz), (0, LANES - pairs)).reshape(1, LANES)
        cum_t_cache, lf_rows, cum_rows, cum_t_new = _decode_cum(cache_lf, fl_rows, bias_lanes, c=t['c_cum'])
        logf_new = from_lanes(lf_rows)
        cq = jnp.broadcast_to(cum_rows[:length, :pairs].T[:, :, None], (pairs, length, LANES))
        y_fox = _fox_decode(proj, c_k, c_v, cq, cum_t_cache[:pairs, None, :], cum_t_new[:pairs, None, :],
                            bsz=bsz, lq=length, tk=t['tk_dec'])
    sc_hist = zeros(bsz, SC_WIDTH - 1, SC_DIM) if st_sconv is None else st_sconv
    y_sc, sc_tails = _sconv(proj, p['sconv_w'][0], _hist8(sc_hist), seq_len=length, tm=t['tm_sc'])
    sconv_new = _tails(sc_tails, bsz, length, t['tm_sc'], SC_WIDTH - 1)
    xf = _proj_residual(xf, y_fox, y_sc, wb['cd_out'], tm=t['tm_out'], tn=t['tn_out'])

    ffn_hist1 = zeros(bsz, FFN_CONV - 1, D_FF) if st_ffn is None else st_ffn[1]
    xf, tails, _ = _conv_ffn(xf, p['ffn_norm_w'][1], *wb['ffn1'], p['ffn_conv_w'][1], p['ffn_conv_b'][1],
                             _hist8(ffn_hist1), p['final_norm_w'], layer=0, seq_len=length, tm=t['tm_ffn'],
                             tf=t['tf_ffn'], ts=t['ts_ffn'], final=True)
    ffn_new.append(_tails(tails, bsz, length, t['tm_ffn'], FFN_CONV - 1))

    return (xf.reshape(bsz, length, d), ret_new[None], ssd_new[None], ssd_conv_new[None], k_new[None],
            v_new[None], logf_new[None], sconv_new[None], jnp.stack(ffn_new))


def _largest_divisor(n, cap, multiple=1):
    best = None
    for cand in range(multiple, min(n, cap) + 1, multiple):
        if n % cand == 0:
            best = cand
    assert best is not None, (n, cap, multiple)
    return best


def _tiles(bsz, length, past=None):
    m = bsz * length
    seq_tile = lambda cap: _largest_divisor(length, cap, SUBLANES)
    row_tile = lambda cap: (_largest_divisor(length, cap, SUBLANES) if length >= cap
                            else _largest_divisor(m, cap, length))
    t = dict(
        tm_ab=row_tile(1024), tn_ab=2048, tm_cd=row_tile(1024), tn_cd=2048,
        tm_out=row_tile(1024), tn_out=2048,
        tm_ffn=row_tile(1024), tf_ffn=512, ts_ffn=512,
        tm_sc=row_tile(1024),
        c_ret=seq_tile(256), c_ssd=seq_tile(256),
    )
    if past is None:
        t['t_fox'] = seq_tile(2048)
        t['ts_fox'] = _largest_divisor(t['t_fox'], 256, LANES)
        t['ts_fox_diag'] = _largest_divisor(t['t_fox'], 512, LANES)
        t['t_prep'] = seq_tile(512)
    else:
        t['tk_dec'] = _largest_divisor(past, 1024, LANES)
        t['c_cum'] = _largest_divisor(past, 256, LANES)
    return t


def kernel(x_prompt, x_sample, state_ret, state_ssd, state_ssd_conv, cache_fox_k, cache_fox_v, cache_fox_logf, state_sconv, state_ffn_conv, ab_norm_w, ab_w_in, ret_norm_w, ssd_conv_w, ssd_conv_b, ssd_dt_bias, ssd_A_log, ssd_D, ssd_norm_w, ab_w_out, cd_norm_w, cd_w_in, fox_f_bias, sconv_w, cd_w_out, ffn_norm_w, ffn_w_gate, ffn_w_up, ffn_conv_w, ffn_conv_b, ffn_w_down, final_norm_w):
    p = dict(ab_norm_w=ab_norm_w, ab_w_in=ab_w_in, ret_norm_w=ret_norm_w, ssd_conv_w=ssd_conv_w,
             ssd_conv_b=ssd_conv_b, ssd_dt_bias=ssd_dt_bias, ssd_A_log=ssd_A_log, ssd_D=ssd_D,
             ssd_norm_w=ssd_norm_w, ab_w_out=ab_w_out, cd_norm_w=cd_norm_w, cd_w_in=cd_w_in,
             fox_f_bias=fox_f_bias, sconv_w=sconv_w, cd_w_out=cd_w_out, ffn_norm_w=ffn_norm_w,
             ffn_w_gate=ffn_w_gate, ffn_w_up=ffn_w_up, ffn_conv_w=ffn_conv_w, ffn_conv_b=ffn_conv_b,
             ffn_w_down=ffn_w_down, final_norm_w=final_norm_w)
    assert x_prompt.shape[-1] == D_MODEL and ab_w_in.shape == (1, D_MODEL, AB_MAIN + SSD_HEADS)
    assert cd_w_in.shape == (1, D_MODEL, CD_MAIN + FOX_HEADS) and ffn_w_gate.shape == (2, D_MODEL, D_FF)
    wb = _prep_weights(p)
    bp, lp_, _ = x_prompt.shape
    bs, ls, _ = x_sample.shape
    past = cache_fox_k.shape[2]
    (y_prompt, p_ret, p_ssd, p_ssd_conv, p_fox_k, p_fox_v, p_fox_logf, p_sconv, p_ffn_conv) = _trunk(
        x_prompt, 0, None, None, None, None, None, None, None, None, p, wb, _tiles(bp, lp_))
    (y_sample, s_ret, s_ssd, s_ssd_conv, s_fox_k, s_fox_v, s_fox_logf, s_sconv, s_ffn_conv) = _trunk(
        x_sample, past, state_ret[0], state_ssd[0], state_ssd_conv[0], cache_fox_k[0], cache_fox_v[0],
        cache_fox_logf[0], state_sconv[0], state_ffn_conv, p, wb, _tiles(bs, ls, past))
    return (y_prompt, y_sample, p_ret, s_ret, p_ssd, s_ssd, p_ssd_conv, s_ssd_conv, p_fox_k, s_fox_k,
            p_fox_v, s_fox_v, p_fox_logf, s_fox_logf, p_sconv, s_sconv, p_ffn_conv, s_ffn_conv)
```

```python
import functools
import math

import numpy as np
import jax
import jax.numpy as jnp
from jax import lax
from jax.experimental import pallas as pl
from jax.experimental.pallas import tpu as pltpu

F32 = jnp.float32
BF16 = jnp.bfloat16
EPS = 1e-6
ROPE_BASE = 10000.0
NEG_INF = float("-inf")

D_MODEL = 2048
RET_HEADS, RET_DK, RET_DV = 4, 128, 256
SSD_DINNER, SSD_HEADDIM, SSD_HEADS, SSD_GROUPS, SSD_DSTATE, SSD_CONV = 1024, 64, 16, 2, 128, 4
SSD_CONV_DIM = SSD_DINNER + 2 * SSD_GROUPS * SSD_DSTATE
FOX_HEADS, FOX_HEAD_DIM = 8, 128
FOX_WIDTH = FOX_HEADS * FOX_HEAD_DIM
SC_DIM, SC_WIDTH = 1024, 3
D_FF, FFN_CONV = 5632, 3
AB_MAIN = 2 * RET_HEADS * RET_DK + 2 * RET_HEADS * RET_DV + SSD_DINNER + SSD_CONV_DIM
CD_MAIN = 3 * FOX_WIDTH + 3 * SC_DIM

LANES = 128
SUBLANES = 8
VMEM_LIMIT = 60 * 1024 * 1024


def _cparams(n_axes):
    return pltpu.CompilerParams(dimension_semantics=("arbitrary",) * n_axes,
                                vmem_limit_bytes=VMEM_LIMIT)


def _rms(xf, w):
    return xf * lax.rsqrt(jnp.mean(xf * xf, axis=-1, keepdims=True) + EPS) * w


def _softplus(x):
    return jnp.maximum(x, 0.0) + jnp.log1p(jnp.exp(-jnp.abs(x)))


def _split3(x):
    hi = x.astype(BF16)
    r1 = x - hi.astype(F32)
    mid = r1.astype(BF16)
    lo = (r1 - mid.astype(F32)).astype(BF16)
    return hi, mid, lo


def _widen(x, n):
    return x[:, 0:n] if n <= LANES else jnp.concatenate([x] * (n // LANES), axis=1)


def _dot(a, b):
    return jnp.dot(a, b, preferred_element_type=F32)


def _dot_nt(a, b):
    return lax.dot_general(a, b, (((1,), (1,)), ((), ())), preferred_element_type=F32)


def _dot_tn(a, b):
    return lax.dot_general(a, b, (((0,), (0,)), ((), ())), preferred_element_type=F32)


def _exact_lhs_dot(m_bf16, x):
    hi, mid, lo = _split3(x)
    return _dot(m_bf16, hi) + _dot(m_bf16, mid) + _dot(m_bf16, lo)


def _exact_rhs_dot(x, m_bf16):
    hi, mid, lo = _split3(x)
    return _dot(hi, m_bf16) + _dot(mid, m_bf16) + _dot(lo, m_bf16)


def _conv_rows(ext_ref, x, prev8, w_ref, width, rows, w_cols=slice(None)):
    ext_ref[0:SUBLANES, :] = prev8
    ext_ref[SUBLANES:SUBLANES + rows, :] = x
    out = None
    for j in range(width):
        off = SUBLANES - (width - 1) + j
        term = ext_ref[off:off + rows, :] * w_ref[j:j + 1, w_cols]
        out = term if out is None else out + term
    return out


def _seq_tiling(seq_len, tile_rows):
    if seq_len >= tile_rows:
        assert seq_len % tile_rows == 0
        return tile_rows, 1, seq_len // tile_rows
    assert tile_rows % seq_len == 0 and seq_len % SUBLANES == 0
    return seq_len, tile_rows // seq_len, 1


def _norm_matmul_kernel(x_ref, nw_ref, w_ref, ws_ref, o_ref, os_ref, h_ref, *, n_tiles, tn, last_w):
    j = pl.program_id(1)

    @pl.when(j == 0)
    def _():
        h_ref[...] = _rms(x_ref[...], nw_ref[...]).astype(BF16)
        os_ref[...] = _dot(h_ref[...], ws_ref[...])

    if last_w == tn:
        o_ref[...] = _dot(h_ref[...], w_ref[...])
    else:
        @pl.when(j < n_tiles - 1)
        def _():
            o_ref[...] = _dot(h_ref[...], w_ref[...])

        @pl.when(j == n_tiles - 1)
        def _():
            o_ref[:, 0:last_w] = _dot(h_ref[...], w_ref[:, 0:last_w])


def _norm_matmul(x, norm_w, w, w_small, *, n, tm, tn):
    m, d = x.shape
    n_tiles = pl.cdiv(n, tn)
    last_w = n - (n_tiles - 1) * tn
    assert m % tm == 0 and last_w % LANES == 0 and n <= w.shape[1] and w_small.shape == (d, LANES)
    return pl.pallas_call(
        functools.partial(_norm_matmul_kernel, n_tiles=n_tiles, tn=tn, last_w=last_w),
        grid=(m // tm, n_tiles),
        in_specs=[pl.BlockSpec((tm, d), lambda i, j: (i, 0)),
                  pl.BlockSpec((1, d), lambda i, j: (0, 0)),
                  pl.BlockSpec((d, tn), lambda i, j: (0, j)),
                  pl.BlockSpec((d, LANES), lambda i, j: (0, 0))],
        out_specs=[pl.BlockSpec((tm, tn), lambda i, j: (i, j)),
                   pl.BlockSpec((tm, LANES), lambda i, j: (i, 0))],
        out_shape=[jax.ShapeDtypeStruct((m, n), F32),
                   jax.ShapeDtypeStruct((m, LANES), F32)],
        scratch_shapes=[pltpu.VMEM((tm, d), BF16)],
        compiler_params=_cparams(2),
        name="norm_in_proj",
    )(x, norm_w.reshape(1, d), w, w_small)


def _proj_res_kernel(x_ref, a_ref, b_ref, wa_ref, wb_ref, o_ref):
    acc = _dot(a_ref[...], wa_ref[...])
    acc = acc + _dot(b_ref[...], wb_ref[...])
    o_ref[...] = x_ref[...] + acc


def _proj_residual(x, a, b, w, *, tm, tn):
    m, d = x.shape
    ka, kb = a.shape[1], b.shape[1]
    assert m % tm == 0 and d % tn == 0 and ka == kb and w.shape == (ka + kb, d)
    return pl.pallas_call(
        _proj_res_kernel,
        grid=(m // tm, d // tn),
        in_specs=[pl.BlockSpec((tm, tn), lambda i, j: (i, j)),
                  pl.BlockSpec((tm, ka), lambda i, j: (i, 0)),
                  pl.BlockSpec((tm, kb), lambda i, j: (i, 0)),
                  pl.BlockSpec((ka, tn), lambda i, j: (0, j)),
                  pl.BlockSpec((kb, tn), lambda i, j: (1, j))],
        out_specs=pl.BlockSpec((tm, tn), lambda i, j: (i, j)),
        out_shape=jax.ShapeDtypeStruct((m, d), F32),
        compiler_params=_cparams(2),
        name="out_proj_residual",
    )(x, a, b, w, w)


def _ffn_kernel(*refs, tm, rows, spt, tpb, nf, ts, nsub, nsub_last, final, n_cast):
    x_ref, nw_ref, wg_ref, wu_ref, wd_ref, cw_ref, cb_ref, hist_ref, fw_ref = refs[:9]
    cast_in = refs[9:9 + n_cast]
    o_ref, tail_ref = refs[9 + n_cast:11 + n_cast]
    cast_out = refs[11 + n_cast:11 + 2 * n_cast]
    h_ref, carry_ref, ext_ref = refs[11 + 2 * n_cast:]
    i = pl.program_id(0)
    f = pl.program_id(1)

    for src_ref, dst_ref in zip(cast_in, cast_out):
        dst_ref[...] = src_ref[...].astype(BF16)

    @pl.when(f == 0)
    def _():
        xf = x_ref[...]
        h_ref[...] = _rms(xf, nw_ref[...]).astype(BF16)
        o_ref[...] = xf

    first = (i % tpb) == 0

    def sub_block(sb):
        cols = slice(sb * ts, (sb + 1) * ts)
        h = h_ref[...]
        a = _dot(h, wg_ref[:, cols])
        u = _dot(h, wu_ref[:, cols])
        convs = []
        for s in range(spt):
            a_s = a[s * rows:(s + 1) * rows]
            if tpb == 1:
                prev = hist_ref[s, :, cols]
            else:
                prev = jnp.where(first, hist_ref[s, :, cols], carry_ref[f * nsub + sb])
            convs.append(_conv_rows(ext_ref, a_s, prev, cw_ref, FFN_CONV, rows, cols))
            tail_ref[s, :, cols] = a_s[rows - SUBLANES:rows]
        if tpb > 1:
            carry_ref[f * nsub + sb] = a[tm - SUBLANES:tm]
        conv = convs[0] if spt == 1 else jnp.concatenate(convs, axis=0)
        act = (jax.nn.silu(conv + cb_ref[:, cols]) * u).astype(BF16)
        o_ref[...] += _dot(act, wd_ref[cols, :])

    if nsub_last == nsub:
        for sb in range(nsub):
            sub_block(sb)
    else:
        @pl.when(f < nf - 1)
        def _():
            for sb in range(nsub):
                sub_block(sb)

        @pl.when(f == nf - 1)
        def _():
            for sb in range(nsub_last):
                sub_block(sb)

    if final:
        @pl.when(f == nf - 1)
        def _():
            o_ref[...] = _rms(o_ref[...], fw_ref[...])


def _conv_ffn(x, norm_w, wg, wu, wd, conv_w, conv_b, hist8, final_w, *, layer, seq_len, tm, tf, ts, final,
              cast_next=None):
    m, d = x.shape
    ff = wg.shape[2]
    assert m % tm == 0 and tf % ts == 0 and ff % ts == 0
    rows, spt, tpb = _seq_tiling(seq_len, tm)
    nm, nf = m // tm, pl.cdiv(ff, tf)
    nsub = tf // ts
    nsub_last = (ff - (nf - 1) * tf) // ts
    hist_map = (lambda i, f: (i // tpb, 0, f)) if spt == 1 else (lambda i, f: (i, 0, f))
    cast_in, cast_specs, cast_src_specs = [], [], []
    if cast_next is not None:
        assert d % nm == 0 and (d // nm) % LANES == 0 and ff % nf == 0 and (ff // nf) % LANES == 0
        dr, fc = d // nm, ff // nf
        cast_layer, cast_in = cast_next[0], list(cast_next[1:])
        cast_specs = [pl.BlockSpec((dr, fc), lambda i, f: (i, f)),
                      pl.BlockSpec((dr, fc), lambda i, f: (i, f)),
                      pl.BlockSpec((fc, dr), lambda i, f: (f, i))]
        cast_src_specs = [pl.BlockSpec((None, dr, fc), lambda i, f: (cast_layer, i, f)),
                          pl.BlockSpec((None, dr, fc), lambda i, f: (cast_layer, i, f)),
                          pl.BlockSpec((None, fc, dr), lambda i, f: (cast_layer, f, i))]
    kern = functools.partial(_ffn_kernel, tm=tm, rows=rows, spt=spt, tpb=tpb, nf=nf, ts=ts, nsub=nsub,
                             nsub_last=nsub_last, final=final, n_cast=len(cast_in))
    results = pl.pallas_call(
        kern,
        grid=(nm, nf),
        in_specs=[pl.BlockSpec((tm, d), lambda i, f: (i, 0)),
                  pl.BlockSpec((1, d), lambda i, f: (0, 0)),
                  pl.BlockSpec((None, d, tf), lambda i, f: (layer, 0, f)),
                  pl.BlockSpec((None, d, tf), lambda i, f: (layer, 0, f)),
                  pl.BlockSpec((None, tf, d), lambda i, f: (layer, f, 0)),
                  pl.BlockSpec((FFN_CONV, tf), lambda i, f: (0, f)),
                  pl.BlockSpec((1, tf), lambda i, f: (0, f)),
                  pl.BlockSpec((spt, SUBLANES, tf), hist_map),
                  pl.BlockSpec((1, d), lambda i, f: (0, 0))] + cast_src_specs,
        out_specs=[pl.BlockSpec((tm, d), lambda i, f: (i, 0)),
                   pl.BlockSpec((spt, SUBLANES, tf), lambda i, f: (i, 0, f))] + cast_specs,
        out_shape=[jax.ShapeDtypeStruct((m, d), F32),
                   jax.ShapeDtypeStruct((nm * spt, SUBLANES, ff), F32)]
                  + [jax.ShapeDtypeStruct(w.shape[1:], BF16) for w in cast_in],
        scratch_shapes=[pltpu.VMEM((tm, d), BF16),
                        pltpu.VMEM((nf * nsub, SUBLANES, ts), F32),
                        pltpu.VMEM((rows + SUBLANES, ts), F32)],
        compiler_params=_cparams(2),
        name="conv_ffn",
    )(x, norm_w.reshape(1, d), wg, wu, wd, conv_w, conv_b.reshape(1, ff), hist8, final_w.reshape(1, d), *cast_in)
    return results[0], results[1], tuple(results[2:])


def _retention_kernel(*refs, c, side_cast):
    q_ref, k_ref, v_ref, g_ref, cos_ref, sin_ref, st_ref, nw_ref = refs[:8]
    y_ref, so_ref = refs[8 + side_cast:10 + side_cast]
    ci = pl.program_id(1)

    if side_cast:
        w = refs[8][...]
        w_main_ref, w_gate_ref = refs[11:13]
        f0 = 3 * FOX_WIDTH
        w_main_ref[:, 0:f0] = w[:, 0:f0].astype(BF16)
        w_main_ref[:, f0:CD_MAIN] = w[:, f0 + FOX_HEADS:CD_MAIN + FOX_HEADS].astype(BF16)
        gate = jnp.concatenate([w[:, f0:f0 + FOX_HEADS], jnp.zeros((w.shape[0], LANES - FOX_HEADS), F32)], axis=1)
        w_gate_ref[...] = gate.astype(BF16)

    @pl.when(ci == 0)
    def _():
        so_ref[...] = st_ref[...]

    cos = cos_ref[...]
    sin = sin_ref[...]
    ii = lax.broadcasted_iota(jnp.int32, (c, c), 0)
    jj = lax.broadcasted_iota(jnp.int32, (c, c), 1)
    diff = (ii - jj).astype(F32)
    causal = ii >= jj
    ridx = lax.broadcasted_iota(jnp.int32, (c, 1), 0).astype(F32)
    for h in range(RET_HEADS):
        lg = math.log1p(-(2.0 ** (-5.0 - h)))
        q = q_ref[:, h * RET_DK:(h + 1) * RET_DK]
        k = k_ref[:, h * RET_DK:(h + 1) * RET_DK]
        v = v_ref[:, h * RET_DV:(h + 1) * RET_DV]
        qr = q * cos + pltpu.roll(q, RET_DK // 2, 1) * sin
        kr = (k * cos + pltpu.roll(k, RET_DK // 2, 1) * sin) * (RET_DK ** -0.5)
        qb = qr.astype(BF16)
        kb = kr.astype(BF16)
        vb = v.astype(BF16)
        decay = jnp.exp(jnp.where(causal, diff * lg, NEG_INF))
        inner = jnp.exp((ridx + 1.0) * lg)
        sdecay = jnp.exp((c - 1.0 - ridx) * lg)
        s = so_ref[0, h]
        scores = _dot_nt(qb, kb) * decay
        y = _dot(scores.astype(BF16), vb)
        y = y + _dot(qb, s.astype(BF16)) * inner
        kd = (kr * sdecay).astype(BF16)
        so_ref[0, h] = math.exp(c * lg) * s + _dot_tn(kd, vb)
        mu = jnp.mean(y, axis=-1, keepdims=True)
        yc = y - mu
        var = jnp.mean(yc * yc, axis=-1, keepdims=True)
        yn = yc * lax.rsqrt(var + EPS) * nw_ref[:, h * RET_DV:(h + 1) * RET_DV]
        g = g_ref[:, h * RET_DV:(h + 1) * RET_DV]
        y_ref[:, h * RET_DV:(h + 1) * RET_DV] = (jax.nn.silu(g) * yn).astype(BF16)


def _retention(proj, cosf, sinf, state, norm_w, *, bsz, seq_len, c, cd_w_in=None):
    m = proj.shape[0]
    nc = seq_len // c
    assert seq_len % c == 0
    qk_w = RET_HEADS * RET_DK
    v_w = RET_HEADS * RET_DV
    row = lambda b, ci: b * nc + ci
    side_in, side_src, side_dst, side_shapes = [], [], [], []
    if cd_w_in is not None and cd_w_in.shape[0] % (bsz * nc * 16) == 0:
        d, wide = cd_w_in.shape
        rb = d // (bsz * nc)
        side_in = [cd_w_in]
        side_src = [pl.BlockSpec((rb, wide), lambda b, ci: (row(b, ci), 0))]
        side_dst = [pl.BlockSpec((rb, CD_MAIN), lambda b, ci: (row(b, ci), 0)),
                    pl.BlockSpec((rb, LANES), lambda b, ci: (row(b, ci), 0))]
        side_shapes = [jax.ShapeDtypeStruct((d, CD_MAIN), BF16), jax.ShapeDtypeStruct((d, LANES), BF16)]
    results = pl.pallas_call(
        functools.partial(_retention_kernel, c=c, side_cast=len(side_in)),
        grid=(bsz, nc),
        in_specs=[pl.BlockSpec((c, qk_w), lambda b, ci: (row(b, ci), 0)),
                  pl.BlockSpec((c, qk_w), lambda b, ci: (row(b, ci), 1)),
                  pl.BlockSpec((c, v_w), lambda b, ci: (row(b, ci), 1)),
                  pl.BlockSpec((c, v_w), lambda b, ci: (row(b, ci), 2)),
                  pl.BlockSpec((c, RET_DK), lambda b, ci: (ci, 0)),
                  pl.BlockSpec((c, RET_DK), lambda b, ci: (ci, 0)),
                  pl.BlockSpec((1, RET_HEADS, RET_DK, RET_DV), lambda b, ci: (b, 0, 0, 0)),
                  pl.BlockSpec((1, v_w), lambda b, ci: (0, 0))] + side_src,
        out_specs=[pl.BlockSpec((c, v_w), lambda b, ci: (row(b, ci), 0)),
                   pl.BlockSpec((1, RET_HEADS, RET_DK, RET_DV), lambda b, ci: (b, 0, 0, 0))] + side_dst,
        out_shape=[jax.ShapeDtypeStruct((m, v_w), BF16),
                   jax.ShapeDtypeStruct(state.shape, F32)] + side_shapes,
        compiler_params=_cparams(2),
        name="retention",
    )(proj, proj, proj, proj, cosf, sinf, state, norm_w.reshape(1, v_w), *side_in)
    return results[0], results[1], (tuple(results[2:]) if side_in else None)


def _ssd_kernel(*refs, c, nc, n_cast):
    (z_ref, xs_ref, bc_ref, dt_ref, hx_ref, hbc_ref, st_ref, cwx_ref, cwbc_ref, cbx_ref, cbbc_ref,
     dtb_ref, alog_ref, dsk_ref, nw_ref, tri_ref, exp_ref) = refs[:17]
    cast_in = refs[17:17 + n_cast]
    y_ref, so_ref = refs[17 + n_cast:19 + n_cast]
    cast_out = refs[19 + n_cast:19 + 2 * n_cast]
    st_scr, cx_scr, cbc_scr, extx_scr, extbc_scr, yh_scr, xs_scr = refs[19 + 2 * n_cast:]
    ci = pl.program_id(1)

    for src_ref, dst_ref in zip(cast_in, cast_out):
        dst_ref[...] = src_ref[...].astype(BF16)
    gw = SSD_DINNER // SSD_GROUPS
    hpg = SSD_HEADS // SSD_GROUPS

    @pl.when(ci == 0)
    def _():
        st_scr[...] = st_ref[0].T
        cx_scr[...] = hx_ref[0]
        cbc_scr[...] = hbc_ref[0]

    xs_raw = xs_ref[...]
    bc_raw = bc_ref[...]
    xs_scr[...] = jax.nn.silu(_conv_rows(extx_scr, xs_raw, cx_scr[...], cwx_ref, SSD_CONV, c) + cbx_ref[...])
    bcm = jax.nn.silu(_conv_rows(extbc_scr, bc_raw, cbc_scr[...], cwbc_ref, SSD_CONV, c) + cbbc_ref[...])
    cx_scr[...] = xs_raw[c - SUBLANES:c]
    cbc_scr[...] = bc_raw[c - SUBLANES:c]

    tri = tri_ref[...]
    dt = _softplus(dt_ref[...] + dtb_ref[...])
    a = -jnp.exp(alog_ref[...])
    acs = _exact_lhs_dot(tri, dt * a)
    acs_t = acs.T
    acs_last = acs[c - 1:c, :]
    exp_acs = jnp.exp(acs)
    to_end = jnp.exp(acs_last - acs)
    chunk_dec = jnp.exp(acs_last)

    ii = lax.broadcasted_iota(jnp.int32, (c, c), 0)
    jj = lax.broadcasted_iota(jnp.int32, (c, c), 1)
    causal = ii >= jj
    nb = SSD_GROUPS * SSD_DSTATE
    for g in range(SSD_GROUPS):
        cols = slice(g * gw, (g + 1) * gw)
        expand = exp_ref[:, cols]
        xdt = xs_scr[:, cols] * _exact_rhs_dot(dt, expand)
        xdt_b = xdt.astype(BF16)
        xend_b = (xdt * _exact_rhs_dot(to_end, expand)).astype(BF16)
        b_g = bcm[:, g * SSD_DSTATE:(g + 1) * SSD_DSTATE].astype(BF16)
        c_g = bcm[:, nb + g * SSD_DSTATE:nb + (g + 1) * SSD_DSTATE].astype(BF16)
        cb = _dot_nt(c_g, b_g)
        s_g = st_scr[:, cols]
        y_state = _dot(c_g, s_g.astype(BF16)) * _exact_rhs_dot(exp_acs, expand)
        for r in range(hpg):
            hh = g * hpg + r
            seg = acs[:, hh:hh + 1] - acs_t[hh:hh + 1, :]
            lmat = jnp.exp(jnp.where(causal, seg, NEG_INF))
            mm = (cb * lmat).astype(BF16)
            head = slice(r * SSD_HEADDIM, (r + 1) * SSD_HEADDIM)
            yh_scr[:, hh * SSD_HEADDIM:(hh + 1) * SSD_HEADDIM] = _dot(mm, xdt_b[:, head]) + y_state[:, head]
        upd = _dot_tn(b_g, xend_b)
        st_scr[:, cols] = _exact_rhs_dot(chunk_dec, expand) * s_g + upd

    y = yh_scr[...] + dsk_ref[...] * xs_scr[...]
    z = z_ref[...]
    y_ref[...] = _rms(y * jax.nn.silu(z), nw_ref[...]).astype(BF16)

    @pl.when(ci == nc - 1)
    def _():
        so_ref[0] = st_scr[...].T


def _row_block_cast_specs(w3, layer, n_steps, step_of):
    _, r, cols = w3.shape
    for per in (1, 2, 4, 8):
        if (r * per) % n_steps == 0 and (r * per // n_steps) % 16 == 0:
            rb = r * per // n_steps
            src = pl.BlockSpec((None, rb, cols), lambda *g: (layer, step_of(*g) // per, 0))
            dst = pl.BlockSpec((rb, cols), lambda *g: (step_of(*g) // per, 0))
            return src, dst, jax.ShapeDtypeStruct((r, cols), BF16)
    return None


def _ssd(proj, dt_proj, hist8, state, conv_w, conv_b, dt_bias, a_log, d_skip, norm_w, *, bsz, seq_len, c,
         cast=None):
    m = proj.shape[0]
    nc = seq_len // c
    assert seq_len % c == 0
    row = lambda b, ci: b * nc + ci
    cast_in, cast_src, cast_dst, cast_shapes = [], [], [], []
    if cast is not None:
        specs = [_row_block_cast_specs(w, cast[0], bsz * nc, row) for w in cast[1]]
        if all(s is not None for s in specs):
            cast_in = list(cast[1])
            cast_src, cast_dst, cast_shapes = (list(x) for x in zip(*specs))
    const2 = lambda b, ci: (0, 0)
    di, bcw = SSD_DINNER, 2 * SSD_GROUPS * SSD_DSTATE
    tri = jnp.asarray(np.tril(np.ones((c, c), np.float32)), BF16)
    expand = np.zeros((LANES, di), np.float32)
    for h in range(SSD_HEADS):
        expand[h, h * SSD_HEADDIM:(h + 1) * SSD_HEADDIM] = 1.0
    expand = jnp.asarray(expand, BF16)
    pad_row = lambda v: jnp.pad(v.astype(F32), (0, LANES - v.shape[0])).reshape(1, LANES)
    st2 = state.reshape(bsz, di, SSD_DSTATE)
    results = pl.pallas_call(
        functools.partial(_ssd_kernel, c=c, nc=nc, n_cast=len(cast_in)),
        grid=(bsz, nc),
        in_specs=[pl.BlockSpec((c, di), lambda b, ci: (row(b, ci), 3)),
                  pl.BlockSpec((c, di), lambda b, ci: (row(b, ci), 4)),
                  pl.BlockSpec((c, bcw), lambda b, ci: (row(b, ci), 10)),
                  pl.BlockSpec((c, LANES), lambda b, ci: (row(b, ci), 0)),
                  pl.BlockSpec((1, SUBLANES, di), lambda b, ci: (b, 0, 0)),
                  pl.BlockSpec((1, SUBLANES, bcw), lambda b, ci: (b, 0, 2)),
                  pl.BlockSpec((1, di, SSD_DSTATE), lambda b, ci: (b, 0, 0)),
                  pl.BlockSpec((SSD_CONV, di), const2),
                  pl.BlockSpec((SSD_CONV, bcw), lambda b, ci: (0, 2)),
                  pl.BlockSpec((1, di), const2),
                  pl.BlockSpec((1, bcw), lambda b, ci: (0, 2)),
                  pl.BlockSpec((1, LANES), const2),
                  pl.BlockSpec((1, LANES), const2),
                  pl.BlockSpec((1, di), const2),
                  pl.BlockSpec((1, di), const2),
                  pl.BlockSpec((c, c), const2),
                  pl.BlockSpec((LANES, di), const2)] + cast_src,
        out_specs=[pl.BlockSpec((c, di), lambda b, ci: (row(b, ci), 0)),
                   pl.BlockSpec((1, di, SSD_DSTATE), lambda b, ci: (b, 0, 0))] + cast_dst,
        out_shape=[jax.ShapeDtypeStruct((m, di), BF16),
                   jax.ShapeDtypeStruct(st2.shape, F32)] + cast_shapes,
        scratch_shapes=[pltpu.VMEM((SSD_DSTATE, di), F32),
                        pltpu.VMEM((SUBLANES, di), F32),
                        pltpu.VMEM((SUBLANES, bcw), F32),
                        pltpu.VMEM((c + SUBLANES, di), F32),
                        pltpu.VMEM((c + SUBLANES, bcw), F32),
                        pltpu.VMEM((c, di), F32),
                        pltpu.VMEM((c, di), F32)],
        compiler_params=_cparams(2),
        name="ssd",
    )(proj, proj, proj, dt_proj, hist8, hist8, st2,
      conv_w, conv_w, conv_b.reshape(1, -1), conv_b.reshape(1, -1),
      pad_row(dt_bias), pad_row(a_log), jnp.repeat(d_skip.astype(F32), SSD_HEADDIM).reshape(1, di),
      norm_w.reshape(1, di), tri, expand, *cast_in)
    return results[0], results[1].reshape(state.shape), (list(results[2:]) if cast_in else None)


def _decode_cum_kernel(lfc_ref, fl_ref, b_ref, tri_ref, cumt_c_ref, lfn_ref, cumn_ref, cumt_n_ref,
                       carry_ref, *, c, ncb):
    j = pl.program_id(0)

    @pl.when(j == 0)
    def _():
        carry_ref[...] = jnp.zeros_like(carry_ref)

    @pl.when(j < ncb)
    def _():
        cum = _exact_lhs_dot(tri_ref[...], lfc_ref[...]) + carry_ref[...]
        carry_ref[...] = cum[c - 1:c, :]
        cumt_c_ref[...] = cum.T

    @pl.when(j == ncb)
    def _():
        lf = -_softplus(-(fl_ref[...] + b_ref[...]))
        lfn_ref[...] = lf
        cum = _exact_lhs_dot(tri_ref[0:LANES, 0:LANES], lf) + carry_ref[...]
        cumn_ref[...] = cum
        cumt_n_ref[...] = cum.T


def _decode_cum(cache_lf, fl_new, bias, *, c):
    past = cache_lf.shape[0]
    assert past % c == 0 and c % LANES == 0 and fl_new.shape == (LANES, LANES)
    ncb = past // c
    tri = jnp.asarray(np.tril(np.ones((c, c), np.float32)), BF16)
    blk = lambda j: jnp.minimum(j, ncb - 1)
    sq = jax.ShapeDtypeStruct((LANES, LANES), F32)
    return pl.pallas_call(
        functools.partial(_decode_cum_kernel, c=c, ncb=ncb),
        grid=(ncb + 1,),
        in_specs=[pl.BlockSpec((c, LANES), lambda j: (blk(j), 0)),
                  pl.BlockSpec((LANES, LANES), lambda j: (0, 0)),
                  pl.BlockSpec((1, LANES), lambda j: (0, 0)),
                  pl.BlockSpec((c, c), lambda j: (0, 0))],
        out_specs=[pl.BlockSpec((LANES, c), lambda j: (0, blk(j))),
                   pl.BlockSpec((LANES, LANES), lambda j: (0, 0)),
                   pl.BlockSpec((LANES, LANES), lambda j: (0, 0)),
                   pl.BlockSpec((LANES, LANES), lambda j: (0, 0))],
        out_shape=[jax.ShapeDtypeStruct((LANES, past), F32), sq, sq, sq],
        scratch_shapes=[pltpu.VMEM((1, LANES), F32)],
        compiler_params=_cparams(1),
        name="decode_logf_cumsum",
    )(cache_lf, fl_new, bias, tri)


FOX_AUG = 2 * FOX_HEAD_DIM
N_BIAS_PIECES = 3


def _fox_prep_kernel(q_ref, k_ref, v_ref, fl_ref, fb_ref, tri_ref, place_ref, ones_ref,
                     qa_ref, ka_ref, k32_ref, v32_ref, lf_ref, carry_ref, *, tp):
    @pl.when(pl.program_id(1) == 0)
    def _():
        carry_ref[...] = jnp.zeros_like(carry_ref)

    lf = -_softplus(-(fl_ref[...] + fb_ref[...]))
    lf_ref[...] = lf
    cum = _exact_lhs_dot(tri_ref[...], lf) + carry_ref[...]
    carry_ref[...] = cum[tp - 1:tp, :]
    pieces = _split3(cum * (FOX_HEAD_DIM ** 0.5))
    n = N_BIAS_PIECES
    aug_q = ones_ref[0:1, :] + sum(_dot(pieces[r], place_ref[r]) for r in range(n))
    aug_k = ones_ref[1:2, :] - sum(_dot(pieces[r], place_ref[n + r]) for r in range(n))
    for h in range(FOX_HEADS):
        src = slice(h * FOX_HEAD_DIM, (h + 1) * FOX_HEAD_DIM)
        feat = slice(h * FOX_AUG, h * FOX_AUG + FOX_HEAD_DIM)
        bias = slice(h * FOX_AUG + FOX_HEAD_DIM, (h + 1) * FOX_AUG)
        qa_ref[:, feat] = q_ref[:, src].astype(BF16)
        qa_ref[:, bias] = aug_q[:, src].astype(BF16)
        ka_ref[:, feat] = k_ref[:, src].astype(BF16)
        ka_ref[:, bias] = aug_k[:, src].astype(BF16)
    k = k_ref[...]
    v = v_ref[...]
    k32_ref[...] = pltpu.einshape("m(hd)->mhd", k, h=FOX_HEADS)
    v32_ref[...] = pltpu.einshape("m(hd)->mhd", v, h=FOX_HEADS)


def _fox_prep(proj, fl_proj, f_bias, *, bsz, seq_len, tp):
    m = proj.shape[0]
    nt = seq_len // tp
    assert seq_len % tp == 0
    w = FOX_WIDTH
    tri = jnp.asarray(np.tril(np.ones((tp, tp), np.float32)), BF16)
    n = N_BIAS_PIECES
    place = np.zeros((2 * n, LANES, w), np.float32)
    ones = np.zeros((SUBLANES, w), np.float32)
    for h in range(FOX_HEADS):
        for r in range(2 * n):
            place[r, h, h * FOX_HEAD_DIM + r] = 1.0
        ones[0, h * FOX_HEAD_DIM + n:h * FOX_HEAD_DIM + 2 * n] = 1.0
        ones[1, h * FOX_HEAD_DIM:h * FOX_HEAD_DIM + n] = 1.0
    row = lambda b, ti: (b * nt + ti, 0)
    const2 = lambda b, ti: (0, 0)
    return pl.pallas_call(
        functools.partial(_fox_prep_kernel, tp=tp),
        grid=(bsz, nt),
        in_specs=[pl.BlockSpec((tp, w), lambda b, ti: (b * nt + ti, 0)),
                  pl.BlockSpec((tp, w), lambda b, ti: (b * nt + ti, 1)),
                  pl.BlockSpec((tp, w), lambda b, ti: (b * nt + ti, 2)),
                  pl.BlockSpec((tp, LANES), row),
                  pl.BlockSpec((1, LANES), const2),
                  pl.BlockSpec((tp, tp), const2),
                  pl.BlockSpec((2 * n, LANES, w), lambda b, ti: (0, 0, 0)),
                  pl.BlockSpec((SUBLANES, w), const2)],
        out_specs=[pl.BlockSpec((tp, FOX_HEADS * FOX_AUG), row),
                   pl.BlockSpec((tp, FOX_HEADS * FOX_AUG), row),
                   pl.BlockSpec((tp, FOX_HEADS, FOX_HEAD_DIM), lambda b, ti: (b * nt + ti, 0, 0)),
                   pl.BlockSpec((tp, FOX_HEADS, FOX_HEAD_DIM), lambda b, ti: (b * nt + ti, 0, 0)),
                   pl.BlockSpec((tp, LANES), row)],
        out_shape=[jax.ShapeDtypeStruct((m, FOX_HEADS * FOX_AUG), BF16),
                   jax.ShapeDtypeStruct((m, FOX_HEADS * FOX_AUG), BF16),
                   jax.ShapeDtypeStruct((m, FOX_HEADS, FOX_HEAD_DIM), F32),
                   jax.ShapeDtypeStruct((m, FOX_HEADS, FOX_HEAD_DIM), F32),
                   jax.ShapeDtypeStruct((m, LANES), F32)],
        scratch_shapes=[pltpu.VMEM((1, LANES), F32)],
        compiler_params=_cparams(2),
        name="fox_prep",
    )(proj, proj, proj, fl_proj, f_bias, tri, jnp.asarray(place, BF16), jnp.asarray(ones, F32))


def _fox_kernel(qi_ref, ki_ref, q_ref, k_ref, v_ref, o_ref, m_ref, acc_ref, va_ref, *, t, ts, ts_diag):
    step = pl.program_id(2)
    qi = qi_ref[step]
    ki = ki_ref[step]
    to_log2 = (FOX_HEAD_DIM ** -0.5) * math.log2(math.e)
    hd = FOX_HEAD_DIM

    @pl.when(ki == 0)
    def _():
        m_ref[...] = jnp.full_like(m_ref, NEG_INF)
        acc_ref[...] = jnp.zeros_like(acc_ref)
        va_ref[:, hd:2 * hd] = jnp.ones((t, hd), BF16)

    va_ref[:, 0:hd] = v_ref[...].astype(BF16)

    def update(diagonal):
        tc = ts_diag if diagonal else ts
        for r in range(t // tc):
            rows = slice(r * tc, (r + 1) * tc)
            nk = (r + 1) * tc if diagonal else t
            s = _dot_nt(q_ref[rows, :], k_ref[0:nk, :])
            if diagonal:
                ri = lax.broadcasted_iota(jnp.int32, (tc, nk), 0) + r * tc
                ci = lax.broadcasted_iota(jnp.int32, (tc, nk), 1)
                s = jnp.where(ci <= ri, s, NEG_INF)
            m_old = m_ref[rows, :]
            m_new = jnp.maximum(m_old, jnp.max(s, axis=1, keepdims=True))
            m_ref[rows, :] = m_new
            alpha = jnp.exp2((m_old - m_new) * to_log2)
            p = jnp.exp2((s - _widen(m_new, nk)) * to_log2)
            acc_ref[rows, :] = _widen(alpha, 2 * hd) * acc_ref[rows, :] + _dot(p.astype(BF16), va_ref[0:nk, :])

    @pl.when(ki < qi)
    def _():
        update(False)

    @pl.when(ki == qi)
    def _():
        update(True)
        o_ref[...] = (acc_ref[:, 0:hd] / acc_ref[:, hd:2 * hd]).astype(BF16)


def _fox_prompt(qa, ka, proj, *, bsz, seq_len, t, ts, ts_diag):
    m = qa.shape[0]
    nq = seq_len // t
    assert seq_len % t == 0 and t % ts == 0 and t % ts_diag == 0
    pairs = [(qi, ki) for qi in range(nq) for ki in range(qi + 1)]
    qi_tab = jnp.asarray([p[0] for p in pairs], jnp.int32)
    ki_tab = jnp.asarray([p[1] for p in pairs], jnp.int32)
    grid_spec = pltpu.PrefetchScalarGridSpec(
        num_scalar_prefetch=2,
        grid=(bsz, FOX_HEADS, len(pairs)),
        in_specs=[pl.BlockSpec((t, FOX_AUG), lambda b, h, s, qi, ki: (b * nq + qi[s], h)),
                  pl.BlockSpec((t, FOX_AUG), lambda b, h, s, qi, ki: (b * nq + ki[s], h)),
                  pl.BlockSpec((t, FOX_HEAD_DIM), lambda b, h, s, qi, ki: (b * nq + ki[s], 2 * FOX_HEADS + h))],
        out_specs=pl.BlockSpec((t, FOX_HEAD_DIM), lambda b, h, s, qi, ki: (b * nq + qi[s], h)),
        scratch_shapes=[pltpu.VMEM((t, LANES), F32), pltpu.VMEM((t, 2 * FOX_HEAD_DIM), F32),
                        pltpu.VMEM((t, 2 * FOX_HEAD_DIM), BF16)],
    )
    return pl.pallas_call(
        functools.partial(_fox_kernel, t=t, ts=ts, ts_diag=ts_diag),
        grid_spec=grid_spec,
        out_shape=jax.ShapeDtypeStruct((m, FOX_WIDTH), BF16),
        compiler_params=_cparams(3),
        name="fox_attention",
    )(qi_tab, ki_tab, qa, ka, proj)


def _fox_decode_kernel(q_ref, kn_ref, vn_ref, kc_ref, vc_ref, cq_ref, ckc_ref, ckn_ref, o_ref,
                       m_ref, l_ref, acc_ref, *, lq, ncb):
    j = pl.program_id(1)
    nh, hd = FOX_HEADS, FOX_HEAD_DIM

    @pl.when(j == 0)
    def _():
        m_ref[...] = jnp.full_like(m_ref, NEG_INF)
        l_ref[...] = jnp.zeros_like(l_ref)
        acc_ref[...] = jnp.zeros_like(acc_ref)

    def attend(k_head, v_head, ck_head, causal):
        for h in range(nh):
            qh = q_ref[:, h * hd:(h + 1) * hd].astype(BF16)
            ck = ck_head(h)
            tk = ck.shape[1]
            s = _dot_nt(qh, k_head(h).astype(BF16)) * (hd ** -0.5)
            s = s + (_widen(cq_ref[h], tk) - ck)
            if causal:
                rows = lax.broadcasted_iota(jnp.int32, (lq, tk), 0)
                cols = lax.broadcasted_iota(jnp.int32, (lq, tk), 1)
                s = jnp.where(cols <= rows, s, NEG_INF)
            m_old = m_ref[h]
            m_new = jnp.maximum(m_old, jnp.max(s, axis=1, keepdims=True))
            alpha = jnp.exp(m_old - m_new)
            p = jnp.exp(s - _widen(m_new, tk))
            l_ref[h] = alpha * l_ref[h] + jnp.sum(p, axis=1, keepdims=True)
            acc_ref[h] = alpha * acc_ref[h] + _dot(p.astype(BF16), v_head(h).astype(BF16))
            m_ref[h] = m_new

    @pl.when(j < ncb)
    def _():
        k_hm = pltpu.einshape("mhd->hmd", kc_ref[0])
        v_hm = pltpu.einshape("mhd->hmd", vc_ref[0])
        attend(lambda h: k_hm[h], lambda h: v_hm[h], lambda h: ckc_ref[h], False)

    @pl.when(j == ncb)
    def _():
        attend(lambda h: kn_ref[:, h * hd:(h + 1) * hd], lambda h: vn_ref[:, h * hd:(h + 1) * hd],
               lambda h: ckn_ref[h][:, 0:lq], True)
        for h in range(nh):
            o_ref[:, h * hd:(h + 1) * hd] = (acc_ref[h] / l_ref[h]).astype(BF16)


def _fox_decode(proj, cache_k, cache_v, cq, ck_cache, ck_new, *, bsz, lq, tk):
    past = cache_k.shape[1]
    assert past % tk == 0 and tk % LANES == 0 and lq <= LANES
    ncb = past // tk
    nh, hd, w = FOX_HEADS, FOX_HEAD_DIM, FOX_WIDTH
    tile = lambda j: jnp.minimum(j, ncb - 1)
    cache_spec = pl.BlockSpec((1, tk, nh, hd), lambda b, j: (b, tile(j), 0, 0))
    return pl.pallas_call(
        functools.partial(_fox_decode_kernel, lq=lq, ncb=ncb),
        grid=(bsz, ncb + 1),
        in_specs=[pl.BlockSpec((lq, w), lambda b, j: (b, 0)),
                  pl.BlockSpec((lq, w), lambda b, j: (b, 1)),
                  pl.BlockSpec((lq, w), lambda b, j: (b, 2)),
                  cache_spec,
                  cache_spec,
                  pl.BlockSpec((nh, lq, LANES), lambda b, j: (b, 0, 0)),
                  pl.BlockSpec((nh, 1, tk), lambda b, j: (b, 0, tile(j))),
                  pl.BlockSpec((nh, 1, LANES), lambda b, j: (b, 0, 0))],
        out_specs=pl.BlockSpec((lq, w), lambda b, j: (b, 0)),
        out_shape=jax.ShapeDtypeStruct((bsz * lq, w), BF16),
        scratch_shapes=[pltpu.VMEM((nh, lq, LANES), F32), pltpu.VMEM((nh, lq, LANES), F32),
                        pltpu.VMEM((nh, lq, hd), F32)],
        compiler_params=_cparams(2),
        name="fox_decode",
    )(proj, proj, proj, cache_k, cache_v, cq, ck_cache, ck_new)


def _sconv_kernel(u_ref, bg_ref, cg_ref, cw_ref, hist_ref, y_ref, tail_ref, carry_ref, ext_ref,
                  *, tm, rows, spt, tpb):
    i = pl.program_id(0)
    w = cg_ref[...] * u_ref[...]
    first = (i % tpb) == 0
    for s in range(spt):
        w_s = w[s * rows:(s + 1) * rows]
        if tpb == 1:
            prev = hist_ref[s]
        else:
            prev = jnp.where(first, hist_ref[s], carry_ref[...])
        conv = _conv_rows(ext_ref, w_s, prev, cw_ref, SC_WIDTH, rows)
        y_ref[s * rows:(s + 1) * rows, :] = (bg_ref[s * rows:(s + 1) * rows, :] * conv).astype(BF16)
        tail_ref[s] = w_s[rows - SUBLANES:rows]
    if tpb > 1:
        carry_ref[...] = w[tm - SUBLANES:tm]


def _sconv(proj, conv_w, hist8, *, seq_len, tm):
    m = proj.shape[0]
    assert m % tm == 0
    rows, spt, tpb = _seq_tiling(seq_len, tm)
    nm = m // tm
    hist_map = (lambda i: (i // tpb, 0, 0)) if spt == 1 else (lambda i: (i, 0, 0))
    base = 3 * FOX_WIDTH // SC_DIM
    return pl.pallas_call(
        functools.partial(_sconv_kernel, tm=tm, rows=rows, spt=spt, tpb=tpb),
        grid=(nm,),
        in_specs=[pl.BlockSpec((tm, SC_DIM), lambda i: (i, base)),
                  pl.BlockSpec((tm, SC_DIM), lambda i: (i, base + 1)),
                  pl.BlockSpec((tm, SC_DIM), lambda i: (i, base + 2)),
                  pl.BlockSpec((SC_WIDTH, SC_DIM), lambda i: (0, 0)),
                  pl.BlockSpec((spt, SUBLANES, SC_DIM), hist_map)],
        out_specs=[pl.BlockSpec((tm, SC_DIM), lambda i: (i, 0)),
                   pl.BlockSpec((spt, SUBLANES, SC_DIM), lambda i: (i, 0, 0))],
        out_shape=[jax.ShapeDtypeStruct((m, SC_DIM), BF16),
                   jax.ShapeDtypeStruct((nm * spt, SUBLANES, SC_DIM), F32)],
        scratch_shapes=[pltpu.VMEM((SUBLANES, SC_DIM), F32),
                        pltpu.VMEM((rows + SUBLANES, SC_DIM), F32)],
        compiler_params=_cparams(1),
        name="gated_short_conv",
    )(proj, proj, proj, conv_w, hist8)


def _hist8(state):
    n, w1, c = state.shape
    return jnp.concatenate([jnp.zeros((n, SUBLANES - w1, c), F32), state.astype(F32)], axis=1)


def _tails(tails, n_seq, seq_len, tile_rows, keep):
    per_seq = max(1, seq_len // tile_rows)
    pieces = tails.reshape(n_seq, per_seq, SUBLANES, tails.shape[-1])
    return pieces[:, per_seq - 1, SUBLANES - keep:, :]


def _rope_tables(pos0, length):
    half = RET_DK // 2
    inv = ROPE_BASE ** (-np.arange(half, dtype=np.float64) / half)
    ang = (pos0 + np.arange(length, dtype=np.float64))[:, None] * inv[None, :]
    cos, sin = np.cos(ang), np.sin(ang)
    return (jnp.asarray(np.concatenate([cos, cos], axis=1), F32),
            jnp.asarray(np.concatenate([-sin, sin], axis=1), F32))


def _pad_cols_bf16(w):
    return jnp.pad(w, ((0, 0), (0, LANES - w.shape[1]))).astype(BF16)


def _cd_weights_xla(cd_in):
    f0 = 3 * FOX_WIDTH
    main = jnp.concatenate([cd_in[:, :f0], cd_in[:, f0 + FOX_HEADS:]], axis=1).astype(BF16)
    return main, _pad_cols_bf16(cd_in[:, f0:f0 + FOX_HEADS])


def _prep_weights(p):
    ab_in = p['ab_w_in'][0]
    return dict(
        ab_in=ab_in.astype(BF16),
        ab_small=_pad_cols_bf16(ab_in[:, AB_MAIN:]),
        cd_in=None,
        cd_small=None,
        ab_out=p['ab_w_out'][0].astype(BF16),
        cd_out=p['cd_w_out'][0].astype(BF16),
        ffn0=None,
        ffn1=None,
    )


def _trunk(x, pos0, st_ret, st_ssd, st_ssd_conv, c_k, c_v, c_logf, st_sconv, st_ffn, p, wb, t):
    bsz, length, d = x.shape
    m = bsz * length
    xf = x.reshape(m, d)
    zeros = lambda *shape: jnp.zeros(shape, F32)

    proj, dt_proj = _norm_matmul(xf, p['ab_norm_w'][0], wb['ab_in'], wb['ab_small'],
                                 n=AB_MAIN, tm=t['tm_ab'], tn=t['tn_ab'])
    cosf, sinf = _rope_tables(pos0, length)
    ret_state = zeros(bsz, RET_HEADS, RET_DK, RET_DV) if st_ret is None else st_ret
    y_ret, ret_new, cd_bf16 = _retention(proj, cosf, sinf, ret_state, p['ret_norm_w'][0],
                                         bsz=bsz, seq_len=length, c=t['c_ret'],
                                         cd_w_in=p['cd_w_in'][0] if wb['cd_in'] is None else None)
    if wb['cd_in'] is None:
        wb['cd_in'], wb['cd_small'] = cd_bf16 if cd_bf16 is not None else _cd_weights_xla(p['cd_w_in'][0])
    ssd_state = zeros(bsz, SSD_HEADS, SSD_HEADDIM, SSD_DSTATE) if st_ssd is None else st_ssd
    ssd_hist = zeros(bsz, SSD_CONV - 1, SSD_CONV_DIM) if st_ssd_conv is None else st_ssd_conv
    ffn_f32 = [p['ffn_w_gate'], p['ffn_w_up'], p['ffn_w_down']]
    y_ssd, ssd_new, ffn0_bf16 = _ssd(proj, dt_proj, _hist8(ssd_hist), ssd_state, p['ssd_conv_w'][0],
                                     p['ssd_conv_b'][0], p['ssd_dt_bias'][0], p['ssd_A_log'][0], p['ssd_D'][0],
                                     p['ssd_norm_w'][0], bsz=bsz, seq_len=length, c=t['c_ssd'],
                                     cast=(0, ffn_f32) if wb['ffn0'] is None else None)
    if wb['ffn0'] is None:
        wb['ffn0'] = (tuple(w[None] for w in ffn0_bf16) if ffn0_bf16 is not None
                      else tuple(w[0:1].astype(BF16) for w in ffn_f32))
    xbc_lo = AB_MAIN - SSD_CONV_DIM
    ssd_conv_new = proj.reshape(bsz, length, -1)[:, length - (SSD_CONV - 1):, xbc_lo:AB_MAIN]
    xf = _proj_residual(xf, y_ret, y_ssd, wb['ab_out'], tm=t['tm_out'], tn=t['tn_out'])

    ffn_new = []
    ffn_hist0 = zeros(bsz, FFN_CONV - 1, D_FF) if st_ffn is None else st_ffn[0]
    next_f32 = (1, p['ffn_w_gate'], p['ffn_w_up'], p['ffn_w_down'])
    n_row_tiles = m // t['tm_ffn']
    if wb['ffn1'] is None and not (d % (n_row_tiles * LANES) == 0 and D_FF % t['tf_ffn'] == 0):
        wb['ffn1'] = tuple(w[1:2].astype(BF16) for w in next_f32[1:])
    xf, tails, next_bf16 = _conv_ffn(
        xf, p['ffn_norm_w'][0], *wb['ffn0'], p['ffn_conv_w'][0], p['ffn_conv_b'][0], _hist8(ffn_hist0),
        p['final_norm_w'], layer=0, seq_len=length, tm=t['tm_ffn'], tf=t['tf_ffn'], ts=t['ts_ffn'], final=False,
        cast_next=next_f32 if wb['ffn1'] is None else None)
    if wb['ffn1'] is None:
        wb['ffn1'] = tuple(w[None] for w in next_bf16)
    ffn_new.append(_tails(tails, bsz, length, t['tm_ffn'], FFN_CONV - 1))

    proj, fl_proj = _norm_matmul(xf, p['cd_norm_w'][0], wb['cd_in'], wb['cd_small'],
                                 n=CD_MAIN, tm=t['tm_cd'], tn=t['tn_cd'])
    f_bias = jnp.pad(p['fox_f_bias'][0].astype(F32), (0, LANES - FOX_HEADS)).reshape(1, LANES)
    head_shape = (bsz, length, FOX_HEADS, FOX_HEAD_DIM)
    if c_k is None:
        qa, ka, k32, v32, logf = _fox_prep(proj, fl_proj, f_bias, bsz=bsz, seq_len=length, tp=t['t_prep'])
        y_fox = _fox_prompt(qa, ka, proj, bsz=bsz, seq_len=length, t=t['t_fox'], ts=t['ts_fox'],
                            ts_diag=t['ts_fox_diag'])
        logf_new = logf.reshape(bsz, length, LANES)[:, :, :FOX_HEADS]
        k_new, v_new = k32.reshape(head_shape), v32.reshape(head_shape)
    else:
        proj3 = proj.reshape(bsz, length, -1)
        k_new = proj3[:, :, FOX_WIDTH:2 * FOX_WIDTH].reshape(head_shape)
        v_new = proj3[:, :, 2 * FOX_WIDTH:3 * FOX_WIDTH].reshape(head_shape)
        past = c_k.shape[1]
        pairs = bsz * FOX_HEADS
        assert pairs <= LANES and length <= LANES
        to_lanes = lambda a, rows: jnp.pad(jnp.swapaxes(a, 0, 1).reshape(a.shape[1], pairs),
                                           ((0, rows - a.shape[1]), (0, LANES - pairs)))
        from_lanes = lambda a: jnp.swapaxes(a[:length, :pairs].reshape(length, bsz, FOX_HEADS), 0, 1)
        cache_lf = to_lanes(c_logf.astype(F32), past)
        fl_rows = to_lanes(fl_proj.reshape(bsz, length, LANES)[:, :, :FOX_HEADS], LANES)
        bias_lanes = jnp.pad(jnp.tile(p['fox_f_bias'][0].astype(F32), bsz), (0, LANES - pairs)).reshape(1, LANES)
        cum_t_cache, lf_rows, cum_rows, cum_t_new = _decode_cum(cache_lf, fl_rows, bias_lanes, c=t['c_cum'])
        logf_new = from_lanes(lf_rows)
        cq = jnp.broadcast_to(cum_rows[:length, :pairs].T[:, :, None], (pairs, length, LANES))
        y_fox = _fox_decode(proj, c_k, c_v, cq, cum_t_cache[:pairs, None, :], cum_t_new[:pairs, None, :],
                            bsz=bsz, lq=length, tk=t['tk_dec'])
    sc_hist = zeros(bsz, SC_WIDTH - 1, SC_DIM) if st_sconv is None else st_sconv
    y_sc, sc_tails = _sconv(proj, p['sconv_w'][0], _hist8(sc_hist), seq_len=length, tm=t['tm_sc'])
    sconv_new = _tails(sc_tails, bsz, length, t['tm_sc'], SC_WIDTH - 1)
    xf = _proj_residual(xf, y_fox, y_sc, wb['cd_out'], tm=t['tm_out'], tn=t['tn_out'])

    ffn_hist1 = zeros(bsz, FFN_CONV - 1, D_FF) if st_ffn is None else st_ffn[1]
    xf, tails, _ = _conv_ffn(xf, p['ffn_norm_w'][1], *wb['ffn1'], p['ffn_conv_w'][1], p['ffn_conv_b'][1],
                             _hist8(ffn_hist1), p['final_norm_w'], layer=0, seq_len=length, tm=t['tm_ffn'],
                             tf=t['tf_ffn'], ts=t['ts_ffn'], final=True)
    ffn_new.append(_tails(tails, bsz, length, t['tm_ffn'], FFN_CONV - 1))

    return (xf.reshape(bsz, length, d), ret_new[None], ssd_new[None], ssd_conv_new[None], k_new[None],
            v_new[None], logf_new[None], sconv_new[None], jnp.stack(ffn_new))


def _largest_divisor(n, cap, multiple=1):
    best = None
    for cand in range(multiple, min(n, cap) + 1, multiple):
        if n % cand == 0:
            best = cand
    assert best is not None, (n, cap, multiple)
    return best


def _tiles(bsz, length, past=None):
    m = bsz * length
    seq_tile = lambda cap: _largest_divisor(length, cap, SUBLANES)
    row_tile = lambda cap: (_largest_divisor(length, cap, SUBLANES) if length >= cap
                            else _largest_divisor(m, cap, length))
    t = dict(
        tm_ab=row_tile(1024), tn_ab=2048, tm_cd=row_tile(1024), tn_cd=2048,
        tm_out=row_tile(1024), tn_out=2048,
        tm_ffn=row_tile(1024), tf_ffn=512, ts_ffn=512,
        tm_sc=row_tile(1024),
        c_ret=seq_tile(256), c_ssd=seq_tile(256),
    )
    if past is None:
        t['t_fox'] = seq_tile(2048)
        t['ts_fox'] = _largest_divisor(t['t_fox'], 256, LANES)
        t['ts_fox_diag'] = _largest_divisor(t['t_fox'], 512, LANES)
        t['t_prep'] = seq_tile(512)
    else:
        t['tk_dec'] = _largest_divisor(past, 2048, LANES)
        t['c_cum'] = _largest_divisor(past, 256, LANES)
    return t


def kernel(x_prompt, x_sample, state_ret, state_ssd, state_ssd_conv, cache_fox_k, cache_fox_v, cache_fox_logf, state_sconv, state_ffn_conv, ab_norm_w, ab_w_in, ret_norm_w, ssd_conv_w, ssd_conv_b, ssd_dt_bias, ssd_A_log, ssd_D, ssd_norm_w, ab_w_out, cd_norm_w, cd_w_in, fox_f_bias, sconv_w, cd_w_out, ffn_norm_w, ffn_w_gate, ffn_w_up, ffn_conv_w, ffn_conv_b, ffn_w_down, final_norm_w):
    p = dict(ab_norm_w=ab_norm_w, ab_w_in=ab_w_in, ret_norm_w=ret_norm_w, ssd_conv_w=ssd_conv_w,
             ssd_conv_b=ssd_conv_b, ssd_dt_bias=ssd_dt_bias, ssd_A_log=ssd_A_log, ssd_D=ssd_D,
             ssd_norm_w=ssd_norm_w, ab_w_out=ab_w_out, cd_norm_w=cd_norm_w, cd_w_in=cd_w_in,
             fox_f_bias=fox_f_bias, sconv_w=sconv_w, cd_w_out=cd_w_out, ffn_norm_w=ffn_norm_w,
             ffn_w_gate=ffn_w_gate, ffn_w_up=ffn_w_up, ffn_conv_w=ffn_conv_w, ffn_conv_b=ffn_conv_b,
             ffn_w_down=ffn_w_down, final_norm_w=final_norm_w)
    assert x_prompt.shape[-1] == D_MODEL and ab_w_in.shape == (1, D_MODEL, AB_MAIN + SSD_HEADS)
    assert cd_w_in.shape == (1, D_MODEL, CD_MAIN + FOX_HEADS) and ffn_w_gate.shape == (2, D_MODEL, D_FF)
    wb = _prep_weights(p)
    bp, lp_, _ = x_prompt.shape
    bs, ls, _ = x_sample.shape
    past = cache_fox_k.shape[2]
    (y_prompt, p_ret, p_ssd, p_ssd_conv, p_fox_k, p_fox_v, p_fox_logf, p_sconv, p_ffn_conv) = _trunk(
        x_prompt, 0, None, None, None, None, None, None, None, None, p, wb, _tiles(bp, lp_))
    (y_sample, s_ret, s_ssd, s_ssd_conv, s_fox_k, s_fox_v, s_fox_logf, s_sconv, s_ffn_conv) = _trunk(
        x_sample, past, state_ret[0], state_ssd[0], state_ssd_conv[0], cache_fox_k[0], cache_fox_v[0],
        cache_fox_logf[0], state_sconv[0], state_ffn_conv, p, wb, _tiles(bs, ls, past))
    return (y_prompt, y_sample, p_ret, s_ret, p_ssd, s_ssd, p_ssd_conv, s_ssd_conv, p_fox_k, s_fox_k,
            p_fox_v, s_fox_v, p_fox_logf, s_fox_logf, p_sconv, s_sconv, p_ffn_conv, s_ffn_conv)
```
